```python
import math
import jax, jax.numpy as jnp
from jax import lax
import numpy as np

D_MODEL = 1024
BATCH = 32
SEQ = 2048
DEPTH = 1

CHUNK = 64
Q_BLOCK = 128

D_MIX = D_MODEL
SB_HEAD_DIM = 64
SB_HEADS = (D_MIX // 2) // SB_HEAD_DIM
SB_W = SB_HEADS * SB_HEAD_DIM
MLA_V_DIM = 64
MLA_HEADS = (D_MIX - SB_W) // MLA_V_DIM
MLA_W = MLA_HEADS * MLA_V_DIM
MLA_NOPE_DIM = 64
MLA_ROPE_DIM = 32
MLA_QK_DIM = MLA_NOPE_DIM + MLA_ROPE_DIM
MLA_Q_RANK = (3 * D_MODEL) // 8
MLA_KV_RANK = D_MODEL // 4
ROPE_BASE = 10000.0

IN_SPLITS = (SB_W, SB_W, SB_W, MLA_Q_RANK, MLA_KV_RANK, MLA_ROPE_DIM)
D_IN = sum(IN_SPLITS)

D_FF = 4 * D_MODEL

LN_EPS = 1e-5
RMS_EPS = 1e-6
DEEPNORM_ALPHA = (2.0 * DEPTH) ** 0.25
DEEPNORM_BETA = (8.0 * DEPTH) ** -0.25
N_MOD = 6

kernel_name = "hybrid_sb_mla_deepnorm_adaln_block"


def layer_norm(x, g, b):
    xf = x.astype(jnp.float32)
    mu = jnp.mean(xf, axis=-1, keepdims=True)
    var = jnp.mean(jnp.square(xf - mu), axis=-1, keepdims=True)
    return ((xf - mu) * lax.rsqrt(var + LN_EPS) * g.astype(jnp.float32) + b.astype(jnp.float32)).astype(x.dtype)


def rms_norm(x, g):
    xf = x.astype(jnp.float32)
    ms = jnp.mean(jnp.square(xf), axis=-1, keepdims=True)
    return (xf * lax.rsqrt(ms + RMS_EPS) * g.astype(jnp.float32)).astype(x.dtype)


def rope_tables(seq):
    inv_freq = 1.0 / (ROPE_BASE ** (jnp.arange(0, MLA_ROPE_DIM, 2, dtype=jnp.float32) / MLA_ROPE_DIM))
    ang = jnp.arange(seq, dtype=jnp.float32)[:, None] * inv_freq[None, :]
    return jnp.cos(ang), jnp.sin(ang)


def apply_rope(x, cos, sin):
    xf = x.astype(jnp.float32)
    x1, x2 = jnp.split(xf, 2, axis=-1)
    return jnp.concatenate([x1 * cos - x2 * sin, x2 * cos + x1 * sin], axis=-1).astype(x.dtype)


def stick_breaking_block(q_blk, k_pre, v_pre, q_start):
    qb = q_blk.shape[1]
    kl = k_pre.shape[1]
    z = jnp.einsum('bqhd,bkhd->bhqk', q_blk, k_pre).astype(jnp.float32) * (SB_HEAD_DIM ** -0.5)
    t_idx = q_start + jnp.arange(qb)
    s_idx = jnp.arange(kl)
    strict = s_idx[None, :] < t_idx[:, None]
    log_keep = jnp.where(strict, jax.nn.log_sigmoid(-z), 0.0)
    log_later = lax.cumsum(log_keep, axis=3, reverse=True) - log_keep
    w = jnp.where(strict, jnp.exp(jax.nn.log_sigmoid(z) + log_later), 0.0)
    return jnp.einsum('bhqk,bkhd->bqhd', w.astype(v_pre.dtype), v_pre)


def mla_block(qn_blk, qr_blk, kn_pre, kr_pre, v_pre, q_start):
    qb = qn_blk.shape[1]
    kl = kn_pre.shape[1]
    scores = (jnp.einsum('bqhd,bkhd->bhqk', qn_blk, kn_pre)
              + jnp.einsum('bqhr,bkr->bhqk', qr_blk, kr_pre)).astype(jnp.float32) * (MLA_QK_DIM ** -0.5)
    q_chunk = (q_start + jnp.arange(qb)) // CHUNK
    k_chunk = jnp.arange(kl) // CHUNK
    allowed = k_chunk[None, :] <= q_chunk[:, None]
    scores = jnp.where(allowed, scores, jnp.finfo(jnp.float32).min)
    p = jax.nn.softmax(scores, axis=-1)
    return jnp.einsum('bhqk,bkhd->bqhd', p.astype(v_pre.dtype), v_pre)


def _fwd_setup_inputs(seed: int = 0) -> dict:
    key = jax.random.key(seed)
    ks = jax.random.split(key, 24)
    f32 = jnp.float32
    nrm = lambda k, shape, s: jax.random.normal(k, shape, f32) * s
    L, D = DEPTH, D_MODEL
    x = jax.random.normal(ks[0], (BATCH, SEQ, D), f32)
    c = jax.random.normal(ks[1], (BATCH, D), f32)
    ln_in_g = 1.0 + nrm(ks[2], (D,), 0.02)
    ln_in_b = nrm(ks[3], (D,), 0.02)
    w_ada = nrm(ks[4], (L, D, N_MOD * D), 0.1 * D ** -0.5)
    b_ada = nrm(ks[5], (L, N_MOD * D), 0.02)
    w_in_qk = nrm(ks[6], (L, D, 2 * SB_W), D ** -0.5)
    w_in_v = nrm(ks[7], (L, D, SB_W), D ** -0.5) * DEEPNORM_BETA
    w_in_lat = nrm(ks[8], (L, D, MLA_Q_RANK + MLA_KV_RANK + MLA_ROPE_DIM), D ** -0.5)
    w_in = jnp.concatenate([w_in_qk, w_in_v, w_in_lat], axis=-1)
    q_norm_g = 1.0 + nrm(ks[9], (L, MLA_Q_RANK), 0.02)
    kv_norm_g = 1.0 + nrm(ks[10], (L, MLA_KV_RANK), 0.02)
    w_uq = nrm(ks[11], (L, MLA_Q_RANK, MLA_HEADS * MLA_QK_DIM), MLA_Q_RANK ** -0.5)
    w_uk = nrm(ks[12], (L, MLA_KV_RANK, MLA_HEADS * MLA_NOPE_DIM), MLA_KV_RANK ** -0.5)
    w_uv = nrm(ks[13], (L, MLA_KV_RANK, MLA_HEADS * MLA_V_DIM), MLA_KV_RANK ** -0.5) * DEEPNORM_BETA
    w_ukv = jnp.concatenate([w_uk, w_uv], axis=-1)
    w_o = nrm(ks[14], (L, D_MIX, D), D_MIX ** -0.5) * DEEPNORM_BETA
    ln1_g = 1.0 + nrm(ks[15], (L, D), 0.02)
    ln1_b = nrm(ks[16], (L, D), 0.02)
    w_up = nrm(ks[17], (L, D, D_FF), D ** -0.5)
    w_down = nrm(ks[18], (L, D_FF, D), D_FF ** -0.5) * DEEPNORM_BETA
    ln2_g = 1.0 + nrm(ks[19], (L, D), 0.02)
    ln2_b = nrm(ks[20], (L, D), 0.02)
    return {"x": x, "c": c, "ln_in_g": ln_in_g, "ln_in_b": ln_in_b,
            "w_ada": w_ada, "b_ada": b_ada, "w_in": w_in,
            "q_norm_g": q_norm_g, "kv_norm_g": kv_norm_g, "w_uq": w_uq, "w_ukv": w_ukv,
            "w_o": w_o, "ln1_g": ln1_g, "ln1_b": ln1_b,
            "w_up": w_up, "w_down": w_down, "ln2_g": ln2_g, "ln2_b": ln2_b}


def _fwd_reference(x, c, ln_in_g, ln_in_b, w_ada, b_ada, w_in, q_norm_g, kv_norm_g, w_uq, w_ukv,
              w_o, ln1_g, ln1_b, w_up, w_down, ln2_g, ln2_b):
    bsz, seq, _ = x.shape
    n_blocks = seq // Q_BLOCK
    cos, sin = rope_tables(seq)
    split_at = list(np.cumsum(IN_SPLITS)[:-1])
    c_act = jax.nn.silu(c)

    x = layer_norm(x, ln_in_g, ln_in_b)

    for l in range(DEPTH):
        mod = c_act @ w_ada[l] + b_ada[l]
        shift1, scale1, gate1, shift2, scale2, gate2 = [m[:, None, :] for m in jnp.split(mod, N_MOD, axis=-1)]

        h = x * (1.0 + scale1) + shift1
        proj = h @ w_in[l]
        sb_q, sb_k, sb_v, c_q, c_kv, k_rope = jnp.split(proj, split_at, axis=-1)
        sb_q = sb_q.reshape(bsz, seq, SB_HEADS, SB_HEAD_DIM)
        sb_k = sb_k.reshape(bsz, seq, SB_HEADS, SB_HEAD_DIM)
        sb_v = sb_v.reshape(bsz, seq, SB_HEADS, SB_HEAD_DIM)

        q_all = (rms_norm(c_q, q_norm_g[l]) @ w_uq[l]).reshape(bsz, seq, MLA_HEADS, MLA_QK_DIM)
        q_nope, q_rot = q_all[..., :MLA_NOPE_DIM], q_all[..., MLA_NOPE_DIM:]
        q_rot = apply_rope(q_rot, cos[None, :, None, :], sin[None, :, None, :])
        kv = rms_norm(c_kv, kv_norm_g[l]) @ w_ukv[l]
        k_nope = kv[..., :MLA_HEADS * MLA_NOPE_DIM].reshape(bsz, seq, MLA_HEADS, MLA_NOPE_DIM)
        mla_v = kv[..., MLA_HEADS * MLA_NOPE_DIM:].reshape(bsz, seq, MLA_HEADS, MLA_V_DIM)
        k_rot = apply_rope(k_rope, cos[None], sin[None])

        sb_out, mla_out = [], []
        for i in range(n_blocks):
            q0 = i * Q_BLOCK
            q1 = q0 + Q_BLOCK
            sb_out.append(stick_breaking_block(sb_q[:, q0:q1], sb_k[:, :q1], sb_v[:, :q1], q0))
            mla_out.append(mla_block(q_nope[:, q0:q1], q_rot[:, q0:q1], k_nope[:, :q1],
                                     k_rot[:, :q1], mla_v[:, :q1], q0))
        sb_y = jnp.concatenate(sb_out, axis=1).reshape(bsz, seq, SB_W)
        mla_y = jnp.concatenate(mla_out, axis=1).reshape(bsz, seq, MLA_W)
        mix = jnp.concatenate([sb_y, mla_y], axis=-1) @ w_o[l]
        x = layer_norm(DEEPNORM_ALPHA * x + (1.0 + gate1) * mix, ln1_g[l], ln1_b[l])

        h2 = x * (1.0 + scale2) + shift2
        ff = jnp.square(jax.nn.relu(h2 @ w_up[l])) @ w_down[l]
        x = layer_norm(DEEPNORM_ALPHA * x + (1.0 + gate2) * ff, ln2_g[l], ln2_b[l])

    return x


import jax as _jax
import jax.numpy as _jnp

TWIN_FORMAT = 'train_step'
FWD_PARAMS = ['x', 'c', 'ln_in_g', 'ln_in_b', 'w_ada', 'b_ada', 'w_in', 'q_norm_g', 'kv_norm_g', 'w_uq', 'w_ukv', 'w_o', 'ln1_g', 'ln1_b', 'w_up', 'w_down', 'ln2_g', 'ln2_b']
TWIN_WEIGHTS = ['ln_in_g', 'ln_in_b', 'w_ada', 'b_ada', 'w_in', 'q_norm_g', 'kv_norm_g', 'w_uq', 'w_ukv', 'w_o', 'ln1_g', 'ln1_b', 'w_up', 'w_down', 'ln2_g', 'ln2_b']
TWIN_DIFF_INPUT = 'x'
TWIN_INPUTS = ['x', 'c', 'ln_in_g', 'ln_in_b', 'w_ada', 'b_ada', 'w_in', 'q_norm_g', 'kv_norm_g', 'w_uq', 'w_ukv', 'w_o', 'ln1_g', 'ln1_b', 'w_up', 'w_down', 'ln2_g', 'ln2_b', 'loss_target', 'm_ln_in_g', 'm_ln_in_b', 'm_w_ada', 'm_b_ada', 'm_w_in', 'm_q_norm_g', 'm_kv_norm_g', 'm_w_uq', 'm_w_ukv', 'm_w_o', 'm_ln1_g', 'm_ln1_b', 'm_w_up', 'm_w_down', 'm_ln2_g', 'm_ln2_b', 'v_ln_in_g', 'v_ln_in_b', 'v_w_ada', 'v_b_ada', 'v_w_in', 'v_q_norm_g', 'v_kv_norm_g', 'v_w_uq', 'v_w_ukv', 'v_w_o', 'v_ln1_g', 'v_ln1_b', 'v_w_up', 'v_w_down', 'v_ln2_g', 'v_ln2_b']
TWIN_OUTPUTS = ['loss', 'grad_x', 'grad_ln_in_g', 'grad_ln_in_b', 'grad_w_ada', 'grad_b_ada', 'grad_w_in', 'grad_q_norm_g', 'grad_kv_norm_g', 'grad_w_uq', 'grad_w_ukv', 'grad_w_o', 'grad_ln1_g', 'grad_ln1_b', 'grad_w_up', 'grad_w_down', 'grad_ln2_g', 'grad_ln2_b', 'delta_ln_in_g', 'delta_ln_in_b', 'delta_w_ada', 'delta_b_ada', 'delta_w_in', 'delta_q_norm_g', 'delta_kv_norm_g', 'delta_w_uq', 'delta_w_ukv', 'delta_w_o', 'delta_ln1_g', 'delta_ln1_b', 'delta_w_up', 'delta_w_down', 'delta_ln2_g', 'delta_ln2_b', 'new_m_ln_in_g', 'new_m_ln_in_b', 'new_m_w_ada', 'new_m_b_ada', 'new_m_w_in', 'new_m_q_norm_g', 'new_m_kv_norm_g', 'new_m_w_uq', 'new_m_w_ukv', 'new_m_w_o', 'new_m_ln1_g', 'new_m_ln1_b', 'new_m_w_up', 'new_m_w_down', 'new_m_ln2_g', 'new_m_ln2_b', 'new_v_ln_in_g', 'new_v_ln_in_b', 'new_v_w_ada', 'new_v_b_ada', 'new_v_w_in', 'new_v_q_norm_g', 'new_v_kv_norm_g', 'new_v_w_uq', 'new_v_w_ukv', 'new_v_w_o', 'new_v_ln1_g', 'new_v_ln1_b', 'new_v_w_up', 'new_v_w_down', 'new_v_ln2_g', 'new_v_ln2_b']
TWIN_LEAF_KINDS = {'loss': 'loss', 'grad_x': 'grad_x', 'grad_ln_in_g': 'grad_w', 'grad_ln_in_b': 'grad_w', 'grad_w_ada': 'grad_w', 'grad_b_ada': 'grad_w', 'grad_w_in': 'grad_w', 'grad_q_norm_g': 'grad_w', 'grad_kv_norm_g': 'grad_w', 'grad_w_uq': 'grad_w', 'grad_w_ukv': 'grad_w', 'grad_w_o': 'grad_w', 'grad_ln1_g': 'grad_w', 'grad_ln1_b': 'grad_w', 'grad_w_up': 'grad_w', 'grad_w_down': 'grad_w', 'grad_ln2_g': 'grad_w', 'grad_ln2_b': 'grad_w', 'delta_ln_in_g': 'delta_w', 'delta_ln_in_b': 'delta_w', 'delta_w_ada': 'delta_w', 'delta_b_ada': 'delta_w', 'delta_w_in': 'delta_w', 'delta_q_norm_g': 'delta_w', 'delta_kv_norm_g': 'delta_w', 'delta_w_uq': 'delta_w', 'delta_w_ukv': 'delta_w', 'delta_w_o': 'delta_w', 'delta_ln1_g': 'delta_w', 'delta_ln1_b': 'delta_w', 'delta_w_up': 'delta_w', 'delta_w_down': 'delta_w', 'delta_ln2_g': 'delta_w', 'delta_ln2_b': 'delta_w', 'new_m_ln_in_g': 'new_m', 'new_m_ln_in_b': 'new_m', 'new_m_w_ada': 'new_m', 'new_m_b_ada': 'new_m', 'new_m_w_in': 'new_m', 'new_m_q_norm_g': 'new_m', 'new_m_kv_norm_g': 'new_m', 'new_m_w_uq': 'new_m', 'new_m_w_ukv': 'new_m', 'new_m_w_o': 'new_m', 'new_m_ln1_g': 'new_m', 'new_m_ln1_b': 'new_m', 'new_m_w_up': 'new_m', 'new_m_w_down': 'new_m', 'new_m_ln2_g': 'new_m', 'new_m_ln2_b': 'new_m', 'new_v_ln_in_g': 'new_v', 'new_v_ln_in_b': 'new_v', 'new_v_w_ada': 'new_v', 'new_v_b_ada': 'new_v', 'new_v_w_in': 'new_v', 'new_v_q_norm_g': 'new_v', 'new_v_kv_norm_g': 'new_v', 'new_v_w_uq': 'new_v', 'new_v_w_ukv': 'new_v', 'new_v_w_o': 'new_v', 'new_v_ln1_g': 'new_v', 'new_v_ln1_b': 'new_v', 'new_v_w_up': 'new_v', 'new_v_w_down': 'new_v', 'new_v_ln2_g': 'new_v', 'new_v_ln2_b': 'new_v'}


def _forward(args):
    return _fwd_reference(*[args[k] for k in FWD_PARAMS])


def _output_shape():
    out = _jax.eval_shape(lambda: _forward(_fwd_setup_inputs(0)))
    return out.shape, out.dtype

N_MICROBATCH = 1
ADAM_LR = 0.001
ADAM_B1 = 0.9
ADAM_B2 = 0.999
ADAM_EPS = 1e-08
ADAM_WD = 0.01
ADAM_STEP = 10
PER_EXAMPLE_BATCH_AXIS = {'x': 0, 'c': 0, 'loss_target': 0}
SHARED_INPUTS = []
_WEIGHT_DTYPES = {'ln_in_g': _jnp.float32, 'ln_in_b': _jnp.float32, 'w_ada': _jnp.float32, 'b_ada': _jnp.float32, 'w_in': _jnp.float32, 'q_norm_g': _jnp.float32, 'kv_norm_g': _jnp.float32, 'w_uq': _jnp.float32, 'w_ukv': _jnp.float32, 'w_o': _jnp.float32, 'ln1_g': _jnp.float32, 'ln1_b': _jnp.float32, 'w_up': _jnp.float32, 'w_down': _jnp.float32, 'ln2_g': _jnp.float32, 'ln2_b': _jnp.float32}
MOMENT_SCALE = {'ln_in_g': 1.154209e+00, 'ln_in_b': 8.048803e-01, 'w_ada': 9.007402e-02, 'b_ada': 2.206163e-01, 'w_in': 4.450041e-02, 'q_norm_g': 1.254431e-02, 'kv_norm_g': 2.418707e-02, 'w_uq': 9.371053e-03, 'w_ukv': 1.725076e-02, 'w_o': 6.161770e-02, 'ln1_g': 1.198897e+00, 'ln1_b': 7.589324e-01, 'w_up': 7.650014e-02, 'w_down': 3.080687e-01, 'ln2_g': 6.406267e+01, 'ln2_b': 1.366526e+01}


def _to_microbatches(a, axis):
    t = _jnp.moveaxis(a, axis, 0)
    t = t.reshape((N_MICROBATCH, t.shape[0] // N_MICROBATCH) + t.shape[1:])
    return _jnp.moveaxis(t, 1, axis + 1)


def setup_inputs(seed: int = 0) -> dict:
    inp = _fwd_setup_inputs(seed)
    key = _jax.random.fold_in(_jax.random.key(seed), 7919)
    shape, _ = _output_shape()
    out = dict(inp)
    out["loss_target"] = _jax.random.normal(_jax.random.fold_in(key, 0), shape, _jnp.float32)
    for i, name in enumerate(TWIN_WEIGHTS):
        w = inp[name].astype(_jnp.float32)
        if MOMENT_SCALE is None:
            s = _jnp.sqrt(_jnp.mean(_jnp.square(w)) + 1e-30)
        else:
            s = MOMENT_SCALE[name]
        km, kv = _jax.random.split(_jax.random.fold_in(key, i + 1))
        out[name] = w
        out["m_" + name] = s * _jax.random.normal(km, w.shape, _jnp.float32)
        out["v_" + name] = (s * s) * _jax.random.uniform(kv, w.shape, _jnp.float32, 0.5, 1.5)
    if N_MICROBATCH > 1:
        for name, axis in PER_EXAMPLE_BATCH_AXIS.items():
            out[name] = _to_microbatches(out[name], axis)
    return {'x': out['x'], 'c': out['c'], 'ln_in_g': out['ln_in_g'], 'ln_in_b': out['ln_in_b'], 'w_ada': out['w_ada'], 'b_ada': out['b_ada'], 'w_in': out['w_in'], 'q_norm_g': out['q_norm_g'], 'kv_norm_g': out['kv_norm_g'], 'w_uq': out['w_uq'], 'w_ukv': out['w_ukv'], 'w_o': out['w_o'], 'ln1_g': out['ln1_g'], 'ln1_b': out['ln1_b'], 'w_up': out['w_up'], 'w_down': out['w_down'], 'ln2_g': out['ln2_g'], 'ln2_b': out['ln2_b'], 'loss_target': out['loss_target'], 'm_ln_in_g': out['m_ln_in_g'], 'm_ln_in_b': out['m_ln_in_b'], 'm_w_ada': out['m_w_ada'], 'm_b_ada': out['m_b_ada'], 'm_w_in': out['m_w_in'], 'm_q_norm_g': out['m_q_norm_g'], 'm_kv_norm_g': out['m_kv_norm_g'], 'm_w_uq': out['m_w_uq'], 'm_w_ukv': out['m_w_ukv'], 'm_w_o': out['m_w_o'], 'm_ln1_g': out['m_ln1_g'], 'm_ln1_b': out['m_ln1_b'], 'm_w_up': out['m_w_up'], 'm_w_down': out['m_w_down'], 'm_ln2_g': out['m_ln2_g'], 'm_ln2_b': out['m_ln2_b'], 'v_ln_in_g': out['v_ln_in_g'], 'v_ln_in_b': out['v_ln_in_b'], 'v_w_ada': out['v_w_ada'], 'v_b_ada': out['v_b_ada'], 'v_w_in': out['v_w_in'], 'v_q_norm_g': out['v_q_norm_g'], 'v_kv_norm_g': out['v_kv_norm_g'], 'v_w_uq': out['v_w_uq'], 'v_w_ukv': out['v_w_ukv'], 'v_w_o': out['v_w_o'], 'v_ln1_g': out['v_ln1_g'], 'v_ln1_b': out['v_ln1_b'], 'v_w_up': out['v_w_up'], 'v_w_down': out['v_w_down'], 'v_ln2_g': out['v_ln2_g'], 'v_ln2_b': out['v_ln2_b']}


def _loss(weights, diff, rest, loss_target):
    with _jax.named_scope("forward"):
        args = {**rest, TWIN_DIFF_INPUT: diff, **{k: w.astype(_WEIGHT_DTYPES[k]) for k, w in weights.items()}}
        y = _forward(args)
    with _jax.named_scope("loss_head"):
        err = _jnp.square(y.astype(_jnp.float32) - loss_target)
        return 0.5 * _jnp.sum(_jnp.mean(err, axis=-1)) if err.ndim else 0.5 * err


def _adamw(w, g, m, v):
    m = ADAM_B1 * m + (1.0 - ADAM_B1) * g
    v = ADAM_B2 * v + (1.0 - ADAM_B2) * _jnp.square(g)
    m_hat = m / (1.0 - ADAM_B1 ** ADAM_STEP)
    v_hat = v / (1.0 - ADAM_B2 ** ADAM_STEP)
    delta = -ADAM_LR * (m_hat / (_jnp.sqrt(v_hat) + ADAM_EPS) + ADAM_WD * w)
    return delta, m, v


def reference(x, c, ln_in_g, ln_in_b, w_ada, b_ada, w_in, q_norm_g, kv_norm_g, w_uq, w_ukv, w_o, ln1_g, ln1_b, w_up, w_down, ln2_g, ln2_b, loss_target, m_ln_in_g, m_ln_in_b, m_w_ada, m_b_ada, m_w_in, m_q_norm_g, m_kv_norm_g, m_w_uq, m_w_ukv, m_w_o, m_ln1_g, m_ln1_b, m_w_up, m_w_down, m_ln2_g, m_ln2_b, v_ln_in_g, v_ln_in_b, v_w_ada, v_b_ada, v_w_in, v_q_norm_g, v_kv_norm_g, v_w_uq, v_w_ukv, v_w_o, v_ln1_g, v_ln1_b, v_w_up, v_w_down, v_ln2_g, v_ln2_b):
    given = dict(x=x, c=c, ln_in_g=ln_in_g, ln_in_b=ln_in_b, w_ada=w_ada, b_ada=b_ada, w_in=w_in, q_norm_g=q_norm_g, kv_norm_g=kv_norm_g, w_uq=w_uq, w_ukv=w_ukv, w_o=w_o, ln1_g=ln1_g, ln1_b=ln1_b, w_up=w_up, w_down=w_down, ln2_g=ln2_g, ln2_b=ln2_b, loss_target=loss_target, m_ln_in_g=m_ln_in_g, m_ln_in_b=m_ln_in_b, m_w_ada=m_w_ada, m_b_ada=m_b_ada, m_w_in=m_w_in, m_q_norm_g=m_q_norm_g, m_kv_norm_g=m_kv_norm_g, m_w_uq=m_w_uq, m_w_ukv=m_w_ukv, m_w_o=m_w_o, m_ln1_g=m_ln1_g, m_ln1_b=m_ln1_b, m_w_up=m_w_up, m_w_down=m_w_down, m_ln2_g=m_ln2_g, m_ln2_b=m_ln2_b, v_ln_in_g=v_ln_in_g, v_ln_in_b=v_ln_in_b, v_w_ada=v_w_ada, v_b_ada=v_b_ada, v_w_in=v_w_in, v_q_norm_g=v_q_norm_g, v_kv_norm_g=v_kv_norm_g, v_w_uq=v_w_uq, v_w_ukv=v_w_ukv, v_w_o=v_w_o, v_ln1_g=v_ln1_g, v_ln1_b=v_ln1_b, v_w_up=v_w_up, v_w_down=v_w_down, v_ln2_g=v_ln2_g, v_ln2_b=v_ln2_b)
    weights = {n: given[n] for n in TWIN_WEIGHTS}
    shared = {n: given[n] for n in SHARED_INPUTS}
    per_example = {n: given[n] for n in ['x', 'c']}
    grad_fn = _jax.value_and_grad(_loss, argnums=(0, 1))

    def one_microbatch(ex, loss_target):
        ex = dict(ex)
        diff = ex.pop(TWIN_DIFF_INPUT)
        return grad_fn(weights, diff, {**shared, **ex}, loss_target)

    if N_MICROBATCH == 1:
        loss, (grad_w, grad_x) = one_microbatch(per_example, given["loss_target"])
    else:
        def body(carry, xs):
            loss_sum, grad_sum = carry
            l_k, (gw_k, gx_k) = one_microbatch(xs[0], xs[1])
            with _jax.named_scope("update"):
                return (loss_sum + l_k, _jax.tree.map(_jnp.add, grad_sum, gw_k)), gx_k

        init = (_jnp.zeros((), _jnp.float32), _jax.tree.map(_jnp.zeros_like, weights))
        (loss, grad_w), grad_x = _jax.lax.scan(body, init, (per_example, given["loss_target"]))
    with _jax.named_scope("update"):
        delta_w, new_m, new_v = {}, {}, {}
        for n in TWIN_WEIGHTS:
            delta_w[n], new_m[n], new_v[n] = _adamw(weights[n], grad_w[n], given["m_" + n], given["v_" + n])
    return (loss, grad_x, *[grad_w[n] for n in TWIN_WEIGHTS], *[delta_w[n] for n in TWIN_WEIGHTS],
            *[new_m[n] for n in TWIN_WEIGHTS], *[new_v[n] for n in TWIN_WEIGHTS])
```

```python
import functools
import math

import jax
import jax.numpy as jnp
from jax import lax
from jax.experimental import pallas as pl
from jax.experimental.pallas import tpu as pltpu

F32 = jnp.float32
BF16 = jnp.bfloat16
MESH_IDS = pl.DeviceIdType.MESH

D_MODEL = 1024
HEADS = 8
HEAD_PAIRS = HEADS // 2
SB_W = 512
MLA_W = 512
NOPE = 64
ROPE = 32
Q_RANK = 384
KV_RANK = 256
D_IN = 2208
D_IN_PAD = 2304
D_FF = 4096
N_MOD = 6
LN_EPS = 1e-5
RMS_EPS = 1e-6
ALPHA = 2.0 ** 0.25
ROPE_BASE = 10000.0
SB_SCALE = 64 ** -0.5
MLA_SCALE = 96 ** -0.5
ADAM_LR = 0.001
ADAM_B1 = 0.9
ADAM_B2 = 0.999
ADAM_EPS = 1e-08
ADAM_WD = 0.01
ADAM_STEP = 10

LANES = 128
ROW_TILE = 256
ATTN_TILE = 256
VMEM_LIMIT = 56 << 20

NT = (((1,), (1,)), ((), ()))
TN = (((0,), (0,)), ((), ()))


def _params(sem=None):
    return pltpu.CompilerParams(vmem_limit_bytes=VMEM_LIMIT, dimension_semantics=sem)


def _const_spec(shape):
    zeros = (0,) * len(shape)
    return pl.BlockSpec(shape, lambda *_: zeros, pipeline_mode=pl.Buffered(1))


def _dot(a, b, dims=None):
    if dims is None:
        return jnp.dot(a, b, preferred_element_type=F32)
    return lax.dot_general(a, b, dims, preferred_element_type=F32)


def _mean(v):
    return jnp.mean(v, axis=-1, keepdims=True)


def _rowsum(v):
    return jnp.sum(v, axis=0, keepdims=True)


def _ln_fwd(y, g, b):
    mu = _mean(y)
    yc = y - mu
    rstd = lax.rsqrt(_mean(yc * yc) + LN_EPS)
    xhat = yc * rstd
    return xhat * g + b, xhat, rstd


def _ln_bwd(dx, xhat, rstd, g):
    dxh = dx * g
    return rstd * (dxh - _mean(dxh) - xhat * _mean(dxh * xhat))


def _rope(v, cos, sin_a, sin_b):
    return v * cos + pltpu.roll(v, 112, 1) * sin_a + pltpu.roll(v, 16, 1) * sin_b


def _rope_t(dv, cos, sin_a, sin_b):
    return dv * cos + pltpu.roll(dv * sin_a, 16, 1) + pltpu.roll(dv * sin_b, 112, 1)


def _my_place():
    return lax.axis_index("x"), lax.axis_index("y"), lax.axis_index("c")


def _gather8(v, name):
    m, n = v.shape

    def body(v_ref, out_ref, send_sems, recv_sems, local_sem):
        x, y, c = _my_place()
        me, sibling = (x, y, c), (x, y, 1 - c)
        chips = [(1 - x, y), (x, 1 - y), (1 - x, 1 - y)]

        def rows(px, py, pc):
            return out_ref.at[4 * px + 2 * py + pc]

        def copy(k, block, to, src=None):
            return pltpu.make_async_remote_copy(
                src_ref=rows(*block) if src is None else src, dst_ref=rows(*block),
                send_sem=send_sems.at[k], recv_sem=recv_sems.at[k], device_id=to, device_id_type=MESH_IDS)

        mine = pltpu.make_async_copy(v_ref, rows(*me), local_sem)
        mine.start()
        first = [copy(0, me, sibling, src=v_ref)]
        first += [copy(1 + j, me, (*chip, c), src=v_ref) for j, chip in enumerate(chips)]
        for cp in first:
            cp.start()
        passed = [copy(4 + j, (*chip, c), sibling) for j, chip in enumerate(chips)]
        for j, chip in enumerate(chips):
            copy(1 + j, (*chip, c), me).wait_recv()
            passed[j].start()
        copy(0, sibling, me).wait_recv()
        for j, chip in enumerate(chips):
            copy(4 + j, (*chip, 1 - c), me).wait_recv()
        for cp in first + passed:
            cp.wait_send()
        mine.wait()

    return pl.pallas_call(
        body, name=name,
        out_shape=jax.ShapeDtypeStruct((8, m, n), v.dtype),
        in_specs=[pl.BlockSpec(memory_space=pl.ANY)],
        out_specs=pl.BlockSpec(memory_space=pl.ANY),
        scratch_shapes=[pltpu.SemaphoreType.DMA((7,)), pltpu.SemaphoreType.DMA((7,)), pltpu.SemaphoreType.DMA],
    )(v)


def _swap_cores(blocks, name):
    _, m, n = blocks.shape

    def body(g_ref, out_ref, send_sems, recv_sems):
        x, y, c = _my_place()
        copies = [pltpu.make_async_remote_copy(
            src_ref=g_ref.at[2 * j + (1 - c)], dst_ref=out_ref.at[j],
            send_sem=send_sems.at[j], recv_sem=recv_sems.at[j],
            device_id=(x, y, 1 - c), device_id_type=MESH_IDS) for j in range(4)]
        for cp in copies:
            cp.start()
        for cp in copies:
            cp.wait()

    return pl.pallas_call(
        body, name=name,
        out_shape=jax.ShapeDtypeStruct((4, m, n), blocks.dtype),
        in_specs=[pl.BlockSpec(memory_space=pl.ANY)],
        out_specs=pl.BlockSpec(memory_space=pl.ANY),
        scratch_shapes=[pltpu.SemaphoreType.DMA((4,)), pltpu.SemaphoreType.DMA((4,))],
    )(blocks)


def _scatter_chips(parts, name):
    _, m, n = parts.shape
    flips = [(1, 0), (0, 1), (1, 1)]

    def body(p_ref, out_ref, send_sems, recv_sems):
        x, y, c = _my_place()
        copies = []
        for k, (fx, fy) in enumerate(flips):
            tx = 1 - x if fx else x
            ty = 1 - y if fy else y
            copies.append(pltpu.make_async_remote_copy(
                src_ref=p_ref.at[2 * tx + ty], dst_ref=out_ref.at[k],
                send_sem=send_sems.at[k], recv_sem=recv_sems.at[k],
                device_id=(tx, ty, c), device_id_type=MESH_IDS))
        for cp in copies:
            cp.start()
        for cp in copies:
            cp.wait()

    return pl.pallas_call(
        body, name=name,
        out_shape=jax.ShapeDtypeStruct((3, m, n), parts.dtype),
        in_specs=[pl.BlockSpec(memory_space=pl.ANY)],
        out_specs=pl.BlockSpec(memory_space=pl.ANY),
        scratch_shapes=[pltpu.SemaphoreType.DMA((3,)), pltpu.SemaphoreType.DMA((3,))],
    )(parts)


def _swap_one(v, name):
    def body(v_ref, out_ref, send_sem, recv_sem):
        x, y, c = _my_place()
        cp = pltpu.make_async_remote_copy(src_ref=v_ref, dst_ref=out_ref, send_sem=send_sem, recv_sem=recv_sem,
                                          device_id=(x, y, 1 - c), device_id_type=MESH_IDS)
        cp.start()
        cp.wait()

    return pl.pallas_call(
        body, name=name,
        out_shape=jax.ShapeDtypeStruct(v.shape, v.dtype),
        in_specs=[pl.BlockSpec(memory_space=pl.ANY)],
        out_specs=pl.BlockSpec(memory_space=pl.ANY),
        scratch_shapes=[pltpu.SemaphoreType.DMA, pltpu.SemaphoreType.DMA],
    )(v)


def _ada_fwd(c_all, w_ada_sh, b_ada_sh):
    nb, cols = c_all.shape[0], w_ada_sh.shape[1]
    tn = 512

    def body(c_ref, w_ref, b_ref, o_ref):
        cv = c_ref[...]
        act = (cv * jax.nn.sigmoid(cv)).astype(BF16)
        o_ref[...] = _dot(act, w_ref[...].astype(BF16)) + b_ref[...]

    return pl.pallas_call(
        body, name="ada_fwd", grid=(cols // tn,),
        out_shape=jax.ShapeDtypeStruct((nb, cols), F32),
        in_specs=[pl.BlockSpec((nb, D_MODEL), lambda j: (0, 0)),
                  pl.BlockSpec((D_MODEL, tn), lambda j: (0, j)),
                  pl.BlockSpec((1, tn), lambda j: (0, j))],
        out_specs=pl.BlockSpec((nb, tn), lambda j: (0, j)),
        compiler_params=_params(("arbitrary",)),
    )(c_all, w_ada_sh, b_ada_sh)


def _ada_bwd(c_all, dmod_sh):
    nb, cols = dmod_sh.shape
    tn = 512

    def body(c_ref, d_ref, o_ref):
        cv = c_ref[...]
        act = (cv * jax.nn.sigmoid(cv)).astype(BF16)
        o_ref[...] = _dot(act, d_ref[...].astype(BF16), TN)

    return pl.pallas_call(
        body, name="ada_bwd", grid=(cols // tn,),
        out_shape=jax.ShapeDtypeStruct((D_MODEL, cols), F32),
        in_specs=[pl.BlockSpec((nb, D_MODEL), lambda j: (0, 0)),
                  pl.BlockSpec((nb, tn), lambda j: (0, j))],
        out_specs=pl.BlockSpec((D_MODEL, tn), lambda j: (0, j)),
        compiler_params=_params(("arbitrary",)),
    )(c_all, dmod_sh)


def _sum_lead(v, name):
    k, m, n = v.shape

    def body(v_ref, o_ref):
        acc = v_ref[0]
        for i in range(1, k):
            acc = acc + v_ref[i]
        o_ref[...] = acc

    return pl.pallas_call(
        body, name=name, out_shape=jax.ShapeDtypeStruct((m, n), F32),
        in_specs=[pl.BlockSpec((k, m, n), lambda: (0, 0, 0))],
        out_specs=pl.BlockSpec((m, n), lambda: (0, 0)),
        compiler_params=_params(),
    )(v)


def _adamw(w, g, m, v, name):
    rows, cols = w.shape
    tr = rows
    while tr * cols * 4 > (2 << 20) and tr % 16 == 0:
        tr //= 2

    def body(w_ref, g_ref, m_ref, v_ref, d_ref, mo_ref, vo_ref):
        gv = g_ref[...]
        mn = ADAM_B1 * m_ref[...] + (1.0 - ADAM_B1) * gv
        vn = ADAM_B2 * v_ref[...] + (1.0 - ADAM_B2) * (gv * gv)
        m_hat = mn / (1.0 - ADAM_B1 ** ADAM_STEP)
        v_hat = vn / (1.0 - ADAM_B2 ** ADAM_STEP)
        d_ref[...] = -ADAM_LR * (m_hat / (jnp.sqrt(v_hat) + ADAM_EPS) + ADAM_WD * w_ref[...])
        mo_ref[...] = mn
        vo_ref[...] = vn

    spec = pl.BlockSpec((tr, cols), lambda i: (i, 0))
    shape = jax.ShapeDtypeStruct((rows, cols), F32)
    return pl.pallas_call(
        body, name=name, grid=(rows // tr,), out_shape=(shape, shape, shape),
        in_specs=[spec, spec, spec, spec], out_specs=(spec, spec, spec),
        compiler_params=_params(("arbitrary",)),
    )(w, g, m, v)


def _add_pairs(blocks, recv, c_idx):
    _, m, n = blocks.shape
    tr = 8
    while m % (tr * 2) == 0 and tr * 2 * n * 4 <= (1 << 20):
        tr *= 2
    if m % tr:
        tr = m

    def body(c_ref, a_ref, b_ref, o_ref, ob_ref):
        s = a_ref[...] + b_ref[...]
        o_ref[...] = s
        ob_ref[...] = s.astype(BF16)

    grid_spec = pltpu.PrefetchScalarGridSpec(
        num_scalar_prefetch=1, grid=(4, m // tr),
        in_specs=[pl.BlockSpec((1, tr, n), lambda j, i, c: (2 * j + c[0], i, 0)),
                  pl.BlockSpec((1, tr, n), lambda j, i, c: (j, i, 0))],
        out_specs=(pl.BlockSpec((1, tr, n), lambda j, i, c: (j, i, 0)),
                   pl.BlockSpec((1, tr, n), lambda j, i, c: (j, i, 0))))
    return pl.pallas_call(
        body, name="grad_add_cores", grid_spec=grid_spec,
        out_shape=(jax.ShapeDtypeStruct((4, m, n), F32), jax.ShapeDtypeStruct((4, m, n), BF16)),
        compiler_params=_params(("arbitrary", "arbitrary")),
    )(c_idx, blocks, recv)


def _add_chips(own, recv):
    m, n = own.shape
    tr = 8
    while m % (tr * 2) == 0 and tr * 2 * n * 4 <= (1 << 20):
        tr *= 2
    if m % tr:
        tr = m

    def body(a_ref, r_ref, o_ref):
        acc = a_ref[...]
        for k in range(3):
            acc = acc + r_ref[k].astype(F32)
        o_ref[...] = acc

    return pl.pallas_call(
        body, name="grad_add_chips", grid=(m // tr,),
        out_shape=jax.ShapeDtypeStruct((m, n), F32),
        in_specs=[pl.BlockSpec((tr, n), lambda i: (i, 0)), pl.BlockSpec((3, tr, n), lambda i: (0, i, 0))],
        out_specs=pl.BlockSpec((tr, n), lambda i: (i, 0)),
        compiler_params=_params(("arbitrary",)),
    )(own, recv)


def _row_spec(cols):
    return pl.BlockSpec((ROW_TILE, cols), lambda i: (i, 0))


def _mod_spec(tiles_per_seq):
    return pl.BlockSpec((1, 8, D_MODEL), lambda i: (i // tiles_per_seq, 0, 0))


def _table_spec(tiles_per_seq):
    return pl.BlockSpec((ROW_TILE, LANES), lambda i: (i % tiles_per_seq, 0))


def _fwd_in(x, mod, ln_g, ln_b, w_in, q_g, kv_g, w_uq, w_ukv, cos_t, sin_a, sin_b, seq):
    rows = x.shape[0]
    tm = ROW_TILE
    tps = seq // tm

    def body(x_ref, mod_ref, g_ref, b_ref, win_ref, qg_ref, kvg_ref, wuq_ref, wukv_ref, cos_ref, sa_ref, sb_ref,
             x0_ref, h_ref, qkv_ref, lat_ref, qp_ref, kp_ref, vm_ref):
        x0, _, _ = _ln_fwd(x_ref[...], g_ref[...], b_ref[...])
        x0_ref[...] = x0
        h = (x0 * (1.0 + mod_ref[0, 1:2, :]) + mod_ref[0, 0:1, :]).astype(BF16)
        h_ref[...] = h
        proj = _dot(h, win_ref[...])
        qkv_ref[...] = proj[:, :3 * SB_W].astype(BF16)
        lat_ref[...] = proj[:, 3 * SB_W:3 * SB_W + Q_RANK + KV_RANK]
        cq = proj[:, 3 * SB_W:3 * SB_W + Q_RANK]
        ckv = proj[:, 3 * SB_W + Q_RANK:3 * SB_W + Q_RANK + KV_RANK]
        kr = proj[:, D_IN_PAD - LANES:]
        cos, sa, sb = cos_ref[...], sa_ref[...], sb_ref[...]
        cqn = (cq * lax.rsqrt(_mean(cq * cq) + RMS_EPS) * qg_ref[...]).astype(BF16)
        q_all = _dot(cqn, wuq_ref[...])
        for hd in range(HEADS):
            sl = slice(hd * LANES, (hd + 1) * LANES)
            qp_ref[:, sl] = _rope(q_all[:, sl], cos, sa, sb).astype(BF16)
        ckvn = (ckv * lax.rsqrt(_mean(ckv * ckv) + RMS_EPS) * kvg_ref[...]).astype(BF16)
        kv = _dot(ckvn, wukv_ref[...])
        kr_rot = _rope(kr, cos, sa, sb)
        for hd in range(HEADS):
            sl = slice(hd * LANES, (hd + 1) * LANES)
            kp_ref[:, sl] = (kv[:, sl] + kr_rot).astype(BF16)
        vm_ref[...] = kv[:, HEADS * LANES:].astype(BF16)

    outs = [(D_MODEL, F32), (D_MODEL, BF16), (3 * SB_W, BF16), (Q_RANK + KV_RANK, F32),
            (HEADS * LANES, BF16), (HEADS * LANES, BF16), (MLA_W, BF16)]
    return pl.pallas_call(
        body, name="fwd_in", grid=(rows // tm,),
        out_shape=tuple(jax.ShapeDtypeStruct((rows, n), dt) for n, dt in outs),
        in_specs=[_row_spec(D_MODEL), _mod_spec(tps), _const_spec((1, D_MODEL)), _const_spec((1, D_MODEL)),
                  _const_spec(w_in.shape), _const_spec((1, Q_RANK)), _const_spec((1, KV_RANK)),
                  _const_spec(w_uq.shape), _const_spec(w_ukv.shape),
                  _table_spec(tps), _table_spec(tps), _table_spec(tps)],
        out_specs=tuple(_row_spec(n) for n, _ in outs),
        compiler_params=_params(("arbitrary",)),
    )(x, mod, ln_g, ln_b, w_in, q_g, kv_g, w_uq, w_ukv, cos_t, sin_a, sin_b)


def _fwd_mix(sb_y, mla_y, x0, mod, w_o, ln_g, ln_b, seq):
    rows = x0.shape[0]
    tm = ROW_TILE
    tps = seq // tm

    def body(sb_ref, ml_ref, x0_ref, mod_ref, wo_ref, g_ref, b_ref, mix_ref, y1_ref, x1_ref, h2_ref):
        mix = _dot(sb_ref[...].astype(BF16), wo_ref[:SB_W, :]) + _dot(ml_ref[...].astype(BF16), wo_ref[SB_W:, :])
        mix_ref[...] = mix
        y1 = ALPHA * x0_ref[...] + (1.0 + mod_ref[0, 2:3, :]) * mix
        y1_ref[...] = y1
        x1, _, _ = _ln_fwd(y1, g_ref[...], b_ref[...])
        x1_ref[...] = x1
        h2_ref[...] = (x1 * (1.0 + mod_ref[0, 4:5, :]) + mod_ref[0, 3:4, :]).astype(BF16)

    outs = [(D_MODEL, F32), (D_MODEL, F32), (D_MODEL, F32), (D_MODEL, BF16)]
    return pl.pallas_call(
        body, name="fwd_mix", grid=(rows // tm,),
        out_shape=tuple(jax.ShapeDtypeStruct((rows, n), dt) for n, dt in outs),
        in_specs=[_row_spec(SB_W), _row_spec(MLA_W), _row_spec(D_MODEL), _mod_spec(tps), _const_spec(w_o.shape),
                  _const_spec((1, D_MODEL)), _const_spec((1, D_MODEL))],
        out_specs=tuple(_row_spec(n) for n, _ in outs),
        compiler_params=_params(("arbitrary",)),
    )(sb_y, mla_y, x0, mod, w_o, ln_g, ln_b)


def _fwd_mlp(h2, x1, mod, w_up, w_down, seq):
    rows = x1.shape[0]
    tm = ROW_TILE
    tps = seq // tm

    def body(h2_ref, x1_ref, mod_ref, wu_ref, wd_ref, u_ref, ff_ref, y2_ref):
        u = _dot(h2_ref[...], wu_ref[...])
        u_ref[...] = u.astype(BF16)
        act = jnp.square(jnp.maximum(u, 0.0)).astype(BF16)
        ff = _dot(act, wd_ref[...])
        ff_ref[...] = ff
        y2_ref[...] = ALPHA * x1_ref[...] + (1.0 + mod_ref[0, 5:6, :]) * ff

    outs = [(D_FF, BF16), (D_MODEL, F32), (D_MODEL, F32)]
    return pl.pallas_call(
        body, name="fwd_mlp", grid=(rows // tm,),
        out_shape=tuple(jax.ShapeDtypeStruct((rows, n), dt) for n, dt in outs),
        in_specs=[_row_spec(D_MODEL), _row_spec(D_MODEL), _mod_spec(tps), _const_spec(w_up.shape),
                  _const_spec(w_down.shape)],
        out_specs=tuple(_row_spec(n) for n, _ in outs),
        compiler_params=_params(("arbitrary",)),
    )(h2, x1, mod, w_up, w_down)


def _acc_spec(rows=8, cols=D_MODEL):
    return pl.BlockSpec((rows, cols), lambda i: (0, 0))


def _bwd_out(y2, tgt, ff, u, mod, ln_g, ln_b, w_down, seq):
    rows = y2.shape[0]
    nb = rows // seq
    tm = ROW_TILE
    tps = seq // tm

    def body(y2_ref, t_ref, ff_ref, u_ref, mod_ref, g_ref, b_ref, wd_ref, dy2_ref, dff_ref, du_ref, acc_ref, dmod_ref):
        i = pl.program_id(0)

        @pl.when(i == 0)
        def _():
            acc_ref[...] = jnp.zeros_like(acc_ref)

        @pl.when(i % tps == 0)
        def _():
            dmod_ref[...] = jnp.zeros_like(dmod_ref)

        g = g_ref[...]
        x2, xhat, rstd = _ln_fwd(y2_ref[...], g, b_ref[...])
        err = x2 - t_ref[...]
        dx2 = err * (1.0 / D_MODEL)
        acc_ref[0:1, :] += _rowsum(dx2 * xhat)
        acc_ref[1:2, :] += _rowsum(dx2)
        acc_ref[2:3, :] += _rowsum(err * err) * (0.5 / D_MODEL)
        dy2 = _ln_bwd(dx2, xhat, rstd, g)
        dy2_ref[...] = dy2
        dmod_ref[0, 5:6, :] += _rowsum(dy2 * ff_ref[...])
        dff = ((1.0 + mod_ref[0, 5:6, :]) * dy2).astype(BF16)
        dff_ref[...] = dff
        da = _dot(dff, wd_ref[...], NT)
        du_ref[...] = (da * (2.0 * jnp.maximum(u_ref[...].astype(F32), 0.0))).astype(BF16)

    outs = [(D_MODEL, F32), (D_MODEL, BF16), (D_FF, BF16)]
    return pl.pallas_call(
        body, name="bwd_out", grid=(rows // tm,),
        out_shape=tuple(jax.ShapeDtypeStruct((rows, n), dt) for n, dt in outs)
        + (jax.ShapeDtypeStruct((8, D_MODEL), F32), jax.ShapeDtypeStruct((nb, 8, D_MODEL), F32)),
        in_specs=[_row_spec(D_MODEL), _row_spec(D_MODEL), _row_spec(D_MODEL), _row_spec(D_FF), _mod_spec(tps),
                  _const_spec((1, D_MODEL)), _const_spec((1, D_MODEL)), _const_spec(w_down.shape)],
        out_specs=tuple(_row_spec(n) for n, _ in outs) + (_acc_spec(), _mod_spec(tps)),
        compiler_params=_params(("arbitrary",)),
    )(y2, tgt, ff, u, mod, ln_g, ln_b, w_down)


def _bwd_mid(du, y1, mix, dy2, mod, ln_g, ln_b, w_up, w_o, seq):
    rows = y1.shape[0]
    nb = rows // seq
    tm = ROW_TILE
    tps = seq // tm

    def body(du_ref, y1_ref, mix_ref, dy2_ref, mod_ref, g_ref, b_ref, wu_ref, wo_ref,
             dy1_ref, dmix_ref, do_ref, acc_ref, dmod_ref):
        i = pl.program_id(0)

        @pl.when(i == 0)
        def _():
            acc_ref[...] = jnp.zeros_like(acc_ref)

        @pl.when(i % tps == 0)
        def _():
            dmod_ref[...] = jnp.zeros_like(dmod_ref)

        g = g_ref[...]
        x1, xhat, rstd = _ln_fwd(y1_ref[...], g, b_ref[...])
        dh2 = _dot(du_ref[...], wu_ref[...], NT)
        dmod_ref[0, 3:4, :] += _rowsum(dh2)
        dmod_ref[0, 4:5, :] += _rowsum(dh2 * x1)
        dx1 = ALPHA * dy2_ref[...] + dh2 * (1.0 + mod_ref[0, 4:5, :])
        acc_ref[0:1, :] += _rowsum(dx1 * xhat)
        acc_ref[1:2, :] += _rowsum(dx1)
        dy1 = _ln_bwd(dx1, xhat, rstd, g)
        dy1_ref[...] = dy1
        dmod_ref[0, 2:3, :] += _rowsum(dy1 * mix_ref[...])
        dmix = ((1.0 + mod_ref[0, 2:3, :]) * dy1).astype(BF16)
        dmix_ref[...] = dmix
        do_ref[...] = _dot(dmix, wo_ref[...], NT)

    outs = [(D_MODEL, F32), (D_MODEL, BF16), (D_MODEL, F32)]
    return pl.pallas_call(
        body, name="bwd_mid", grid=(rows // tm,),
        out_shape=tuple(jax.ShapeDtypeStruct((rows, n), dt) for n, dt in outs)
        + (jax.ShapeDtypeStruct((8, D_MODEL), F32), jax.ShapeDtypeStruct((nb, 8, D_MODEL), F32)),
        in_specs=[_row_spec(D_FF), _row_spec(D_MODEL), _row_spec(D_MODEL), _row_spec(D_MODEL), _mod_spec(tps),
                  _const_spec((1, D_MODEL)), _const_spec((1, D_MODEL)), _const_spec(w_up.shape),
                  _const_spec(w_o.shape)],
        out_specs=tuple(_row_spec(n) for n, _ in outs) + (_acc_spec(), _mod_spec(tps)),
        compiler_params=_params(("arbitrary",)),
    )(du, y1, mix, dy2, mod, ln_g, ln_b, w_up, w_o)


def _bwd_in(dqp, dkp, dvm, dq_sb, dk_sb, dv_sb, lat, x, x0, dy1, mod, ln_g, ln_b, w_in, q_g, kv_g, w_uq, w_ukv,
            cos_t, sin_a, sin_b, seq):
    rows = x.shape[0]
    nb = rows // seq
    tm = ROW_TILE
    tps = seq // tm
    n_lat = Q_RANK + KV_RANK

    def body(dqp_ref, dkp_ref, dvm_ref, dqs_ref, dks_ref, dvs_ref, lat_ref, x_ref, x0_ref, dy1_ref, mod_ref,
             g_ref, b_ref, win_ref, qg_ref, kvg_ref, wuq_ref, wukv_ref, cos_ref, sa_ref, sb_ref,
             dx_ref, dproj_ref, dqall_ref, dkv_ref, latn_ref, acc_ref, accl_ref, dmod_ref):
        i = pl.program_id(0)

        @pl.when(i == 0)
        def _():
            acc_ref[...] = jnp.zeros_like(acc_ref)
            accl_ref[...] = jnp.zeros_like(accl_ref)

        @pl.when(i % tps == 0)
        def _():
            dmod_ref[...] = jnp.zeros_like(dmod_ref)

        cos, sa, sb = cos_ref[...], sa_ref[...], sb_ref[...]
        lane = lax.broadcasted_iota(jnp.int32, (tm, LANES), 1)
        for hd in range(HEADS):
            sl = slice(hd * LANES, (hd + 1) * LANES)
            dqall_ref[:, sl] = _rope_t(dqp_ref[:, sl], cos, sa, sb).astype(BF16)
        dcqn = _dot(dqall_ref[...], wuq_ref[...], NT)
        cq = lat_ref[:, :Q_RANK]
        qg = qg_ref[...]
        rq = lax.rsqrt(_mean(cq * cq) + RMS_EPS)
        cqn = cq * rq
        latn_ref[:, :Q_RANK] = (cqn * qg).astype(BF16)
        accl_ref[0:1, :Q_RANK] += _rowsum(dcqn * cqn)
        dqg = dcqn * qg
        dcq = rq * (dqg - cqn * _mean(dqg * cqn))
        dkr = jnp.zeros((tm, LANES), F32)
        for hd in range(HEADS):
            sl = slice(hd * LANES, (hd + 1) * LANES)
            dk = dkp_ref[:, sl]
            dkr = dkr + dk
            dkv_ref[:, sl] = jnp.where(lane < NOPE, dk, 0.0).astype(BF16)
        dkv_ref[:, HEADS * LANES:] = dvm_ref[...].astype(BF16)
        dckvn = _dot(dkv_ref[...], wukv_ref[...], NT)
        ckv = lat_ref[:, Q_RANK:]
        kvg = kvg_ref[...]
        rkv = lax.rsqrt(_mean(ckv * ckv) + RMS_EPS)
        ckvn = ckv * rkv
        latn_ref[:, Q_RANK:] = (ckvn * kvg).astype(BF16)
        accl_ref[1:2, :KV_RANK] += _rowsum(dckvn * ckvn)
        dkg = dckvn * kvg
        dckv = rkv * (dkg - ckvn * _mean(dkg * ckvn))
        dkr = _rope_t(jnp.where(lane >= NOPE, dkr, 0.0), cos, sa, sb)
        dproj_ref[:, :SB_W] = dqs_ref[...]
        dproj_ref[:, SB_W:2 * SB_W] = dks_ref[...].astype(BF16)
        dproj_ref[:, 2 * SB_W:3 * SB_W] = dvs_ref[...].astype(BF16)
        dproj_ref[:, 3 * SB_W:3 * SB_W + Q_RANK] = dcq.astype(BF16)
        dproj_ref[:, 3 * SB_W + Q_RANK:3 * SB_W + n_lat] = dckv.astype(BF16)
        dproj_ref[:, D_IN_PAD - LANES:] = dkr.astype(BF16)
        dh = _dot(dproj_ref[...], win_ref[...], NT)
        x0 = x0_ref[...]
        dmod_ref[0, 0:1, :] += _rowsum(dh)
        dmod_ref[0, 1:2, :] += _rowsum(dh * x0)
        dx0 = ALPHA * dy1_ref[...] + dh * (1.0 + mod_ref[0, 1:2, :])
        g = g_ref[...]
        _, xhat, rstd = _ln_fwd(x_ref[...], g, b_ref[...])
        acc_ref[0:1, :] += _rowsum(dx0 * xhat)
        acc_ref[1:2, :] += _rowsum(dx0)
        dx_ref[...] = _ln_bwd(dx0, xhat, rstd, g)

    outs = [(D_MODEL, F32), (D_IN_PAD, BF16), (HEADS * LANES, BF16), (HEADS * LANES + MLA_W, BF16), (n_lat, BF16)]
    return pl.pallas_call(
        body, name="bwd_in", grid=(rows // tm,),
        out_shape=tuple(jax.ShapeDtypeStruct((rows, n), dt) for n, dt in outs)
        + (jax.ShapeDtypeStruct((8, D_MODEL), F32), jax.ShapeDtypeStruct((8, Q_RANK), F32),
           jax.ShapeDtypeStruct((nb, 8, D_MODEL), F32)),
        in_specs=[_row_spec(HEADS * LANES), _row_spec(HEADS * LANES), _row_spec(MLA_W),
                  _row_spec(SB_W), _row_spec(SB_W), _row_spec(SB_W), _row_spec(n_lat),
                  _row_spec(D_MODEL), _row_spec(D_MODEL), _row_spec(D_MODEL), _mod_spec(tps),
                  _const_spec((1, D_MODEL)), _const_spec((1, D_MODEL)), _const_spec(w_in.shape),
                  _const_spec((1, Q_RANK)), _const_spec((1, KV_RANK)), _const_spec(w_uq.shape),
                  _const_spec(w_ukv.shape), _table_spec(tps), _table_spec(tps), _table_spec(tps)],
        out_specs=tuple(_row_spec(n) for n, _ in outs) + (_acc_spec(), _acc_spec(8, Q_RANK), _mod_spec(tps)),
        compiler_params=_params(("arbitrary",)),
    )(dqp, dkp, dvm, dq_sb, dk_sb, dv_sb, lat, x, x0, dy1, mod, ln_g, ln_b, w_in, q_g, kv_g, w_uq, w_ukv,
      cos_t, sin_a, sin_b)


def _wgrad(a, b, name, pre=None, tm=512, tn=1024, tk=512):
    rows, m = a.shape
    n = b.shape[1]
    tm, tn = min(tm, m), min(tn, n)
    if m % tm:
        tm = m
    if n % tn:
        tn = n

    def body(a_ref, b_ref, o_ref):
        @pl.when(pl.program_id(2) == 0)
        def _():
            o_ref[...] = jnp.zeros_like(o_ref)

        av = a_ref[...]
        if pre == "relu2":
            av = jnp.square(jnp.maximum(av.astype(F32), 0.0))
        o_ref[...] += _dot(av.astype(BF16), b_ref[...].astype(BF16), TN)

    return pl.pallas_call(
        body, name=name, grid=(m // tm, n // tn, rows // tk),
        out_shape=jax.ShapeDtypeStruct((m, n), F32),
        in_specs=[pl.BlockSpec((tk, tm), lambda i, j, k: (k, i)), pl.BlockSpec((tk, tn), lambda i, j, k: (k, j))],
        out_specs=pl.BlockSpec((tm, tn), lambda i, j, k: (i, j)),
        compiler_params=_params(("arbitrary", "arbitrary", "arbitrary")),
    )(a, b)


def _head_mask(lane, hh):
    return jnp.where((lane >= 64) if hh else (lane < 64), 1.0, 0.0).astype(BF16)


def _tri(t, kind):
    r = lax.broadcasted_iota(jnp.int32, (t, t), 0)
    c = lax.broadcasted_iota(jnp.int32, (t, t), 1)
    return jnp.where(r > c if kind == "suffix" else r < c, 1.0, 0.0).astype(BF16)


def _split_dot(v, tri):
    hi = v.astype(BF16)
    lo = (v - hi.astype(F32)).astype(BF16)
    return _dot(hi, tri) + _dot(lo, tri)


def _sb_logits(qm, ks, valid):
    z = _dot(qm, ks, NT) * SB_SCALE
    l1p = jnp.log(1.0 + jnp.exp(-jnp.abs(z)))
    log_keep = -(jnp.maximum(z, 0.0) + l1p)
    log_beta = jnp.minimum(z, 0.0) - l1p
    if valid is not None:
        log_keep = jnp.where(valid, log_keep, 0.0)
    return log_keep, log_beta


def _sb_fwd(qkv, seq):
    rows = qkv.shape[0]
    nb = rows // seq
    t = min(ATTN_TILE, seq)
    nq = seq // t

    def body(q_ref, k_ref, v_ref, tri_ref, o_ref, car_ref):
        i = pl.program_id(2)
        lane = lax.broadcasted_iota(jnp.int32, (t, LANES), 1)
        row = lax.broadcasted_iota(jnp.int32, (t, t), 0)
        col = lax.broadcasted_iota(jnp.int32, (t, t), 1)
        strict = col < row
        tri = tri_ref[...]
        qb = q_ref[...]
        acc = jnp.zeros((t, LANES), F32)
        cars = jnp.zeros((t, LANES), F32)
        for hh in range(2):
            head = _head_mask(lane, hh)
            qm = qb * head

            def step(kb, carry, valid):
                c_sum, acc, cars = carry
                start = pl.multiple_of(kb * t, t)
                ks = k_ref[pl.ds(start, t), :]
                vs = v_ref[pl.ds(start, t), :]
                log_keep, log_beta = _sb_logits(qm, ks, valid)
                w = jnp.exp(log_beta + _split_dot(log_keep, tri) + c_sum)
                if valid is not None:
                    w = jnp.where(valid, w, 0.0)
                acc = acc + _dot(w.astype(BF16), vs * head)
                cars = jnp.where(lane == hh * 8 + kb, c_sum, cars)
                return c_sum + jnp.sum(log_keep, axis=1, keepdims=True), acc, cars

            carry = step(i, (jnp.zeros((t, 1), F32), acc, cars), strict)
            carry = lax.fori_loop(0, i, lambda j, cr: step(i - 1 - j, cr, None), carry)
            _, acc, cars = carry
        o_ref[...] = acc
        car_ref[...] = cars

    qspec = pl.BlockSpec((t, LANES), lambda b, p, i: (b * nq + i, p))
    return pl.pallas_call(
        body, name="sb_fwd", grid=(nb, HEAD_PAIRS, nq),
        out_shape=(jax.ShapeDtypeStruct((rows, SB_W), F32), jax.ShapeDtypeStruct((rows, SB_W), F32)),
        in_specs=[qspec,
                  pl.BlockSpec((seq, LANES), lambda b, p, i: (b, HEAD_PAIRS + p)),
                  pl.BlockSpec((seq, LANES), lambda b, p, i: (b, 2 * HEAD_PAIRS + p)),
                  _const_spec((t, t))],
        out_specs=(qspec, qspec),
        compiler_params=_params(("arbitrary", "arbitrary", "arbitrary")),
    )(qkv, qkv, qkv, _tri(t, "suffix"))


def _sb_bwd(qkv, d_out, cars, seq):
    rows = qkv.shape[0]
    nb = rows // seq
    t = min(ATTN_TILE, seq)
    nq = seq // t

    def body(q_ref, k_ref, v_ref, do_ref, car_ref, tri_ref, pre_ref, dq_ref, dk_ref, dv_ref):
        i = pl.program_id(2)

        @pl.when(i == 0)
        def _():
            dk_ref[...] = jnp.zeros_like(dk_ref)
            dv_ref[...] = jnp.zeros_like(dv_ref)

        lane = lax.broadcasted_iota(jnp.int32, (t, LANES), 1)
        row = lax.broadcasted_iota(jnp.int32, (t, t), 0)
        col = lax.broadcasted_iota(jnp.int32, (t, t), 1)
        strict = col < row
        tri, pre = tri_ref[...], pre_ref[...]
        qb = q_ref[...]
        dob = do_ref[...].astype(BF16)
        cars = car_ref[...]
        dq = jnp.zeros((t, LANES), F32)
        for hh in range(2):
            head = _head_mask(lane, hh)
            qm = qb * head
            dom = dob * head

            def step(kb, carry, valid):
                g_pre, dq = carry
                start = pl.multiple_of(kb * t, t)
                ks = k_ref[pl.ds(start, t), :]
                vs = v_ref[pl.ds(start, t), :]
                log_keep, log_beta = _sb_logits(qm, ks, valid)
                c_sum = jnp.sum(jnp.where(lane == hh * 8 + kb, cars, 0.0), axis=1, keepdims=True)
                w = jnp.exp(log_beta + _split_dot(log_keep, tri) + c_sum)
                if valid is not None:
                    w = jnp.where(valid, w, 0.0)
                g = _dot(dom, vs, NT) * w
                g_before = g_pre + _split_dot(g, pre)
                dz = g * jnp.exp(log_keep) - jnp.exp(log_beta) * g_before
                if valid is not None:
                    dz = jnp.where(valid, dz, 0.0)
                dzb = (dz * SB_SCALE).astype(BF16)
                dq = dq + _dot(dzb, ks * head)
                dk_ref[pl.ds(start, t), :] += _dot(dzb, qm, TN)
                dv_ref[pl.ds(start, t), :] += _dot(w.astype(BF16), dom, TN)
                return g_pre + jnp.sum(g, axis=1, keepdims=True), dq

            carry = lax.fori_loop(0, i, lambda kb, cr: step(kb, cr, None), (jnp.zeros((t, 1), F32), dq))
            _, dq = step(i, carry, strict)
        dq_ref[...] = dq.astype(BF16)

    qspec = pl.BlockSpec((t, LANES), lambda b, p, i: (b * nq + i, p))
    kspec_out = pl.BlockSpec((seq, LANES), lambda b, p, i: (b, p))
    return pl.pallas_call(
        body, name="sb_bwd", grid=(nb, HEAD_PAIRS, nq),
        out_shape=(jax.ShapeDtypeStruct((rows, SB_W), BF16), jax.ShapeDtypeStruct((rows, SB_W), F32),
                   jax.ShapeDtypeStruct((rows, SB_W), F32)),
        in_specs=[qspec,
                  pl.BlockSpec((seq, LANES), lambda b, p, i: (b, HEAD_PAIRS + p)),
                  pl.BlockSpec((seq, LANES), lambda b, p, i: (b, 2 * HEAD_PAIRS + p)),
                  qspec, qspec, _const_spec((t, t)), _const_spec((t, t))],
        out_specs=(qspec, kspec_out, kspec_out),
        compiler_params=_params(("arbitrary", "arbitrary", "arbitrary")),
    )(qkv, qkv, qkv, d_out, cars, _tri(t, "suffix"), _tri(t, "prefix"))


def _mla_scores(qh, ks, allowed):
    s = _dot(qh, ks, NT) * MLA_SCALE
    if allowed is not None:
        s = jnp.where(allowed, s, jnp.finfo(F32).min)
    return s


def _mla_fwd(qp, kp, vm, seq, chunk=64):
    rows = qp.shape[0]
    nb = rows // seq
    t = min(ATTN_TILE, seq)
    nq = seq // t
    shift = int(math.log2(chunk))

    def body(q_ref, k_ref, v_ref, o_ref, lse_ref):
        i = pl.program_id(2)
        lane = lax.broadcasted_iota(jnp.int32, (t, LANES), 1)
        row = lax.broadcasted_iota(jnp.int32, (t, t), 0)
        col = lax.broadcasted_iota(jnp.int32, (t, t), 1)
        allowed_diag = jnp.right_shift(col, shift) <= jnp.right_shift(row, shift)
        out = jnp.zeros((t, LANES), F32)
        lses = jnp.zeros((t, LANES), F32)
        for hh in range(2):
            head = _head_mask(lane, hh)
            hs = slice(hh * LANES, (hh + 1) * LANES)
            qh = q_ref[:, hs]

            def step(kb, carry, allowed):
                m_run, l_run, acc = carry
                start = pl.multiple_of(kb * t, t)
                ks = k_ref[pl.ds(start, t), hs]
                vs = v_ref[pl.ds(start, t), :]
                s = _mla_scores(qh, ks, allowed)
                m_new = jnp.maximum(m_run, jnp.max(s, axis=1, keepdims=True))
                p = jnp.exp(s - m_new)
                scale = jnp.exp(m_run - m_new)
                l_run = scale * l_run + jnp.sum(p, axis=1, keepdims=True)
                acc = scale * acc + _dot(p.astype(BF16), vs * head)
                return m_new, l_run, acc

            init = (jnp.full((t, 1), jnp.finfo(F32).min, F32), jnp.zeros((t, 1), F32), jnp.zeros((t, LANES), F32))
            carry = step(i, init, allowed_diag)
            m_run, l_run, acc = lax.fori_loop(0, i, lambda kb, cr: step(kb, cr, None), carry)
            out = out + acc / l_run
            lses = jnp.where(lane == hh, m_run + jnp.log(l_run), lses)
        o_ref[...] = out
        lse_ref[...] = lses

    ospec = pl.BlockSpec((t, LANES), lambda b, p, i: (b * nq + i, p))
    return pl.pallas_call(
        body, name="mla_fwd", grid=(nb, HEAD_PAIRS, nq),
        out_shape=(jax.ShapeDtypeStruct((rows, MLA_W), F32), jax.ShapeDtypeStruct((rows, MLA_W), F32)),
        in_specs=[pl.BlockSpec((t, 2 * LANES), lambda b, p, i: (b * nq + i, p)),
                  pl.BlockSpec((seq, 2 * LANES), lambda b, p, i: (b, p)),
                  pl.BlockSpec((seq, LANES), lambda b, p, i: (b, p))],
        out_specs=(ospec, ospec),
        compiler_params=_params(("arbitrary", "arbitrary", "arbitrary")),
    )(qp, kp, vm)


def _mla_bwd(qp, kp, vm, d_out, out, lse, seq, chunk=64):
    rows = qp.shape[0]
    nb = rows // seq
    t = min(ATTN_TILE, seq)
    nq = seq // t
    shift = int(math.log2(chunk))

    def body(q_ref, k_ref, v_ref, do_ref, o_ref, lse_ref, dq_ref, dk_ref, dv_ref):
        i = pl.program_id(2)

        @pl.when(i == 0)
        def _():
            dk_ref[...] = jnp.zeros_like(dk_ref)
            dv_ref[...] = jnp.zeros_like(dv_ref)

        lane = lax.broadcasted_iota(jnp.int32, (t, LANES), 1)
        row = lax.broadcasted_iota(jnp.int32, (t, t), 0)
        col = lax.broadcasted_iota(jnp.int32, (t, t), 1)
        allowed_diag = jnp.right_shift(col, shift) <= jnp.right_shift(row, shift)
        do = do_ref[...]
        dob = do.astype(BF16)
        d_o = do * o_ref[...]
        lses = lse_ref[...]
        for hh in range(2):
            head = _head_mask(lane, hh)
            hs = slice(hh * LANES, (hh + 1) * LANES)
            qh = q_ref[:, hs]
            dom = dob * head
            delta = jnp.sum(jnp.where((lane >= 64) if hh else (lane < 64), d_o, 0.0), axis=1, keepdims=True)
            lse_h = jnp.sum(jnp.where(lane == hh, lses, 0.0), axis=1, keepdims=True)

            def step(kb, dq, allowed):
                start = pl.multiple_of(kb * t, t)
                ks = k_ref[pl.ds(start, t), hs]
                vs = v_ref[pl.ds(start, t), :]
                p = jnp.exp(_mla_scores(qh, ks, allowed) - lse_h)
                ds = (p * (_dot(dom, vs, NT) - delta) * MLA_SCALE).astype(BF16)
                dk_ref[pl.ds(start, t), hs] += _dot(ds, qh, TN)
                dv_ref[pl.ds(start, t), :] += _dot(p.astype(BF16), dom, TN)
                return dq + _dot(ds, ks)

            dq = lax.fori_loop(0, i, lambda kb, cr: step(kb, cr, None), jnp.zeros((t, LANES), F32))
            dq_ref[:, hs] = step(i, dq, allowed_diag)

    ospec = pl.BlockSpec((t, LANES), lambda b, p, i: (b * nq + i, p))
    return pl.pallas_call(
        body, name="mla_bwd", grid=(nb, HEAD_PAIRS, nq),
        out_shape=(jax.ShapeDtypeStruct((rows, HEADS * LANES), F32), jax.ShapeDtypeStruct((rows, HEADS * LANES), F32),
                   jax.ShapeDtypeStruct((rows, MLA_W), F32)),
        in_specs=[pl.BlockSpec((t, 2 * LANES), lambda b, p, i: (b * nq + i, p)),
                  pl.BlockSpec((seq, 2 * LANES), lambda b, p, i: (b, p)),
                  pl.BlockSpec((seq, LANES), lambda b, p, i: (b, p)),
                  pl.BlockSpec((t, LANES), lambda b, p, i: (b * nq + i, HEAD_PAIRS + p)),
                  ospec, ospec],
        out_specs=(pl.BlockSpec((t, 2 * LANES), lambda b, p, i: (b * nq + i, p)),
                   pl.BlockSpec((seq, 2 * LANES), lambda b, p, i: (b, p)),
                   pl.BlockSpec((seq, LANES), lambda b, p, i: (b, p))),
        compiler_params=_params(("arbitrary", "arbitrary", "arbitrary")),
    )(qp, kp, vm, d_out, out, lse)


PACK_COLS = 1024
PACK_ROWS = 1536
BIG = (("w_in", 1024, 552, 1), ("w_uq", 384, 192, 1), ("w_ukv", 256, 256, 1),
       ("w_o", 256, 1024, 0), ("w_up", 1024, 1024, 1), ("w_down", 1024, 1024, 0))


def _pack_rows(r, c):
    return (r // 2) * c // PACK_COLS


def _pad_pack(p):
    return jnp.pad(p, ((0, 0), (0, PACK_ROWS - p.shape[1]), (0, 0)))


def _pack_halves(shards):
    parts = [s.reshape(2, _pack_rows(r, c), PACK_COLS) for s, (_, r, c, _) in zip(shards, BIG)]
    return _pad_pack(jnp.concatenate(parts, axis=1))


def _unpack_halves(packed):
    out, at = [], 0
    for _, r, c, _ in BIG:
        n = _pack_rows(r, c)
        out.append(packed[:, at:at + n, :].reshape(r, c))
        at += n
    return out


def _unpack_full(gathered):
    out, at = [], 0
    for _, r, c, axis in BIG:
        n = _pack_rows(r, c)
        shards = gathered[:, at:at + n, :].reshape(4, r, c)
        at += n
        out.append(shards.reshape(4 * r, c) if axis == 0 else jnp.moveaxis(shards, 0, 1).reshape(r, 4 * c))
    return out


def _pack_full(grads):
    parts = []
    for gr, (_, r, c, axis) in zip(grads, BIG):
        shards = gr.reshape(4, r, c) if axis == 0 else jnp.moveaxis(gr.reshape(r, 4, c), 1, 0)
        parts.append(shards.reshape(8, _pack_rows(r, c), PACK_COLS))
    return _pad_pack(jnp.concatenate(parts, axis=1))


def _pad_w_in(w_in):
    z = jnp.zeros((D_MODEL, 1), w_in.dtype)
    return jnp.concatenate([w_in[:, :2176], jnp.tile(z, (1, 64)), w_in[:, 2176:], jnp.tile(z, (1, 32))], axis=1)


def _unpad_w_in(g):
    return jnp.concatenate([g[:, :2176], g[:, 2240:2272]], axis=1)


def _pad_heads(w, used):
    k = w.shape[0]
    w3 = w.reshape(k, HEADS, used)
    return jnp.pad(w3, ((0, 0), (0, 0), (0, LANES - used))).reshape(k, HEADS * LANES)


def _unpad_heads(g, used):
    k = g.shape[0]
    return g.reshape(k, HEADS, LANES)[:, :, :used].reshape(k, HEADS * used)


def _rope_tables(seq):
    inv_freq = 1.0 / (ROPE_BASE ** (jnp.arange(0, ROPE, 2, dtype=F32) / ROPE))
    ang = jnp.arange(seq, dtype=F32)[:, None] * inv_freq[None, :]
    cos, sin = jnp.cos(ang), jnp.sin(ang)
    one, zero = jnp.ones((seq, NOPE), F32), jnp.zeros((seq, NOPE), F32)
    z16, z32 = jnp.zeros((seq, 16), F32), jnp.zeros((seq, 32), F32)
    cos_t = jnp.concatenate([one, cos, cos, jnp.ones((seq, 32), F32)], axis=1)
    sin_a = jnp.concatenate([zero, -sin, z16, z32], axis=1)
    sin_b = jnp.concatenate([zero, z16, sin, z32], axis=1)
    return cos_t, sin_a, sin_b


SMALL = (("ln_in_g", 1024), ("ln_in_b", 1024), ("b_ada", 6144), ("q_norm_g", 384), ("kv_norm_g", 256),
         ("ln1_g", 1024), ("ln1_b", 1024), ("ln2_g", 1024), ("ln2_b", 1024))
SMALL_TOTAL = sum(n for _, n in SMALL)
SMALL_ROWS = -(-SMALL_TOTAL // LANES // 8) * 8


def _pack_small(vals):
    flat = jnp.concatenate([v.reshape(-1) for v in vals])
    return jnp.pad(flat, (0, SMALL_ROWS * LANES - SMALL_TOTAL)).reshape(SMALL_ROWS, LANES)


def _unpack_small(packed, like):
    flat, out, at = packed.reshape(-1), [], 0
    for (_, n), ref in zip(SMALL, like):
        out.append(flat[at:at + n].reshape(ref.shape))
        at += n
    return out


def kernel(x, c, ln_in_g, ln_in_b, w_ada, b_ada, w_in, q_norm_g, kv_norm_g, w_uq, w_ukv, w_o, ln1_g, ln1_b, w_up, w_down, ln2_g, ln2_b, loss_target, m_ln_in_g, m_ln_in_b, m_w_ada, m_b_ada, m_w_in, m_q_norm_g, m_kv_norm_g, m_w_uq, m_w_ukv, m_w_o, m_ln1_g, m_ln1_b, m_w_up, m_w_down, m_ln2_g, m_ln2_b, v_ln_in_g, v_ln_in_b, v_w_ada, v_b_ada, v_w_in, v_q_norm_g, v_kv_norm_g, v_w_uq, v_w_ukv, v_w_o, v_ln1_g, v_ln1_b, v_w_up, v_w_down, v_ln2_g, v_ln2_b):
    nb, seq, _ = x.shape
    rows = nb * seq
    ix, iy, ic = lax.axis_index("x"), lax.axis_index("y"), lax.axis_index("c")
    chip = 2 * ix + iy
    dev = 2 * chip + ic

    shards = [w_in[0], w_uq[0], w_ukv[0], w_o[0], w_up[0], w_down[0]]
    packed = _pack_halves([s.astype(BF16) for s in shards])
    my_half = lax.dynamic_index_in_dim(packed, ic, 0, keepdims=False)
    f_in, f_uq, f_ukv, f_o, f_up, f_down = _unpack_full(_gather8(my_half, "gather_weights"))
    w_in_p = _pad_w_in(f_in)
    uq3 = f_uq.reshape(Q_RANK, HEADS, NOPE + ROPE)
    w_uq_p = jnp.pad(uq3, ((0, 0), (0, 0), (0, LANES - NOPE - ROPE))).reshape(Q_RANK, HEADS * LANES)
    w_ukv_p = jnp.concatenate([_pad_heads(f_ukv[:, :HEADS * NOPE], NOPE), f_ukv[:, HEADS * NOPE:]], axis=1)

    n_all = 8 * nb
    c_all = _gather8(c.reshape(-1, LANES), "gather_c").reshape(n_all, D_MODEL)
    ada_cols = w_ada.shape[2]
    b_sh = lax.dynamic_slice_in_dim(b_ada, chip * ada_cols, ada_cols, axis=1)
    mod_sh = _ada_fwd(c_all, w_ada[0], b_sh)
    mod_g = _gather8(mod_sh, "gather_mod")[0::2]
    mod_all = jnp.moveaxis(mod_g, 0, 1).reshape(n_all, N_MOD * D_MODEL)
    mod_mine = lax.dynamic_slice_in_dim(mod_all, dev * nb, nb, axis=0).reshape(nb, N_MOD, D_MODEL)
    mod = jnp.pad(mod_mine, ((0, 0), (0, 8 - N_MOD), (0, 0)))

    cos_t, sin_a, sin_b = _rope_tables(seq)
    row2 = lambda v: v.reshape(1, -1)

    x2d = x.reshape(rows, D_MODEL)
    x0, h, qkv, lat, qp, kp, vm = _fwd_in(x2d, mod, row2(ln_in_g), row2(ln_in_b), w_in_p, q_norm_g, kv_norm_g,
                                          w_uq_p, w_ukv_p, cos_t, sin_a, sin_b, seq)
    sb_y, cars = _sb_fwd(qkv, seq)
    mla_y, lse = _mla_fwd(qp, kp, vm, seq)
    mix, y1, x1, h2 = _fwd_mix(sb_y, mla_y, x0, mod, f_o, ln1_g, ln1_b, seq)
    u, ff, y2 = _fwd_mlp(h2, x1, mod, f_up, f_down, seq)

    dy2, dff, du, acc2, dmod_a = _bwd_out(y2, loss_target.reshape(rows, D_MODEL), ff, u, mod, ln2_g, ln2_b, f_down, seq)
    dy1, dmix, d_attn, acc1, dmod_b = _bwd_mid(du, y1, mix, dy2, mod, ln1_g, ln1_b, f_up, f_o, seq)
    dq_sb, dk_sb, dv_sb = _sb_bwd(qkv, d_attn, cars, seq)
    dqp, dkp, dvm = _mla_bwd(qp, kp, vm, d_attn, mla_y, lse, seq)
    grad_x, dproj, dqall, dkv, latn, acc0, acc_lat, dmod_c = _bwd_in(
        dqp, dkp, dvm, dq_sb, dk_sb, dv_sb, lat, x2d, x0, dy1, mod, row2(ln_in_g), row2(ln_in_b), w_in_p,
        q_norm_g, kv_norm_g, w_uq_p, w_ukv_p, cos_t, sin_a, sin_b, seq)

    g_down = _wgrad(u, dff, "wgrad_down", pre="relu2")
    g_up = _wgrad(h2, du, "wgrad_up")
    g_o = jnp.concatenate([_wgrad(sb_y, dmix, "wgrad_o_sb"), _wgrad(mla_y, dmix, "wgrad_o_mla")], axis=0)
    g_in = _unpad_w_in(_wgrad(h, dproj, "wgrad_in", tn=768))
    g_uq = _unpad_heads(_wgrad(latn[:, :Q_RANK], dqall, "wgrad_uq"), NOPE + ROPE)
    g_ukv_p = _wgrad(latn[:, Q_RANK:], dkv, "wgrad_ukv", tn=512)
    g_ukv = jnp.concatenate([_unpad_heads(g_ukv_p[:, :HEADS * LANES], NOPE), g_ukv_p[:, HEADS * LANES:]], axis=1)

    blocks = _pack_full([g_in, g_uq, g_ukv, g_o, g_up, g_down])
    from_sibling = _swap_cores(blocks, "grads_to_sibling")
    part, part_bf = _add_pairs(blocks, from_sibling, ic.reshape(1).astype(jnp.int32))
    from_chips = _scatter_chips(part_bf, "grads_to_chips")
    own = lax.dynamic_index_in_dim(part, chip, 0, keepdims=False)
    half = _add_chips(own, from_chips)
    other = _swap_one(half, "grads_halves")
    both = jnp.where(ic == 0, jnp.stack([half, other]), jnp.stack([other, half]))
    gs_in, gs_uq, gs_ukv, gs_o, gs_up, gs_down = _unpack_halves(both)

    dmod = (dmod_a + dmod_b + dmod_c)[:, :N_MOD, :]
    small_part = _pack_small([acc0[0], acc0[1], jnp.zeros((N_MOD * D_MODEL,), F32), acc_lat[0, :Q_RANK],
                              acc_lat[1, :KV_RANK], acc1[0], acc1[1], acc2[0], acc2[1]])
    payload = jnp.concatenate([small_part, dmod.reshape(-1, LANES)], axis=0)
    gathered = _gather8(payload, "gather_small")
    small_sum = _sum_lead(gathered[:, :SMALL_ROWS, :], "sum_small")
    dmod_all = gathered[:, SMALL_ROWS:, :].reshape(n_all, N_MOD * D_MODEL)
    g_b_ada = _sum_lead(dmod_all.reshape(n_all, N_MOD * D_MODEL // LANES, LANES), "sum_b_ada").reshape(1, -1)
    dmod_sh = lax.dynamic_slice_in_dim(dmod_all, chip * ada_cols, ada_cols, axis=1)
    g_w_ada = _ada_bwd(c_all, dmod_sh)

    small_like = [ln_in_g, ln_in_b, b_ada, q_norm_g, kv_norm_g, ln1_g, ln1_b, ln2_g, ln2_b]
    small_grads = _unpack_small(small_sum, small_like)
    small_grads[2] = g_b_ada
    loss = lax.psum(jnp.sum(acc2[2]), ("x", "y", "c"))

    big_w = {"w_ada": (w_ada[0], g_w_ada, m_w_ada[0], v_w_ada[0]), "w_in": (w_in[0], gs_in, m_w_in[0], v_w_in[0]),
             "w_uq": (w_uq[0], gs_uq, m_w_uq[0], v_w_uq[0]), "w_ukv": (w_ukv[0], gs_ukv, m_w_ukv[0], v_w_ukv[0]),
             "w_o": (w_o[0], gs_o, m_w_o[0], v_w_o[0]), "w_up": (w_up[0], gs_up, m_w_up[0], v_w_up[0]),
             "w_down": (w_down[0], gs_down, m_w_down[0], v_w_down[0])}
    res = {}
    for name, (w, g, m, v) in big_w.items():
        d, mn, vn = _adamw(w, g, m, v, "adamw_" + name)
        res[name] = (g[None], d[None], mn[None], vn[None])
    small_m = [m_ln_in_g, m_ln_in_b, m_b_ada, m_q_norm_g, m_kv_norm_g, m_ln1_g, m_ln1_b, m_ln2_g, m_ln2_b]
    small_v = [v_ln_in_g, v_ln_in_b, v_b_ada, v_q_norm_g, v_kv_norm_g, v_ln1_g, v_ln1_b, v_ln2_g, v_ln2_b]
    sd, sm, sv = _adamw(_pack_small(small_like), _pack_small(small_grads), _pack_small(small_m), _pack_small(small_v),
                        "adamw_small")
    for (name, _), g, d, mn, vn in zip(SMALL, small_grads, _unpack_small(sd, small_like),
                                       _unpack_small(sm, small_like), _unpack_small(sv, small_like)):
        res[name] = (g, d, mn, vn)

    order = ["ln_in_g", "ln_in_b", "w_ada", "b_ada", "w_in", "q_norm_g", "kv_norm_g", "w_uq", "w_ukv", "w_o",
             "ln1_g", "ln1_b", "w_up", "w_down", "ln2_g", "ln2_b"]
    outs = [loss, grad_x.reshape(nb, seq, D_MODEL)]
    for k in range(4):
        outs += [res[name][k] for name in order]
    return tuple(outs)
```

```python
import functools
import math

import jax
import jax.numpy as jnp
from jax import lax
from jax.experimental import pallas as pl
from jax.experimental.pallas import tpu as pltpu

F32 = jnp.float32
BF16 = jnp.bfloat16
MESH_IDS = pl.DeviceIdType.MESH

D_MODEL = 1024
HEADS = 8
HEAD_PAIRS = HEADS // 2
SB_W = 512
MLA_W = 512
NOPE = 64
ROPE = 32
Q_RANK = 384
KV_RANK = 256
D_IN = 2208
D_IN_PAD = 2304
D_FF = 4096
N_MOD = 6
LN_EPS = 1e-5
RMS_EPS = 1e-6
ALPHA = 2.0 ** 0.25
ROPE_BASE = 10000.0
SB_SCALE = 64 ** -0.5
MLA_SCALE = 96 ** -0.5
ADAM_LR = 0.001
ADAM_B1 = 0.9
ADAM_B2 = 0.999
ADAM_EPS = 1e-08
ADAM_WD = 0.01
ADAM_STEP = 10

LANES = 128
ROW_TILE = 256
ATTN_TILE = 256
ATTN_PAIRS = 2
VMEM_LIMIT = 56 << 20

NT = (((1,), (1,)), ((), ()))
TN = (((0,), (0,)), ((), ()))


def _params(sem=None):
    return pltpu.CompilerParams(vmem_limit_bytes=VMEM_LIMIT, dimension_semantics=sem)


def _const_spec(shape):
    zeros = (0,) * len(shape)
    return pl.BlockSpec(shape, lambda *_: zeros, pipeline_mode=pl.Buffered(1))


def _dot(a, b, dims=None):
    if dims is None:
        return jnp.dot(a, b, preferred_element_type=F32)
    return lax.dot_general(a, b, dims, preferred_element_type=F32)


def _mean(v):
    return jnp.mean(v, axis=-1, keepdims=True)


def _rowsum(v):
    return jnp.sum(v, axis=0, keepdims=True)


def _ln_fwd(y, g, b):
    mu = _mean(y)
    yc = y - mu
    rstd = lax.rsqrt(_mean(yc * yc) + LN_EPS)
    xhat = yc * rstd
    return xhat * g + b, xhat, rstd


def _ln_bwd(dx, xhat, rstd, g):
    dxh = dx * g
    return rstd * (dxh - _mean(dxh) - xhat * _mean(dxh * xhat))


def _rope(v, cos, sin_a, sin_b):
    return v * cos + pltpu.roll(v, 112, 1) * sin_a + pltpu.roll(v, 16, 1) * sin_b


def _rope_t(dv, cos, sin_a, sin_b):
    return dv * cos + pltpu.roll(dv * sin_a, 16, 1) + pltpu.roll(dv * sin_b, 112, 1)


def _my_place():
    return lax.axis_index("x"), lax.axis_index("y"), lax.axis_index("c")


def _gather8(v, name):
    m, n = v.shape

    def body(v_ref, out_ref, send_sems, recv_sems, local_sem):
        x, y, c = _my_place()
        me, sibling = (x, y, c), (x, y, 1 - c)
        chips = [(1 - x, y), (x, 1 - y), (1 - x, 1 - y)]

        def rows(px, py, pc):
            return out_ref.at[4 * px + 2 * py + pc]

        def copy(k, block, to, src=None):
            return pltpu.make_async_remote_copy(
                src_ref=rows(*block) if src is None else src, dst_ref=rows(*block),
                send_sem=send_sems.at[k], recv_sem=recv_sems.at[k], device_id=to, device_id_type=MESH_IDS)

        mine = pltpu.make_async_copy(v_ref, rows(*me), local_sem)
        mine.start()
        first = [copy(0, me, sibling, src=v_ref)]
        first += [copy(1 + j, me, (*chip, c), src=v_ref) for j, chip in enumerate(chips)]
        for cp in first:
            cp.start()
        passed = [copy(4 + j, (*chip, c), sibling) for j, chip in enumerate(chips)]
        for j, chip in enumerate(chips):
            copy(1 + j, (*chip, c), me).wait_recv()
            passed[j].start()
        copy(0, sibling, me).wait_recv()
        for j, chip in enumerate(chips):
            copy(4 + j, (*chip, 1 - c), me).wait_recv()
        for cp in first + passed:
            cp.wait_send()
        mine.wait()

    return pl.pallas_call(
        body, name=name,
        out_shape=jax.ShapeDtypeStruct((8, m, n), v.dtype),
        in_specs=[pl.BlockSpec(memory_space=pl.ANY)],
        out_specs=pl.BlockSpec(memory_space=pl.ANY),
        scratch_shapes=[pltpu.SemaphoreType.DMA((7,)), pltpu.SemaphoreType.DMA((7,)), pltpu.SemaphoreType.DMA],
    )(v)


def _swap_cores(blocks, name):
    _, m, n = blocks.shape

    def body(g_ref, out_ref, send_sems, recv_sems):
        x, y, c = _my_place()
        copies = [pltpu.make_async_remote_copy(
            src_ref=g_ref.at[2 * j + (1 - c)], dst_ref=out_ref.at[j],
            send_sem=send_sems.at[j], recv_sem=recv_sems.at[j],
            device_id=(x, y, 1 - c), device_id_type=MESH_IDS) for j in range(4)]
        for cp in copies:
            cp.start()
        for cp in copies:
            cp.wait()

    return pl.pallas_call(
        body, name=name,
        out_shape=jax.ShapeDtypeStruct((4, m, n), blocks.dtype),
        in_specs=[pl.BlockSpec(memory_space=pl.ANY)],
        out_specs=pl.BlockSpec(memory_space=pl.ANY),
        scratch_shapes=[pltpu.SemaphoreType.DMA((4,)), pltpu.SemaphoreType.DMA((4,))],
    )(blocks)


def _scatter_chips(parts, name):
    _, m, n = parts.shape
    flips = [(1, 0), (0, 1), (1, 1)]

    def body(p_ref, out_ref, send_sems, recv_sems):
        x, y, c = _my_place()
        copies = []
        for k, (fx, fy) in enumerate(flips):
            tx = 1 - x if fx else x
            ty = 1 - y if fy else y
            copies.append(pltpu.make_async_remote_copy(
                src_ref=p_ref.at[2 * tx + ty], dst_ref=out_ref.at[k],
                send_sem=send_sems.at[k], recv_sem=recv_sems.at[k],
                device_id=(tx, ty, c), device_id_type=MESH_IDS))
        for cp in copies:
            cp.start()
        for cp in copies:
            cp.wait()

    return pl.pallas_call(
        body, name=name,
        out_shape=jax.ShapeDtypeStruct((3, m, n), parts.dtype),
        in_specs=[pl.BlockSpec(memory_space=pl.ANY)],
        out_specs=pl.BlockSpec(memory_space=pl.ANY),
        scratch_shapes=[pltpu.SemaphoreType.DMA((3,)), pltpu.SemaphoreType.DMA((3,))],
    )(parts)


def _swap_one(v, name):
    def body(v_ref, out_ref, send_sem, recv_sem):
        x, y, c = _my_place()
        cp = pltpu.make_async_remote_copy(src_ref=v_ref, dst_ref=out_ref, send_sem=send_sem, recv_sem=recv_sem,
                                          device_id=(x, y, 1 - c), device_id_type=MESH_IDS)
        cp.start()
        cp.wait()

    return pl.pallas_call(
        body, name=name,
        out_shape=jax.ShapeDtypeStruct(v.shape, v.dtype),
        in_specs=[pl.BlockSpec(memory_space=pl.ANY)],
        out_specs=pl.BlockSpec(memory_space=pl.ANY),
        scratch_shapes=[pltpu.SemaphoreType.DMA, pltpu.SemaphoreType.DMA],
    )(v)


def _ada_fwd(c_all, w_ada_sh, b_ada_sh):
    nb, cols = c_all.shape[0], w_ada_sh.shape[1]
    tn = 512

    def body(c_ref, w_ref, b_ref, o_ref):
        cv = c_ref[...]
        act = (cv * jax.nn.sigmoid(cv)).astype(BF16)
        o_ref[...] = _dot(act, w_ref[...].astype(BF16)) + b_ref[...]

    return pl.pallas_call(
        body, name="ada_fwd", grid=(cols // tn,),
        out_shape=jax.ShapeDtypeStruct((nb, cols), F32),
        in_specs=[pl.BlockSpec((nb, D_MODEL), lambda j: (0, 0)),
                  pl.BlockSpec((D_MODEL, tn), lambda j: (0, j)),
                  pl.BlockSpec((1, tn), lambda j: (0, j))],
        out_specs=pl.BlockSpec((nb, tn), lambda j: (0, j)),
        compiler_params=_params(("arbitrary",)),
    )(c_all, w_ada_sh, b_ada_sh)


def _ada_bwd(c_all, dmod_sh):
    nb, cols = dmod_sh.shape
    tn = 512

    def body(c_ref, d_ref, o_ref):
        cv = c_ref[...]
        act = (cv * jax.nn.sigmoid(cv)).astype(BF16)
        o_ref[...] = _dot(act, d_ref[...].astype(BF16), TN)

    return pl.pallas_call(
        body, name="ada_bwd", grid=(cols // tn,),
        out_shape=jax.ShapeDtypeStruct((D_MODEL, cols), F32),
        in_specs=[pl.BlockSpec((nb, D_MODEL), lambda j: (0, 0)),
                  pl.BlockSpec((nb, tn), lambda j: (0, j))],
        out_specs=pl.BlockSpec((D_MODEL, tn), lambda j: (0, j)),
        compiler_params=_params(("arbitrary",)),
    )(c_all, dmod_sh)


def _sum_lead(v, name):
    k, m, n = v.shape

    def body(v_ref, o_ref):
        acc = v_ref[0]
        for i in range(1, k):
            acc = acc + v_ref[i]
        o_ref[...] = acc

    return pl.pallas_call(
        body, name=name, out_shape=jax.ShapeDtypeStruct((m, n), F32),
        in_specs=[pl.BlockSpec((k, m, n), lambda: (0, 0, 0))],
        out_specs=pl.BlockSpec((m, n), lambda: (0, 0)),
        compiler_params=_params(),
    )(v)


def _adamw(w, g, m, v, name):
    rows, cols = w.shape
    tr = rows
    while tr * cols * 4 > (2 << 20) and tr % 16 == 0:
        tr //= 2

    def body(w_ref, g_ref, m_ref, v_ref, d_ref, mo_ref, vo_ref):
        gv = g_ref[...]
        mn = ADAM_B1 * m_ref[...] + (1.0 - ADAM_B1) * gv
        vn = ADAM_B2 * v_ref[...] + (1.0 - ADAM_B2) * (gv * gv)
        m_hat = mn / (1.0 - ADAM_B1 ** ADAM_STEP)
        v_hat = vn / (1.0 - ADAM_B2 ** ADAM_STEP)
        d_ref[...] = -ADAM_LR * (m_hat / (jnp.sqrt(v_hat) + ADAM_EPS) + ADAM_WD * w_ref[...])
        mo_ref[...] = mn
        vo_ref[...] = vn

    spec = pl.BlockSpec((tr, cols), lambda i: (i, 0))
    shape = jax.ShapeDtypeStruct((rows, cols), F32)
    return pl.pallas_call(
        body, name=name, grid=(rows // tr,), out_shape=(shape, shape, shape),
        in_specs=[spec, spec, spec, spec], out_specs=(spec, spec, spec),
        compiler_params=_params(("arbitrary",)),
    )(w, g, m, v)


def _add_pairs(blocks, recv, c_idx):
    _, m, n = blocks.shape
    tr = 8
    while m % (tr * 2) == 0 and tr * 2 * n * 4 <= (1 << 20):
        tr *= 2
    if m % tr:
        tr = m

    def body(c_ref, a_ref, b_ref, o_ref, ob_ref):
        s = a_ref[...] + b_ref[...]
        o_ref[...] = s
        ob_ref[...] = s.astype(BF16)

    grid_spec = pltpu.PrefetchScalarGridSpec(
        num_scalar_prefetch=1, grid=(4, m // tr),
        in_specs=[pl.BlockSpec((1, tr, n), lambda j, i, c: (2 * j + c[0], i, 0)),
                  pl.BlockSpec((1, tr, n), lambda j, i, c: (j, i, 0))],
        out_specs=(pl.BlockSpec((1, tr, n), lambda j, i, c: (j, i, 0)),
                   pl.BlockSpec((1, tr, n), lambda j, i, c: (j, i, 0))))
    return pl.pallas_call(
        body, name="grad_add_cores", grid_spec=grid_spec,
        out_shape=(jax.ShapeDtypeStruct((4, m, n), F32), jax.ShapeDtypeStruct((4, m, n), BF16)),
        compiler_params=_params(("arbitrary", "arbitrary")),
    )(c_idx, blocks, recv)


def _add_chips(own, recv):
    m, n = own.shape
    tr = 8
    while m % (tr * 2) == 0 and tr * 2 * n * 4 <= (1 << 20):
        tr *= 2
    if m % tr:
        tr = m

    def body(a_ref, r_ref, o_ref):
        acc = a_ref[...]
        for k in range(3):
            acc = acc + r_ref[k].astype(F32)
        o_ref[...] = acc

    return pl.pallas_call(
        body, name="grad_add_chips", grid=(m // tr,),
        out_shape=jax.ShapeDtypeStruct((m, n), F32),
        in_specs=[pl.BlockSpec((tr, n), lambda i: (i, 0)), pl.BlockSpec((3, tr, n), lambda i: (0, i, 0))],
        out_specs=pl.BlockSpec((tr, n), lambda i: (i, 0)),
        compiler_params=_params(("arbitrary",)),
    )(own, recv)


def _row_spec(cols):
    return pl.BlockSpec((ROW_TILE, cols), lambda i: (i, 0))


def _mod_spec(tiles_per_seq):
    return pl.BlockSpec((1, 8, D_MODEL), lambda i: (i // tiles_per_seq, 0, 0))


def _table_spec(tiles_per_seq):
    return pl.BlockSpec((ROW_TILE, LANES), lambda i: (i % tiles_per_seq, 0))


def _fwd_in(x, mod, ln_g, ln_b, w_in, q_g, kv_g, w_uq, w_ukv, cos_t, sin_a, sin_b, seq):
    rows = x.shape[0]
    tm = ROW_TILE
    tps = seq // tm

    def body(x_ref, mod_ref, g_ref, b_ref, win_ref, qg_ref, kvg_ref, wuq_ref, wukv_ref, cos_ref, sa_ref, sb_ref,
             x0_ref, h_ref, qkv_ref, lat_ref, qp_ref, kp_ref, vm_ref):
        x0, _, _ = _ln_fwd(x_ref[...], g_ref[...], b_ref[...])
        x0_ref[...] = x0
        h = (x0 * (1.0 + mod_ref[0, 1:2, :]) + mod_ref[0, 0:1, :]).astype(BF16)
        h_ref[...] = h
        proj = _dot(h, win_ref[...])
        qkv_ref[:, :SB_W] = (proj[:, :SB_W] * SB_SCALE).astype(BF16)
        qkv_ref[:, SB_W:] = proj[:, SB_W:3 * SB_W].astype(BF16)
        lat_ref[...] = proj[:, 3 * SB_W:3 * SB_W + Q_RANK + KV_RANK]
        cq = proj[:, 3 * SB_W:3 * SB_W + Q_RANK]
        ckv = proj[:, 3 * SB_W + Q_RANK:3 * SB_W + Q_RANK + KV_RANK]
        kr = proj[:, D_IN_PAD - LANES:]
        cos, sa, sb = cos_ref[...], sa_ref[...], sb_ref[...]
        cqn = (cq * lax.rsqrt(_mean(cq * cq) + RMS_EPS) * qg_ref[...]).astype(BF16)
        q_all = _dot(cqn, wuq_ref[...])
        for hd in range(HEADS):
            sl = slice(hd * LANES, (hd + 1) * LANES)
            qp_ref[:, sl] = _rope(q_all[:, sl], cos, sa, sb).astype(BF16)
        ckvn = (ckv * lax.rsqrt(_mean(ckv * ckv) + RMS_EPS) * kvg_ref[...]).astype(BF16)
        kv = _dot(ckvn, wukv_ref[...])
        kr_rot = _rope(kr, cos, sa, sb)
        for hd in range(HEADS):
            sl = slice(hd * LANES, (hd + 1) * LANES)
            kp_ref[:, sl] = (kv[:, sl] + kr_rot).astype(BF16)
        vm_ref[...] = kv[:, HEADS * LANES:].astype(BF16)

    outs = [(D_MODEL, F32), (D_MODEL, BF16), (3 * SB_W, BF16), (Q_RANK + KV_RANK, F32),
            (HEADS * LANES, BF16), (HEADS * LANES, BF16), (MLA_W, BF16)]
    return pl.pallas_call(
        body, name="fwd_in", grid=(rows // tm,),
        out_shape=tuple(jax.ShapeDtypeStruct((rows, n), dt) for n, dt in outs),
        in_specs=[_row_spec(D_MODEL), _mod_spec(tps), _const_spec((1, D_MODEL)), _const_spec((1, D_MODEL)),
                  _const_spec(w_in.shape), _const_spec((1, Q_RANK)), _const_spec((1, KV_RANK)),
                  _const_spec(w_uq.shape), _const_spec(w_ukv.shape),
                  _table_spec(tps), _table_spec(tps), _table_spec(tps)],
        out_specs=tuple(_row_spec(n) for n, _ in outs),
        compiler_params=_params(("arbitrary",)),
    )(x, mod, ln_g, ln_b, w_in, q_g, kv_g, w_uq, w_ukv, cos_t, sin_a, sin_b)


def _fwd_mix(sb_y, mla_y, x0, mod, w_o, ln_g, ln_b, seq):
    rows = x0.shape[0]
    tm = ROW_TILE
    tps = seq // tm

    def body(sb_ref, ml_ref, x0_ref, mod_ref, wo_ref, g_ref, b_ref, mix_ref, y1_ref, x1_ref, h2_ref):
        mix = _dot(sb_ref[...].astype(BF16), wo_ref[:SB_W, :]) + _dot(ml_ref[...].astype(BF16), wo_ref[SB_W:, :])
        mix_ref[...] = mix
        y1 = ALPHA * x0_ref[...] + (1.0 + mod_ref[0, 2:3, :]) * mix
        y1_ref[...] = y1
        x1, _, _ = _ln_fwd(y1, g_ref[...], b_ref[...])
        x1_ref[...] = x1
        h2_ref[...] = (x1 * (1.0 + mod_ref[0, 4:5, :]) + mod_ref[0, 3:4, :]).astype(BF16)

    outs = [(D_MODEL, F32), (D_MODEL, F32), (D_MODEL, F32), (D_MODEL, BF16)]
    return pl.pallas_call(
        body, name="fwd_mix", grid=(rows // tm,),
        out_shape=tuple(jax.ShapeDtypeStruct((rows, n), dt) for n, dt in outs),
        in_specs=[_row_spec(SB_W), _row_spec(MLA_W), _row_spec(D_MODEL), _mod_spec(tps), _const_spec(w_o.shape),
                  _const_spec((1, D_MODEL)), _const_spec((1, D_MODEL))],
        out_specs=tuple(_row_spec(n) for n, _ in outs),
        compiler_params=_params(("arbitrary",)),
    )(sb_y, mla_y, x0, mod, w_o, ln_g, ln_b)


def _fwd_mlp(h2, x1, mod, w_up, w_down, seq):
    rows = x1.shape[0]
    tm = ROW_TILE
    tps = seq // tm

    def body(h2_ref, x1_ref, mod_ref, wu_ref, wd_ref, u_ref, ff_ref, y2_ref):
        u = _dot(h2_ref[...], wu_ref[...])
        u_ref[...] = u.astype(BF16)
        act = jnp.square(jnp.maximum(u, 0.0)).astype(BF16)
        ff = _dot(act, wd_ref[...])
        ff_ref[...] = ff
        y2_ref[...] = ALPHA * x1_ref[...] + (1.0 + mod_ref[0, 5:6, :]) * ff

    outs = [(D_FF, BF16), (D_MODEL, F32), (D_MODEL, F32)]
    return pl.pallas_call(
        body, name="fwd_mlp", grid=(rows // tm,),
        out_shape=tuple(jax.ShapeDtypeStruct((rows, n), dt) for n, dt in outs),
        in_specs=[_row_spec(D_MODEL), _row_spec(D_MODEL), _mod_spec(tps), _const_spec(w_up.shape),
                  _const_spec(w_down.shape)],
        out_specs=tuple(_row_spec(n) for n, _ in outs),
        compiler_params=_params(("arbitrary",)),
    )(h2, x1, mod, w_up, w_down)


def _acc_spec(rows=8, cols=D_MODEL):
    return pl.BlockSpec((rows, cols), lambda i: (0, 0))


def _bwd_out(y2, tgt, ff, u, mod, ln_g, ln_b, w_down, seq):
    rows = y2.shape[0]
    nb = rows // seq
    tm = ROW_TILE
    tps = seq // tm

    def body(y2_ref, t_ref, ff_ref, u_ref, mod_ref, g_ref, b_ref, wd_ref, dy2_ref, dff_ref, du_ref, acc_ref, dmod_ref):
        i = pl.program_id(0)

        @pl.when(i == 0)
        def _():
            acc_ref[...] = jnp.zeros_like(acc_ref)

        @pl.when(i % tps == 0)
        def _():
            dmod_ref[...] = jnp.zeros_like(dmod_ref)

        g = g_ref[...]
        x2, xhat, rstd = _ln_fwd(y2_ref[...], g, b_ref[...])
        err = x2 - t_ref[...]
        dx2 = err * (1.0 / D_MODEL)
        acc_ref[0:1, :] += _rowsum(dx2 * xhat)
        acc_ref[1:2, :] += _rowsum(dx2)
        acc_ref[2:3, :] += _rowsum(err * err) * (0.5 / D_MODEL)
        dy2 = _ln_bwd(dx2, xhat, rstd, g)
        dy2_ref[...] = dy2
        dmod_ref[0, 5:6, :] += _rowsum(dy2 * ff_ref[...])
        dff = ((1.0 + mod_ref[0, 5:6, :]) * dy2).astype(BF16)
        dff_ref[...] = dff
        da = _dot(dff, wd_ref[...], NT)
        du_ref[...] = (da * (2.0 * jnp.maximum(u_ref[...].astype(F32), 0.0))).astype(BF16)

    outs = [(D_MODEL, F32), (D_MODEL, BF16), (D_FF, BF16)]
    return pl.pallas_call(
        body, name="bwd_out", grid=(rows // tm,),
        out_shape=tuple(jax.ShapeDtypeStruct((rows, n), dt) for n, dt in outs)
        + (jax.ShapeDtypeStruct((8, D_MODEL), F32), jax.ShapeDtypeStruct((nb, 8, D_MODEL), F32)),
        in_specs=[_row_spec(D_MODEL), _row_spec(D_MODEL), _row_spec(D_MODEL), _row_spec(D_FF), _mod_spec(tps),
                  _const_spec((1, D_MODEL)), _const_spec((1, D_MODEL)), _const_spec(w_down.shape)],
        out_specs=tuple(_row_spec(n) for n, _ in outs) + (_acc_spec(), _mod_spec(tps)),
        compiler_params=_params(("arbitrary",)),
    )(y2, tgt, ff, u, mod, ln_g, ln_b, w_down)


def _bwd_mid(du, y1, mix, dy2, mod, ln_g, ln_b, w_up, w_o, seq):
    rows = y1.shape[0]
    nb = rows // seq
    tm = ROW_TILE
    tps = seq // tm

    def body(du_ref, y1_ref, mix_ref, dy2_ref, mod_ref, g_ref, b_ref, wu_ref, wo_ref,
             dy1_ref, dmix_ref, do_ref, acc_ref, dmod_ref):
        i = pl.program_id(0)

        @pl.when(i == 0)
        def _():
            acc_ref[...] = jnp.zeros_like(acc_ref)

        @pl.when(i % tps == 0)
        def _():
            dmod_ref[...] = jnp.zeros_like(dmod_ref)

        g = g_ref[...]
        x1, xhat, rstd = _ln_fwd(y1_ref[...], g, b_ref[...])
        dh2 = _dot(du_ref[...], wu_ref[...], NT)
        dmod_ref[0, 3:4, :] += _rowsum(dh2)
        dmod_ref[0, 4:5, :] += _rowsum(dh2 * x1)
        dx1 = ALPHA * dy2_ref[...] + dh2 * (1.0 + mod_ref[0, 4:5, :])
        acc_ref[0:1, :] += _rowsum(dx1 * xhat)
        acc_ref[1:2, :] += _rowsum(dx1)
        dy1 = _ln_bwd(dx1, xhat, rstd, g)
        dy1_ref[...] = dy1
        dmod_ref[0, 2:3, :] += _rowsum(dy1 * mix_ref[...])
        dmix = ((1.0 + mod_ref[0, 2:3, :]) * dy1).astype(BF16)
        dmix_ref[...] = dmix
        do_ref[...] = _dot(dmix, wo_ref[...], NT)

    outs = [(D_MODEL, F32), (D_MODEL, BF16), (D_MODEL, F32)]
    return pl.pallas_call(
        body, name="bwd_mid", grid=(rows // tm,),
        out_shape=tuple(jax.ShapeDtypeStruct((rows, n), dt) for n, dt in outs)
        + (jax.ShapeDtypeStruct((8, D_MODEL), F32), jax.ShapeDtypeStruct((nb, 8, D_MODEL), F32)),
        in_specs=[_row_spec(D_FF), _row_spec(D_MODEL), _row_spec(D_MODEL), _row_spec(D_MODEL), _mod_spec(tps),
                  _const_spec((1, D_MODEL)), _const_spec((1, D_MODEL)), _const_spec(w_up.shape),
                  _const_spec(w_o.shape)],
        out_specs=tuple(_row_spec(n) for n, _ in outs) + (_acc_spec(), _mod_spec(tps)),
        compiler_params=_params(("arbitrary",)),
    )(du, y1, mix, dy2, mod, ln_g, ln_b, w_up, w_o)


def _bwd_in(dqp, dkp, dvm, dq_sb, dk_sb, dv_sb, lat, x, x0, dy1, mod, ln_g, ln_b, w_in, q_g, kv_g, w_uq, w_ukv,
            cos_t, sin_a, sin_b, seq):
    rows = x.shape[0]
    nb = rows // seq
    tm = ROW_TILE
    tps = seq // tm
    n_lat = Q_RANK + KV_RANK

    def body(dqp_ref, dkp_ref, dvm_ref, dqs_ref, dks_ref, dvs_ref, lat_ref, x_ref, x0_ref, dy1_ref, mod_ref,
             g_ref, b_ref, win_ref, qg_ref, kvg_ref, wuq_ref, wukv_ref, cos_ref, sa_ref, sb_ref,
             dx_ref, dproj_ref, dqall_ref, dkv_ref, latn_ref, acc_ref, accl_ref, dmod_ref):
        i = pl.program_id(0)

        @pl.when(i == 0)
        def _():
            acc_ref[...] = jnp.zeros_like(acc_ref)
            accl_ref[...] = jnp.zeros_like(accl_ref)

        @pl.when(i % tps == 0)
        def _():
            dmod_ref[...] = jnp.zeros_like(dmod_ref)

        cos, sa, sb = cos_ref[...], sa_ref[...], sb_ref[...]
        lane = lax.broadcasted_iota(jnp.int32, (tm, LANES), 1)
        for hd in range(HEADS):
            sl = slice(hd * LANES, (hd + 1) * LANES)
            dqall_ref[:, sl] = _rope_t(dqp_ref[:, sl], cos, sa, sb).astype(BF16)
        dcqn = _dot(dqall_ref[...], wuq_ref[...], NT)
        cq = lat_ref[:, :Q_RANK]
        qg = qg_ref[...]
        rq = lax.rsqrt(_mean(cq * cq) + RMS_EPS)
        cqn = cq * rq
        latn_ref[:, :Q_RANK] = (cqn * qg).astype(BF16)
        accl_ref[0:1, :Q_RANK] += _rowsum(dcqn * cqn)
        dqg = dcqn * qg
        dcq = rq * (dqg - cqn * _mean(dqg * cqn))
        dkr = jnp.zeros((tm, LANES), F32)
        for hd in range(HEADS):
            sl = slice(hd * LANES, (hd + 1) * LANES)
            dk = dkp_ref[:, sl]
            dkr = dkr + dk
            dkv_ref[:, sl] = jnp.where(lane < NOPE, dk, 0.0).astype(BF16)
        dkv_ref[:, HEADS * LANES:] = dvm_ref[...].astype(BF16)
        dckvn = _dot(dkv_ref[...], wukv_ref[...], NT)
        ckv = lat_ref[:, Q_RANK:]
        kvg = kvg_ref[...]
        rkv = lax.rsqrt(_mean(ckv * ckv) + RMS_EPS)
        ckvn = ckv * rkv
        latn_ref[:, Q_RANK:] = (ckvn * kvg).astype(BF16)
        accl_ref[1:2, :KV_RANK] += _rowsum(dckvn * ckvn)
        dkg = dckvn * kvg
        dckv = rkv * (dkg - ckvn * _mean(dkg * ckvn))
        dkr = _rope_t(jnp.where(lane >= NOPE, dkr, 0.0), cos, sa, sb)
        dproj_ref[:, :SB_W] = dqs_ref[...]
        dproj_ref[:, SB_W:2 * SB_W] = dks_ref[...].astype(BF16)
        dproj_ref[:, 2 * SB_W:3 * SB_W] = dvs_ref[...].astype(BF16)
        dproj_ref[:, 3 * SB_W:3 * SB_W + Q_RANK] = dcq.astype(BF16)
        dproj_ref[:, 3 * SB_W + Q_RANK:3 * SB_W + n_lat] = dckv.astype(BF16)
        dproj_ref[:, D_IN_PAD - LANES:] = dkr.astype(BF16)
        dh = _dot(dproj_ref[...], win_ref[...], NT)
        x0 = x0_ref[...]
        dmod_ref[0, 0:1, :] += _rowsum(dh)
        dmod_ref[0, 1:2, :] += _rowsum(dh * x0)
        dx0 = ALPHA * dy1_ref[...] + dh * (1.0 + mod_ref[0, 1:2, :])
        g = g_ref[...]
        _, xhat, rstd = _ln_fwd(x_ref[...], g, b_ref[...])
        acc_ref[0:1, :] += _rowsum(dx0 * xhat)
        acc_ref[1:2, :] += _rowsum(dx0)
        dx_ref[...] = _ln_bwd(dx0, xhat, rstd, g)

    outs = [(D_MODEL, F32), (D_IN_PAD, BF16), (HEADS * LANES, BF16), (HEADS * LANES + MLA_W, BF16), (n_lat, BF16)]
    return pl.pallas_call(
        body, name="bwd_in", grid=(rows // tm,),
        out_shape=tuple(jax.ShapeDtypeStruct((rows, n), dt) for n, dt in outs)
        + (jax.ShapeDtypeStruct((8, D_MODEL), F32), jax.ShapeDtypeStruct((8, Q_RANK), F32),
           jax.ShapeDtypeStruct((nb, 8, D_MODEL), F32)),
        in_specs=[_row_spec(HEADS * LANES), _row_spec(HEADS * LANES), _row_spec(MLA_W),
                  _row_spec(SB_W), _row_spec(SB_W), _row_spec(SB_W), _row_spec(n_lat),
                  _row_spec(D_MODEL), _row_spec(D_MODEL), _row_spec(D_MODEL), _mod_spec(tps),
                  _const_spec((1, D_MODEL)), _const_spec((1, D_MODEL)), _const_spec(w_in.shape),
                  _const_spec((1, Q_RANK)), _const_spec((1, KV_RANK)), _const_spec(w_uq.shape),
                  _const_spec(w_ukv.shape), _table_spec(tps), _table_spec(tps), _table_spec(tps)],
        out_specs=tuple(_row_spec(n) for n, _ in outs) + (_acc_spec(), _acc_spec(8, Q_RANK), _mod_spec(tps)),
        compiler_params=_params(("arbitrary",)),
    )(dqp, dkp, dvm, dq_sb, dk_sb, dv_sb, lat, x, x0, dy1, mod, ln_g, ln_b, w_in, q_g, kv_g, w_uq, w_ukv,
      cos_t, sin_a, sin_b)


def _wgrad(a, b, name, pre=None, tm=512, tn=1024, tk=512):
    rows, m = a.shape
    n = b.shape[1]
    tm, tn = min(tm, m), min(tn, n)
    if m % tm:
        tm = m
    if n % tn:
        tn = n

    def body(a_ref, b_ref, o_ref):
        @pl.when(pl.program_id(2) == 0)
        def _():
            o_ref[...] = jnp.zeros_like(o_ref)

        av = a_ref[...]
        if pre == "relu2":
            av = jnp.square(jnp.maximum(av.astype(F32), 0.0))
        o_ref[...] += _dot(av.astype(BF16), b_ref[...].astype(BF16), TN)

    return pl.pallas_call(
        body, name=name, grid=(m // tm, n // tn, rows // tk),
        out_shape=jax.ShapeDtypeStruct((m, n), F32),
        in_specs=[pl.BlockSpec((tk, tm), lambda i, j, k: (k, i)), pl.BlockSpec((tk, tn), lambda i, j, k: (k, j))],
        out_specs=pl.BlockSpec((tm, tn), lambda i, j, k: (i, j)),
        compiler_params=_params(("arbitrary", "arbitrary", "arbitrary")),
    )(a, b)


def _pair(pp):
    return slice(pp * LANES, (pp + 1) * LANES)


def _head_mask(lane, hh):
    return jnp.where((lane >= 64) if hh else (lane < 64), 1.0, 0.0).astype(BF16)


def _tri(t, kind):
    r = lax.broadcasted_iota(jnp.int32, (t, t), 0)
    c = lax.broadcasted_iota(jnp.int32, (t, t), 1)
    one = jnp.where(r > c if kind == "suffix" else r < c, 1.0, 0.0).astype(BF16)
    return jnp.concatenate([one, one], axis=0)


def _split_dot(v, tri2):
    hi = v.astype(BF16)
    lo = (v - hi.astype(F32)).astype(BF16)
    return _dot(jnp.concatenate([hi, lo], axis=1), tri2)


def _sb_logits(z, valid):
    log_keep = -(jnp.maximum(z, 0.0) + jnp.log(1.0 + jnp.exp(-jnp.abs(z))))
    log_beta = z + log_keep
    if valid is not None:
        log_keep = jnp.where(valid, log_keep, 0.0)
    return log_keep, log_beta


def _sb_fwd(qkv, seq):
    rows = qkv.shape[0]
    nb = rows // seq
    t = min(ATTN_TILE, seq)
    nq = seq // t

    ap = ATTN_PAIRS
    width = ap * LANES
    groups = SB_W // width
    hds = [(pp, hh) for pp in range(ap) for hh in range(2)]

    def body(q_ref, k_ref, v_ref, tri_ref, o_ref, car_ref):
        i = pl.program_id(2)
        lane = lax.broadcasted_iota(jnp.int32, (t, LANES), 1)
        row = lax.broadcasted_iota(jnp.int32, (t, t), 0)
        col = lax.broadcasted_iota(jnp.int32, (t, t), 1)
        strict = col < row
        tri = tri_ref[...]
        masks = [_head_mask(lane, hh) for hh in range(2)]
        qms = [q_ref[:, _pair(pp)] * masks[hh] for pp, hh in hds]

        def step(kb, carry, valid):
            c_sums, accs, cars = carry
            accs, cars = list(accs), list(cars)
            start = pl.multiple_of(kb * t, t)
            kss = [k_ref[pl.ds(start, t), _pair(pp)] for pp in range(ap)]
            vss = [v_ref[pl.ds(start, t), _pair(pp)] for pp in range(ap)]
            zs = [_dot(qms[n], kss[pp], NT) for n, (pp, _) in enumerate(hds)]
            logs = [_sb_logits(z, valid) for z in zs]
            sufs = [_split_dot(lg[0], tri) for lg in logs]
            new_sums = []
            for n, (pp, hh) in enumerate(hds):
                log_keep, log_beta = logs[n]
                w = jnp.exp(log_beta + sufs[n] + c_sums[n])
                if valid is not None:
                    w = jnp.where(valid, w, 0.0)
                accs[pp] = accs[pp] + _dot(w.astype(BF16), vss[pp] * masks[hh])
                cars[pp] = jnp.where(lane == hh * 8 + kb, c_sums[n], cars[pp])
                new_sums.append(c_sums[n] + jnp.sum(log_keep, axis=1, keepdims=True))
            return tuple(new_sums), tuple(accs), tuple(cars)

        zeros = tuple(jnp.zeros((t, LANES), F32) for _ in range(ap))
        carry = step(i, (tuple(jnp.zeros((t, 1), F32) for _ in hds), zeros, zeros), strict)
        _, accs, cars = lax.fori_loop(0, i, lambda j, cr: step(i - 1 - j, cr, None), carry)
        for pp in range(ap):
            o_ref[:, _pair(pp)] = accs[pp]
            car_ref[:, _pair(pp)] = cars[pp]

    qspec = pl.BlockSpec((t, width), lambda b, p, i: (b * nq + i, p))
    return pl.pallas_call(
        body, name="sb_fwd", grid=(nb, groups, nq),
        out_shape=(jax.ShapeDtypeStruct((rows, SB_W), F32), jax.ShapeDtypeStruct((rows, SB_W), F32)),
        in_specs=[qspec,
                  pl.BlockSpec((seq, width), lambda b, p, i: (b, groups + p)),
                  pl.BlockSpec((seq, width), lambda b, p, i: (b, 2 * groups + p)),
                  _const_spec((2 * t, t))],
        out_specs=(qspec, qspec),
        compiler_params=_params(("arbitrary", "arbitrary", "arbitrary")),
    )(qkv, qkv, qkv, _tri(t, "suffix"))


def _sb_bwd(qkv, d_out, cars, seq):
    rows = qkv.shape[0]
    nb = rows // seq
    t = min(ATTN_TILE, seq)
    nq = seq // t
    ap = ATTN_PAIRS
    width = ap * LANES
    groups = SB_W // width
    hds = [(pp, hh) for pp in range(ap) for hh in range(2)]

    def body(q_ref, k_ref, v_ref, do_ref, car_ref, tri_ref, pre_ref, dq_ref, dk_ref, dv_ref):
        i = pl.program_id(2)

        @pl.when(i == 0)
        def _():
            dk_ref[...] = jnp.zeros_like(dk_ref)
            dv_ref[...] = jnp.zeros_like(dv_ref)

        lane = lax.broadcasted_iota(jnp.int32, (t, LANES), 1)
        row = lax.broadcasted_iota(jnp.int32, (t, t), 0)
        col = lax.broadcasted_iota(jnp.int32, (t, t), 1)
        strict = col < row
        tri, pre = tri_ref[...], pre_ref[...]
        masks = [_head_mask(lane, hh) for hh in range(2)]
        qms = [q_ref[:, _pair(pp)] * masks[hh] for pp, hh in hds]
        doms = [do_ref[:, _pair(pp)].astype(BF16) * masks[hh] for pp, hh in hds]
        cars = [car_ref[:, _pair(pp)] for pp in range(ap)]

        def step(kb, carry, valid):
            g_pres, dqs = carry
            dqs = list(dqs)
            start = pl.multiple_of(kb * t, t)
            kss = [k_ref[pl.ds(start, t), _pair(pp)] for pp in range(ap)]
            vss = [v_ref[pl.ds(start, t), _pair(pp)] for pp in range(ap)]
            zs = [_dot(qms[n], kss[pp], NT) for n, (pp, _) in enumerate(hds)]
            dws = [_dot(doms[n], vss[pp], NT) for n, (pp, _) in enumerate(hds)]
            logs = [_sb_logits(z, valid) for z in zs]
            sufs = [_split_dot(lg[0], tri) for lg in logs]
            ws, gs = [], []
            for n, (pp, hh) in enumerate(hds):
                c_sum = jnp.sum(jnp.where(lane == hh * 8 + kb, cars[pp], 0.0), axis=1, keepdims=True)
                w = jnp.exp(logs[n][1] + sufs[n] + c_sum)
                if valid is not None:
                    w = jnp.where(valid, w, 0.0)
                ws.append(w)
                gs.append(dws[n] * w)
            befores = [g_pres[n] + _split_dot(gs[n], pre) for n in range(len(hds))]
            for pp in range(ap):
                dv_ref[pl.ds(start, t), _pair(pp)] += (_dot(ws[2 * pp].astype(BF16), doms[2 * pp], TN)
                                                       + _dot(ws[2 * pp + 1].astype(BF16), doms[2 * pp + 1], TN))
            dzbs = []
            for n in range(len(hds)):
                dz = gs[n] * jnp.exp(logs[n][0]) - jnp.exp(logs[n][1]) * befores[n]
                if valid is not None:
                    dz = jnp.where(valid, dz, 0.0)
                dzbs.append(dz.astype(BF16))
            for pp in range(ap):
                a, b = 2 * pp, 2 * pp + 1
                dqs[pp] = dqs[pp] + _dot(dzbs[a], kss[pp] * masks[0]) + _dot(dzbs[b], kss[pp] * masks[1])
                dk_ref[pl.ds(start, t), _pair(pp)] += _dot(dzbs[a], qms[a], TN) + _dot(dzbs[b], qms[b], TN)
            new_pres = [g_pres[n] + jnp.sum(gs[n], axis=1, keepdims=True) for n in range(len(hds))]
            return tuple(new_pres), tuple(dqs)

        init = (tuple(jnp.zeros((t, 1), F32) for _ in hds), tuple(jnp.zeros((t, LANES), F32) for _ in range(ap)))
        carry = lax.fori_loop(0, i, lambda kb, cr: step(kb, cr, None), init)
        _, dqs = step(i, carry, strict)
        for pp in range(ap):
            dq_ref[:, _pair(pp)] = (dqs[pp] * SB_SCALE).astype(BF16)

    qspec = pl.BlockSpec((t, width), lambda b, p, i: (b * nq + i, p))
    kspec_out = pl.BlockSpec((seq, width), lambda b, p, i: (b, p))
    return pl.pallas_call(
        body, name="sb_bwd", grid=(nb, groups, nq),
        out_shape=(jax.ShapeDtypeStruct((rows, SB_W), BF16), jax.ShapeDtypeStruct((rows, SB_W), F32),
                   jax.ShapeDtypeStruct((rows, SB_W), F32)),
        in_specs=[qspec,
                  pl.BlockSpec((seq, width), lambda b, p, i: (b, groups + p)),
                  pl.BlockSpec((seq, width), lambda b, p, i: (b, 2 * groups + p)),
                  qspec, qspec, _const_spec((2 * t, t)), _const_spec((2 * t, t))],
        out_specs=(qspec, kspec_out, kspec_out),
        compiler_params=_params(("arbitrary", "arbitrary", "arbitrary")),
    )(qkv, qkv, qkv, d_out, cars, _tri(t, "suffix"), _tri(t, "prefix"))


def _mla_scores(qh, ks, allowed):
    s = _dot(qh, ks, NT) * MLA_SCALE
    if allowed is not None:
        s = jnp.where(allowed, s, jnp.finfo(F32).min)
    return s


def _mla_fwd(qp, kp, vm, seq, chunk=64):
    rows = qp.shape[0]
    nb = rows // seq
    t = min(ATTN_TILE, seq)
    nq = seq // t
    shift = int(math.log2(chunk))
    ap = ATTN_PAIRS
    width = ap * LANES
    groups = MLA_W // width
    hds = [(pp, hh) for pp in range(ap) for hh in range(2)]

    def body(q_ref, k_ref, v_ref, o_ref, lse_ref):
        i = pl.program_id(2)
        lane = lax.broadcasted_iota(jnp.int32, (t, LANES), 1)
        row = lax.broadcasted_iota(jnp.int32, (t, t), 0)
        col = lax.broadcasted_iota(jnp.int32, (t, t), 1)
        allowed_diag = jnp.right_shift(col, shift) <= jnp.right_shift(row, shift)
        masks = [_head_mask(lane, hh) for hh in range(2)]
        qhs = [q_ref[:, _pair(n)] for n in range(len(hds))]

        def step(kb, carry, allowed):
            start = pl.multiple_of(kb * t, t)
            vss = [v_ref[pl.ds(start, t), _pair(pp)] for pp in range(ap)]
            scores = [_mla_scores(qhs[n], k_ref[pl.ds(start, t), _pair(n)], allowed) for n in range(len(hds))]
            new = []
            for n, (pp, hh) in enumerate(hds):
                m_run, l_run, acc = carry[n]
                s = scores[n]
                m_new = jnp.maximum(m_run, jnp.max(s, axis=1, keepdims=True))
                p = jnp.exp(s - m_new)
                scale = jnp.exp(m_run - m_new)
                l_run = scale * l_run + jnp.sum(p, axis=1, keepdims=True)
                acc = scale * acc + _dot(p.astype(BF16), vss[pp] * masks[hh])
                new.append((m_new, l_run, acc))
            return tuple(new)

        init = (jnp.full((t, 1), jnp.finfo(F32).min, F32), jnp.zeros((t, 1), F32), jnp.zeros((t, LANES), F32))
        carry = step(i, tuple(init for _ in hds), allowed_diag)
        carry = lax.fori_loop(0, i, lambda kb, cr: step(kb, cr, None), carry)
        for pp in range(ap):
            out = jnp.zeros((t, LANES), F32)
            lses = jnp.zeros((t, LANES), F32)
            for hh in range(2):
                m_run, l_run, acc = carry[2 * pp + hh]
                out = out + acc / l_run
                lses = jnp.where(lane == hh, m_run + jnp.log(l_run), lses)
            o_ref[:, _pair(pp)] = out
            lse_ref[:, _pair(pp)] = lses

    ospec = pl.BlockSpec((t, width), lambda b, p, i: (b * nq + i, p))
    return pl.pallas_call(
        body, name="mla_fwd", grid=(nb, groups, nq),
        out_shape=(jax.ShapeDtypeStruct((rows, MLA_W), F32), jax.ShapeDtypeStruct((rows, MLA_W), F32)),
        in_specs=[pl.BlockSpec((t, 2 * width), lambda b, p, i: (b * nq + i, p)),
                  pl.BlockSpec((seq, 2 * width), lambda b, p, i: (b, p)),
                  pl.BlockSpec((seq, width), lambda b, p, i: (b, p))],
        out_specs=(ospec, ospec),
        compiler_params=_params(("arbitrary", "arbitrary", "arbitrary")),
    )(qp, kp, vm)


def _mla_bwd(qp, kp, vm, d_out, out, lse, seq, chunk=64):
    rows = qp.shape[0]
    nb = rows // seq
    t = min(ATTN_TILE, seq)
    nq = seq // t
    shift = int(math.log2(chunk))
    ap = ATTN_PAIRS
    width = ap * LANES
    groups = MLA_W // width
    hds = [(pp, hh) for pp in range(ap) for hh in range(2)]
    nh = len(hds)

    def body(q_ref, k_ref, v_ref, do_ref, o_ref, lse_ref, dq_ref, dk_ref, dv_ref):
        i = pl.program_id(2)

        @pl.when(i == 0)
        def _():
            dk_ref[...] = jnp.zeros_like(dk_ref)
            dv_ref[...] = jnp.zeros_like(dv_ref)

        lane = lax.broadcasted_iota(jnp.int32, (t, LANES), 1)
        row = lax.broadcasted_iota(jnp.int32, (t, t), 0)
        col = lax.broadcasted_iota(jnp.int32, (t, t), 1)
        allowed_diag = jnp.right_shift(col, shift) <= jnp.right_shift(row, shift)
        qhs = [q_ref[:, _pair(n)] for n in range(nh)]
        doms, deltas, lse_hs = [], [], []
        for pp, hh in hds:
            do = do_ref[:, _pair(pp)]
            d_o = do * o_ref[:, _pair(pp)]
            doms.append(do.astype(BF16) * _head_mask(lane, hh))
            deltas.append(jnp.sum(jnp.where((lane >= 64) if hh else (lane < 64), d_o, 0.0), axis=1, keepdims=True))
            lse_hs.append(jnp.sum(jnp.where(lane == hh, lse_ref[:, _pair(pp)], 0.0), axis=1, keepdims=True))

        def step(kb, dqs, allowed):
            start = pl.multiple_of(kb * t, t)
            vss = [v_ref[pl.ds(start, t), _pair(pp)] for pp in range(ap)]
            kss = [k_ref[pl.ds(start, t), _pair(n)] for n in range(nh)]
            scores = [_mla_scores(qhs[n], kss[n], allowed) for n in range(nh)]
            dps = [_dot(doms[n], vss[pp], NT) for n, (pp, _) in enumerate(hds)]
            ps = [jnp.exp(scores[n] - lse_hs[n]) for n in range(nh)]
            dss = [(ps[n] * (dps[n] - deltas[n]) * MLA_SCALE).astype(BF16) for n in range(nh)]
            for pp in range(ap):
                a, b = 2 * pp, 2 * pp + 1
                dv_ref[pl.ds(start, t), _pair(pp)] += (_dot(ps[a].astype(BF16), doms[a], TN)
                                                       + _dot(ps[b].astype(BF16), doms[b], TN))
            for n in range(nh):
                dk_ref[pl.ds(start, t), _pair(n)] += _dot(dss[n], qhs[n], TN)
            return tuple(dqs[n] + _dot(dss[n], kss[n]) for n in range(nh))

        dqs = lax.fori_loop(0, i, lambda kb, cr: step(kb, cr, None),
                            tuple(jnp.zeros((t, LANES), F32) for _ in range(nh)))
        dqs = step(i, dqs, allowed_diag)
        for n in range(nh):
            dq_ref[:, _pair(n)] = dqs[n]

    ospec = pl.BlockSpec((t, width), lambda b, p, i: (b * nq + i, p))
    return pl.pallas_call(
        body, name="mla_bwd", grid=(nb, groups, nq),
        out_shape=(jax.ShapeDtypeStruct((rows, HEADS * LANES), F32), jax.ShapeDtypeStruct((rows, HEADS * LANES), F32),
                   jax.ShapeDtypeStruct((rows, MLA_W), F32)),
        in_specs=[pl.BlockSpec((t, 2 * width), lambda b, p, i: (b * nq + i, p)),
                  pl.BlockSpec((seq, 2 * width), lambda b, p, i: (b, p)),
                  pl.BlockSpec((seq, width), lambda b, p, i: (b, p)),
                  pl.BlockSpec((t, width), lambda b, p, i: (b * nq + i, groups + p)),
                  ospec, ospec],
        out_specs=(pl.BlockSpec((t, 2 * width), lambda b, p, i: (b * nq + i, p)),
                   pl.BlockSpec((seq, 2 * width), lambda b, p, i: (b, p)),
                   pl.BlockSpec((seq, width), lambda b, p, i: (b, p))),
        compiler_params=_params(("arbitrary", "arbitrary", "arbitrary")),
    )(qp, kp, vm, d_out, out, lse)


PACK_COLS = 1024
PACK_ROWS = 1536
PACK_ALIGN = 16
BIG = (("w_in", 1024, 552, 1), ("w_uq", 384, 192, 1), ("w_ukv", 256, 256, 1),
       ("w_o", 256, 1024, 0), ("w_up", 1024, 1024, 1), ("w_down", 1024, 1024, 0))


def _pack_rows(r, c):
    return (r // 2) * c // PACK_COLS


def _slot_rows(r, c):
    return -(-_pack_rows(r, c) // PACK_ALIGN) * PACK_ALIGN


def _join_slots(parts):
    padded = [jnp.pad(p, ((0, 0), (0, _slot_rows(r, c) - p.shape[1]), (0, 0))) for p, (_, r, c, _) in zip(parts, BIG)]
    used = sum(_slot_rows(r, c) for _, r, c, _ in BIG)
    padded.append(jnp.zeros((parts[0].shape[0], PACK_ROWS - used, PACK_COLS), parts[0].dtype))
    return jnp.concatenate(padded, axis=1)


def _split_slots(packed):
    out, at = [], 0
    for _, r, c, _ in BIG:
        out.append(packed[:, at:at + _pack_rows(r, c), :])
        at += _slot_rows(r, c)
    return out


def _pack_halves(shards):
    return _join_slots([s.reshape(2, _pack_rows(r, c), PACK_COLS) for s, (_, r, c, _) in zip(shards, BIG)])


def _unpack_halves(packed):
    return [p.reshape(r, c) for p, (_, r, c, _) in zip(_split_slots(packed), BIG)]


def _unpack_full(gathered):
    out = []
    for p, (_, r, c, axis) in zip(_split_slots(gathered), BIG):
        shards = p.reshape(4, r, c)
        out.append(shards.reshape(4 * r, c) if axis == 0 else jnp.moveaxis(shards, 0, 1).reshape(r, 4 * c))
    return out


def _pack_full(grads):
    parts = []
    for gr, (_, r, c, axis) in zip(grads, BIG):
        shards = gr.reshape(4, r, c) if axis == 0 else jnp.moveaxis(gr.reshape(r, 4, c), 1, 0)
        parts.append(shards.reshape(8, _pack_rows(r, c), PACK_COLS))
    return _join_slots(parts)


def _pad_w_in(w_in):
    z = jnp.zeros((D_MODEL, 1), w_in.dtype)
    return jnp.concatenate([w_in[:, :2176], jnp.tile(z, (1, 64)), w_in[:, 2176:], jnp.tile(z, (1, 32))], axis=1)


def _unpad_w_in(g):
    return jnp.concatenate([g[:, :2176], g[:, 2240:2272]], axis=1)


def _pad_heads(w, used):
    k = w.shape[0]
    w3 = w.reshape(k, HEADS, used)
    return jnp.pad(w3, ((0, 0), (0, 0), (0, LANES - used))).reshape(k, HEADS * LANES)


def _unpad_heads(g, used):
    k = g.shape[0]
    return g.reshape(k, HEADS, LANES)[:, :, :used].reshape(k, HEADS * used)


def _rope_tables(seq):
    inv_freq = 1.0 / (ROPE_BASE ** (jnp.arange(0, ROPE, 2, dtype=F32) / ROPE))
    ang = jnp.arange(seq, dtype=F32)[:, None] * inv_freq[None, :]
    cos, sin = jnp.cos(ang), jnp.sin(ang)
    one, zero = jnp.ones((seq, NOPE), F32), jnp.zeros((seq, NOPE), F32)
    z16, z32 = jnp.zeros((seq, 16), F32), jnp.zeros((seq, 32), F32)
    cos_t = jnp.concatenate([one, cos, cos, jnp.ones((seq, 32), F32)], axis=1)
    sin_a = jnp.concatenate([zero, -sin, z16, z32], axis=1)
    sin_b = jnp.concatenate([zero, z16, sin, z32], axis=1)
    return cos_t, sin_a, sin_b


SMALL = (("ln_in_g", 1024), ("ln_in_b", 1024), ("b_ada", 6144), ("q_norm_g", 384), ("kv_norm_g", 256),
         ("ln1_g", 1024), ("ln1_b", 1024), ("ln2_g", 1024), ("ln2_b", 1024))
SMALL_TOTAL = sum(n for _, n in SMALL)
SMALL_ROWS = -(-SMALL_TOTAL // LANES // 8) * 8


def _pack_small(vals):
    flat = jnp.concatenate([v.reshape(-1) for v in vals])
    return jnp.pad(flat, (0, SMALL_ROWS * LANES - SMALL_TOTAL)).reshape(SMALL_ROWS, LANES)


def _unpack_small(packed, like):
    flat, out, at = packed.reshape(-1), [], 0
    for (_, n), ref in zip(SMALL, like):
        out.append(flat[at:at + n].reshape(ref.shape))
        at += n
    return out


def kernel(x, c, ln_in_g, ln_in_b, w_ada, b_ada, w_in, q_norm_g, kv_norm_g, w_uq, w_ukv, w_o, ln1_g, ln1_b, w_up, w_down, ln2_g, ln2_b, loss_target, m_ln_in_g, m_ln_in_b, m_w_ada, m_b_ada, m_w_in, m_q_norm_g, m_kv_norm_g, m_w_uq, m_w_ukv, m_w_o, m_ln1_g, m_ln1_b, m_w_up, m_w_down, m_ln2_g, m_ln2_b, v_ln_in_g, v_ln_in_b, v_w_ada, v_b_ada, v_w_in, v_q_norm_g, v_kv_norm_g, v_w_uq, v_w_ukv, v_w_o, v_ln1_g, v_ln1_b, v_w_up, v_w_down, v_ln2_g, v_ln2_b):
    nb, seq, _ = x.shape
    rows = nb * seq
    ix, iy, ic = lax.axis_index("x"), lax.axis_index("y"), lax.axis_index("c")
    chip = 2 * ix + iy
    dev = 2 * chip + ic

    shards = [w_in[0], w_uq[0], w_ukv[0], w_o[0], w_up[0], w_down[0]]
    packed = _pack_halves([s.astype(BF16) for s in shards])
    my_half = lax.dynamic_index_in_dim(packed, ic, 0, keepdims=False)
    f_in, f_uq, f_ukv, f_o, f_up, f_down = _unpack_full(_gather8(my_half, "gather_weights"))
    w_in_p = _pad_w_in(f_in)
    uq3 = f_uq.reshape(Q_RANK, HEADS, NOPE + ROPE)
    w_uq_p = jnp.pad(uq3, ((0, 0), (0, 0), (0, LANES - NOPE - ROPE))).reshape(Q_RANK, HEADS * LANES)
    w_ukv_p = jnp.concatenate([_pad_heads(f_ukv[:, :HEADS * NOPE], NOPE), f_ukv[:, HEADS * NOPE:]], axis=1)

    n_all = 8 * nb
    c_all = _gather8(c.reshape(-1, LANES), "gather_c").reshape(n_all, D_MODEL)
    ada_cols = w_ada.shape[2]
    b_sh = lax.dynamic_slice_in_dim(b_ada, chip * ada_cols, ada_cols, axis=1)
    mod_sh = _ada_fwd(c_all, w_ada[0], b_sh)
    mod_g = _gather8(mod_sh, "gather_mod")[0::2]
    mod_all = jnp.moveaxis(mod_g, 0, 1).reshape(n_all, N_MOD * D_MODEL)
    mod_mine = lax.dynamic_slice_in_dim(mod_all, dev * nb, nb, axis=0).reshape(nb, N_MOD, D_MODEL)
    mod = jnp.pad(mod_mine, ((0, 0), (0, 8 - N_MOD), (0, 0)))

    cos_t, sin_a, sin_b = _rope_tables(seq)
    row2 = lambda v: v.reshape(1, -1)

    x2d = x.reshape(rows, D_MODEL)
    x0, h, qkv, lat, qp, kp, vm = _fwd_in(x2d, mod, row2(ln_in_g), row2(ln_in_b), w_in_p, q_norm_g, kv_norm_g,
                                          w_uq_p, w_ukv_p, cos_t, sin_a, sin_b, seq)
    sb_y, cars = _sb_fwd(qkv, seq)
    mla_y, lse = _mla_fwd(qp, kp, vm, seq)
    mix, y1, x1, h2 = _fwd_mix(sb_y, mla_y, x0, mod, f_o, ln1_g, ln1_b, seq)
    u, ff, y2 = _fwd_mlp(h2, x1, mod, f_up, f_down, seq)

    dy2, dff, du, acc2, dmod_a = _bwd_out(y2, loss_target.reshape(rows, D_MODEL), ff, u, mod, ln2_g, ln2_b, f_down, seq)
    dy1, dmix, d_attn, acc1, dmod_b = _bwd_mid(du, y1, mix, dy2, mod, ln1_g, ln1_b, f_up, f_o, seq)
    dq_sb, dk_sb, dv_sb = _sb_bwd(qkv, d_attn, cars, seq)
    dqp, dkp, dvm = _mla_bwd(qp, kp, vm, d_attn, mla_y, lse, seq)
    grad_x, dproj, dqall, dkv, latn, acc0, acc_lat, dmod_c = _bwd_in(
        dqp, dkp, dvm, dq_sb, dk_sb, dv_sb, lat, x2d, x0, dy1, mod, row2(ln_in_g), row2(ln_in_b), w_in_p,
        q_norm_g, kv_norm_g, w_uq_p, w_ukv_p, cos_t, sin_a, sin_b, seq)

    g_down = _wgrad(u, dff, "wgrad_down", pre="relu2")
    g_up = _wgrad(h2, du, "wgrad_up")
    g_o = jnp.concatenate([_wgrad(sb_y, dmix, "wgrad_o_sb"), _wgrad(mla_y, dmix, "wgrad_o_mla")], axis=0)
    g_in = _unpad_w_in(_wgrad(h, dproj, "wgrad_in", tn=768))
    g_uq = _unpad_heads(_wgrad(latn[:, :Q_RANK], dqall, "wgrad_uq"), NOPE + ROPE)
    g_ukv_p = _wgrad(latn[:, Q_RANK:], dkv, "wgrad_ukv", tn=512)
    g_ukv = jnp.concatenate([_unpad_heads(g_ukv_p[:, :HEADS * LANES], NOPE), g_ukv_p[:, HEADS * LANES:]], axis=1)

    blocks = _pack_full([g_in, g_uq, g_ukv, g_o, g_up, g_down])
    from_sibling = _swap_cores(blocks, "grads_to_sibling")
    part, part_bf = _add_pairs(blocks, from_sibling, ic.reshape(1).astype(jnp.int32))
    from_chips = _scatter_chips(part_bf, "grads_to_chips")
    own = lax.dynamic_index_in_dim(part, chip, 0, keepdims=False)
    half = _add_chips(own, from_chips)
    other = _swap_one(half, "grads_halves")
    both = jnp.where(ic == 0, jnp.stack([half, other]), jnp.stack([other, half]))
    gs_in, gs_uq, gs_ukv, gs_o, gs_up, gs_down = _unpack_halves(both)

    dmod = (dmod_a + dmod_b + dmod_c)[:, :N_MOD, :]
    small_part = _pack_small([acc0[0], acc0[1], jnp.zeros((N_MOD * D_MODEL,), F32), acc_lat[0, :Q_RANK],
                              acc_lat[1, :KV_RANK], acc1[0], acc1[1], acc2[0], acc2[1]])
    payload = jnp.concatenate([small_part, dmod.reshape(-1, LANES)], axis=0)
    gathered = _gather8(payload, "gather_small")
    small_sum = _sum_lead(gathered[:, :SMALL_ROWS, :], "sum_small")
    dmod_all = gathered[:, SMALL_ROWS:, :].reshape(n_all, N_MOD * D_MODEL)
    g_b_ada = _sum_lead(dmod_all.reshape(n_all, N_MOD * D_MODEL // LANES, LANES), "sum_b_ada").reshape(1, -1)
    dmod_sh = lax.dynamic_slice_in_dim(dmod_all, chip * ada_cols, ada_cols, axis=1)
    g_w_ada = _ada_bwd(c_all, dmod_sh)

    small_like = [ln_in_g, ln_in_b, b_ada, q_norm_g, kv_norm_g, ln1_g, ln1_b, ln2_g, ln2_b]
    small_grads = _unpack_small(small_sum, small_like)
    small_grads[2] = g_b_ada
    loss = lax.psum(jnp.sum(acc2[2]), ("x", "y", "c"))

    big_w = {"w_ada": (w_ada[0], g_w_ada, m_w_ada[0], v_w_ada[0]), "w_in": (w_in[0], gs_in, m_w_in[0], v_w_in[0]),
             "w_uq": (w_uq[0], gs_uq, m_w_uq[0], v_w_uq[0]), "w_ukv": (w_ukv[0], gs_ukv, m_w_ukv[0], v_w_ukv[0]),
             "w_o": (w_o[0], gs_o, m_w_o[0], v_w_o[0]), "w_up": (w_up[0], gs_up, m_w_up[0], v_w_up[0]),
             "w_down": (w_down[0], gs_down, m_w_down[0], v_w_down[0])}
    res = {}
    for name, (w, g, m, v) in big_w.items():
        d, mn, vn = _adamw(w, g, m, v, "adamw_" + name)
        res[name] = (g[None], d[None], mn[None], vn[None])
    small_m = [m_ln_in_g, m_ln_in_b, m_b_ada, m_q_norm_g, m_kv_norm_g, m_ln1_g, m_ln1_b, m_ln2_g, m_ln2_b]
    small_v = [v_ln_in_g, v_ln_in_b, v_b_ada, v_q_norm_g, v_kv_norm_g, v_ln1_g, v_ln1_b, v_ln2_g, v_ln2_b]
    sd, sm, sv = _adamw(_pack_small(small_like), _pack_small(small_grads), _pack_small(small_m), _pack_small(small_v),
                        "adamw_small")
    for (name, _), g, d, mn, vn in zip(SMALL, small_grads, _unpack_small(sd, small_like),
                                       _unpack_small(sm, small_like), _unpack_small(sv, small_like)):
        res[name] = (g, d, mn, vn)

    order = ["ln_in_g", "ln_in_b", "w_ada", "b_ada", "w_in", "q_norm_g", "kv_norm_g", "w_uq", "w_ukv", "w_o",
             "ln1_g", "ln1_b", "w_up", "w_down", "ln2_g", "ln2_b"]
    outs = [loss, grad_x.reshape(nb, seq, D_MODEL)]
    for k in range(4):
        outs += [res[name][k] for name in order]
    return tuple(outs)
```

```python
import functools
import math

import jax
import jax.numpy as jnp
from jax import lax
from jax.experimental import pallas as pl
from jax.experimental.pallas import tpu as pltpu

F32 = jnp.float32
BF16 = jnp.bfloat16
MESH_IDS = pl.DeviceIdType.MESH

D_MODEL = 1024
HEADS = 8
HEAD_PAIRS = HEADS // 2
SB_W = 512
MLA_W = 512
NOPE = 64
ROPE = 32
Q_RANK = 384
KV_RANK = 256
D_IN = 2208
D_IN_PAD = 2304
D_FF = 4096
N_MOD = 6
LN_EPS = 1e-5
RMS_EPS = 1e-6
ALPHA = 2.0 ** 0.25
ROPE_BASE = 10000.0
SB_SCALE = 64 ** -0.5
MLA_SCALE = 96 ** -0.5
ADAM_LR = 0.001
ADAM_B1 = 0.9
ADAM_B2 = 0.999
ADAM_EPS = 1e-08
ADAM_WD = 0.01
ADAM_STEP = 10

LANES = 128
ROW_TILE = 256
ATTN_TILE = 256
ATTN_PAIRS = 2
VMEM_LIMIT = 56 << 20

NT = (((1,), (1,)), ((), ()))
TN = (((0,), (0,)), ((), ()))


def _params(sem=None):
    return pltpu.CompilerParams(vmem_limit_bytes=VMEM_LIMIT, dimension_semantics=sem)


def _const_spec(shape):
    zeros = (0,) * len(shape)
    return pl.BlockSpec(shape, lambda *_: zeros, pipeline_mode=pl.Buffered(1))


def _dot(a, b, dims=None):
    if dims is None:
        return jnp.dot(a, b, preferred_element_type=F32)
    return lax.dot_general(a, b, dims, preferred_element_type=F32)


def _mean(v):
    return jnp.mean(v, axis=-1, keepdims=True)


def _rowsum(v):
    return jnp.sum(v, axis=0, keepdims=True)


def _ln_fwd(y, g, b):
    mu = _mean(y)
    yc = y - mu
    rstd = lax.rsqrt(_mean(yc * yc) + LN_EPS)
    xhat = yc * rstd
    return xhat * g + b, xhat, rstd


def _ln_bwd(dx, xhat, rstd, g):
    dxh = dx * g
    return rstd * (dxh - _mean(dxh) - xhat * _mean(dxh * xhat))


def _rope(v, cos, sin_a, sin_b):
    return v * cos + pltpu.roll(v, 112, 1) * sin_a + pltpu.roll(v, 16, 1) * sin_b


def _rope_t(dv, cos, sin_a, sin_b):
    return dv * cos + pltpu.roll(dv * sin_a, 16, 1) + pltpu.roll(dv * sin_b, 112, 1)


def _my_place():
    return lax.axis_index("x"), lax.axis_index("y"), lax.axis_index("c")


class _Exchange:
    def __init__(self, operand, out_shape, n_copies, phases):
        self.operand = operand
        self.out_shape = out_shape
        self.phases = phases
        self.scratch = [pltpu.SemaphoreType.DMA((n_copies,)), pltpu.SemaphoreType.DMA((n_copies,)),
                        pltpu.SemaphoreType.DMA]


def _run_exchange(ex, name):
    def body(in_ref, out_ref, send_sems, recv_sems, local_sem):
        for phase in ex.phases(in_ref, out_ref, send_sems, recv_sems, local_sem):
            phase()

    return pl.pallas_call(
        body, name=name, out_shape=ex.out_shape,
        in_specs=[pl.BlockSpec(memory_space=pl.ANY)], out_specs=pl.BlockSpec(memory_space=pl.ANY),
        scratch_shapes=ex.scratch,
    )(ex.operand)


def _nothing():
    pass


def _gather_exchange(v):
    m, n = v.shape

    def phases(v_ref, out_ref, send_sems, recv_sems, local_sem):
        x, y, c = _my_place()
        me, sibling = (x, y, c), (x, y, 1 - c)
        chips = [(1 - x, y), (x, 1 - y), (1 - x, 1 - y)]

        def rows(px, py, pc):
            return out_ref.at[4 * px + 2 * py + pc]

        def copy(k, block, to, src=None):
            return pltpu.make_async_remote_copy(
                src_ref=rows(*block) if src is None else src, dst_ref=rows(*block),
                send_sem=send_sems.at[k], recv_sem=recv_sems.at[k], device_id=to, device_id_type=MESH_IDS)

        mine = pltpu.make_async_copy(v_ref, rows(*me), local_sem)
        first = [copy(0, me, sibling, src=v_ref)]
        first += [copy(1 + j, me, (*chip, c), src=v_ref) for j, chip in enumerate(chips)]
        passed = [copy(4 + j, (*chip, c), sibling) for j, chip in enumerate(chips)]

        def start():
            mine.start()
            for cp in first:
                cp.start()

        def middle():
            for j, chip in enumerate(chips):
                copy(1 + j, (*chip, c), me).wait_recv()
                passed[j].start()

        def finish():
            copy(0, sibling, me).wait_recv()
            for j, chip in enumerate(chips):
                copy(4 + j, (*chip, 1 - c), me).wait_recv()
            for cp in first + passed:
                cp.wait_send()
            mine.wait()

        return start, middle, finish

    return _Exchange(v, jax.ShapeDtypeStruct((8, m, n), v.dtype), 7, phases)


def _direct_exchange(operand, out_shape, n_copies, make_copies):
    def phases(in_ref, out_ref, send_sems, recv_sems, local_sem):
        copies = make_copies(in_ref, out_ref, send_sems, recv_sems)

        def start():
            for cp in copies:
                cp.start()

        def finish():
            for cp in copies:
                cp.wait()

        return start, _nothing, finish

    return _Exchange(operand, out_shape, n_copies, phases)


def _swap_cores_exchange(blocks):
    _, m, n = blocks.shape

    def make_copies(g_ref, out_ref, send_sems, recv_sems):
        x, y, c = _my_place()
        return [pltpu.make_async_remote_copy(
            src_ref=g_ref.at[2 * j + (1 - c)], dst_ref=out_ref.at[j],
            send_sem=send_sems.at[j], recv_sem=recv_sems.at[j],
            device_id=(x, y, 1 - c), device_id_type=MESH_IDS) for j in range(4)]

    return _direct_exchange(blocks, jax.ShapeDtypeStruct((4, m, n), blocks.dtype), 4, make_copies)


def _scatter_chips_exchange(parts):
    _, m, n = parts.shape
    flips = [(1, 0), (0, 1), (1, 1)]

    def make_copies(p_ref, out_ref, send_sems, recv_sems):
        x, y, c = _my_place()
        copies = []
        for k, (fx, fy) in enumerate(flips):
            tx = 1 - x if fx else x
            ty = 1 - y if fy else y
            copies.append(pltpu.make_async_remote_copy(
                src_ref=p_ref.at[2 * tx + ty], dst_ref=out_ref.at[k],
                send_sem=send_sems.at[k], recv_sem=recv_sems.at[k],
                device_id=(tx, ty, c), device_id_type=MESH_IDS))
        return copies

    return _direct_exchange(parts, jax.ShapeDtypeStruct((3, m, n), parts.dtype), 3, make_copies)


def _swap_one_exchange(v):
    def make_copies(v_ref, out_ref, send_sems, recv_sems):
        x, y, c = _my_place()
        return [pltpu.make_async_remote_copy(src_ref=v_ref, dst_ref=out_ref, send_sem=send_sems.at[0],
                                             recv_sem=recv_sems.at[0], device_id=(x, y, 1 - c),
                                             device_id_type=MESH_IDS)]

    return _direct_exchange(v, jax.ShapeDtypeStruct(v.shape, v.dtype), 1, make_copies)


def _gather8(v, name):
    return _run_exchange(_gather_exchange(v), name)


def _carried(ex, refs, n_in, n_out):
    ins, ex_in = refs[:n_in], refs[n_in]
    outs, ex_out = refs[n_in + 1:n_in + 1 + n_out], refs[n_in + 1 + n_out]
    return ins, outs, ex.phases(ex_in, ex_out, *refs[n_in + 2 + n_out:])


def _ada_fwd(c_all, w_ada_sh, b_ada_sh):
    nb, cols = c_all.shape[0], w_ada_sh.shape[1]
    tn = 512

    def body(c_ref, w_ref, b_ref, o_ref):
        cv = c_ref[...]
        act = (cv * jax.nn.sigmoid(cv)).astype(BF16)
        o_ref[...] = _dot(act, w_ref[...].astype(BF16)) + b_ref[...]

    return pl.pallas_call(
        body, name="ada_fwd", grid=(cols // tn,),
        out_shape=jax.ShapeDtypeStruct((nb, cols), F32),
        in_specs=[pl.BlockSpec((nb, D_MODEL), lambda j: (0, 0)),
                  pl.BlockSpec((D_MODEL, tn), lambda j: (0, j)),
                  pl.BlockSpec((1, tn), lambda j: (0, j))],
        out_specs=pl.BlockSpec((nb, tn), lambda j: (0, j)),
        compiler_params=_params(("arbitrary",)),
    )(c_all, w_ada_sh, b_ada_sh)


def _ada_bwd(c_all, dmod_sh):
    nb, cols = dmod_sh.shape
    tn = 512

    def body(c_ref, d_ref, o_ref):
        cv = c_ref[...]
        act = (cv * jax.nn.sigmoid(cv)).astype(BF16)
        o_ref[...] = _dot(act, d_ref[...].astype(BF16), TN)

    return pl.pallas_call(
        body, name="ada_bwd", grid=(cols // tn,),
        out_shape=jax.ShapeDtypeStruct((D_MODEL, cols), F32),
        in_specs=[pl.BlockSpec((nb, D_MODEL), lambda j: (0, 0)),
                  pl.BlockSpec((nb, tn), lambda j: (0, j))],
        out_specs=pl.BlockSpec((D_MODEL, tn), lambda j: (0, j)),
        compiler_params=_params(("arbitrary",)),
    )(c_all, dmod_sh)


def _sum_lead(v, name):
    k, m, n = v.shape

    def body(v_ref, o_ref):
        acc = v_ref[0]
        for i in range(1, k):
            acc = acc + v_ref[i]
        o_ref[...] = acc

    return pl.pallas_call(
        body, name=name, out_shape=jax.ShapeDtypeStruct((m, n), F32),
        in_specs=[pl.BlockSpec((k, m, n), lambda: (0, 0, 0))],
        out_specs=pl.BlockSpec((m, n), lambda: (0, 0)),
        compiler_params=_params(),
    )(v)


def _adamw(w, g, m, v, name):
    rows, cols = w.shape
    tr = rows
    while tr * cols * 4 > (2 << 20) and tr % 16 == 0:
        tr //= 2

    def body(w_ref, g_ref, m_ref, v_ref, d_ref, mo_ref, vo_ref):
        gv = g_ref[...]
        mn = ADAM_B1 * m_ref[...] + (1.0 - ADAM_B1) * gv
        vn = ADAM_B2 * v_ref[...] + (1.0 - ADAM_B2) * (gv * gv)
        m_hat = mn / (1.0 - ADAM_B1 ** ADAM_STEP)
        v_hat = vn / (1.0 - ADAM_B2 ** ADAM_STEP)
        d_ref[...] = -ADAM_LR * (m_hat / (jnp.sqrt(v_hat) + ADAM_EPS) + ADAM_WD * w_ref[...])
        mo_ref[...] = mn
        vo_ref[...] = vn

    spec = pl.BlockSpec((tr, cols), lambda i: (i, 0))
    shape = jax.ShapeDtypeStruct((rows, cols), F32)
    return pl.pallas_call(
        body, name=name, grid=(rows // tr,), out_shape=(shape, shape, shape),
        in_specs=[spec, spec, spec, spec], out_specs=(spec, spec, spec),
        compiler_params=_params(("arbitrary",)),
    )(w, g, m, v)


def _add_rows(m, n):
    tr = 8
    while m % (tr * 2) == 0 and tr * 2 * n * 4 <= (1 << 20):
        tr *= 2
    assert m % tr == 0, (m, tr)
    return tr


def _add_pairs(blocks, recv, c_idx, name):
    _, m, n = blocks.shape
    tr = _add_rows(m, n)

    def body(c_ref, a_ref, b_ref, o_ref, ob_ref):
        s = a_ref[...] + b_ref[...]
        o_ref[...] = s
        ob_ref[...] = s.astype(BF16)

    grid_spec = pltpu.PrefetchScalarGridSpec(
        num_scalar_prefetch=1, grid=(4, m // tr),
        in_specs=[pl.BlockSpec((1, tr, n), lambda j, i, c: (2 * j + c[0], i, 0)),
                  pl.BlockSpec((1, tr, n), lambda j, i, c: (j, i, 0))],
        out_specs=(pl.BlockSpec((1, tr, n), lambda j, i, c: (j, i, 0)),
                   pl.BlockSpec((1, tr, n), lambda j, i, c: (j, i, 0))))
    return pl.pallas_call(
        body, name=name, grid_spec=grid_spec,
        out_shape=(jax.ShapeDtypeStruct((4, m, n), F32), jax.ShapeDtypeStruct((4, m, n), BF16)),
        compiler_params=_params(("arbitrary", "arbitrary")),
    )(c_idx, blocks, recv)


def _add_chips(own, recv, name):
    m, n = own.shape
    tr = _add_rows(m, n)

    def body(a_ref, r_ref, o_ref):
        acc = a_ref[...]
        for k in range(3):
            acc = acc + r_ref[k].astype(F32)
        o_ref[...] = acc

    return pl.pallas_call(
        body, name=name, grid=(m // tr,),
        out_shape=jax.ShapeDtypeStruct((m, n), F32),
        in_specs=[pl.BlockSpec((tr, n), lambda i: (i, 0)), pl.BlockSpec((3, tr, n), lambda i: (0, i, 0))],
        out_specs=pl.BlockSpec((tr, n), lambda i: (i, 0)),
        compiler_params=_params(("arbitrary",)),
    )(own, recv)


def _row_spec(cols):
    return pl.BlockSpec((ROW_TILE, cols), lambda i: (i, 0))


def _mod_spec(tiles_per_seq):
    return pl.BlockSpec((1, 8, D_MODEL), lambda i: (i // tiles_per_seq, 0, 0))


def _table_spec(tiles_per_seq):
    return pl.BlockSpec((ROW_TILE, LANES), lambda i: (i % tiles_per_seq, 0))


def _fwd_in(x, mod, ln_g, ln_b, w_in, q_g, kv_g, w_uq, w_ukv, cos_t, sin_a, sin_b, seq):
    rows = x.shape[0]
    tm = ROW_TILE
    tps = seq // tm

    def body(x_ref, mod_ref, g_ref, b_ref, win_ref, qg_ref, kvg_ref, wuq_ref, wukv_ref, cos_ref, sa_ref, sb_ref,
             x0_ref, h_ref, qkv_ref, lat_ref, qp_ref, kp_ref, vm_ref):
        x0, _, _ = _ln_fwd(x_ref[...], g_ref[...], b_ref[...])
        x0_ref[...] = x0
        h = (x0 * (1.0 + mod_ref[0, 1:2, :]) + mod_ref[0, 0:1, :]).astype(BF16)
        h_ref[...] = h
        proj = _dot(h, win_ref[...])
        qkv_ref[:, :SB_W] = (proj[:, :SB_W] * SB_SCALE).astype(BF16)
        qkv_ref[:, SB_W:] = proj[:, SB_W:3 * SB_W].astype(BF16)
        lat_ref[...] = proj[:, 3 * SB_W:3 * SB_W + Q_RANK + KV_RANK]
        cq = proj[:, 3 * SB_W:3 * SB_W + Q_RANK]
        ckv = proj[:, 3 * SB_W + Q_RANK:3 * SB_W + Q_RANK + KV_RANK]
        kr = proj[:, D_IN_PAD - LANES:]
        cos, sa, sb = cos_ref[...], sa_ref[...], sb_ref[...]
        cqn = (cq * lax.rsqrt(_mean(cq * cq) + RMS_EPS) * qg_ref[...]).astype(BF16)
        q_all = _dot(cqn, wuq_ref[...])
        for hd in range(HEADS):
            sl = slice(hd * LANES, (hd + 1) * LANES)
            qp_ref[:, sl] = _rope(q_all[:, sl], cos, sa, sb).astype(BF16)
        ckvn = (ckv * lax.rsqrt(_mean(ckv * ckv) + RMS_EPS) * kvg_ref[...]).astype(BF16)
        kv = _dot(ckvn, wukv_ref[...])
        kr_rot = _rope(kr, cos, sa, sb)
        for hd in range(HEADS):
            sl = slice(hd * LANES, (hd + 1) * LANES)
            kp_ref[:, sl] = (kv[:, sl] + kr_rot).astype(BF16)
        vm_ref[...] = kv[:, HEADS * LANES:].astype(BF16)

    outs = [(D_MODEL, F32), (D_MODEL, BF16), (3 * SB_W, BF16), (Q_RANK + KV_RANK, F32),
            (HEADS * LANES, BF16), (HEADS * LANES, BF16), (MLA_W, BF16)]
    return pl.pallas_call(
        body, name="fwd_in", grid=(rows // tm,),
        out_shape=tuple(jax.ShapeDtypeStruct((rows, n), dt) for n, dt in outs),
        in_specs=[_row_spec(D_MODEL), _mod_spec(tps), _const_spec((1, D_MODEL)), _const_spec((1, D_MODEL)),
                  _const_spec(w_in.shape), _const_spec((1, Q_RANK)), _const_spec((1, KV_RANK)),
                  _const_spec(w_uq.shape), _const_spec(w_ukv.shape),
                  _table_spec(tps), _table_spec(tps), _table_spec(tps)],
        out_specs=tuple(_row_spec(n) for n, _ in outs),
        compiler_params=_params(("arbitrary",)),
    )(x, mod, ln_g, ln_b, w_in, q_g, kv_g, w_uq, w_ukv, cos_t, sin_a, sin_b)


def _fwd_mix(sb_y, mla_y, x0, mod, w_o, ln_g, ln_b, seq):
    rows = x0.shape[0]
    tm = ROW_TILE
    tps = seq // tm

    def body(sb_ref, ml_ref, x0_ref, mod_ref, wo_ref, g_ref, b_ref, mix_ref, y1_ref, x1_ref, h2_ref):
        mix = _dot(sb_ref[...].astype(BF16), wo_ref[:SB_W, :]) + _dot(ml_ref[...].astype(BF16), wo_ref[SB_W:, :])
        mix_ref[...] = mix
        y1 = ALPHA * x0_ref[...] + (1.0 + mod_ref[0, 2:3, :]) * mix
        y1_ref[...] = y1
        x1, _, _ = _ln_fwd(y1, g_ref[...], b_ref[...])
        x1_ref[...] = x1
        h2_ref[...] = (x1 * (1.0 + mod_ref[0, 4:5, :]) + mod_ref[0, 3:4, :]).astype(BF16)

    outs = [(D_MODEL, F32), (D_MODEL, F32), (D_MODEL, F32), (D_MODEL, BF16)]
    return pl.pallas_call(
        body, name="fwd_mix", grid=(rows // tm,),
        out_shape=tuple(jax.ShapeDtypeStruct((rows, n), dt) for n, dt in outs),
        in_specs=[_row_spec(SB_W), _row_spec(MLA_W), _row_spec(D_MODEL), _mod_spec(tps), _const_spec(w_o.shape),
                  _const_spec((1, D_MODEL)), _const_spec((1, D_MODEL))],
        out_specs=tuple(_row_spec(n) for n, _ in outs),
        compiler_params=_params(("arbitrary",)),
    )(sb_y, mla_y, x0, mod, w_o, ln_g, ln_b)


def _fwd_mlp(h2, x1, mod, w_up, w_down, seq):
    rows = x1.shape[0]
    tm = ROW_TILE
    tps = seq // tm

    def body(h2_ref, x1_ref, mod_ref, wu_ref, wd_ref, u_ref, ff_ref, y2_ref):
        u = _dot(h2_ref[...], wu_ref[...])
        u_ref[...] = u.astype(BF16)
        act = jnp.square(jnp.maximum(u, 0.0)).astype(BF16)
        ff = _dot(act, wd_ref[...])
        ff_ref[...] = ff
        y2_ref[...] = ALPHA * x1_ref[...] + (1.0 + mod_ref[0, 5:6, :]) * ff

    outs = [(D_FF, BF16), (D_MODEL, F32), (D_MODEL, F32)]
    return pl.pallas_call(
        body, name="fwd_mlp", grid=(rows // tm,),
        out_shape=tuple(jax.ShapeDtypeStruct((rows, n), dt) for n, dt in outs),
        in_specs=[_row_spec(D_MODEL), _row_spec(D_MODEL), _mod_spec(tps), _const_spec(w_up.shape),
                  _const_spec(w_down.shape)],
        out_specs=tuple(_row_spec(n) for n, _ in outs),
        compiler_params=_params(("arbitrary",)),
    )(h2, x1, mod, w_up, w_down)


def _acc_spec(rows=8, cols=D_MODEL):
    return pl.BlockSpec((rows, cols), lambda i: (0, 0))


def _bwd_out(y2, tgt, ff, u, mod, ln_g, ln_b, w_down, seq):
    rows = y2.shape[0]
    nb = rows // seq
    tm = ROW_TILE
    tps = seq // tm

    def body(y2_ref, t_ref, ff_ref, u_ref, mod_ref, g_ref, b_ref, wd_ref, dy2_ref, dff_ref, du_ref, acc_ref, dmod_ref):
        i = pl.program_id(0)

        @pl.when(i == 0)
        def _():
            acc_ref[...] = jnp.zeros_like(acc_ref)

        @pl.when(i % tps == 0)
        def _():
            dmod_ref[...] = jnp.zeros_like(dmod_ref)

        g = g_ref[...]
        x2, xhat, rstd = _ln_fwd(y2_ref[...], g, b_ref[...])
        err = x2 - t_ref[...]
        dx2 = err * (1.0 / D_MODEL)
        acc_ref[0:1, :] += _rowsum(dx2 * xhat)
        acc_ref[1:2, :] += _rowsum(dx2)
        acc_ref[2:3, :] += _rowsum(err * err) * (0.5 / D_MODEL)
        dy2 = _ln_bwd(dx2, xhat, rstd, g)
        dy2_ref[...] = dy2
        dmod_ref[0, 5:6, :] += _rowsum(dy2 * ff_ref[...])
        dff = ((1.0 + mod_ref[0, 5:6, :]) * dy2).astype(BF16)
        dff_ref[...] = dff
        da = _dot(dff, wd_ref[...], NT)
        du_ref[...] = (da * (2.0 * jnp.maximum(u_ref[...].astype(F32), 0.0))).astype(BF16)

    outs = [(D_MODEL, F32), (D_MODEL, BF16), (D_FF, BF16)]
    return pl.pallas_call(
        body, name="bwd_out", grid=(rows // tm,),
        out_shape=tuple(jax.ShapeDtypeStruct((rows, n), dt) for n, dt in outs)
        + (jax.ShapeDtypeStruct((8, D_MODEL), F32), jax.ShapeDtypeStruct((nb, 8, D_MODEL), F32)),
        in_specs=[_row_spec(D_MODEL), _row_spec(D_MODEL), _row_spec(D_MODEL), _row_spec(D_FF), _mod_spec(tps),
                  _const_spec((1, D_MODEL)), _const_spec((1, D_MODEL)), _const_spec(w_down.shape)],
        out_specs=tuple(_row_spec(n) for n, _ in outs) + (_acc_spec(), _mod_spec(tps)),
        compiler_params=_params(("arbitrary",)),
    )(y2, tgt, ff, u, mod, ln_g, ln_b, w_down)


def _bwd_mid(du, y1, mix, dy2, mod, ln_g, ln_b, w_up, w_o, seq):
    rows = y1.shape[0]
    nb = rows // seq
    tm = ROW_TILE
    tps = seq // tm

    def body(du_ref, y1_ref, mix_ref, dy2_ref, mod_ref, g_ref, b_ref, wu_ref, wo_ref,
             dy1_ref, dmix_ref, do_ref, acc_ref, dmod_ref):
        i = pl.program_id(0)

        @pl.when(i == 0)
        def _():
            acc_ref[...] = jnp.zeros_like(acc_ref)

        @pl.when(i % tps == 0)
        def _():
            dmod_ref[...] = jnp.zeros_like(dmod_ref)

        g = g_ref[...]
        x1, xhat, rstd = _ln_fwd(y1_ref[...], g, b_ref[...])
        dh2 = _dot(du_ref[...], wu_ref[...], NT)
        dmod_ref[0, 3:4, :] += _rowsum(dh2)
        dmod_ref[0, 4:5, :] += _rowsum(dh2 * x1)
        dx1 = ALPHA * dy2_ref[...] + dh2 * (1.0 + mod_ref[0, 4:5, :])
        acc_ref[0:1, :] += _rowsum(dx1 * xhat)
        acc_ref[1:2, :] += _rowsum(dx1)
        dy1 = _ln_bwd(dx1, xhat, rstd, g)
        dy1_ref[...] = dy1
        dmod_ref[0, 2:3, :] += _rowsum(dy1 * mix_ref[...])
        dmix = ((1.0 + mod_ref[0, 2:3, :]) * dy1).astype(BF16)
        dmix_ref[...] = dmix
        do_ref[...] = _dot(dmix, wo_ref[...], NT)

    outs = [(D_MODEL, F32), (D_MODEL, BF16), (D_MODEL, F32)]
    return pl.pallas_call(
        body, name="bwd_mid", grid=(rows // tm,),
        out_shape=tuple(jax.ShapeDtypeStruct((rows, n), dt) for n, dt in outs)
        + (jax.ShapeDtypeStruct((8, D_MODEL), F32), jax.ShapeDtypeStruct((nb, 8, D_MODEL), F32)),
        in_specs=[_row_spec(D_FF), _row_spec(D_MODEL), _row_spec(D_MODEL), _row_spec(D_MODEL), _mod_spec(tps),
                  _const_spec((1, D_MODEL)), _const_spec((1, D_MODEL)), _const_spec(w_up.shape),
                  _const_spec(w_o.shape)],
        out_specs=tuple(_row_spec(n) for n, _ in outs) + (_acc_spec(), _mod_spec(tps)),
        compiler_params=_params(("arbitrary",)),
    )(du, y1, mix, dy2, mod, ln_g, ln_b, w_up, w_o)


def _bwd_in(dqp, dkp, dvm, dq_sb, dk_sb, dv_sb, lat, x, x0, dy1, mod, ln_g, ln_b, w_in, q_g, kv_g, w_uq, w_ukv,
            cos_t, sin_a, sin_b, seq):
    rows = x.shape[0]
    nb = rows // seq
    tm = ROW_TILE
    tps = seq // tm
    n_lat = Q_RANK + KV_RANK

    def body(dqp_ref, dkp_ref, dvm_ref, dqs_ref, dks_ref, dvs_ref, lat_ref, x_ref, x0_ref, dy1_ref, mod_ref,
             g_ref, b_ref, win_ref, qg_ref, kvg_ref, wuq_ref, wukv_ref, cos_ref, sa_ref, sb_ref,
             dx_ref, dproj_ref, dqall_ref, dkv_ref, latn_ref, acc_ref, accl_ref, dmod_ref):
        i = pl.program_id(0)

        @pl.when(i == 0)
        def _():
            acc_ref[...] = jnp.zeros_like(acc_ref)
            accl_ref[...] = jnp.zeros_like(accl_ref)

        @pl.when(i % tps == 0)
        def _():
            dmod_ref[...] = jnp.zeros_like(dmod_ref)

        cos, sa, sb = cos_ref[...], sa_ref[...], sb_ref[...]
        lane = lax.broadcasted_iota(jnp.int32, (tm, LANES), 1)
        for hd in range(HEADS):
            sl = slice(hd * LANES, (hd + 1) * LANES)
            dqall_ref[:, sl] = _rope_t(dqp_ref[:, sl], cos, sa, sb).astype(BF16)
        dcqn = _dot(dqall_ref[...], wuq_ref[...], NT)
        cq = lat_ref[:, :Q_RANK]
        qg = qg_ref[...]
        rq = lax.rsqrt(_mean(cq * cq) + RMS_EPS)
        cqn = cq * rq
        latn_ref[:, :Q_RANK] = (cqn * qg).astype(BF16)
        accl_ref[0:1, :Q_RANK] += _rowsum(dcqn * cqn)
        dqg = dcqn * qg
        dcq = rq * (dqg - cqn * _mean(dqg * cqn))
        dkr = jnp.zeros((tm, LANES), F32)
        for hd in range(HEADS):
            sl = slice(hd * LANES, (hd + 1) * LANES)
            dk = dkp_ref[:, sl]
            dkr = dkr + dk
            dkv_ref[:, sl] = jnp.where(lane < NOPE, dk, 0.0).astype(BF16)
        dkv_ref[:, HEADS * LANES:] = dvm_ref[...].astype(BF16)
        dckvn = _dot(dkv_ref[...], wukv_ref[...], NT)
        ckv = lat_ref[:, Q_RANK:]
        kvg = kvg_ref[...]
        rkv = lax.rsqrt(_mean(ckv * ckv) + RMS_EPS)
        ckvn = ckv * rkv
        latn_ref[:, Q_RANK:] = (ckvn * kvg).astype(BF16)
        accl_ref[1:2, :KV_RANK] += _rowsum(dckvn * ckvn)
        dkg = dckvn * kvg
        dckv = rkv * (dkg - ckvn * _mean(dkg * ckvn))
        dkr = _rope_t(jnp.where(lane >= NOPE, dkr, 0.0), cos, sa, sb)
        dproj_ref[:, :SB_W] = dqs_ref[...]
        dproj_ref[:, SB_W:2 * SB_W] = dks_ref[...].astype(BF16)
        dproj_ref[:, 2 * SB_W:3 * SB_W] = dvs_ref[...].astype(BF16)
        dproj_ref[:, 3 * SB_W:3 * SB_W + Q_RANK] = dcq.astype(BF16)
        dproj_ref[:, 3 * SB_W + Q_RANK:3 * SB_W + n_lat] = dckv.astype(BF16)
        dproj_ref[:, D_IN_PAD - LANES:] = dkr.astype(BF16)
        dh = _dot(dproj_ref[...], win_ref[...], NT)
        x0 = x0_ref[...]
        dmod_ref[0, 0:1, :] += _rowsum(dh)
        dmod_ref[0, 1:2, :] += _rowsum(dh * x0)
        dx0 = ALPHA * dy1_ref[...] + dh * (1.0 + mod_ref[0, 1:2, :])
        g = g_ref[...]
        _, xhat, rstd = _ln_fwd(x_ref[...], g, b_ref[...])
        acc_ref[0:1, :] += _rowsum(dx0 * xhat)
        acc_ref[1:2, :] += _rowsum(dx0)
        dx_ref[...] = _ln_bwd(dx0, xhat, rstd, g)

    outs = [(D_MODEL, F32), (D_IN_PAD, BF16), (HEADS * LANES, BF16), (HEADS * LANES + MLA_W, BF16), (n_lat, BF16)]
    return pl.pallas_call(
        body, name="bwd_in", grid=(rows // tm,),
        out_shape=tuple(jax.ShapeDtypeStruct((rows, n), dt) for n, dt in outs)
        + (jax.ShapeDtypeStruct((8, D_MODEL), F32), jax.ShapeDtypeStruct((8, Q_RANK), F32),
           jax.ShapeDtypeStruct((nb, 8, D_MODEL), F32)),
        in_specs=[_row_spec(HEADS * LANES), _row_spec(HEADS * LANES), _row_spec(MLA_W),
                  _row_spec(SB_W), _row_spec(SB_W), _row_spec(SB_W), _row_spec(n_lat),
                  _row_spec(D_MODEL), _row_spec(D_MODEL), _row_spec(D_MODEL), _mod_spec(tps),
                  _const_spec((1, D_MODEL)), _const_spec((1, D_MODEL)), _const_spec(w_in.shape),
                  _const_spec((1, Q_RANK)), _const_spec((1, KV_RANK)), _const_spec(w_uq.shape),
                  _const_spec(w_ukv.shape), _table_spec(tps), _table_spec(tps), _table_spec(tps)],
        out_specs=tuple(_row_spec(n) for n, _ in outs) + (_acc_spec(), _acc_spec(8, Q_RANK), _mod_spec(tps)),
        compiler_params=_params(("arbitrary",)),
    )(dqp, dkp, dvm, dq_sb, dk_sb, dv_sb, lat, x, x0, dy1, mod, ln_g, ln_b, w_in, q_g, kv_g, w_uq, w_ukv,
      cos_t, sin_a, sin_b)


def _wgrad(a, b, name, pre=None, tm=512, tn=1024, tk=2048):
    rows, m = a.shape
    n = b.shape[1]
    tm, tn, tk = min(tm, m), min(tn, n), min(tk, rows)
    if m % tm:
        tm = m
    if n % tn:
        tn = n

    def body(a_ref, b_ref, o_ref):
        @pl.when(pl.program_id(2) == 0)
        def _():
            o_ref[...] = jnp.zeros_like(o_ref)

        av = a_ref[...]
        if pre == "relu2":
            av = jnp.square(jnp.maximum(av.astype(F32), 0.0))
        o_ref[...] += _dot(av.astype(BF16), b_ref[...].astype(BF16), TN)

    return pl.pallas_call(
        body, name=name, grid=(m // tm, n // tn, rows // tk),
        out_shape=jax.ShapeDtypeStruct((m, n), F32),
        in_specs=[pl.BlockSpec((tk, tm), lambda i, j, k: (k, i)), pl.BlockSpec((tk, tn), lambda i, j, k: (k, j))],
        out_specs=pl.BlockSpec((tm, tn), lambda i, j, k: (i, j)),
        compiler_params=_params(("arbitrary", "arbitrary", "arbitrary")),
    )(a, b)


def _pair(pp):
    return slice(pp * LANES, (pp + 1) * LANES)


def _head_mask(lane, hh):
    return jnp.where((lane >= 64) if hh else (lane < 64), 1.0, 0.0).astype(BF16)


def _tri(t, kind):
    r = lax.broadcasted_iota(jnp.int32, (t, t), 0)
    c = lax.broadcasted_iota(jnp.int32, (t, t), 1)
    one = jnp.where(r > c if kind == "suffix" else r < c, 1.0, 0.0).astype(BF16)
    return jnp.concatenate([one, one], axis=0)


def _split_dot(v, tri2):
    hi = v.astype(BF16)
    lo = (v - hi.astype(F32)).astype(BF16)
    return _dot(jnp.concatenate([hi, lo], axis=1), tri2)


def _sb_logits(z, valid):
    log_keep = -(jnp.maximum(z, 0.0) + jnp.log(1.0 + jnp.exp(-jnp.abs(z))))
    log_beta = z + log_keep
    if valid is not None:
        log_keep = jnp.where(valid, log_keep, 0.0)
    return log_keep, log_beta


def _attention_call(body, ex, name, grid, operands, in_specs, out_shapes, out_specs):
    n_in, n_out = len(operands), len(out_shapes)
    total = grid[0] * grid[1] * grid[2]
    any_spec = pl.BlockSpec(memory_space=pl.ANY)

    def carrier(*refs):
        ins, outs, (start, middle, finish) = _carried(ex, refs, n_in, n_out)
        step = (pl.program_id(0) * grid[1] + pl.program_id(1)) * grid[2] + pl.program_id(2)
        pl.when(step == 0)(start)
        pl.when(step == total // 2)(middle)
        body(*ins, *outs)
        pl.when(step == total - 1)(finish)

    carried = ex is not None
    return pl.pallas_call(
        carrier if carried else body, name=name, grid=grid,
        out_shape=tuple(out_shapes) + ((ex.out_shape,) if carried else ()),
        in_specs=list(in_specs) + ([any_spec] if carried else []),
        out_specs=tuple(out_specs) + ((any_spec,) if carried else ()),
        scratch_shapes=ex.scratch if carried else [],
        compiler_params=_params(("arbitrary", "arbitrary", "arbitrary")),
    )(*operands, *([ex.operand] if carried else []))


def _sb_fwd(qkv, seq, ex=None):
    rows = qkv.shape[0]
    nb = rows // seq
    t = min(ATTN_TILE, seq)
    nq = seq // t

    ap = ATTN_PAIRS
    width = ap * LANES
    groups = SB_W // width
    hds = [(pp, hh) for pp in range(ap) for hh in range(2)]

    def body(q_ref, k_ref, v_ref, tri_ref, o_ref, car_ref):
        i = pl.program_id(2)
        lane = lax.broadcasted_iota(jnp.int32, (t, LANES), 1)
        row = lax.broadcasted_iota(jnp.int32, (t, t), 0)
        col = lax.broadcasted_iota(jnp.int32, (t, t), 1)
        strict = col < row
        tri = tri_ref[...]
        masks = [_head_mask(lane, hh) for hh in range(2)]
        qms = [q_ref[:, _pair(pp)] * masks[hh] for pp, hh in hds]

        def step(kb, carry, valid):
            c_sums, accs, cars = carry
            accs, cars = list(accs), list(cars)
            start = pl.multiple_of(kb * t, t)
            kss = [k_ref[pl.ds(start, t), _pair(pp)] for pp in range(ap)]
            vss = [v_ref[pl.ds(start, t), _pair(pp)] for pp in range(ap)]
            zs = [_dot(qms[n], kss[pp], NT) for n, (pp, _) in enumerate(hds)]
            logs = [_sb_logits(z, valid) for z in zs]
            sufs = [_split_dot(lg[0], tri) for lg in logs]
            new_sums = []
            for n, (pp, hh) in enumerate(hds):
                log_keep, log_beta = logs[n]
                w = jnp.exp(log_beta + sufs[n] + c_sums[n])
                if valid is not None:
                    w = jnp.where(valid, w, 0.0)
                accs[pp] = accs[pp] + _dot(w.astype(BF16), vss[pp] * masks[hh])
                cars[pp] = jnp.where(lane == hh * 8 + kb, c_sums[n], cars[pp])
                new_sums.append(c_sums[n] + jnp.sum(log_keep, axis=1, keepdims=True))
            return tuple(new_sums), tuple(accs), tuple(cars)

        zeros = tuple(jnp.zeros((t, LANES), F32) for _ in range(ap))
        carry = step(i, (tuple(jnp.zeros((t, 1), F32) for _ in hds), zeros, zeros), strict)
        _, accs, cars = lax.fori_loop(0, i, lambda j, cr: step(i - 1 - j, cr, None), carry)
        for pp in range(ap):
            o_ref[:, _pair(pp)] = accs[pp]
            car_ref[:, _pair(pp)] = cars[pp]

    qspec = pl.BlockSpec((t, width), lambda b, p, i: (b * nq + i, p))
    return _attention_call(
        body, ex, "sb_fwd", (nb, groups, nq),
        [qkv, qkv, qkv, _tri(t, "suffix")],
        [qspec,
         pl.BlockSpec((seq, width), lambda b, p, i: (b, groups + p)),
         pl.BlockSpec((seq, width), lambda b, p, i: (b, 2 * groups + p)),
         _const_spec((2 * t, t))],
        [jax.ShapeDtypeStruct((rows, SB_W), F32), jax.ShapeDtypeStruct((rows, SB_W), F32)],
        [qspec, qspec])


def _sb_bwd(qkv, d_out, cars, seq, ex=None):
    rows = qkv.shape[0]
    nb = rows // seq
    t = min(ATTN_TILE, seq)
    nq = seq // t
    ap = ATTN_PAIRS
    width = ap * LANES
    groups = SB_W // width
    hds = [(pp, hh) for pp in range(ap) for hh in range(2)]

    def body(q_ref, k_ref, v_ref, do_ref, car_ref, tri_ref, pre_ref, dq_ref, dk_ref, dv_ref):
        i = pl.program_id(2)

        @pl.when(i == 0)
        def _():
            dk_ref[...] = jnp.zeros_like(dk_ref)
            dv_ref[...] = jnp.zeros_like(dv_ref)

        lane = lax.broadcasted_iota(jnp.int32, (t, LANES), 1)
        row = lax.broadcasted_iota(jnp.int32, (t, t), 0)
        col = lax.broadcasted_iota(jnp.int32, (t, t), 1)
        strict = col < row
        tri, pre = tri_ref[...], pre_ref[...]
        masks = [_head_mask(lane, hh) for hh in range(2)]
        qms = [q_ref[:, _pair(pp)] * masks[hh] for pp, hh in hds]
        doms = [do_ref[:, _pair(pp)].astype(BF16) * masks[hh] for pp, hh in hds]
        cars = [car_ref[:, _pair(pp)] for pp in range(ap)]

        def step(kb, carry, valid):
            g_pres, dqs = carry
            dqs = list(dqs)
            start = pl.multiple_of(kb * t, t)
            kss = [k_ref[pl.ds(start, t), _pair(pp)] for pp in range(ap)]
            vss = [v_ref[pl.ds(start, t), _pair(pp)] for pp in range(ap)]
            zs = [_dot(qms[n], kss[pp], NT) for n, (pp, _) in enumerate(hds)]
            dws = [_dot(doms[n], vss[pp], NT) for n, (pp, _) in enumerate(hds)]
            logs = [_sb_logits(z, valid) for z in zs]
            sufs = [_split_dot(lg[0], tri) for lg in logs]
            ws, gs = [], []
            for n, (pp, hh) in enumerate(hds):
                c_sum = jnp.sum(jnp.where(lane == hh * 8 + kb, cars[pp], 0.0), axis=1, keepdims=True)
                w = jnp.exp(logs[n][1] + sufs[n] + c_sum)
                if valid is not None:
                    w = jnp.where(valid, w, 0.0)
                ws.append(w)
                gs.append(dws[n] * w)
            befores = [g_pres[n] + _split_dot(gs[n], pre) for n in range(len(hds))]
            for pp in range(ap):
                dv_ref[pl.ds(start, t), _pair(pp)] += (_dot(ws[2 * pp].astype(BF16), doms[2 * pp], TN)
                                                       + _dot(ws[2 * pp + 1].astype(BF16), doms[2 * pp + 1], TN))
            dzbs = []
            for n in range(len(hds)):
                dz = gs[n] * jnp.exp(logs[n][0]) - jnp.exp(logs[n][1]) * befores[n]
                if valid is not None:
                    dz = jnp.where(valid, dz, 0.0)
                dzbs.append(dz.astype(BF16))
            for pp in range(ap):
                a, b = 2 * pp, 2 * pp + 1
                dqs[pp] = dqs[pp] + _dot(dzbs[a], kss[pp] * masks[0]) + _dot(dzbs[b], kss[pp] * masks[1])
                dk_ref[pl.ds(start, t), _pair(pp)] += _dot(dzbs[a], qms[a], TN) + _dot(dzbs[b], qms[b], TN)
            new_pres = [g_pres[n] + jnp.sum(gs[n], axis=1, keepdims=True) for n in range(len(hds))]
            return tuple(new_pres), tuple(dqs)

        init = (tuple(jnp.zeros((t, 1), F32) for _ in hds), tuple(jnp.zeros((t, LANES), F32) for _ in range(ap)))
        carry = lax.fori_loop(0, i, lambda kb, cr: step(kb, cr, None), init)
        _, dqs = step(i, carry, strict)
        for pp in range(ap):
            dq_ref[:, _pair(pp)] = (dqs[pp] * SB_SCALE).astype(BF16)

    qspec = pl.BlockSpec((t, width), lambda b, p, i: (b * nq + i, p))
    kspec_out = pl.BlockSpec((seq, width), lambda b, p, i: (b, p))
    return _attention_call(
        body, ex, "sb_bwd", (nb, groups, nq),
        [qkv, qkv, qkv, d_out, cars, _tri(t, "suffix"), _tri(t, "prefix")],
        [qspec,
         pl.BlockSpec((seq, width), lambda b, p, i: (b, groups + p)),
         pl.BlockSpec((seq, width), lambda b, p, i: (b, 2 * groups + p)),
         qspec, qspec, _const_spec((2 * t, t)), _const_spec((2 * t, t))],
        [jax.ShapeDtypeStruct((rows, SB_W), BF16), jax.ShapeDtypeStruct((rows, SB_W), F32),
         jax.ShapeDtypeStruct((rows, SB_W), F32)],
        [qspec, kspec_out, kspec_out])


def _mla_scores(qh, ks, allowed):
    s = _dot(qh, ks, NT) * MLA_SCALE
    if allowed is not None:
        s = jnp.where(allowed, s, jnp.finfo(F32).min)
    return s


def _mla_fwd(qp, kp, vm, seq, ex=None, chunk=64):
    rows = qp.shape[0]
    nb = rows // seq
    t = min(ATTN_TILE, seq)
    nq = seq // t
    shift = int(math.log2(chunk))
    ap = ATTN_PAIRS
    width = ap * LANES
    groups = MLA_W // width
    hds = [(pp, hh) for pp in range(ap) for hh in range(2)]

    def body(q_ref, k_ref, v_ref, o_ref, lse_ref):
        i = pl.program_id(2)
        lane = lax.broadcasted_iota(jnp.int32, (t, LANES), 1)
        row = lax.broadcasted_iota(jnp.int32, (t, t), 0)
        col = lax.broadcasted_iota(jnp.int32, (t, t), 1)
        allowed_diag = jnp.right_shift(col, shift) <= jnp.right_shift(row, shift)
        masks = [_head_mask(lane, hh) for hh in range(2)]
        qhs = [q_ref[:, _pair(n)] for n in range(len(hds))]

        def step(kb, carry, allowed):
            start = pl.multiple_of(kb * t, t)
            vss = [v_ref[pl.ds(start, t), _pair(pp)] for pp in range(ap)]
            scores = [_mla_scores(qhs[n], k_ref[pl.ds(start, t), _pair(n)], allowed) for n in range(len(hds))]
            new = []
            for n, (pp, hh) in enumerate(hds):
                m_run, l_run, acc = carry[n]
                s = scores[n]
                m_new = jnp.maximum(m_run, jnp.max(s, axis=1, keepdims=True))
                p = jnp.exp(s - m_new)
                scale = jnp.exp(m_run - m_new)
                l_run = scale * l_run + jnp.sum(p, axis=1, keepdims=True)
                acc = scale * acc + _dot(p.astype(BF16), vss[pp] * masks[hh])
                new.append((m_new, l_run, acc))
            return tuple(new)

        init = (jnp.full((t, 1), jnp.finfo(F32).min, F32), jnp.zeros((t, 1), F32), jnp.zeros((t, LANES), F32))
        carry = step(i, tuple(init for _ in hds), allowed_diag)
        carry = lax.fori_loop(0, i, lambda kb, cr: step(kb, cr, None), carry)
        for pp in range(ap):
            out = jnp.zeros((t, LANES), F32)
            lses = jnp.zeros((t, LANES), F32)
            for hh in range(2):
                m_run, l_run, acc = carry[2 * pp + hh]
                out = out + acc / l_run
                lses = jnp.where(lane == hh, m_run + jnp.log(l_run), lses)
            o_ref[:, _pair(pp)] = out
            lse_ref[:, _pair(pp)] = lses

    ospec = pl.BlockSpec((t, width), lambda b, p, i: (b * nq + i, p))
    return _attention_call(
        body, ex, "mla_fwd", (nb, groups, nq), [qp, kp, vm],
        [pl.BlockSpec((t, 2 * width), lambda b, p, i: (b * nq + i, p)),
         pl.BlockSpec((seq, 2 * width), lambda b, p, i: (b, p)),
         pl.BlockSpec((seq, width), lambda b, p, i: (b, p))],
        [jax.ShapeDtypeStruct((rows, MLA_W), F32), jax.ShapeDtypeStruct((rows, MLA_W), F32)],
        [ospec, ospec])


def _mla_bwd(qp, kp, vm, d_out, out, lse, seq, ex=None, chunk=64):
    rows = qp.shape[0]
    nb = rows // seq
    t = min(ATTN_TILE, seq)
    nq = seq // t
    shift = int(math.log2(chunk))
    ap = ATTN_PAIRS
    width = ap * LANES
    groups = MLA_W // width
    hds = [(pp, hh) for pp in range(ap) for hh in range(2)]
    nh = len(hds)

    def body(q_ref, k_ref, v_ref, do_ref, o_ref, lse_ref, dq_ref, dk_ref, dv_ref):
        i = pl.program_id(2)

        @pl.when(i == 0)
        def _():
            dk_ref[...] = jnp.zeros_like(dk_ref)
            dv_ref[...] = jnp.zeros_like(dv_ref)

        lane = lax.broadcasted_iota(jnp.int32, (t, LANES), 1)
        row = lax.broadcasted_iota(jnp.int32, (t, t), 0)
        col = lax.broadcasted_iota(jnp.int32, (t, t), 1)
        allowed_diag = jnp.right_shift(col, shift) <= jnp.right_shift(row, shift)
        qhs = [q_ref[:, _pair(n)] for n in range(nh)]
        doms, deltas, lse_hs = [], [], []
        for pp, hh in hds:
            do = do_ref[:, _pair(pp)]
            d_o = do * o_ref[:, _pair(pp)]
            doms.append(do.astype(BF16) * _head_mask(lane, hh))
            deltas.append(jnp.sum(jnp.where((lane >= 64) if hh else (lane < 64), d_o, 0.0), axis=1, keepdims=True))
            lse_hs.append(jnp.sum(jnp.where(lane == hh, lse_ref[:, _pair(pp)], 0.0), axis=1, keepdims=True))

        def step(kb, dqs, allowed):
            start = pl.multiple_of(kb * t, t)
            vss = [v_ref[pl.ds(start, t), _pair(pp)] for pp in range(ap)]
            kss = [k_ref[pl.ds(start, t), _pair(n)] for n in range(nh)]
            scores = [_mla_scores(qhs[n], kss[n], allowed) for n in range(nh)]
            dps = [_dot(doms[n], vss[pp], NT) for n, (pp, _) in enumerate(hds)]
            ps = [jnp.exp(scores[n] - lse_hs[n]) for n in range(nh)]
            dss = [(ps[n] * (dps[n] - deltas[n]) * MLA_SCALE).astype(BF16) for n in range(nh)]
            for pp in range(ap):
                a, b = 2 * pp, 2 * pp + 1
                dv_ref[pl.ds(start, t), _pair(pp)] += (_dot(ps[a].astype(BF16), doms[a], TN)
                                                       + _dot(ps[b].astype(BF16), doms[b], TN))
            for n in range(nh):
                dk_ref[pl.ds(start, t), _pair(n)] += _dot(dss[n], qhs[n], TN)
            return tuple(dqs[n] + _dot(dss[n], kss[n]) for n in range(nh))

        dqs = lax.fori_loop(0, i, lambda kb, cr: step(kb, cr, None),
                            tuple(jnp.zeros((t, LANES), F32) for _ in range(nh)))
        dqs = step(i, dqs, allowed_diag)
        for n in range(nh):
            dq_ref[:, _pair(n)] = dqs[n]

    ospec = pl.BlockSpec((t, width), lambda b, p, i: (b * nq + i, p))
    return _attention_call(
        body, ex, "mla_bwd", (nb, groups, nq), [qp, kp, vm, d_out, out, lse],
        [pl.BlockSpec((t, 2 * width), lambda b, p, i: (b * nq + i, p)),
         pl.BlockSpec((seq, 2 * width), lambda b, p, i: (b, p)),
         pl.BlockSpec((seq, width), lambda b, p, i: (b, p)),
         pl.BlockSpec((t, width), lambda b, p, i: (b * nq + i, groups + p)),
         ospec, ospec],
        [jax.ShapeDtypeStruct((rows, HEADS * LANES), F32), jax.ShapeDtypeStruct((rows, HEADS * LANES), F32),
         jax.ShapeDtypeStruct((rows, MLA_W), F32)],
        [pl.BlockSpec((t, 2 * width), lambda b, p, i: (b * nq + i, p)),
         pl.BlockSpec((seq, 2 * width), lambda b, p, i: (b, p)),
         pl.BlockSpec((seq, width), lambda b, p, i: (b, p))])


PACK_COLS = 1024
PACK_ALIGN = 16
GROUP_IN = (384, ((1024, 552, 1), (384, 192, 1), (256, 256, 1)))
GROUP_MLP = (1152, ((256, 1024, 0), (1024, 1024, 1), (1024, 1024, 0)))


def _pack_rows(r, c):
    return (r // 2) * c // PACK_COLS


def _slot_rows(r, c):
    return -(-_pack_rows(r, c) // PACK_ALIGN) * PACK_ALIGN


def _join_slots(parts, group):
    total, weights = group
    padded = [jnp.pad(p, ((0, 0), (0, _slot_rows(r, c) - p.shape[1]), (0, 0))) for p, (r, c, _) in zip(parts, weights)]
    used = sum(_slot_rows(r, c) for r, c, _ in weights)
    if total > used:
        padded.append(jnp.zeros((parts[0].shape[0], total - used, PACK_COLS), parts[0].dtype))
    return jnp.concatenate(padded, axis=1)


def _split_slots(packed, group):
    out, at = [], 0
    for r, c, _ in group[1]:
        out.append(packed[:, at:at + _pack_rows(r, c), :])
        at += _slot_rows(r, c)
    return out


def _pack_halves(shards, group):
    return _join_slots([s.reshape(2, _pack_rows(r, c), PACK_COLS) for s, (r, c, _) in zip(shards, group[1])], group)


def _unpack_halves(packed, group):
    return [p.reshape(r, c) for p, (r, c, _) in zip(_split_slots(packed, group), group[1])]


def _unpack_full(gathered, group):
    out = []
    for p, (r, c, axis) in zip(_split_slots(gathered, group), group[1]):
        shards = p.reshape(4, r, c)
        out.append(shards.reshape(4 * r, c) if axis == 0 else jnp.moveaxis(shards, 0, 1).reshape(r, 4 * c))
    return out


def _pack_full(grads, group):
    parts = []
    for gr, (r, c, axis) in zip(grads, group[1]):
        shards = gr.reshape(4, r, c) if axis == 0 else jnp.moveaxis(gr.reshape(r, 4, c), 1, 0)
        parts.append(shards.reshape(8, _pack_rows(r, c), PACK_COLS))
    return _join_slots(parts, group)


def _pad_w_in(w_in):
    z = jnp.zeros((D_MODEL, 1), w_in.dtype)
    return jnp.concatenate([w_in[:, :2176], jnp.tile(z, (1, 64)), w_in[:, 2176:], jnp.tile(z, (1, 32))], axis=1)


def _unpad_w_in(g):
    return jnp.concatenate([g[:, :2176], g[:, 2240:2272]], axis=1)


def _pad_heads(w, used):
    k = w.shape[0]
    w3 = w.reshape(k, HEADS, used)
    return jnp.pad(w3, ((0, 0), (0, 0), (0, LANES - used))).reshape(k, HEADS * LANES)


def _unpad_heads(g, used):
    k = g.shape[0]
    return g.reshape(k, HEADS, LANES)[:, :, :used].reshape(k, HEADS * used)


def _rope_tables(seq):
    inv_freq = 1.0 / (ROPE_BASE ** (jnp.arange(0, ROPE, 2, dtype=F32) / ROPE))
    ang = jnp.arange(seq, dtype=F32)[:, None] * inv_freq[None, :]
    cos, sin = jnp.cos(ang), jnp.sin(ang)
    one, zero = jnp.ones((seq, NOPE), F32), jnp.zeros((seq, NOPE), F32)
    z16, z32 = jnp.zeros((seq, 16), F32), jnp.zeros((seq, 32), F32)
    cos_t = jnp.concatenate([one, cos, cos, jnp.ones((seq, 32), F32)], axis=1)
    sin_a = jnp.concatenate([zero, -sin, z16, z32], axis=1)
    sin_b = jnp.concatenate([zero, z16, sin, z32], axis=1)
    return cos_t, sin_a, sin_b


SMALL = (("ln_in_g", 1024), ("ln_in_b", 1024), ("b_ada", 6144), ("q_norm_g", 384), ("kv_norm_g", 256),
         ("ln1_g", 1024), ("ln1_b", 1024), ("ln2_g", 1024), ("ln2_b", 1024))
SMALL_TOTAL = sum(n for _, n in SMALL)
SMALL_ROWS = -(-SMALL_TOTAL // LANES // 8) * 8


def _pack_small(vals):
    flat = jnp.concatenate([v.reshape(-1) for v in vals])
    return jnp.pad(flat, (0, SMALL_ROWS * LANES - SMALL_TOTAL)).reshape(SMALL_ROWS, LANES)


def _unpack_small(packed, like):
    flat, out, at = packed.reshape(-1), [], 0
    for (_, n), ref in zip(SMALL, like):
        out.append(flat[at:at + n].reshape(ref.shape))
        at += n
    return out


def kernel(x, c, ln_in_g, ln_in_b, w_ada, b_ada, w_in, q_norm_g, kv_norm_g, w_uq, w_ukv, w_o, ln1_g, ln1_b, w_up, w_down, ln2_g, ln2_b, loss_target, m_ln_in_g, m_ln_in_b, m_w_ada, m_b_ada, m_w_in, m_q_norm_g, m_kv_norm_g, m_w_uq, m_w_ukv, m_w_o, m_ln1_g, m_ln1_b, m_w_up, m_w_down, m_ln2_g, m_ln2_b, v_ln_in_g, v_ln_in_b, v_w_ada, v_b_ada, v_w_in, v_q_norm_g, v_kv_norm_g, v_w_uq, v_w_ukv, v_w_o, v_ln1_g, v_ln1_b, v_w_up, v_w_down, v_ln2_g, v_ln2_b):
    nb, seq, _ = x.shape
    rows = nb * seq
    ix, iy, ic = lax.axis_index("x"), lax.axis_index("y"), lax.axis_index("c")
    chip = 2 * ix + iy
    dev = 2 * chip + ic

    def my_half(shards, group):
        packed = _pack_halves([s.astype(BF16) for s in shards], group)
        return lax.dynamic_index_in_dim(packed, ic, 0, keepdims=False)

    f_in, f_uq, f_ukv = _unpack_full(_gather8(my_half([w_in[0], w_uq[0], w_ukv[0]], GROUP_IN), "gather_w_in"),
                                     GROUP_IN)
    late_weights = _gather_exchange(my_half([w_o[0], w_up[0], w_down[0]], GROUP_MLP))
    w_in_p = _pad_w_in(f_in)
    uq3 = f_uq.reshape(Q_RANK, HEADS, NOPE + ROPE)
    w_uq_p = jnp.pad(uq3, ((0, 0), (0, 0), (0, LANES - NOPE - ROPE))).reshape(Q_RANK, HEADS * LANES)
    w_ukv_p = jnp.concatenate([_pad_heads(f_ukv[:, :HEADS * NOPE], NOPE), f_ukv[:, HEADS * NOPE:]], axis=1)

    n_all = 8 * nb
    c_all = _gather8(c.reshape(-1, LANES), "gather_c").reshape(n_all, D_MODEL)
    ada_cols = w_ada.shape[2]
    b_sh = lax.dynamic_slice_in_dim(b_ada, chip * ada_cols, ada_cols, axis=1)
    mod_sh = _ada_fwd(c_all, w_ada[0], b_sh)
    mod_g = _gather8(mod_sh, "gather_mod")[0::2]
    mod_all = jnp.moveaxis(mod_g, 0, 1).reshape(n_all, N_MOD * D_MODEL)
    mod_mine = lax.dynamic_slice_in_dim(mod_all, dev * nb, nb, axis=0).reshape(nb, N_MOD, D_MODEL)
    mod = jnp.pad(mod_mine, ((0, 0), (0, 8 - N_MOD), (0, 0)))

    cos_t, sin_a, sin_b = _rope_tables(seq)
    row2 = lambda v: v.reshape(1, -1)

    x2d = x.reshape(rows, D_MODEL)
    x0, h, qkv, lat, qp, kp, vm = _fwd_in(x2d, mod, row2(ln_in_g), row2(ln_in_b), w_in_p, q_norm_g, kv_norm_g,
                                          w_uq_p, w_ukv_p, cos_t, sin_a, sin_b, seq)
    sb_y, cars, gathered_mlp = _sb_fwd(qkv, seq, late_weights)
    f_o, f_up, f_down = _unpack_full(gathered_mlp, GROUP_MLP)
    mla_y, lse = _mla_fwd(qp, kp, vm, seq)
    mix, y1, x1, h2 = _fwd_mix(sb_y, mla_y, x0, mod, f_o, ln1_g, ln1_b, seq)
    u, ff, y2 = _fwd_mlp(h2, x1, mod, f_up, f_down, seq)

    dy2, dff, du, acc2, dmod_a = _bwd_out(y2, loss_target.reshape(rows, D_MODEL), ff, u, mod, ln2_g, ln2_b, f_down, seq)
    dy1, dmix, d_attn, acc1, dmod_b = _bwd_mid(du, y1, mix, dy2, mod, ln1_g, ln1_b, f_up, f_o, seq)
    c_idx = ic.reshape(1).astype(jnp.int32)
    g_down = _wgrad(u, dff, "wgrad_down", pre="relu2")
    g_up = _wgrad(h2, du, "wgrad_up")
    g_o = jnp.concatenate([_wgrad(sb_y, dmix, "wgrad_o_sb"), _wgrad(mla_y, dmix, "wgrad_o_mla")], axis=0)
    blocks_mlp = _pack_full([g_o, g_up, g_down], GROUP_MLP)
    dq_sb, dk_sb, dv_sb, sibling_mlp = _sb_bwd(qkv, d_attn, cars, seq, _swap_cores_exchange(blocks_mlp))
    part_mlp, part_mlp_bf = _add_pairs(blocks_mlp, sibling_mlp, c_idx, "grad_add_cores_mlp")
    dqp, dkp, dvm, chips_mlp = _mla_bwd(qp, kp, vm, d_attn, mla_y, lse, seq, _scatter_chips_exchange(part_mlp_bf))
    grad_x, dproj, dqall, dkv, latn, acc0, acc_lat, dmod_c = _bwd_in(
        dqp, dkp, dvm, dq_sb, dk_sb, dv_sb, lat, x2d, x0, dy1, mod, row2(ln_in_g), row2(ln_in_b), w_in_p,
        q_norm_g, kv_norm_g, w_uq_p, w_ukv_p, cos_t, sin_a, sin_b, seq)

    g_in = _unpad_w_in(_wgrad(h, dproj, "wgrad_in", tn=768))
    g_uq = _unpad_heads(_wgrad(latn[:, :Q_RANK], dqall, "wgrad_uq"), NOPE + ROPE)
    g_ukv_p = _wgrad(latn[:, Q_RANK:], dkv, "wgrad_ukv", tn=512)
    g_ukv = jnp.concatenate([_unpad_heads(g_ukv_p[:, :HEADS * LANES], NOPE), g_ukv_p[:, HEADS * LANES:]], axis=1)
    blocks_in = _pack_full([g_in, g_uq, g_ukv], GROUP_IN)
    sibling_in = _run_exchange(_swap_cores_exchange(blocks_in), "grads_in_to_sibling")
    part_in, part_in_bf = _add_pairs(blocks_in, sibling_in, c_idx, "grad_add_cores_in")
    chips_in = _run_exchange(_scatter_chips_exchange(part_in_bf), "grads_in_to_chips")

    def own(part):
        return lax.dynamic_index_in_dim(part, chip, 0, keepdims=False)

    half = jnp.concatenate([_add_chips(own(part_in), chips_in, "grad_add_chips_in"),
                            _add_chips(own(part_mlp), chips_mlp, "grad_add_chips_mlp")], axis=0)
    other = _run_exchange(_swap_one_exchange(half), "grads_halves")
    both = jnp.where(ic == 0, jnp.stack([half, other]), jnp.stack([other, half]))
    gs_in, gs_uq, gs_ukv = _unpack_halves(both[:, :GROUP_IN[0]], GROUP_IN)
    gs_o, gs_up, gs_down = _unpack_halves(both[:, GROUP_IN[0]:], GROUP_MLP)

    dmod = (dmod_a + dmod_b + dmod_c)[:, :N_MOD, :]
    small_part = _pack_small([acc0[0], acc0[1], jnp.zeros((N_MOD * D_MODEL,), F32), acc_lat[0, :Q_RANK],
                              acc_lat[1, :KV_RANK], acc1[0], acc1[1], acc2[0], acc2[1]])
    payload = jnp.concatenate([small_part, dmod.reshape(-1, LANES)], axis=0)
    gathered = _gather8(payload, "gather_small")
    small_sum = _sum_lead(gathered[:, :SMALL_ROWS, :], "sum_small")
    dmod_all = gathered[:, SMALL_ROWS:, :].reshape(n_all, N_MOD * D_MODEL)
    g_b_ada = _sum_lead(dmod_all.reshape(n_all, N_MOD * D_MODEL // LANES, LANES), "sum_b_ada").reshape(1, -1)
    dmod_sh = lax.dynamic_slice_in_dim(dmod_all, chip * ada_cols, ada_cols, axis=1)
    g_w_ada = _ada_bwd(c_all, dmod_sh)

    small_like = [ln_in_g, ln_in_b, b_ada, q_norm_g, kv_norm_g, ln1_g, ln1_b, ln2_g, ln2_b]
    small_grads = _unpack_small(small_sum, small_like)
    small_grads[2] = g_b_ada
    loss = lax.psum(jnp.sum(acc2[2]), ("x", "y", "c"))

    big_w = {"w_ada": (w_ada[0], g_w_ada, m_w_ada[0], v_w_ada[0]), "w_in": (w_in[0], gs_in, m_w_in[0], v_w_in[0]),
             "w_uq": (w_uq[0], gs_uq, m_w_uq[0], v_w_uq[0]), "w_ukv": (w_ukv[0], gs_ukv, m_w_ukv[0], v_w_ukv[0]),
             "w_o": (w_o[0], gs_o, m_w_o[0], v_w_o[0]), "w_up": (w_up[0], gs_up, m_w_up[0], v_w_up[0]),
             "w_down": (w_down[0], gs_down, m_w_down[0], v_w_down[0])}
    res = {}
    for name, (w, g, m, v) in big_w.items():
        d, mn, vn = _adamw(w, g, m, v, "adamw_" + name)
        res[name] = (g[None], d[None], mn[None], vn[None])
    small_m = [m_ln_in_g, m_ln_in_b, m_b_ada, m_q_norm_g, m_kv_norm_g, m_ln1_g, m_ln1_b, m_ln2_g, m_ln2_b]
    small_v = [v_ln_in_g, v_ln_in_b, v_b_ada, v_q_norm_g, v_kv_norm_g, v_ln1_g, v_ln1_b, v_ln2_g, v_ln2_b]
    sd, sm, sv = _adamw(_pack_small(small_like), _pack_small(small_grads), _pack_small(small_m), _pack_small(small_v),
                        "adamw_small")
    for (name, _), g, d, mn, vn in zip(SMALL, small_grads, _unpack_small(sd, small_like),
                                       _unpack_small(sm, small_like), _unpack_small(sv, small_like)):
        res[name] = (g, d, mn, vn)

    order = ["ln_in_g", "ln_in_b", "w_ada", "b_ada", "w_in", "q_norm_g", "kv_norm_g", "w_uq", "w_ukv", "w_o",
             "ln1_g", "ln1_b", "w_up", "w_down", "ln2_g", "ln2_b"]
    outs = [loss, grad_x.reshape(nb, seq, D_MODEL)]
    for k in range(4):
        outs += [res[name][k] for name in order]
    return tuple(outs)
```

```python
import functools
import math

import jax
import jax.numpy as jnp
from jax import lax
from jax.experimental import pallas as pl
from jax.experimental.pallas import tpu as pltpu

F32 = jnp.float32
BF16 = jnp.bfloat16
MESH_IDS = pl.DeviceIdType.MESH

D_MODEL = 1024
HEADS = 8
HEAD_PAIRS = HEADS // 2
SB_W = 512
MLA_W = 512
NOPE = 64
ROPE = 32
Q_RANK = 384
KV_RANK = 256
D_IN = 2208
D_IN_PAD = 2304
D_FF = 4096
N_MOD = 6
LN_EPS = 1e-5
RMS_EPS = 1e-6
ALPHA = 2.0 ** 0.25
ROPE_BASE = 10000.0
SB_SCALE = 64 ** -0.5
MLA_SCALE = 96 ** -0.5
ADAM_LR = 0.001
ADAM_B1 = 0.9
ADAM_B2 = 0.999
ADAM_EPS = 1e-08
ADAM_WD = 0.01
ADAM_STEP = 10

LANES = 128
ROW_TILE = 256
ATTN_TILE = 256
ATTN_PAIRS = {"sb_fwd": 4, "sb_bwd": 2, "mla_fwd": 4, "mla_bwd": 4}
VMEM_LIMIT = 56 << 20

NT = (((1,), (1,)), ((), ()))
TN = (((0,), (0,)), ((), ()))


def _params(sem=None):
    return pltpu.CompilerParams(vmem_limit_bytes=VMEM_LIMIT, dimension_semantics=sem)


def _const_spec(shape):
    zeros = (0,) * len(shape)
    return pl.BlockSpec(shape, lambda *_: zeros, pipeline_mode=pl.Buffered(1))


def _dot(a, b, dims=None):
    if dims is None:
        return jnp.dot(a, b, preferred_element_type=F32)
    return lax.dot_general(a, b, dims, preferred_element_type=F32)


def _mean(v):
    return jnp.mean(v, axis=-1, keepdims=True)


def _rowsum(v):
    return jnp.sum(v, axis=0, keepdims=True)


def _ln_fwd(y, g, b):
    mu = _mean(y)
    yc = y - mu
    rstd = lax.rsqrt(_mean(yc * yc) + LN_EPS)
    xhat = yc * rstd
    return xhat * g + b, xhat, rstd


def _ln_bwd(dx, xhat, rstd, g):
    dxh = dx * g
    return rstd * (dxh - _mean(dxh) - xhat * _mean(dxh * xhat))


def _rope(v, cos, sin_a, sin_b):
    return v * cos + pltpu.roll(v, 112, 1) * sin_a + pltpu.roll(v, 16, 1) * sin_b


def _rope_t(dv, cos, sin_a, sin_b):
    return dv * cos + pltpu.roll(dv * sin_a, 16, 1) + pltpu.roll(dv * sin_b, 112, 1)


def _my_place():
    return lax.axis_index("x"), lax.axis_index("y"), lax.axis_index("c")


class _Exchange:
    def __init__(self, operand, out_shape, n_copies, phases):
        self.operand = operand
        self.out_shape = out_shape
        self.phases = phases
        self.scratch = [pltpu.SemaphoreType.DMA((n_copies,)), pltpu.SemaphoreType.DMA((n_copies,)),
                        pltpu.SemaphoreType.DMA]


def _run_exchange(ex, name):
    def body(in_ref, out_ref, send_sems, recv_sems, local_sem):
        for phase in ex.phases(in_ref, out_ref, send_sems, recv_sems, local_sem):
            phase()

    return pl.pallas_call(
        body, name=name, out_shape=ex.out_shape,
        in_specs=[pl.BlockSpec(memory_space=pl.ANY)], out_specs=pl.BlockSpec(memory_space=pl.ANY),
        scratch_shapes=ex.scratch,
    )(ex.operand)


def _nothing():
    pass


def _gather_exchange(v):
    m, n = v.shape

    def phases(v_ref, out_ref, send_sems, recv_sems, local_sem):
        x, y, c = _my_place()
        me, sibling = (x, y, c), (x, y, 1 - c)
        chips = [(1 - x, y), (x, 1 - y), (1 - x, 1 - y)]

        def rows(px, py, pc):
            return out_ref.at[4 * px + 2 * py + pc]

        def copy(k, block, to, src=None):
            return pltpu.make_async_remote_copy(
                src_ref=rows(*block) if src is None else src, dst_ref=rows(*block),
                send_sem=send_sems.at[k], recv_sem=recv_sems.at[k], device_id=to, device_id_type=MESH_IDS)

        mine = pltpu.make_async_copy(v_ref, rows(*me), local_sem)
        first = [copy(0, me, sibling, src=v_ref)]
        first += [copy(1 + j, me, (*chip, c), src=v_ref) for j, chip in enumerate(chips)]
        passed = [copy(4 + j, (*chip, c), sibling) for j, chip in enumerate(chips)]

        def start():
            mine.start()
            for cp in first:
                cp.start()

        def middle():
            for j, chip in enumerate(chips):
                copy(1 + j, (*chip, c), me).wait_recv()
                passed[j].start()

        def finish():
            copy(0, sibling, me).wait_recv()
            for j, chip in enumerate(chips):
                copy(4 + j, (*chip, 1 - c), me).wait_recv()
            for cp in first + passed:
                cp.wait_send()
            mine.wait()

        return start, middle, finish

    return _Exchange(v, jax.ShapeDtypeStruct((8, m, n), v.dtype), 7, phases)


def _direct_exchange(operand, out_shape, n_copies, make_copies):
    def phases(in_ref, out_ref, send_sems, recv_sems, local_sem):
        copies = make_copies(in_ref, out_ref, send_sems, recv_sems)

        def start():
            for cp in copies:
                cp.start()

        def finish():
            for cp in copies:
                cp.wait()

        return start, _nothing, finish

    return _Exchange(operand, out_shape, n_copies, phases)


def _swap_cores_exchange(blocks):
    _, m, n = blocks.shape

    def make_copies(g_ref, out_ref, send_sems, recv_sems):
        x, y, c = _my_place()
        return [pltpu.make_async_remote_copy(
            src_ref=g_ref.at[2 * j + (1 - c)], dst_ref=out_ref.at[j],
            send_sem=send_sems.at[j], recv_sem=recv_sems.at[j],
            device_id=(x, y, 1 - c), device_id_type=MESH_IDS) for j in range(4)]

    return _direct_exchange(blocks, jax.ShapeDtypeStruct((4, m, n), blocks.dtype), 4, make_copies)


def _scatter_chips_exchange(parts):
    _, m, n = parts.shape
    flips = [(1, 0), (0, 1), (1, 1)]

    def make_copies(p_ref, out_ref, send_sems, recv_sems):
        x, y, c = _my_place()
        copies = []
        for k, (fx, fy) in enumerate(flips):
            tx = 1 - x if fx else x
            ty = 1 - y if fy else y
            copies.append(pltpu.make_async_remote_copy(
                src_ref=p_ref.at[2 * tx + ty], dst_ref=out_ref.at[k],
                send_sem=send_sems.at[k], recv_sem=recv_sems.at[k],
                device_id=(tx, ty, c), device_id_type=MESH_IDS))
        return copies

    return _direct_exchange(parts, jax.ShapeDtypeStruct((3, m, n), parts.dtype), 3, make_copies)


def _pair_exchange(v):
    m, n = v.shape

    def phases(v_ref, out_ref, send_sems, recv_sems, local_sem):
        x, y, c = _my_place()
        mine = pltpu.make_async_copy(v_ref, out_ref.at[c], local_sem)
        theirs = pltpu.make_async_remote_copy(src_ref=v_ref, dst_ref=out_ref.at[c], send_sem=send_sems.at[0],
                                              recv_sem=recv_sems.at[0], device_id=(x, y, 1 - c),
                                              device_id_type=MESH_IDS)

        def start():
            mine.start()
            theirs.start()

        def finish():
            theirs.wait()
            mine.wait()

        return start, _nothing, finish

    return _Exchange(v, jax.ShapeDtypeStruct((2, m, n), v.dtype), 1, phases)


def _gather8(v, name):
    return _run_exchange(_gather_exchange(v), name)


def _carried(ex, refs, n_in, n_out):
    ins, ex_in = refs[:n_in], refs[n_in]
    outs, ex_out = refs[n_in + 1:n_in + 1 + n_out], refs[n_in + 1 + n_out]
    return ins, outs, ex.phases(ex_in, ex_out, *refs[n_in + 2 + n_out:])


def _ada_fwd(c_all, w_ada_sh, b_ada_sh):
    nb, cols = c_all.shape[0], w_ada_sh.shape[1]
    tn = 512

    def body(c_ref, w_ref, b_ref, o_ref):
        cv = c_ref[...]
        act = (cv * jax.nn.sigmoid(cv)).astype(BF16)
        o_ref[...] = _dot(act, w_ref[...].astype(BF16)) + b_ref[...]

    return pl.pallas_call(
        body, name="ada_fwd", grid=(cols // tn,),
        out_shape=jax.ShapeDtypeStruct((nb, cols), F32),
        in_specs=[pl.BlockSpec((nb, D_MODEL), lambda j: (0, 0)),
                  pl.BlockSpec((D_MODEL, tn), lambda j: (0, j)),
                  pl.BlockSpec((1, tn), lambda j: (0, j))],
        out_specs=pl.BlockSpec((nb, tn), lambda j: (0, j)),
        compiler_params=_params(("arbitrary",)),
    )(c_all, w_ada_sh, b_ada_sh)


def _ada_bwd(c_all, dmod_sh):
    nb, cols = dmod_sh.shape
    tn = 512

    def body(c_ref, d_ref, o_ref):
        cv = c_ref[...]
        act = (cv * jax.nn.sigmoid(cv)).astype(BF16)
        o_ref[...] = _dot(act, d_ref[...].astype(BF16), TN)

    return pl.pallas_call(
        body, name="ada_bwd", grid=(cols // tn,),
        out_shape=jax.ShapeDtypeStruct((D_MODEL, cols), F32),
        in_specs=[pl.BlockSpec((nb, D_MODEL), lambda j: (0, 0)),
                  pl.BlockSpec((nb, tn), lambda j: (0, j))],
        out_specs=pl.BlockSpec((D_MODEL, tn), lambda j: (0, j)),
        compiler_params=_params(("arbitrary",)),
    )(c_all, dmod_sh)


def _sum_lead(v, name):
    k, m, n = v.shape

    def body(v_ref, o_ref):
        acc = v_ref[0]
        for i in range(1, k):
            acc = acc + v_ref[i]
        o_ref[...] = acc

    return pl.pallas_call(
        body, name=name, out_shape=jax.ShapeDtypeStruct((m, n), F32),
        in_specs=[pl.BlockSpec((k, m, n), lambda: (0, 0, 0))],
        out_specs=pl.BlockSpec((m, n), lambda: (0, 0)),
        compiler_params=_params(),
    )(v)


def _adamw(w, g, m, v, name):
    rows, cols = w.shape
    tr = rows
    while tr * cols * 4 > (2 << 20) and tr % 16 == 0:
        tr //= 2

    def body(w_ref, g_ref, m_ref, v_ref, d_ref, mo_ref, vo_ref):
        gv = g_ref[...]
        mn = ADAM_B1 * m_ref[...] + (1.0 - ADAM_B1) * gv
        vn = ADAM_B2 * v_ref[...] + (1.0 - ADAM_B2) * (gv * gv)
        m_hat = mn / (1.0 - ADAM_B1 ** ADAM_STEP)
        v_hat = vn / (1.0 - ADAM_B2 ** ADAM_STEP)
        d_ref[...] = -ADAM_LR * (m_hat / (jnp.sqrt(v_hat) + ADAM_EPS) + ADAM_WD * w_ref[...])
        mo_ref[...] = mn
        vo_ref[...] = vn

    spec = pl.BlockSpec((tr, cols), lambda i: (i, 0))
    shape = jax.ShapeDtypeStruct((rows, cols), F32)
    return pl.pallas_call(
        body, name=name, grid=(rows // tr,), out_shape=(shape, shape, shape),
        in_specs=[spec, spec, spec, spec], out_specs=(spec, spec, spec),
        compiler_params=_params(("arbitrary",)),
    )(w, g, m, v)


def _add_rows(m, n):
    tr = 8
    while m % (tr * 2) == 0 and tr * 2 * n * 4 <= (1 << 20):
        tr *= 2
    assert m % tr == 0, (m, tr)
    return tr


def _add_pairs(blocks, recv, c_idx, name):
    _, m, n = blocks.shape
    tr = _add_rows(m, n)

    def body(c_ref, a_ref, b_ref, o_ref, ob_ref):
        s = a_ref[...] + b_ref[...]
        o_ref[...] = s
        ob_ref[...] = s.astype(BF16)

    grid_spec = pltpu.PrefetchScalarGridSpec(
        num_scalar_prefetch=1, grid=(4, m // tr),
        in_specs=[pl.BlockSpec((1, tr, n), lambda j, i, c: (2 * j + c[0], i, 0)),
                  pl.BlockSpec((1, tr, n), lambda j, i, c: (j, i, 0))],
        out_specs=(pl.BlockSpec((1, tr, n), lambda j, i, c: (j, i, 0)),
                   pl.BlockSpec((1, tr, n), lambda j, i, c: (j, i, 0))))
    return pl.pallas_call(
        body, name=name, grid_spec=grid_spec,
        out_shape=(jax.ShapeDtypeStruct((4, m, n), F32), jax.ShapeDtypeStruct((4, m, n), BF16)),
        compiler_params=_params(("arbitrary", "arbitrary")),
    )(c_idx, blocks, recv)


def _add_chips(own, recv, name):
    m, n = own.shape
    tr = _add_rows(m, n)

    def body(a_ref, r_ref, o_ref):
        acc = a_ref[...]
        for k in range(3):
            acc = acc + r_ref[k].astype(F32)
        o_ref[...] = acc

    return pl.pallas_call(
        body, name=name, grid=(m // tr,),
        out_shape=jax.ShapeDtypeStruct((m, n), F32),
        in_specs=[pl.BlockSpec((tr, n), lambda i: (i, 0)), pl.BlockSpec((3, tr, n), lambda i: (0, i, 0))],
        out_specs=pl.BlockSpec((tr, n), lambda i: (i, 0)),
        compiler_params=_params(("arbitrary",)),
    )(own, recv)


def _row_spec(cols):
    return pl.BlockSpec((ROW_TILE, cols), lambda i: (i, 0))


def _mod_spec(tiles_per_seq):
    return pl.BlockSpec((1, 8, D_MODEL), lambda i: (i // tiles_per_seq, 0, 0))


def _table_spec(tiles_per_seq):
    return pl.BlockSpec((ROW_TILE, LANES), lambda i: (i % tiles_per_seq, 0))


def _fwd_in(x, mod, ln_g, ln_b, w_in, q_g, kv_g, w_uq, w_ukv, cos_t, sin_a, sin_b, seq):
    rows = x.shape[0]
    tm = ROW_TILE
    tps = seq // tm

    def body(x_ref, mod_ref, g_ref, b_ref, win_ref, qg_ref, kvg_ref, wuq_ref, wukv_ref, cos_ref, sa_ref, sb_ref,
             x0_ref, h_ref, qkv_ref, lat_ref, qp_ref, kp_ref, vm_ref):
        x0, _, _ = _ln_fwd(x_ref[...], g_ref[...], b_ref[...])
        x0_ref[...] = x0
        h = (x0 * (1.0 + mod_ref[0, 1:2, :]) + mod_ref[0, 0:1, :]).astype(BF16)
        h_ref[...] = h
        proj = _dot(h, win_ref[...])
        qkv_ref[:, :SB_W] = (proj[:, :SB_W] * SB_SCALE).astype(BF16)
        qkv_ref[:, SB_W:] = proj[:, SB_W:3 * SB_W].astype(BF16)
        lat_ref[...] = proj[:, 3 * SB_W:3 * SB_W + Q_RANK + KV_RANK]
        cq = proj[:, 3 * SB_W:3 * SB_W + Q_RANK]
        ckv = proj[:, 3 * SB_W + Q_RANK:3 * SB_W + Q_RANK + KV_RANK]
        kr = proj[:, D_IN_PAD - LANES:]
        cos, sa, sb = cos_ref[...], sa_ref[...], sb_ref[...]
        cqn = (cq * lax.rsqrt(_mean(cq * cq) + RMS_EPS) * qg_ref[...]).astype(BF16)
        q_all = _dot(cqn, wuq_ref[...])
        for hd in range(HEADS):
            sl = slice(hd * LANES, (hd + 1) * LANES)
            qp_ref[:, sl] = _rope(q_all[:, sl], cos, sa, sb).astype(BF16)
        ckvn = (ckv * lax.rsqrt(_mean(ckv * ckv) + RMS_EPS) * kvg_ref[...]).astype(BF16)
        kv = _dot(ckvn, wukv_ref[...])
        kr_rot = _rope(kr, cos, sa, sb)
        for hd in range(HEADS):
            sl = slice(hd * LANES, (hd + 1) * LANES)
            kp_ref[:, sl] = (kv[:, sl] + kr_rot).astype(BF16)
        vm_ref[...] = kv[:, HEADS * LANES:].astype(BF16)

    outs = [(D_MODEL, F32), (D_MODEL, BF16), (3 * SB_W, BF16), (Q_RANK + KV_RANK, F32),
            (HEADS * LANES, BF16), (HEADS * LANES, BF16), (MLA_W, BF16)]
    return pl.pallas_call(
        body, name="fwd_in", grid=(rows // tm,),
        out_shape=tuple(jax.ShapeDtypeStruct((rows, n), dt) for n, dt in outs),
        in_specs=[_row_spec(D_MODEL), _mod_spec(tps), _const_spec((1, D_MODEL)), _const_spec((1, D_MODEL)),
                  _const_spec(w_in.shape), _const_spec((1, Q_RANK)), _const_spec((1, KV_RANK)),
                  _const_spec(w_uq.shape), _const_spec(w_ukv.shape),
                  _table_spec(tps), _table_spec(tps), _table_spec(tps)],
        out_specs=tuple(_row_spec(n) for n, _ in outs),
        compiler_params=_params(("arbitrary",)),
    )(x, mod, ln_g, ln_b, w_in, q_g, kv_g, w_uq, w_ukv, cos_t, sin_a, sin_b)


def _fwd_mix(sb_y, mla_y, x0, mod, w_o, ln_g, ln_b, seq):
    rows = x0.shape[0]
    tm = ROW_TILE
    tps = seq // tm

    def body(sb_ref, ml_ref, x0_ref, mod_ref, wo_ref, g_ref, b_ref, mix_ref, y1_ref, x1_ref, h2_ref):
        mix = _dot(sb_ref[...].astype(BF16), wo_ref[:SB_W, :]) + _dot(ml_ref[...].astype(BF16), wo_ref[SB_W:, :])
        mix_ref[...] = mix
        y1 = ALPHA * x0_ref[...] + (1.0 + mod_ref[0, 2:3, :]) * mix
        y1_ref[...] = y1
        x1, _, _ = _ln_fwd(y1, g_ref[...], b_ref[...])
        x1_ref[...] = x1
        h2_ref[...] = (x1 * (1.0 + mod_ref[0, 4:5, :]) + mod_ref[0, 3:4, :]).astype(BF16)

    outs = [(D_MODEL, F32), (D_MODEL, F32), (D_MODEL, F32), (D_MODEL, BF16)]
    return pl.pallas_call(
        body, name="fwd_mix", grid=(rows // tm,),
        out_shape=tuple(jax.ShapeDtypeStruct((rows, n), dt) for n, dt in outs),
        in_specs=[_row_spec(SB_W), _row_spec(MLA_W), _row_spec(D_MODEL), _mod_spec(tps), _const_spec(w_o.shape),
                  _const_spec((1, D_MODEL)), _const_spec((1, D_MODEL))],
        out_specs=tuple(_row_spec(n) for n, _ in outs),
        compiler_params=_params(("arbitrary",)),
    )(sb_y, mla_y, x0, mod, w_o, ln_g, ln_b)


HALF = 512
SHARD = 1024


def _mlp_weight_specs():
    return [pl.BlockSpec((8, HALF, SHARD), lambda i: (0, 0, 0), pipeline_mode=pl.Buffered(1)),
            pl.BlockSpec((8, HALF, SHARD), lambda i: (0, 1, 0), pipeline_mode=pl.Buffered(1))]


def _fwd_mlp(h2, x1, mod, g_mlp, seq):
    rows = x1.shape[0]
    tm = ROW_TILE
    tps = seq // tm

    def body(h2_ref, x1_ref, mod_ref, wu_ref, wd_ref, u_ref, ff_ref, y2_ref):
        h_lo, h_hi = h2_ref[:, :HALF], h2_ref[:, HALF:]
        ff = jnp.zeros((tm, D_MODEL), F32)
        for chip in range(4):
            u = _dot(h_lo, wu_ref[2 * chip]) + _dot(h_hi, wu_ref[2 * chip + 1])
            u_ref[:, chip * SHARD:(chip + 1) * SHARD] = u.astype(BF16)
            act = jnp.square(jnp.maximum(u, 0.0)).astype(BF16)
            ff = ff + _dot(act[:, :HALF], wd_ref[2 * chip]) + _dot(act[:, HALF:], wd_ref[2 * chip + 1])
        ff_ref[...] = ff
        y2_ref[...] = ALPHA * x1_ref[...] + (1.0 + mod_ref[0, 5:6, :]) * ff

    outs = [(D_FF, BF16), (D_MODEL, F32), (D_MODEL, F32)]
    return pl.pallas_call(
        body, name="fwd_mlp", grid=(rows // tm,),
        out_shape=tuple(jax.ShapeDtypeStruct((rows, n), dt) for n, dt in outs),
        in_specs=[_row_spec(D_MODEL), _row_spec(D_MODEL), _mod_spec(tps)] + _mlp_weight_specs(),
        out_specs=tuple(_row_spec(n) for n, _ in outs),
        compiler_params=_params(("arbitrary",)),
    )(h2, x1, mod, g_mlp, g_mlp)


def _acc_spec(rows=8, cols=D_MODEL):
    return pl.BlockSpec((rows, cols), lambda i: (0, 0))


def _bwd_out(y2, tgt, ff, u, mod, ln_g, ln_b, g_mlp, seq):
    rows = y2.shape[0]
    nb = rows // seq
    tm = ROW_TILE
    tps = seq // tm

    def body(y2_ref, t_ref, ff_ref, u_ref, mod_ref, g_ref, b_ref, wd_ref, dy2_ref, dff_ref, du_ref, acc_ref, dmod_ref):
        i = pl.program_id(0)

        @pl.when(i == 0)
        def _():
            acc_ref[...] = jnp.zeros_like(acc_ref)

        @pl.when(i % tps == 0)
        def _():
            dmod_ref[...] = jnp.zeros_like(dmod_ref)

        g = g_ref[...]
        x2, xhat, rstd = _ln_fwd(y2_ref[...], g, b_ref[...])
        err = x2 - t_ref[...]
        dx2 = err * (1.0 / D_MODEL)
        acc_ref[0:1, :] += _rowsum(dx2 * xhat)
        acc_ref[1:2, :] += _rowsum(dx2)
        acc_ref[2:3, :] += _rowsum(err * err) * (0.5 / D_MODEL)
        dy2 = _ln_bwd(dx2, xhat, rstd, g)
        dy2_ref[...] = dy2
        dmod_ref[0, 5:6, :] += _rowsum(dy2 * ff_ref[...])
        dff = ((1.0 + mod_ref[0, 5:6, :]) * dy2).astype(BF16)
        dff_ref[...] = dff
        for blk in range(8):
            cols = slice(blk * HALF, (blk + 1) * HALF)
            da = _dot(dff, wd_ref[blk], NT)
            du_ref[:, cols] = (da * (2.0 * jnp.maximum(u_ref[:, cols].astype(F32), 0.0))).astype(BF16)

    outs = [(D_MODEL, F32), (D_MODEL, BF16), (D_FF, BF16)]
    return pl.pallas_call(
        body, name="bwd_out", grid=(rows // tm,),
        out_shape=tuple(jax.ShapeDtypeStruct((rows, n), dt) for n, dt in outs)
        + (jax.ShapeDtypeStruct((8, D_MODEL), F32), jax.ShapeDtypeStruct((nb, 8, D_MODEL), F32)),
        in_specs=[_row_spec(D_MODEL), _row_spec(D_MODEL), _row_spec(D_MODEL), _row_spec(D_FF), _mod_spec(tps),
                  _const_spec((1, D_MODEL)), _const_spec((1, D_MODEL)), _mlp_weight_specs()[1]],
        out_specs=tuple(_row_spec(n) for n, _ in outs) + (_acc_spec(), _mod_spec(tps)),
        compiler_params=_params(("arbitrary",)),
    )(y2, tgt, ff, u, mod, ln_g, ln_b, g_mlp)


def _bwd_mid(du, y1, mix, dy2, mod, ln_g, ln_b, g_mlp, w_o, seq):
    rows = y1.shape[0]
    nb = rows // seq
    tm = ROW_TILE
    tps = seq // tm

    def body(du_ref, y1_ref, mix_ref, dy2_ref, mod_ref, g_ref, b_ref, wu_ref, wo_ref,
             dy1_ref, dmix_ref, do_ref, acc_ref, dmod_ref):
        i = pl.program_id(0)

        @pl.when(i == 0)
        def _():
            acc_ref[...] = jnp.zeros_like(acc_ref)

        @pl.when(i % tps == 0)
        def _():
            dmod_ref[...] = jnp.zeros_like(dmod_ref)

        g = g_ref[...]
        x1, xhat, rstd = _ln_fwd(y1_ref[...], g, b_ref[...])
        halves = []
        for half in range(2):
            acc = jnp.zeros((tm, HALF), F32)
            for chip in range(4):
                acc = acc + _dot(du_ref[:, chip * SHARD:(chip + 1) * SHARD], wu_ref[2 * chip + half], NT)
            halves.append(acc)
        dh2 = jnp.concatenate(halves, axis=1)
        dmod_ref[0, 3:4, :] += _rowsum(dh2)
        dmod_ref[0, 4:5, :] += _rowsum(dh2 * x1)
        dx1 = ALPHA * dy2_ref[...] + dh2 * (1.0 + mod_ref[0, 4:5, :])
        acc_ref[0:1, :] += _rowsum(dx1 * xhat)
        acc_ref[1:2, :] += _rowsum(dx1)
        dy1 = _ln_bwd(dx1, xhat, rstd, g)
        dy1_ref[...] = dy1
        dmod_ref[0, 2:3, :] += _rowsum(dy1 * mix_ref[...])
        dmix = ((1.0 + mod_ref[0, 2:3, :]) * dy1).astype(BF16)
        dmix_ref[...] = dmix
        do_ref[...] = _dot(dmix, wo_ref[...], NT)

    outs = [(D_MODEL, F32), (D_MODEL, BF16), (D_MODEL, F32)]
    return pl.pallas_call(
        body, name="bwd_mid", grid=(rows // tm,),
        out_shape=tuple(jax.ShapeDtypeStruct((rows, n), dt) for n, dt in outs)
        + (jax.ShapeDtypeStruct((8, D_MODEL), F32), jax.ShapeDtypeStruct((nb, 8, D_MODEL), F32)),
        in_specs=[_row_spec(D_FF), _row_spec(D_MODEL), _row_spec(D_MODEL), _row_spec(D_MODEL), _mod_spec(tps),
                  _const_spec((1, D_MODEL)), _const_spec((1, D_MODEL)), _mlp_weight_specs()[0],
                  _const_spec(w_o.shape)],
        out_specs=tuple(_row_spec(n) for n, _ in outs) + (_acc_spec(), _mod_spec(tps)),
        compiler_params=_params(("arbitrary",)),
    )(du, y1, mix, dy2, mod, ln_g, ln_b, g_mlp, w_o)


def _bwd_in(dqp, dkp, dvm, dq_sb, dk_sb, dv_sb, lat, x, x0, dy1, mod, ln_g, ln_b, w_in, q_g, kv_g, w_uq, w_ukv,
            cos_t, sin_a, sin_b, seq):
    rows = x.shape[0]
    nb = rows // seq
    tm = ROW_TILE
    tps = seq // tm
    n_lat = Q_RANK + KV_RANK

    def body(dqp_ref, dkp_ref, dvm_ref, dqs_ref, dks_ref, dvs_ref, lat_ref, x_ref, x0_ref, dy1_ref, mod_ref,
             g_ref, b_ref, win_ref, qg_ref, kvg_ref, wuq_ref, wukv_ref, cos_ref, sa_ref, sb_ref,
             dx_ref, dproj_ref, dqall_ref, dkv_ref, latn_ref, acc_ref, accl_ref, dmod_ref):
        i = pl.program_id(0)

        @pl.when(i == 0)
        def _():
            acc_ref[...] = jnp.zeros_like(acc_ref)
            accl_ref[...] = jnp.zeros_like(accl_ref)

        @pl.when(i % tps == 0)
        def _():
            dmod_ref[...] = jnp.zeros_like(dmod_ref)

        cos, sa, sb = cos_ref[...], sa_ref[...], sb_ref[...]
        lane = lax.broadcasted_iota(jnp.int32, (tm, LANES), 1)
        for hd in range(HEADS):
            sl = slice(hd * LANES, (hd + 1) * LANES)
            dqall_ref[:, sl] = _rope_t(dqp_ref[:, sl], cos, sa, sb).astype(BF16)
        dcqn = _dot(dqall_ref[...], wuq_ref[...], NT)
        cq = lat_ref[:, :Q_RANK]
        qg = qg_ref[...]
        rq = lax.rsqrt(_mean(cq * cq) + RMS_EPS)
        cqn = cq * rq
        latn_ref[:, :Q_RANK] = (cqn * qg).astype(BF16)
        accl_ref[0:1, :Q_RANK] += _rowsum(dcqn * cqn)
        dqg = dcqn * qg
        dcq = rq * (dqg - cqn * _mean(dqg * cqn))
        dkr = jnp.zeros((tm, LANES), F32)
        for hd in range(HEADS):
            sl = slice(hd * LANES, (hd + 1) * LANES)
            dk = dkp_ref[:, sl]
            dkr = dkr + dk
            dkv_ref[:, sl] = jnp.where(lane < NOPE, dk, 0.0).astype(BF16)
        dkv_ref[:, HEADS * LANES:] = dvm_ref[...].astype(BF16)
        dckvn = _dot(dkv_ref[...], wukv_ref[...], NT)
        ckv = lat_ref[:, Q_RANK:]
        kvg = kvg_ref[...]
        rkv = lax.rsqrt(_mean(ckv * ckv) + RMS_EPS)
        ckvn = ckv * rkv
        latn_ref[:, Q_RANK:] = (ckvn * kvg).astype(BF16)
        accl_ref[1:2, :KV_RANK] += _rowsum(dckvn * ckvn)
        dkg = dckvn * kvg
        dckv = rkv * (dkg - ckvn * _mean(dkg * ckvn))
        dkr = _rope_t(jnp.where(lane >= NOPE, dkr, 0.0), cos, sa, sb)
        dproj_ref[:, :SB_W] = dqs_ref[...]
        dproj_ref[:, SB_W:2 * SB_W] = dks_ref[...].astype(BF16)
        dproj_ref[:, 2 * SB_W:3 * SB_W] = dvs_ref[...].astype(BF16)
        dproj_ref[:, 3 * SB_W:3 * SB_W + Q_RANK] = dcq.astype(BF16)
        dproj_ref[:, 3 * SB_W + Q_RANK:3 * SB_W + n_lat] = dckv.astype(BF16)
        dproj_ref[:, D_IN_PAD - LANES:] = dkr.astype(BF16)
        dh = _dot(dproj_ref[...], win_ref[...], NT)
        x0 = x0_ref[...]
        dmod_ref[0, 0:1, :] += _rowsum(dh)
        dmod_ref[0, 1:2, :] += _rowsum(dh * x0)
        dx0 = ALPHA * dy1_ref[...] + dh * (1.0 + mod_ref[0, 1:2, :])
        g = g_ref[...]
        _, xhat, rstd = _ln_fwd(x_ref[...], g, b_ref[...])
        acc_ref[0:1, :] += _rowsum(dx0 * xhat)
        acc_ref[1:2, :] += _rowsum(dx0)
        dx_ref[...] = _ln_bwd(dx0, xhat, rstd, g)

    outs = [(D_MODEL, F32), (D_IN_PAD, BF16), (HEADS * LANES, BF16), (HEADS * LANES + MLA_W, BF16), (n_lat, BF16)]
    return pl.pallas_call(
        body, name="bwd_in", grid=(rows // tm,),
        out_shape=tuple(jax.ShapeDtypeStruct((rows, n), dt) for n, dt in outs)
        + (jax.ShapeDtypeStruct((8, D_MODEL), F32), jax.ShapeDtypeStruct((8, Q_RANK), F32),
           jax.ShapeDtypeStruct((nb, 8, D_MODEL), F32)),
        in_specs=[_row_spec(HEADS * LANES), _row_spec(HEADS * LANES), _row_spec(MLA_W),
                  _row_spec(SB_W), _row_spec(SB_W), _row_spec(SB_W), _row_spec(n_lat),
                  _row_spec(D_MODEL), _row_spec(D_MODEL), _row_spec(D_MODEL), _mod_spec(tps),
                  _const_spec((1, D_MODEL)), _const_spec((1, D_MODEL)), _const_spec(w_in.shape),
                  _const_spec((1, Q_RANK)), _const_spec((1, KV_RANK)), _const_spec(w_uq.shape),
                  _const_spec(w_ukv.shape), _table_spec(tps), _table_spec(tps), _table_spec(tps)],
        out_specs=tuple(_row_spec(n) for n, _ in outs) + (_acc_spec(), _acc_spec(8, Q_RANK), _mod_spec(tps)),
        compiler_params=_params(("arbitrary",)),
    )(dqp, dkp, dvm, dq_sb, dk_sb, dv_sb, lat, x, x0, dy1, mod, ln_g, ln_b, w_in, q_g, kv_g, w_uq, w_ukv,
      cos_t, sin_a, sin_b)


def _wgrad(a, b, name, pre=None, tm=512, tn=1024, tk=2048):
    rows, m = a.shape
    n = b.shape[1]
    tm, tn, tk = min(tm, m), min(tn, n), min(tk, rows)
    if m % tm:
        tm = m
    if n % tn:
        tn = n

    def body(a_ref, b_ref, o_ref):
        @pl.when(pl.program_id(2) == 0)
        def _():
            o_ref[...] = jnp.zeros_like(o_ref)

        av = a_ref[...]
        if pre == "relu2":
            av = jnp.square(jnp.maximum(av.astype(F32), 0.0))
        o_ref[...] += _dot(av.astype(BF16), b_ref[...].astype(BF16), TN)

    return pl.pallas_call(
        body, name=name, grid=(m // tm, n // tn, rows // tk),
        out_shape=jax.ShapeDtypeStruct((m, n), F32),
        in_specs=[pl.BlockSpec((tk, tm), lambda i, j, k: (k, i)), pl.BlockSpec((tk, tn), lambda i, j, k: (k, j))],
        out_specs=pl.BlockSpec((tm, tn), lambda i, j, k: (i, j)),
        compiler_params=_params(("arbitrary", "arbitrary", "arbitrary")),
    )(a, b)


def _wgrad_packed(a, b, name, tm, block_of, row_block, pre=None, into=None, tk=2048):
    rows, m = a.shape
    n = b.shape[1]
    tk = min(tk, rows)
    shape = jax.ShapeDtypeStruct((8, GROUP_MLP[0], PACK_COLS), F32)

    def body(a_ref, b_ref, *rest):
        o_ref = rest[-1]

        @pl.when(pl.program_id(2) == 0)
        def _():
            o_ref[...] = jnp.zeros_like(o_ref)

        av = a_ref[...]
        if pre == "relu2":
            av = jnp.square(jnp.maximum(av.astype(F32), 0.0))
        o_ref[0] += _dot(av.astype(BF16), b_ref[...].astype(BF16), TN)

    in_specs = [pl.BlockSpec((tk, tm), lambda i, j, k: (k, i)), pl.BlockSpec((tk, SHARD), lambda i, j, k: (k, j))]
    operands = [a, b]
    if into is not None:
        in_specs.append(pl.BlockSpec(memory_space=pl.ANY))
        operands.append(into)
    return pl.pallas_call(
        body, name=name, grid=(m // tm, n // SHARD, rows // tk), out_shape=shape,
        in_specs=in_specs,
        out_specs=pl.BlockSpec((1, tm, SHARD), lambda i, j, k: (block_of(i, j), row_block, 0)),
        input_output_aliases={} if into is None else {2: 0},
        compiler_params=_params(("arbitrary", "arbitrary", "arbitrary")),
    )(*operands)


def _pair(pp):
    return slice(pp * LANES, (pp + 1) * LANES)


def _head_mask(lane, hh):
    return jnp.where((lane >= 64) if hh else (lane < 64), 1.0, 0.0).astype(BF16)


def _tri(t, kind):
    r = lax.broadcasted_iota(jnp.int32, (t, t), 0)
    c = lax.broadcasted_iota(jnp.int32, (t, t), 1)
    one = jnp.where(r > c if kind == "suffix" else r < c, 1.0, 0.0).astype(BF16)
    return jnp.concatenate([one, one], axis=0)


def _split_dot(v, tri2):
    hi = v.astype(BF16)
    lo = (v - hi.astype(F32)).astype(BF16)
    return _dot(jnp.concatenate([hi, lo], axis=1), tri2)


def _sb_logits(z, valid):
    log_keep = -(jnp.maximum(z, 0.0) + jnp.log(1.0 + jnp.exp(-jnp.abs(z))))
    log_beta = z + log_keep
    if valid is not None:
        log_keep = jnp.where(valid, log_keep, 0.0)
    return log_keep, log_beta


def _attention_call(body, ex, name, grid, operands, in_specs, out_shapes, out_specs):
    n_in, n_out = len(operands), len(out_shapes)
    total = grid[0] * grid[1] * grid[2]
    any_spec = pl.BlockSpec(memory_space=pl.ANY)

    def carrier(*refs):
        ins, outs, (start, middle, finish) = _carried(ex, refs, n_in, n_out)
        step = (pl.program_id(0) * grid[1] + pl.program_id(1)) * grid[2] + pl.program_id(2)
        pl.when(step == 0)(start)
        pl.when(step == total // 2)(middle)
        body(*ins, *outs)
        pl.when(step == total - 1)(finish)

    carried = ex is not None
    return pl.pallas_call(
        carrier if carried else body, name=name, grid=grid,
        out_shape=tuple(out_shapes) + ((ex.out_shape,) if carried else ()),
        in_specs=list(in_specs) + ([any_spec] if carried else []),
        out_specs=tuple(out_specs) + ((any_spec,) if carried else ()),
        scratch_shapes=ex.scratch if carried else [],
        compiler_params=_params(("arbitrary", "arbitrary", "arbitrary")),
    )(*operands, *([ex.operand] if carried else []))


def _sb_fwd(qkv, seq, ex=None):
    rows = qkv.shape[0]
    nb = rows // seq
    t = min(ATTN_TILE, seq)
    nq = seq // t
    ap = ATTN_PAIRS["sb_fwd"]
    width = ap * LANES
    groups = SB_W // width
    hds = [(pp, hh) for pp in range(ap) for hh in range(2)]

    def body(q_ref, k_ref, v_ref, tri_ref, o_ref, car_ref):
        i = pl.program_id(2)
        lane = lax.broadcasted_iota(jnp.int32, (t, LANES), 1)
        row = lax.broadcasted_iota(jnp.int32, (t, t), 0)
        col = lax.broadcasted_iota(jnp.int32, (t, t), 1)
        strict = col < row
        tri = tri_ref[...]
        masks = [_head_mask(lane, hh) for hh in range(2)]
        qms = [q_ref[:, _pair(pp)] * masks[hh] for pp, hh in hds]

        def step(kb, carry, valid):
            c_sums, accs, cars = carry
            accs, cars = list(accs), list(cars)
            start = pl.multiple_of(kb * t, t)
            kss = [k_ref[pl.ds(start, t), _pair(pp)] for pp in range(ap)]
            vss = [v_ref[pl.ds(start, t), _pair(pp)] for pp in range(ap)]
            zs = [_dot(qms[n], kss[pp], NT) for n, (pp, _) in enumerate(hds)]
            logs = [_sb_logits(z, valid) for z in zs]
            sufs = [_split_dot(lg[0], tri) for lg in logs]
            new_sums = []
            for n, (pp, hh) in enumerate(hds):
                log_keep, log_beta = logs[n]
                w = jnp.exp(log_beta + sufs[n] + c_sums[n])
                if valid is not None:
                    w = jnp.where(valid, w, 0.0)
                accs[pp] = accs[pp] + _dot(w.astype(BF16), vss[pp] * masks[hh])
                cars[pp] = jnp.where(lane == hh * 8 + kb, c_sums[n], cars[pp])
                new_sums.append(c_sums[n] + jnp.sum(log_keep, axis=1, keepdims=True))
            return tuple(new_sums), tuple(accs), tuple(cars)

        zeros = tuple(jnp.zeros((t, LANES), F32) for _ in range(ap))
        carry = step(i, (tuple(jnp.zeros((t, 1), F32) for _ in hds), zeros, zeros), strict)
        _, accs, cars = lax.fori_loop(0, i, lambda j, cr: step(i - 1 - j, cr, None), carry)
        for pp in range(ap):
            o_ref[:, _pair(pp)] = accs[pp].astype(BF16)
            car_ref[:, _pair(pp)] = cars[pp]

    qspec = pl.BlockSpec((t, width), lambda b, p, i: (b * nq + i, p))
    return _attention_call(
        body, ex, "sb_fwd", (nb, groups, nq),
        [qkv, qkv, qkv, _tri(t, "suffix")],
        [qspec,
         pl.BlockSpec((seq, width), lambda b, p, i: (b, groups + p)),
         pl.BlockSpec((seq, width), lambda b, p, i: (b, 2 * groups + p)),
         _const_spec((2 * t, t))],
        [jax.ShapeDtypeStruct((rows, SB_W), BF16), jax.ShapeDtypeStruct((rows, SB_W), F32)],
        [qspec, qspec])


def _sb_bwd(qkv, d_out, cars, seq, ex=None):
    rows = qkv.shape[0]
    nb = rows // seq
    t = min(ATTN_TILE, seq)
    nq = seq // t
    ap = ATTN_PAIRS["sb_bwd"]
    width = ap * LANES
    groups = SB_W // width
    hds = [(pp, hh) for pp in range(ap) for hh in range(2)]

    def body(q_ref, k_ref, v_ref, do_ref, car_ref, tri_ref, pre_ref, dq_ref, dk_ref, dv_ref):
        i = pl.program_id(2)

        @pl.when(i == 0)
        def _():
            dk_ref[...] = jnp.zeros_like(dk_ref)
            dv_ref[...] = jnp.zeros_like(dv_ref)

        lane = lax.broadcasted_iota(jnp.int32, (t, LANES), 1)
        row = lax.broadcasted_iota(jnp.int32, (t, t), 0)
        col = lax.broadcasted_iota(jnp.int32, (t, t), 1)
        strict = col < row
        tri, pre = tri_ref[...], pre_ref[...]
        masks = [_head_mask(lane, hh) for hh in range(2)]
        qms = [q_ref[:, _pair(pp)] * masks[hh] for pp, hh in hds]
        doms = [do_ref[:, _pair(pp)].astype(BF16) * masks[hh] for pp, hh in hds]
        cars = [car_ref[:, _pair(pp)] for pp in range(ap)]

        def step(kb, carry, valid):
            g_pres, dqs = carry
            dqs = list(dqs)
            start = pl.multiple_of(kb * t, t)
            kss = [k_ref[pl.ds(start, t), _pair(pp)] for pp in range(ap)]
            vss = [v_ref[pl.ds(start, t), _pair(pp)] for pp in range(ap)]
            zs = [_dot(qms[n], kss[pp], NT) for n, (pp, _) in enumerate(hds)]
            dws = [_dot(doms[n], vss[pp], NT) for n, (pp, _) in enumerate(hds)]
            logs = [_sb_logits(z, valid) for z in zs]
            sufs = [_split_dot(lg[0], tri) for lg in logs]
            ws, gs = [], []
            for n, (pp, hh) in enumerate(hds):
                c_sum = jnp.sum(jnp.where(lane == hh * 8 + kb, cars[pp], 0.0), axis=1, keepdims=True)
                w = jnp.exp(logs[n][1] + sufs[n] + c_sum)
                if valid is not None:
                    w = jnp.where(valid, w, 0.0)
                ws.append(w)
                gs.append(dws[n] * w)
            befores = [g_pres[n] + _split_dot(gs[n], pre) for n in range(len(hds))]
            for pp in range(ap):
                dv_ref[pl.ds(start, t), _pair(pp)] += (_dot(ws[2 * pp].astype(BF16), doms[2 * pp], TN)
                                                       + _dot(ws[2 * pp + 1].astype(BF16), doms[2 * pp + 1], TN))
            dzbs = []
            for n in range(len(hds)):
                dz = gs[n] * jnp.exp(logs[n][0]) - jnp.exp(logs[n][1]) * befores[n]
                if valid is not None:
                    dz = jnp.where(valid, dz, 0.0)
                dzbs.append(dz.astype(BF16))
            for pp in range(ap):
                a, b = 2 * pp, 2 * pp + 1
                dqs[pp] = dqs[pp] + _dot(dzbs[a], kss[pp] * masks[0]) + _dot(dzbs[b], kss[pp] * masks[1])
                dk_ref[pl.ds(start, t), _pair(pp)] += _dot(dzbs[a], qms[a], TN) + _dot(dzbs[b], qms[b], TN)
            new_pres = [g_pres[n] + jnp.sum(gs[n], axis=1, keepdims=True) for n in range(len(hds))]
            return tuple(new_pres), tuple(dqs)

        init = (tuple(jnp.zeros((t, 1), F32) for _ in hds), tuple(jnp.zeros((t, LANES), F32) for _ in range(ap)))
        carry = lax.fori_loop(0, i, lambda kb, cr: step(kb, cr, None), init)
        _, dqs = step(i, carry, strict)
        for pp in range(ap):
            dq_ref[:, _pair(pp)] = (dqs[pp] * SB_SCALE).astype(BF16)

    qspec = pl.BlockSpec((t, width), lambda b, p, i: (b * nq + i, p))
    kspec_out = pl.BlockSpec((seq, width), lambda b, p, i: (b, p))
    return _attention_call(
        body, ex, "sb_bwd", (nb, groups, nq),
        [qkv, qkv, qkv, d_out, cars, _tri(t, "suffix"), _tri(t, "prefix")],
        [qspec,
         pl.BlockSpec((seq, width), lambda b, p, i: (b, groups + p)),
         pl.BlockSpec((seq, width), lambda b, p, i: (b, 2 * groups + p)),
         qspec, qspec, _const_spec((2 * t, t)), _const_spec((2 * t, t))],
        [jax.ShapeDtypeStruct((rows, SB_W), BF16), jax.ShapeDtypeStruct((rows, SB_W), F32),
         jax.ShapeDtypeStruct((rows, SB_W), F32)],
        [qspec, kspec_out, kspec_out])


def _mla_scores(qh, ks, allowed):
    s = _dot(qh, ks, NT) * MLA_SCALE
    if allowed is not None:
        s = jnp.where(allowed, s, jnp.finfo(F32).min)
    return s


def _mla_fwd(qp, kp, vm, seq, ex=None, chunk=64):
    rows = qp.shape[0]
    nb = rows // seq
    t = min(ATTN_TILE, seq)
    nq = seq // t
    shift = int(math.log2(chunk))
    ap = ATTN_PAIRS["mla_fwd"]
    width = ap * LANES
    groups = MLA_W // width
    hds = [(pp, hh) for pp in range(ap) for hh in range(2)]

    def body(q_ref, k_ref, v_ref, o_ref, lse_ref):
        i = pl.program_id(2)
        lane = lax.broadcasted_iota(jnp.int32, (t, LANES), 1)
        row = lax.broadcasted_iota(jnp.int32, (t, t), 0)
        col = lax.broadcasted_iota(jnp.int32, (t, t), 1)
        allowed_diag = jnp.right_shift(col, shift) <= jnp.right_shift(row, shift)
        masks = [_head_mask(lane, hh) for hh in range(2)]
        qhs = [q_ref[:, _pair(n)] for n in range(len(hds))]

        def step(kb, carry, allowed):
            start = pl.multiple_of(kb * t, t)
            vss = [v_ref[pl.ds(start, t), _pair(pp)] for pp in range(ap)]
            scores = [_mla_scores(qhs[n], k_ref[pl.ds(start, t), _pair(n)], allowed) for n in range(len(hds))]
            new = []
            for n, (pp, hh) in enumerate(hds):
                m_run, l_run, acc = carry[n]
                s = scores[n]
                m_new = jnp.maximum(m_run, jnp.max(s, axis=1, keepdims=True))
                p = jnp.exp(s - m_new)
                scale = jnp.exp(m_run - m_new)
                l_run = scale * l_run + jnp.sum(p, axis=1, keepdims=True)
                acc = scale * acc + _dot(p.astype(BF16), vss[pp] * masks[hh])
                new.append((m_new, l_run, acc))
            return tuple(new)

        init = (jnp.full((t, 1), jnp.finfo(F32).min, F32), jnp.zeros((t, 1), F32), jnp.zeros((t, LANES), F32))
        carry = step(i, tuple(init for _ in hds), allowed_diag)
        carry = lax.fori_loop(0, i, lambda kb, cr: step(kb, cr, None), carry)
        for pp in range(ap):
            out = jnp.zeros((t, LANES), F32)
            lses = jnp.zeros((t, LANES), F32)
            for hh in range(2):
                m_run, l_run, acc = carry[2 * pp + hh]
                out = out + acc / l_run
                lses = jnp.where(lane == hh, m_run + jnp.log(l_run), lses)
            o_ref[:, _pair(pp)] = out
            lse_ref[:, _pair(pp)] = lses

    ospec = pl.BlockSpec((t, width), lambda b, p, i: (b * nq + i, p))
    return _attention_call(
        body, ex, "mla_fwd", (nb, groups, nq), [qp, kp, vm],
        [pl.BlockSpec((t, 2 * width), lambda b, p, i: (b * nq + i, p)),
         pl.BlockSpec((seq, 2 * width), lambda b, p, i: (b, p)),
         pl.BlockSpec((seq, width), lambda b, p, i: (b, p))],
        [jax.ShapeDtypeStruct((rows, MLA_W), F32), jax.ShapeDtypeStruct((rows, MLA_W), F32)],
        [ospec, ospec])


def _mla_bwd(qp, kp, vm, d_out, out, lse, seq, ex=None, chunk=64):
    rows = qp.shape[0]
    nb = rows // seq
    t = min(ATTN_TILE, seq)
    nq = seq // t
    shift = int(math.log2(chunk))
    ap = ATTN_PAIRS["mla_bwd"]
    width = ap * LANES
    groups = MLA_W // width
    hds = [(pp, hh) for pp in range(ap) for hh in range(2)]
    nh = len(hds)

    def body(q_ref, k_ref, v_ref, do_ref, o_ref, lse_ref, dq_ref, dk_ref, dv_ref):
        i = pl.program_id(2)

        @pl.when(i == 0)
        def _():
            dk_ref[...] = jnp.zeros_like(dk_ref)
            dv_ref[...] = jnp.zeros_like(dv_ref)

        lane = lax.broadcasted_iota(jnp.int32, (t, LANES), 1)
        row = lax.broadcasted_iota(jnp.int32, (t, t), 0)
        col = lax.broadcasted_iota(jnp.int32, (t, t), 1)
        allowed_diag = jnp.right_shift(col, shift) <= jnp.right_shift(row, shift)
        qhs = [q_ref[:, _pair(n)] for n in range(nh)]
        doms, deltas, lse_hs = [], [], []
        for pp, hh in hds:
            do = do_ref[:, _pair(pp)]
            d_o = do * o_ref[:, _pair(pp)]
            doms.append(do.astype(BF16) * _head_mask(lane, hh))
            deltas.append(jnp.sum(jnp.where((lane >= 64) if hh else (lane < 64), d_o, 0.0), axis=1, keepdims=True))
            lse_hs.append(jnp.sum(jnp.where(lane == hh, lse_ref[:, _pair(pp)], 0.0), axis=1, keepdims=True))

        def step(kb, dqs, allowed):
            start = pl.multiple_of(kb * t, t)
            vss = [v_ref[pl.ds(start, t), _pair(pp)] for pp in range(ap)]
            kss = [k_ref[pl.ds(start, t), _pair(n)] for n in range(nh)]
            scores = [_mla_scores(qhs[n], kss[n], allowed) for n in range(nh)]
            dps = [_dot(doms[n], vss[pp], NT) for n, (pp, _) in enumerate(hds)]
            ps = [jnp.exp(scores[n] - lse_hs[n]) for n in range(nh)]
            dss = [(ps[n] * (dps[n] - deltas[n]) * MLA_SCALE).astype(BF16) for n in range(nh)]
            for pp in range(ap):
                a, b = 2 * pp, 2 * pp + 1
                dv_ref[pl.ds(start, t), _pair(pp)] += (_dot(ps[a].astype(BF16), doms[a], TN)
                                                       + _dot(ps[b].astype(BF16), doms[b], TN))
            for n in range(nh):
                dk_ref[pl.ds(start, t), _pair(n)] += _dot(dss[n], qhs[n], TN)
            return tuple(dqs[n] + _dot(dss[n], kss[n]) for n in range(nh))

        dqs = lax.fori_loop(0, i, lambda kb, cr: step(kb, cr, None),
                            tuple(jnp.zeros((t, LANES), F32) for _ in range(nh)))
        dqs = step(i, dqs, allowed_diag)
        for n in range(nh):
            dq_ref[:, _pair(n)] = dqs[n]

    ospec = pl.BlockSpec((t, width), lambda b, p, i: (b * nq + i, p))
    return _attention_call(
        body, ex, "mla_bwd", (nb, groups, nq), [qp, kp, vm, d_out, out, lse],
        [pl.BlockSpec((t, 2 * width), lambda b, p, i: (b * nq + i, p)),
         pl.BlockSpec((seq, 2 * width), lambda b, p, i: (b, p)),
         pl.BlockSpec((seq, width), lambda b, p, i: (b, p)),
         pl.BlockSpec((t, width), lambda b, p, i: (b * nq + i, groups + p)),
         ospec, ospec],
        [jax.ShapeDtypeStruct((rows, HEADS * LANES), F32), jax.ShapeDtypeStruct((rows, HEADS * LANES), F32),
         jax.ShapeDtypeStruct((rows, MLA_W), F32)],
        [pl.BlockSpec((t, 2 * width), lambda b, p, i: (b * nq + i, p)),
         pl.BlockSpec((seq, 2 * width), lambda b, p, i: (b, p)),
         pl.BlockSpec((seq, width), lambda b, p, i: (b, p))])


PACK_COLS = 1024
PACK_ALIGN = 16
GROUP_IN = (384, ((1024, 552, 1), (384, 192, 1), (256, 256, 1)))
GROUP_MLP = (1152, ((1024, 1024, 1), (1024, 1024, 0), (256, 1024, 0)))


def _pack_rows(r, c):
    return (r // 2) * c // PACK_COLS


def _slot_rows(r, c):
    return -(-_pack_rows(r, c) // PACK_ALIGN) * PACK_ALIGN


def _join_slots(parts, group):
    total, weights = group
    padded = [jnp.pad(p, ((0, 0), (0, _slot_rows(r, c) - p.shape[1]), (0, 0))) for p, (r, c, _) in zip(parts, weights)]
    used = sum(_slot_rows(r, c) for r, c, _ in weights)
    if total > used:
        padded.append(jnp.zeros((parts[0].shape[0], total - used, PACK_COLS), parts[0].dtype))
    return jnp.concatenate(padded, axis=1)


def _split_slots(packed, group):
    out, at = [], 0
    for r, c, _ in group[1]:
        out.append(packed[:, at:at + _pack_rows(r, c), :])
        at += _slot_rows(r, c)
    return out


def _pack_halves(shards, group):
    return _join_slots([s.reshape(2, _pack_rows(r, c), PACK_COLS) for s, (r, c, _) in zip(shards, group[1])], group)


def _unpack_halves(packed, group):
    return [p.reshape(r, c) for p, (r, c, _) in zip(_split_slots(packed, group), group[1])]


def _unpack_full(gathered, group):
    out = []
    for p, (r, c, axis) in zip(_split_slots(gathered, group), group[1]):
        shards = p.reshape(4, r, c)
        out.append(shards.reshape(4 * r, c) if axis == 0 else jnp.moveaxis(shards, 0, 1).reshape(r, 4 * c))
    return out


def _pack_full(grads, group):
    parts = []
    for gr, (r, c, axis) in zip(grads, group[1]):
        shards = gr.reshape(4, r, c) if axis == 0 else jnp.moveaxis(gr.reshape(r, 4, c), 1, 0)
        parts.append(shards.reshape(8, _pack_rows(r, c), PACK_COLS))
    return _join_slots(parts, group)


def _pad_w_in(w_in):
    z = jnp.zeros((D_MODEL, 1), w_in.dtype)
    return jnp.concatenate([w_in[:, :2176], jnp.tile(z, (1, 64)), w_in[:, 2176:], jnp.tile(z, (1, 32))], axis=1)


def _unpad_w_in(g):
    return jnp.concatenate([g[:, :2176], g[:, 2240:2272]], axis=1)


def _pad_heads(w, used):
    k = w.shape[0]
    w3 = w.reshape(k, HEADS, used)
    return jnp.pad(w3, ((0, 0), (0, 0), (0, LANES - used))).reshape(k, HEADS * LANES)


def _unpad_heads(g, used):
    k = g.shape[0]
    return g.reshape(k, HEADS, LANES)[:, :, :used].reshape(k, HEADS * used)


def _rope_tables(seq):
    inv_freq = 1.0 / (ROPE_BASE ** (jnp.arange(0, ROPE, 2, dtype=F32) / ROPE))
    ang = jnp.arange(seq, dtype=F32)[:, None] * inv_freq[None, :]
    cos, sin = jnp.cos(ang), jnp.sin(ang)
    one, zero = jnp.ones((seq, NOPE), F32), jnp.zeros((seq, NOPE), F32)
    z16, z32 = jnp.zeros((seq, 16), F32), jnp.zeros((seq, 32), F32)
    cos_t = jnp.concatenate([one, cos, cos, jnp.ones((seq, 32), F32)], axis=1)
    sin_a = jnp.concatenate([zero, -sin, z16, z32], axis=1)
    sin_b = jnp.concatenate([zero, z16, sin, z32], axis=1)
    return cos_t, sin_a, sin_b


SMALL = (("ln_in_g", 1024), ("ln_in_b", 1024), ("b_ada", 6144), ("q_norm_g", 384), ("kv_norm_g", 256),
         ("ln1_g", 1024), ("ln1_b", 1024), ("ln2_g", 1024), ("ln2_b", 1024))
SMALL_TOTAL = sum(n for _, n in SMALL)
SMALL_ROWS = -(-SMALL_TOTAL // LANES // 8) * 8


def _pack_small(vals):
    flat = jnp.concatenate([v.reshape(-1) for v in vals])
    return jnp.pad(flat, (0, SMALL_ROWS * LANES - SMALL_TOTAL)).reshape(SMALL_ROWS, LANES)


def _unpack_small(packed, like):
    flat, out, at = packed.reshape(-1), [], 0
    for (_, n), ref in zip(SMALL, like):
        out.append(flat[at:at + n].reshape(ref.shape))
        at += n
    return out


def kernel(x, c, ln_in_g, ln_in_b, w_ada, b_ada, w_in, q_norm_g, kv_norm_g, w_uq, w_ukv, w_o, ln1_g, ln1_b, w_up, w_down, ln2_g, ln2_b, loss_target, m_ln_in_g, m_ln_in_b, m_w_ada, m_b_ada, m_w_in, m_q_norm_g, m_kv_norm_g, m_w_uq, m_w_ukv, m_w_o, m_ln1_g, m_ln1_b, m_w_up, m_w_down, m_ln2_g, m_ln2_b, v_ln_in_g, v_ln_in_b, v_w_ada, v_b_ada, v_w_in, v_q_norm_g, v_kv_norm_g, v_w_uq, v_w_ukv, v_w_o, v_ln1_g, v_ln1_b, v_w_up, v_w_down, v_ln2_g, v_ln2_b):
    nb, seq, _ = x.shape
    rows = nb * seq
    ix, iy, ic = lax.axis_index("x"), lax.axis_index("y"), lax.axis_index("c")
    chip = 2 * ix + iy
    dev = 2 * chip + ic

    def my_half(shards, group):
        packed = _pack_halves([s.astype(BF16) for s in shards], group)
        return lax.dynamic_index_in_dim(packed, ic, 0, keepdims=False)

    f_in, f_uq, f_ukv = _unpack_full(_gather8(my_half([w_in[0], w_uq[0], w_ukv[0]], GROUP_IN), "gather_w_in"),
                                     GROUP_IN)
    late_weights = _gather_exchange(my_half([w_up[0], w_down[0], w_o[0]], GROUP_MLP))
    w_in_p = _pad_w_in(f_in)
    uq3 = f_uq.reshape(Q_RANK, HEADS, NOPE + ROPE)
    w_uq_p = jnp.pad(uq3, ((0, 0), (0, 0), (0, LANES - NOPE - ROPE))).reshape(Q_RANK, HEADS * LANES)
    w_ukv_p = jnp.concatenate([_pad_heads(f_ukv[:, :HEADS * NOPE], NOPE), f_ukv[:, HEADS * NOPE:]], axis=1)

    n_all = 8 * nb
    c_all = _gather8(c.reshape(-1, LANES), "gather_c").reshape(n_all, D_MODEL)
    ada_cols = w_ada.shape[2]
    b_sh = lax.dynamic_slice_in_dim(b_ada, chip * ada_cols, ada_cols, axis=1)
    mod_sh = _ada_fwd(c_all, w_ada[0], b_sh)
    mod_g = _gather8(mod_sh, "gather_mod")[0::2]
    mod_all = jnp.moveaxis(mod_g, 0, 1).reshape(n_all, N_MOD * D_MODEL)
    mod_mine = lax.dynamic_slice_in_dim(mod_all, dev * nb, nb, axis=0).reshape(nb, N_MOD, D_MODEL)
    mod = jnp.pad(mod_mine, ((0, 0), (0, 8 - N_MOD), (0, 0)))

    cos_t, sin_a, sin_b = _rope_tables(seq)
    row2 = lambda v: v.reshape(1, -1)

    x2d = x.reshape(rows, D_MODEL)
    x0, h, qkv, lat, qp, kp, vm = _fwd_in(x2d, mod, row2(ln_in_g), row2(ln_in_b), w_in_p, q_norm_g, kv_norm_g,
                                          w_uq_p, w_ukv_p, cos_t, sin_a, sin_b, seq)
    sb_y, cars, g_mlp = _sb_fwd(qkv, seq, late_weights)
    f_o = _split_slots(g_mlp, GROUP_MLP)[2].reshape(D_MODEL, D_MODEL)
    mla_y, lse = _mla_fwd(qp, kp, vm, seq)
    mix, y1, x1, h2 = _fwd_mix(sb_y, mla_y, x0, mod, f_o, ln1_g, ln1_b, seq)
    u, ff, y2 = _fwd_mlp(h2, x1, mod, g_mlp, seq)

    dy2, dff, du, acc2, dmod_a = _bwd_out(y2, loss_target.reshape(rows, D_MODEL), ff, u, mod, ln2_g, ln2_b, g_mlp, seq)
    dy1, dmix, d_attn, acc1, dmod_b = _bwd_mid(du, y1, mix, dy2, mod, ln1_g, ln1_b, g_mlp, f_o, seq)
    c_idx = ic.reshape(1).astype(jnp.int32)
    blocks_mlp = _wgrad_packed(h2, du, "wgrad_up", HALF, lambda i, j: 2 * j + i, 0)
    blocks_mlp = _wgrad_packed(u, dff, "wgrad_down", HALF, lambda i, j: i, 1, pre="relu2", into=blocks_mlp)
    blocks_mlp = _wgrad_packed(sb_y, dmix, "wgrad_o_sb", LANES, lambda i, j: i, 8, into=blocks_mlp)
    blocks_mlp = _wgrad_packed(mla_y, dmix, "wgrad_o_mla", LANES, lambda i, j: 4 + i, 8, into=blocks_mlp)
    dq_sb, dk_sb, dv_sb, sibling_mlp = _sb_bwd(qkv, d_attn, cars, seq, _swap_cores_exchange(blocks_mlp))
    part_mlp, part_mlp_bf = _add_pairs(blocks_mlp, sibling_mlp, c_idx, "grad_add_cores_mlp")
    dqp, dkp, dvm, chips_mlp = _mla_bwd(qp, kp, vm, d_attn, mla_y, lse, seq, _scatter_chips_exchange(part_mlp_bf))
    grad_x, dproj, dqall, dkv, latn, acc0, acc_lat, dmod_c = _bwd_in(
        dqp, dkp, dvm, dq_sb, dk_sb, dv_sb, lat, x2d, x0, dy1, mod, row2(ln_in_g), row2(ln_in_b), w_in_p,
        q_norm_g, kv_norm_g, w_uq_p, w_ukv_p, cos_t, sin_a, sin_b, seq)

    g_in = _unpad_w_in(_wgrad(h, dproj, "wgrad_in", tn=768))
    g_uq = _unpad_heads(_wgrad(latn[:, :Q_RANK], dqall, "wgrad_uq"), NOPE + ROPE)
    g_ukv_p = _wgrad(latn[:, Q_RANK:], dkv, "wgrad_ukv", tn=512)
    g_ukv = jnp.concatenate([_unpad_heads(g_ukv_p[:, :HEADS * LANES], NOPE), g_ukv_p[:, HEADS * LANES:]], axis=1)
    blocks_in = _pack_full([g_in, g_uq, g_ukv], GROUP_IN)
    sibling_in = _run_exchange(_swap_cores_exchange(blocks_in), "grads_in_to_sibling")
    part_in, part_in_bf = _add_pairs(blocks_in, sibling_in, c_idx, "grad_add_cores_in")
    chips_in = _run_exchange(_scatter_chips_exchange(part_in_bf), "grads_in_to_chips")

    def own(part):
        return lax.dynamic_index_in_dim(part, chip, 0, keepdims=False)

    half = jnp.concatenate([_add_chips(own(part_in), chips_in, "grad_add_chips_in"),
                            _add_chips(own(part_mlp), chips_mlp, "grad_add_chips_mlp")], axis=0)
    both = _run_exchange(_pair_exchange(half), "grads_halves")
    gs_in, gs_uq, gs_ukv = _unpack_halves(both[:, :GROUP_IN[0]], GROUP_IN)
    gs_up, gs_down, gs_o = _unpack_halves(both[:, GROUP_IN[0]:], GROUP_MLP)

    dmod = (dmod_a + dmod_b + dmod_c)[:, :N_MOD, :]
    small_part = _pack_small([acc0[0], acc0[1], jnp.zeros((N_MOD * D_MODEL,), F32), acc_lat[0, :Q_RANK],
                              acc_lat[1, :KV_RANK], acc1[0], acc1[1], acc2[0], acc2[1]])
    n_sum = SMALL_ROWS + D_MODEL // LANES
    payload = jnp.concatenate([small_part, acc2[2].reshape(-1, LANES), dmod.reshape(-1, LANES)], axis=0)
    gathered = _gather8(payload, "gather_small")
    small_sum = _sum_lead(gathered[:, :n_sum, :], "sum_small")
    loss = jnp.sum(small_sum[SMALL_ROWS:])
    dmod_all = gathered[:, n_sum:, :].reshape(n_all, N_MOD * D_MODEL)
    g_b_ada = _sum_lead(dmod_all.reshape(n_all, N_MOD * D_MODEL // LANES, LANES), "sum_b_ada").reshape(1, -1)
    dmod_sh = lax.dynamic_slice_in_dim(dmod_all, chip * ada_cols, ada_cols, axis=1)
    g_w_ada = _ada_bwd(c_all, dmod_sh)

    small_like = [ln_in_g, ln_in_b, b_ada, q_norm_g, kv_norm_g, ln1_g, ln1_b, ln2_g, ln2_b]
    small_grads = _unpack_small(small_sum, small_like)
    small_grads[2] = g_b_ada

    big_w = {"w_ada": (w_ada[0], g_w_ada, m_w_ada[0], v_w_ada[0]), "w_in": (w_in[0], gs_in, m_w_in[0], v_w_in[0]),
             "w_uq": (w_uq[0], gs_uq, m_w_uq[0], v_w_uq[0]), "w_ukv": (w_ukv[0], gs_ukv, m_w_ukv[0], v_w_ukv[0]),
             "w_o": (w_o[0], gs_o, m_w_o[0], v_w_o[0]), "w_up": (w_up[0], gs_up, m_w_up[0], v_w_up[0]),
             "w_down": (w_down[0], gs_down, m_w_down[0], v_w_down[0])}
    res = {}
    for name, (w, g, m, v) in big_w.items():
        d, mn, vn = _adamw(w, g, m, v, "adamw_" + name)
        res[name] = (g[None], d[None], mn[None], vn[None])
    small_m = [m_ln_in_g, m_ln_in_b, m_b_ada, m_q_norm_g, m_kv_norm_g, m_ln1_g, m_ln1_b, m_ln2_g, m_ln2_b]
    small_v = [v_ln_in_g, v_ln_in_b, v_b_ada, v_q_norm_g, v_kv_norm_g, v_ln1_g, v_ln1_b, v_ln2_g, v_ln2_b]
    sd, sm, sv = _adamw(_pack_small(small_like), _pack_small(small_grads), _pack_small(small_m), _pack_small(small_v),
                        "adamw_small")
    for (name, _), g, d, mn, vn in zip(SMALL, small_grads, _unpack_small(sd, small_like),
                                       _unpack_small(sm, small_like), _unpack_small(sv, small_like)):
        res[name] = (g, d, mn, vn)

    order = ["ln_in_g", "ln_in_b", "w_ada", "b_ada", "w_in", "q_norm_g", "kv_norm_g", "w_uq", "w_ukv", "w_o",
             "ln1_g", "ln1_b", "w_up", "w_down", "ln2_g", "ln2_b"]
    outs = [loss, grad_x.reshape(nb, seq, D_MODEL)]
    for k in range(4):
        outs += [res[name][k] for name in order]
    return tuple(outs)
```

```python
import functools
import math

import jax
import jax.numpy as jnp
from jax import lax
from jax.experimental import pallas as pl
from jax.experimental.pallas import tpu as pltpu

F32 = jnp.float32
BF16 = jnp.bfloat16
MESH_IDS = pl.DeviceIdType.MESH

D_MODEL = 1024
HEADS = 8
HEAD_PAIRS = HEADS // 2
SB_W = 512
MLA_W = 512
NOPE = 64
ROPE = 32
Q_RANK = 384
KV_RANK = 256
D_IN = 2208
D_IN_PAD = 2304
D_FF = 4096
N_MOD = 6
LN_EPS = 1e-5
RMS_EPS = 1e-6
ALPHA = 2.0 ** 0.25
ROPE_BASE = 10000.0
SB_SCALE = 64 ** -0.5
MLA_SCALE = 96 ** -0.5
ADAM_LR = 0.001
ADAM_B1 = 0.9
ADAM_B2 = 0.999
ADAM_EPS = 1e-08
ADAM_WD = 0.01
ADAM_STEP = 10

LANES = 128
ROW_TILE = 256
ATTN_TILE = 256
ATTN_PAIRS = {"sb_fwd": 4, "sb_bwd": 2, "mla_fwd": 4, "mla_bwd": 4}
VMEM_LIMIT = 56 << 20

NT = (((1,), (1,)), ((), ()))
TN = (((0,), (0,)), ((), ()))


def _params(sem=None):
    return pltpu.CompilerParams(vmem_limit_bytes=VMEM_LIMIT, dimension_semantics=sem)


def _const_spec(shape):
    zeros = (0,) * len(shape)
    return pl.BlockSpec(shape, lambda *_: zeros, pipeline_mode=pl.Buffered(1))


def _dot(a, b, dims=None):
    if dims is None:
        return jnp.dot(a, b, preferred_element_type=F32)
    return lax.dot_general(a, b, dims, preferred_element_type=F32)


def _mean(v):
    return jnp.mean(v, axis=-1, keepdims=True)


def _rowsum(v):
    return jnp.sum(v, axis=0, keepdims=True)


def _ln_fwd(y, g, b):
    mu = _mean(y)
    yc = y - mu
    rstd = lax.rsqrt(_mean(yc * yc) + LN_EPS)
    xhat = yc * rstd
    return xhat * g + b, xhat, rstd


def _ln_bwd(dx, xhat, rstd, g):
    dxh = dx * g
    return rstd * (dxh - _mean(dxh) - xhat * _mean(dxh * xhat))


def _rope(v, cos, sin_a, sin_b):
    return v * cos + pltpu.roll(v, 112, 1) * sin_a + pltpu.roll(v, 16, 1) * sin_b


def _rope_t(dv, cos, sin_a, sin_b):
    return dv * cos + pltpu.roll(dv * sin_a, 16, 1) + pltpu.roll(dv * sin_b, 112, 1)


def _my_place():
    return lax.axis_index("x"), lax.axis_index("y"), lax.axis_index("c")


class _Exchange:
    def __init__(self, operand, out_shape, n_copies, phases):
        self.operand = operand
        self.out_shape = out_shape
        self.phases = phases
        self.scratch = [pltpu.SemaphoreType.DMA((n_copies,)), pltpu.SemaphoreType.DMA((n_copies,))]


def _run_exchange(ex, name):
    def body(in_ref, out_ref, send_sems, recv_sems):
        for phase in ex.phases(in_ref, out_ref, send_sems, recv_sems):
            phase()

    return pl.pallas_call(
        body, name=name, out_shape=ex.out_shape,
        in_specs=[pl.BlockSpec(memory_space=pl.ANY)], out_specs=pl.BlockSpec(memory_space=pl.ANY),
        scratch_shapes=ex.scratch,
    )(ex.operand)


def _nothing():
    pass


def _gather_exchange(v):
    m, n = v.shape

    def phases(v_ref, out_ref, send_sems, recv_sems):
        x, y, c = _my_place()
        me, sibling = (x, y, c), (x, y, 1 - c)
        chips = [(1 - x, y), (x, 1 - y), (1 - x, 1 - y)]

        def rows(px, py, pc):
            return out_ref.at[4 * px + 2 * py + pc]

        def copy(k, block, to, src=None):
            return pltpu.make_async_remote_copy(
                src_ref=rows(*block) if src is None else src, dst_ref=rows(*block),
                send_sem=send_sems.at[k], recv_sem=recv_sems.at[k], device_id=to, device_id_type=MESH_IDS)

        first = [copy(0, me, sibling, src=v_ref)]
        first += [copy(1 + j, me, (*chip, c), src=v_ref) for j, chip in enumerate(chips)]
        passed = [copy(4 + j, (*chip, c), sibling) for j, chip in enumerate(chips)]

        def start():
            for cp in first:
                cp.start()

        def middle():
            for j, chip in enumerate(chips):
                copy(1 + j, (*chip, c), me).wait_recv()
                passed[j].start()

        def finish():
            copy(0, sibling, me).wait_recv()
            for j, chip in enumerate(chips):
                copy(4 + j, (*chip, 1 - c), me).wait_recv()
            for cp in first + passed:
                cp.wait_send()

        return start, middle, finish

    return _Exchange(v, jax.ShapeDtypeStruct((8, m, n), v.dtype), 7, phases)


def _with_own(gathered, v):
    dev = 4 * lax.axis_index("x") + 2 * lax.axis_index("y") + lax.axis_index("c")
    return lax.dynamic_update_index_in_dim(gathered, v, dev, 0)


def _direct_exchange(operand, out_shape, n_copies, make_copies):
    def phases(in_ref, out_ref, send_sems, recv_sems):
        copies = make_copies(in_ref, out_ref, send_sems, recv_sems)

        def start():
            for cp in copies:
                cp.start()

        def finish():
            for cp in copies:
                cp.wait()

        return start, _nothing, finish

    return _Exchange(operand, out_shape, n_copies, phases)


def _swap_cores_exchange(blocks):
    _, m, n = blocks.shape

    def make_copies(g_ref, out_ref, send_sems, recv_sems):
        x, y, c = _my_place()
        return [pltpu.make_async_remote_copy(
            src_ref=g_ref.at[2 * j + (1 - c)], dst_ref=out_ref.at[j],
            send_sem=send_sems.at[j], recv_sem=recv_sems.at[j],
            device_id=(x, y, 1 - c), device_id_type=MESH_IDS) for j in range(4)]

    return _direct_exchange(blocks, jax.ShapeDtypeStruct((4, m, n), blocks.dtype), 4, make_copies)


def _scatter_chips_exchange(parts):
    _, m, n = parts.shape
    flips = [(1, 0), (0, 1), (1, 1)]

    def make_copies(p_ref, out_ref, send_sems, recv_sems):
        x, y, c = _my_place()
        copies = []
        for k, (fx, fy) in enumerate(flips):
            tx = 1 - x if fx else x
            ty = 1 - y if fy else y
            copies.append(pltpu.make_async_remote_copy(
                src_ref=p_ref.at[2 * tx + ty], dst_ref=out_ref.at[k],
                send_sem=send_sems.at[k], recv_sem=recv_sems.at[k],
                device_id=(tx, ty, c), device_id_type=MESH_IDS))
        return copies

    return _direct_exchange(parts, jax.ShapeDtypeStruct((3, m, n), parts.dtype), 3, make_copies)


def _swap_one_exchange(v):
    def make_copies(v_ref, out_ref, send_sems, recv_sems):
        x, y, c = _my_place()
        return [pltpu.make_async_remote_copy(src_ref=v_ref, dst_ref=out_ref, send_sem=send_sems.at[0],
                                             recv_sem=recv_sems.at[0], device_id=(x, y, 1 - c),
                                             device_id_type=MESH_IDS)]

    return _direct_exchange(v, jax.ShapeDtypeStruct(v.shape, v.dtype), 1, make_copies)


def _gather8(v, name):
    return _with_own(_run_exchange(_gather_exchange(v), name), v)


def _carried(ex, refs, n_in, n_out):
    ins, ex_in = refs[:n_in], refs[n_in]
    outs, ex_out = refs[n_in + 1:n_in + 1 + n_out], refs[n_in + 1 + n_out]
    return ins, outs, ex.phases(ex_in, ex_out, *refs[n_in + 2 + n_out:])


def _ada_fwd(c_all, w_ada_sh, b_ada_sh):
    nb, cols = c_all.shape[0], w_ada_sh.shape[1]
    tn = 512

    def body(c_ref, w_ref, b_ref, o_ref):
        cv = c_ref[...]
        act = (cv * jax.nn.sigmoid(cv)).astype(BF16)
        o_ref[...] = _dot(act, w_ref[...].astype(BF16)) + b_ref[...]

    return pl.pallas_call(
        body, name="ada_fwd", grid=(cols // tn,),
        out_shape=jax.ShapeDtypeStruct((nb, cols), F32),
        in_specs=[pl.BlockSpec((nb, D_MODEL), lambda j: (0, 0)),
                  pl.BlockSpec((D_MODEL, tn), lambda j: (0, j)),
                  pl.BlockSpec((1, tn), lambda j: (0, j))],
        out_specs=pl.BlockSpec((nb, tn), lambda j: (0, j)),
        compiler_params=_params(("arbitrary",)),
    )(c_all, w_ada_sh, b_ada_sh)


def _ada_bwd(c_all, dmod_sh):
    nb, cols = dmod_sh.shape
    tn = 512

    def body(c_ref, d_ref, o_ref):
        cv = c_ref[...]
        act = (cv * jax.nn.sigmoid(cv)).astype(BF16)
        o_ref[...] = _dot(act, d_ref[...].astype(BF16), TN)

    return pl.pallas_call(
        body, name="ada_bwd", grid=(cols // tn,),
        out_shape=jax.ShapeDtypeStruct((D_MODEL, cols), F32),
        in_specs=[pl.BlockSpec((nb, D_MODEL), lambda j: (0, 0)),
                  pl.BlockSpec((nb, tn), lambda j: (0, j))],
        out_specs=pl.BlockSpec((D_MODEL, tn), lambda j: (0, j)),
        compiler_params=_params(("arbitrary",)),
    )(c_all, dmod_sh)


def _sum_lead(v, name):
    k, m, n = v.shape

    def body(v_ref, o_ref):
        acc = v_ref[0]
        for i in range(1, k):
            acc = acc + v_ref[i]
        o_ref[...] = acc

    return pl.pallas_call(
        body, name=name, out_shape=jax.ShapeDtypeStruct((m, n), F32),
        in_specs=[pl.BlockSpec((k, m, n), lambda: (0, 0, 0))],
        out_specs=pl.BlockSpec((m, n), lambda: (0, 0)),
        compiler_params=_params(),
    )(v)


def _adamw(w, g, m, v, name):
    rows, cols = w.shape
    tr = rows
    while tr * cols * 4 > (2 << 20) and tr % 16 == 0:
        tr //= 2

    def body(w_ref, g_ref, m_ref, v_ref, d_ref, mo_ref, vo_ref):
        gv = g_ref[...]
        mn = ADAM_B1 * m_ref[...] + (1.0 - ADAM_B1) * gv
        vn = ADAM_B2 * v_ref[...] + (1.0 - ADAM_B2) * (gv * gv)
        m_hat = mn / (1.0 - ADAM_B1 ** ADAM_STEP)
        v_hat = vn / (1.0 - ADAM_B2 ** ADAM_STEP)
        d_ref[...] = -ADAM_LR * (m_hat / (jnp.sqrt(v_hat) + ADAM_EPS) + ADAM_WD * w_ref[...])
        mo_ref[...] = mn
        vo_ref[...] = vn

    spec = pl.BlockSpec((tr, cols), lambda i: (i, 0))
    shape = jax.ShapeDtypeStruct((rows, cols), F32)
    return pl.pallas_call(
        body, name=name, grid=(rows // tr,), out_shape=(shape, shape, shape),
        in_specs=[spec, spec, spec, spec], out_specs=(spec, spec, spec),
        compiler_params=_params(("arbitrary",)),
    )(w, g, m, v)


def _add_rows(m, n):
    tr = 8
    while m % (tr * 2) == 0 and tr * 2 * n * 4 <= (1 << 20):
        tr *= 2
    assert m % tr == 0, (m, tr)
    return tr


def _add_pairs(blocks, recv, c_idx, name):
    _, m, n = blocks.shape
    tr = _add_rows(m, n)

    def body(c_ref, a_ref, b_ref, o_ref, ob_ref):
        s = a_ref[...] + b_ref[...]
        o_ref[...] = s
        ob_ref[...] = s.astype(BF16)

    grid_spec = pltpu.PrefetchScalarGridSpec(
        num_scalar_prefetch=1, grid=(4, m // tr),
        in_specs=[pl.BlockSpec((1, tr, n), lambda j, i, c: (2 * j + c[0], i, 0)),
                  pl.BlockSpec((1, tr, n), lambda j, i, c: (j, i, 0))],
        out_specs=(pl.BlockSpec((1, tr, n), lambda j, i, c: (j, i, 0)),
                   pl.BlockSpec((1, tr, n), lambda j, i, c: (j, i, 0))))
    return pl.pallas_call(
        body, name=name, grid_spec=grid_spec,
        out_shape=(jax.ShapeDtypeStruct((4, m, n), F32), jax.ShapeDtypeStruct((4, m, n), BF16)),
        compiler_params=_params(("arbitrary", "arbitrary")),
    )(c_idx, blocks, recv)


def _add_chips(own, recv, name):
    m, n = own.shape
    tr = _add_rows(m, n)

    def body(a_ref, r_ref, o_ref):
        acc = a_ref[...]
        for k in range(3):
            acc = acc + r_ref[k].astype(F32)
        o_ref[...] = acc

    return pl.pallas_call(
        body, name=name, grid=(m // tr,),
        out_shape=jax.ShapeDtypeStruct((m, n), F32),
        in_specs=[pl.BlockSpec((tr, n), lambda i: (i, 0)), pl.BlockSpec((3, tr, n), lambda i: (0, i, 0))],
        out_specs=pl.BlockSpec((tr, n), lambda i: (i, 0)),
        compiler_params=_params(("arbitrary",)),
    )(own, recv)


def _row_spec(cols):
    return pl.BlockSpec((ROW_TILE, cols), lambda i: (i, 0))


def _mod_spec(tiles_per_seq):
    return pl.BlockSpec((1, 8, D_MODEL), lambda i: (i // tiles_per_seq, 0, 0))


def _table_spec(tiles_per_seq):
    return pl.BlockSpec((ROW_TILE, LANES), lambda i: (i % tiles_per_seq, 0))


def _fwd_in(x, mod, ln_g, ln_b, w_in, q_g, kv_g, w_uq, w_ukv, cos_t, sin_a, sin_b, seq):
    rows = x.shape[0]
    tm = ROW_TILE
    tps = seq // tm

    def body(x_ref, mod_ref, g_ref, b_ref, win_ref, qg_ref, kvg_ref, wuq_ref, wukv_ref, cos_ref, sa_ref, sb_ref,
             x0_ref, h_ref, qkv_ref, lat_ref, qp_ref, kp_ref, vm_ref):
        x0, _, _ = _ln_fwd(x_ref[...], g_ref[...], b_ref[...])
        x0_ref[...] = x0
        h = (x0 * (1.0 + mod_ref[0, 1:2, :]) + mod_ref[0, 0:1, :]).astype(BF16)
        h_ref[...] = h
        proj = _dot(h, win_ref[...])
        qkv_ref[:, :SB_W] = (proj[:, :SB_W] * SB_SCALE).astype(BF16)
        qkv_ref[:, SB_W:] = proj[:, SB_W:3 * SB_W].astype(BF16)
        lat_ref[...] = proj[:, 3 * SB_W:3 * SB_W + Q_RANK + KV_RANK]
        cq = proj[:, 3 * SB_W:3 * SB_W + Q_RANK]
        ckv = proj[:, 3 * SB_W + Q_RANK:3 * SB_W + Q_RANK + KV_RANK]
        kr = proj[:, D_IN_PAD - LANES:]
        cos, sa, sb = cos_ref[...], sa_ref[...], sb_ref[...]
        cqn = (cq * lax.rsqrt(_mean(cq * cq) + RMS_EPS) * qg_ref[...]).astype(BF16)
        q_all = _dot(cqn, wuq_ref[...])
        for hd in range(HEADS):
            sl = slice(hd * LANES, (hd + 1) * LANES)
            qp_ref[:, sl] = _rope(q_all[:, sl], cos, sa, sb).astype(BF16)
        ckvn = (ckv * lax.rsqrt(_mean(ckv * ckv) + RMS_EPS) * kvg_ref[...]).astype(BF16)
        kv = _dot(ckvn, wukv_ref[...])
        kr_rot = _rope(kr, cos, sa, sb)
        for hd in range(HEADS):
            sl = slice(hd * LANES, (hd + 1) * LANES)
            kp_ref[:, sl] = (kv[:, sl] + kr_rot).astype(BF16)
        vm_ref[...] = kv[:, HEADS * LANES:].astype(BF16)

    outs = [(D_MODEL, F32), (D_MODEL, BF16), (3 * SB_W, BF16), (Q_RANK + KV_RANK, F32),
            (HEADS * LANES, BF16), (HEADS * LANES, BF16), (MLA_W, BF16)]
    return pl.pallas_call(
        body, name="fwd_in", grid=(rows // tm,),
        out_shape=tuple(jax.ShapeDtypeStruct((rows, n), dt) for n, dt in outs),
        in_specs=[_row_spec(D_MODEL), _mod_spec(tps), _const_spec((1, D_MODEL)), _const_spec((1, D_MODEL)),
                  _const_spec(w_in.shape), _const_spec((1, Q_RANK)), _const_spec((1, KV_RANK)),
                  _const_spec(w_uq.shape), _const_spec(w_ukv.shape),
                  _table_spec(tps), _table_spec(tps), _table_spec(tps)],
        out_specs=tuple(_row_spec(n) for n, _ in outs),
        compiler_params=_params(("arbitrary",)),
    )(x, mod, ln_g, ln_b, w_in, q_g, kv_g, w_uq, w_ukv, cos_t, sin_a, sin_b)


def _fwd_mix(sb_y, mla_y, x0, mod, w_o, ln_g, ln_b, seq):
    rows = x0.shape[0]
    tm = ROW_TILE
    tps = seq // tm

    def body(sb_ref, ml_ref, x0_ref, mod_ref, wo_ref, g_ref, b_ref, mix_ref, y1_ref, x1_ref, h2_ref):
        mix = _dot(sb_ref[...].astype(BF16), wo_ref[:SB_W, :]) + _dot(ml_ref[...].astype(BF16), wo_ref[SB_W:, :])
        mix_ref[...] = mix
        y1 = ALPHA * x0_ref[...] + (1.0 + mod_ref[0, 2:3, :]) * mix
        y1_ref[...] = y1
        x1, _, _ = _ln_fwd(y1, g_ref[...], b_ref[...])
        x1_ref[...] = x1
        h2_ref[...] = (x1 * (1.0 + mod_ref[0, 4:5, :]) + mod_ref[0, 3:4, :]).astype(BF16)

    outs = [(D_MODEL, F32), (D_MODEL, F32), (D_MODEL, F32), (D_MODEL, BF16)]
    return pl.pallas_call(
        body, name="fwd_mix", grid=(rows // tm,),
        out_shape=tuple(jax.ShapeDtypeStruct((rows, n), dt) for n, dt in outs),
        in_specs=[_row_spec(SB_W), _row_spec(MLA_W), _row_spec(D_MODEL), _mod_spec(tps), _const_spec(w_o.shape),
                  _const_spec((1, D_MODEL)), _const_spec((1, D_MODEL))],
        out_specs=tuple(_row_spec(n) for n, _ in outs),
        compiler_params=_params(("arbitrary",)),
    )(sb_y, mla_y, x0, mod, w_o, ln_g, ln_b)


HALF = 512
SHARD = 1024


def _mlp_weight_specs():
    return [pl.BlockSpec((8, HALF, SHARD), lambda i: (0, 0, 0), pipeline_mode=pl.Buffered(1)),
            pl.BlockSpec((8, HALF, SHARD), lambda i: (0, 1, 0), pipeline_mode=pl.Buffered(1))]


def _fwd_mlp(h2, x1, mod, g_mlp, seq):
    rows = x1.shape[0]
    tm = ROW_TILE
    tps = seq // tm

    def body(h2_ref, x1_ref, mod_ref, wu_ref, wd_ref, u_ref, ff_ref, y2_ref):
        h_lo, h_hi = h2_ref[:, :HALF], h2_ref[:, HALF:]
        ff = jnp.zeros((tm, D_MODEL), F32)
        for chip in range(4):
            u = _dot(h_lo, wu_ref[2 * chip]) + _dot(h_hi, wu_ref[2 * chip + 1])
            u_ref[:, chip * SHARD:(chip + 1) * SHARD] = u.astype(BF16)
            act = jnp.square(jnp.maximum(u, 0.0)).astype(BF16)
            ff = ff + _dot(act[:, :HALF], wd_ref[2 * chip]) + _dot(act[:, HALF:], wd_ref[2 * chip + 1])
        ff_ref[...] = ff
        y2_ref[...] = ALPHA * x1_ref[...] + (1.0 + mod_ref[0, 5:6, :]) * ff

    outs = [(D_FF, BF16), (D_MODEL, F32), (D_MODEL, F32)]
    return pl.pallas_call(
        body, name="fwd_mlp", grid=(rows // tm,),
        out_shape=tuple(jax.ShapeDtypeStruct((rows, n), dt) for n, dt in outs),
        in_specs=[_row_spec(D_MODEL), _row_spec(D_MODEL), _mod_spec(tps)] + _mlp_weight_specs(),
        out_specs=tuple(_row_spec(n) for n, _ in outs),
        compiler_params=_params(("arbitrary",)),
    )(h2, x1, mod, g_mlp, g_mlp)


def _acc_spec(rows=8, cols=D_MODEL):
    return pl.BlockSpec((rows, cols), lambda i: (0, 0))


def _bwd_out(y2, tgt, ff, u, mod, ln_g, ln_b, g_mlp, seq):
    rows = y2.shape[0]
    nb = rows // seq
    tm = ROW_TILE
    tps = seq // tm

    def body(y2_ref, t_ref, ff_ref, u_ref, mod_ref, g_ref, b_ref, wd_ref, dy2_ref, dff_ref, du_ref, acc_ref, dmod_ref):
        i = pl.program_id(0)

        @pl.when(i == 0)
        def _():
            acc_ref[...] = jnp.zeros_like(acc_ref)

        @pl.when(i % tps == 0)
        def _():
            dmod_ref[...] = jnp.zeros_like(dmod_ref)

        g = g_ref[...]
        x2, xhat, rstd = _ln_fwd(y2_ref[...], g, b_ref[...])
        err = x2 - t_ref[...]
        dx2 = err * (1.0 / D_MODEL)
        acc_ref[0:1, :] += _rowsum(dx2 * xhat)
        acc_ref[1:2, :] += _rowsum(dx2)
        acc_ref[2:3, :] += _rowsum(err * err) * (0.5 / D_MODEL)
        dy2 = _ln_bwd(dx2, xhat, rstd, g)
        dy2_ref[...] = dy2
        dmod_ref[0, 5:6, :] += _rowsum(dy2 * ff_ref[...])
        dff = ((1.0 + mod_ref[0, 5:6, :]) * dy2).astype(BF16)
        dff_ref[...] = dff
        for blk in range(8):
            cols = slice(blk * HALF, (blk + 1) * HALF)
            da = _dot(dff, wd_ref[blk], NT)
            du_ref[:, cols] = (da * (2.0 * jnp.maximum(u_ref[:, cols].astype(F32), 0.0))).astype(BF16)

    outs = [(D_MODEL, F32), (D_MODEL, BF16), (D_FF, BF16)]
    return pl.pallas_call(
        body, name="bwd_out", grid=(rows // tm,),
        out_shape=tuple(jax.ShapeDtypeStruct((rows, n), dt) for n, dt in outs)
        + (jax.ShapeDtypeStruct((8, D_MODEL), F32), jax.ShapeDtypeStruct((nb, 8, D_MODEL), F32)),
        in_specs=[_row_spec(D_MODEL), _row_spec(D_MODEL), _row_spec(D_MODEL), _row_spec(D_FF), _mod_spec(tps),
                  _const_spec((1, D_MODEL)), _const_spec((1, D_MODEL)), _mlp_weight_specs()[1]],
        out_specs=tuple(_row_spec(n) for n, _ in outs) + (_acc_spec(), _mod_spec(tps)),
        compiler_params=_params(("arbitrary",)),
    )(y2, tgt, ff, u, mod, ln_g, ln_b, g_mlp)


def _bwd_mid(du, y1, mix, dy2, mod, ln_g, ln_b, g_mlp, w_o, seq):
    rows = y1.shape[0]
    nb = rows // seq
    tm = ROW_TILE
    tps = seq // tm

    def body(du_ref, y1_ref, mix_ref, dy2_ref, mod_ref, g_ref, b_ref, wu_ref, wo_ref,
             dy1_ref, dmix_ref, do_ref, acc_ref, dmod_ref):
        i = pl.program_id(0)

        @pl.when(i == 0)
        def _():
            acc_ref[...] = jnp.zeros_like(acc_ref)

        @pl.when(i % tps == 0)
        def _():
            dmod_ref[...] = jnp.zeros_like(dmod_ref)

        g = g_ref[...]
        x1, xhat, rstd = _ln_fwd(y1_ref[...], g, b_ref[...])
        halves = []
        for half in range(2):
            acc = jnp.zeros((tm, HALF), F32)
            for chip in range(4):
                acc = acc + _dot(du_ref[:, chip * SHARD:(chip + 1) * SHARD], wu_ref[2 * chip + half], NT)
            halves.append(acc)
        dh2 = jnp.concatenate(halves, axis=1)
        dmod_ref[0, 3:4, :] += _rowsum(dh2)
        dmod_ref[0, 4:5, :] += _rowsum(dh2 * x1)
        dx1 = ALPHA * dy2_ref[...] + dh2 * (1.0 + mod_ref[0, 4:5, :])
        acc_ref[0:1, :] += _rowsum(dx1 * xhat)
        acc_ref[1:2, :] += _rowsum(dx1)
        dy1 = _ln_bwd(dx1, xhat, rstd, g)
        dy1_ref[...] = dy1
        dmod_ref[0, 2:3, :] += _rowsum(dy1 * mix_ref[...])
        dmix = ((1.0 + mod_ref[0, 2:3, :]) * dy1).astype(BF16)
        dmix_ref[...] = dmix
        do_ref[...] = _dot(dmix, wo_ref[...], NT)

    outs = [(D_MODEL, F32), (D_MODEL, BF16), (D_MODEL, F32)]
    return pl.pallas_call(
        body, name="bwd_mid", grid=(rows // tm,),
        out_shape=tuple(jax.ShapeDtypeStruct((rows, n), dt) for n, dt in outs)
        + (jax.ShapeDtypeStruct((8, D_MODEL), F32), jax.ShapeDtypeStruct((nb, 8, D_MODEL), F32)),
        in_specs=[_row_spec(D_FF), _row_spec(D_MODEL), _row_spec(D_MODEL), _row_spec(D_MODEL), _mod_spec(tps),
                  _const_spec((1, D_MODEL)), _const_spec((1, D_MODEL)), _mlp_weight_specs()[0],
                  _const_spec(w_o.shape)],
        out_specs=tuple(_row_spec(n) for n, _ in outs) + (_acc_spec(), _mod_spec(tps)),
        compiler_params=_params(("arbitrary",)),
    )(du, y1, mix, dy2, mod, ln_g, ln_b, g_mlp, w_o)


def _bwd_in(dqp, dkp, dvm, dq_sb, dk_sb, dv_sb, lat, x, x0, dy1, mod, ln_g, ln_b, w_in, q_g, kv_g, w_uq, w_ukv,
            cos_t, sin_a, sin_b, seq):
    rows = x.shape[0]
    nb = rows // seq
    tm = ROW_TILE
    tps = seq // tm
    n_lat = Q_RANK + KV_RANK

    def body(dqp_ref, dkp_ref, dvm_ref, dqs_ref, dks_ref, dvs_ref, lat_ref, x_ref, x0_ref, dy1_ref, mod_ref,
             g_ref, b_ref, win_ref, qg_ref, kvg_ref, wuq_ref, wukv_ref, cos_ref, sa_ref, sb_ref,
             dx_ref, dproj_ref, dqall_ref, dkv_ref, latn_ref, acc_ref, accl_ref, dmod_ref):
        i = pl.program_id(0)

        @pl.when(i == 0)
        def _():
            acc_ref[...] = jnp.zeros_like(acc_ref)
            accl_ref[...] = jnp.zeros_like(accl_ref)

        @pl.when(i % tps == 0)
        def _():
            dmod_ref[...] = jnp.zeros_like(dmod_ref)

        cos, sa, sb = cos_ref[...], sa_ref[...], sb_ref[...]
        lane = lax.broadcasted_iota(jnp.int32, (tm, LANES), 1)
        for hd in range(HEADS):
            sl = slice(hd * LANES, (hd + 1) * LANES)
            dqall_ref[:, sl] = _rope_t(dqp_ref[:, sl], cos, sa, sb).astype(BF16)
        dcqn = _dot(dqall_ref[...], wuq_ref[...], NT)
        cq = lat_ref[:, :Q_RANK]
        qg = qg_ref[...]
        rq = lax.rsqrt(_mean(cq * cq) + RMS_EPS)
        cqn = cq * rq
        latn_ref[:, :Q_RANK] = (cqn * qg).astype(BF16)
        accl_ref[0:1, :Q_RANK] += _rowsum(dcqn * cqn)
        dqg = dcqn * qg
        dcq = rq * (dqg - cqn * _mean(dqg * cqn))
        dkr = jnp.zeros((tm, LANES), F32)
        for hd in range(HEADS):
            sl = slice(hd * LANES, (hd + 1) * LANES)
            dk = dkp_ref[:, sl]
            dkr = dkr + dk
            dkv_ref[:, sl] = jnp.where(lane < NOPE, dk, 0.0).astype(BF16)
        dkv_ref[:, HEADS * LANES:] = dvm_ref[...].astype(BF16)
        dckvn = _dot(dkv_ref[...], wukv_ref[...], NT)
        ckv = lat_ref[:, Q_RANK:]
        kvg = kvg_ref[...]
        rkv = lax.rsqrt(_mean(ckv * ckv) + RMS_EPS)
        ckvn = ckv * rkv
        latn_ref[:, Q_RANK:] = (ckvn * kvg).astype(BF16)
        accl_ref[1:2, :KV_RANK] += _rowsum(dckvn * ckvn)
        dkg = dckvn * kvg
        dckv = rkv * (dkg - ckvn * _mean(dkg * ckvn))
        dkr = _rope_t(jnp.where(lane >= NOPE, dkr, 0.0), cos, sa, sb)
        dproj_ref[:, :SB_W] = dqs_ref[...]
        dproj_ref[:, SB_W:2 * SB_W] = dks_ref[...].astype(BF16)
        dproj_ref[:, 2 * SB_W:3 * SB_W] = dvs_ref[...].astype(BF16)
        dproj_ref[:, 3 * SB_W:3 * SB_W + Q_RANK] = dcq.astype(BF16)
        dproj_ref[:, 3 * SB_W + Q_RANK:3 * SB_W + n_lat] = dckv.astype(BF16)
        dproj_ref[:, D_IN_PAD - LANES:] = dkr.astype(BF16)
        dh = _dot(dproj_ref[...], win_ref[...], NT)
        x0 = x0_ref[...]
        dmod_ref[0, 0:1, :] += _rowsum(dh)
        dmod_ref[0, 1:2, :] += _rowsum(dh * x0)
        dx0 = ALPHA * dy1_ref[...] + dh * (1.0 + mod_ref[0, 1:2, :])
        g = g_ref[...]
        _, xhat, rstd = _ln_fwd(x_ref[...], g, b_ref[...])
        acc_ref[0:1, :] += _rowsum(dx0 * xhat)
        acc_ref[1:2, :] += _rowsum(dx0)
        dx_ref[...] = _ln_bwd(dx0, xhat, rstd, g)

    outs = [(D_MODEL, F32), (D_IN_PAD, BF16), (HEADS * LANES, BF16), (HEADS * LANES + MLA_W, BF16), (n_lat, BF16)]
    return pl.pallas_call(
        body, name="bwd_in", grid=(rows // tm,),
        out_shape=tuple(jax.ShapeDtypeStruct((rows, n), dt) for n, dt in outs)
        + (jax.ShapeDtypeStruct((8, D_MODEL), F32), jax.ShapeDtypeStruct((8, Q_RANK), F32),
           jax.ShapeDtypeStruct((nb, 8, D_MODEL), F32)),
        in_specs=[_row_spec(HEADS * LANES), _row_spec(HEADS * LANES), _row_spec(MLA_W),
                  _row_spec(SB_W), _row_spec(SB_W), _row_spec(SB_W), _row_spec(n_lat),
                  _row_spec(D_MODEL), _row_spec(D_MODEL), _row_spec(D_MODEL), _mod_spec(tps),
                  _const_spec((1, D_MODEL)), _const_spec((1, D_MODEL)), _const_spec(w_in.shape),
                  _const_spec((1, Q_RANK)), _const_spec((1, KV_RANK)), _const_spec(w_uq.shape),
                  _const_spec(w_ukv.shape), _table_spec(tps), _table_spec(tps), _table_spec(tps)],
        out_specs=tuple(_row_spec(n) for n, _ in outs) + (_acc_spec(), _acc_spec(8, Q_RANK), _mod_spec(tps)),
        compiler_params=_params(("arbitrary",)),
    )(dqp, dkp, dvm, dq_sb, dk_sb, dv_sb, lat, x, x0, dy1, mod, ln_g, ln_b, w_in, q_g, kv_g, w_uq, w_ukv,
      cos_t, sin_a, sin_b)


def _wgrad(a, b, name, pre=None, tm=512, tn=1024, tk=2048):
    rows, m = a.shape
    n = b.shape[1]
    tm, tn, tk = min(tm, m), min(tn, n), min(tk, rows)
    if m % tm:
        tm = m
    if n % tn:
        tn = n

    def body(a_ref, b_ref, o_ref):
        @pl.when(pl.program_id(2) == 0)
        def _():
            o_ref[...] = jnp.zeros_like(o_ref)

        av = a_ref[...]
        if pre == "relu2":
            av = jnp.square(jnp.maximum(av.astype(F32), 0.0))
        o_ref[...] += _dot(av.astype(BF16), b_ref[...].astype(BF16), TN)

    return pl.pallas_call(
        body, name=name, grid=(m // tm, n // tn, rows // tk),
        out_shape=jax.ShapeDtypeStruct((m, n), F32),
        in_specs=[pl.BlockSpec((tk, tm), lambda i, j, k: (k, i)), pl.BlockSpec((tk, tn), lambda i, j, k: (k, j))],
        out_specs=pl.BlockSpec((tm, tn), lambda i, j, k: (i, j)),
        compiler_params=_params(("arbitrary", "arbitrary", "arbitrary")),
    )(a, b)


def _wgrad_packed(a, b, name, block_of, row_block, split=1, pre=None, into=None, tk=2048):
    rows, m = a.shape
    n = b.shape[1]
    tm = HALF
    part = tm // split
    tk = min(tk, rows)
    shape = jax.ShapeDtypeStruct((8, GROUP_MLP[0], PACK_COLS), F32)

    def body(a_ref, b_ref, *rest):
        o_ref = rest[-1]

        @pl.when(pl.program_id(2) == 0)
        def _():
            o_ref[...] = jnp.zeros_like(o_ref)

        av = a_ref[...]
        if pre == "relu2":
            av = jnp.square(jnp.maximum(av.astype(F32), 0.0))
        prod = _dot(av.astype(BF16), b_ref[...].astype(BF16), TN)
        for s in range(split):
            o_ref[s] += prod[s * part:(s + 1) * part]

    in_specs = [pl.BlockSpec((tk, tm), lambda i, j, k: (k, i)), pl.BlockSpec((tk, SHARD), lambda i, j, k: (k, j))]
    operands = [a, b]
    if into is not None:
        in_specs.append(pl.BlockSpec(memory_space=pl.ANY))
        operands.append(into)
    return pl.pallas_call(
        body, name=name, grid=(m // tm, n // SHARD, rows // tk), out_shape=shape,
        in_specs=in_specs,
        out_specs=pl.BlockSpec((split, part, SHARD), lambda i, j, k: (block_of(i, j), row_block, 0)),
        input_output_aliases={} if into is None else {2: 0},
        compiler_params=_params(("arbitrary", "arbitrary", "arbitrary")),
    )(*operands)


def _pair(pp):
    return slice(pp * LANES, (pp + 1) * LANES)


def _head_mask(lane, hh):
    return jnp.where((lane >= 64) if hh else (lane < 64), 1.0, 0.0).astype(BF16)


def _tri(t, kind):
    r = lax.broadcasted_iota(jnp.int32, (t, t), 0)
    c = lax.broadcasted_iota(jnp.int32, (t, t), 1)
    one = jnp.where(r > c if kind == "suffix" else r < c, 1.0, 0.0).astype(BF16)
    return jnp.concatenate([one, one], axis=0)


def _split_dot(v, tri2):
    hi = v.astype(BF16)
    lo = (v - hi.astype(F32)).astype(BF16)
    return _dot(jnp.concatenate([hi, lo], axis=1), tri2)


def _sb_logits(z, valid):
    log_keep = -(jnp.maximum(z, 0.0) + jnp.log(1.0 + jnp.exp(-jnp.abs(z))))
    log_beta = z + log_keep
    if valid is not None:
        log_keep = jnp.where(valid, log_keep, 0.0)
    return log_keep, log_beta


def _attention_call(body, ex, name, grid, operands, in_specs, out_shapes, out_specs):
    n_in, n_out = len(operands), len(out_shapes)
    total = grid[0] * grid[1] * grid[2]
    any_spec = pl.BlockSpec(memory_space=pl.ANY)

    def carrier(*refs):
        ins, outs, (start, middle, finish) = _carried(ex, refs, n_in, n_out)
        step = (pl.program_id(0) * grid[1] + pl.program_id(1)) * grid[2] + pl.program_id(2)
        pl.when(step == 0)(start)
        pl.when(step == total // 2)(middle)
        body(*ins, *outs)
        pl.when(step == total - 1)(finish)

    carried = ex is not None
    return pl.pallas_call(
        carrier if carried else body, name=name, grid=grid,
        out_shape=tuple(out_shapes) + ((ex.out_shape,) if carried else ()),
        in_specs=list(in_specs) + ([any_spec] if carried else []),
        out_specs=tuple(out_specs) + ((any_spec,) if carried else ()),
        scratch_shapes=ex.scratch if carried else [],
        compiler_params=_params(("arbitrary", "arbitrary", "arbitrary")),
    )(*operands, *([ex.operand] if carried else []))


def _sb_fwd(qkv, seq, ex=None):
    rows = qkv.shape[0]
    nb = rows // seq
    t = min(ATTN_TILE, seq)
    nq = seq // t
    ap = ATTN_PAIRS["sb_fwd"]
    width = ap * LANES
    groups = SB_W // width
    hds = [(pp, hh) for pp in range(ap) for hh in range(2)]

    def body(q_ref, k_ref, v_ref, tri_ref, o_ref, car_ref):
        i = pl.program_id(2)
        lane = lax.broadcasted_iota(jnp.int32, (t, LANES), 1)
        row = lax.broadcasted_iota(jnp.int32, (t, t), 0)
        col = lax.broadcasted_iota(jnp.int32, (t, t), 1)
        strict = col < row
        tri = tri_ref[...]
        masks = [_head_mask(lane, hh) for hh in range(2)]
        qms = [q_ref[:, _pair(pp)] * masks[hh] for pp, hh in hds]

        def step(kb, carry, valid):
            c_sums, accs, cars = carry
            accs, cars = list(accs), list(cars)
            start = pl.multiple_of(kb * t, t)
            kss = [k_ref[pl.ds(start, t), _pair(pp)] for pp in range(ap)]
            vss = [v_ref[pl.ds(start, t), _pair(pp)] for pp in range(ap)]
            zs = [_dot(qms[n], kss[pp], NT) for n, (pp, _) in enumerate(hds)]
            logs = [_sb_logits(z, valid) for z in zs]
            sufs = [_split_dot(lg[0], tri) for lg in logs]
            new_sums = []
            for n, (pp, hh) in enumerate(hds):
                log_keep, log_beta = logs[n]
                w = jnp.exp(log_beta + sufs[n] + c_sums[n])
                if valid is not None:
                    w = jnp.where(valid, w, 0.0)
                accs[pp] = accs[pp] + _dot(w.astype(BF16), vss[pp] * masks[hh])
                cars[pp] = jnp.where(lane == hh * 8 + kb, c_sums[n], cars[pp])
                new_sums.append(c_sums[n] + jnp.sum(log_keep, axis=1, keepdims=True))
            return tuple(new_sums), tuple(accs), tuple(cars)

        zeros = tuple(jnp.zeros((t, LANES), F32) for _ in range(ap))
        carry = step(i, (tuple(jnp.zeros((t, 1), F32) for _ in hds), zeros, zeros), strict)
        _, accs, cars = lax.fori_loop(0, i, lambda j, cr: step(i - 1 - j, cr, None), carry)
        for pp in range(ap):
            o_ref[:, _pair(pp)] = accs[pp].astype(BF16)
            car_ref[:, _pair(pp)] = cars[pp]

    qspec = pl.BlockSpec((t, width), lambda b, p, i: (b * nq + i, p))
    return _attention_call(
        body, ex, "sb_fwd", (nb, groups, nq),
        [qkv, qkv, qkv, _tri(t, "suffix")],
        [qspec,
         pl.BlockSpec((seq, width), lambda b, p, i: (b, groups + p)),
         pl.BlockSpec((seq, width), lambda b, p, i: (b, 2 * groups + p)),
         _const_spec((2 * t, t))],
        [jax.ShapeDtypeStruct((rows, SB_W), BF16), jax.ShapeDtypeStruct((rows, SB_W), F32)],
        [qspec, qspec])


def _sb_bwd(qkv, d_out, cars, seq, ex=None):
    rows = qkv.shape[0]
    nb = rows // seq
    t = min(ATTN_TILE, seq)
    nq = seq // t
    ap = ATTN_PAIRS["sb_bwd"]
    width = ap * LANES
    groups = SB_W // width
    hds = [(pp, hh) for pp in range(ap) for hh in range(2)]

    def body(q_ref, k_ref, v_ref, do_ref, car_ref, tri_ref, pre_ref, dq_ref, dk_ref, dv_ref):
        i = pl.program_id(2)

        @pl.when(i == 0)
        def _():
            dk_ref[...] = jnp.zeros_like(dk_ref)
            dv_ref[...] = jnp.zeros_like(dv_ref)

        lane = lax.broadcasted_iota(jnp.int32, (t, LANES), 1)
        row = lax.broadcasted_iota(jnp.int32, (t, t), 0)
        col = lax.broadcasted_iota(jnp.int32, (t, t), 1)
        strict = col < row
        tri, pre = tri_ref[...], pre_ref[...]
        masks = [_head_mask(lane, hh) for hh in range(2)]
        qms = [q_ref[:, _pair(pp)] * masks[hh] for pp, hh in hds]
        doms = [do_ref[:, _pair(pp)].astype(BF16) * masks[hh] for pp, hh in hds]
        cars = [car_ref[:, _pair(pp)] for pp in range(ap)]

        def step(kb, carry, valid):
            g_pres, dqs = carry
            dqs = list(dqs)
            start = pl.multiple_of(kb * t, t)
            kss = [k_ref[pl.ds(start, t), _pair(pp)] for pp in range(ap)]
            vss = [v_ref[pl.ds(start, t), _pair(pp)] for pp in range(ap)]
            zs = [_dot(qms[n], kss[pp], NT) for n, (pp, _) in enumerate(hds)]
            dws = [_dot(doms[n], vss[pp], NT) for n, (pp, _) in enumerate(hds)]
            logs = [_sb_logits(z, valid) for z in zs]
            sufs = [_split_dot(lg[0], tri) for lg in logs]
            ws, gs = [], []
            for n, (pp, hh) in enumerate(hds):
                c_sum = jnp.sum(jnp.where(lane == hh * 8 + kb, cars[pp], 0.0), axis=1, keepdims=True)
                w = jnp.exp(logs[n][1] + sufs[n] + c_sum)
                if valid is not None:
                    w = jnp.where(valid, w, 0.0)
                ws.append(w)
                gs.append(dws[n] * w)
            befores = [g_pres[n] + _split_dot(gs[n], pre) for n in range(len(hds))]
            for pp in range(ap):
                dv_ref[pl.ds(start, t), _pair(pp)] += (_dot(ws[2 * pp].astype(BF16), doms[2 * pp], TN)
                                                       + _dot(ws[2 * pp + 1].astype(BF16), doms[2 * pp + 1], TN))
            dzbs = []
            for n in range(len(hds)):
                dz = gs[n] * jnp.exp(logs[n][0]) - jnp.exp(logs[n][1]) * befores[n]
                if valid is not None:
                    dz = jnp.where(valid, dz, 0.0)
                dzbs.append(dz.astype(BF16))
            for pp in range(ap):
                a, b = 2 * pp, 2 * pp + 1
                dqs[pp] = dqs[pp] + _dot(dzbs[a], kss[pp] * masks[0]) + _dot(dzbs[b], kss[pp] * masks[1])
                dk_ref[pl.ds(start, t), _pair(pp)] += _dot(dzbs[a], qms[a], TN) + _dot(dzbs[b], qms[b], TN)
            new_pres = [g_pres[n] + jnp.sum(gs[n], axis=1, keepdims=True) for n in range(len(hds))]
            return tuple(new_pres), tuple(dqs)

        init = (tuple(jnp.zeros((t, 1), F32) for _ in hds), tuple(jnp.zeros((t, LANES), F32) for _ in range(ap)))
        carry = lax.fori_loop(0, i, lambda kb, cr: step(kb, cr, None), init)
        _, dqs = step(i, carry, strict)
        for pp in range(ap):
            dq_ref[:, _pair(pp)] = (dqs[pp] * SB_SCALE).astype(BF16)

    qspec = pl.BlockSpec((t, width), lambda b, p, i: (b * nq + i, p))
    kspec_out = pl.BlockSpec((seq, width), lambda b, p, i: (b, p))
    return _attention_call(
        body, ex, "sb_bwd", (nb, groups, nq),
        [qkv, qkv, qkv, d_out, cars, _tri(t, "suffix"), _tri(t, "prefix")],
        [qspec,
         pl.BlockSpec((seq, width), lambda b, p, i: (b, groups + p)),
         pl.BlockSpec((seq, width), lambda b, p, i: (b, 2 * groups + p)),
         qspec, qspec, _const_spec((2 * t, t)), _const_spec((2 * t, t))],
        [jax.ShapeDtypeStruct((rows, SB_W), BF16), jax.ShapeDtypeStruct((rows, SB_W), F32),
         jax.ShapeDtypeStruct((rows, SB_W), F32)],
        [qspec, kspec_out, kspec_out])


def _mla_scores(qh, ks, allowed):
    s = _dot(qh, ks, NT) * MLA_SCALE
    if allowed is not None:
        s = jnp.where(allowed, s, jnp.finfo(F32).min)
    return s


def _mla_fwd(qp, kp, vm, seq, ex=None, chunk=64):
    rows = qp.shape[0]
    nb = rows // seq
    t = min(ATTN_TILE, seq)
    nq = seq // t
    shift = int(math.log2(chunk))
    ap = ATTN_PAIRS["mla_fwd"]
    width = ap * LANES
    groups = MLA_W // width
    hds = [(pp, hh) for pp in range(ap) for hh in range(2)]

    def body(q_ref, k_ref, v_ref, o_ref, lse_ref):
        i = pl.program_id(2)
        lane = lax.broadcasted_iota(jnp.int32, (t, LANES), 1)
        row = lax.broadcasted_iota(jnp.int32, (t, t), 0)
        col = lax.broadcasted_iota(jnp.int32, (t, t), 1)
        allowed_diag = jnp.right_shift(col, shift) <= jnp.right_shift(row, shift)
        masks = [_head_mask(lane, hh) for hh in range(2)]
        qhs = [q_ref[:, _pair(n)] for n in range(len(hds))]

        def step(kb, carry, allowed):
            start = pl.multiple_of(kb * t, t)
            vss = [v_ref[pl.ds(start, t), _pair(pp)] for pp in range(ap)]
            scores = [_mla_scores(qhs[n], k_ref[pl.ds(start, t), _pair(n)], allowed) for n in range(len(hds))]
            new = []
            for n, (pp, hh) in enumerate(hds):
                m_run, l_run, acc = carry[n]
                s = scores[n]
                m_new = jnp.maximum(m_run, jnp.max(s, axis=1, keepdims=True))
                p = jnp.exp(s - m_new)
                scale = jnp.exp(m_run - m_new)
                l_run = scale * l_run + jnp.sum(p, axis=1, keepdims=True)
                acc = scale * acc + _dot(p.astype(BF16), vss[pp] * masks[hh])
                new.append((m_new, l_run, acc))
            return tuple(new)

        init = (jnp.full((t, 1), jnp.finfo(F32).min, F32), jnp.zeros((t, 1), F32), jnp.zeros((t, LANES), F32))
        carry = step(i, tuple(init for _ in hds), allowed_diag)
        carry = lax.fori_loop(0, i, lambda kb, cr: step(kb, cr, None), carry)
        for pp in range(ap):
            out = jnp.zeros((t, LANES), F32)
            lses = jnp.zeros((t, LANES), F32)
            for hh in range(2):
                m_run, l_run, acc = carry[2 * pp + hh]
                out = out + acc / l_run
                lses = jnp.where(lane == hh, m_run + jnp.log(l_run), lses)
            o_ref[:, _pair(pp)] = out
            lse_ref[:, _pair(pp)] = lses

    ospec = pl.BlockSpec((t, width), lambda b, p, i: (b * nq + i, p))
    return _attention_call(
        body, ex, "mla_fwd", (nb, groups, nq), [qp, kp, vm],
        [pl.BlockSpec((t, 2 * width), lambda b, p, i: (b * nq + i, p)),
         pl.BlockSpec((seq, 2 * width), lambda b, p, i: (b, p)),
         pl.BlockSpec((seq, width), lambda b, p, i: (b, p))],
        [jax.ShapeDtypeStruct((rows, MLA_W), F32), jax.ShapeDtypeStruct((rows, MLA_W), F32)],
        [ospec, ospec])


def _mla_bwd(qp, kp, vm, d_out, out, lse, seq, ex=None, chunk=64):
    rows = qp.shape[0]
    nb = rows // seq
    t = min(ATTN_TILE, seq)
    nq = seq // t
    shift = int(math.log2(chunk))
    ap = ATTN_PAIRS["mla_bwd"]
    width = ap * LANES
    groups = MLA_W // width
    hds = [(pp, hh) for pp in range(ap) for hh in range(2)]
    nh = len(hds)

    def body(q_ref, k_ref, v_ref, do_ref, o_ref, lse_ref, dq_ref, dk_ref, dv_ref):
        i = pl.program_id(2)

        @pl.when(i == 0)
        def _():
            dk_ref[...] = jnp.zeros_like(dk_ref)
            dv_ref[...] = jnp.zeros_like(dv_ref)

        lane = lax.broadcasted_iota(jnp.int32, (t, LANES), 1)
        row = lax.broadcasted_iota(jnp.int32, (t, t), 0)
        col = lax.broadcasted_iota(jnp.int32, (t, t), 1)
        allowed_diag = jnp.right_shift(col, shift) <= jnp.right_shift(row, shift)
        qhs = [q_ref[:, _pair(n)] for n in range(nh)]
        doms, deltas, lse_hs = [], [], []
        for pp, hh in hds:
            do = do_ref[:, _pair(pp)]
            d_o = do * o_ref[:, _pair(pp)]
            doms.append(do.astype(BF16) * _head_mask(lane, hh))
            deltas.append(jnp.sum(jnp.where((lane >= 64) if hh else (lane < 64), d_o, 0.0), axis=1, keepdims=True))
            lse_hs.append(jnp.sum(jnp.where(lane == hh, lse_ref[:, _pair(pp)], 0.0), axis=1, keepdims=True))

        def step(kb, dqs, allowed):
            start = pl.multiple_of(kb * t, t)
            vss = [v_ref[pl.ds(start, t), _pair(pp)] for pp in range(ap)]
            kss = [k_ref[pl.ds(start, t), _pair(n)] for n in range(nh)]
            scores = [_mla_scores(qhs[n], kss[n], allowed) for n in range(nh)]
            dps = [_dot(doms[n], vss[pp], NT) for n, (pp, _) in enumerate(hds)]
            ps = [jnp.exp(scores[n] - lse_hs[n]) for n in range(nh)]
            dss = [(ps[n] * (dps[n] - deltas[n]) * MLA_SCALE).astype(BF16) for n in range(nh)]
            for pp in range(ap):
                a, b = 2 * pp, 2 * pp + 1
                dv_ref[pl.ds(start, t), _pair(pp)] += (_dot(ps[a].astype(BF16), doms[a], TN)
                                                       + _dot(ps[b].astype(BF16), doms[b], TN))
            for n in range(nh):
                dk_ref[pl.ds(start, t), _pair(n)] += _dot(dss[n], qhs[n], TN)
            return tuple(dqs[n] + _dot(dss[n], kss[n]) for n in range(nh))

        dqs = lax.fori_loop(0, i, lambda kb, cr: step(kb, cr, None),
                            tuple(jnp.zeros((t, LANES), F32) for _ in range(nh)))
        dqs = step(i, dqs, allowed_diag)
        for n in range(nh):
            dq_ref[:, _pair(n)] = dqs[n]

    ospec = pl.BlockSpec((t, width), lambda b, p, i: (b * nq + i, p))
    return _attention_call(
        body, ex, "mla_bwd", (nb, groups, nq), [qp, kp, vm, d_out, out, lse],
        [pl.BlockSpec((t, 2 * width), lambda b, p, i: (b * nq + i, p)),
         pl.BlockSpec((seq, 2 * width), lambda b, p, i: (b, p)),
         pl.BlockSpec((seq, width), lambda b, p, i: (b, p)),
         pl.BlockSpec((t, width), lambda b, p, i: (b * nq + i, groups + p)),
         ospec, ospec],
        [jax.ShapeDtypeStruct((rows, HEADS * LANES), F32), jax.ShapeDtypeStruct((rows, HEADS * LANES), F32),
         jax.ShapeDtypeStruct((rows, MLA_W), F32)],
        [pl.BlockSpec((t, 2 * width), lambda b, p, i: (b * nq + i, p)),
         pl.BlockSpec((seq, 2 * width), lambda b, p, i: (b, p)),
         pl.BlockSpec((seq, width), lambda b, p, i: (b, p))])


PACK_COLS = 1024
PACK_ALIGN = 16
GROUP_IN = (384, ((1024, 552, 1), (384, 192, 1), (256, 256, 1)))
GROUP_MLP = (1152, ((1024, 1024, 1), (1024, 1024, 0), (256, 1024, 0)))


def _pack_rows(r, c):
    return (r // 2) * c // PACK_COLS


def _slot_rows(r, c):
    return -(-_pack_rows(r, c) // PACK_ALIGN) * PACK_ALIGN


def _join_slots(parts, group):
    total, weights = group
    padded = [jnp.pad(p, ((0, 0), (0, _slot_rows(r, c) - p.shape[1]), (0, 0))) for p, (r, c, _) in zip(parts, weights)]
    used = sum(_slot_rows(r, c) for r, c, _ in weights)
    if total > used:
        padded.append(jnp.zeros((parts[0].shape[0], total - used, PACK_COLS), parts[0].dtype))
    return jnp.concatenate(padded, axis=1)


def _split_slots(packed, group):
    out, at = [], 0
    for r, c, _ in group[1]:
        out.append(packed[:, at:at + _pack_rows(r, c), :])
        at += _slot_rows(r, c)
    return out


def _pack_halves(shards, group):
    return _join_slots([s.reshape(2, _pack_rows(r, c), PACK_COLS) for s, (r, c, _) in zip(shards, group[1])], group)


def _unpack_halves(packed, group):
    return [p.reshape(r, c) for p, (r, c, _) in zip(_split_slots(packed, group), group[1])]


def _unpack_full(gathered, group):
    out = []
    for p, (r, c, axis) in zip(_split_slots(gathered, group), group[1]):
        shards = p.reshape(4, r, c)
        out.append(shards.reshape(4 * r, c) if axis == 0 else jnp.moveaxis(shards, 0, 1).reshape(r, 4 * c))
    return out


def _pack_full(grads, group):
    parts = []
    for gr, (r, c, axis) in zip(grads, group[1]):
        shards = gr.reshape(4, r, c) if axis == 0 else jnp.moveaxis(gr.reshape(r, 4, c), 1, 0)
        parts.append(shards.reshape(8, _pack_rows(r, c), PACK_COLS))
    return _join_slots(parts, group)


def _pad_w_in(w_in):
    z = jnp.zeros((D_MODEL, 1), w_in.dtype)
    return jnp.concatenate([w_in[:, :2176], jnp.tile(z, (1, 64)), w_in[:, 2176:], jnp.tile(z, (1, 32))], axis=1)


def _unpad_w_in(g):
    return jnp.concatenate([g[:, :2176], g[:, 2240:2272]], axis=1)


def _pad_heads(w, used):
    k = w.shape[0]
    w3 = w.reshape(k, HEADS, used)
    return jnp.pad(w3, ((0, 0), (0, 0), (0, LANES - used))).reshape(k, HEADS * LANES)


def _unpad_heads(g, used):
    k = g.shape[0]
    return g.reshape(k, HEADS, LANES)[:, :, :used].reshape(k, HEADS * used)


def _rope_tables(seq):
    inv_freq = 1.0 / (ROPE_BASE ** (jnp.arange(0, ROPE, 2, dtype=F32) / ROPE))
    ang = jnp.arange(seq, dtype=F32)[:, None] * inv_freq[None, :]
    cos, sin = jnp.cos(ang), jnp.sin(ang)
    one, zero = jnp.ones((seq, NOPE), F32), jnp.zeros((seq, NOPE), F32)
    z16, z32 = jnp.zeros((seq, 16), F32), jnp.zeros((seq, 32), F32)
    cos_t = jnp.concatenate([one, cos, cos, jnp.ones((seq, 32), F32)], axis=1)
    sin_a = jnp.concatenate([zero, -sin, z16, z32], axis=1)
    sin_b = jnp.concatenate([zero, z16, sin, z32], axis=1)
    return cos_t, sin_a, sin_b


SMALL = (("ln_in_g", 1024), ("ln_in_b", 1024), ("b_ada", 6144), ("q_norm_g", 384), ("kv_norm_g", 256),
         ("ln1_g", 1024), ("ln1_b", 1024), ("ln2_g", 1024), ("ln2_b", 1024))
SMALL_TOTAL = sum(n for _, n in SMALL)
SMALL_ROWS = -(-SMALL_TOTAL // LANES // 8) * 8


def _pack_small(vals):
    flat = jnp.concatenate([v.reshape(-1) for v in vals])
    return jnp.pad(flat, (0, SMALL_ROWS * LANES - SMALL_TOTAL)).reshape(SMALL_ROWS, LANES)


def _unpack_small(packed, like):
    flat, out, at = packed.reshape(-1), [], 0
    for (_, n), ref in zip(SMALL, like):
        out.append(flat[at:at + n].reshape(ref.shape))
        at += n
    return out


def kernel(x, c, ln_in_g, ln_in_b, w_ada, b_ada, w_in, q_norm_g, kv_norm_g, w_uq, w_ukv, w_o, ln1_g, ln1_b, w_up, w_down, ln2_g, ln2_b, loss_target, m_ln_in_g, m_ln_in_b, m_w_ada, m_b_ada, m_w_in, m_q_norm_g, m_kv_norm_g, m_w_uq, m_w_ukv, m_w_o, m_ln1_g, m_ln1_b, m_w_up, m_w_down, m_ln2_g, m_ln2_b, v_ln_in_g, v_ln_in_b, v_w_ada, v_b_ada, v_w_in, v_q_norm_g, v_kv_norm_g, v_w_uq, v_w_ukv, v_w_o, v_ln1_g, v_ln1_b, v_w_up, v_w_down, v_ln2_g, v_ln2_b):
    nb, seq, _ = x.shape
    rows = nb * seq
    ix, iy, ic = lax.axis_index("x"), lax.axis_index("y"), lax.axis_index("c")
    chip = 2 * ix + iy
    dev = 2 * chip + ic

    def my_half(shards, group):
        packed = _pack_halves([s.astype(BF16) for s in shards], group)
        return lax.dynamic_index_in_dim(packed, ic, 0, keepdims=False)

    f_in, f_uq, f_ukv = _unpack_full(_gather8(my_half([w_in[0], w_uq[0], w_ukv[0]], GROUP_IN), "gather_w_in"),
                                     GROUP_IN)
    half_mlp = my_half([w_up[0], w_down[0], w_o[0]], GROUP_MLP)
    late_weights = _gather_exchange(half_mlp)
    w_in_p = _pad_w_in(f_in)
    uq3 = f_uq.reshape(Q_RANK, HEADS, NOPE + ROPE)
    w_uq_p = jnp.pad(uq3, ((0, 0), (0, 0), (0, LANES - NOPE - ROPE))).reshape(Q_RANK, HEADS * LANES)
    w_ukv_p = jnp.concatenate([_pad_heads(f_ukv[:, :HEADS * NOPE], NOPE), f_ukv[:, HEADS * NOPE:]], axis=1)

    n_all = 8 * nb
    c_all = _gather8(c.reshape(-1, LANES), "gather_c").reshape(n_all, D_MODEL)
    ada_cols = w_ada.shape[2]
    b_sh = lax.dynamic_slice_in_dim(b_ada, chip * ada_cols, ada_cols, axis=1)
    mod_sh = _ada_fwd(c_all, w_ada[0], b_sh)
    mod_g = _gather8(mod_sh, "gather_mod")[0::2]
    mod_all = jnp.moveaxis(mod_g, 0, 1).reshape(n_all, N_MOD * D_MODEL)
    mod_mine = lax.dynamic_slice_in_dim(mod_all, dev * nb, nb, axis=0).reshape(nb, N_MOD, D_MODEL)
    mod = jnp.pad(mod_mine, ((0, 0), (0, 8 - N_MOD), (0, 0)))

    cos_t, sin_a, sin_b = _rope_tables(seq)
    row2 = lambda v: v.reshape(1, -1)

    x2d = x.reshape(rows, D_MODEL)
    x0, h, qkv, lat, qp, kp, vm = _fwd_in(x2d, mod, row2(ln_in_g), row2(ln_in_b), w_in_p, q_norm_g, kv_norm_g,
                                          w_uq_p, w_ukv_p, cos_t, sin_a, sin_b, seq)
    sb_y, cars, g_mlp = _sb_fwd(qkv, seq, late_weights)
    g_mlp = _with_own(g_mlp, half_mlp)
    f_o = _split_slots(g_mlp, GROUP_MLP)[2].reshape(D_MODEL, D_MODEL)
    mla_y, lse = _mla_fwd(qp, kp, vm, seq)
    mix, y1, x1, h2 = _fwd_mix(sb_y, mla_y, x0, mod, f_o, ln1_g, ln1_b, seq)
    u, ff, y2 = _fwd_mlp(h2, x1, mod, g_mlp, seq)

    dy2, dff, du, acc2, dmod_a = _bwd_out(y2, loss_target.reshape(rows, D_MODEL), ff, u, mod, ln2_g, ln2_b, g_mlp, seq)
    dy1, dmix, d_attn, acc1, dmod_b = _bwd_mid(du, y1, mix, dy2, mod, ln1_g, ln1_b, g_mlp, f_o, seq)
    c_idx = ic.reshape(1).astype(jnp.int32)
    blocks_mlp = _wgrad_packed(h2, du, "wgrad_up", lambda i, j: 2 * j + i, 0)
    blocks_mlp = _wgrad_packed(u, dff, "wgrad_down", lambda i, j: i, 1, pre="relu2", into=blocks_mlp)
    blocks_mlp = _wgrad_packed(sb_y, dmix, "wgrad_o_sb", lambda i, j: 0, 8, split=4, into=blocks_mlp)
    blocks_mlp = _wgrad_packed(mla_y, dmix, "wgrad_o_mla", lambda i, j: 1, 8, split=4, into=blocks_mlp)
    dq_sb, dk_sb, dv_sb, sibling_mlp = _sb_bwd(qkv, d_attn, cars, seq, _swap_cores_exchange(blocks_mlp))
    part_mlp, part_mlp_bf = _add_pairs(blocks_mlp, sibling_mlp, c_idx, "grad_add_cores_mlp")
    dqp, dkp, dvm, chips_mlp = _mla_bwd(qp, kp, vm, d_attn, mla_y, lse, seq, _scatter_chips_exchange(part_mlp_bf))
    grad_x, dproj, dqall, dkv, latn, acc0, acc_lat, dmod_c = _bwd_in(
        dqp, dkp, dvm, dq_sb, dk_sb, dv_sb, lat, x2d, x0, dy1, mod, row2(ln_in_g), row2(ln_in_b), w_in_p,
        q_norm_g, kv_norm_g, w_uq_p, w_ukv_p, cos_t, sin_a, sin_b, seq)

    g_in = _unpad_w_in(_wgrad(h, dproj, "wgrad_in", tn=768))
    g_uq = _unpad_heads(_wgrad(latn[:, :Q_RANK], dqall, "wgrad_uq"), NOPE + ROPE)
    g_ukv_p = _wgrad(latn[:, Q_RANK:], dkv, "wgrad_ukv", tn=512)
    g_ukv = jnp.concatenate([_unpad_heads(g_ukv_p[:, :HEADS * LANES], NOPE), g_ukv_p[:, HEADS * LANES:]], axis=1)
    blocks_in = _pack_full([g_in, g_uq, g_ukv], GROUP_IN)
    sibling_in = _run_exchange(_swap_cores_exchange(blocks_in), "grads_in_to_sibling")
    part_in, part_in_bf = _add_pairs(blocks_in, sibling_in, c_idx, "grad_add_cores_in")
    chips_in = _run_exchange(_scatter_chips_exchange(part_in_bf), "grads_in_to_chips")

    def own(part):
        return lax.dynamic_index_in_dim(part, chip, 0, keepdims=False)

    half = jnp.concatenate([_add_chips(own(part_in), chips_in, "grad_add_chips_in"),
                            _add_chips(own(part_mlp), chips_mlp, "grad_add_chips_mlp")], axis=0)
    other = _run_exchange(_swap_one_exchange(half), "grads_halves")
    both = jnp.where(ic == 0, jnp.stack([half, other]), jnp.stack([other, half]))
    gs_in, gs_uq, gs_ukv = _unpack_halves(both[:, :GROUP_IN[0]], GROUP_IN)
    gs_up, gs_down, gs_o = _unpack_halves(both[:, GROUP_IN[0]:], GROUP_MLP)

    dmod = (dmod_a + dmod_b + dmod_c)[:, :N_MOD, :]
    small_part = _pack_small([acc0[0], acc0[1], jnp.zeros((N_MOD * D_MODEL,), F32), acc_lat[0, :Q_RANK],
                              acc_lat[1, :KV_RANK], acc1[0], acc1[1], acc2[0], acc2[1]])
    n_sum = SMALL_ROWS + D_MODEL // LANES
    payload = jnp.concatenate([small_part, acc2[2].reshape(-1, LANES), dmod.reshape(-1, LANES)], axis=0)
    gathered = _gather8(payload, "gather_small")
    small_sum = _sum_lead(gathered[:, :n_sum, :], "sum_small")
    loss = jnp.sum(small_sum[SMALL_ROWS:])
    dmod_all = gathered[:, n_sum:, :].reshape(n_all, N_MOD * D_MODEL)
    g_b_ada = _sum_lead(dmod_all.reshape(n_all, N_MOD * D_MODEL // LANES, LANES), "sum_b_ada").reshape(1, -1)
    dmod_sh = lax.dynamic_slice_in_dim(dmod_all, chip * ada_cols, ada_cols, axis=1)
    g_w_ada = _ada_bwd(c_all, dmod_sh)

    small_like = [ln_in_g, ln_in_b, b_ada, q_norm_g, kv_norm_g, ln1_g, ln1_b, ln2_g, ln2_b]
    small_grads = _unpack_small(small_sum, small_like)
    small_grads[2] = g_b_ada

    big_w = {"w_ada": (w_ada[0], g_w_ada, m_w_ada[0], v_w_ada[0]), "w_in": (w_in[0], gs_in, m_w_in[0], v_w_in[0]),
             "w_uq": (w_uq[0], gs_uq, m_w_uq[0], v_w_uq[0]), "w_ukv": (w_ukv[0], gs_ukv, m_w_ukv[0], v_w_ukv[0]),
             "w_o": (w_o[0], gs_o, m_w_o[0], v_w_o[0]), "w_up": (w_up[0], gs_up, m_w_up[0], v_w_up[0]),
             "w_down": (w_down[0], gs_down, m_w_down[0], v_w_down[0])}
    res = {}
    for name, (w, g, m, v) in big_w.items():
        d, mn, vn = _adamw(w, g, m, v, "adamw_" + name)
        res[name] = (g[None], d[None], mn[None], vn[None])
    small_m = [m_ln_in_g, m_ln_in_b, m_b_ada, m_q_norm_g, m_kv_norm_g, m_ln1_g, m_ln1_b, m_ln2_g, m_ln2_b]
    small_v = [v_ln_in_g, v_ln_in_b, v_b_ada, v_q_norm_g, v_kv_norm_g, v_ln1_g, v_ln1_b, v_ln2_g, v_ln2_b]
    sd, sm, sv = _adamw(_pack_small(small_like), _pack_small(small_grads), _pack_small(small_m), _pack_small(small_v),
                        "adamw_small")
    for (name, _), g, d, mn, vn in zip(SMALL, small_grads, _unpack_small(sd, small_like),
                                       _unpack_small(sm, small_like), _unpack_small(sv, small_like)):
        res[name] = (g, d, mn, vn)

    order = ["ln_in_g", "ln_in_b", "w_ada", "b_ada", "w_in", "q_norm_g", "kv_norm_g", "w_uq", "w_ukv", "w_o",
             "ln1_g", "ln1_b", "w_up", "w_down", "ln2_g", "ln2_b"]
    outs = [loss, grad_x.reshape(nb, seq, D_MODEL)]
    for k in range(4):
        outs += [res[name][k] for name in order]
    return tuple(outs)
```

```python
import functools
import math

import jax
import jax.numpy as jnp
from jax import lax
from jax.experimental import pallas as pl
from jax.experimental.pallas import tpu as pltpu

F32 = jnp.float32
BF16 = jnp.bfloat16
MESH_IDS = pl.DeviceIdType.MESH

D_MODEL = 1024
HEADS = 8
HEAD_PAIRS = HEADS // 2
SB_W = 512
MLA_W = 512
NOPE = 64
ROPE = 32
Q_RANK = 384
KV_RANK = 256
D_IN = 2208
D_IN_PAD = 2304
D_FF = 4096
N_MOD = 6
LN_EPS = 1e-5
RMS_EPS = 1e-6
ALPHA = 2.0 ** 0.25
ROPE_BASE = 10000.0
SB_SCALE = 64 ** -0.5
MLA_SCALE = 96 ** -0.5
ADAM_LR = 0.001
ADAM_B1 = 0.9
ADAM_B2 = 0.999
ADAM_EPS = 1e-08
ADAM_WD = 0.01
ADAM_STEP = 10

LANES = 128
ROW_TILE = 256
ATTN_TILE = 256
ATTN_PAIRS = {"sb_fwd": 4, "sb_bwd": 2, "mla_fwd": 4, "mla_bwd": 4}
VMEM_LIMIT = 56 << 20

NT = (((1,), (1,)), ((), ()))
TN = (((0,), (0,)), ((), ()))


def _params(sem=None):
    return pltpu.CompilerParams(vmem_limit_bytes=VMEM_LIMIT, dimension_semantics=sem)


def _const_spec(shape):
    zeros = (0,) * len(shape)
    return pl.BlockSpec(shape, lambda *_: zeros, pipeline_mode=pl.Buffered(1))


def _dot(a, b, dims=None):
    if dims is None:
        return jnp.dot(a, b, preferred_element_type=F32)
    return lax.dot_general(a, b, dims, preferred_element_type=F32)


def _mean(v):
    return jnp.mean(v, axis=-1, keepdims=True)


def _rowsum(v):
    return jnp.sum(v, axis=0, keepdims=True)


def _ln_fwd(y, g, b):
    mu = _mean(y)
    yc = y - mu
    rstd = lax.rsqrt(_mean(yc * yc) + LN_EPS)
    xhat = yc * rstd
    return xhat * g + b, xhat, rstd


def _ln_bwd(dx, xhat, rstd, g):
    dxh = dx * g
    return rstd * (dxh - _mean(dxh) - xhat * _mean(dxh * xhat))


def _rope(v, cos, sin_a, sin_b):
    return v * cos + pltpu.roll(v, 112, 1) * sin_a + pltpu.roll(v, 16, 1) * sin_b


def _rope_t(dv, cos, sin_a, sin_b):
    return dv * cos + pltpu.roll(dv * sin_a, 16, 1) + pltpu.roll(dv * sin_b, 112, 1)


def _my_place():
    return lax.axis_index("x"), lax.axis_index("y"), lax.axis_index("c")


class _Exchange:
    def __init__(self, operand, out_shape, n_copies, phases):
        self.operand = operand
        self.out_shape = out_shape
        self.phases = phases
        self.scratch = [pltpu.SemaphoreType.DMA((n_copies,)), pltpu.SemaphoreType.DMA((n_copies,))]


def _run_exchange(ex, name):
    def body(in_ref, out_ref, send_sems, recv_sems):
        for phase in ex.phases(in_ref, out_ref, send_sems, recv_sems):
            phase()

    return pl.pallas_call(
        body, name=name, out_shape=ex.out_shape,
        in_specs=[pl.BlockSpec(memory_space=pl.ANY)], out_specs=pl.BlockSpec(memory_space=pl.ANY),
        scratch_shapes=ex.scratch,
    )(ex.operand)


def _nothing():
    pass


def _gather_exchange(v):
    m, n = v.shape

    def phases(v_ref, out_ref, send_sems, recv_sems):
        x, y, c = _my_place()
        me, sibling = (x, y, c), (x, y, 1 - c)
        chips = [(1 - x, y), (x, 1 - y), (1 - x, 1 - y)]

        def rows(px, py, pc):
            return out_ref.at[4 * px + 2 * py + pc]

        def copy(k, block, to, src=None):
            return pltpu.make_async_remote_copy(
                src_ref=rows(*block) if src is None else src, dst_ref=rows(*block),
                send_sem=send_sems.at[k], recv_sem=recv_sems.at[k], device_id=to, device_id_type=MESH_IDS)

        first = [copy(0, me, sibling, src=v_ref)]
        first += [copy(1 + j, me, (*chip, c), src=v_ref) for j, chip in enumerate(chips)]
        passed = [copy(4 + j, (*chip, c), sibling) for j, chip in enumerate(chips)]

        def start():
            for cp in first:
                cp.start()

        def middle():
            for j, chip in enumerate(chips):
                copy(1 + j, (*chip, c), me).wait_recv()
                passed[j].start()

        def finish():
            copy(0, sibling, me).wait_recv()
            for j, chip in enumerate(chips):
                copy(4 + j, (*chip, 1 - c), me).wait_recv()
            for cp in first + passed:
                cp.wait_send()

        return start, middle, finish

    return _Exchange(v, jax.ShapeDtypeStruct((8, m, n), v.dtype), 7, phases)


def _with_own(gathered, v):
    dev = 4 * lax.axis_index("x") + 2 * lax.axis_index("y") + lax.axis_index("c")
    return lax.dynamic_update_index_in_dim(gathered, v, dev, 0)


def _direct_exchange(operand, out_shape, n_copies, make_copies):
    def phases(in_ref, out_ref, send_sems, recv_sems):
        copies = make_copies(in_ref, out_ref, send_sems, recv_sems)

        def start():
            for cp in copies:
                cp.start()

        def finish():
            for cp in copies:
                cp.wait()

        return start, _nothing, finish

    return _Exchange(operand, out_shape, n_copies, phases)


def _swap_cores_exchange(blocks):
    _, m, n = blocks.shape

    def make_copies(g_ref, out_ref, send_sems, recv_sems):
        x, y, c = _my_place()
        return [pltpu.make_async_remote_copy(
            src_ref=g_ref.at[2 * j + (1 - c)], dst_ref=out_ref.at[j],
            send_sem=send_sems.at[j], recv_sem=recv_sems.at[j],
            device_id=(x, y, 1 - c), device_id_type=MESH_IDS) for j in range(4)]

    return _direct_exchange(blocks, jax.ShapeDtypeStruct((4, m, n), blocks.dtype), 4, make_copies)


def _scatter_chips_exchange(parts):
    _, m, n = parts.shape
    flips = [(1, 0), (0, 1), (1, 1)]

    def make_copies(p_ref, out_ref, send_sems, recv_sems):
        x, y, c = _my_place()
        copies = []
        for k, (fx, fy) in enumerate(flips):
            tx = 1 - x if fx else x
            ty = 1 - y if fy else y
            copies.append(pltpu.make_async_remote_copy(
                src_ref=p_ref.at[2 * tx + ty], dst_ref=out_ref.at[k],
                send_sem=send_sems.at[k], recv_sem=recv_sems.at[k],
                device_id=(tx, ty, c), device_id_type=MESH_IDS))
        return copies

    return _direct_exchange(parts, jax.ShapeDtypeStruct((3, m, n), parts.dtype), 3, make_copies)


def _swap_one_exchange(v):
    def make_copies(v_ref, out_ref, send_sems, recv_sems):
        x, y, c = _my_place()
        return [pltpu.make_async_remote_copy(src_ref=v_ref, dst_ref=out_ref, send_sem=send_sems.at[0],
                                             recv_sem=recv_sems.at[0], device_id=(x, y, 1 - c),
                                             device_id_type=MESH_IDS)]

    return _direct_exchange(v, jax.ShapeDtypeStruct(v.shape, v.dtype), 1, make_copies)


def _gather8(v, name):
    return _with_own(_run_exchange(_gather_exchange(v), name), v)


def _carried(ex, refs, n_in, n_out):
    ins, ex_in = refs[:n_in], refs[n_in]
    outs, ex_out = refs[n_in + 1:n_in + 1 + n_out], refs[n_in + 1 + n_out]
    return ins, outs, ex.phases(ex_in, ex_out, *refs[n_in + 2 + n_out:])


def _ada_fwd(c_all, w_ada_sh, b_ada_sh):
    nb, cols = c_all.shape[0], w_ada_sh.shape[1]
    tn = 512

    def body(c_ref, w_ref, b_ref, o_ref):
        cv = c_ref[...]
        act = (cv * jax.nn.sigmoid(cv)).astype(BF16)
        o_ref[...] = _dot(act, w_ref[...].astype(BF16)) + b_ref[...]

    return pl.pallas_call(
        body, name="ada_fwd", grid=(cols // tn,),
        out_shape=jax.ShapeDtypeStruct((nb, cols), F32),
        in_specs=[pl.BlockSpec((nb, D_MODEL), lambda j: (0, 0)),
                  pl.BlockSpec((D_MODEL, tn), lambda j: (0, j)),
                  pl.BlockSpec((1, tn), lambda j: (0, j))],
        out_specs=pl.BlockSpec((nb, tn), lambda j: (0, j)),
        compiler_params=_params(("arbitrary",)),
    )(c_all, w_ada_sh, b_ada_sh)


def _ada_bwd(c_all, dmod_sh):
    nb, cols = dmod_sh.shape
    tn = 512

    def body(c_ref, d_ref, o_ref):
        cv = c_ref[...]
        act = (cv * jax.nn.sigmoid(cv)).astype(BF16)
        o_ref[...] = _dot(act, d_ref[...].astype(BF16), TN)

    return pl.pallas_call(
        body, name="ada_bwd", grid=(cols // tn,),
        out_shape=jax.ShapeDtypeStruct((D_MODEL, cols), F32),
        in_specs=[pl.BlockSpec((nb, D_MODEL), lambda j: (0, 0)),
                  pl.BlockSpec((nb, tn), lambda j: (0, j))],
        out_specs=pl.BlockSpec((D_MODEL, tn), lambda j: (0, j)),
        compiler_params=_params(("arbitrary",)),
    )(c_all, dmod_sh)


def _sum_lead(v, name):
    k, m, n = v.shape

    def body(v_ref, o_ref):
        acc = v_ref[0]
        for i in range(1, k):
            acc = acc + v_ref[i]
        o_ref[...] = acc

    return pl.pallas_call(
        body, name=name, out_shape=jax.ShapeDtypeStruct((m, n), F32),
        in_specs=[pl.BlockSpec((k, m, n), lambda: (0, 0, 0))],
        out_specs=pl.BlockSpec((m, n), lambda: (0, 0)),
        compiler_params=_params(),
    )(v)


def _adamw(w, g, m, v, name):
    rows, cols = w.shape
    tr = rows
    while tr * cols * 4 > (2 << 20) and tr % 16 == 0:
        tr //= 2

    def body(w_ref, g_ref, m_ref, v_ref, d_ref, mo_ref, vo_ref):
        gv = g_ref[...]
        mn = ADAM_B1 * m_ref[...] + (1.0 - ADAM_B1) * gv
        vn = ADAM_B2 * v_ref[...] + (1.0 - ADAM_B2) * (gv * gv)
        m_hat = mn / (1.0 - ADAM_B1 ** ADAM_STEP)
        v_hat = vn / (1.0 - ADAM_B2 ** ADAM_STEP)
        d_ref[...] = -ADAM_LR * (m_hat / (jnp.sqrt(v_hat) + ADAM_EPS) + ADAM_WD * w_ref[...])
        mo_ref[...] = mn
        vo_ref[...] = vn

    spec = pl.BlockSpec((tr, cols), lambda i: (i, 0))
    shape = jax.ShapeDtypeStruct((rows, cols), F32)
    return pl.pallas_call(
        body, name=name, grid=(rows // tr,), out_shape=(shape, shape, shape),
        in_specs=[spec, spec, spec, spec], out_specs=(spec, spec, spec),
        compiler_params=_params(("arbitrary",)),
    )(w, g, m, v)


def _add_rows(m, n):
    fits = [d for d in range(16, m + 1, 16) if m % d == 0 and d * n * 4 <= (5 << 19)]
    assert fits, (m, n)
    return max(fits)


def _add_pairs(blocks, recv, c_idx, name):
    _, m, n = blocks.shape
    tr = _add_rows(m, n)

    def body(c_ref, a_ref, b_ref, o_ref, ob_ref):
        s = a_ref[...] + b_ref[...]
        o_ref[...] = s
        ob_ref[...] = s.astype(BF16)

    grid_spec = pltpu.PrefetchScalarGridSpec(
        num_scalar_prefetch=1, grid=(4, m // tr),
        in_specs=[pl.BlockSpec((1, tr, n), lambda j, i, c: (2 * j + c[0], i, 0)),
                  pl.BlockSpec((1, tr, n), lambda j, i, c: (j, i, 0))],
        out_specs=(pl.BlockSpec((1, tr, n), lambda j, i, c: (j, i, 0)),
                   pl.BlockSpec((1, tr, n), lambda j, i, c: (j, i, 0))))
    return pl.pallas_call(
        body, name=name, grid_spec=grid_spec,
        out_shape=(jax.ShapeDtypeStruct((4, m, n), F32), jax.ShapeDtypeStruct((4, m, n), BF16)),
        compiler_params=_params(("arbitrary", "arbitrary")),
    )(c_idx, blocks, recv)


def _add_chips(own, recv, name):
    m, n = own.shape
    tr = _add_rows(m, n)

    def body(a_ref, r_ref, o_ref):
        acc = a_ref[...]
        for k in range(3):
            acc = acc + r_ref[k].astype(F32)
        o_ref[...] = acc

    return pl.pallas_call(
        body, name=name, grid=(m // tr,),
        out_shape=jax.ShapeDtypeStruct((m, n), F32),
        in_specs=[pl.BlockSpec((tr, n), lambda i: (i, 0)), pl.BlockSpec((3, tr, n), lambda i: (0, i, 0))],
        out_specs=pl.BlockSpec((tr, n), lambda i: (i, 0)),
        compiler_params=_params(("arbitrary",)),
    )(own, recv)


def _row_spec(cols):
    return pl.BlockSpec((ROW_TILE, cols), lambda i: (i, 0))


def _mod_spec(tiles_per_seq):
    return pl.BlockSpec((1, 8, D_MODEL), lambda i: (i // tiles_per_seq, 0, 0))


def _table_spec(tiles_per_seq):
    return pl.BlockSpec((ROW_TILE, LANES), lambda i: (i % tiles_per_seq, 0))


def _fwd_in(x, mod, ln_g, ln_b, w_in, q_g, kv_g, w_uq, w_ukv, cos_t, sin_a, sin_b, seq):
    rows = x.shape[0]
    tm = ROW_TILE
    tps = seq // tm

    def body(x_ref, mod_ref, g_ref, b_ref, win_ref, qg_ref, kvg_ref, wuq_ref, wukv_ref, cos_ref, sa_ref, sb_ref,
             x0_ref, h_ref, qkv_ref, lat_ref, qp_ref, kp_ref, vm_ref):
        x0, _, _ = _ln_fwd(x_ref[...], g_ref[...], b_ref[...])
        x0_ref[...] = x0
        h = (x0 * (1.0 + mod_ref[0, 1:2, :]) + mod_ref[0, 0:1, :]).astype(BF16)
        h_ref[...] = h
        proj = _dot(h, win_ref[...])
        qkv_ref[:, :SB_W] = (proj[:, :SB_W] * SB_SCALE).astype(BF16)
        qkv_ref[:, SB_W:] = proj[:, SB_W:3 * SB_W].astype(BF16)
        lat_ref[...] = proj[:, 3 * SB_W:3 * SB_W + Q_RANK + KV_RANK]
        cq = proj[:, 3 * SB_W:3 * SB_W + Q_RANK]
        ckv = proj[:, 3 * SB_W + Q_RANK:3 * SB_W + Q_RANK + KV_RANK]
        kr = proj[:, D_IN_PAD - LANES:]
        cos, sa, sb = cos_ref[...], sa_ref[...], sb_ref[...]
        cqn = (cq * lax.rsqrt(_mean(cq * cq) + RMS_EPS) * qg_ref[...]).astype(BF16)
        q_all = _dot(cqn, wuq_ref[...])
        for hd in range(HEADS):
            sl = slice(hd * LANES, (hd + 1) * LANES)
            qp_ref[:, sl] = _rope(q_all[:, sl], cos, sa, sb).astype(BF16)
        ckvn = (ckv * lax.rsqrt(_mean(ckv * ckv) + RMS_EPS) * kvg_ref[...]).astype(BF16)
        kv = _dot(ckvn, wukv_ref[...])
        kr_rot = _rope(kr, cos, sa, sb)
        for hd in range(HEADS):
            sl = slice(hd * LANES, (hd + 1) * LANES)
            kp_ref[:, sl] = (kv[:, sl] + kr_rot).astype(BF16)
        vm_ref[...] = kv[:, HEADS * LANES:].astype(BF16)

    outs = [(D_MODEL, F32), (D_MODEL, BF16), (3 * SB_W, BF16), (Q_RANK + KV_RANK, F32),
            (HEADS * LANES, BF16), (HEADS * LANES, BF16), (MLA_W, BF16)]
    return pl.pallas_call(
        body, name="fwd_in", grid=(rows // tm,),
        out_shape=tuple(jax.ShapeDtypeStruct((rows, n), dt) for n, dt in outs),
        in_specs=[_row_spec(D_MODEL), _mod_spec(tps), _const_spec((1, D_MODEL)), _const_spec((1, D_MODEL)),
                  _const_spec(w_in.shape), _const_spec((1, Q_RANK)), _const_spec((1, KV_RANK)),
                  _const_spec(w_uq.shape), _const_spec(w_ukv.shape),
                  _table_spec(tps), _table_spec(tps), _table_spec(tps)],
        out_specs=tuple(_row_spec(n) for n, _ in outs),
        compiler_params=_params(("arbitrary",)),
    )(x, mod, ln_g, ln_b, w_in, q_g, kv_g, w_uq, w_ukv, cos_t, sin_a, sin_b)


HALF = 512
SHARD = 1024


def _mlp_weight_specs():
    return [pl.BlockSpec((8, HALF, SHARD), lambda i: (0, 0, 0), pipeline_mode=pl.Buffered(1)),
            pl.BlockSpec((8, HALF, SHARD), lambda i: (0, 1, 0), pipeline_mode=pl.Buffered(1))]


def _fwd_out(sb_y, mla_y, x0, mod, w_o, ln_g, ln_b, g_mlp, seq):
    rows = x0.shape[0]
    tm = ROW_TILE
    tps = seq // tm

    def body(sb_ref, ml_ref, x0_ref, mod_ref, wo_ref, g_ref, b_ref, wu_ref, wd_ref,
             mix_ref, y1_ref, h2_ref, u_ref, ff_ref, y2_ref):
        mix = _dot(sb_ref[...], wo_ref[:SB_W, :]) + _dot(ml_ref[...].astype(BF16), wo_ref[SB_W:, :])
        mix_ref[...] = mix
        y1 = ALPHA * x0_ref[...] + (1.0 + mod_ref[0, 2:3, :]) * mix
        y1_ref[...] = y1
        x1, _, _ = _ln_fwd(y1, g_ref[...], b_ref[...])
        h2 = (x1 * (1.0 + mod_ref[0, 4:5, :]) + mod_ref[0, 3:4, :]).astype(BF16)
        h2_ref[...] = h2
        h_lo, h_hi = h2[:, :HALF], h2[:, HALF:]
        ff = jnp.zeros((tm, D_MODEL), F32)
        for chip in range(4):
            u = _dot(h_lo, wu_ref[2 * chip]) + _dot(h_hi, wu_ref[2 * chip + 1])
            u_ref[:, chip * SHARD:(chip + 1) * SHARD] = u.astype(BF16)
            act = jnp.square(jnp.maximum(u, 0.0)).astype(BF16)
            ff = ff + _dot(act[:, :HALF], wd_ref[2 * chip]) + _dot(act[:, HALF:], wd_ref[2 * chip + 1])
        ff_ref[...] = ff
        y2_ref[...] = ALPHA * x1 + (1.0 + mod_ref[0, 5:6, :]) * ff

    outs = [(D_MODEL, F32), (D_MODEL, F32), (D_MODEL, BF16), (D_FF, BF16), (D_MODEL, F32), (D_MODEL, F32)]
    return pl.pallas_call(
        body, name="fwd_out", grid=(rows // tm,),
        out_shape=tuple(jax.ShapeDtypeStruct((rows, n), dt) for n, dt in outs),
        in_specs=[_row_spec(SB_W), _row_spec(MLA_W), _row_spec(D_MODEL), _mod_spec(tps), _const_spec(w_o.shape),
                  _const_spec((1, D_MODEL)), _const_spec((1, D_MODEL))] + _mlp_weight_specs(),
        out_specs=tuple(_row_spec(n) for n, _ in outs),
        compiler_params=_params(("arbitrary",)),
    )(sb_y, mla_y, x0, mod, w_o, ln_g, ln_b, g_mlp, g_mlp)


def _acc_spec(rows=8, cols=D_MODEL):
    return pl.BlockSpec((rows, cols), lambda i: (0, 0))


def _bwd_out(y2, tgt, ff, u, y1, mix, mod, ln2_g, ln2_b, ln1_g, ln1_b, g_mlp, w_o, seq):
    rows = y2.shape[0]
    nb = rows // seq
    tm = ROW_TILE
    tps = seq // tm

    def body(y2_ref, t_ref, ff_ref, u_ref, y1_ref, mix_ref, mod_ref, g2_ref, b2_ref, g_ref, b_ref, wu_ref, wd_ref,
             wo_ref, dy1_ref, dmix_ref, do_ref, dff_ref, du_ref, acc_ref, dmod_ref):
        i = pl.program_id(0)

        @pl.when(i == 0)
        def _():
            acc_ref[...] = jnp.zeros_like(acc_ref)

        @pl.when(i % tps == 0)
        def _():
            dmod_ref[...] = jnp.zeros_like(dmod_ref)

        g2 = g2_ref[...]
        x2, xhat2, rstd2 = _ln_fwd(y2_ref[...], g2, b2_ref[...])
        err = x2 - t_ref[...]
        dx2 = err * (1.0 / D_MODEL)
        acc_ref[0:1, :] += _rowsum(dx2 * xhat2)
        acc_ref[1:2, :] += _rowsum(dx2)
        acc_ref[2:3, :] += _rowsum(err * err) * (0.5 / D_MODEL)
        dy2 = _ln_bwd(dx2, xhat2, rstd2, g2)
        dmod_ref[0, 5:6, :] += _rowsum(dy2 * ff_ref[...])
        dff = ((1.0 + mod_ref[0, 5:6, :]) * dy2).astype(BF16)
        dff_ref[...] = dff
        for blk in range(8):
            cols = slice(blk * HALF, (blk + 1) * HALF)
            da = _dot(dff, wd_ref[blk], NT)
            du_ref[:, cols] = (da * (2.0 * jnp.maximum(u_ref[:, cols].astype(F32), 0.0))).astype(BF16)

        g = g_ref[...]
        x1, xhat, rstd = _ln_fwd(y1_ref[...], g, b_ref[...])
        halves = []
        for half in range(2):
            acc = jnp.zeros((tm, HALF), F32)
            for chip in range(4):
                acc = acc + _dot(du_ref[:, chip * SHARD:(chip + 1) * SHARD], wu_ref[2 * chip + half], NT)
            halves.append(acc)
        dh2 = jnp.concatenate(halves, axis=1)
        dmod_ref[0, 3:4, :] += _rowsum(dh2)
        dmod_ref[0, 4:5, :] += _rowsum(dh2 * x1)
        dx1 = ALPHA * dy2 + dh2 * (1.0 + mod_ref[0, 4:5, :])
        acc_ref[3:4, :] += _rowsum(dx1 * xhat)
        acc_ref[4:5, :] += _rowsum(dx1)
        dy1 = _ln_bwd(dx1, xhat, rstd, g)
        dy1_ref[...] = dy1
        dmod_ref[0, 2:3, :] += _rowsum(dy1 * mix_ref[...])
        dmix = ((1.0 + mod_ref[0, 2:3, :]) * dy1).astype(BF16)
        dmix_ref[...] = dmix
        do_ref[...] = _dot(dmix, wo_ref[...], NT)

    outs = [(D_MODEL, F32), (D_MODEL, BF16), (D_MODEL, F32), (D_MODEL, BF16), (D_FF, BF16)]
    return pl.pallas_call(
        body, name="bwd_out", grid=(rows // tm,),
        out_shape=tuple(jax.ShapeDtypeStruct((rows, n), dt) for n, dt in outs)
        + (jax.ShapeDtypeStruct((8, D_MODEL), F32), jax.ShapeDtypeStruct((nb, 8, D_MODEL), F32)),
        in_specs=[_row_spec(D_MODEL), _row_spec(D_MODEL), _row_spec(D_MODEL), _row_spec(D_FF), _row_spec(D_MODEL),
                  _row_spec(D_MODEL), _mod_spec(tps), _const_spec((1, D_MODEL)), _const_spec((1, D_MODEL)),
                  _const_spec((1, D_MODEL)), _const_spec((1, D_MODEL))] + _mlp_weight_specs()
        + [_const_spec(w_o.shape)],
        out_specs=tuple(_row_spec(n) for n, _ in outs) + (_acc_spec(), _mod_spec(tps)),
        compiler_params=_params(("arbitrary",)),
    )(y2, tgt, ff, u, y1, mix, mod, ln2_g, ln2_b, ln1_g, ln1_b, g_mlp, g_mlp, w_o)


def _bwd_in(dqp, dkp, dvm, dq_sb, dk_sb, dv_sb, lat, x, x0, dy1, mod, ln_g, ln_b, w_in, q_g, kv_g, w_uq, w_ukv,
            cos_t, sin_a, sin_b, seq):
    rows = x.shape[0]
    nb = rows // seq
    tm = ROW_TILE
    tps = seq // tm
    n_lat = Q_RANK + KV_RANK

    def body(dqp_ref, dkp_ref, dvm_ref, dqs_ref, dks_ref, dvs_ref, lat_ref, x_ref, x0_ref, dy1_ref, mod_ref,
             g_ref, b_ref, win_ref, qg_ref, kvg_ref, wuq_ref, wukv_ref, cos_ref, sa_ref, sb_ref,
             dx_ref, dproj_ref, dqall_ref, dkv_ref, latn_ref, acc_ref, accl_ref, dmod_ref):
        i = pl.program_id(0)

        @pl.when(i == 0)
        def _():
            acc_ref[...] = jnp.zeros_like(acc_ref)
            accl_ref[...] = jnp.zeros_like(accl_ref)

        @pl.when(i % tps == 0)
        def _():
            dmod_ref[...] = jnp.zeros_like(dmod_ref)

        cos, sa, sb = cos_ref[...], sa_ref[...], sb_ref[...]
        lane = lax.broadcasted_iota(jnp.int32, (tm, LANES), 1)
        for hd in range(HEADS):
            sl = slice(hd * LANES, (hd + 1) * LANES)
            dqall_ref[:, sl] = _rope_t(dqp_ref[:, sl], cos, sa, sb).astype(BF16)
        dcqn = _dot(dqall_ref[...], wuq_ref[...], NT)
        cq = lat_ref[:, :Q_RANK]
        qg = qg_ref[...]
        rq = lax.rsqrt(_mean(cq * cq) + RMS_EPS)
        cqn = cq * rq
        latn_ref[:, :Q_RANK] = (cqn * qg).astype(BF16)
        accl_ref[0:1, :Q_RANK] += _rowsum(dcqn * cqn)
        dqg = dcqn * qg
        dcq = rq * (dqg - cqn * _mean(dqg * cqn))
        dkr = jnp.zeros((tm, LANES), F32)
        for hd in range(HEADS):
            sl = slice(hd * LANES, (hd + 1) * LANES)
            dk = dkp_ref[:, sl]
            dkr = dkr + dk
            dkv_ref[:, sl] = jnp.where(lane < NOPE, dk, 0.0).astype(BF16)
        dkv_ref[:, HEADS * LANES:] = dvm_ref[...].astype(BF16)
        dckvn = _dot(dkv_ref[...], wukv_ref[...], NT)
        ckv = lat_ref[:, Q_RANK:]
        kvg = kvg_ref[...]
        rkv = lax.rsqrt(_mean(ckv * ckv) + RMS_EPS)
        ckvn = ckv * rkv
        latn_ref[:, Q_RANK:] = (ckvn * kvg).astype(BF16)
        accl_ref[1:2, :KV_RANK] += _rowsum(dckvn * ckvn)
        dkg = dckvn * kvg
        dckv = rkv * (dkg - ckvn * _mean(dkg * ckvn))
        dkr = _rope_t(jnp.where(lane >= NOPE, dkr, 0.0), cos, sa, sb)
        dproj_ref[:, :SB_W] = dqs_ref[...]
        dproj_ref[:, SB_W:2 * SB_W] = dks_ref[...].astype(BF16)
        dproj_ref[:, 2 * SB_W:3 * SB_W] = dvs_ref[...].astype(BF16)
        dproj_ref[:, 3 * SB_W:3 * SB_W + Q_RANK] = dcq.astype(BF16)
        dproj_ref[:, 3 * SB_W + Q_RANK:3 * SB_W + n_lat] = dckv.astype(BF16)
        dproj_ref[:, D_IN_PAD - LANES:] = dkr.astype(BF16)
        dh = _dot(dproj_ref[...], win_ref[...], NT)
        x0 = x0_ref[...]
        dmod_ref[0, 0:1, :] += _rowsum(dh)
        dmod_ref[0, 1:2, :] += _rowsum(dh * x0)
        dx0 = ALPHA * dy1_ref[...] + dh * (1.0 + mod_ref[0, 1:2, :])
        g = g_ref[...]
        _, xhat, rstd = _ln_fwd(x_ref[...], g, b_ref[...])
        acc_ref[0:1, :] += _rowsum(dx0 * xhat)
        acc_ref[1:2, :] += _rowsum(dx0)
        dx_ref[...] = _ln_bwd(dx0, xhat, rstd, g)

    outs = [(D_MODEL, F32), (D_IN_PAD, BF16), (HEADS * LANES, BF16), (HEADS * LANES + MLA_W, BF16), (n_lat, BF16)]
    return pl.pallas_call(
        body, name="bwd_in", grid=(rows // tm,),
        out_shape=tuple(jax.ShapeDtypeStruct((rows, n), dt) for n, dt in outs)
        + (jax.ShapeDtypeStruct((8, D_MODEL), F32), jax.ShapeDtypeStruct((8, Q_RANK), F32),
           jax.ShapeDtypeStruct((nb, 8, D_MODEL), F32)),
        in_specs=[_row_spec(HEADS * LANES), _row_spec(HEADS * LANES), _row_spec(MLA_W),
                  _row_spec(SB_W), _row_spec(SB_W), _row_spec(SB_W), _row_spec(n_lat),
                  _row_spec(D_MODEL), _row_spec(D_MODEL), _row_spec(D_MODEL), _mod_spec(tps),
                  _const_spec((1, D_MODEL)), _const_spec((1, D_MODEL)), _const_spec(w_in.shape),
                  _const_spec((1, Q_RANK)), _const_spec((1, KV_RANK)), _const_spec(w_uq.shape),
                  _const_spec(w_ukv.shape), _table_spec(tps), _table_spec(tps), _table_spec(tps)],
        out_specs=tuple(_row_spec(n) for n, _ in outs) + (_acc_spec(), _acc_spec(8, Q_RANK), _mod_spec(tps)),
        compiler_params=_params(("arbitrary",)),
    )(dqp, dkp, dvm, dq_sb, dk_sb, dv_sb, lat, x, x0, dy1, mod, ln_g, ln_b, w_in, q_g, kv_g, w_uq, w_ukv,
      cos_t, sin_a, sin_b)


def _wgrad(a, b, name, pre=None, tm=512, tn=1024, tk=2048):
    rows, m = a.shape
    n = b.shape[1]
    tm, tn, tk = min(tm, m), min(tn, n), min(tk, rows)
    if m % tm:
        tm = m
    if n % tn:
        tn = n

    def body(a_ref, b_ref, o_ref):
        @pl.when(pl.program_id(2) == 0)
        def _():
            o_ref[...] = jnp.zeros_like(o_ref)

        av = a_ref[...]
        if pre == "relu2":
            av = jnp.square(jnp.maximum(av.astype(F32), 0.0))
        o_ref[...] += _dot(av.astype(BF16), b_ref[...].astype(BF16), TN)

    return pl.pallas_call(
        body, name=name, grid=(m // tm, n // tn, rows // tk),
        out_shape=jax.ShapeDtypeStruct((m, n), F32),
        in_specs=[pl.BlockSpec((tk, tm), lambda i, j, k: (k, i)), pl.BlockSpec((tk, tn), lambda i, j, k: (k, j))],
        out_specs=pl.BlockSpec((tm, tn), lambda i, j, k: (i, j)),
        compiler_params=_params(("arbitrary", "arbitrary", "arbitrary")),
    )(a, b)


def _wgrad_packed(a, b, name, block_of, row_block, split=1, pre=None, into=None, tk=2048):
    rows, m = a.shape
    n = b.shape[1]
    tm = HALF
    part = tm // split
    tk = min(tk, rows)
    shape = jax.ShapeDtypeStruct((8, GROUP_MLP[0], PACK_COLS), F32)

    def body(a_ref, b_ref, *rest):
        o_ref = rest[-1]

        @pl.when(pl.program_id(2) == 0)
        def _():
            o_ref[...] = jnp.zeros_like(o_ref)

        av = a_ref[...]
        if pre == "relu2":
            av = jnp.square(jnp.maximum(av.astype(F32), 0.0))
        prod = _dot(av.astype(BF16), b_ref[...].astype(BF16), TN)
        for s in range(split):
            o_ref[s] += prod[s * part:(s + 1) * part]

    in_specs = [pl.BlockSpec((tk, tm), lambda i, j, k: (k, i)), pl.BlockSpec((tk, SHARD), lambda i, j, k: (k, j))]
    operands = [a, b]
    if into is not None:
        in_specs.append(pl.BlockSpec(memory_space=pl.ANY))
        operands.append(into)
    return pl.pallas_call(
        body, name=name, grid=(m // tm, n // SHARD, rows // tk), out_shape=shape,
        in_specs=in_specs,
        out_specs=pl.BlockSpec((split, part, SHARD), lambda i, j, k: (block_of(i, j), row_block, 0)),
        input_output_aliases={} if into is None else {2: 0},
        compiler_params=_params(("arbitrary", "arbitrary", "arbitrary")),
    )(*operands)


def _pair(pp):
    return slice(pp * LANES, (pp + 1) * LANES)


def _head_mask(lane, hh):
    return jnp.where((lane >= 64) if hh else (lane < 64), 1.0, 0.0).astype(BF16)


def _tri(t, kind):
    r = lax.broadcasted_iota(jnp.int32, (t, t), 0)
    c = lax.broadcasted_iota(jnp.int32, (t, t), 1)
    one = jnp.where(r > c if kind == "suffix" else r < c, 1.0, 0.0).astype(BF16)
    return jnp.concatenate([one, one], axis=0)


def _split_dot(v, tri2):
    hi = v.astype(BF16)
    lo = (v - hi.astype(F32)).astype(BF16)
    return _dot(jnp.concatenate([hi, lo], axis=1), tri2)


def _sb_logits(z, valid):
    log_keep = -(jnp.maximum(z, 0.0) + jnp.log(1.0 + jnp.exp(-jnp.abs(z))))
    log_beta = z + log_keep
    if valid is not None:
        log_keep = jnp.where(valid, log_keep, 0.0)
    return log_keep, log_beta


def _attention_call(body, ex, name, grid, operands, in_specs, out_shapes, out_specs):
    n_in, n_out = len(operands), len(out_shapes)
    total = grid[0] * grid[1] * grid[2]
    any_spec = pl.BlockSpec(memory_space=pl.ANY)

    def carrier(*refs):
        ins, outs, (start, middle, finish) = _carried(ex, refs, n_in, n_out)
        step = (pl.program_id(0) * grid[1] + pl.program_id(1)) * grid[2] + pl.program_id(2)
        pl.when(step == 0)(start)
        pl.when(step == total // 2)(middle)
        body(*ins, *outs)
        pl.when(step == total - 1)(finish)

    carried = ex is not None
    return pl.pallas_call(
        carrier if carried else body, name=name, grid=grid,
        out_shape=tuple(out_shapes) + ((ex.out_shape,) if carried else ()),
        in_specs=list(in_specs) + ([any_spec] if carried else []),
        out_specs=tuple(out_specs) + ((any_spec,) if carried else ()),
        scratch_shapes=ex.scratch if carried else [],
        compiler_params=_params(("arbitrary", "arbitrary", "arbitrary")),
    )(*operands, *([ex.operand] if carried else []))


def _sb_fwd(qkv, seq, ex=None):
    rows = qkv.shape[0]
    nb = rows // seq
    t = min(ATTN_TILE, seq)
    nq = seq // t
    ap = ATTN_PAIRS["sb_fwd"]
    width = ap * LANES
    groups = SB_W // width
    hds = [(pp, hh) for pp in range(ap) for hh in range(2)]

    def body(q_ref, k_ref, v_ref, tri_ref, o_ref, car_ref):
        i = pl.program_id(2)
        lane = lax.broadcasted_iota(jnp.int32, (t, LANES), 1)
        row = lax.broadcasted_iota(jnp.int32, (t, t), 0)
        col = lax.broadcasted_iota(jnp.int32, (t, t), 1)
        strict = col < row
        tri = tri_ref[...]
        masks = [_head_mask(lane, hh) for hh in range(2)]
        qms = [q_ref[:, _pair(pp)] * masks[hh] for pp, hh in hds]

        def step(kb, carry, valid):
            c_sums, accs, cars = carry
            accs, cars = list(accs), list(cars)
            start = pl.multiple_of(kb * t, t)
            kss = [k_ref[pl.ds(start, t), _pair(pp)] for pp in range(ap)]
            vss = [v_ref[pl.ds(start, t), _pair(pp)] for pp in range(ap)]
            zs = [_dot(qms[n], kss[pp], NT) for n, (pp, _) in enumerate(hds)]
            logs = [_sb_logits(z, valid) for z in zs]
            sufs = [_split_dot(lg[0], tri) for lg in logs]
            new_sums = []
            for n, (pp, hh) in enumerate(hds):
                log_keep, log_beta = logs[n]
                w = jnp.exp(log_beta + sufs[n] + c_sums[n])
                if valid is not None:
                    w = jnp.where(valid, w, 0.0)
                accs[pp] = accs[pp] + _dot(w.astype(BF16), vss[pp] * masks[hh])
                cars[pp] = jnp.where(lane == hh * 8 + kb, c_sums[n], cars[pp])
                new_sums.append(c_sums[n] + jnp.sum(log_keep, axis=1, keepdims=True))
            return tuple(new_sums), tuple(accs), tuple(cars)

        zeros = tuple(jnp.zeros((t, LANES), F32) for _ in range(ap))
        carry = step(i, (tuple(jnp.zeros((t, 1), F32) for _ in hds), zeros, zeros), strict)
        _, accs, cars = lax.fori_loop(0, i, lambda j, cr: step(i - 1 - j, cr, None), carry)
        for pp in range(ap):
            o_ref[:, _pair(pp)] = accs[pp].astype(BF16)
            car_ref[:, _pair(pp)] = cars[pp]

    qspec = pl.BlockSpec((t, width), lambda b, p, i: (b * nq + i, p))
    return _attention_call(
        body, ex, "sb_fwd", (nb, groups, nq),
        [qkv, qkv, qkv, _tri(t, "suffix")],
        [qspec,
         pl.BlockSpec((seq, width), lambda b, p, i: (b, groups + p)),
         pl.BlockSpec((seq, width), lambda b, p, i: (b, 2 * groups + p)),
         _const_spec((2 * t, t))],
        [jax.ShapeDtypeStruct((rows, SB_W), BF16), jax.ShapeDtypeStruct((rows, SB_W), F32)],
        [qspec, qspec])


def _sb_bwd(qkv, d_out, cars, seq, ex=None):
    rows = qkv.shape[0]
    nb = rows // seq
    t = min(ATTN_TILE, seq)
    nq = seq // t
    ap = ATTN_PAIRS["sb_bwd"]
    width = ap * LANES
    groups = SB_W // width
    hds = [(pp, hh) for pp in range(ap) for hh in range(2)]

    def body(q_ref, k_ref, v_ref, do_ref, car_ref, tri_ref, pre_ref, dq_ref, dk_ref, dv_ref):
        i = pl.program_id(2)

        @pl.when(i == 0)
        def _():
            dk_ref[...] = jnp.zeros_like(dk_ref)
            dv_ref[...] = jnp.zeros_like(dv_ref)

        lane = lax.broadcasted_iota(jnp.int32, (t, LANES), 1)
        row = lax.broadcasted_iota(jnp.int32, (t, t), 0)
        col = lax.broadcasted_iota(jnp.int32, (t, t), 1)
        strict = col < row
        tri, pre = tri_ref[...], pre_ref[...]
        masks = [_head_mask(lane, hh) for hh in range(2)]
        qms = [q_ref[:, _pair(pp)] * masks[hh] for pp, hh in hds]
        doms = [do_ref[:, _pair(pp)].astype(BF16) * masks[hh] for pp, hh in hds]
        cars = [car_ref[:, _pair(pp)] for pp in range(ap)]

        def step(kb, carry, valid):
            g_pres, dqs = carry
            dqs = list(dqs)
            start = pl.multiple_of(kb * t, t)
            kss = [k_ref[pl.ds(start, t), _pair(pp)] for pp in range(ap)]
            vss = [v_ref[pl.ds(start, t), _pair(pp)] for pp in range(ap)]
            zs = [_dot(qms[n], kss[pp], NT) for n, (pp, _) in enumerate(hds)]
            dws = [_dot(doms[n], vss[pp], NT) for n, (pp, _) in enumerate(hds)]
            logs = [_sb_logits(z, valid) for z in zs]
            sufs = [_split_dot(lg[0], tri) for lg in logs]
            ws, gs = [], []
            for n, (pp, hh) in enumerate(hds):
                c_sum = jnp.sum(jnp.where(lane == hh * 8 + kb, cars[pp], 0.0), axis=1, keepdims=True)
                w = jnp.exp(logs[n][1] + sufs[n] + c_sum)
                if valid is not None:
                    w = jnp.where(valid, w, 0.0)
                ws.append(w)
                gs.append(dws[n] * w)
            befores = [g_pres[n] + _split_dot(gs[n], pre) for n in range(len(hds))]
            for pp in range(ap):
                dv_ref[pl.ds(start, t), _pair(pp)] += (_dot(ws[2 * pp].astype(BF16), doms[2 * pp], TN)
                                                       + _dot(ws[2 * pp + 1].astype(BF16), doms[2 * pp + 1], TN))
            dzbs = []
            for n in range(len(hds)):
                beta = jnp.exp(logs[n][1])
                dz = gs[n] * (1.0 - beta) - beta * befores[n]
                if valid is not None:
                    dz = jnp.where(valid, dz, 0.0)
                dzbs.append(dz.astype(BF16))
            for pp in range(ap):
                a, b = 2 * pp, 2 * pp + 1
                dqs[pp] = dqs[pp] + _dot(dzbs[a], kss[pp] * masks[0]) + _dot(dzbs[b], kss[pp] * masks[1])
                dk_ref[pl.ds(start, t), _pair(pp)] += _dot(dzbs[a], qms[a], TN) + _dot(dzbs[b], qms[b], TN)
            new_pres = [g_pres[n] + jnp.sum(gs[n], axis=1, keepdims=True) for n in range(len(hds))]
            return tuple(new_pres), tuple(dqs)

        init = (tuple(jnp.zeros((t, 1), F32) for _ in hds), tuple(jnp.zeros((t, LANES), F32) for _ in range(ap)))
        carry = lax.fori_loop(0, i, lambda kb, cr: step(kb, cr, None), init)
        _, dqs = step(i, carry, strict)
        for pp in range(ap):
            dq_ref[:, _pair(pp)] = (dqs[pp] * SB_SCALE).astype(BF16)

    qspec = pl.BlockSpec((t, width), lambda b, p, i: (b * nq + i, p))
    kspec_out = pl.BlockSpec((seq, width), lambda b, p, i: (b, p))
    return _attention_call(
        body, ex, "sb_bwd", (nb, groups, nq),
        [qkv, qkv, qkv, d_out, cars, _tri(t, "suffix"), _tri(t, "prefix")],
        [qspec,
         pl.BlockSpec((seq, width), lambda b, p, i: (b, groups + p)),
         pl.BlockSpec((seq, width), lambda b, p, i: (b, 2 * groups + p)),
         qspec, qspec, _const_spec((2 * t, t)), _const_spec((2 * t, t))],
        [jax.ShapeDtypeStruct((rows, SB_W), BF16), jax.ShapeDtypeStruct((rows, SB_W), F32),
         jax.ShapeDtypeStruct((rows, SB_W), F32)],
        [qspec, kspec_out, kspec_out])


def _mla_scores(qh, ks, allowed):
    s = _dot(qh, ks, NT) * MLA_SCALE
    if allowed is not None:
        s = jnp.where(allowed, s, jnp.finfo(F32).min)
    return s


def _mla_fwd(qp, kp, vm, seq, ex=None, chunk=64):
    rows = qp.shape[0]
    nb = rows // seq
    t = min(ATTN_TILE, seq)
    nq = seq // t
    shift = int(math.log2(chunk))
    ap = ATTN_PAIRS["mla_fwd"]
    width = ap * LANES
    groups = MLA_W // width
    hds = [(pp, hh) for pp in range(ap) for hh in range(2)]

    def body(q_ref, k_ref, v_ref, o_ref, lse_ref):
        i = pl.program_id(2)
        lane = lax.broadcasted_iota(jnp.int32, (t, LANES), 1)
        row = lax.broadcasted_iota(jnp.int32, (t, t), 0)
        col = lax.broadcasted_iota(jnp.int32, (t, t), 1)
        allowed_diag = jnp.right_shift(col, shift) <= jnp.right_shift(row, shift)
        masks = [_head_mask(lane, hh) for hh in range(2)]
        qhs = [q_ref[:, _pair(n)] for n in range(len(hds))]

        def step(kb, carry, allowed):
            start = pl.multiple_of(kb * t, t)
            vss = [v_ref[pl.ds(start, t), _pair(pp)] for pp in range(ap)]
            scores = [_mla_scores(qhs[n], k_ref[pl.ds(start, t), _pair(n)], allowed) for n in range(len(hds))]
            new = []
            for n, (pp, hh) in enumerate(hds):
                m_run, l_run, acc = carry[n]
                s = scores[n]
                m_new = jnp.maximum(m_run, jnp.max(s, axis=1, keepdims=True))
                p = jnp.exp(s - m_new)
                scale = jnp.exp(m_run - m_new)
                l_run = scale * l_run + jnp.sum(p, axis=1, keepdims=True)
                acc = scale * acc + _dot(p.astype(BF16), vss[pp] * masks[hh])
                new.append((m_new, l_run, acc))
            return tuple(new)

        init = (jnp.full((t, 1), jnp.finfo(F32).min, F32), jnp.zeros((t, 1), F32), jnp.zeros((t, LANES), F32))
        carry = step(i, tuple(init for _ in hds), allowed_diag)
        carry = lax.fori_loop(0, i, lambda kb, cr: step(kb, cr, None), carry)
        for pp in range(ap):
            out = jnp.zeros((t, LANES), F32)
            lses = jnp.zeros((t, LANES), F32)
            for hh in range(2):
                m_run, l_run, acc = carry[2 * pp + hh]
                out = out + acc / l_run
                lses = jnp.where(lane == hh, m_run + jnp.log(l_run), lses)
            o_ref[:, _pair(pp)] = out
            lse_ref[:, _pair(pp)] = lses

    ospec = pl.BlockSpec((t, width), lambda b, p, i: (b * nq + i, p))
    return _attention_call(
        body, ex, "mla_fwd", (nb, groups, nq), [qp, kp, vm],
        [pl.BlockSpec((t, 2 * width), lambda b, p, i: (b * nq + i, p)),
         pl.BlockSpec((seq, 2 * width), lambda b, p, i: (b, p)),
         pl.BlockSpec((seq, width), lambda b, p, i: (b, p))],
        [jax.ShapeDtypeStruct((rows, MLA_W), F32), jax.ShapeDtypeStruct((rows, MLA_W), F32)],
        [ospec, ospec])


def _mla_bwd(qp, kp, vm, d_out, out, lse, seq, ex=None, chunk=64):
    rows = qp.shape[0]
    nb = rows // seq
    t = min(ATTN_TILE, seq)
    nq = seq // t
    shift = int(math.log2(chunk))
    ap = ATTN_PAIRS["mla_bwd"]
    width = ap * LANES
    groups = MLA_W // width
    hds = [(pp, hh) for pp in range(ap) for hh in range(2)]
    nh = len(hds)

    def body(q_ref, k_ref, v_ref, do_ref, o_ref, lse_ref, dq_ref, dk_ref, dv_ref):
        i = pl.program_id(2)

        @pl.when(i == 0)
        def _():
            dk_ref[...] = jnp.zeros_like(dk_ref)
            dv_ref[...] = jnp.zeros_like(dv_ref)

        lane = lax.broadcasted_iota(jnp.int32, (t, LANES), 1)
        row = lax.broadcasted_iota(jnp.int32, (t, t), 0)
        col = lax.broadcasted_iota(jnp.int32, (t, t), 1)
        allowed_diag = jnp.right_shift(col, shift) <= jnp.right_shift(row, shift)
        qhs = [q_ref[:, _pair(n)] for n in range(nh)]
        doms, deltas, lse_hs = [], [], []
        for pp, hh in hds:
            do = do_ref[:, _pair(pp)]
            d_o = do * o_ref[:, _pair(pp)]
            doms.append(do.astype(BF16) * _head_mask(lane, hh))
            deltas.append(jnp.sum(jnp.where((lane >= 64) if hh else (lane < 64), d_o, 0.0), axis=1, keepdims=True))
            lse_hs.append(jnp.sum(jnp.where(lane == hh, lse_ref[:, _pair(pp)], 0.0), axis=1, keepdims=True))

        def step(kb, dqs, allowed):
            start = pl.multiple_of(kb * t, t)
            vss = [v_ref[pl.ds(start, t), _pair(pp)] for pp in range(ap)]
            kss = [k_ref[pl.ds(start, t), _pair(n)] for n in range(nh)]
            scores = [_mla_scores(qhs[n], kss[n], allowed) for n in range(nh)]
            dps = [_dot(doms[n], vss[pp], NT) for n, (pp, _) in enumerate(hds)]
            ps = [jnp.exp(scores[n] - lse_hs[n]) for n in range(nh)]
            dss = [(ps[n] * (dps[n] - deltas[n]) * MLA_SCALE).astype(BF16) for n in range(nh)]
            for pp in range(ap):
                a, b = 2 * pp, 2 * pp + 1
                dv_ref[pl.ds(start, t), _pair(pp)] += (_dot(ps[a].astype(BF16), doms[a], TN)
                                                       + _dot(ps[b].astype(BF16), doms[b], TN))
            for n in range(nh):
                dk_ref[pl.ds(start, t), _pair(n)] += _dot(dss[n], qhs[n], TN)
            return tuple(dqs[n] + _dot(dss[n], kss[n]) for n in range(nh))

        dqs = lax.fori_loop(0, i, lambda kb, cr: step(kb, cr, None),
                            tuple(jnp.zeros((t, LANES), F32) for _ in range(nh)))
        dqs = step(i, dqs, allowed_diag)
        for n in range(nh):
            dq_ref[:, _pair(n)] = dqs[n]

    ospec = pl.BlockSpec((t, width), lambda b, p, i: (b * nq + i, p))
    return _attention_call(
        body, ex, "mla_bwd", (nb, groups, nq), [qp, kp, vm, d_out, out, lse],
        [pl.BlockSpec((t, 2 * width), lambda b, p, i: (b * nq + i, p)),
         pl.BlockSpec((seq, 2 * width), lambda b, p, i: (b, p)),
         pl.BlockSpec((seq, width), lambda b, p, i: (b, p)),
         pl.BlockSpec((t, width), lambda b, p, i: (b * nq + i, groups + p)),
         ospec, ospec],
        [jax.ShapeDtypeStruct((rows, HEADS * LANES), F32), jax.ShapeDtypeStruct((rows, HEADS * LANES), F32),
         jax.ShapeDtypeStruct((rows, MLA_W), F32)],
        [pl.BlockSpec((t, 2 * width), lambda b, p, i: (b * nq + i, p)),
         pl.BlockSpec((seq, 2 * width), lambda b, p, i: (b, p)),
         pl.BlockSpec((seq, width), lambda b, p, i: (b, p))])


PACK_COLS = 1024
PACK_ALIGN = 16
GROUP_IN = (384, ((1024, 552, 1), (384, 192, 1), (256, 256, 1)))
GROUP_MLP = (1152, ((1024, 1024, 1), (1024, 1024, 0), (256, 1024, 0)))


def _pack_rows(r, c):
    return (r // 2) * c // PACK_COLS


def _slot_rows(r, c):
    return -(-_pack_rows(r, c) // PACK_ALIGN) * PACK_ALIGN


def _join_slots(parts, group):
    total, weights = group
    padded = [jnp.pad(p, ((0, 0), (0, _slot_rows(r, c) - p.shape[1]), (0, 0))) for p, (r, c, _) in zip(parts, weights)]
    used = sum(_slot_rows(r, c) for r, c, _ in weights)
    if total > used:
        padded.append(jnp.zeros((parts[0].shape[0], total - used, PACK_COLS), parts[0].dtype))
    return jnp.concatenate(padded, axis=1)


def _split_slots(packed, group):
    out, at = [], 0
    for r, c, _ in group[1]:
        out.append(packed[:, at:at + _pack_rows(r, c), :])
        at += _slot_rows(r, c)
    return out


def _pack_halves(shards, group):
    return _join_slots([s.reshape(2, _pack_rows(r, c), PACK_COLS) for s, (r, c, _) in zip(shards, group[1])], group)


def _unpack_halves(packed, group):
    return [p.reshape(r, c) for p, (r, c, _) in zip(_split_slots(packed, group), group[1])]


def _unpack_full(gathered, group):
    out = []
    for p, (r, c, axis) in zip(_split_slots(gathered, group), group[1]):
        shards = p.reshape(4, r, c)
        out.append(shards.reshape(4 * r, c) if axis == 0 else jnp.moveaxis(shards, 0, 1).reshape(r, 4 * c))
    return out


def _pack_full(grads, group):
    parts = []
    for gr, (r, c, axis) in zip(grads, group[1]):
        shards = gr.reshape(4, r, c) if axis == 0 else jnp.moveaxis(gr.reshape(r, 4, c), 1, 0)
        parts.append(shards.reshape(8, _pack_rows(r, c), PACK_COLS))
    return _join_slots(parts, group)


def _pad_w_in(w_in):
    z = jnp.zeros((D_MODEL, 1), w_in.dtype)
    return jnp.concatenate([w_in[:, :2176], jnp.tile(z, (1, 64)), w_in[:, 2176:], jnp.tile(z, (1, 32))], axis=1)


def _unpad_w_in(g):
    return jnp.concatenate([g[:, :2176], g[:, 2240:2272]], axis=1)


def _pad_heads(w, used):
    k = w.shape[0]
    w3 = w.reshape(k, HEADS, used)
    return jnp.pad(w3, ((0, 0), (0, 0), (0, LANES - used))).reshape(k, HEADS * LANES)


def _unpad_heads(g, used):
    k = g.shape[0]
    return g.reshape(k, HEADS, LANES)[:, :, :used].reshape(k, HEADS * used)


def _rope_tables(seq):
    inv_freq = 1.0 / (ROPE_BASE ** (jnp.arange(0, ROPE, 2, dtype=F32) / ROPE))
    ang = jnp.arange(seq, dtype=F32)[:, None] * inv_freq[None, :]
    cos, sin = jnp.cos(ang), jnp.sin(ang)
    one, zero = jnp.ones((seq, NOPE), F32), jnp.zeros((seq, NOPE), F32)
    z16, z32 = jnp.zeros((seq, 16), F32), jnp.zeros((seq, 32), F32)
    cos_t = jnp.concatenate([one, cos, cos, jnp.ones((seq, 32), F32)], axis=1)
    sin_a = jnp.concatenate([zero, -sin, z16, z32], axis=1)
    sin_b = jnp.concatenate([zero, z16, sin, z32], axis=1)
    return cos_t, sin_a, sin_b


SMALL = (("ln_in_g", 1024), ("ln_in_b", 1024), ("b_ada", 6144), ("q_norm_g", 384), ("kv_norm_g", 256),
         ("ln1_g", 1024), ("ln1_b", 1024), ("ln2_g", 1024), ("ln2_b", 1024))
SMALL_TOTAL = sum(n for _, n in SMALL)
SMALL_ROWS = -(-SMALL_TOTAL // LANES // 8) * 8


def _pack_small(vals):
    flat = jnp.concatenate([v.reshape(-1) for v in vals])
    return jnp.pad(flat, (0, SMALL_ROWS * LANES - SMALL_TOTAL)).reshape(SMALL_ROWS, LANES)


def _unpack_small(packed, like):
    flat, out, at = packed.reshape(-1), [], 0
    for (_, n), ref in zip(SMALL, like):
        out.append(flat[at:at + n].reshape(ref.shape))
        at += n
    return out


def kernel(x, c, ln_in_g, ln_in_b, w_ada, b_ada, w_in, q_norm_g, kv_norm_g, w_uq, w_ukv, w_o, ln1_g, ln1_b, w_up, w_down, ln2_g, ln2_b, loss_target, m_ln_in_g, m_ln_in_b, m_w_ada, m_b_ada, m_w_in, m_q_norm_g, m_kv_norm_g, m_w_uq, m_w_ukv, m_w_o, m_ln1_g, m_ln1_b, m_w_up, m_w_down, m_ln2_g, m_ln2_b, v_ln_in_g, v_ln_in_b, v_w_ada, v_b_ada, v_w_in, v_q_norm_g, v_kv_norm_g, v_w_uq, v_w_ukv, v_w_o, v_ln1_g, v_ln1_b, v_w_up, v_w_down, v_ln2_g, v_ln2_b):
    nb, seq, _ = x.shape
    rows = nb * seq
    ix, iy, ic = lax.axis_index("x"), lax.axis_index("y"), lax.axis_index("c")
    chip = 2 * ix + iy
    dev = 2 * chip + ic

    def my_half(shards, group):
        packed = _pack_halves([s.astype(BF16) for s in shards], group)
        return lax.dynamic_index_in_dim(packed, ic, 0, keepdims=False)

    f_in, f_uq, f_ukv = _unpack_full(_gather8(my_half([w_in[0], w_uq[0], w_ukv[0]], GROUP_IN), "gather_w_in"),
                                     GROUP_IN)
    half_mlp = my_half([w_up[0], w_down[0], w_o[0]], GROUP_MLP)
    late_weights = _gather_exchange(half_mlp)
    w_in_p = _pad_w_in(f_in)
    uq3 = f_uq.reshape(Q_RANK, HEADS, NOPE + ROPE)
    w_uq_p = jnp.pad(uq3, ((0, 0), (0, 0), (0, LANES - NOPE - ROPE))).reshape(Q_RANK, HEADS * LANES)
    w_ukv_p = jnp.concatenate([_pad_heads(f_ukv[:, :HEADS * NOPE], NOPE), f_ukv[:, HEADS * NOPE:]], axis=1)

    n_all = 8 * nb
    c_all = _gather8(c.reshape(-1, LANES), "gather_c").reshape(n_all, D_MODEL)
    ada_cols = w_ada.shape[2]
    b_sh = lax.dynamic_slice_in_dim(b_ada, chip * ada_cols, ada_cols, axis=1)
    mod_sh = _ada_fwd(c_all, w_ada[0], b_sh)
    mod_g = _gather8(mod_sh, "gather_mod")[0::2]
    mod_all = jnp.moveaxis(mod_g, 0, 1).reshape(n_all, N_MOD * D_MODEL)
    mod_mine = lax.dynamic_slice_in_dim(mod_all, dev * nb, nb, axis=0).reshape(nb, N_MOD, D_MODEL)
    mod = jnp.pad(mod_mine, ((0, 0), (0, 8 - N_MOD), (0, 0)))

    cos_t, sin_a, sin_b = _rope_tables(seq)
    row2 = lambda v: v.reshape(1, -1)

    x2d = x.reshape(rows, D_MODEL)
    x0, h, qkv, lat, qp, kp, vm = _fwd_in(x2d, mod, row2(ln_in_g), row2(ln_in_b), w_in_p, q_norm_g, kv_norm_g,
                                          w_uq_p, w_ukv_p, cos_t, sin_a, sin_b, seq)
    sb_y, cars, g_mlp = _sb_fwd(qkv, seq, late_weights)
    g_mlp = _with_own(g_mlp, half_mlp)
    f_o = _split_slots(g_mlp, GROUP_MLP)[2].reshape(D_MODEL, D_MODEL)
    mla_y, lse = _mla_fwd(qp, kp, vm, seq)
    mix, y1, h2, u, ff, y2 = _fwd_out(sb_y, mla_y, x0, mod, f_o, ln1_g, ln1_b, g_mlp, seq)

    dy1, dmix, d_attn, dff, du, acc_out, dmod_a = _bwd_out(
        y2, loss_target.reshape(rows, D_MODEL), ff, u, y1, mix, mod, ln2_g, ln2_b, ln1_g, ln1_b, g_mlp, f_o, seq)
    c_idx = ic.reshape(1).astype(jnp.int32)
    blocks_mlp = _wgrad_packed(h2, du, "wgrad_up", lambda i, j: 2 * j + i, 0)
    blocks_mlp = _wgrad_packed(u, dff, "wgrad_down", lambda i, j: i, 1, pre="relu2", into=blocks_mlp)
    blocks_mlp = _wgrad_packed(sb_y, dmix, "wgrad_o_sb", lambda i, j: 0, 8, split=4, into=blocks_mlp)
    blocks_mlp = _wgrad_packed(mla_y, dmix, "wgrad_o_mla", lambda i, j: 1, 8, split=4, into=blocks_mlp)
    dq_sb, dk_sb, dv_sb, sibling_mlp = _sb_bwd(qkv, d_attn, cars, seq, _swap_cores_exchange(blocks_mlp))
    part_mlp, part_mlp_bf = _add_pairs(blocks_mlp, sibling_mlp, c_idx, "grad_add_cores_mlp")
    dqp, dkp, dvm, chips_mlp = _mla_bwd(qp, kp, vm, d_attn, mla_y, lse, seq, _scatter_chips_exchange(part_mlp_bf))
    grad_x, dproj, dqall, dkv, latn, acc0, acc_lat, dmod_c = _bwd_in(
        dqp, dkp, dvm, dq_sb, dk_sb, dv_sb, lat, x2d, x0, dy1, mod, row2(ln_in_g), row2(ln_in_b), w_in_p,
        q_norm_g, kv_norm_g, w_uq_p, w_ukv_p, cos_t, sin_a, sin_b, seq)

    g_in = _unpad_w_in(_wgrad(h, dproj, "wgrad_in", tn=768))
    g_uq = _unpad_heads(_wgrad(latn[:, :Q_RANK], dqall, "wgrad_uq"), NOPE + ROPE)
    g_ukv_p = _wgrad(latn[:, Q_RANK:], dkv, "wgrad_ukv", tn=512)
    g_ukv = jnp.concatenate([_unpad_heads(g_ukv_p[:, :HEADS * LANES], NOPE), g_ukv_p[:, HEADS * LANES:]], axis=1)
    blocks_in = _pack_full([g_in, g_uq, g_ukv], GROUP_IN)
    sibling_in = _run_exchange(_swap_cores_exchange(blocks_in), "grads_in_to_sibling")
    part_in, part_in_bf = _add_pairs(blocks_in, sibling_in, c_idx, "grad_add_cores_in")
    chips_in = _run_exchange(_scatter_chips_exchange(part_in_bf), "grads_in_to_chips")

    def own(part):
        return lax.dynamic_index_in_dim(part, chip, 0, keepdims=False)

    half = jnp.concatenate([_add_chips(own(part_in), chips_in, "grad_add_chips_in"),
                            _add_chips(own(part_mlp), chips_mlp, "grad_add_chips_mlp")], axis=0)
    other = _run_exchange(_swap_one_exchange(half), "grads_halves")
    both = jnp.where(ic == 0, jnp.stack([half, other]), jnp.stack([other, half]))
    gs_in, gs_uq, gs_ukv = _unpack_halves(both[:, :GROUP_IN[0]], GROUP_IN)
    gs_up, gs_down, gs_o = _unpack_halves(both[:, GROUP_IN[0]:], GROUP_MLP)

    dmod = (dmod_a + dmod_c)[:, :N_MOD, :]
    small_part = _pack_small([acc0[0], acc0[1], jnp.zeros((N_MOD * D_MODEL,), F32), acc_lat[0, :Q_RANK],
                              acc_lat[1, :KV_RANK], acc_out[3], acc_out[4], acc_out[0], acc_out[1]])
    n_sum = SMALL_ROWS + D_MODEL // LANES
    payload = jnp.concatenate([small_part, acc_out[2].reshape(-1, LANES), dmod.reshape(-1, LANES)], axis=0)
    gathered = _gather8(payload, "gather_small")
    small_sum = _sum_lead(gathered[:, :n_sum, :], "sum_small")
    loss = jnp.sum(small_sum[SMALL_ROWS:])
    dmod_all = gathered[:, n_sum:, :].reshape(n_all, N_MOD * D_MODEL)
    g_b_ada = _sum_lead(dmod_all.reshape(n_all, N_MOD * D_MODEL // LANES, LANES), "sum_b_ada").reshape(1, -1)
    dmod_sh = lax.dynamic_slice_in_dim(dmod_all, chip * ada_cols, ada_cols, axis=1)
    g_w_ada = _ada_bwd(c_all, dmod_sh)

    small_like = [ln_in_g, ln_in_b, b_ada, q_norm_g, kv_norm_g, ln1_g, ln1_b, ln2_g, ln2_b]
    small_grads = _unpack_small(small_sum, small_like)
    small_grads[2] = g_b_ada

    big_w = {"w_ada": (w_ada[0], g_w_ada, m_w_ada[0], v_w_ada[0]), "w_in": (w_in[0], gs_in, m_w_in[0], v_w_in[0]),
             "w_uq": (w_uq[0], gs_uq, m_w_uq[0], v_w_uq[0]), "w_ukv": (w_ukv[0], gs_ukv, m_w_ukv[0], v_w_ukv[0]),
             "w_o": (w_o[0], gs_o, m_w_o[0], v_w_o[0]), "w_up": (w_up[0], gs_up, m_w_up[0], v_w_up[0]),
             "w_down": (w_down[0], gs_down, m_w_down[0], v_w_down[0])}
    res = {}
    for name, (w, g, m, v) in big_w.items():
        d, mn, vn = _adamw(w, g, m, v, "adamw_" + name)
        res[name] = (g[None], d[None], mn[None], vn[None])
    small_m = [m_ln_in_g, m_ln_in_b, m_b_ada, m_q_norm_g, m_kv_norm_g, m_ln1_g, m_ln1_b, m_ln2_g, m_ln2_b]
    small_v = [v_ln_in_g, v_ln_in_b, v_b_ada, v_q_norm_g, v_kv_norm_g, v_ln1_g, v_ln1_b, v_ln2_g, v_ln2_b]
    sd, sm, sv = _adamw(_pack_small(small_like), _pack_small(small_grads), _pack_small(small_m), _pack_small(small_v),
                        "adamw_small")
    for (name, _), g, d, mn, vn in zip(SMALL, small_grads, _unpack_small(sd, small_like),
                                       _unpack_small(sm, small_like), _unpack_small(sv, small_like)):
        res[name] = (g, d, mn, vn)

    order = ["ln_in_g", "ln_in_b", "w_ada", "b_ada", "w_in", "q_norm_g", "kv_norm_g", "w_uq", "w_ukv", "w_o",
             "ln1_g", "ln1_b", "w_up", "w_down", "ln2_g", "ln2_b"]
    outs = [loss, grad_x.reshape(nb, seq, D_MODEL)]
    for k in range(4):
        outs += [res[name][k] for name in order]
    return tuple(outs)
```

```python
import functools
import math

import jax
import jax.numpy as jnp
from jax import lax
from jax.experimental import pallas as pl
from jax.experimental.pallas import tpu as pltpu

F32 = jnp.float32
BF16 = jnp.bfloat16
MESH_IDS = pl.DeviceIdType.MESH

D_MODEL = 1024
HEADS = 8
HEAD_PAIRS = HEADS // 2
SB_W = 512
MLA_W = 512
NOPE = 64
ROPE = 32
Q_RANK = 384
KV_RANK = 256
D_IN = 2208
D_IN_PAD = 2304
D_FF = 4096
N_MOD = 6
LN_EPS = 1e-5
RMS_EPS = 1e-6
ALPHA = 2.0 ** 0.25
ROPE_BASE = 10000.0
SB_SCALE = 64 ** -0.5
MLA_SCALE = 96 ** -0.5
ADAM_LR = 0.001
ADAM_B1 = 0.9
ADAM_B2 = 0.999
ADAM_EPS = 1e-08
ADAM_WD = 0.01
ADAM_STEP = 10

LANES = 128
ROW_TILE = 256
ATTN_TILE = 256
ATTN_PAIRS = {"sb_fwd": 4, "sb_bwd": 2, "mla_fwd": 4, "mla_bwd": 4}
VMEM_LIMIT = 56 << 20

NT = (((1,), (1,)), ((), ()))
TN = (((0,), (0,)), ((), ()))


def _params(sem=None):
    return pltpu.CompilerParams(vmem_limit_bytes=VMEM_LIMIT, dimension_semantics=sem)


def _const_spec(shape):
    zeros = (0,) * len(shape)
    return pl.BlockSpec(shape, lambda *_: zeros, pipeline_mode=pl.Buffered(1))


def _dot(a, b, dims=None):
    if dims is None:
        return jnp.dot(a, b, preferred_element_type=F32)
    return lax.dot_general(a, b, dims, preferred_element_type=F32)


def _mean(v):
    return jnp.mean(v, axis=-1, keepdims=True)


def _rowsum(v):
    return jnp.sum(v, axis=0, keepdims=True)


def _ln_fwd(y, g, b):
    mu = _mean(y)
    yc = y - mu
    rstd = lax.rsqrt(_mean(yc * yc) + LN_EPS)
    xhat = yc * rstd
    return xhat * g + b, xhat, rstd


def _ln_bwd(dx, xhat, rstd, g):
    dxh = dx * g
    return rstd * (dxh - _mean(dxh) - xhat * _mean(dxh * xhat))


def _rope(v, cos, sin_a, sin_b):
    return v * cos + pltpu.roll(v, 112, 1) * sin_a + pltpu.roll(v, 16, 1) * sin_b


def _rope_t(dv, cos, sin_a, sin_b):
    return dv * cos + pltpu.roll(dv * sin_a, 16, 1) + pltpu.roll(dv * sin_b, 112, 1)


def _my_place():
    return lax.axis_index("x"), lax.axis_index("y"), lax.axis_index("c")


class _Exchange:
    def __init__(self, operand, out_shape, n_copies, phases):
        self.operand = operand
        self.out_shape = out_shape
        self.phases = phases
        self.scratch = [pltpu.SemaphoreType.DMA((n_copies,)), pltpu.SemaphoreType.DMA((n_copies,))]


def _run_exchange(ex, name):
    def body(in_ref, out_ref, send_sems, recv_sems):
        for phase in ex.phases(in_ref, out_ref, send_sems, recv_sems):
            phase()

    return pl.pallas_call(
        body, name=name, out_shape=ex.out_shape,
        in_specs=[pl.BlockSpec(memory_space=pl.ANY)], out_specs=pl.BlockSpec(memory_space=pl.ANY),
        scratch_shapes=ex.scratch,
    )(ex.operand)


def _nothing():
    pass


def _gather_exchange(v):
    m, n = v.shape

    def phases(v_ref, out_ref, send_sems, recv_sems):
        x, y, c = _my_place()
        me, sibling = (x, y, c), (x, y, 1 - c)
        chips = [(1 - x, y), (x, 1 - y), (1 - x, 1 - y)]

        def rows(px, py, pc):
            return out_ref.at[4 * px + 2 * py + pc]

        def copy(k, block, to, src=None):
            return pltpu.make_async_remote_copy(
                src_ref=rows(*block) if src is None else src, dst_ref=rows(*block),
                send_sem=send_sems.at[k], recv_sem=recv_sems.at[k], device_id=to, device_id_type=MESH_IDS)

        first = [copy(0, me, sibling, src=v_ref)]
        first += [copy(1 + j, me, (*chip, c), src=v_ref) for j, chip in enumerate(chips)]
        passed = [copy(4 + j, (*chip, c), sibling) for j, chip in enumerate(chips)]

        def start():
            for cp in first:
                cp.start()

        def middle():
            for j, chip in enumerate(chips):
                copy(1 + j, (*chip, c), me).wait_recv()
                passed[j].start()

        def finish():
            copy(0, sibling, me).wait_recv()
            for j, chip in enumerate(chips):
                copy(4 + j, (*chip, 1 - c), me).wait_recv()
            for cp in first + passed:
                cp.wait_send()

        return start, middle, finish

    return _Exchange(v, jax.ShapeDtypeStruct((8, m, n), v.dtype), 7, phases)


def _with_own(gathered, v):
    dev = 4 * lax.axis_index("x") + 2 * lax.axis_index("y") + lax.axis_index("c")
    return lax.dynamic_update_index_in_dim(gathered, v, dev, 0)


def _direct_exchange(operand, out_shape, n_copies, make_copies):
    def phases(in_ref, out_ref, send_sems, recv_sems):
        copies = make_copies(in_ref, out_ref, send_sems, recv_sems)

        def start():
            for cp in copies:
                cp.start()

        def finish():
            for cp in copies:
                cp.wait()

        return start, _nothing, finish

    return _Exchange(operand, out_shape, n_copies, phases)


def _swap_cores_exchange(blocks):
    _, m, n = blocks.shape

    def make_copies(g_ref, out_ref, send_sems, recv_sems):
        x, y, c = _my_place()
        return [pltpu.make_async_remote_copy(
            src_ref=g_ref.at[2 * j + (1 - c)], dst_ref=out_ref.at[j],
            send_sem=send_sems.at[j], recv_sem=recv_sems.at[j],
            device_id=(x, y, 1 - c), device_id_type=MESH_IDS) for j in range(4)]

    return _direct_exchange(blocks, jax.ShapeDtypeStruct((4, m, n), blocks.dtype), 4, make_copies)


def _scatter_chips_exchange(parts):
    _, m, n = parts.shape
    flips = [(1, 0), (0, 1), (1, 1)]

    def make_copies(p_ref, out_ref, send_sems, recv_sems):
        x, y, c = _my_place()
        copies = []
        for k, (fx, fy) in enumerate(flips):
            tx = 1 - x if fx else x
            ty = 1 - y if fy else y
            copies.append(pltpu.make_async_remote_copy(
                src_ref=p_ref.at[2 * tx + ty], dst_ref=out_ref.at[k],
                send_sem=send_sems.at[k], recv_sem=recv_sems.at[k],
                device_id=(tx, ty, c), device_id_type=MESH_IDS))
        return copies

    return _direct_exchange(parts, jax.ShapeDtypeStruct((3, m, n), parts.dtype), 3, make_copies)


def _swap_one_exchange(v):
    def make_copies(v_ref, out_ref, send_sems, recv_sems):
        x, y, c = _my_place()
        return [pltpu.make_async_remote_copy(src_ref=v_ref, dst_ref=out_ref, send_sem=send_sems.at[0],
                                             recv_sem=recv_sems.at[0], device_id=(x, y, 1 - c),
                                             device_id_type=MESH_IDS)]

    return _direct_exchange(v, jax.ShapeDtypeStruct(v.shape, v.dtype), 1, make_copies)


def _gather8(v, name):
    return _with_own(_run_exchange(_gather_exchange(v), name), v)


def _carried(ex, refs, n_in, n_out, n_scratch):
    ins, ex_in = refs[:n_in], refs[n_in]
    outs, ex_out = refs[n_in + 1:n_in + 1 + n_out], refs[n_in + 1 + n_out]
    at = n_in + 2 + n_out
    return ins, outs + refs[at:at + n_scratch], ex.phases(ex_in, ex_out, *refs[at + n_scratch:])


def _ada_fwd(c_all, w_ada_sh, b_ada_sh):
    nb, cols = c_all.shape[0], w_ada_sh.shape[1]
    tn = 512

    def body(c_ref, w_ref, b_ref, o_ref):
        cv = c_ref[...]
        act = (cv * jax.nn.sigmoid(cv)).astype(BF16)
        o_ref[...] = _dot(act, w_ref[...].astype(BF16)) + b_ref[...]

    return pl.pallas_call(
        body, name="ada_fwd", grid=(cols // tn,),
        out_shape=jax.ShapeDtypeStruct((nb, cols), F32),
        in_specs=[pl.BlockSpec((nb, D_MODEL), lambda j: (0, 0)),
                  pl.BlockSpec((D_MODEL, tn), lambda j: (0, j)),
                  pl.BlockSpec((1, tn), lambda j: (0, j))],
        out_specs=pl.BlockSpec((nb, tn), lambda j: (0, j)),
        compiler_params=_params(("arbitrary",)),
    )(c_all, w_ada_sh, b_ada_sh)


def _ada_bwd(c_all, dmod_sh):
    nb, cols = dmod_sh.shape
    tn = 512

    def body(c_ref, d_ref, o_ref):
        cv = c_ref[...]
        act = (cv * jax.nn.sigmoid(cv)).astype(BF16)
        o_ref[...] = _dot(act, d_ref[...].astype(BF16), TN)

    return pl.pallas_call(
        body, name="ada_bwd", grid=(cols // tn,),
        out_shape=jax.ShapeDtypeStruct((D_MODEL, cols), F32),
        in_specs=[pl.BlockSpec((nb, D_MODEL), lambda j: (0, 0)),
                  pl.BlockSpec((nb, tn), lambda j: (0, j))],
        out_specs=pl.BlockSpec((D_MODEL, tn), lambda j: (0, j)),
        compiler_params=_params(("arbitrary",)),
    )(c_all, dmod_sh)


def _sum_lead(v, name):
    k, m, n = v.shape

    def body(v_ref, o_ref):
        acc = v_ref[0]
        for i in range(1, k):
            acc = acc + v_ref[i]
        o_ref[...] = acc

    return pl.pallas_call(
        body, name=name, out_shape=jax.ShapeDtypeStruct((m, n), F32),
        in_specs=[pl.BlockSpec((k, m, n), lambda: (0, 0, 0))],
        out_specs=pl.BlockSpec((m, n), lambda: (0, 0)),
        compiler_params=_params(),
    )(v)


def _adamw_math(w, g, m, v):
    mn = ADAM_B1 * m + (1.0 - ADAM_B1) * g
    vn = ADAM_B2 * v + (1.0 - ADAM_B2) * (g * g)
    m_hat = mn / (1.0 - ADAM_B1 ** ADAM_STEP)
    v_hat = vn / (1.0 - ADAM_B2 ** ADAM_STEP)
    return -ADAM_LR * (m_hat / (jnp.sqrt(v_hat) + ADAM_EPS) + ADAM_WD * w), mn, vn


def _adamw_small(g_sum, g_b_ada, ws, ms, vs):
    n = len(SMALL)

    def body(gs_ref, gb_ref, *refs):
        outs = refs[3 * n:]
        for p in range(n):
            rows_p = SMALL[p][1] // LANES
            g = gb_ref[...] if SMALL[p][0] == "b_ada" else gs_ref[SMALL_AT[p]:SMALL_AT[p] + rows_p, :]
            d, mn, vn = _adamw_math(refs[p][...], g, refs[n + p][...], refs[2 * n + p][...])
            outs[4 * p][...] = g
            outs[4 * p + 1][...] = d
            outs[4 * p + 2][...] = mn
            outs[4 * p + 3][...] = vn

    shapes = [jax.ShapeDtypeStruct((size // LANES, LANES), F32) for _, size in SMALL for _ in range(4)]
    flat = lambda arrs: [a.reshape(-1, LANES) for a in arrs]
    res = pl.pallas_call(body, name="adamw_small", out_shape=tuple(shapes), compiler_params=_params())(
        g_sum, g_b_ada, *flat(ws), *flat(ms), *flat(vs))
    return [tuple(r.reshape(w.shape) for r in res[4 * p:4 * p + 4]) for p, w in enumerate(ws)]


def _adamw(w, g, m, v, name):
    rows, cols = w.shape
    tr = rows
    while tr * cols * 4 > (2 << 20) and tr % 16 == 0:
        tr //= 2

    def body(w_ref, g_ref, m_ref, v_ref, d_ref, mo_ref, vo_ref):
        d_ref[...], mo_ref[...], vo_ref[...] = _adamw_math(w_ref[...], g_ref[...], m_ref[...], v_ref[...])

    spec = pl.BlockSpec((tr, cols), lambda i: (i, 0))
    shape = jax.ShapeDtypeStruct((rows, cols), F32)
    return pl.pallas_call(
        body, name=name, grid=(rows // tr,), out_shape=(shape, shape, shape),
        in_specs=[spec, spec, spec, spec], out_specs=(spec, spec, spec),
        compiler_params=_params(("arbitrary",)),
    )(w, g, m, v)


def _add_rows(m, n):
    fits = [d for d in range(16, m + 1, 16) if m % d == 0 and d * n * 4 <= (5 << 19)]
    assert fits, (m, n)
    return max(fits)


def _add_pairs(blocks, recv, c_idx, name):
    _, m, n = blocks.shape
    tr = _add_rows(m, n)

    def body(c_ref, a_ref, b_ref, o_ref, ob_ref):
        s = a_ref[...] + b_ref[...]
        o_ref[...] = s
        ob_ref[...] = s.astype(BF16)

    grid_spec = pltpu.PrefetchScalarGridSpec(
        num_scalar_prefetch=1, grid=(4, m // tr),
        in_specs=[pl.BlockSpec((1, tr, n), lambda j, i, c: (2 * j + c[0], i, 0)),
                  pl.BlockSpec((1, tr, n), lambda j, i, c: (j, i, 0))],
        out_specs=(pl.BlockSpec((1, tr, n), lambda j, i, c: (j, i, 0)),
                   pl.BlockSpec((1, tr, n), lambda j, i, c: (j, i, 0))))
    return pl.pallas_call(
        body, name=name, grid_spec=grid_spec,
        out_shape=(jax.ShapeDtypeStruct((4, m, n), F32), jax.ShapeDtypeStruct((4, m, n), BF16)),
        compiler_params=_params(("arbitrary", "arbitrary")),
    )(c_idx, blocks, recv)


def _add_chips(own, recv, name):
    m, n = own.shape
    tr = _add_rows(m, n)

    def body(a_ref, r_ref, o_ref):
        acc = a_ref[...]
        for k in range(3):
            acc = acc + r_ref[k].astype(F32)
        o_ref[...] = acc

    return pl.pallas_call(
        body, name=name, grid=(m // tr,),
        out_shape=jax.ShapeDtypeStruct((m, n), F32),
        in_specs=[pl.BlockSpec((tr, n), lambda i: (i, 0)), pl.BlockSpec((3, tr, n), lambda i: (0, i, 0))],
        out_specs=pl.BlockSpec((tr, n), lambda i: (i, 0)),
        compiler_params=_params(("arbitrary",)),
    )(own, recv)


def _row_spec(cols):
    return pl.BlockSpec((ROW_TILE, cols), lambda i: (i, 0))


def _mod_spec(tiles_per_seq):
    return pl.BlockSpec((1, 8, D_MODEL), lambda i: (i // tiles_per_seq, 0, 0))


def _table_spec(tiles_per_seq):
    return pl.BlockSpec((ROW_TILE, LANES), lambda i: (i % tiles_per_seq, 0))


def _fwd_in(x, mod, ln_g, ln_b, w_in, q_g, kv_g, w_uq, w_ukv, cos_t, sin_a, sin_b, seq):
    rows = x.shape[0]
    tm = ROW_TILE
    tps = seq // tm

    def body(x_ref, mod_ref, g_ref, b_ref, win_ref, qg_ref, kvg_ref, wuq_ref, wukv_ref, cos_ref, sa_ref, sb_ref,
             x0_ref, h_ref, qkv_ref, lat_ref, qp_ref, kp_ref, vm_ref):
        x0, _, _ = _ln_fwd(x_ref[...], g_ref[...], b_ref[...])
        x0_ref[...] = x0
        h = (x0 * (1.0 + mod_ref[0, 1:2, :]) + mod_ref[0, 0:1, :]).astype(BF16)
        h_ref[...] = h
        proj = _dot(h, win_ref[...])
        qkv_ref[:, :SB_W] = (proj[:, :SB_W] * SB_SCALE).astype(BF16)
        qkv_ref[:, SB_W:] = proj[:, SB_W:3 * SB_W].astype(BF16)
        lat_ref[...] = proj[:, 3 * SB_W:3 * SB_W + Q_RANK + KV_RANK]
        cq = proj[:, 3 * SB_W:3 * SB_W + Q_RANK]
        ckv = proj[:, 3 * SB_W + Q_RANK:3 * SB_W + Q_RANK + KV_RANK]
        kr = proj[:, D_IN_PAD - LANES:]
        cos, sa, sb = cos_ref[...], sa_ref[...], sb_ref[...]
        cqn = (cq * lax.rsqrt(_mean(cq * cq) + RMS_EPS) * qg_ref[...]).astype(BF16)
        q_all = _dot(cqn, wuq_ref[...])
        for hd in range(HEADS):
            sl = slice(hd * LANES, (hd + 1) * LANES)
            qp_ref[:, sl] = _rope(q_all[:, sl], cos, sa, sb).astype(BF16)
        ckvn = (ckv * lax.rsqrt(_mean(ckv * ckv) + RMS_EPS) * kvg_ref[...]).astype(BF16)
        kv = _dot(ckvn, wukv_ref[...])
        kr_rot = _rope(kr, cos, sa, sb)
        for hd in range(HEADS):
            sl = slice(hd * LANES, (hd + 1) * LANES)
            kp_ref[:, sl] = (kv[:, sl] + kr_rot).astype(BF16)
        vm_ref[...] = kv[:, HEADS * LANES:].astype(BF16)

    outs = [(D_MODEL, F32), (D_MODEL, BF16), (3 * SB_W, BF16), (Q_RANK + KV_RANK, F32),
            (HEADS * LANES, BF16), (HEADS * LANES, BF16), (MLA_W, BF16)]
    return pl.pallas_call(
        body, name="fwd_in", grid=(rows // tm,),
        out_shape=tuple(jax.ShapeDtypeStruct((rows, n), dt) for n, dt in outs),
        in_specs=[_row_spec(D_MODEL), _mod_spec(tps), _const_spec((1, D_MODEL)), _const_spec((1, D_MODEL)),
                  _const_spec(w_in.shape), _const_spec((1, Q_RANK)), _const_spec((1, KV_RANK)),
                  _const_spec(w_uq.shape), _const_spec(w_ukv.shape),
                  _table_spec(tps), _table_spec(tps), _table_spec(tps)],
        out_specs=tuple(_row_spec(n) for n, _ in outs),
        compiler_params=_params(("arbitrary",)),
    )(x, mod, ln_g, ln_b, w_in, q_g, kv_g, w_uq, w_ukv, cos_t, sin_a, sin_b)


HALF = 512
SHARD = 1024


def _mlp_weight_specs():
    return [pl.BlockSpec((8, HALF, SHARD), lambda i: (0, 0, 0), pipeline_mode=pl.Buffered(1)),
            pl.BlockSpec((8, HALF, SHARD), lambda i: (0, 1, 0), pipeline_mode=pl.Buffered(1))]


def _fwd_out(sb_y, mla_y, x0, mod, w_o, ln_g, ln_b, g_mlp, seq):
    rows = x0.shape[0]
    tm = ROW_TILE
    tps = seq // tm

    def body(sb_ref, ml_ref, x0_ref, mod_ref, wo_ref, g_ref, b_ref, wu_ref, wd_ref,
             mix_ref, y1_ref, h2_ref, u_ref, ff_ref, y2_ref):
        mix = _dot(sb_ref[...], wo_ref[:SB_W, :]) + _dot(ml_ref[...].astype(BF16), wo_ref[SB_W:, :])
        mix_ref[...] = mix
        y1 = ALPHA * x0_ref[...] + (1.0 + mod_ref[0, 2:3, :]) * mix
        y1_ref[...] = y1
        x1, _, _ = _ln_fwd(y1, g_ref[...], b_ref[...])
        h2 = (x1 * (1.0 + mod_ref[0, 4:5, :]) + mod_ref[0, 3:4, :]).astype(BF16)
        h2_ref[...] = h2
        h_lo, h_hi = h2[:, :HALF], h2[:, HALF:]
        ff = jnp.zeros((tm, D_MODEL), F32)
        for chip in range(4):
            u = _dot(h_lo, wu_ref[2 * chip]) + _dot(h_hi, wu_ref[2 * chip + 1])
            u_ref[:, chip * SHARD:(chip + 1) * SHARD] = u.astype(BF16)
            act = jnp.square(jnp.maximum(u, 0.0)).astype(BF16)
            ff = ff + _dot(act[:, :HALF], wd_ref[2 * chip]) + _dot(act[:, HALF:], wd_ref[2 * chip + 1])
        ff_ref[...] = ff
        y2_ref[...] = ALPHA * x1 + (1.0 + mod_ref[0, 5:6, :]) * ff

    outs = [(D_MODEL, F32), (D_MODEL, F32), (D_MODEL, BF16), (D_FF, BF16), (D_MODEL, F32), (D_MODEL, F32)]
    return pl.pallas_call(
        body, name="fwd_out", grid=(rows // tm,),
        out_shape=tuple(jax.ShapeDtypeStruct((rows, n), dt) for n, dt in outs),
        in_specs=[_row_spec(SB_W), _row_spec(MLA_W), _row_spec(D_MODEL), _mod_spec(tps), _const_spec(w_o.shape),
                  _const_spec((1, D_MODEL)), _const_spec((1, D_MODEL))] + _mlp_weight_specs(),
        out_specs=tuple(_row_spec(n) for n, _ in outs),
        compiler_params=_params(("arbitrary",)),
    )(sb_y, mla_y, x0, mod, w_o, ln_g, ln_b, g_mlp, g_mlp)


def _acc_spec(rows=8, cols=D_MODEL):
    return pl.BlockSpec((rows, cols), lambda i: (0, 0))


def _bwd_out(y2, tgt, ff, u, y1, mix, mod, ln2_g, ln2_b, ln1_g, ln1_b, g_mlp, w_o, seq):
    rows = y2.shape[0]
    nb = rows // seq
    tm = ROW_TILE
    tps = seq // tm

    def body(y2_ref, t_ref, ff_ref, u_ref, y1_ref, mix_ref, mod_ref, g2_ref, b2_ref, g_ref, b_ref, wu_ref, wd_ref,
             wo_ref, dy1_ref, dmix_ref, do_ref, dff_ref, du_ref, acc_ref, dmod_ref):
        i = pl.program_id(0)

        @pl.when(i == 0)
        def _():
            acc_ref[...] = jnp.zeros_like(acc_ref)

        @pl.when(i % tps == 0)
        def _():
            dmod_ref[...] = jnp.zeros_like(dmod_ref)

        g2 = g2_ref[...]
        x2, xhat2, rstd2 = _ln_fwd(y2_ref[...], g2, b2_ref[...])
        err = x2 - t_ref[...]
        dx2 = err * (1.0 / D_MODEL)
        acc_ref[0:1, :] += _rowsum(dx2 * xhat2)
        acc_ref[1:2, :] += _rowsum(dx2)
        acc_ref[2:3, :] += _rowsum(err * err) * (0.5 / D_MODEL)
        dy2 = _ln_bwd(dx2, xhat2, rstd2, g2)
        dmod_ref[0, 5:6, :] += _rowsum(dy2 * ff_ref[...])
        dff = ((1.0 + mod_ref[0, 5:6, :]) * dy2).astype(BF16)
        dff_ref[...] = dff
        for blk in range(8):
            cols = slice(blk * HALF, (blk + 1) * HALF)
            da = _dot(dff, wd_ref[blk], NT)
            du_ref[:, cols] = (da * (2.0 * jnp.maximum(u_ref[:, cols].astype(F32), 0.0))).astype(BF16)

        g = g_ref[...]
        x1, xhat, rstd = _ln_fwd(y1_ref[...], g, b_ref[...])
        halves = []
        for half in range(2):
            acc = jnp.zeros((tm, HALF), F32)
            for chip in range(4):
                acc = acc + _dot(du_ref[:, chip * SHARD:(chip + 1) * SHARD], wu_ref[2 * chip + half], NT)
            halves.append(acc)
        dh2 = jnp.concatenate(halves, axis=1)
        dmod_ref[0, 3:4, :] += _rowsum(dh2)
        dmod_ref[0, 4:5, :] += _rowsum(dh2 * x1)
        dx1 = ALPHA * dy2 + dh2 * (1.0 + mod_ref[0, 4:5, :])
        acc_ref[3:4, :] += _rowsum(dx1 * xhat)
        acc_ref[4:5, :] += _rowsum(dx1)
        dy1 = _ln_bwd(dx1, xhat, rstd, g)
        dy1_ref[...] = dy1
        dmod_ref[0, 2:3, :] += _rowsum(dy1 * mix_ref[...])
        dmix = ((1.0 + mod_ref[0, 2:3, :]) * dy1).astype(BF16)
        dmix_ref[...] = dmix
        do_ref[...] = _dot(dmix, wo_ref[...], NT)

    outs = [(D_MODEL, F32), (D_MODEL, BF16), (D_MODEL, F32), (D_MODEL, BF16), (D_FF, BF16)]
    return pl.pallas_call(
        body, name="bwd_out", grid=(rows // tm,),
        out_shape=tuple(jax.ShapeDtypeStruct((rows, n), dt) for n, dt in outs)
        + (jax.ShapeDtypeStruct((8, D_MODEL), F32), jax.ShapeDtypeStruct((nb, 8, D_MODEL), F32)),
        in_specs=[_row_spec(D_MODEL), _row_spec(D_MODEL), _row_spec(D_MODEL), _row_spec(D_FF), _row_spec(D_MODEL),
                  _row_spec(D_MODEL), _mod_spec(tps), _const_spec((1, D_MODEL)), _const_spec((1, D_MODEL)),
                  _const_spec((1, D_MODEL)), _const_spec((1, D_MODEL))] + _mlp_weight_specs()
        + [_const_spec(w_o.shape)],
        out_specs=tuple(_row_spec(n) for n, _ in outs) + (_acc_spec(), _mod_spec(tps)),
        compiler_params=_params(("arbitrary",)),
    )(y2, tgt, ff, u, y1, mix, mod, ln2_g, ln2_b, ln1_g, ln1_b, g_mlp, g_mlp, w_o)


def _bwd_in(dqp, dkp, dvm, dq_sb, dk_sb, dv_sb, lat, x, x0, dy1, mod, ln_g, ln_b, w_in, q_g, kv_g, w_uq, w_ukv,
            cos_t, sin_a, sin_b, seq):
    rows = x.shape[0]
    nb = rows // seq
    tm = ROW_TILE
    tps = seq // tm
    n_lat = Q_RANK + KV_RANK

    def body(dqp_ref, dkp_ref, dvm_ref, dqs_ref, dks_ref, dvs_ref, lat_ref, x_ref, x0_ref, dy1_ref, mod_ref,
             g_ref, b_ref, win_ref, qg_ref, kvg_ref, wuq_ref, wukv_ref, cos_ref, sa_ref, sb_ref,
             dx_ref, dproj_ref, dqall_ref, dkv_ref, latn_ref, acc_ref, accl_ref, dmod_ref):
        i = pl.program_id(0)

        @pl.when(i == 0)
        def _():
            acc_ref[...] = jnp.zeros_like(acc_ref)
            accl_ref[...] = jnp.zeros_like(accl_ref)

        @pl.when(i % tps == 0)
        def _():
            dmod_ref[...] = jnp.zeros_like(dmod_ref)

        cos, sa, sb = cos_ref[...], sa_ref[...], sb_ref[...]
        lane = lax.broadcasted_iota(jnp.int32, (tm, LANES), 1)
        for hd in range(HEADS):
            sl = slice(hd * LANES, (hd + 1) * LANES)
            dqall_ref[:, sl] = _rope_t(dqp_ref[:, sl], cos, sa, sb).astype(BF16)
        dcqn = _dot(dqall_ref[...], wuq_ref[...], NT)
        cq = lat_ref[:, :Q_RANK]
        qg = qg_ref[...]
        rq = lax.rsqrt(_mean(cq * cq) + RMS_EPS)
        cqn = cq * rq
        latn_ref[:, :Q_RANK] = (cqn * qg).astype(BF16)
        accl_ref[0:1, :Q_RANK] += _rowsum(dcqn * cqn)
        dqg = dcqn * qg
        dcq = rq * (dqg - cqn * _mean(dqg * cqn))
        dkr = jnp.zeros((tm, LANES), F32)
        for hd in range(HEADS):
            sl = slice(hd * LANES, (hd + 1) * LANES)
            dk = dkp_ref[:, sl]
            dkr = dkr + dk
            dkv_ref[:, sl] = jnp.where(lane < NOPE, dk, 0.0).astype(BF16)
        dkv_ref[:, HEADS * LANES:] = dvm_ref[...].astype(BF16)
        dckvn = _dot(dkv_ref[...], wukv_ref[...], NT)
        ckv = lat_ref[:, Q_RANK:]
        kvg = kvg_ref[...]
        rkv = lax.rsqrt(_mean(ckv * ckv) + RMS_EPS)
        ckvn = ckv * rkv
        latn_ref[:, Q_RANK:] = (ckvn * kvg).astype(BF16)
        accl_ref[1:2, :KV_RANK] += _rowsum(dckvn * ckvn)
        dkg = dckvn * kvg
        dckv = rkv * (dkg - ckvn * _mean(dkg * ckvn))
        dkr = _rope_t(jnp.where(lane >= NOPE, dkr, 0.0), cos, sa, sb)
        dproj_ref[:, :SB_W] = dqs_ref[...]
        dproj_ref[:, SB_W:2 * SB_W] = dks_ref[...].astype(BF16)
        dproj_ref[:, 2 * SB_W:3 * SB_W] = dvs_ref[...].astype(BF16)
        dproj_ref[:, 3 * SB_W:3 * SB_W + Q_RANK] = dcq.astype(BF16)
        dproj_ref[:, 3 * SB_W + Q_RANK:3 * SB_W + n_lat] = dckv.astype(BF16)
        dproj_ref[:, D_IN_PAD - LANES:] = dkr.astype(BF16)
        dh = _dot(dproj_ref[...], win_ref[...], NT)
        x0 = x0_ref[...]
        dmod_ref[0, 0:1, :] += _rowsum(dh)
        dmod_ref[0, 1:2, :] += _rowsum(dh * x0)
        dx0 = ALPHA * dy1_ref[...] + dh * (1.0 + mod_ref[0, 1:2, :])
        g = g_ref[...]
        _, xhat, rstd = _ln_fwd(x_ref[...], g, b_ref[...])
        acc_ref[0:1, :] += _rowsum(dx0 * xhat)
        acc_ref[1:2, :] += _rowsum(dx0)
        dx_ref[...] = _ln_bwd(dx0, xhat, rstd, g)

    outs = [(D_MODEL, F32), (D_IN_PAD, BF16), (HEADS * LANES, BF16), (HEADS * LANES + MLA_W, BF16), (n_lat, BF16)]
    return pl.pallas_call(
        body, name="bwd_in", grid=(rows // tm,),
        out_shape=tuple(jax.ShapeDtypeStruct((rows, n), dt) for n, dt in outs)
        + (jax.ShapeDtypeStruct((8, D_MODEL), F32), jax.ShapeDtypeStruct((8, Q_RANK), F32),
           jax.ShapeDtypeStruct((nb, 8, D_MODEL), F32)),
        in_specs=[_row_spec(HEADS * LANES), _row_spec(HEADS * LANES), _row_spec(MLA_W),
                  _row_spec(SB_W), _row_spec(SB_W), _row_spec(SB_W), _row_spec(n_lat),
                  _row_spec(D_MODEL), _row_spec(D_MODEL), _row_spec(D_MODEL), _mod_spec(tps),
                  _const_spec((1, D_MODEL)), _const_spec((1, D_MODEL)), _const_spec(w_in.shape),
                  _const_spec((1, Q_RANK)), _const_spec((1, KV_RANK)), _const_spec(w_uq.shape),
                  _const_spec(w_ukv.shape), _table_spec(tps), _table_spec(tps), _table_spec(tps)],
        out_specs=tuple(_row_spec(n) for n, _ in outs) + (_acc_spec(), _acc_spec(8, Q_RANK), _mod_spec(tps)),
        compiler_params=_params(("arbitrary",)),
    )(dqp, dkp, dvm, dq_sb, dk_sb, dv_sb, lat, x, x0, dy1, mod, ln_g, ln_b, w_in, q_g, kv_g, w_uq, w_ukv,
      cos_t, sin_a, sin_b)


def _wgrad(a, b, name, pre=None, tm=512, tn=1024, tk=2048):
    rows, m = a.shape
    n = b.shape[1]
    tm, tn, tk = min(tm, m), min(tn, n), min(tk, rows)
    if m % tm:
        tm = m
    if n % tn:
        tn = n

    def body(a_ref, b_ref, o_ref):
        @pl.when(pl.program_id(2) == 0)
        def _():
            o_ref[...] = jnp.zeros_like(o_ref)

        av = a_ref[...]
        if pre == "relu2":
            av = jnp.square(jnp.maximum(av.astype(F32), 0.0))
        o_ref[...] += _dot(av.astype(BF16), b_ref[...].astype(BF16), TN)

    return pl.pallas_call(
        body, name=name, grid=(m // tm, n // tn, rows // tk),
        out_shape=jax.ShapeDtypeStruct((m, n), F32),
        in_specs=[pl.BlockSpec((tk, tm), lambda i, j, k: (k, i)), pl.BlockSpec((tk, tn), lambda i, j, k: (k, j))],
        out_specs=pl.BlockSpec((tm, tn), lambda i, j, k: (i, j)),
        compiler_params=_params(("arbitrary", "arbitrary", "arbitrary")),
    )(a, b)


def _wgrad_packed(a, b, name, block_of, row_block, split=1, pre=None, into=None, tk=2048):
    rows, m = a.shape
    n = b.shape[1]
    tm = HALF
    part = tm // split
    tk = min(tk, rows)
    shape = jax.ShapeDtypeStruct((8, GROUP_MLP[0], PACK_COLS), F32)

    def body(a_ref, b_ref, *rest):
        o_ref = rest[-1]

        @pl.when(pl.program_id(2) == 0)
        def _():
            o_ref[...] = jnp.zeros_like(o_ref)

        av = a_ref[...]
        if pre == "relu2":
            av = jnp.square(jnp.maximum(av.astype(F32), 0.0))
        prod = _dot(av.astype(BF16), b_ref[...].astype(BF16), TN)
        for s in range(split):
            o_ref[s] += prod[s * part:(s + 1) * part]

    in_specs = [pl.BlockSpec((tk, tm), lambda i, j, k: (k, i)), pl.BlockSpec((tk, SHARD), lambda i, j, k: (k, j))]
    operands = [a, b]
    if into is not None:
        in_specs.append(pl.BlockSpec(memory_space=pl.ANY))
        operands.append(into)
    return pl.pallas_call(
        body, name=name, grid=(m // tm, n // SHARD, rows // tk), out_shape=shape,
        in_specs=in_specs,
        out_specs=pl.BlockSpec((split, part, SHARD), lambda i, j, k: (block_of(i, j), row_block, 0)),
        input_output_aliases={} if into is None else {2: 0},
        compiler_params=_params(("arbitrary", "arbitrary", "arbitrary")),
    )(*operands)


def _pair(pp):
    return slice(pp * LANES, (pp + 1) * LANES)


def _head_mask(lane, hh):
    return jnp.where((lane >= 64) if hh else (lane < 64), 1.0, 0.0).astype(BF16)


def _tri(t, kind):
    r = lax.broadcasted_iota(jnp.int32, (t, t), 0)
    c = lax.broadcasted_iota(jnp.int32, (t, t), 1)
    one = jnp.where(r > c if kind == "suffix" else r < c, 1.0, 0.0).astype(BF16)
    return jnp.concatenate([one, one], axis=0)


def _split_dot(v, tri2):
    hi = v.astype(BF16)
    lo = (v - hi.astype(F32)).astype(BF16)
    return _dot(jnp.concatenate([hi, lo], axis=1), tri2)


def _sb_logits(z, valid):
    log_keep = -(jnp.maximum(z, 0.0) + jnp.log(1.0 + jnp.exp(-jnp.abs(z))))
    log_beta = z + log_keep
    if valid is not None:
        log_keep = jnp.where(valid, log_keep, 0.0)
    return log_keep, log_beta


def _attention_call(body, ex, name, grid, operands, in_specs, out_shapes, out_specs, scratch=()):
    n_in, n_out = len(operands), len(out_shapes)
    total = grid[0] * grid[1] * grid[2]
    any_spec = pl.BlockSpec(memory_space=pl.ANY)

    def carrier(*refs):
        ins, outs, (start, middle, finish) = _carried(ex, refs, n_in, n_out, len(scratch))
        step = (pl.program_id(0) * grid[1] + pl.program_id(1)) * grid[2] + pl.program_id(2)
        pl.when(step == 0)(start)
        pl.when(step == total // 2)(middle)
        body(*ins, *outs)
        pl.when(step == total - 1)(finish)

    carried = ex is not None
    return pl.pallas_call(
        carrier if carried else body, name=name, grid=grid,
        out_shape=tuple(out_shapes) + ((ex.out_shape,) if carried else ()),
        in_specs=list(in_specs) + ([any_spec] if carried else []),
        out_specs=tuple(out_specs) + ((any_spec,) if carried else ()),
        scratch_shapes=list(scratch) + (ex.scratch if carried else []),
        compiler_params=_params(("arbitrary", "arbitrary", "arbitrary")),
    )(*operands, *([ex.operand] if carried else []))


def _sb_fwd(qkv, seq, ex=None):
    rows = qkv.shape[0]
    nb = rows // seq
    t = min(ATTN_TILE, seq)
    nq = seq // t
    ap = ATTN_PAIRS["sb_fwd"]
    width = ap * LANES
    groups = SB_W // width
    hds = [(pp, hh) for pp in range(ap) for hh in range(2)]

    def body(q_ref, k_ref, v_ref, tri_ref, o_ref, car_ref, acc_ref):
        i = pl.program_id(2)
        lane = lax.broadcasted_iota(jnp.int32, (t, LANES), 1)
        row = lax.broadcasted_iota(jnp.int32, (t, t), 0)
        col = lax.broadcasted_iota(jnp.int32, (t, t), 1)
        strict = col < row
        tri = tri_ref[...]
        masks = [_head_mask(lane, hh) for hh in range(2)]
        qms = [q_ref[:, _pair(pp)] * masks[hh] for pp, hh in hds]
        acc_ref[...] = jnp.zeros_like(acc_ref)
        car_ref[...] = jnp.zeros_like(car_ref)

        def step(kb, c_sums, valid):
            start = pl.multiple_of(kb * t, t)
            kss = [k_ref[pl.ds(start, t), _pair(pp)] for pp in range(ap)]
            vss = [v_ref[pl.ds(start, t), _pair(pp)] for pp in range(ap)]
            zs = [_dot(qms[n], kss[pp], NT) for n, (pp, _) in enumerate(hds)]
            logs = [_sb_logits(z, valid) for z in zs]
            sufs = [_split_dot(lg[0], tri) for lg in logs]
            new_sums = []
            for n, (pp, hh) in enumerate(hds):
                log_keep, log_beta = logs[n]
                w = jnp.exp(log_beta + sufs[n] + c_sums[n])
                if valid is not None:
                    w = jnp.where(valid, w, 0.0)
                acc_ref[pp] += _dot(w.astype(BF16), vss[pp] * masks[hh])
                car_ref[:, _pair(pp)] = jnp.where(lane == hh * 8 + kb, c_sums[n], car_ref[:, _pair(pp)])
                new_sums.append(c_sums[n] + jnp.sum(log_keep, axis=1, keepdims=True))
            return tuple(new_sums)

        c_sums = step(i, tuple(jnp.zeros((t, 1), F32) for _ in hds), strict)
        lax.fori_loop(0, i, lambda j, cr: step(i - 1 - j, cr, None), c_sums)
        for pp in range(ap):
            o_ref[:, _pair(pp)] = acc_ref[pp].astype(BF16)

    qspec = pl.BlockSpec((t, width), lambda b, p, i: (b * nq + i, p))
    return _attention_call(
        body, ex, "sb_fwd", (nb, groups, nq),
        [qkv, qkv, qkv, _tri(t, "suffix")],
        [qspec,
         pl.BlockSpec((seq, width), lambda b, p, i: (b, groups + p)),
         pl.BlockSpec((seq, width), lambda b, p, i: (b, 2 * groups + p)),
         _const_spec((2 * t, t))],
        [jax.ShapeDtypeStruct((rows, SB_W), BF16), jax.ShapeDtypeStruct((rows, SB_W), F32)],
        [qspec, qspec], scratch=[pltpu.VMEM((ap, t, LANES), F32)])


def _sb_bwd(qkv, d_out, cars, seq, ex=None):
    rows = qkv.shape[0]
    nb = rows // seq
    t = min(ATTN_TILE, seq)
    nq = seq // t
    ap = ATTN_PAIRS["sb_bwd"]
    width = ap * LANES
    groups = SB_W // width
    hds = [(pp, hh) for pp in range(ap) for hh in range(2)]

    def body(q_ref, k_ref, v_ref, do_ref, car_ref, tri_ref, pre_ref, dq_ref, dk_ref, dv_ref, dq_acc):
        i = pl.program_id(2)

        @pl.when(i == 0)
        def _():
            dk_ref[...] = jnp.zeros_like(dk_ref)
            dv_ref[...] = jnp.zeros_like(dv_ref)

        lane = lax.broadcasted_iota(jnp.int32, (t, LANES), 1)
        row = lax.broadcasted_iota(jnp.int32, (t, t), 0)
        col = lax.broadcasted_iota(jnp.int32, (t, t), 1)
        strict = col < row
        tri, pre = tri_ref[...], pre_ref[...]
        masks = [_head_mask(lane, hh) for hh in range(2)]
        qms = [q_ref[:, _pair(pp)] * masks[hh] for pp, hh in hds]
        doms = [do_ref[:, _pair(pp)].astype(BF16) * masks[hh] for pp, hh in hds]
        cars = [car_ref[:, _pair(pp)] for pp in range(ap)]
        dq_acc[...] = jnp.zeros_like(dq_acc)

        def step(kb, g_pres, valid):
            start = pl.multiple_of(kb * t, t)
            kss = [k_ref[pl.ds(start, t), _pair(pp)] for pp in range(ap)]
            vss = [v_ref[pl.ds(start, t), _pair(pp)] for pp in range(ap)]
            zs = [_dot(qms[n], kss[pp], NT) for n, (pp, _) in enumerate(hds)]
            dws = [_dot(doms[n], vss[pp], NT) for n, (pp, _) in enumerate(hds)]
            logs = [_sb_logits(z, valid) for z in zs]
            sufs = [_split_dot(lg[0], tri) for lg in logs]
            ws, gs = [], []
            for n, (pp, hh) in enumerate(hds):
                c_sum = jnp.sum(jnp.where(lane == hh * 8 + kb, cars[pp], 0.0), axis=1, keepdims=True)
                w = jnp.exp(logs[n][1] + sufs[n] + c_sum)
                if valid is not None:
                    w = jnp.where(valid, w, 0.0)
                ws.append(w)
                gs.append(dws[n] * w)
            befores = [g_pres[n] + _split_dot(gs[n], pre) for n in range(len(hds))]
            for pp in range(ap):
                dv_ref[pl.ds(start, t), _pair(pp)] += (_dot(ws[2 * pp].astype(BF16), doms[2 * pp], TN)
                                                       + _dot(ws[2 * pp + 1].astype(BF16), doms[2 * pp + 1], TN))
            dzbs = []
            for n in range(len(hds)):
                beta = jnp.exp(logs[n][1])
                dz = gs[n] * (1.0 - beta) - beta * befores[n]
                if valid is not None:
                    dz = jnp.where(valid, dz, 0.0)
                dzbs.append(dz.astype(BF16))
            for pp in range(ap):
                a, b = 2 * pp, 2 * pp + 1
                dq_acc[pp] += _dot(dzbs[a], kss[pp] * masks[0]) + _dot(dzbs[b], kss[pp] * masks[1])
                dk_ref[pl.ds(start, t), _pair(pp)] += _dot(dzbs[a], qms[a], TN) + _dot(dzbs[b], qms[b], TN)
            return tuple(g_pres[n] + jnp.sum(gs[n], axis=1, keepdims=True) for n in range(len(hds)))

        g_pres = lax.fori_loop(0, i, lambda kb, cr: step(kb, cr, None), tuple(jnp.zeros((t, 1), F32) for _ in hds))
        step(i, g_pres, strict)
        for pp in range(ap):
            dq_ref[:, _pair(pp)] = (dq_acc[pp] * SB_SCALE).astype(BF16)

    qspec = pl.BlockSpec((t, width), lambda b, p, i: (b * nq + i, p))
    kspec_out = pl.BlockSpec((seq, width), lambda b, p, i: (b, p))
    return _attention_call(
        body, ex, "sb_bwd", (nb, groups, nq),
        [qkv, qkv, qkv, d_out, cars, _tri(t, "suffix"), _tri(t, "prefix")],
        [qspec,
         pl.BlockSpec((seq, width), lambda b, p, i: (b, groups + p)),
         pl.BlockSpec((seq, width), lambda b, p, i: (b, 2 * groups + p)),
         qspec, qspec, _const_spec((2 * t, t)), _const_spec((2 * t, t))],
        [jax.ShapeDtypeStruct((rows, SB_W), BF16), jax.ShapeDtypeStruct((rows, SB_W), F32),
         jax.ShapeDtypeStruct((rows, SB_W), F32)],
        [qspec, kspec_out, kspec_out], scratch=[pltpu.VMEM((ap, t, LANES), F32)])


def _mla_scores(qh, ks, allowed):
    s = _dot(qh, ks, NT) * MLA_SCALE
    if allowed is not None:
        s = jnp.where(allowed, s, jnp.finfo(F32).min)
    return s


def _mla_fwd(qp, kp, vm, seq, ex=None, chunk=64):
    rows = qp.shape[0]
    nb = rows // seq
    t = min(ATTN_TILE, seq)
    nq = seq // t
    shift = int(math.log2(chunk))
    ap = ATTN_PAIRS["mla_fwd"]
    width = ap * LANES
    groups = MLA_W // width
    hds = [(pp, hh) for pp in range(ap) for hh in range(2)]

    def body(q_ref, k_ref, v_ref, o_ref, lse_ref, acc_ref):
        i = pl.program_id(2)
        lane = lax.broadcasted_iota(jnp.int32, (t, LANES), 1)
        row = lax.broadcasted_iota(jnp.int32, (t, t), 0)
        col = lax.broadcasted_iota(jnp.int32, (t, t), 1)
        allowed_diag = jnp.right_shift(col, shift) <= jnp.right_shift(row, shift)
        masks = [_head_mask(lane, hh) for hh in range(2)]
        qhs = [q_ref[:, _pair(n)] for n in range(len(hds))]
        acc_ref[...] = jnp.zeros_like(acc_ref)

        def step(kb, carry, allowed):
            start = pl.multiple_of(kb * t, t)
            vss = [v_ref[pl.ds(start, t), _pair(pp)] for pp in range(ap)]
            new = []
            ahead = _mla_scores(qhs[0], k_ref[pl.ds(start, t), _pair(0)], allowed)
            for n, (pp, hh) in enumerate(hds):
                m_run, l_run = carry[n]
                s = ahead
                if n + 1 < len(hds):
                    ahead = _mla_scores(qhs[n + 1], k_ref[pl.ds(start, t), _pair(n + 1)], allowed)
                m_new = jnp.maximum(m_run, jnp.max(s, axis=1, keepdims=True))
                p = jnp.exp(s - m_new)
                scale = jnp.exp(m_run - m_new)
                l_run = scale * l_run + jnp.sum(p, axis=1, keepdims=True)
                acc_ref[n] = scale * acc_ref[n] + _dot(p.astype(BF16), vss[pp] * masks[hh])
                new.append((m_new, l_run))
            return tuple(new)

        init = (jnp.full((t, 1), jnp.finfo(F32).min, F32), jnp.zeros((t, 1), F32))
        carry = step(i, tuple(init for _ in hds), allowed_diag)
        carry = lax.fori_loop(0, i, lambda kb, cr: step(kb, cr, None), carry)
        for pp in range(ap):
            out = jnp.zeros((t, LANES), F32)
            lses = jnp.zeros((t, LANES), F32)
            for hh in range(2):
                m_run, l_run = carry[2 * pp + hh]
                out = out + acc_ref[2 * pp + hh] / l_run
                lses = jnp.where(lane == hh, m_run + jnp.log(l_run), lses)
            o_ref[:, _pair(pp)] = out
            lse_ref[:, _pair(pp)] = lses

    ospec = pl.BlockSpec((t, width), lambda b, p, i: (b * nq + i, p))
    return _attention_call(
        body, ex, "mla_fwd", (nb, groups, nq), [qp, kp, vm],
        [pl.BlockSpec((t, 2 * width), lambda b, p, i: (b * nq + i, p)),
         pl.BlockSpec((seq, 2 * width), lambda b, p, i: (b, p)),
         pl.BlockSpec((seq, width), lambda b, p, i: (b, p))],
        [jax.ShapeDtypeStruct((rows, MLA_W), F32), jax.ShapeDtypeStruct((rows, MLA_W), F32)],
        [ospec, ospec], scratch=[pltpu.VMEM((len(hds), t, LANES), F32)])


def _mla_bwd(qp, kp, vm, d_out, out, lse, seq, ex=None, chunk=64):
    rows = qp.shape[0]
    nb = rows // seq
    t = min(ATTN_TILE, seq)
    nq = seq // t
    shift = int(math.log2(chunk))
    ap = ATTN_PAIRS["mla_bwd"]
    width = ap * LANES
    groups = MLA_W // width
    hds = [(pp, hh) for pp in range(ap) for hh in range(2)]
    nh = len(hds)

    def body(q_ref, k_ref, v_ref, do_ref, o_ref, lse_ref, dq_ref, dk_ref, dv_ref):
        i = pl.program_id(2)

        @pl.when(i == 0)
        def _():
            dk_ref[...] = jnp.zeros_like(dk_ref)
            dv_ref[...] = jnp.zeros_like(dv_ref)

        lane = lax.broadcasted_iota(jnp.int32, (t, LANES), 1)
        row = lax.broadcasted_iota(jnp.int32, (t, t), 0)
        col = lax.broadcasted_iota(jnp.int32, (t, t), 1)
        allowed_diag = jnp.right_shift(col, shift) <= jnp.right_shift(row, shift)
        qhs = [q_ref[:, _pair(n)] for n in range(nh)]
        doms, deltas, lse_hs = [], [], []
        for pp, hh in hds:
            do = do_ref[:, _pair(pp)]
            d_o = do * o_ref[:, _pair(pp)]
            doms.append(do.astype(BF16) * _head_mask(lane, hh))
            deltas.append(jnp.sum(jnp.where((lane >= 64) if hh else (lane < 64), d_o, 0.0), axis=1, keepdims=True))
            lse_hs.append(jnp.sum(jnp.where(lane == hh, lse_ref[:, _pair(pp)], 0.0), axis=1, keepdims=True))

        dq_ref[...] = jnp.zeros_like(dq_ref)

        def step(kb, allowed):
            start = pl.multiple_of(kb * t, t)
            vss = [v_ref[pl.ds(start, t), _pair(pp)] for pp in range(ap)]
            kss = [k_ref[pl.ds(start, t), _pair(n)] for n in range(nh)]
            scores = [_mla_scores(qhs[n], kss[n], allowed) for n in range(nh)]
            dps = [_dot(doms[n], vss[pp], NT) for n, (pp, _) in enumerate(hds)]
            ps = [jnp.exp(scores[n] - lse_hs[n]) for n in range(nh)]
            dss = [(ps[n] * (dps[n] - deltas[n]) * MLA_SCALE).astype(BF16) for n in range(nh)]
            for pp in range(ap):
                a, b = 2 * pp, 2 * pp + 1
                dv_ref[pl.ds(start, t), _pair(pp)] += (_dot(ps[a].astype(BF16), doms[a], TN)
                                                       + _dot(ps[b].astype(BF16), doms[b], TN))
            for n in range(nh):
                dk_ref[pl.ds(start, t), _pair(n)] += _dot(dss[n], qhs[n], TN)
                dq_ref[:, _pair(n)] += _dot(dss[n], kss[n])

        def off_diagonal(kb, nothing):
            step(kb, None)
            return nothing

        lax.fori_loop(0, i, off_diagonal, 0)
        step(i, allowed_diag)

    ospec = pl.BlockSpec((t, width), lambda b, p, i: (b * nq + i, p))
    return _attention_call(
        body, ex, "mla_bwd", (nb, groups, nq), [qp, kp, vm, d_out, out, lse],
        [pl.BlockSpec((t, 2 * width), lambda b, p, i: (b * nq + i, p)),
         pl.BlockSpec((seq, 2 * width), lambda b, p, i: (b, p)),
         pl.BlockSpec((seq, width), lambda b, p, i: (b, p)),
         pl.BlockSpec((t, width), lambda b, p, i: (b * nq + i, groups + p)),
         ospec, ospec],
        [jax.ShapeDtypeStruct((rows, HEADS * LANES), F32), jax.ShapeDtypeStruct((rows, HEADS * LANES), F32),
         jax.ShapeDtypeStruct((rows, MLA_W), F32)],
        [pl.BlockSpec((t, 2 * width), lambda b, p, i: (b * nq + i, p)),
         pl.BlockSpec((seq, 2 * width), lambda b, p, i: (b, p)),
         pl.BlockSpec((seq, width), lambda b, p, i: (b, p))])


PACK_COLS = 1024
PACK_ALIGN = 16
GROUP_IN = (384, ((1024, 552, 1), (384, 192, 1), (256, 256, 1)))
GROUP_MLP = (1152, ((1024, 1024, 1), (1024, 1024, 0), (256, 1024, 0)))


def _pack_rows(r, c):
    return (r // 2) * c // PACK_COLS


def _slot_rows(r, c):
    return -(-_pack_rows(r, c) // PACK_ALIGN) * PACK_ALIGN


def _join_slots(parts, group):
    total, weights = group
    padded = [jnp.pad(p, ((0, 0), (0, _slot_rows(r, c) - p.shape[1]), (0, 0))) for p, (r, c, _) in zip(parts, weights)]
    used = sum(_slot_rows(r, c) for r, c, _ in weights)
    if total > used:
        padded.append(jnp.zeros((parts[0].shape[0], total - used, PACK_COLS), parts[0].dtype))
    return jnp.concatenate(padded, axis=1)


def _split_slots(packed, group):
    out, at = [], 0
    for r, c, _ in group[1]:
        out.append(packed[:, at:at + _pack_rows(r, c), :])
        at += _slot_rows(r, c)
    return out


def _pack_halves(shards, group):
    return _join_slots([s.reshape(2, _pack_rows(r, c), PACK_COLS) for s, (r, c, _) in zip(shards, group[1])], group)


def _unpack_halves(packed, group):
    return [p.reshape(r, c) for p, (r, c, _) in zip(_split_slots(packed, group), group[1])]


def _unpack_full(gathered, group):
    out = []
    for p, (r, c, axis) in zip(_split_slots(gathered, group), group[1]):
        shards = p.reshape(4, r, c)
        out.append(shards.reshape(4 * r, c) if axis == 0 else jnp.moveaxis(shards, 0, 1).reshape(r, 4 * c))
    return out


def _pack_full(grads, group):
    parts = []
    for gr, (r, c, axis) in zip(grads, group[1]):
        shards = gr.reshape(4, r, c) if axis == 0 else jnp.moveaxis(gr.reshape(r, 4, c), 1, 0)
        parts.append(shards.reshape(8, _pack_rows(r, c), PACK_COLS))
    return _join_slots(parts, group)


def _pad_w_in(w_in):
    z = jnp.zeros((D_MODEL, 1), w_in.dtype)
    return jnp.concatenate([w_in[:, :2176], jnp.tile(z, (1, 64)), w_in[:, 2176:], jnp.tile(z, (1, 32))], axis=1)


def _unpad_w_in(g):
    return jnp.concatenate([g[:, :2176], g[:, 2240:2272]], axis=1)


def _pad_heads(w, used):
    k = w.shape[0]
    w3 = w.reshape(k, HEADS, used)
    return jnp.pad(w3, ((0, 0), (0, 0), (0, LANES - used))).reshape(k, HEADS * LANES)


def _unpad_heads(g, used):
    k = g.shape[0]
    return g.reshape(k, HEADS, LANES)[:, :, :used].reshape(k, HEADS * used)


def _rope_tables(seq):
    inv_freq = 1.0 / (ROPE_BASE ** (jnp.arange(0, ROPE, 2, dtype=F32) / ROPE))
    ang = jnp.arange(seq, dtype=F32)[:, None] * inv_freq[None, :]
    cos, sin = jnp.cos(ang), jnp.sin(ang)
    one, zero = jnp.ones((seq, NOPE), F32), jnp.zeros((seq, NOPE), F32)
    z16, z32 = jnp.zeros((seq, 16), F32), jnp.zeros((seq, 32), F32)
    cos_t = jnp.concatenate([one, cos, cos, jnp.ones((seq, 32), F32)], axis=1)
    sin_a = jnp.concatenate([zero, -sin, z16, z32], axis=1)
    sin_b = jnp.concatenate([zero, z16, sin, z32], axis=1)
    return cos_t, sin_a, sin_b


SMALL = (("ln_in_g", 1024), ("ln_in_b", 1024), ("b_ada", 6144), ("q_norm_g", 384), ("kv_norm_g", 256),
         ("ln1_g", 1024), ("ln1_b", 1024), ("ln2_g", 1024), ("ln2_b", 1024))
SUBLANES = 8
SMALL_SLOTS = [-(-n // LANES // SUBLANES) * SUBLANES for _, n in SMALL]
SMALL_AT = [sum(SMALL_SLOTS[:p]) for p in range(len(SMALL))]
SMALL_ROWS = sum(SMALL_SLOTS)


def _pack_small(vals):
    parts = []
    for v, slot in zip(vals, SMALL_SLOTS):
        rows = v.reshape(-1, LANES)
        parts.append(jnp.pad(rows, ((0, slot - rows.shape[0]), (0, 0))))
    return jnp.concatenate(parts, axis=0)


def kernel(x, c, ln_in_g, ln_in_b, w_ada, b_ada, w_in, q_norm_g, kv_norm_g, w_uq, w_ukv, w_o, ln1_g, ln1_b, w_up, w_down, ln2_g, ln2_b, loss_target, m_ln_in_g, m_ln_in_b, m_w_ada, m_b_ada, m_w_in, m_q_norm_g, m_kv_norm_g, m_w_uq, m_w_ukv, m_w_o, m_ln1_g, m_ln1_b, m_w_up, m_w_down, m_ln2_g, m_ln2_b, v_ln_in_g, v_ln_in_b, v_w_ada, v_b_ada, v_w_in, v_q_norm_g, v_kv_norm_g, v_w_uq, v_w_ukv, v_w_o, v_ln1_g, v_ln1_b, v_w_up, v_w_down, v_ln2_g, v_ln2_b):
    nb, seq, _ = x.shape
    rows = nb * seq
    ix, iy, ic = lax.axis_index("x"), lax.axis_index("y"), lax.axis_index("c")
    chip = 2 * ix + iy
    dev = 2 * chip + ic

    def my_half(shards, group):
        packed = _pack_halves([s.astype(BF16) for s in shards], group)
        return lax.dynamic_index_in_dim(packed, ic, 0, keepdims=False)

    f_in, f_uq, f_ukv = _unpack_full(_gather8(my_half([w_in[0], w_uq[0], w_ukv[0]], GROUP_IN), "gather_w_in"),
                                     GROUP_IN)
    half_mlp = my_half([w_up[0], w_down[0], w_o[0]], GROUP_MLP)
    late_weights = _gather_exchange(half_mlp)
    w_in_p = _pad_w_in(f_in)
    uq3 = f_uq.reshape(Q_RANK, HEADS, NOPE + ROPE)
    w_uq_p = jnp.pad(uq3, ((0, 0), (0, 0), (0, LANES - NOPE - ROPE))).reshape(Q_RANK, HEADS * LANES)
    w_ukv_p = jnp.concatenate([_pad_heads(f_ukv[:, :HEADS * NOPE], NOPE), f_ukv[:, HEADS * NOPE:]], axis=1)

    n_all = 8 * nb
    c_all = _gather8(c.reshape(-1, LANES), "gather_c").reshape(n_all, D_MODEL)
    ada_cols = w_ada.shape[2]
    b_sh = lax.dynamic_slice_in_dim(b_ada, chip * ada_cols, ada_cols, axis=1)
    mod_sh = _ada_fwd(c_all, w_ada[0], b_sh)
    mod_g = _gather8(mod_sh, "gather_mod")[0::2]
    mod_all = jnp.moveaxis(mod_g, 0, 1).reshape(n_all, N_MOD * D_MODEL)
    mod_mine = lax.dynamic_slice_in_dim(mod_all, dev * nb, nb, axis=0).reshape(nb, N_MOD, D_MODEL)
    mod = jnp.pad(mod_mine, ((0, 0), (0, 8 - N_MOD), (0, 0)))

    cos_t, sin_a, sin_b = _rope_tables(seq)
    row2 = lambda v: v.reshape(1, -1)

    x2d = x.reshape(rows, D_MODEL)
    x0, h, qkv, lat, qp, kp, vm = _fwd_in(x2d, mod, row2(ln_in_g), row2(ln_in_b), w_in_p, q_norm_g, kv_norm_g,
                                          w_uq_p, w_ukv_p, cos_t, sin_a, sin_b, seq)
    sb_y, cars, g_mlp = _sb_fwd(qkv, seq, late_weights)
    g_mlp = _with_own(g_mlp, half_mlp)
    f_o = _split_slots(g_mlp, GROUP_MLP)[2].reshape(D_MODEL, D_MODEL)
    mla_y, lse = _mla_fwd(qp, kp, vm, seq)
    mix, y1, h2, u, ff, y2 = _fwd_out(sb_y, mla_y, x0, mod, f_o, ln1_g, ln1_b, g_mlp, seq)

    dy1, dmix, d_attn, dff, du, acc_out, dmod_a = _bwd_out(
        y2, loss_target.reshape(rows, D_MODEL), ff, u, y1, mix, mod, ln2_g, ln2_b, ln1_g, ln1_b, g_mlp, f_o, seq)
    c_idx = ic.reshape(1).astype(jnp.int32)
    blocks_mlp = _wgrad_packed(h2, du, "wgrad_up", lambda i, j: 2 * j + i, 0)
    blocks_mlp = _wgrad_packed(u, dff, "wgrad_down", lambda i, j: i, 1, pre="relu2", into=blocks_mlp)
    blocks_mlp = _wgrad_packed(sb_y, dmix, "wgrad_o_sb", lambda i, j: 0, 8, split=4, into=blocks_mlp)
    blocks_mlp = _wgrad_packed(mla_y, dmix, "wgrad_o_mla", lambda i, j: 1, 8, split=4, into=blocks_mlp)
    dq_sb, dk_sb, dv_sb, sibling_mlp = _sb_bwd(qkv, d_attn, cars, seq, _swap_cores_exchange(blocks_mlp))
    part_mlp, part_mlp_bf = _add_pairs(blocks_mlp, sibling_mlp, c_idx, "grad_add_cores_mlp")
    dqp, dkp, dvm, chips_mlp = _mla_bwd(qp, kp, vm, d_attn, mla_y, lse, seq, _scatter_chips_exchange(part_mlp_bf))
    grad_x, dproj, dqall, dkv, latn, acc0, acc_lat, dmod_c = _bwd_in(
        dqp, dkp, dvm, dq_sb, dk_sb, dv_sb, lat, x2d, x0, dy1, mod, row2(ln_in_g), row2(ln_in_b), w_in_p,
        q_norm_g, kv_norm_g, w_uq_p, w_ukv_p, cos_t, sin_a, sin_b, seq)

    g_in = _unpad_w_in(_wgrad(h, dproj, "wgrad_in", tn=768))
    g_uq = _unpad_heads(_wgrad(latn[:, :Q_RANK], dqall, "wgrad_uq"), NOPE + ROPE)
    g_ukv_p = _wgrad(latn[:, Q_RANK:], dkv, "wgrad_ukv", tn=512)
    g_ukv = jnp.concatenate([_unpad_heads(g_ukv_p[:, :HEADS * LANES], NOPE), g_ukv_p[:, HEADS * LANES:]], axis=1)
    blocks_in = _pack_full([g_in, g_uq, g_ukv], GROUP_IN)
    sibling_in = _run_exchange(_swap_cores_exchange(blocks_in), "grads_in_to_sibling")
    part_in, part_in_bf = _add_pairs(blocks_in, sibling_in, c_idx, "grad_add_cores_in")
    chips_in = _run_exchange(_scatter_chips_exchange(part_in_bf), "grads_in_to_chips")

    def own(part):
        return lax.dynamic_index_in_dim(part, chip, 0, keepdims=False)

    half = jnp.concatenate([_add_chips(own(part_in), chips_in, "grad_add_chips_in"),
                            _add_chips(own(part_mlp), chips_mlp, "grad_add_chips_mlp")], axis=0)
    other = _run_exchange(_swap_one_exchange(half), "grads_halves")
    both = jnp.where(ic == 0, jnp.stack([half, other]), jnp.stack([other, half]))
    gs_in, gs_uq, gs_ukv = _unpack_halves(both[:, :GROUP_IN[0]], GROUP_IN)
    gs_up, gs_down, gs_o = _unpack_halves(both[:, GROUP_IN[0]:], GROUP_MLP)

    dmod = (dmod_a + dmod_c)[:, :N_MOD, :]
    small_part = _pack_small([acc0[0], acc0[1], jnp.zeros((N_MOD * D_MODEL,), F32), acc_lat[0, :Q_RANK],
                              acc_lat[1, :KV_RANK], acc_out[3], acc_out[4], acc_out[0], acc_out[1]])
    n_sum = SMALL_ROWS + D_MODEL // LANES
    payload = jnp.concatenate([small_part, acc_out[2].reshape(-1, LANES), dmod.reshape(-1, LANES)], axis=0)
    gathered = _gather8(payload, "gather_small")
    small_sum = _sum_lead(gathered[:, :n_sum, :], "sum_small")
    loss = jnp.sum(small_sum[SMALL_ROWS:])
    dmod_all = gathered[:, n_sum:, :].reshape(n_all, N_MOD * D_MODEL)
    g_b_ada = _sum_lead(dmod_all.reshape(n_all, N_MOD * D_MODEL // LANES, LANES), "sum_b_ada")
    dmod_sh = lax.dynamic_slice_in_dim(dmod_all, chip * ada_cols, ada_cols, axis=1)
    g_w_ada = _ada_bwd(c_all, dmod_sh)

    big_w = {"w_ada": (w_ada[0], g_w_ada, m_w_ada[0], v_w_ada[0]), "w_in": (w_in[0], gs_in, m_w_in[0], v_w_in[0]),
             "w_uq": (w_uq[0], gs_uq, m_w_uq[0], v_w_uq[0]), "w_ukv": (w_ukv[0], gs_ukv, m_w_ukv[0], v_w_ukv[0]),
             "w_o": (w_o[0], gs_o, m_w_o[0], v_w_o[0]), "w_up": (w_up[0], gs_up, m_w_up[0], v_w_up[0]),
             "w_down": (w_down[0], gs_down, m_w_down[0], v_w_down[0])}
    res = {}
    for name, (w, g, m, v) in big_w.items():
        d, mn, vn = _adamw(w, g, m, v, "adamw_" + name)
        res[name] = (g[None], d[None], mn[None], vn[None])
    small_w = [ln_in_g, ln_in_b, b_ada, q_norm_g, kv_norm_g, ln1_g, ln1_b, ln2_g, ln2_b]
    small_m = [m_ln_in_g, m_ln_in_b, m_b_ada, m_q_norm_g, m_kv_norm_g, m_ln1_g, m_ln1_b, m_ln2_g, m_ln2_b]
    small_v = [v_ln_in_g, v_ln_in_b, v_b_ada, v_q_norm_g, v_kv_norm_g, v_ln1_g, v_ln1_b, v_ln2_g, v_ln2_b]
    for (name, _), quad in zip(SMALL, _adamw_small(small_sum, g_b_ada, small_w, small_m, small_v)):
        res[name] = quad

    order = ["ln_in_g", "ln_in_b", "w_ada", "b_ada", "w_in", "q_norm_g", "kv_norm_g", "w_uq", "w_ukv", "w_o",
             "ln1_g", "ln1_b", "w_up", "w_down", "ln2_g", "ln2_b"]
    outs = [loss, grad_x.reshape(nb, seq, D_MODEL)]
    for k in range(4):
        outs += [res[name][k] for name in order]
    return tuple(outs)
```

```python
import functools
import math

import jax
import jax.numpy as jnp
from jax import lax
from jax.experimental import pallas as pl
from jax.experimental.pallas import tpu as pltpu

F32 = jnp.float32
BF16 = jnp.bfloat16
MESH_IDS = pl.DeviceIdType.MESH

D_MODEL = 1024
HEADS = 8
HEAD_PAIRS = HEADS // 2
SB_W = 512
MLA_W = 512
NOPE = 64
ROPE = 32
Q_RANK = 384
KV_RANK = 256
D_IN = 2208
D_IN_PAD = 2304
D_FF = 4096
N_MOD = 6
LN_EPS = 1e-5
RMS_EPS = 1e-6
ALPHA = 2.0 ** 0.25
ROPE_BASE = 10000.0
SB_SCALE = 64 ** -0.5
MLA_SCALE = 96 ** -0.5
ADAM_LR = 0.001
ADAM_B1 = 0.9
ADAM_B2 = 0.999
ADAM_EPS = 1e-08
ADAM_WD = 0.01
ADAM_STEP = 10

LANES = 128
ROW_TILE = 256
ATTN_TILE = 256
CAR_SLOTS = 8
ATTN_PAIRS = 4
VMEM_LIMIT = 56 << 20

NT = (((1,), (1,)), ((), ()))
TN = (((0,), (0,)), ((), ()))


def _params(sem=None):
    return pltpu.CompilerParams(vmem_limit_bytes=VMEM_LIMIT, dimension_semantics=sem)


def _const_spec(shape):
    zeros = (0,) * len(shape)
    return pl.BlockSpec(shape, lambda *_: zeros, pipeline_mode=pl.Buffered(1))


def _dot(a, b, dims=None):
    if dims is None:
        return jnp.dot(a, b, preferred_element_type=F32)
    return lax.dot_general(a, b, dims, preferred_element_type=F32)


def _mean(v):
    return jnp.mean(v, axis=-1, keepdims=True)


def _rowsum(v):
    return jnp.sum(v, axis=0, keepdims=True)


def _ln_fwd(y, g, b):
    mu = _mean(y)
    yc = y - mu
    rstd = lax.rsqrt(_mean(yc * yc) + LN_EPS)
    xhat = yc * rstd
    return xhat * g + b, xhat, rstd


def _ln_bwd(dx, xhat, rstd, g):
    dxh = dx * g
    return rstd * (dxh - _mean(dxh) - xhat * _mean(dxh * xhat))


def _rope(v, cos, sin_a, sin_b):
    return v * cos + pltpu.roll(v, 112, 1) * sin_a + pltpu.roll(v, 16, 1) * sin_b


def _rope_t(dv, cos, sin_a, sin_b):
    return dv * cos + pltpu.roll(dv * sin_a, 16, 1) + pltpu.roll(dv * sin_b, 112, 1)


def _my_place():
    return lax.axis_index("x"), lax.axis_index("y"), lax.axis_index("c")


class _Exchange:
    def __init__(self, operand, out_shape, n_copies, phases):
        self.operand = operand
        self.out_shape = out_shape
        self.phases = phases
        self.scratch = [pltpu.SemaphoreType.DMA((n_copies,)), pltpu.SemaphoreType.DMA((n_copies,))]


def _run_exchange(ex, name):
    def body(in_ref, out_ref, send_sems, recv_sems):
        for phase in ex.phases(in_ref, out_ref, send_sems, recv_sems):
            phase()

    return pl.pallas_call(
        body, name=name, out_shape=ex.out_shape,
        in_specs=[pl.BlockSpec(memory_space=pl.ANY)], out_specs=pl.BlockSpec(memory_space=pl.ANY),
        scratch_shapes=ex.scratch,
    )(ex.operand)


def _nothing():
    pass


def _gather_exchange(v):
    m, n = v.shape

    def phases(v_ref, out_ref, send_sems, recv_sems):
        x, y, c = _my_place()
        me, sibling = (x, y, c), (x, y, 1 - c)
        chips = [(1 - x, y), (x, 1 - y), (1 - x, 1 - y)]

        def rows(px, py, pc):
            return out_ref.at[4 * px + 2 * py + pc]

        def copy(k, block, to, src=None):
            return pltpu.make_async_remote_copy(
                src_ref=rows(*block) if src is None else src, dst_ref=rows(*block),
                send_sem=send_sems.at[k], recv_sem=recv_sems.at[k], device_id=to, device_id_type=MESH_IDS)

        first = [copy(0, me, sibling, src=v_ref)]
        first += [copy(1 + j, me, (*chip, c), src=v_ref) for j, chip in enumerate(chips)]
        passed = [copy(4 + j, (*chip, c), sibling) for j, chip in enumerate(chips)]

        def start():
            for cp in first:
                cp.start()

        def middle():
            for j, chip in enumerate(chips):
                copy(1 + j, (*chip, c), me).wait_recv()
                passed[j].start()

        def finish():
            copy(0, sibling, me).wait_recv()
            for j, chip in enumerate(chips):
                copy(4 + j, (*chip, 1 - c), me).wait_recv()
            for cp in first + passed:
                cp.wait_send()

        return start, middle, finish

    return _Exchange(v, jax.ShapeDtypeStruct((8, m, n), v.dtype), 7, phases)


def _with_own(gathered, v):
    dev = 4 * lax.axis_index("x") + 2 * lax.axis_index("y") + lax.axis_index("c")
    return lax.dynamic_update_index_in_dim(gathered, v, dev, 0)


def _direct_exchange(operand, out_shape, n_copies, make_copies):
    def phases(in_ref, out_ref, send_sems, recv_sems):
        copies = make_copies(in_ref, out_ref, send_sems, recv_sems)

        def start():
            for cp in copies:
                cp.start()

        def finish():
            for cp in copies:
                cp.wait()

        return start, _nothing, finish

    return _Exchange(operand, out_shape, n_copies, phases)


def _swap_cores_exchange(blocks):
    _, m, n = blocks.shape

    def make_copies(g_ref, out_ref, send_sems, recv_sems):
        x, y, c = _my_place()
        return [pltpu.make_async_remote_copy(
            src_ref=g_ref.at[2 * j + (1 - c)], dst_ref=out_ref.at[j],
            send_sem=send_sems.at[j], recv_sem=recv_sems.at[j],
            device_id=(x, y, 1 - c), device_id_type=MESH_IDS) for j in range(4)]

    return _direct_exchange(blocks, jax.ShapeDtypeStruct((4, m, n), blocks.dtype), 4, make_copies)


def _scatter_chips_exchange(parts):
    _, m, n = parts.shape
    flips = [(1, 0), (0, 1), (1, 1)]

    def make_copies(p_ref, out_ref, send_sems, recv_sems):
        x, y, c = _my_place()
        copies = []
        for k, (fx, fy) in enumerate(flips):
            tx = 1 - x if fx else x
            ty = 1 - y if fy else y
            copies.append(pltpu.make_async_remote_copy(
                src_ref=p_ref.at[2 * tx + ty], dst_ref=out_ref.at[k],
                send_sem=send_sems.at[k], recv_sem=recv_sems.at[k],
                device_id=(tx, ty, c), device_id_type=MESH_IDS))
        return copies

    return _direct_exchange(parts, jax.ShapeDtypeStruct((3, m, n), parts.dtype), 3, make_copies)


def _swap_one_exchange(v):
    def make_copies(v_ref, out_ref, send_sems, recv_sems):
        x, y, c = _my_place()
        return [pltpu.make_async_remote_copy(src_ref=v_ref, dst_ref=out_ref, send_sem=send_sems.at[0],
                                             recv_sem=recv_sems.at[0], device_id=(x, y, 1 - c),
                                             device_id_type=MESH_IDS)]

    return _direct_exchange(v, jax.ShapeDtypeStruct(v.shape, v.dtype), 1, make_copies)


def _gather8(v, name):
    return _with_own(_run_exchange(_gather_exchange(v), name), v)


def _carried(ex, refs, n_in, n_out, n_scratch):
    ins, ex_in = refs[:n_in], refs[n_in]
    outs, ex_out = refs[n_in + 1:n_in + 1 + n_out], refs[n_in + 1 + n_out]
    at = n_in + 2 + n_out
    return ins, outs + refs[at:at + n_scratch], ex.phases(ex_in, ex_out, *refs[at + n_scratch:])


def _ada_fwd(c_all, w_ada_sh, b_ada_sh):
    nb, cols = c_all.shape[0], w_ada_sh.shape[1]
    tn = 512

    def body(c_ref, w_ref, b_ref, o_ref):
        cv = c_ref[...]
        act = (cv * jax.nn.sigmoid(cv)).astype(BF16)
        o_ref[...] = _dot(act, w_ref[...].astype(BF16)) + b_ref[...]

    return pl.pallas_call(
        body, name="ada_fwd", grid=(cols // tn,),
        out_shape=jax.ShapeDtypeStruct((nb, cols), F32),
        in_specs=[pl.BlockSpec((nb, D_MODEL), lambda j: (0, 0)),
                  pl.BlockSpec((D_MODEL, tn), lambda j: (0, j)),
                  pl.BlockSpec((1, tn), lambda j: (0, j))],
        out_specs=pl.BlockSpec((nb, tn), lambda j: (0, j)),
        compiler_params=_params(("arbitrary",)),
    )(c_all, w_ada_sh, b_ada_sh)


def _ada_bwd(c_all, dmod_sh):
    nb, cols = dmod_sh.shape
    tn = 512

    def body(c_ref, d_ref, o_ref):
        cv = c_ref[...]
        act = (cv * jax.nn.sigmoid(cv)).astype(BF16)
        o_ref[...] = _dot(act, d_ref[...].astype(BF16), TN)

    return pl.pallas_call(
        body, name="ada_bwd", grid=(cols // tn,),
        out_shape=jax.ShapeDtypeStruct((D_MODEL, cols), F32),
        in_specs=[pl.BlockSpec((nb, D_MODEL), lambda j: (0, 0)),
                  pl.BlockSpec((nb, tn), lambda j: (0, j))],
        out_specs=pl.BlockSpec((D_MODEL, tn), lambda j: (0, j)),
        compiler_params=_params(("arbitrary",)),
    )(c_all, dmod_sh)


def _sum_lead(v, name):
    k, m, n = v.shape

    def body(v_ref, o_ref):
        acc = v_ref[0]
        for i in range(1, k):
            acc = acc + v_ref[i]
        o_ref[...] = acc

    return pl.pallas_call(
        body, name=name, out_shape=jax.ShapeDtypeStruct((m, n), F32),
        in_specs=[pl.BlockSpec((k, m, n), lambda: (0, 0, 0))],
        out_specs=pl.BlockSpec((m, n), lambda: (0, 0)),
        compiler_params=_params(),
    )(v)


def _adamw_math(w, g, m, v):
    mn = ADAM_B1 * m + (1.0 - ADAM_B1) * g
    vn = ADAM_B2 * v + (1.0 - ADAM_B2) * (g * g)
    m_hat = mn / (1.0 - ADAM_B1 ** ADAM_STEP)
    v_hat = vn / (1.0 - ADAM_B2 ** ADAM_STEP)
    return -ADAM_LR * (m_hat / (jnp.sqrt(v_hat) + ADAM_EPS) + ADAM_WD * w), mn, vn


def _adamw_small(g_sum, g_b_ada, ws, ms, vs):
    n = len(SMALL)

    def body(gs_ref, gb_ref, *refs):
        outs = refs[3 * n:]
        for p in range(n):
            rows_p = SMALL[p][1] // LANES
            g = gb_ref[...] if SMALL[p][0] == "b_ada" else gs_ref[SMALL_AT[p]:SMALL_AT[p] + rows_p, :]
            d, mn, vn = _adamw_math(refs[p][...], g, refs[n + p][...], refs[2 * n + p][...])
            outs[4 * p][...] = g
            outs[4 * p + 1][...] = d
            outs[4 * p + 2][...] = mn
            outs[4 * p + 3][...] = vn

    shapes = [jax.ShapeDtypeStruct((size // LANES, LANES), F32) for _, size in SMALL for _ in range(4)]
    flat = lambda arrs: [a.reshape(-1, LANES) for a in arrs]
    res = pl.pallas_call(body, name="adamw_small", out_shape=tuple(shapes), compiler_params=_params())(
        g_sum, g_b_ada, *flat(ws), *flat(ms), *flat(vs))
    return [tuple(r.reshape(w.shape) for r in res[4 * p:4 * p + 4]) for p, w in enumerate(ws)]


def _adamw(w, g, m, v, name):
    rows, cols = w.shape
    tr = rows
    while tr * cols * 4 > (2 << 20) and tr % 16 == 0:
        tr //= 2

    def body(w_ref, g_ref, m_ref, v_ref, d_ref, mo_ref, vo_ref):
        d_ref[...], mo_ref[...], vo_ref[...] = _adamw_math(w_ref[...], g_ref[...], m_ref[...], v_ref[...])

    spec = pl.BlockSpec((tr, cols), lambda i: (i, 0))
    shape = jax.ShapeDtypeStruct((rows, cols), F32)
    return pl.pallas_call(
        body, name=name, grid=(rows // tr,), out_shape=(shape, shape, shape),
        in_specs=[spec, spec, spec, spec], out_specs=(spec, spec, spec),
        compiler_params=_params(("arbitrary",)),
    )(w, g, m, v)


def _add_rows(m, n):
    fits = [d for d in range(16, m + 1, 16) if m % d == 0 and d * n * 4 <= (5 << 19)]
    assert fits, (m, n)
    return max(fits)


def _add_pairs(blocks, recv, c_idx, name):
    _, m, n = blocks.shape
    tr = _add_rows(m, n)

    def body(c_ref, a_ref, b_ref, o_ref, ob_ref):
        s = a_ref[...] + b_ref[...]
        o_ref[...] = s
        ob_ref[...] = s.astype(BF16)

    grid_spec = pltpu.PrefetchScalarGridSpec(
        num_scalar_prefetch=1, grid=(4, m // tr),
        in_specs=[pl.BlockSpec((1, tr, n), lambda j, i, c: (2 * j + c[0], i, 0)),
                  pl.BlockSpec((1, tr, n), lambda j, i, c: (j, i, 0))],
        out_specs=(pl.BlockSpec((1, tr, n), lambda j, i, c: (j, i, 0)),
                   pl.BlockSpec((1, tr, n), lambda j, i, c: (j, i, 0))))
    return pl.pallas_call(
        body, name=name, grid_spec=grid_spec,
        out_shape=(jax.ShapeDtypeStruct((4, m, n), F32), jax.ShapeDtypeStruct((4, m, n), BF16)),
        compiler_params=_params(("arbitrary", "arbitrary")),
    )(c_idx, blocks, recv)


def _add_chips(own, recv, name):
    m, n = own.shape
    tr = _add_rows(m, n)

    def body(a_ref, r_ref, o_ref):
        acc = a_ref[...]
        for k in range(3):
            acc = acc + r_ref[k].astype(F32)
        o_ref[...] = acc

    return pl.pallas_call(
        body, name=name, grid=(m // tr,),
        out_shape=jax.ShapeDtypeStruct((m, n), F32),
        in_specs=[pl.BlockSpec((tr, n), lambda i: (i, 0)), pl.BlockSpec((3, tr, n), lambda i: (0, i, 0))],
        out_specs=pl.BlockSpec((tr, n), lambda i: (i, 0)),
        compiler_params=_params(("arbitrary",)),
    )(own, recv)


def _row_spec(cols):
    return pl.BlockSpec((ROW_TILE, cols), lambda i: (i, 0))


def _mod_spec(tiles_per_seq):
    return pl.BlockSpec((1, 8, D_MODEL), lambda i: (i // tiles_per_seq, 0, 0))


def _table_spec(tiles_per_seq):
    return pl.BlockSpec((ROW_TILE, LANES), lambda i: (i % tiles_per_seq, 0))


def _fwd_in(x, mod, ln_g, ln_b, w_in, q_g, kv_g, w_uq, w_ukv, cos_t, sin_a, sin_b, seq):
    rows = x.shape[0]
    tm = ROW_TILE
    tps = seq // tm

    def body(x_ref, mod_ref, g_ref, b_ref, win_ref, qg_ref, kvg_ref, wuq_ref, wukv_ref, cos_ref, sa_ref, sb_ref,
             x0_ref, h_ref, qkv_ref, lat_ref, qp_ref, kp_ref, vm_ref):
        x0, _, _ = _ln_fwd(x_ref[...], g_ref[...], b_ref[...])
        x0_ref[...] = x0
        h = (x0 * (1.0 + mod_ref[0, 1:2, :]) + mod_ref[0, 0:1, :]).astype(BF16)
        h_ref[...] = h
        proj = _dot(h, win_ref[...])
        qkv_ref[:, :SB_W] = (proj[:, :SB_W] * SB_SCALE).astype(BF16)
        qkv_ref[:, SB_W:] = proj[:, SB_W:3 * SB_W].astype(BF16)
        lat_ref[...] = proj[:, 3 * SB_W:3 * SB_W + Q_RANK + KV_RANK]
        cq = proj[:, 3 * SB_W:3 * SB_W + Q_RANK]
        ckv = proj[:, 3 * SB_W + Q_RANK:3 * SB_W + Q_RANK + KV_RANK]
        kr = proj[:, D_IN_PAD - LANES:]
        cos, sa, sb = cos_ref[...], sa_ref[...], sb_ref[...]
        cqn = (cq * lax.rsqrt(_mean(cq * cq) + RMS_EPS) * qg_ref[...]).astype(BF16)
        q_all = _dot(cqn, wuq_ref[...])
        for hd in range(HEADS):
            sl = slice(hd * LANES, (hd + 1) * LANES)
            qp_ref[:, sl] = _rope(q_all[:, sl], cos, sa, sb).astype(BF16)
        ckvn = (ckv * lax.rsqrt(_mean(ckv * ckv) + RMS_EPS) * kvg_ref[...]).astype(BF16)
        kv = _dot(ckvn, wukv_ref[...])
        kr_rot = _rope(kr, cos, sa, sb)
        for hd in range(HEADS):
            sl = slice(hd * LANES, (hd + 1) * LANES)
            kp_ref[:, sl] = (kv[:, sl] + kr_rot).astype(BF16)
        vm_ref[...] = kv[:, HEADS * LANES:].astype(BF16)

    outs = [(D_MODEL, F32), (D_MODEL, BF16), (3 * SB_W, BF16), (Q_RANK + KV_RANK, F32),
            (HEADS * LANES, BF16), (HEADS * LANES, BF16), (MLA_W, BF16)]
    return pl.pallas_call(
        body, name="fwd_in", grid=(rows // tm,),
        out_shape=tuple(jax.ShapeDtypeStruct((rows, n), dt) for n, dt in outs),
        in_specs=[_row_spec(D_MODEL), _mod_spec(tps), _const_spec((1, D_MODEL)), _const_spec((1, D_MODEL)),
                  _const_spec(w_in.shape), _const_spec((1, Q_RANK)), _const_spec((1, KV_RANK)),
                  _const_spec(w_uq.shape), _const_spec(w_ukv.shape),
                  _table_spec(tps), _table_spec(tps), _table_spec(tps)],
        out_specs=tuple(_row_spec(n) for n, _ in outs),
        compiler_params=_params(("arbitrary",)),
    )(x, mod, ln_g, ln_b, w_in, q_g, kv_g, w_uq, w_ukv, cos_t, sin_a, sin_b)


HALF = 512
SHARD = 1024


def _mlp_weight_specs():
    return [pl.BlockSpec((8, HALF, SHARD), lambda i: (0, 0, 0), pipeline_mode=pl.Buffered(1)),
            pl.BlockSpec((8, HALF, SHARD), lambda i: (0, 1, 0), pipeline_mode=pl.Buffered(1))]


def _fwd_out(sb_y, mla_y, x0, mod, w_o, ln_g, ln_b, g_mlp, seq):
    rows = x0.shape[0]
    tm = ROW_TILE
    tps = seq // tm

    def body(sb_ref, ml_ref, x0_ref, mod_ref, wo_ref, g_ref, b_ref, wu_ref, wd_ref,
             mix_ref, y1_ref, h2_ref, u_ref, ff_ref, y2_ref):
        mix = _dot(sb_ref[...], wo_ref[:SB_W, :]) + _dot(ml_ref[...].astype(BF16), wo_ref[SB_W:, :])
        mix_ref[...] = mix
        y1 = ALPHA * x0_ref[...] + (1.0 + mod_ref[0, 2:3, :]) * mix
        y1_ref[...] = y1
        x1, _, _ = _ln_fwd(y1, g_ref[...], b_ref[...])
        h2 = (x1 * (1.0 + mod_ref[0, 4:5, :]) + mod_ref[0, 3:4, :]).astype(BF16)
        h2_ref[...] = h2
        h_lo, h_hi = h2[:, :HALF], h2[:, HALF:]
        ff = jnp.zeros((tm, D_MODEL), F32)
        for chip in range(4):
            u = _dot(h_lo, wu_ref[2 * chip]) + _dot(h_hi, wu_ref[2 * chip + 1])
            u_ref[:, chip * SHARD:(chip + 1) * SHARD] = u.astype(BF16)
            act = jnp.square(jnp.maximum(u, 0.0)).astype(BF16)
            ff = ff + _dot(act[:, :HALF], wd_ref[2 * chip]) + _dot(act[:, HALF:], wd_ref[2 * chip + 1])
        ff_ref[...] = ff
        y2_ref[...] = ALPHA * x1 + (1.0 + mod_ref[0, 5:6, :]) * ff

    outs = [(D_MODEL, F32), (D_MODEL, F32), (D_MODEL, BF16), (D_FF, BF16), (D_MODEL, F32), (D_MODEL, F32)]
    return pl.pallas_call(
        body, name="fwd_out", grid=(rows // tm,),
        out_shape=tuple(jax.ShapeDtypeStruct((rows, n), dt) for n, dt in outs),
        in_specs=[_row_spec(SB_W), _row_spec(MLA_W), _row_spec(D_MODEL), _mod_spec(tps), _const_spec(w_o.shape),
                  _const_spec((1, D_MODEL)), _const_spec((1, D_MODEL))] + _mlp_weight_specs(),
        out_specs=tuple(_row_spec(n) for n, _ in outs),
        compiler_params=_params(("arbitrary",)),
    )(sb_y, mla_y, x0, mod, w_o, ln_g, ln_b, g_mlp, g_mlp)


def _acc_spec(rows=8, cols=D_MODEL):
    return pl.BlockSpec((rows, cols), lambda i: (0, 0))


def _bwd_out(y2, tgt, ff, u, y1, mix, mod, ln2_g, ln2_b, ln1_g, ln1_b, g_mlp, w_o, seq):
    rows = y2.shape[0]
    nb = rows // seq
    tm = ROW_TILE
    tps = seq // tm

    def body(y2_ref, t_ref, ff_ref, u_ref, y1_ref, mix_ref, mod_ref, g2_ref, b2_ref, g_ref, b_ref, wu_ref, wd_ref,
             wo_ref, dy1_ref, dmix_ref, do_ref, dff_ref, du_ref, acc_ref, dmod_ref):
        i = pl.program_id(0)

        @pl.when(i == 0)
        def _():
            acc_ref[...] = jnp.zeros_like(acc_ref)

        @pl.when(i % tps == 0)
        def _():
            dmod_ref[...] = jnp.zeros_like(dmod_ref)

        g2 = g2_ref[...]
        x2, xhat2, rstd2 = _ln_fwd(y2_ref[...], g2, b2_ref[...])
        err = x2 - t_ref[...]
        dx2 = err * (1.0 / D_MODEL)
        acc_ref[0:1, :] += _rowsum(dx2 * xhat2)
        acc_ref[1:2, :] += _rowsum(dx2)
        acc_ref[2:3, :] += _rowsum(err * err) * (0.5 / D_MODEL)
        dy2 = _ln_bwd(dx2, xhat2, rstd2, g2)
        dmod_ref[0, 5:6, :] += _rowsum(dy2 * ff_ref[...])
        dff = ((1.0 + mod_ref[0, 5:6, :]) * dy2).astype(BF16)
        dff_ref[...] = dff
        for blk in range(8):
            cols = slice(blk * HALF, (blk + 1) * HALF)
            da = _dot(dff, wd_ref[blk], NT)
            du_ref[:, cols] = (da * (2.0 * jnp.maximum(u_ref[:, cols].astype(F32), 0.0))).astype(BF16)

        g = g_ref[...]
        x1, xhat, rstd = _ln_fwd(y1_ref[...], g, b_ref[...])
        halves = []
        for half in range(2):
            acc = jnp.zeros((tm, HALF), F32)
            for chip in range(4):
                acc = acc + _dot(du_ref[:, chip * SHARD:(chip + 1) * SHARD], wu_ref[2 * chip + half], NT)
            halves.append(acc)
        dh2 = jnp.concatenate(halves, axis=1)
        dmod_ref[0, 3:4, :] += _rowsum(dh2)
        dmod_ref[0, 4:5, :] += _rowsum(dh2 * x1)
        dx1 = ALPHA * dy2 + dh2 * (1.0 + mod_ref[0, 4:5, :])
        acc_ref[3:4, :] += _rowsum(dx1 * xhat)
        acc_ref[4:5, :] += _rowsum(dx1)
        dy1 = _ln_bwd(dx1, xhat, rstd, g)
        dy1_ref[...] = dy1
        dmod_ref[0, 2:3, :] += _rowsum(dy1 * mix_ref[...])
        dmix = ((1.0 + mod_ref[0, 2:3, :]) * dy1).astype(BF16)
        dmix_ref[...] = dmix
        do_ref[...] = _dot(dmix, wo_ref[...], NT)

    outs = [(D_MODEL, F32), (D_MODEL, BF16), (D_MODEL, F32), (D_MODEL, BF16), (D_FF, BF16)]
    return pl.pallas_call(
        body, name="bwd_out", grid=(rows // tm,),
        out_shape=tuple(jax.ShapeDtypeStruct((rows, n), dt) for n, dt in outs)
        + (jax.ShapeDtypeStruct((8, D_MODEL), F32), jax.ShapeDtypeStruct((nb, 8, D_MODEL), F32)),
        in_specs=[_row_spec(D_MODEL), _row_spec(D_MODEL), _row_spec(D_MODEL), _row_spec(D_FF), _row_spec(D_MODEL),
                  _row_spec(D_MODEL), _mod_spec(tps), _const_spec((1, D_MODEL)), _const_spec((1, D_MODEL)),
                  _const_spec((1, D_MODEL)), _const_spec((1, D_MODEL))] + _mlp_weight_specs()
        + [_const_spec(w_o.shape)],
        out_specs=tuple(_row_spec(n) for n, _ in outs) + (_acc_spec(), _mod_spec(tps)),
        compiler_params=_params(("arbitrary",)),
    )(y2, tgt, ff, u, y1, mix, mod, ln2_g, ln2_b, ln1_g, ln1_b, g_mlp, g_mlp, w_o)


def _bwd_in(dqp, dkp, dvm, dq_sb, dk_sb, dv_sb, lat, x, x0, dy1, mod, ln_g, ln_b, w_in, q_g, kv_g, w_uq, w_ukv,
            cos_t, sin_a, sin_b, seq):
    rows = x.shape[0]
    nb = rows // seq
    tm = ROW_TILE
    tps = seq // tm
    n_lat = Q_RANK + KV_RANK

    def body(dqp_ref, dkp_ref, dvm_ref, dqs_ref, dks_ref, dvs_ref, lat_ref, x_ref, x0_ref, dy1_ref, mod_ref,
             g_ref, b_ref, win_ref, qg_ref, kvg_ref, wuq_ref, wukv_ref, cos_ref, sa_ref, sb_ref,
             dx_ref, dproj_ref, dqall_ref, dkv_ref, latn_ref, acc_ref, accl_ref, dmod_ref):
        i = pl.program_id(0)

        @pl.when(i == 0)
        def _():
            acc_ref[...] = jnp.zeros_like(acc_ref)
            accl_ref[...] = jnp.zeros_like(accl_ref)

        @pl.when(i % tps == 0)
        def _():
            dmod_ref[...] = jnp.zeros_like(dmod_ref)

        cos, sa, sb = cos_ref[...], sa_ref[...], sb_ref[...]
        lane = lax.broadcasted_iota(jnp.int32, (tm, LANES), 1)
        for hd in range(HEADS):
            sl = slice(hd * LANES, (hd + 1) * LANES)
            dqall_ref[:, sl] = _rope_t(dqp_ref[:, sl], cos, sa, sb).astype(BF16)
        dcqn = _dot(dqall_ref[...], wuq_ref[...], NT)
        cq = lat_ref[:, :Q_RANK]
        qg = qg_ref[...]
        rq = lax.rsqrt(_mean(cq * cq) + RMS_EPS)
        cqn = cq * rq
        latn_ref[:, :Q_RANK] = (cqn * qg).astype(BF16)
        accl_ref[0:1, :Q_RANK] += _rowsum(dcqn * cqn)
        dqg = dcqn * qg
        dcq = rq * (dqg - cqn * _mean(dqg * cqn))
        dkr = jnp.zeros((tm, LANES), F32)
        for hd in range(HEADS):
            sl = slice(hd * LANES, (hd + 1) * LANES)
            dk = dkp_ref[:, sl]
            dkr = dkr + dk
            dkv_ref[:, sl] = jnp.where(lane < NOPE, dk, 0.0).astype(BF16)
        dkv_ref[:, HEADS * LANES:] = dvm_ref[...].astype(BF16)
        dckvn = _dot(dkv_ref[...], wukv_ref[...], NT)
        ckv = lat_ref[:, Q_RANK:]
        kvg = kvg_ref[...]
        rkv = lax.rsqrt(_mean(ckv * ckv) + RMS_EPS)
        ckvn = ckv * rkv
        latn_ref[:, Q_RANK:] = (ckvn * kvg).astype(BF16)
        accl_ref[1:2, :KV_RANK] += _rowsum(dckvn * ckvn)
        dkg = dckvn * kvg
        dckv = rkv * (dkg - ckvn * _mean(dkg * ckvn))
        dkr = _rope_t(jnp.where(lane >= NOPE, dkr, 0.0), cos, sa, sb)
        dproj_ref[:, :SB_W] = dqs_ref[...]
        dproj_ref[:, SB_W:2 * SB_W] = dks_ref[...].astype(BF16)
        dproj_ref[:, 2 * SB_W:3 * SB_W] = dvs_ref[...].astype(BF16)
        dproj_ref[:, 3 * SB_W:3 * SB_W + Q_RANK] = dcq.astype(BF16)
        dproj_ref[:, 3 * SB_W + Q_RANK:3 * SB_W + n_lat] = dckv.astype(BF16)
        dproj_ref[:, D_IN_PAD - LANES:] = dkr.astype(BF16)
        dh = _dot(dproj_ref[...], win_ref[...], NT)
        x0 = x0_ref[...]
        dmod_ref[0, 0:1, :] += _rowsum(dh)
        dmod_ref[0, 1:2, :] += _rowsum(dh * x0)
        dx0 = ALPHA * dy1_ref[...] + dh * (1.0 + mod_ref[0, 1:2, :])
        g = g_ref[...]
        _, xhat, rstd = _ln_fwd(x_ref[...], g, b_ref[...])
        acc_ref[0:1, :] += _rowsum(dx0 * xhat)
        acc_ref[1:2, :] += _rowsum(dx0)
        dx_ref[...] = _ln_bwd(dx0, xhat, rstd, g)

    outs = [(D_MODEL, F32), (D_IN_PAD, BF16), (HEADS * LANES, BF16), (HEADS * LANES + MLA_W, BF16), (n_lat, BF16)]
    return pl.pallas_call(
        body, name="bwd_in", grid=(rows // tm,),
        out_shape=tuple(jax.ShapeDtypeStruct((rows, n), dt) for n, dt in outs)
        + (jax.ShapeDtypeStruct((8, D_MODEL), F32), jax.ShapeDtypeStruct((8, Q_RANK), F32),
           jax.ShapeDtypeStruct((nb, 8, D_MODEL), F32)),
        in_specs=[_row_spec(HEADS * LANES), _row_spec(HEADS * LANES), _row_spec(MLA_W),
                  _row_spec(SB_W), _row_spec(SB_W), _row_spec(SB_W), _row_spec(n_lat),
                  _row_spec(D_MODEL), _row_spec(D_MODEL), _row_spec(D_MODEL), _mod_spec(tps),
                  _const_spec((1, D_MODEL)), _const_spec((1, D_MODEL)), _const_spec(w_in.shape),
                  _const_spec((1, Q_RANK)), _const_spec((1, KV_RANK)), _const_spec(w_uq.shape),
                  _const_spec(w_ukv.shape), _table_spec(tps), _table_spec(tps), _table_spec(tps)],
        out_specs=tuple(_row_spec(n) for n, _ in outs) + (_acc_spec(), _acc_spec(8, Q_RANK), _mod_spec(tps)),
        compiler_params=_params(("arbitrary",)),
    )(dqp, dkp, dvm, dq_sb, dk_sb, dv_sb, lat, x, x0, dy1, mod, ln_g, ln_b, w_in, q_g, kv_g, w_uq, w_ukv,
      cos_t, sin_a, sin_b)


def _wgrad(a, b, name, pre=None, tm=512, tn=1024, tk=2048):
    rows, m = a.shape
    n = b.shape[1]
    tm, tn, tk = min(tm, m), min(tn, n), min(tk, rows)
    if m % tm:
        tm = m
    if n % tn:
        tn = n

    def body(a_ref, b_ref, o_ref):
        @pl.when(pl.program_id(2) == 0)
        def _():
            o_ref[...] = jnp.zeros_like(o_ref)

        av = a_ref[...]
        if pre == "relu2":
            av = jnp.square(jnp.maximum(av.astype(F32), 0.0))
        o_ref[...] += _dot(av.astype(BF16), b_ref[...].astype(BF16), TN)

    return pl.pallas_call(
        body, name=name, grid=(m // tm, n // tn, rows // tk),
        out_shape=jax.ShapeDtypeStruct((m, n), F32),
        in_specs=[pl.BlockSpec((tk, tm), lambda i, j, k: (k, i)), pl.BlockSpec((tk, tn), lambda i, j, k: (k, j))],
        out_specs=pl.BlockSpec((tm, tn), lambda i, j, k: (i, j)),
        compiler_params=_params(("arbitrary", "arbitrary", "arbitrary")),
    )(a, b)


def _wgrad_packed(a, b, name, block_of, row_block, split=1, pre=None, into=None, tk=2048):
    rows, m = a.shape
    n = b.shape[1]
    tm = HALF
    part = tm // split
    tk = min(tk, rows)
    shape = jax.ShapeDtypeStruct((8, GROUP_MLP[0], PACK_COLS), F32)

    def body(a_ref, b_ref, *rest):
        o_ref = rest[-1]

        @pl.when(pl.program_id(2) == 0)
        def _():
            o_ref[...] = jnp.zeros_like(o_ref)

        av = a_ref[...]
        if pre == "relu2":
            av = jnp.square(jnp.maximum(av.astype(F32), 0.0))
        prod = _dot(av.astype(BF16), b_ref[...].astype(BF16), TN)
        for s in range(split):
            o_ref[s] += prod[s * part:(s + 1) * part]

    in_specs = [pl.BlockSpec((tk, tm), lambda i, j, k: (k, i)), pl.BlockSpec((tk, SHARD), lambda i, j, k: (k, j))]
    operands = [a, b]
    if into is not None:
        in_specs.append(pl.BlockSpec(memory_space=pl.ANY))
        operands.append(into)
    return pl.pallas_call(
        body, name=name, grid=(m // tm, n // SHARD, rows // tk), out_shape=shape,
        in_specs=in_specs,
        out_specs=pl.BlockSpec((split, part, SHARD), lambda i, j, k: (block_of(i, j), row_block, 0)),
        input_output_aliases={} if into is None else {2: 0},
        compiler_params=_params(("arbitrary", "arbitrary", "arbitrary")),
    )(*operands)


def _pair(pp):
    return slice(pp * LANES, (pp + 1) * LANES)


def _head_mask(lane, hh):
    return jnp.where((lane >= 64) if hh else (lane < 64), 1.0, 0.0).astype(BF16)


def _tri(t, kind):
    s = lax.broadcasted_iota(jnp.int32, (t, t), 0)
    j = lax.broadcasted_iota(jnp.int32, (t, t), 1)
    one = jnp.where(j > s if kind == "later" else j < s, 1.0, 0.0).astype(BF16)
    return jnp.concatenate([one, one], axis=1)


def _split_dot(tri2, v):
    hi = v.astype(BF16)
    lo = (v - hi.astype(F32)).astype(BF16)
    return _dot(tri2, jnp.concatenate([hi, lo], axis=0))


def _sb_logits(z, valid):
    log_keep = -(jnp.maximum(z, 0.0) + jnp.log(1.0 + jnp.exp(-jnp.abs(z))))
    log_beta = z + log_keep
    if valid is not None:
        log_keep = jnp.where(valid, log_keep, 0.0)
    return log_keep, log_beta


def _attention_call(body, ex, name, grid, operands, in_specs, out_shapes, out_specs, scratch=()):
    n_in, n_out = len(operands), len(out_shapes)
    total = grid[0] * grid[1] * grid[2]
    any_spec = pl.BlockSpec(memory_space=pl.ANY)

    def carrier(*refs):
        ins, outs, (start, middle, finish) = _carried(ex, refs, n_in, n_out, len(scratch))
        step = (pl.program_id(0) * grid[1] + pl.program_id(1)) * grid[2] + pl.program_id(2)
        pl.when(step == 0)(start)
        pl.when(step == total // 2)(middle)
        body(*ins, *outs)
        pl.when(step == total - 1)(finish)

    carried = ex is not None
    return pl.pallas_call(
        carrier if carried else body, name=name, grid=grid,
        out_shape=tuple(out_shapes) + ((ex.out_shape,) if carried else ()),
        in_specs=list(in_specs) + ([any_spec] if carried else []),
        out_specs=tuple(out_specs) + ((any_spec,) if carried else ()),
        scratch_shapes=list(scratch) + (ex.scratch if carried else []),
        compiler_params=_params(("arbitrary", "arbitrary", "arbitrary")),
    )(*operands, *([ex.operand] if carried else []))


def _sb_fwd(qkv, seq, ex=None):
    rows = qkv.shape[0]
    nb = rows // seq
    t = min(ATTN_TILE, seq)
    nq = seq // t
    assert nq <= CAR_SLOTS, (seq, t)
    ap = ATTN_PAIRS
    width = ap * LANES
    groups = SB_W // width
    hds = [(pp, hh) for pp in range(ap) for hh in range(2)]

    def body(q_ref, k_ref, v_ref, tri_ref, o_ref, car_ref, acc_ref):
        i = pl.program_id(2)
        lane = lax.broadcasted_iota(jnp.int32, (t, LANES), 1)
        key = lax.broadcasted_iota(jnp.int32, (t, t), 0)
        qry = lax.broadcasted_iota(jnp.int32, (t, t), 1)
        strict = key < qry
        tri = tri_ref[...]
        masks = [_head_mask(lane, hh) for hh in range(2)]
        qms = [q_ref[:, _pair(pp)] * masks[hh] for pp, hh in hds]
        acc_ref[...] = jnp.zeros_like(acc_ref)
        car_ref[...] = jnp.zeros_like(car_ref)

        def step(kb, c_sums, valid):
            start = pl.multiple_of(kb * t, t)
            kss = [k_ref[pl.ds(start, t), _pair(pp)] for pp in range(ap)]
            vss = [v_ref[pl.ds(start, t), _pair(pp)] for pp in range(ap)]
            zs = [_dot(kss[pp], qms[n], NT) for n, (pp, _) in enumerate(hds)]
            logs = [_sb_logits(z, valid) for z in zs]
            sufs = [_split_dot(tri, lg[0]) for lg in logs]
            new_sums = []
            for n, (pp, hh) in enumerate(hds):
                log_keep, log_beta = logs[n]
                w = jnp.exp(log_beta + sufs[n] + c_sums[n])
                if valid is not None:
                    w = jnp.where(valid, w, 0.0)
                acc_ref[pp] += _dot(vss[pp] * masks[hh], w.astype(BF16), TN)
                car_ref[0, pl.ds(n * CAR_SLOTS + kb, 1), :] = c_sums[n]
                new_sums.append(c_sums[n] + jnp.sum(log_keep, axis=0, keepdims=True))
            return tuple(new_sums)

        c_sums = step(i, tuple(jnp.zeros((1, t), F32) for _ in hds), strict)
        lax.fori_loop(0, i, lambda j, cr: step(i - 1 - j, cr, None), c_sums)
        for pp in range(ap):
            o_ref[:, _pair(pp)] = acc_ref[pp].T.astype(BF16)

    qspec = pl.BlockSpec((t, width), lambda b, p, i: (b * nq + i, p))
    car_rows = len(hds) * CAR_SLOTS
    return _attention_call(
        body, ex, "sb_fwd", (nb, groups, nq),
        [qkv, qkv, qkv, _tri(t, "later")],
        [qspec,
         pl.BlockSpec((seq, width), lambda b, p, i: (b, groups + p)),
         pl.BlockSpec((seq, width), lambda b, p, i: (b, 2 * groups + p)),
         _const_spec((t, 2 * t))],
        [jax.ShapeDtypeStruct((rows, SB_W), BF16), jax.ShapeDtypeStruct((nb * nq, HEADS * CAR_SLOTS, t), F32)],
        [qspec, pl.BlockSpec((1, car_rows, t), lambda b, p, i: (b * nq + i, p, 0))],
        scratch=[pltpu.VMEM((ap, LANES, t), F32)])


def _sb_bwd(qkv, d_out, cars, seq, ex=None):
    rows = qkv.shape[0]
    nb = rows // seq
    t = min(ATTN_TILE, seq)
    nq = seq // t
    ap = ATTN_PAIRS
    width = ap * LANES
    groups = SB_W // width
    hds = [(pp, hh) for pp in range(ap) for hh in range(2)]

    def body(q_ref, k_ref, v_ref, do_ref, car_ref, tri_ref, pre_ref, dq_ref, dk_ref, dv_ref, dq_acc):
        i = pl.program_id(2)

        @pl.when(i == 0)
        def _():
            dk_ref[...] = jnp.zeros_like(dk_ref)
            dv_ref[...] = jnp.zeros_like(dv_ref)

        lane = lax.broadcasted_iota(jnp.int32, (t, LANES), 1)
        key = lax.broadcasted_iota(jnp.int32, (t, t), 0)
        qry = lax.broadcasted_iota(jnp.int32, (t, t), 1)
        strict = key < qry
        tri, pre = tri_ref[...], pre_ref[...]
        masks = [_head_mask(lane, hh) for hh in range(2)]
        qms = [q_ref[:, _pair(pp)] * masks[hh] for pp, hh in hds]
        doms = [do_ref[:, _pair(pp)].astype(BF16) * masks[hh] for pp, hh in hds]
        dq_acc[...] = jnp.zeros_like(dq_acc)

        def step(kb, g_pres, valid):
            start = pl.multiple_of(kb * t, t)
            kss = [k_ref[pl.ds(start, t), _pair(pp)] for pp in range(ap)]
            vss = [v_ref[pl.ds(start, t), _pair(pp)] for pp in range(ap)]
            zs = [_dot(kss[pp], qms[n], NT) for n, (pp, _) in enumerate(hds)]
            dws = [_dot(vss[pp], doms[n], NT) for n, (pp, _) in enumerate(hds)]
            logs = [_sb_logits(z, valid) for z in zs]
            sufs = [_split_dot(tri, lg[0]) for lg in logs]
            ws, gs = [], []
            for n in range(len(hds)):
                c_sum = car_ref[0, pl.ds(n * CAR_SLOTS + kb, 1), :]
                w = jnp.exp(logs[n][1] + sufs[n] + c_sum)
                if valid is not None:
                    w = jnp.where(valid, w, 0.0)
                ws.append(w)
                gs.append(dws[n] * w)
            befores = [g_pres[n] + _split_dot(pre, gs[n]) for n in range(len(hds))]
            for pp in range(ap):
                a, b = 2 * pp, 2 * pp + 1
                dv_ref[pl.ds(start, t), _pair(pp)] += _dot(ws[a].astype(BF16), doms[a]) + _dot(ws[b].astype(BF16), doms[b])
            dzbs = []
            for n in range(len(hds)):
                beta = jnp.exp(logs[n][1])
                dz = gs[n] * (1.0 - beta) - beta * befores[n]
                if valid is not None:
                    dz = jnp.where(valid, dz, 0.0)
                dzbs.append(dz.astype(BF16))
            for pp in range(ap):
                a, b = 2 * pp, 2 * pp + 1
                dq_acc[pp] += _dot(dzbs[a], kss[pp] * masks[0], TN) + _dot(dzbs[b], kss[pp] * masks[1], TN)
                dk_ref[pl.ds(start, t), _pair(pp)] += _dot(dzbs[a], qms[a]) + _dot(dzbs[b], qms[b])
            return tuple(g_pres[n] + jnp.sum(gs[n], axis=0, keepdims=True) for n in range(len(hds)))

        g_pres = lax.fori_loop(0, i, lambda kb, cr: step(kb, cr, None), tuple(jnp.zeros((1, t), F32) for _ in hds))
        step(i, g_pres, strict)
        for pp in range(ap):
            dq_ref[:, _pair(pp)] = (dq_acc[pp] * SB_SCALE).astype(BF16)

    qspec = pl.BlockSpec((t, width), lambda b, p, i: (b * nq + i, p))
    kspec_out = pl.BlockSpec((seq, width), lambda b, p, i: (b, p))
    car_rows = len(hds) * CAR_SLOTS
    return _attention_call(
        body, ex, "sb_bwd", (nb, groups, nq),
        [qkv, qkv, qkv, d_out, cars, _tri(t, "later"), _tri(t, "earlier")],
        [qspec,
         pl.BlockSpec((seq, width), lambda b, p, i: (b, groups + p)),
         pl.BlockSpec((seq, width), lambda b, p, i: (b, 2 * groups + p)),
         qspec, pl.BlockSpec((1, car_rows, t), lambda b, p, i: (b * nq + i, p, 0)),
         _const_spec((t, 2 * t)), _const_spec((t, 2 * t))],
        [jax.ShapeDtypeStruct((rows, SB_W), BF16), jax.ShapeDtypeStruct((rows, SB_W), F32),
         jax.ShapeDtypeStruct((rows, SB_W), F32)],
        [qspec, kspec_out, kspec_out],
        scratch=[pltpu.VMEM((ap, t, LANES), F32)])


def _mla_scores(qh, ks, allowed):
    s = _dot(qh, ks, NT) * MLA_SCALE
    if allowed is not None:
        s = jnp.where(allowed, s, jnp.finfo(F32).min)
    return s


def _mla_fwd(qp, kp, vm, seq, ex=None, chunk=64):
    rows = qp.shape[0]
    nb = rows // seq
    t = min(ATTN_TILE, seq)
    nq = seq // t
    shift = int(math.log2(chunk))
    ap = ATTN_PAIRS
    width = ap * LANES
    groups = MLA_W // width
    hds = [(pp, hh) for pp in range(ap) for hh in range(2)]

    def body(q_ref, k_ref, v_ref, o_ref, lse_ref, acc_ref):
        i = pl.program_id(2)
        lane = lax.broadcasted_iota(jnp.int32, (t, LANES), 1)
        key = lax.broadcasted_iota(jnp.int32, (t, t), 0)
        qry = lax.broadcasted_iota(jnp.int32, (t, t), 1)
        allowed_diag = jnp.right_shift(key, shift) <= jnp.right_shift(qry, shift)
        masks = [_head_mask(lane, hh) for hh in range(2)]
        qhs = [q_ref[:, _pair(n)] for n in range(len(hds))]
        acc_ref[...] = jnp.zeros_like(acc_ref)

        def step(kb, carry, allowed):
            start = pl.multiple_of(kb * t, t)
            vss = [v_ref[pl.ds(start, t), _pair(pp)] for pp in range(ap)]
            scores = [_mla_scores(k_ref[pl.ds(start, t), _pair(n)], qhs[n], allowed) for n in range(len(hds))]
            new = []
            for n, (pp, hh) in enumerate(hds):
                m_run, l_run = carry[n]
                s = scores[n]
                m_new = jnp.maximum(m_run, jnp.max(s, axis=0, keepdims=True))
                p = jnp.exp(s - m_new)
                scale = jnp.exp(m_run - m_new)
                l_run = scale * l_run + jnp.sum(p, axis=0, keepdims=True)
                acc_ref[n] = scale * acc_ref[n] + _dot(vss[pp] * masks[hh], p.astype(BF16), TN)
                new.append((m_new, l_run))
            return tuple(new)

        init = (jnp.full((1, t), jnp.finfo(F32).min, F32), jnp.zeros((1, t), F32))
        carry = step(i, tuple(init for _ in hds), allowed_diag)
        carry = lax.fori_loop(0, i, lambda kb, cr: step(kb, cr, None), carry)
        lse_rows = []
        for pp in range(ap):
            out_t = jnp.zeros((LANES, t), F32)
            for hh in range(2):
                m_run, l_run = carry[2 * pp + hh]
                out_t = out_t + acc_ref[2 * pp + hh] / l_run
                lse_rows.append(m_run + jnp.log(l_run))
            o_ref[:, _pair(pp)] = out_t.T
        lse_t = jnp.concatenate(lse_rows + [jnp.zeros((LANES - len(hds), t), F32)], axis=0)
        lse_ref[...] = jnp.zeros_like(lse_ref)
        lse_ref[:, _pair(0)] = lse_t.T

    ospec = pl.BlockSpec((t, width), lambda b, p, i: (b * nq + i, p))
    return _attention_call(
        body, ex, "mla_fwd", (nb, groups, nq), [qp, kp, vm],
        [pl.BlockSpec((t, 2 * width), lambda b, p, i: (b * nq + i, p)),
         pl.BlockSpec((seq, 2 * width), lambda b, p, i: (b, p)),
         pl.BlockSpec((seq, width), lambda b, p, i: (b, p))],
        [jax.ShapeDtypeStruct((rows, MLA_W), F32), jax.ShapeDtypeStruct((rows, MLA_W), F32)],
        [ospec, ospec], scratch=[pltpu.VMEM((len(hds), LANES, t), F32)])


def _mla_bwd(qp, kp, vm, d_out, out, lse, seq, ex=None, chunk=64):
    rows = qp.shape[0]
    nb = rows // seq
    t = min(ATTN_TILE, seq)
    nq = seq // t
    shift = int(math.log2(chunk))
    ap = ATTN_PAIRS
    width = ap * LANES
    groups = MLA_W // width
    hds = [(pp, hh) for pp in range(ap) for hh in range(2)]
    nh = len(hds)

    def body(q_ref, k_ref, v_ref, do_ref, o_ref, lse_ref, dq_ref, dk_ref, dv_ref):
        i = pl.program_id(2)

        @pl.when(i == 0)
        def _():
            dk_ref[...] = jnp.zeros_like(dk_ref)
            dv_ref[...] = jnp.zeros_like(dv_ref)

        lane = lax.broadcasted_iota(jnp.int32, (t, LANES), 1)
        key = lax.broadcasted_iota(jnp.int32, (t, t), 0)
        qry = lax.broadcasted_iota(jnp.int32, (t, t), 1)
        allowed_diag = jnp.right_shift(key, shift) <= jnp.right_shift(qry, shift)
        qhs = [q_ref[:, _pair(n)] for n in range(nh)]
        lse_t = lse_ref[:, _pair(0)].T
        doms, deltas, lse_hs = [], [], []
        for pp in range(ap):
            do = do_ref[:, _pair(pp)]
            d_o_t = (do * o_ref[:, _pair(pp)]).T
            for hh in range(2):
                doms.append(do.astype(BF16) * _head_mask(lane, hh))
                deltas.append(jnp.sum(d_o_t[hh * 64:(hh + 1) * 64], axis=0, keepdims=True))
                lse_hs.append(lse_t[2 * pp + hh:2 * pp + hh + 1])

        dq_ref[...] = jnp.zeros_like(dq_ref)

        def step(kb, allowed):
            start = pl.multiple_of(kb * t, t)
            vss = [v_ref[pl.ds(start, t), _pair(pp)] for pp in range(ap)]
            kss = [k_ref[pl.ds(start, t), _pair(n)] for n in range(nh)]
            scores = [_mla_scores(kss[n], qhs[n], allowed) for n in range(nh)]
            dps = [_dot(vss[pp], doms[n], NT) for n, (pp, _) in enumerate(hds)]
            ps = [jnp.exp(scores[n] - lse_hs[n]) for n in range(nh)]
            dss = [(ps[n] * (dps[n] - deltas[n]) * MLA_SCALE).astype(BF16) for n in range(nh)]
            for pp in range(ap):
                a, b = 2 * pp, 2 * pp + 1
                dv_ref[pl.ds(start, t), _pair(pp)] += _dot(ps[a].astype(BF16), doms[a]) + _dot(ps[b].astype(BF16), doms[b])
            for n in range(nh):
                dk_ref[pl.ds(start, t), _pair(n)] += _dot(dss[n], qhs[n])
                dq_ref[:, _pair(n)] += _dot(dss[n], kss[n], TN)

        def off_diagonal(kb, nothing):
            step(kb, None)
            return nothing

        lax.fori_loop(0, i, off_diagonal, 0)
        step(i, allowed_diag)

    ospec = pl.BlockSpec((t, width), lambda b, p, i: (b * nq + i, p))
    return _attention_call(
        body, ex, "mla_bwd", (nb, groups, nq), [qp, kp, vm, d_out, out, lse],
        [pl.BlockSpec((t, 2 * width), lambda b, p, i: (b * nq + i, p)),
         pl.BlockSpec((seq, 2 * width), lambda b, p, i: (b, p)),
         pl.BlockSpec((seq, width), lambda b, p, i: (b, p)),
         pl.BlockSpec((t, width), lambda b, p, i: (b * nq + i, groups + p)),
         ospec, ospec],
        [jax.ShapeDtypeStruct((rows, HEADS * LANES), F32), jax.ShapeDtypeStruct((rows, HEADS * LANES), F32),
         jax.ShapeDtypeStruct((rows, MLA_W), F32)],
        [pl.BlockSpec((t, 2 * width), lambda b, p, i: (b * nq + i, p)),
         pl.BlockSpec((seq, 2 * width), lambda b, p, i: (b, p)),
         pl.BlockSpec((seq, width), lambda b, p, i: (b, p))])


PACK_COLS = 1024
PACK_ALIGN = 16
GROUP_IN = (384, ((1024, 552, 1), (384, 192, 1), (256, 256, 1)))
GROUP_MLP = (1152, ((1024, 1024, 1), (1024, 1024, 0), (256, 1024, 0)))


def _pack_rows(r, c):
    return (r // 2) * c // PACK_COLS


def _slot_rows(r, c):
    return -(-_pack_rows(r, c) // PACK_ALIGN) * PACK_ALIGN


def _join_slots(parts, group):
    total, weights = group
    padded = [jnp.pad(p, ((0, 0), (0, _slot_rows(r, c) - p.shape[1]), (0, 0))) for p, (r, c, _) in zip(parts, weights)]
    used = sum(_slot_rows(r, c) for r, c, _ in weights)
    if total > used:
        padded.append(jnp.zeros((parts[0].shape[0], total - used, PACK_COLS), parts[0].dtype))
    return jnp.concatenate(padded, axis=1)


def _split_slots(packed, group):
    out, at = [], 0
    for r, c, _ in group[1]:
        out.append(packed[:, at:at + _pack_rows(r, c), :])
        at += _slot_rows(r, c)
    return out


def _pack_halves(shards, group):
    return _join_slots([s.reshape(2, _pack_rows(r, c), PACK_COLS) for s, (r, c, _) in zip(shards, group[1])], group)


def _unpack_halves(packed, group):
    return [p.reshape(r, c) for p, (r, c, _) in zip(_split_slots(packed, group), group[1])]


def _unpack_full(gathered, group):
    out = []
    for p, (r, c, axis) in zip(_split_slots(gathered, group), group[1]):
        shards = p.reshape(4, r, c)
        out.append(shards.reshape(4 * r, c) if axis == 0 else jnp.moveaxis(shards, 0, 1).reshape(r, 4 * c))
    return out


def _pack_full(grads, group):
    parts = []
    for gr, (r, c, axis) in zip(grads, group[1]):
        shards = gr.reshape(4, r, c) if axis == 0 else jnp.moveaxis(gr.reshape(r, 4, c), 1, 0)
        parts.append(shards.reshape(8, _pack_rows(r, c), PACK_COLS))
    return _join_slots(parts, group)


def _pad_w_in(w_in):
    z = jnp.zeros((D_MODEL, 1), w_in.dtype)
    return jnp.concatenate([w_in[:, :2176], jnp.tile(z, (1, 64)), w_in[:, 2176:], jnp.tile(z, (1, 32))], axis=1)


def _unpad_w_in(g):
    return jnp.concatenate([g[:, :2176], g[:, 2240:2272]], axis=1)


def _pad_heads(w, used):
    k = w.shape[0]
    w3 = w.reshape(k, HEADS, used)
    return jnp.pad(w3, ((0, 0), (0, 0), (0, LANES - used))).reshape(k, HEADS * LANES)


def _unpad_heads(g, used):
    k = g.shape[0]
    return g.reshape(k, HEADS, LANES)[:, :, :used].reshape(k, HEADS * used)


def _rope_tables(seq):
    inv_freq = 1.0 / (ROPE_BASE ** (jnp.arange(0, ROPE, 2, dtype=F32) / ROPE))
    ang = jnp.arange(seq, dtype=F32)[:, None] * inv_freq[None, :]
    cos, sin = jnp.cos(ang), jnp.sin(ang)
    one, zero = jnp.ones((seq, NOPE), F32), jnp.zeros((seq, NOPE), F32)
    z16, z32 = jnp.zeros((seq, 16), F32), jnp.zeros((seq, 32), F32)
    cos_t = jnp.concatenate([one, cos, cos, jnp.ones((seq, 32), F32)], axis=1)
    sin_a = jnp.concatenate([zero, -sin, z16, z32], axis=1)
    sin_b = jnp.concatenate([zero, z16, sin, z32], axis=1)
    return cos_t, sin_a, sin_b


SMALL = (("ln_in_g", 1024), ("ln_in_b", 1024), ("b_ada", 6144), ("q_norm_g", 384), ("kv_norm_g", 256),
         ("ln1_g", 1024), ("ln1_b", 1024), ("ln2_g", 1024), ("ln2_b", 1024))
SUBLANES = 8
SMALL_SLOTS = [-(-n // LANES // SUBLANES) * SUBLANES for _, n in SMALL]
SMALL_AT = [sum(SMALL_SLOTS[:p]) for p in range(len(SMALL))]
SMALL_ROWS = sum(SMALL_SLOTS)


def _pack_small(vals):
    parts = []
    for v, slot in zip(vals, SMALL_SLOTS):
        rows = v.reshape(-1, LANES)
        parts.append(jnp.pad(rows, ((0, slot - rows.shape[0]), (0, 0))))
    return jnp.concatenate(parts, axis=0)


def kernel(x, c, ln_in_g, ln_in_b, w_ada, b_ada, w_in, q_norm_g, kv_norm_g, w_uq, w_ukv, w_o, ln1_g, ln1_b, w_up, w_down, ln2_g, ln2_b, loss_target, m_ln_in_g, m_ln_in_b, m_w_ada, m_b_ada, m_w_in, m_q_norm_g, m_kv_norm_g, m_w_uq, m_w_ukv, m_w_o, m_ln1_g, m_ln1_b, m_w_up, m_w_down, m_ln2_g, m_ln2_b, v_ln_in_g, v_ln_in_b, v_w_ada, v_b_ada, v_w_in, v_q_norm_g, v_kv_norm_g, v_w_uq, v_w_ukv, v_w_o, v_ln1_g, v_ln1_b, v_w_up, v_w_down, v_ln2_g, v_ln2_b):
    nb, seq, _ = x.shape
    rows = nb * seq
    ix, iy, ic = lax.axis_index("x"), lax.axis_index("y"), lax.axis_index("c")
    chip = 2 * ix + iy
    dev = 2 * chip + ic

    def my_half(shards, group):
        packed = _pack_halves([s.astype(BF16) for s in shards], group)
        return lax.dynamic_index_in_dim(packed, ic, 0, keepdims=False)

    f_in, f_uq, f_ukv = _unpack_full(_gather8(my_half([w_in[0], w_uq[0], w_ukv[0]], GROUP_IN), "gather_w_in"),
                                     GROUP_IN)
    half_mlp = my_half([w_up[0], w_down[0], w_o[0]], GROUP_MLP)
    late_weights = _gather_exchange(half_mlp)
    w_in_p = _pad_w_in(f_in)
    uq3 = f_uq.reshape(Q_RANK, HEADS, NOPE + ROPE)
    w_uq_p = jnp.pad(uq3, ((0, 0), (0, 0), (0, LANES - NOPE - ROPE))).reshape(Q_RANK, HEADS * LANES)
    w_ukv_p = jnp.concatenate([_pad_heads(f_ukv[:, :HEADS * NOPE], NOPE), f_ukv[:, HEADS * NOPE:]], axis=1)

    n_all = 8 * nb
    c_all = _gather8(c.reshape(-1, LANES), "gather_c").reshape(n_all, D_MODEL)
    ada_cols = w_ada.shape[2]
    b_sh = lax.dynamic_slice_in_dim(b_ada, chip * ada_cols, ada_cols, axis=1)
    mod_sh = _ada_fwd(c_all, w_ada[0], b_sh)
    mod_g = _gather8(mod_sh, "gather_mod")[0::2]
    mod_all = jnp.moveaxis(mod_g, 0, 1).reshape(n_all, N_MOD * D_MODEL)
    mod_mine = lax.dynamic_slice_in_dim(mod_all, dev * nb, nb, axis=0).reshape(nb, N_MOD, D_MODEL)
    mod = jnp.pad(mod_mine, ((0, 0), (0, 8 - N_MOD), (0, 0)))

    cos_t, sin_a, sin_b = _rope_tables(seq)
    row2 = lambda v: v.reshape(1, -1)

    x2d = x.reshape(rows, D_MODEL)
    x0, h, qkv, lat, qp, kp, vm = _fwd_in(x2d, mod, row2(ln_in_g), row2(ln_in_b), w_in_p, q_norm_g, kv_norm_g,
                                          w_uq_p, w_ukv_p, cos_t, sin_a, sin_b, seq)
    sb_y, cars, g_mlp = _sb_fwd(qkv, seq, late_weights)
    g_mlp = _with_own(g_mlp, half_mlp)
    f_o = _split_slots(g_mlp, GROUP_MLP)[2].reshape(D_MODEL, D_MODEL)
    mla_y, lse = _mla_fwd(qp, kp, vm, seq)
    mix, y1, h2, u, ff, y2 = _fwd_out(sb_y, mla_y, x0, mod, f_o, ln1_g, ln1_b, g_mlp, seq)

    dy1, dmix, d_attn, dff, du, acc_out, dmod_a = _bwd_out(
        y2, loss_target.reshape(rows, D_MODEL), ff, u, y1, mix, mod, ln2_g, ln2_b, ln1_g, ln1_b, g_mlp, f_o, seq)
    c_idx = ic.reshape(1).astype(jnp.int32)
    blocks_mlp = _wgrad_packed(h2, du, "wgrad_up", lambda i, j: 2 * j + i, 0)
    blocks_mlp = _wgrad_packed(u, dff, "wgrad_down", lambda i, j: i, 1, pre="relu2", into=blocks_mlp)
    blocks_mlp = _wgrad_packed(sb_y, dmix, "wgrad_o_sb", lambda i, j: 0, 8, split=4, into=blocks_mlp)
    blocks_mlp = _wgrad_packed(mla_y, dmix, "wgrad_o_mla", lambda i, j: 1, 8, split=4, into=blocks_mlp)
    dq_sb, dk_sb, dv_sb, sibling_mlp = _sb_bwd(qkv, d_attn, cars, seq, _swap_cores_exchange(blocks_mlp))
    part_mlp, part_mlp_bf = _add_pairs(blocks_mlp, sibling_mlp, c_idx, "grad_add_cores_mlp")
    dqp, dkp, dvm, chips_mlp = _mla_bwd(qp, kp, vm, d_attn, mla_y, lse, seq, _scatter_chips_exchange(part_mlp_bf))
    grad_x, dproj, dqall, dkv, latn, acc0, acc_lat, dmod_c = _bwd_in(
        dqp, dkp, dvm, dq_sb, dk_sb, dv_sb, lat, x2d, x0, dy1, mod, row2(ln_in_g), row2(ln_in_b), w_in_p,
        q_norm_g, kv_norm_g, w_uq_p, w_ukv_p, cos_t, sin_a, sin_b, seq)

    g_in = _unpad_w_in(_wgrad(h, dproj, "wgrad_in", tn=768))
    g_uq = _unpad_heads(_wgrad(latn[:, :Q_RANK], dqall, "wgrad_uq"), NOPE + ROPE)
    g_ukv_p = _wgrad(latn[:, Q_RANK:], dkv, "wgrad_ukv", tn=512)
    g_ukv = jnp.concatenate([_unpad_heads(g_ukv_p[:, :HEADS * LANES], NOPE), g_ukv_p[:, HEADS * LANES:]], axis=1)
    blocks_in = _pack_full([g_in, g_uq, g_ukv], GROUP_IN)
    sibling_in = _run_exchange(_swap_cores_exchange(blocks_in), "grads_in_to_sibling")
    part_in, part_in_bf = _add_pairs(blocks_in, sibling_in, c_idx, "grad_add_cores_in")
    chips_in = _run_exchange(_scatter_chips_exchange(part_in_bf), "grads_in_to_chips")

    def own(part):
        return lax.dynamic_index_in_dim(part, chip, 0, keepdims=False)

    half = jnp.concatenate([_add_chips(own(part_in), chips_in, "grad_add_chips_in"),
                            _add_chips(own(part_mlp), chips_mlp, "grad_add_chips_mlp")], axis=0)
    other = _run_exchange(_swap_one_exchange(half), "grads_halves")
    both = jnp.where(ic == 0, jnp.stack([half, other]), jnp.stack([other, half]))
    gs_in, gs_uq, gs_ukv = _unpack_halves(both[:, :GROUP_IN[0]], GROUP_IN)
    gs_up, gs_down, gs_o = _unpack_halves(both[:, GROUP_IN[0]:], GROUP_MLP)

    dmod = (dmod_a + dmod_c)[:, :N_MOD, :]
    small_part = _pack_small([acc0[0], acc0[1], jnp.zeros((N_MOD * D_MODEL,), F32), acc_lat[0, :Q_RANK],
                              acc_lat[1, :KV_RANK], acc_out[3], acc_out[4], acc_out[0], acc_out[1]])
    n_sum = SMALL_ROWS + D_MODEL // LANES
    payload = jnp.concatenate([small_part, acc_out[2].reshape(-1, LANES), dmod.reshape(-1, LANES)], axis=0)
    gathered = _gather8(payload, "gather_small")
    small_sum = _sum_lead(gathered[:, :n_sum, :], "sum_small")
    loss = jnp.sum(small_sum[SMALL_ROWS:])
    dmod_all = gathered[:, n_sum:, :].reshape(n_all, N_MOD * D_MODEL)
    g_b_ada = _sum_lead(dmod_all.reshape(n_all, N_MOD * D_MODEL // LANES, LANES), "sum_b_ada")
    dmod_sh = lax.dynamic_slice_in_dim(dmod_all, chip * ada_cols, ada_cols, axis=1)
    g_w_ada = _ada_bwd(c_all, dmod_sh)

    big_w = {"w_ada": (w_ada[0], g_w_ada, m_w_ada[0], v_w_ada[0]), "w_in": (w_in[0], gs_in, m_w_in[0], v_w_in[0]),
             "w_uq": (w_uq[0], gs_uq, m_w_uq[0], v_w_uq[0]), "w_ukv": (w_ukv[0], gs_ukv, m_w_ukv[0], v_w_ukv[0]),
             "w_o": (w_o[0], gs_o, m_w_o[0], v_w_o[0]), "w_up": (w_up[0], gs_up, m_w_up[0], v_w_up[0]),
             "w_down": (w_down[0], gs_down, m_w_down[0], v_w_down[0])}
    res = {}
    for name, (w, g, m, v) in big_w.items():
        d, mn, vn = _adamw(w, g, m, v, "adamw_" + name)
        res[name] = (g[None], d[None], mn[None], vn[None])
    small_w = [ln_in_g, ln_in_b, b_ada, q_norm_g, kv_norm_g, ln1_g, ln1_b, ln2_g, ln2_b]
    small_m = [m_ln_in_g, m_ln_in_b, m_b_ada, m_q_norm_g, m_kv_norm_g, m_ln1_g, m_ln1_b, m_ln2_g, m_ln2_b]
    small_v = [v_ln_in_g, v_ln_in_b, v_b_ada, v_q_norm_g, v_kv_norm_g, v_ln1_g, v_ln1_b, v_ln2_g, v_ln2_b]
    for (name, _), quad in zip(SMALL, _adamw_small(small_sum, g_b_ada, small_w, small_m, small_v)):
        res[name] = quad

    order = ["ln_in_g", "ln_in_b", "w_ada", "b_ada", "w_in", "q_norm_g", "kv_norm_g", "w_uq", "w_ukv", "w_o",
             "ln1_g", "ln1_b", "w_up", "w_down", "ln2_g", "ln2_b"]
    outs = [loss, grad_x.reshape(nb, seq, D_MODEL)]
    for k in range(4):
        outs += [res[name][k] for name in order]
    return tuple(outs)
```

```python
import functools
import math

import jax
import jax.numpy as jnp
from jax import lax
from jax.experimental import pallas as pl
from jax.experimental.pallas import tpu as pltpu

F32 = jnp.float32
BF16 = jnp.bfloat16
MESH_IDS = pl.DeviceIdType.MESH

D_MODEL = 1024
HEADS = 8
HEAD_PAIRS = HEADS // 2
SB_W = 512
MLA_W = 512
NOPE = 64
ROPE = 32
Q_RANK = 384
KV_RANK = 256
D_IN = 2208
D_IN_PAD = 2304
D_FF = 4096
N_MOD = 6
LN_EPS = 1e-5
RMS_EPS = 1e-6
ALPHA = 2.0 ** 0.25
ROPE_BASE = 10000.0
SB_SCALE = 64 ** -0.5
MLA_SCALE = 96 ** -0.5
ADAM_LR = 0.001
ADAM_B1 = 0.9
ADAM_B2 = 0.999
ADAM_EPS = 1e-08
ADAM_WD = 0.01
ADAM_STEP = 10

LANES = 128
ROW_TILE = 256
ATTN_TILE = 256
CAR_SLOTS = 8
ATTN_PAIRS = 4
VMEM_LIMIT = 56 << 20

NT = (((1,), (1,)), ((), ()))
TN = (((0,), (0,)), ((), ()))


def _params(sem=None):
    return pltpu.CompilerParams(vmem_limit_bytes=VMEM_LIMIT, dimension_semantics=sem)


def _const_spec(shape):
    zeros = (0,) * len(shape)
    return pl.BlockSpec(shape, lambda *_: zeros, pipeline_mode=pl.Buffered(1))


def _dot(a, b, dims=None):
    if dims is None:
        return jnp.dot(a, b, preferred_element_type=F32)
    return lax.dot_general(a, b, dims, preferred_element_type=F32)


def _mean(v):
    return jnp.mean(v, axis=-1, keepdims=True)


def _rowsum(v):
    return jnp.sum(v, axis=0, keepdims=True)


def _ln_fwd(y, g, b):
    mu = _mean(y)
    yc = y - mu
    rstd = lax.rsqrt(_mean(yc * yc) + LN_EPS)
    xhat = yc * rstd
    return xhat * g + b, xhat, rstd


def _ln_bwd(dx, xhat, rstd, g):
    dxh = dx * g
    return rstd * (dxh - _mean(dxh) - xhat * _mean(dxh * xhat))


def _rope(v, cos, sin_a, sin_b):
    return v * cos + pltpu.roll(v, 112, 1) * sin_a + pltpu.roll(v, 16, 1) * sin_b


def _rope_t(dv, cos, sin_a, sin_b):
    return dv * cos + pltpu.roll(dv * sin_a, 16, 1) + pltpu.roll(dv * sin_b, 112, 1)


def _my_place():
    return lax.axis_index("x"), lax.axis_index("y"), lax.axis_index("c")


class _Exchange:
    def __init__(self, operand, out_shape, n_copies, phases):
        self.operand = operand
        self.out_shape = out_shape
        self.phases = phases
        self.scratch = [pltpu.SemaphoreType.DMA((n_copies,)), pltpu.SemaphoreType.DMA((n_copies,))]


def _run_exchange(ex, name):
    def body(in_ref, out_ref, send_sems, recv_sems):
        for phase in ex.phases(in_ref, out_ref, send_sems, recv_sems):
            phase()

    return pl.pallas_call(
        body, name=name, out_shape=ex.out_shape,
        in_specs=[pl.BlockSpec(memory_space=pl.ANY)], out_specs=pl.BlockSpec(memory_space=pl.ANY),
        scratch_shapes=ex.scratch,
    )(ex.operand)


def _nothing():
    pass


def _gather_exchange(v):
    m, n = v.shape

    def phases(v_ref, out_ref, send_sems, recv_sems):
        x, y, c = _my_place()
        me, sibling = (x, y, c), (x, y, 1 - c)
        chips = [(1 - x, y), (x, 1 - y), (1 - x, 1 - y)]

        def rows(px, py, pc):
            return out_ref.at[4 * px + 2 * py + pc]

        def copy(k, block, to, src=None):
            return pltpu.make_async_remote_copy(
                src_ref=rows(*block) if src is None else src, dst_ref=rows(*block),
                send_sem=send_sems.at[k], recv_sem=recv_sems.at[k], device_id=to, device_id_type=MESH_IDS)

        first = [copy(0, me, sibling, src=v_ref)]
        first += [copy(1 + j, me, (*chip, c), src=v_ref) for j, chip in enumerate(chips)]
        passed = [copy(4 + j, (*chip, c), sibling) for j, chip in enumerate(chips)]

        def start():
            for cp in first:
                cp.start()

        def middle():
            for j, chip in enumerate(chips):
                copy(1 + j, (*chip, c), me).wait_recv()
                passed[j].start()

        def finish():
            copy(0, sibling, me).wait_recv()
            for j, chip in enumerate(chips):
                copy(4 + j, (*chip, 1 - c), me).wait_recv()
            for cp in first + passed:
                cp.wait_send()

        return start, middle, finish

    return _Exchange(v, jax.ShapeDtypeStruct((8, m, n), v.dtype), 7, phases)


def _with_own(gathered, v):
    dev = 4 * lax.axis_index("x") + 2 * lax.axis_index("y") + lax.axis_index("c")
    return lax.dynamic_update_index_in_dim(gathered, v, dev, 0)


def _direct_exchange(operand, out_shape, n_copies, make_copies):
    def phases(in_ref, out_ref, send_sems, recv_sems):
        copies = make_copies(in_ref, out_ref, send_sems, recv_sems)

        def start():
            for cp in copies:
                cp.start()

        def finish():
            for cp in copies:
                cp.wait()

        return start, _nothing, finish

    return _Exchange(operand, out_shape, n_copies, phases)


def _swap_cores_exchange(blocks):
    _, m, n = blocks.shape

    def make_copies(g_ref, out_ref, send_sems, recv_sems):
        x, y, c = _my_place()
        return [pltpu.make_async_remote_copy(
            src_ref=g_ref.at[2 * j + (1 - c)], dst_ref=out_ref.at[j],
            send_sem=send_sems.at[j], recv_sem=recv_sems.at[j],
            device_id=(x, y, 1 - c), device_id_type=MESH_IDS) for j in range(4)]

    return _direct_exchange(blocks, jax.ShapeDtypeStruct((4, m, n), blocks.dtype), 4, make_copies)


def _scatter_chips_exchange(parts):
    _, m, n = parts.shape
    flips = [(1, 0), (0, 1), (1, 1)]

    def make_copies(p_ref, out_ref, send_sems, recv_sems):
        x, y, c = _my_place()
        copies = []
        for k, (fx, fy) in enumerate(flips):
            tx = 1 - x if fx else x
            ty = 1 - y if fy else y
            copies.append(pltpu.make_async_remote_copy(
                src_ref=p_ref.at[2 * tx + ty], dst_ref=out_ref.at[k],
                send_sem=send_sems.at[k], recv_sem=recv_sems.at[k],
                device_id=(tx, ty, c), device_id_type=MESH_IDS))
        return copies

    return _direct_exchange(parts, jax.ShapeDtypeStruct((3, m, n), parts.dtype), 3, make_copies)


def _swap_one_exchange(v):
    def make_copies(v_ref, out_ref, send_sems, recv_sems):
        x, y, c = _my_place()
        return [pltpu.make_async_remote_copy(src_ref=v_ref, dst_ref=out_ref, send_sem=send_sems.at[0],
                                             recv_sem=recv_sems.at[0], device_id=(x, y, 1 - c),
                                             device_id_type=MESH_IDS)]

    return _direct_exchange(v, jax.ShapeDtypeStruct(v.shape, v.dtype), 1, make_copies)


def _gather8(v, name):
    return _with_own(_run_exchange(_gather_exchange(v), name), v)


def _carried(ex, refs, n_in, n_out, n_scratch):
    ins, ex_in = refs[:n_in], refs[n_in]
    outs, ex_out = refs[n_in + 1:n_in + 1 + n_out], refs[n_in + 1 + n_out]
    at = n_in + 2 + n_out
    return ins, outs + refs[at:at + n_scratch], ex.phases(ex_in, ex_out, *refs[at + n_scratch:])


def _ada_fwd(c_all, w_ada_sh, b_ada_sh):
    nb, cols = c_all.shape[0], w_ada_sh.shape[1]
    tn = 512

    def body(c_ref, w_ref, b_ref, o_ref):
        cv = c_ref[...]
        act = (cv * jax.nn.sigmoid(cv)).astype(BF16)
        o_ref[...] = _dot(act, w_ref[...].astype(BF16)) + b_ref[...]

    return pl.pallas_call(
        body, name="ada_fwd", grid=(cols // tn,),
        out_shape=jax.ShapeDtypeStruct((nb, cols), F32),
        in_specs=[pl.BlockSpec((nb, D_MODEL), lambda j: (0, 0)),
                  pl.BlockSpec((D_MODEL, tn), lambda j: (0, j)),
                  pl.BlockSpec((1, tn), lambda j: (0, j))],
        out_specs=pl.BlockSpec((nb, tn), lambda j: (0, j)),
        compiler_params=_params(("arbitrary",)),
    )(c_all, w_ada_sh, b_ada_sh)


def _ada_bwd(c_all, dmod_sh):
    nb, cols = dmod_sh.shape
    tn = 512

    def body(c_ref, d_ref, o_ref):
        cv = c_ref[...]
        act = (cv * jax.nn.sigmoid(cv)).astype(BF16)
        o_ref[...] = _dot(act, d_ref[...].astype(BF16), TN)

    return pl.pallas_call(
        body, name="ada_bwd", grid=(cols // tn,),
        out_shape=jax.ShapeDtypeStruct((D_MODEL, cols), F32),
        in_specs=[pl.BlockSpec((nb, D_MODEL), lambda j: (0, 0)),
                  pl.BlockSpec((nb, tn), lambda j: (0, j))],
        out_specs=pl.BlockSpec((D_MODEL, tn), lambda j: (0, j)),
        compiler_params=_params(("arbitrary",)),
    )(c_all, dmod_sh)


def _sum_lead(v, name):
    k, m, n = v.shape

    def body(v_ref, o_ref):
        acc = v_ref[0]
        for i in range(1, k):
            acc = acc + v_ref[i]
        o_ref[...] = acc

    return pl.pallas_call(
        body, name=name, out_shape=jax.ShapeDtypeStruct((m, n), F32),
        in_specs=[pl.BlockSpec((k, m, n), lambda: (0, 0, 0))],
        out_specs=pl.BlockSpec((m, n), lambda: (0, 0)),
        compiler_params=_params(),
    )(v)


def _adamw_math(w, g, m, v):
    mn = ADAM_B1 * m + (1.0 - ADAM_B1) * g
    vn = ADAM_B2 * v + (1.0 - ADAM_B2) * (g * g)
    m_hat = mn / (1.0 - ADAM_B1 ** ADAM_STEP)
    v_hat = vn / (1.0 - ADAM_B2 ** ADAM_STEP)
    return -ADAM_LR * (m_hat / (jnp.sqrt(v_hat) + ADAM_EPS) + ADAM_WD * w), mn, vn


def _adamw_small(g_sum, g_b_ada, ws, ms, vs):
    n = len(SMALL)

    def body(gs_ref, gb_ref, *refs):
        outs = refs[3 * n:]
        for p in range(n):
            rows_p = SMALL[p][1] // LANES
            g = gb_ref[...] if SMALL[p][0] == "b_ada" else gs_ref[SMALL_AT[p]:SMALL_AT[p] + rows_p, :]
            d, mn, vn = _adamw_math(refs[p][...], g, refs[n + p][...], refs[2 * n + p][...])
            outs[4 * p][...] = g
            outs[4 * p + 1][...] = d
            outs[4 * p + 2][...] = mn
            outs[4 * p + 3][...] = vn

    shapes = [jax.ShapeDtypeStruct((size // LANES, LANES), F32) for _, size in SMALL for _ in range(4)]
    flat = lambda arrs: [a.reshape(-1, LANES) for a in arrs]
    res = pl.pallas_call(body, name="adamw_small", out_shape=tuple(shapes), compiler_params=_params())(
        g_sum, g_b_ada, *flat(ws), *flat(ms), *flat(vs))
    return [tuple(r.reshape(w.shape) for r in res[4 * p:4 * p + 4]) for p, w in enumerate(ws)]


def _adamw_halves(w, g_mine, g_other, m, v, c_idx, name):
    r, cols = w.shape
    half = r // 2
    tr = half
    while tr * cols * 4 > (2 << 20) and tr % 16 == 0:
        tr //= 2

    def body(c_ref, w_ref, mine_ref, other_ref, m_ref, v_ref, g_ref, d_ref, mo_ref, vo_ref):
        g = jnp.where(pl.program_id(0) == c_ref[0], mine_ref[...], other_ref[...])
        g_ref[0] = g
        d_ref[0], mo_ref[0], vo_ref[0] = _adamw_math(w_ref[0], g, m_ref[0], v_ref[0])

    full = pl.BlockSpec((1, tr, cols), lambda h, i, c: (h, i, 0))
    part = pl.BlockSpec((tr, cols), lambda h, i, c: (i, 0))
    shape = jax.ShapeDtypeStruct((2, half, cols), F32)
    grid_spec = pltpu.PrefetchScalarGridSpec(
        num_scalar_prefetch=1, grid=(2, half // tr),
        in_specs=[full, part, part, full, full], out_specs=(full, full, full, full))
    split = lambda a: a.reshape(2, half, cols)
    res = pl.pallas_call(
        body, name=name, grid_spec=grid_spec, out_shape=(shape, shape, shape, shape),
        compiler_params=_params(("arbitrary", "arbitrary")),
    )(c_idx, split(w), g_mine, g_other, split(m), split(v))
    return tuple(a.reshape(r, cols) for a in res)


def _adamw(w, g, m, v, name):
    rows, cols = w.shape
    tr = rows
    while tr * cols * 4 > (2 << 20) and tr % 16 == 0:
        tr //= 2

    def body(w_ref, g_ref, m_ref, v_ref, d_ref, mo_ref, vo_ref):
        d_ref[...], mo_ref[...], vo_ref[...] = _adamw_math(w_ref[...], g_ref[...], m_ref[...], v_ref[...])

    spec = pl.BlockSpec((tr, cols), lambda i: (i, 0))
    shape = jax.ShapeDtypeStruct((rows, cols), F32)
    return pl.pallas_call(
        body, name=name, grid=(rows // tr,), out_shape=(shape, shape, shape),
        in_specs=[spec, spec, spec, spec], out_specs=(spec, spec, spec),
        compiler_params=_params(("arbitrary",)),
    )(w, g, m, v)


def _add_rows(m, n):
    fits = [d for d in range(16, m + 1, 16) if m % d == 0 and d * n * 4 <= (5 << 19)]
    assert fits, (m, n)
    return max(fits)


def _add_pairs(blocks, recv, c_idx, name):
    _, m, n = blocks.shape
    tr = _add_rows(m, n)

    def body(c_ref, a_ref, b_ref, o_ref, ob_ref):
        s = a_ref[...] + b_ref[...]
        o_ref[...] = s
        ob_ref[...] = s.astype(BF16)

    grid_spec = pltpu.PrefetchScalarGridSpec(
        num_scalar_prefetch=1, grid=(4, m // tr),
        in_specs=[pl.BlockSpec((1, tr, n), lambda j, i, c: (2 * j + c[0], i, 0)),
                  pl.BlockSpec((1, tr, n), lambda j, i, c: (j, i, 0))],
        out_specs=(pl.BlockSpec((1, tr, n), lambda j, i, c: (j, i, 0)),
                   pl.BlockSpec((1, tr, n), lambda j, i, c: (j, i, 0))))
    return pl.pallas_call(
        body, name=name, grid_spec=grid_spec,
        out_shape=(jax.ShapeDtypeStruct((4, m, n), F32), jax.ShapeDtypeStruct((4, m, n), BF16)),
        compiler_params=_params(("arbitrary", "arbitrary")),
    )(c_idx, blocks, recv)


def _add_chips(own, recv, name):
    m, n = own.shape
    tr = _add_rows(m, n)

    def body(a_ref, r_ref, o_ref):
        acc = a_ref[...]
        for k in range(3):
            acc = acc + r_ref[k].astype(F32)
        o_ref[...] = acc

    return pl.pallas_call(
        body, name=name, grid=(m // tr,),
        out_shape=jax.ShapeDtypeStruct((m, n), F32),
        in_specs=[pl.BlockSpec((tr, n), lambda i: (i, 0)), pl.BlockSpec((3, tr, n), lambda i: (0, i, 0))],
        out_specs=pl.BlockSpec((tr, n), lambda i: (i, 0)),
        compiler_params=_params(("arbitrary",)),
    )(own, recv)


def _row_spec(cols):
    return pl.BlockSpec((ROW_TILE, cols), lambda i: (i, 0))


def _mod_spec(tiles_per_seq):
    return pl.BlockSpec((1, 8, D_MODEL), lambda i: (i // tiles_per_seq, 0, 0))


def _table_spec(tiles_per_seq):
    return pl.BlockSpec((ROW_TILE, LANES), lambda i: (i % tiles_per_seq, 0))


def _fwd_in(x, mod, ln_g, ln_b, w_in, q_g, kv_g, w_uq, w_ukv, cos_t, sin_a, sin_b, seq):
    rows = x.shape[0]
    tm = ROW_TILE
    tps = seq // tm

    def body(x_ref, mod_ref, g_ref, b_ref, win_ref, qg_ref, kvg_ref, wuq_ref, wukv_ref, cos_ref, sa_ref, sb_ref,
             x0_ref, h_ref, qkv_ref, lat_ref, qp_ref, kp_ref, vm_ref):
        x0, _, _ = _ln_fwd(x_ref[...], g_ref[...], b_ref[...])
        x0_ref[...] = x0
        h = (x0 * (1.0 + mod_ref[0, 1:2, :]) + mod_ref[0, 0:1, :]).astype(BF16)
        h_ref[...] = h
        proj = _dot(h, win_ref[...])
        qkv_ref[:, :SB_W] = (proj[:, :SB_W] * SB_SCALE).astype(BF16)
        qkv_ref[:, SB_W:] = proj[:, SB_W:3 * SB_W].astype(BF16)
        lat_ref[...] = proj[:, 3 * SB_W:3 * SB_W + Q_RANK + KV_RANK]
        cq = proj[:, 3 * SB_W:3 * SB_W + Q_RANK]
        ckv = proj[:, 3 * SB_W + Q_RANK:3 * SB_W + Q_RANK + KV_RANK]
        kr = proj[:, D_IN_PAD - LANES:]
        cos, sa, sb = cos_ref[...], sa_ref[...], sb_ref[...]
        cqn = (cq * lax.rsqrt(_mean(cq * cq) + RMS_EPS) * qg_ref[...]).astype(BF16)
        q_all = _dot(cqn, wuq_ref[...])
        for hd in range(HEADS):
            sl = slice(hd * LANES, (hd + 1) * LANES)
            qp_ref[:, sl] = _rope(q_all[:, sl], cos, sa, sb).astype(BF16)
        ckvn = (ckv * lax.rsqrt(_mean(ckv * ckv) + RMS_EPS) * kvg_ref[...]).astype(BF16)
        kv = _dot(ckvn, wukv_ref[...])
        kr_rot = _rope(kr, cos, sa, sb)
        for hd in range(HEADS):
            sl = slice(hd * LANES, (hd + 1) * LANES)
            kp_ref[:, sl] = (kv[:, sl] + kr_rot).astype(BF16)
        vm_ref[...] = kv[:, HEADS * LANES:].astype(BF16)

    outs = [(D_MODEL, F32), (D_MODEL, BF16), (3 * SB_W, BF16), (Q_RANK + KV_RANK, F32),
            (HEADS * LANES, BF16), (HEADS * LANES, BF16), (MLA_W, BF16)]
    return pl.pallas_call(
        body, name="fwd_in", grid=(rows // tm,),
        out_shape=tuple(jax.ShapeDtypeStruct((rows, n), dt) for n, dt in outs),
        in_specs=[_row_spec(D_MODEL), _mod_spec(tps), _const_spec((1, D_MODEL)), _const_spec((1, D_MODEL)),
                  _const_spec(w_in.shape), _const_spec((1, Q_RANK)), _const_spec((1, KV_RANK)),
                  _const_spec(w_uq.shape), _const_spec(w_ukv.shape),
                  _table_spec(tps), _table_spec(tps), _table_spec(tps)],
        out_specs=tuple(_row_spec(n) for n, _ in outs),
        compiler_params=_params(("arbitrary",)),
    )(x, mod, ln_g, ln_b, w_in, q_g, kv_g, w_uq, w_ukv, cos_t, sin_a, sin_b)


HALF = 512
SHARD = 1024


def _mlp_weight_specs():
    return [pl.BlockSpec((8, HALF, SHARD), lambda i: (0, 0, 0), pipeline_mode=pl.Buffered(1)),
            pl.BlockSpec((8, HALF, SHARD), lambda i: (0, 1, 0), pipeline_mode=pl.Buffered(1))]


def _fwd_out(sb_y, mla_y, x0, mod, w_o, ln_g, ln_b, g_mlp, seq):
    rows = x0.shape[0]
    tm = ROW_TILE
    tps = seq // tm

    def body(sb_ref, ml_ref, x0_ref, mod_ref, wo_ref, g_ref, b_ref, wu_ref, wd_ref,
             mix_ref, y1_ref, h2_ref, u_ref, ff_ref, y2_ref):
        mix = _dot(sb_ref[...], wo_ref[:SB_W, :]) + _dot(ml_ref[...].astype(BF16), wo_ref[SB_W:, :])
        mix_ref[...] = mix
        y1 = ALPHA * x0_ref[...] + (1.0 + mod_ref[0, 2:3, :]) * mix
        y1_ref[...] = y1
        x1, _, _ = _ln_fwd(y1, g_ref[...], b_ref[...])
        h2 = (x1 * (1.0 + mod_ref[0, 4:5, :]) + mod_ref[0, 3:4, :]).astype(BF16)
        h2_ref[...] = h2
        h_lo, h_hi = h2[:, :HALF], h2[:, HALF:]
        ff = jnp.zeros((tm, D_MODEL), F32)
        for chip in range(4):
            u = _dot(h_lo, wu_ref[2 * chip]) + _dot(h_hi, wu_ref[2 * chip + 1])
            u_ref[:, chip * SHARD:(chip + 1) * SHARD] = u.astype(BF16)
            act = jnp.square(jnp.maximum(u, 0.0)).astype(BF16)
            ff = ff + _dot(act[:, :HALF], wd_ref[2 * chip]) + _dot(act[:, HALF:], wd_ref[2 * chip + 1])
        ff_ref[...] = ff
        y2_ref[...] = ALPHA * x1 + (1.0 + mod_ref[0, 5:6, :]) * ff

    outs = [(D_MODEL, F32), (D_MODEL, F32), (D_MODEL, BF16), (D_FF, BF16), (D_MODEL, F32), (D_MODEL, F32)]
    return pl.pallas_call(
        body, name="fwd_out", grid=(rows // tm,),
        out_shape=tuple(jax.ShapeDtypeStruct((rows, n), dt) for n, dt in outs),
        in_specs=[_row_spec(SB_W), _row_spec(MLA_W), _row_spec(D_MODEL), _mod_spec(tps), _const_spec(w_o.shape),
                  _const_spec((1, D_MODEL)), _const_spec((1, D_MODEL))] + _mlp_weight_specs(),
        out_specs=tuple(_row_spec(n) for n, _ in outs),
        compiler_params=_params(("arbitrary",)),
    )(sb_y, mla_y, x0, mod, w_o, ln_g, ln_b, g_mlp, g_mlp)


def _acc_spec(rows=8, cols=D_MODEL):
    return pl.BlockSpec((rows, cols), lambda i: (0, 0))


def _bwd_out(y2, tgt, ff, u, y1, mix, mod, ln2_g, ln2_b, ln1_g, ln1_b, g_mlp, w_o, seq):
    rows = y2.shape[0]
    nb = rows // seq
    tm = ROW_TILE
    tps = seq // tm

    def body(y2_ref, t_ref, ff_ref, u_ref, y1_ref, mix_ref, mod_ref, g2_ref, b2_ref, g_ref, b_ref, wu_ref, wd_ref,
             wo_ref, dy1_ref, dmix_ref, do_ref, dff_ref, du_ref, acc_ref, dmod_ref):
        i = pl.program_id(0)

        @pl.when(i == 0)
        def _():
            acc_ref[...] = jnp.zeros_like(acc_ref)

        @pl.when(i % tps == 0)
        def _():
            dmod_ref[...] = jnp.zeros_like(dmod_ref)

        g2 = g2_ref[...]
        x2, xhat2, rstd2 = _ln_fwd(y2_ref[...], g2, b2_ref[...])
        err = x2 - t_ref[...]
        dx2 = err * (1.0 / D_MODEL)
        acc_ref[0:1, :] += _rowsum(dx2 * xhat2)
        acc_ref[1:2, :] += _rowsum(dx2)
        acc_ref[2:3, :] += _rowsum(err * err) * (0.5 / D_MODEL)
        dy2 = _ln_bwd(dx2, xhat2, rstd2, g2)
        dmod_ref[0, 5:6, :] += _rowsum(dy2 * ff_ref[...])
        dff = ((1.0 + mod_ref[0, 5:6, :]) * dy2).astype(BF16)
        dff_ref[...] = dff
        for blk in range(8):
            cols = slice(blk * HALF, (blk + 1) * HALF)
            da = _dot(dff, wd_ref[blk], NT)
            du_ref[:, cols] = (da * (2.0 * jnp.maximum(u_ref[:, cols].astype(F32), 0.0))).astype(BF16)

        g = g_ref[...]
        x1, xhat, rstd = _ln_fwd(y1_ref[...], g, b_ref[...])
        halves = []
        for half in range(2):
            acc = jnp.zeros((tm, HALF), F32)
            for chip in range(4):
                acc = acc + _dot(du_ref[:, chip * SHARD:(chip + 1) * SHARD], wu_ref[2 * chip + half], NT)
            halves.append(acc)
        dh2 = jnp.concatenate(halves, axis=1)
        dmod_ref[0, 3:4, :] += _rowsum(dh2)
        dmod_ref[0, 4:5, :] += _rowsum(dh2 * x1)
        dx1 = ALPHA * dy2 + dh2 * (1.0 + mod_ref[0, 4:5, :])
        acc_ref[3:4, :] += _rowsum(dx1 * xhat)
        acc_ref[4:5, :] += _rowsum(dx1)
        dy1 = _ln_bwd(dx1, xhat, rstd, g)
        dy1_ref[...] = dy1
        dmod_ref[0, 2:3, :] += _rowsum(dy1 * mix_ref[...])
        dmix = ((1.0 + mod_ref[0, 2:3, :]) * dy1).astype(BF16)
        dmix_ref[...] = dmix
        do_ref[...] = _dot(dmix, wo_ref[...], NT)

    outs = [(D_MODEL, F32), (D_MODEL, BF16), (D_MODEL, F32), (D_MODEL, BF16), (D_FF, BF16)]
    return pl.pallas_call(
        body, name="bwd_out", grid=(rows // tm,),
        out_shape=tuple(jax.ShapeDtypeStruct((rows, n), dt) for n, dt in outs)
        + (jax.ShapeDtypeStruct((8, D_MODEL), F32), jax.ShapeDtypeStruct((nb, 8, D_MODEL), F32)),
        in_specs=[_row_spec(D_MODEL), _row_spec(D_MODEL), _row_spec(D_MODEL), _row_spec(D_FF), _row_spec(D_MODEL),
                  _row_spec(D_MODEL), _mod_spec(tps), _const_spec((1, D_MODEL)), _const_spec((1, D_MODEL)),
                  _const_spec((1, D_MODEL)), _const_spec((1, D_MODEL))] + _mlp_weight_specs()
        + [_const_spec(w_o.shape)],
        out_specs=tuple(_row_spec(n) for n, _ in outs) + (_acc_spec(), _mod_spec(tps)),
        compiler_params=_params(("arbitrary",)),
    )(y2, tgt, ff, u, y1, mix, mod, ln2_g, ln2_b, ln1_g, ln1_b, g_mlp, g_mlp, w_o)


def _bwd_in(dqp, dkp, dvm, dq_sb, dk_sb, dv_sb, lat, x, x0, dy1, mod, ln_g, ln_b, w_in, q_g, kv_g, w_uq, w_ukv,
            cos_t, sin_a, sin_b, seq):
    rows = x.shape[0]
    nb = rows // seq
    tm = ROW_TILE
    tps = seq // tm
    n_lat = Q_RANK + KV_RANK

    def body(dqp_ref, dkp_ref, dvm_ref, dqs_ref, dks_ref, dvs_ref, lat_ref, x_ref, x0_ref, dy1_ref, mod_ref,
             g_ref, b_ref, win_ref, qg_ref, kvg_ref, wuq_ref, wukv_ref, cos_ref, sa_ref, sb_ref,
             dx_ref, dproj_ref, dqall_ref, dkv_ref, latn_ref, acc_ref, accl_ref, dmod_ref):
        i = pl.program_id(0)

        @pl.when(i == 0)
        def _():
            acc_ref[...] = jnp.zeros_like(acc_ref)
            accl_ref[...] = jnp.zeros_like(accl_ref)

        @pl.when(i % tps == 0)
        def _():
            dmod_ref[...] = jnp.zeros_like(dmod_ref)

        cos, sa, sb = cos_ref[...], sa_ref[...], sb_ref[...]
        lane = lax.broadcasted_iota(jnp.int32, (tm, LANES), 1)
        for hd in range(HEADS):
            sl = slice(hd * LANES, (hd + 1) * LANES)
            dqall_ref[:, sl] = _rope_t(dqp_ref[:, sl], cos, sa, sb).astype(BF16)
        dcqn = _dot(dqall_ref[...], wuq_ref[...], NT)
        cq = lat_ref[:, :Q_RANK]
        qg = qg_ref[...]
        rq = lax.rsqrt(_mean(cq * cq) + RMS_EPS)
        cqn = cq * rq
        latn_ref[:, :Q_RANK] = (cqn * qg).astype(BF16)
        accl_ref[0:1, :Q_RANK] += _rowsum(dcqn * cqn)
        dqg = dcqn * qg
        dcq = rq * (dqg - cqn * _mean(dqg * cqn))
        dkr = jnp.zeros((tm, LANES), F32)
        for hd in range(HEADS):
            sl = slice(hd * LANES, (hd + 1) * LANES)
            dk = dkp_ref[:, sl]
            dkr = dkr + dk
            dkv_ref[:, sl] = jnp.where(lane < NOPE, dk, 0.0).astype(BF16)
        dkv_ref[:, HEADS * LANES:] = dvm_ref[...].astype(BF16)
        dckvn = _dot(dkv_ref[...], wukv_ref[...], NT)
        ckv = lat_ref[:, Q_RANK:]
        kvg = kvg_ref[...]
        rkv = lax.rsqrt(_mean(ckv * ckv) + RMS_EPS)
        ckvn = ckv * rkv
        latn_ref[:, Q_RANK:] = (ckvn * kvg).astype(BF16)
        accl_ref[1:2, :KV_RANK] += _rowsum(dckvn * ckvn)
        dkg = dckvn * kvg
        dckv = rkv * (dkg - ckvn * _mean(dkg * ckvn))
        dkr = _rope_t(jnp.where(lane >= NOPE, dkr, 0.0), cos, sa, sb)
        dproj_ref[:, :SB_W] = dqs_ref[...]
        dproj_ref[:, SB_W:2 * SB_W] = dks_ref[...].astype(BF16)
        dproj_ref[:, 2 * SB_W:3 * SB_W] = dvs_ref[...].astype(BF16)
        dproj_ref[:, 3 * SB_W:3 * SB_W + Q_RANK] = dcq.astype(BF16)
        dproj_ref[:, 3 * SB_W + Q_RANK:3 * SB_W + n_lat] = dckv.astype(BF16)
        dproj_ref[:, D_IN_PAD - LANES:] = dkr.astype(BF16)
        dh = _dot(dproj_ref[...], win_ref[...], NT)
        x0 = x0_ref[...]
        dmod_ref[0, 0:1, :] += _rowsum(dh)
        dmod_ref[0, 1:2, :] += _rowsum(dh * x0)
        dx0 = ALPHA * dy1_ref[...] + dh * (1.0 + mod_ref[0, 1:2, :])
        g = g_ref[...]
        _, xhat, rstd = _ln_fwd(x_ref[...], g, b_ref[...])
        acc_ref[0:1, :] += _rowsum(dx0 * xhat)
        acc_ref[1:2, :] += _rowsum(dx0)
        dx_ref[...] = _ln_bwd(dx0, xhat, rstd, g)

    outs = [(D_MODEL, F32), (D_IN_PAD, BF16), (HEADS * LANES, BF16), (HEADS * LANES + MLA_W, BF16), (n_lat, BF16)]
    return pl.pallas_call(
        body, name="bwd_in", grid=(rows // tm,),
        out_shape=tuple(jax.ShapeDtypeStruct((rows, n), dt) for n, dt in outs)
        + (jax.ShapeDtypeStruct((8, D_MODEL), F32), jax.ShapeDtypeStruct((8, Q_RANK), F32),
           jax.ShapeDtypeStruct((nb, 8, D_MODEL), F32)),
        in_specs=[_row_spec(HEADS * LANES), _row_spec(HEADS * LANES), _row_spec(MLA_W),
                  _row_spec(SB_W), _row_spec(SB_W), _row_spec(SB_W), _row_spec(n_lat),
                  _row_spec(D_MODEL), _row_spec(D_MODEL), _row_spec(D_MODEL), _mod_spec(tps),
                  _const_spec((1, D_MODEL)), _const_spec((1, D_MODEL)), _const_spec(w_in.shape),
                  _const_spec((1, Q_RANK)), _const_spec((1, KV_RANK)), _const_spec(w_uq.shape),
                  _const_spec(w_ukv.shape), _table_spec(tps), _table_spec(tps), _table_spec(tps)],
        out_specs=tuple(_row_spec(n) for n, _ in outs) + (_acc_spec(), _acc_spec(8, Q_RANK), _mod_spec(tps)),
        compiler_params=_params(("arbitrary",)),
    )(dqp, dkp, dvm, dq_sb, dk_sb, dv_sb, lat, x, x0, dy1, mod, ln_g, ln_b, w_in, q_g, kv_g, w_uq, w_ukv,
      cos_t, sin_a, sin_b)


def _wgrad(a, b, name, pre=None, tm=512, tn=1024, tk=2048):
    rows, m = a.shape
    n = b.shape[1]
    tm, tn, tk = min(tm, m), min(tn, n), min(tk, rows)
    if m % tm:
        tm = m
    if n % tn:
        tn = n

    def body(a_ref, b_ref, o_ref):
        @pl.when(pl.program_id(2) == 0)
        def _():
            o_ref[...] = jnp.zeros_like(o_ref)

        av = a_ref[...]
        if pre == "relu2":
            av = jnp.square(jnp.maximum(av.astype(F32), 0.0))
        o_ref[...] += _dot(av.astype(BF16), b_ref[...].astype(BF16), TN)

    return pl.pallas_call(
        body, name=name, grid=(m // tm, n // tn, rows // tk),
        out_shape=jax.ShapeDtypeStruct((m, n), F32),
        in_specs=[pl.BlockSpec((tk, tm), lambda i, j, k: (k, i)), pl.BlockSpec((tk, tn), lambda i, j, k: (k, j))],
        out_specs=pl.BlockSpec((tm, tn), lambda i, j, k: (i, j)),
        compiler_params=_params(("arbitrary", "arbitrary", "arbitrary")),
    )(a, b)


def _wgrad_packed(a, b, name, block_of, row_block, split=1, pre=None, into=None, tk=2048):
    rows, m = a.shape
    n = b.shape[1]
    tm = HALF
    part = tm // split
    tk = min(tk, rows)
    shape = jax.ShapeDtypeStruct((8, _group_rows(GROUP_MLP), PACK_COLS), F32)

    def body(a_ref, b_ref, *rest):
        o_ref = rest[-1]

        @pl.when(pl.program_id(2) == 0)
        def _():
            o_ref[...] = jnp.zeros_like(o_ref)

        av = a_ref[...]
        if pre == "relu2":
            av = jnp.square(jnp.maximum(av.astype(F32), 0.0))
        prod = _dot(av.astype(BF16), b_ref[...].astype(BF16), TN)
        for s in range(split):
            o_ref[s] += prod[s * part:(s + 1) * part]

    in_specs = [pl.BlockSpec((tk, tm), lambda i, j, k: (k, i)), pl.BlockSpec((tk, SHARD), lambda i, j, k: (k, j))]
    operands = [a, b]
    if into is not None:
        in_specs.append(pl.BlockSpec(memory_space=pl.ANY))
        operands.append(into)
    return pl.pallas_call(
        body, name=name, grid=(m // tm, n // SHARD, rows // tk), out_shape=shape,
        in_specs=in_specs,
        out_specs=pl.BlockSpec((split, part, SHARD), lambda i, j, k: (block_of(i, j), row_block, 0)),
        input_output_aliases={} if into is None else {2: 0},
        compiler_params=_params(("arbitrary", "arbitrary", "arbitrary")),
    )(*operands)


def _pair(pp):
    return slice(pp * LANES, (pp + 1) * LANES)


def _head_mask(lane, hh):
    return jnp.where((lane >= 64) if hh else (lane < 64), 1.0, 0.0).astype(BF16)


def _tri(t, kind):
    s = lax.broadcasted_iota(jnp.int32, (t, t), 0)
    j = lax.broadcasted_iota(jnp.int32, (t, t), 1)
    one = jnp.where(j > s if kind == "later" else j < s, 1.0, 0.0).astype(BF16)
    return jnp.concatenate([one, one], axis=1)


def _split_dot(tri2, v):
    hi = v.astype(BF16)
    lo = (v - hi.astype(F32)).astype(BF16)
    return _dot(tri2, jnp.concatenate([hi, lo], axis=0))


def _sb_logits(z, valid):
    log_keep = -(jnp.maximum(z, 0.0) + jnp.log(1.0 + jnp.exp(-jnp.abs(z))))
    log_beta = z + log_keep
    if valid is not None:
        log_keep = jnp.where(valid, log_keep, 0.0)
    return log_keep, log_beta


def _attention_call(body, ex, name, grid, operands, in_specs, out_shapes, out_specs, scratch=()):
    n_in, n_out = len(operands), len(out_shapes)
    total = grid[0] * grid[1] * grid[2]
    any_spec = pl.BlockSpec(memory_space=pl.ANY)

    def carrier(*refs):
        ins, outs, (start, middle, finish) = _carried(ex, refs, n_in, n_out, len(scratch))
        step = (pl.program_id(0) * grid[1] + pl.program_id(1)) * grid[2] + pl.program_id(2)
        pl.when(step == 0)(start)
        pl.when(step == total // 2)(middle)
        body(*ins, *outs)
        pl.when(step == total - 1)(finish)

    carried = ex is not None
    return pl.pallas_call(
        carrier if carried else body, name=name, grid=grid,
        out_shape=tuple(out_shapes) + ((ex.out_shape,) if carried else ()),
        in_specs=list(in_specs) + ([any_spec] if carried else []),
        out_specs=tuple(out_specs) + ((any_spec,) if carried else ()),
        scratch_shapes=list(scratch) + (ex.scratch if carried else []),
        compiler_params=_params(("arbitrary", "arbitrary", "arbitrary")),
    )(*operands, *([ex.operand] if carried else []))


def _sb_fwd(qkv, seq, ex=None):
    rows = qkv.shape[0]
    nb = rows // seq
    t = min(ATTN_TILE, seq)
    nq = seq // t
    assert nq <= CAR_SLOTS, (seq, t)
    ap = ATTN_PAIRS
    width = ap * LANES
    groups = SB_W // width
    hds = [(pp, hh) for pp in range(ap) for hh in range(2)]

    def body(q_ref, k_ref, v_ref, tri_ref, o_ref, car_ref, acc_ref):
        i = pl.program_id(2)
        lane = lax.broadcasted_iota(jnp.int32, (t, LANES), 1)
        key = lax.broadcasted_iota(jnp.int32, (t, t), 0)
        qry = lax.broadcasted_iota(jnp.int32, (t, t), 1)
        strict = key < qry
        tri = tri_ref[...]
        masks = [_head_mask(lane, hh) for hh in range(2)]
        qms = [q_ref[:, _pair(pp)] * masks[hh] for pp, hh in hds]
        acc_ref[...] = jnp.zeros_like(acc_ref)
        car_ref[...] = jnp.zeros_like(car_ref)

        def step(kb, c_sums, valid):
            start = pl.multiple_of(kb * t, t)
            kss = [k_ref[pl.ds(start, t), _pair(pp)] for pp in range(ap)]
            vss = [v_ref[pl.ds(start, t), _pair(pp)] for pp in range(ap)]
            zs = [_dot(kss[pp], qms[n], NT) for n, (pp, _) in enumerate(hds)]
            logs = [_sb_logits(z, valid) for z in zs]
            sufs = [_split_dot(tri, lg[0]) for lg in logs]
            new_sums = []
            for n, (pp, hh) in enumerate(hds):
                log_keep, log_beta = logs[n]
                w = jnp.exp(log_beta + sufs[n] + c_sums[n])
                if valid is not None:
                    w = jnp.where(valid, w, 0.0)
                acc_ref[pp] += _dot(vss[pp] * masks[hh], w.astype(BF16), TN)
                car_ref[0, pl.ds(n * CAR_SLOTS + kb, 1), :] = c_sums[n]
                new_sums.append(c_sums[n] + jnp.sum(log_keep, axis=0, keepdims=True))
            return tuple(new_sums)

        c_sums = step(i, tuple(jnp.zeros((1, t), F32) for _ in hds), strict)
        lax.fori_loop(0, i, lambda j, cr: step(i - 1 - j, cr, None), c_sums)
        for pp in range(ap):
            o_ref[:, _pair(pp)] = acc_ref[pp].T.astype(BF16)

    qspec = pl.BlockSpec((t, width), lambda b, p, i: (b * nq + i, p))
    car_rows = len(hds) * CAR_SLOTS
    return _attention_call(
        body, ex, "sb_fwd", (nb, groups, nq),
        [qkv, qkv, qkv, _tri(t, "later")],
        [qspec,
         pl.BlockSpec((seq, width), lambda b, p, i: (b, groups + p)),
         pl.BlockSpec((seq, width), lambda b, p, i: (b, 2 * groups + p)),
         _const_spec((t, 2 * t))],
        [jax.ShapeDtypeStruct((rows, SB_W), BF16), jax.ShapeDtypeStruct((nb * nq, HEADS * CAR_SLOTS, t), F32)],
        [qspec, pl.BlockSpec((1, car_rows, t), lambda b, p, i: (b * nq + i, p, 0))],
        scratch=[pltpu.VMEM((ap, LANES, t), F32)])


def _sb_bwd(qkv, d_out, cars, seq, ex=None):
    rows = qkv.shape[0]
    nb = rows // seq
    t = min(ATTN_TILE, seq)
    nq = seq // t
    ap = ATTN_PAIRS
    width = ap * LANES
    groups = SB_W // width
    hds = [(pp, hh) for pp in range(ap) for hh in range(2)]

    def body(q_ref, k_ref, v_ref, do_ref, car_ref, tri_ref, pre_ref, dq_ref, dk_ref, dv_ref, dq_acc):
        i = pl.program_id(2)

        @pl.when(i == 0)
        def _():
            dk_ref[...] = jnp.zeros_like(dk_ref)
            dv_ref[...] = jnp.zeros_like(dv_ref)

        lane = lax.broadcasted_iota(jnp.int32, (t, LANES), 1)
        key = lax.broadcasted_iota(jnp.int32, (t, t), 0)
        qry = lax.broadcasted_iota(jnp.int32, (t, t), 1)
        strict = key < qry
        tri, pre = tri_ref[...], pre_ref[...]
        masks = [_head_mask(lane, hh) for hh in range(2)]
        qms = [q_ref[:, _pair(pp)] * masks[hh] for pp, hh in hds]
        doms = [do_ref[:, _pair(pp)].astype(BF16) * masks[hh] for pp, hh in hds]
        dq_acc[...] = jnp.zeros_like(dq_acc)

        def step(kb, g_pres, valid):
            start = pl.multiple_of(kb * t, t)
            kss = [k_ref[pl.ds(start, t), _pair(pp)] for pp in range(ap)]
            vss = [v_ref[pl.ds(start, t), _pair(pp)] for pp in range(ap)]
            zs = [_dot(kss[pp], qms[n], NT) for n, (pp, _) in enumerate(hds)]
            dws = [_dot(vss[pp], doms[n], NT) for n, (pp, _) in enumerate(hds)]
            logs = [_sb_logits(z, valid) for z in zs]
            sufs = [_split_dot(tri, lg[0]) for lg in logs]
            ws, gs = [], []
            for n in range(len(hds)):
                c_sum = car_ref[0, pl.ds(n * CAR_SLOTS + kb, 1), :]
                w = jnp.exp(logs[n][1] + sufs[n] + c_sum)
                if valid is not None:
                    w = jnp.where(valid, w, 0.0)
                ws.append(w)
                gs.append(dws[n] * w)
            befores = [g_pres[n] + _split_dot(pre, gs[n]) for n in range(len(hds))]
            for pp in range(ap):
                a, b = 2 * pp, 2 * pp + 1
                dv_ref[pl.ds(start, t), _pair(pp)] += _dot(ws[a].astype(BF16), doms[a]) + _dot(ws[b].astype(BF16), doms[b])
            dzbs = []
            for n in range(len(hds)):
                beta = jnp.exp(logs[n][1])
                dz = gs[n] * (1.0 - beta) - beta * befores[n]
                if valid is not None:
                    dz = jnp.where(valid, dz, 0.0)
                dzbs.append(dz.astype(BF16))
            for pp in range(ap):
                a, b = 2 * pp, 2 * pp + 1
                dq_acc[pp] += _dot(dzbs[a], kss[pp] * masks[0], TN) + _dot(dzbs[b], kss[pp] * masks[1], TN)
                dk_ref[pl.ds(start, t), _pair(pp)] += _dot(dzbs[a], qms[a]) + _dot(dzbs[b], qms[b])
            return tuple(g_pres[n] + jnp.sum(gs[n], axis=0, keepdims=True) for n in range(len(hds)))

        g_pres = lax.fori_loop(0, i, lambda kb, cr: step(kb, cr, None), tuple(jnp.zeros((1, t), F32) for _ in hds))
        step(i, g_pres, strict)
        for pp in range(ap):
            dq_ref[:, _pair(pp)] = (dq_acc[pp] * SB_SCALE).astype(BF16)

    qspec = pl.BlockSpec((t, width), lambda b, p, i: (b * nq + i, p))
    kspec_out = pl.BlockSpec((seq, width), lambda b, p, i: (b, p))
    car_rows = len(hds) * CAR_SLOTS
    return _attention_call(
        body, ex, "sb_bwd", (nb, groups, nq),
        [qkv, qkv, qkv, d_out, cars, _tri(t, "later"), _tri(t, "earlier")],
        [qspec,
         pl.BlockSpec((seq, width), lambda b, p, i: (b, groups + p)),
         pl.BlockSpec((seq, width), lambda b, p, i: (b, 2 * groups + p)),
         qspec, pl.BlockSpec((1, car_rows, t), lambda b, p, i: (b * nq + i, p, 0)),
         _const_spec((t, 2 * t)), _const_spec((t, 2 * t))],
        [jax.ShapeDtypeStruct((rows, SB_W), BF16), jax.ShapeDtypeStruct((rows, SB_W), F32),
         jax.ShapeDtypeStruct((rows, SB_W), F32)],
        [qspec, kspec_out, kspec_out],
        scratch=[pltpu.VMEM((ap, t, LANES), F32)])


def _mla_scores(qh, ks, allowed):
    s = _dot(qh, ks, NT) * MLA_SCALE
    if allowed is not None:
        s = jnp.where(allowed, s, jnp.finfo(F32).min)
    return s


def _mla_fwd(qp, kp, vm, seq, ex=None, chunk=64):
    rows = qp.shape[0]
    nb = rows // seq
    t = min(ATTN_TILE, seq)
    nq = seq // t
    shift = int(math.log2(chunk))
    ap = ATTN_PAIRS
    width = ap * LANES
    groups = MLA_W // width
    hds = [(pp, hh) for pp in range(ap) for hh in range(2)]

    def body(q_ref, k_ref, v_ref, o_ref, lse_ref, acc_ref):
        i = pl.program_id(2)
        lane = lax.broadcasted_iota(jnp.int32, (t, LANES), 1)
        key = lax.broadcasted_iota(jnp.int32, (t, t), 0)
        qry = lax.broadcasted_iota(jnp.int32, (t, t), 1)
        allowed_diag = jnp.right_shift(key, shift) <= jnp.right_shift(qry, shift)
        masks = [_head_mask(lane, hh) for hh in range(2)]
        qhs = [q_ref[:, _pair(n)] for n in range(len(hds))]
        acc_ref[...] = jnp.zeros_like(acc_ref)

        def step(kb, carry, allowed):
            start = pl.multiple_of(kb * t, t)
            vss = [v_ref[pl.ds(start, t), _pair(pp)] for pp in range(ap)]
            scores = [_mla_scores(k_ref[pl.ds(start, t), _pair(n)], qhs[n], allowed) for n in range(len(hds))]
            new = []
            for n, (pp, hh) in enumerate(hds):
                m_run, l_run = carry[n]
                s = scores[n]
                m_new = jnp.maximum(m_run, jnp.max(s, axis=0, keepdims=True))
                p = jnp.exp(s - m_new)
                scale = jnp.exp(m_run - m_new)
                l_run = scale * l_run + jnp.sum(p, axis=0, keepdims=True)
                acc_ref[n] = scale * acc_ref[n] + _dot(vss[pp] * masks[hh], p.astype(BF16), TN)
                new.append((m_new, l_run))
            return tuple(new)

        init = (jnp.full((1, t), jnp.finfo(F32).min, F32), jnp.zeros((1, t), F32))
        carry = step(i, tuple(init for _ in hds), allowed_diag)
        carry = lax.fori_loop(0, i, lambda kb, cr: step(kb, cr, None), carry)
        lse_rows = []
        for pp in range(ap):
            out_t = jnp.zeros((LANES, t), F32)
            for hh in range(2):
                m_run, l_run = carry[2 * pp + hh]
                out_t = out_t + acc_ref[2 * pp + hh] / l_run
                lse_rows.append(m_run + jnp.log(l_run))
            o_ref[:, _pair(pp)] = out_t.T
        lse_t = jnp.concatenate(lse_rows + [jnp.zeros((LANES - len(hds), t), F32)], axis=0)
        lse_ref[...] = jnp.zeros_like(lse_ref)
        lse_ref[:, _pair(0)] = lse_t.T

    ospec = pl.BlockSpec((t, width), lambda b, p, i: (b * nq + i, p))
    return _attention_call(
        body, ex, "mla_fwd", (nb, groups, nq), [qp, kp, vm],
        [pl.BlockSpec((t, 2 * width), lambda b, p, i: (b * nq + i, p)),
         pl.BlockSpec((seq, 2 * width), lambda b, p, i: (b, p)),
         pl.BlockSpec((seq, width), lambda b, p, i: (b, p))],
        [jax.ShapeDtypeStruct((rows, MLA_W), F32), jax.ShapeDtypeStruct((rows, MLA_W), F32)],
        [ospec, ospec], scratch=[pltpu.VMEM((len(hds), LANES, t), F32)])


def _mla_bwd(qp, kp, vm, d_out, out, lse, seq, ex=None, chunk=64):
    rows = qp.shape[0]
    nb = rows // seq
    t = min(ATTN_TILE, seq)
    nq = seq // t
    shift = int(math.log2(chunk))
    ap = ATTN_PAIRS
    width = ap * LANES
    groups = MLA_W // width
    hds = [(pp, hh) for pp in range(ap) for hh in range(2)]
    nh = len(hds)

    def body(q_ref, k_ref, v_ref, do_ref, o_ref, lse_ref, dq_ref, dk_ref, dv_ref):
        i = pl.program_id(2)

        @pl.when(i == 0)
        def _():
            dk_ref[...] = jnp.zeros_like(dk_ref)
            dv_ref[...] = jnp.zeros_like(dv_ref)

        lane = lax.broadcasted_iota(jnp.int32, (t, LANES), 1)
        key = lax.broadcasted_iota(jnp.int32, (t, t), 0)
        qry = lax.broadcasted_iota(jnp.int32, (t, t), 1)
        allowed_diag = jnp.right_shift(key, shift) <= jnp.right_shift(qry, shift)
        qhs = [q_ref[:, _pair(n)] for n in range(nh)]
        lse_t = lse_ref[:, _pair(0)].T
        doms, deltas, lse_hs = [], [], []
        for pp in range(ap):
            do = do_ref[:, _pair(pp)]
            d_o_t = (do * o_ref[:, _pair(pp)]).T
            for hh in range(2):
                doms.append(do.astype(BF16) * _head_mask(lane, hh))
                deltas.append(jnp.sum(d_o_t[hh * 64:(hh + 1) * 64], axis=0, keepdims=True))
                lse_hs.append(lse_t[2 * pp + hh:2 * pp + hh + 1])

        dq_ref[...] = jnp.zeros_like(dq_ref)

        def step(kb, allowed):
            start = pl.multiple_of(kb * t, t)
            vss = [v_ref[pl.ds(start, t), _pair(pp)] for pp in range(ap)]
            kss = [k_ref[pl.ds(start, t), _pair(n)] for n in range(nh)]
            scores = [_mla_scores(kss[n], qhs[n], allowed) for n in range(nh)]
            dps = [_dot(vss[pp], doms[n], NT) for n, (pp, _) in enumerate(hds)]
            ps = [jnp.exp(scores[n] - lse_hs[n]) for n in range(nh)]
            dss = [(ps[n] * (dps[n] - deltas[n]) * MLA_SCALE).astype(BF16) for n in range(nh)]
            for pp in range(ap):
                a, b = 2 * pp, 2 * pp + 1
                dv_ref[pl.ds(start, t), _pair(pp)] += _dot(ps[a].astype(BF16), doms[a]) + _dot(ps[b].astype(BF16), doms[b])
            for n in range(nh):
                dk_ref[pl.ds(start, t), _pair(n)] += _dot(dss[n], qhs[n])
                dq_ref[:, _pair(n)] += _dot(dss[n], kss[n], TN)

        def off_diagonal(kb, nothing):
            step(kb, None)
            return nothing

        lax.fori_loop(0, i, off_diagonal, 0)
        step(i, allowed_diag)

    ospec = pl.BlockSpec((t, width), lambda b, p, i: (b * nq + i, p))
    return _attention_call(
        body, ex, "mla_bwd", (nb, groups, nq), [qp, kp, vm, d_out, out, lse],
        [pl.BlockSpec((t, 2 * width), lambda b, p, i: (b * nq + i, p)),
         pl.BlockSpec((seq, 2 * width), lambda b, p, i: (b, p)),
         pl.BlockSpec((seq, width), lambda b, p, i: (b, p)),
         pl.BlockSpec((t, width), lambda b, p, i: (b * nq + i, groups + p)),
         ospec, ospec],
        [jax.ShapeDtypeStruct((rows, HEADS * LANES), F32), jax.ShapeDtypeStruct((rows, HEADS * LANES), F32),
         jax.ShapeDtypeStruct((rows, MLA_W), F32)],
        [pl.BlockSpec((t, 2 * width), lambda b, p, i: (b * nq + i, p)),
         pl.BlockSpec((seq, 2 * width), lambda b, p, i: (b, p)),
         pl.BlockSpec((seq, width), lambda b, p, i: (b, p))])


PACK_COLS = 1024
GROUP_IN = (640, ((1024, 552, 1), (384, 192, 1), (256, 256, 1)))
GROUP_MLP = (PACK_COLS, ((1024, 1024, 1), (1024, 1024, 0), (256, 1024, 0)))


def _group_rows(group):
    return sum(r // 2 for r, _, _ in group[1])


def _join_slots(parts, group):
    width, weights = group
    return jnp.concatenate([jnp.pad(p, ((0, 0), (0, 0), (0, width - c))) for p, (_, c, _) in zip(parts, weights)], axis=1)


def _split_slots(packed, group):
    out, at = [], 0
    for r, c, _ in group[1]:
        out.append(packed[:, at:at + r // 2, :c])
        at += r // 2
    return out


def _pack_halves(shards, group):
    return _join_slots([s.reshape(2, r // 2, c) for s, (r, c, _) in zip(shards, group[1])], group)


def _unpack_halves(packed, group):
    return [p.reshape(r, c) for p, (r, c, _) in zip(_split_slots(packed, group), group[1])]


def _unpack_full(gathered, group):
    out = []
    for p, (r, c, axis) in zip(_split_slots(gathered, group), group[1]):
        shards = p.reshape(4, r, c)
        out.append(shards.reshape(4 * r, c) if axis == 0 else jnp.moveaxis(shards, 0, 1).reshape(r, 4 * c))
    return out


def _pack_full(grads, group):
    parts = []
    for gr, (r, c, axis) in zip(grads, group[1]):
        shards = gr.reshape(4, r, c) if axis == 0 else jnp.moveaxis(gr.reshape(r, 4, c), 1, 0)
        parts.append(shards.reshape(8, r // 2, c))
    return _join_slots(parts, group)


def _pad_w_in(w_in):
    z = jnp.zeros((D_MODEL, 1), w_in.dtype)
    return jnp.concatenate([w_in[:, :2176], jnp.tile(z, (1, 64)), w_in[:, 2176:], jnp.tile(z, (1, 32))], axis=1)


def _unpad_w_in(g):
    return jnp.concatenate([g[:, :2176], g[:, 2240:2272]], axis=1)


def _pad_heads(w, used):
    k = w.shape[0]
    w3 = w.reshape(k, HEADS, used)
    return jnp.pad(w3, ((0, 0), (0, 0), (0, LANES - used))).reshape(k, HEADS * LANES)


def _unpad_heads(g, used):
    k = g.shape[0]
    return g.reshape(k, HEADS, LANES)[:, :, :used].reshape(k, HEADS * used)


def _rope_tables(seq):
    inv_freq = 1.0 / (ROPE_BASE ** (jnp.arange(0, ROPE, 2, dtype=F32) / ROPE))
    ang = jnp.arange(seq, dtype=F32)[:, None] * inv_freq[None, :]
    cos, sin = jnp.cos(ang), jnp.sin(ang)
    one, zero = jnp.ones((seq, NOPE), F32), jnp.zeros((seq, NOPE), F32)
    z16, z32 = jnp.zeros((seq, 16), F32), jnp.zeros((seq, 32), F32)
    cos_t = jnp.concatenate([one, cos, cos, jnp.ones((seq, 32), F32)], axis=1)
    sin_a = jnp.concatenate([zero, -sin, z16, z32], axis=1)
    sin_b = jnp.concatenate([zero, z16, sin, z32], axis=1)
    return cos_t, sin_a, sin_b


SMALL = (("ln_in_g", 1024), ("ln_in_b", 1024), ("b_ada", 6144), ("q_norm_g", 384), ("kv_norm_g", 256),
         ("ln1_g", 1024), ("ln1_b", 1024), ("ln2_g", 1024), ("ln2_b", 1024))
SUBLANES = 8
SMALL_SLOTS = [-(-n // LANES // SUBLANES) * SUBLANES for _, n in SMALL]
SMALL_AT = [sum(SMALL_SLOTS[:p]) for p in range(len(SMALL))]
SMALL_ROWS = sum(SMALL_SLOTS)


def _pack_small(vals):
    parts = []
    for v, slot in zip(vals, SMALL_SLOTS):
        rows = v.reshape(-1, LANES)
        parts.append(jnp.pad(rows, ((0, slot - rows.shape[0]), (0, 0))))
    return jnp.concatenate(parts, axis=0)


def kernel(x, c, ln_in_g, ln_in_b, w_ada, b_ada, w_in, q_norm_g, kv_norm_g, w_uq, w_ukv, w_o, ln1_g, ln1_b, w_up, w_down, ln2_g, ln2_b, loss_target, m_ln_in_g, m_ln_in_b, m_w_ada, m_b_ada, m_w_in, m_q_norm_g, m_kv_norm_g, m_w_uq, m_w_ukv, m_w_o, m_ln1_g, m_ln1_b, m_w_up, m_w_down, m_ln2_g, m_ln2_b, v_ln_in_g, v_ln_in_b, v_w_ada, v_b_ada, v_w_in, v_q_norm_g, v_kv_norm_g, v_w_uq, v_w_ukv, v_w_o, v_ln1_g, v_ln1_b, v_w_up, v_w_down, v_ln2_g, v_ln2_b):
    nb, seq, _ = x.shape
    rows = nb * seq
    ix, iy, ic = lax.axis_index("x"), lax.axis_index("y"), lax.axis_index("c")
    chip = 2 * ix + iy
    dev = 2 * chip + ic

    def my_half(shards, group):
        packed = _pack_halves([s.astype(BF16) for s in shards], group)
        return lax.dynamic_index_in_dim(packed, ic, 0, keepdims=False)

    f_in, f_uq, f_ukv = _unpack_full(_gather8(my_half([w_in[0], w_uq[0], w_ukv[0]], GROUP_IN), "gather_w_in"),
                                     GROUP_IN)
    half_mlp = my_half([w_up[0], w_down[0], w_o[0]], GROUP_MLP)
    late_weights = _gather_exchange(half_mlp)
    w_in_p = _pad_w_in(f_in)
    uq3 = f_uq.reshape(Q_RANK, HEADS, NOPE + ROPE)
    w_uq_p = jnp.pad(uq3, ((0, 0), (0, 0), (0, LANES - NOPE - ROPE))).reshape(Q_RANK, HEADS * LANES)
    w_ukv_p = jnp.concatenate([_pad_heads(f_ukv[:, :HEADS * NOPE], NOPE), f_ukv[:, HEADS * NOPE:]], axis=1)

    n_all = 8 * nb
    c_all = _gather8(c.reshape(-1, LANES), "gather_c").reshape(n_all, D_MODEL)
    ada_cols = w_ada.shape[2]
    b_sh = lax.dynamic_slice_in_dim(b_ada, chip * ada_cols, ada_cols, axis=1)
    mod_sh = _ada_fwd(c_all, w_ada[0], b_sh)
    mod_g = _gather8(mod_sh, "gather_mod")[0::2]
    mod_all = jnp.moveaxis(mod_g, 0, 1).reshape(n_all, N_MOD * D_MODEL)
    mod_mine = lax.dynamic_slice_in_dim(mod_all, dev * nb, nb, axis=0).reshape(nb, N_MOD, D_MODEL)
    mod = jnp.pad(mod_mine, ((0, 0), (0, 8 - N_MOD), (0, 0)))

    cos_t, sin_a, sin_b = _rope_tables(seq)
    row2 = lambda v: v.reshape(1, -1)

    x2d = x.reshape(rows, D_MODEL)
    x0, h, qkv, lat, qp, kp, vm = _fwd_in(x2d, mod, row2(ln_in_g), row2(ln_in_b), w_in_p, q_norm_g, kv_norm_g,
                                          w_uq_p, w_ukv_p, cos_t, sin_a, sin_b, seq)
    sb_y, cars, g_mlp = _sb_fwd(qkv, seq, late_weights)
    g_mlp = _with_own(g_mlp, half_mlp)
    f_o = _split_slots(g_mlp, GROUP_MLP)[2].reshape(D_MODEL, D_MODEL)
    mla_y, lse = _mla_fwd(qp, kp, vm, seq)
    mix, y1, h2, u, ff, y2 = _fwd_out(sb_y, mla_y, x0, mod, f_o, ln1_g, ln1_b, g_mlp, seq)

    dy1, dmix, d_attn, dff, du, acc_out, dmod_a = _bwd_out(
        y2, loss_target.reshape(rows, D_MODEL), ff, u, y1, mix, mod, ln2_g, ln2_b, ln1_g, ln1_b, g_mlp, f_o, seq)
    c_idx = ic.reshape(1).astype(jnp.int32)
    blocks_mlp = _wgrad_packed(h2, du, "wgrad_up", lambda i, j: 2 * j + i, 0)
    blocks_mlp = _wgrad_packed(u, dff, "wgrad_down", lambda i, j: i, 1, pre="relu2", into=blocks_mlp)
    blocks_mlp = _wgrad_packed(sb_y, dmix, "wgrad_o_sb", lambda i, j: 0, 8, split=4, into=blocks_mlp)
    blocks_mlp = _wgrad_packed(mla_y, dmix, "wgrad_o_mla", lambda i, j: 1, 8, split=4, into=blocks_mlp)
    dq_sb, dk_sb, dv_sb, sibling_mlp = _sb_bwd(qkv, d_attn, cars, seq, _swap_cores_exchange(blocks_mlp))
    part_mlp, part_mlp_bf = _add_pairs(blocks_mlp, sibling_mlp, c_idx, "grad_add_cores_mlp")
    dqp, dkp, dvm, chips_mlp = _mla_bwd(qp, kp, vm, d_attn, mla_y, lse, seq, _scatter_chips_exchange(part_mlp_bf))
    grad_x, dproj, dqall, dkv, latn, acc0, acc_lat, dmod_c = _bwd_in(
        dqp, dkp, dvm, dq_sb, dk_sb, dv_sb, lat, x2d, x0, dy1, mod, row2(ln_in_g), row2(ln_in_b), w_in_p,
        q_norm_g, kv_norm_g, w_uq_p, w_ukv_p, cos_t, sin_a, sin_b, seq)

    g_in = _unpad_w_in(_wgrad(h, dproj, "wgrad_in", tn=768))
    g_uq = _unpad_heads(_wgrad(latn[:, :Q_RANK], dqall, "wgrad_uq"), NOPE + ROPE)
    g_ukv_p = _wgrad(latn[:, Q_RANK:], dkv, "wgrad_ukv", tn=512)
    g_ukv = jnp.concatenate([_unpad_heads(g_ukv_p[:, :HEADS * LANES], NOPE), g_ukv_p[:, HEADS * LANES:]], axis=1)
    blocks_in = _pack_full([g_in, g_uq, g_ukv], GROUP_IN)
    sibling_in = _run_exchange(_swap_cores_exchange(blocks_in), "grads_in_to_sibling")
    part_in, part_in_bf = _add_pairs(blocks_in, sibling_in, c_idx, "grad_add_cores_in")
    chips_in = _run_exchange(_scatter_chips_exchange(part_in_bf), "grads_in_to_chips")

    def own(part):
        return lax.dynamic_index_in_dim(part, chip, 0, keepdims=False)

    half_mlp = _add_chips(own(part_mlp), chips_mlp, "grad_add_chips_mlp")
    half_in = _add_chips(own(part_in), chips_in, "grad_add_chips_in")
    other_mlp = _run_exchange(_swap_one_exchange(half_mlp), "grads_halves_mlp")
    other_in = _run_exchange(_swap_one_exchange(half_in), "grads_halves_in")
    mine = _split_slots(half_in[None], GROUP_IN) + _split_slots(half_mlp[None], GROUP_MLP)
    theirs = _split_slots(other_in[None], GROUP_IN) + _split_slots(other_mlp[None], GROUP_MLP)

    dmod = (dmod_a + dmod_c)[:, :N_MOD, :]
    small_part = _pack_small([acc0[0], acc0[1], jnp.zeros((N_MOD * D_MODEL,), F32), acc_lat[0, :Q_RANK],
                              acc_lat[1, :KV_RANK], acc_out[3], acc_out[4], acc_out[0], acc_out[1]])
    n_sum = SMALL_ROWS + D_MODEL // LANES
    payload = jnp.concatenate([small_part, acc_out[2].reshape(-1, LANES), dmod.reshape(-1, LANES)], axis=0)
    gathered = _gather8(payload, "gather_small")
    small_sum = _sum_lead(gathered[:, :n_sum, :], "sum_small")
    loss = jnp.sum(small_sum[SMALL_ROWS:])
    dmod_all = gathered[:, n_sum:, :].reshape(n_all, N_MOD * D_MODEL)
    g_b_ada = _sum_lead(dmod_all.reshape(n_all, N_MOD * D_MODEL // LANES, LANES), "sum_b_ada")
    dmod_sh = lax.dynamic_slice_in_dim(dmod_all, chip * ada_cols, ada_cols, axis=1)
    g_w_ada = _ada_bwd(c_all, dmod_sh)

    res = {}
    d_ada, m_ada, v_ada = _adamw(w_ada[0], g_w_ada, m_w_ada[0], v_w_ada[0], "adamw_w_ada")
    res["w_ada"] = (g_w_ada[None], d_ada[None], m_ada[None], v_ada[None])
    sharded = {"w_in": (w_in, m_w_in, v_w_in), "w_uq": (w_uq, m_w_uq, v_w_uq), "w_ukv": (w_ukv, m_w_ukv, v_w_ukv),
               "w_up": (w_up, m_w_up, v_w_up), "w_down": (w_down, m_w_down, v_w_down), "w_o": (w_o, m_w_o, v_w_o)}
    for (name, (w, m, v)), g_mine, g_other in zip(sharded.items(), mine, theirs):
        quad = _adamw_halves(w[0], g_mine[0], g_other[0], m[0], v[0], c_idx, "adamw_" + name)
        res[name] = tuple(a[None] for a in quad)
    small_w = [ln_in_g, ln_in_b, b_ada, q_norm_g, kv_norm_g, ln1_g, ln1_b, ln2_g, ln2_b]
    small_m = [m_ln_in_g, m_ln_in_b, m_b_ada, m_q_norm_g, m_kv_norm_g, m_ln1_g, m_ln1_b, m_ln2_g, m_ln2_b]
    small_v = [v_ln_in_g, v_ln_in_b, v_b_ada, v_q_norm_g, v_kv_norm_g, v_ln1_g, v_ln1_b, v_ln2_g, v_ln2_b]
    for (name, _), quad in zip(SMALL, _adamw_small(small_sum, g_b_ada, small_w, small_m, small_v)):
        res[name] = quad

    order = ["ln_in_g", "ln_in_b", "w_ada", "b_ada", "w_in", "q_norm_g", "kv_norm_g", "w_uq", "w_ukv", "w_o",
             "ln1_g", "ln1_b", "w_up", "w_down", "ln2_g", "ln2_b"]
    outs = [loss, grad_x.reshape(nb, seq, D_MODEL)]
    for k in range(4):
        outs += [res[name][k] for name in order]
    return tuple(outs)
```

```python
import functools
import math

import jax
import jax.numpy as jnp
from jax import lax
from jax.experimental import pallas as pl
from jax.experimental.pallas import tpu as pltpu

F32 = jnp.float32
BF16 = jnp.bfloat16
MESH_IDS = pl.DeviceIdType.MESH

D_MODEL = 1024
HEADS = 8
HEAD_PAIRS = HEADS // 2
SB_W = 512
MLA_W = 512
NOPE = 64
ROPE = 32
Q_RANK = 384
KV_RANK = 256
D_IN = 2208
D_IN_PAD = 2304
D_FF = 4096
N_MOD = 6
LN_EPS = 1e-5
RMS_EPS = 1e-6
ALPHA = 2.0 ** 0.25
ROPE_BASE = 10000.0
SB_SCALE = 64 ** -0.5
MLA_SCALE = 96 ** -0.5
ADAM_LR = 0.001
ADAM_B1 = 0.9
ADAM_B2 = 0.999
ADAM_EPS = 1e-08
ADAM_WD = 0.01
ADAM_STEP = 10

LANES = 128
ROW_TILE = 256
ATTN_TILE = 256
CAR_SLOTS = 8
ATTN_PAIRS = 4
VMEM_LIMIT = 56 << 20

NT = (((1,), (1,)), ((), ()))
TN = (((0,), (0,)), ((), ()))


def _params(sem=None):
    return pltpu.CompilerParams(vmem_limit_bytes=VMEM_LIMIT, dimension_semantics=sem)


def _const_spec(shape):
    zeros = (0,) * len(shape)
    return pl.BlockSpec(shape, lambda *_: zeros, pipeline_mode=pl.Buffered(1))


def _dot(a, b, dims=None):
    if dims is None:
        return jnp.dot(a, b, preferred_element_type=F32)
    return lax.dot_general(a, b, dims, preferred_element_type=F32)


def _mean(v):
    return jnp.mean(v, axis=-1, keepdims=True)


def _rowsum(v):
    return jnp.sum(v, axis=0, keepdims=True)


def _ln_fwd(y, g, b):
    mu = _mean(y)
    yc = y - mu
    rstd = lax.rsqrt(_mean(yc * yc) + LN_EPS)
    xhat = yc * rstd
    return xhat * g + b, xhat, rstd


def _ln_bwd(dx, xhat, rstd, g):
    dxh = dx * g
    return rstd * (dxh - _mean(dxh) - xhat * _mean(dxh * xhat))


def _rope(v, cos, sin_a, sin_b):
    return v * cos + pltpu.roll(v, 112, 1) * sin_a + pltpu.roll(v, 16, 1) * sin_b


def _rope_t(dv, cos, sin_a, sin_b):
    return dv * cos + pltpu.roll(dv * sin_a, 16, 1) + pltpu.roll(dv * sin_b, 112, 1)


def _my_place():
    return lax.axis_index("x"), lax.axis_index("y"), lax.axis_index("c")


class _Exchange:
    def __init__(self, operand, out_shape, n_copies, phases):
        self.operand = operand
        self.out_shape = out_shape
        self.phases = phases
        self.scratch = [pltpu.SemaphoreType.DMA((n_copies,)), pltpu.SemaphoreType.DMA((n_copies,))]


def _run_exchange(ex, name):
    def body(in_ref, out_ref, send_sems, recv_sems):
        for phase in ex.phases(in_ref, out_ref, send_sems, recv_sems):
            phase()

    return pl.pallas_call(
        body, name=name, out_shape=ex.out_shape,
        in_specs=[pl.BlockSpec(memory_space=pl.ANY)], out_specs=pl.BlockSpec(memory_space=pl.ANY),
        scratch_shapes=ex.scratch,
    )(ex.operand)


def _nothing():
    pass


def _gather_exchange(v):
    m, n = v.shape

    def phases(v_ref, out_ref, send_sems, recv_sems):
        x, y, c = _my_place()
        me, sibling = (x, y, c), (x, y, 1 - c)
        chips = [(1 - x, y), (x, 1 - y), (1 - x, 1 - y)]

        def rows(px, py, pc):
            return out_ref.at[4 * px + 2 * py + pc]

        def copy(k, block, to, src=None):
            return pltpu.make_async_remote_copy(
                src_ref=rows(*block) if src is None else src, dst_ref=rows(*block),
                send_sem=send_sems.at[k], recv_sem=recv_sems.at[k], device_id=to, device_id_type=MESH_IDS)

        first = [copy(0, me, sibling, src=v_ref)]
        first += [copy(1 + j, me, (*chip, c), src=v_ref) for j, chip in enumerate(chips)]
        passed = [copy(4 + j, (*chip, c), sibling) for j, chip in enumerate(chips)]

        def start():
            for cp in first:
                cp.start()

        def middle():
            for j, chip in enumerate(chips):
                copy(1 + j, (*chip, c), me).wait_recv()
                passed[j].start()

        def finish():
            copy(0, sibling, me).wait_recv()
            for j, chip in enumerate(chips):
                copy(4 + j, (*chip, 1 - c), me).wait_recv()
            for cp in first + passed:
                cp.wait_send()

        return start, middle, finish

    return _Exchange(v, jax.ShapeDtypeStruct((8, m, n), v.dtype), 7, phases)


def _with_own(gathered, v):
    dev = 4 * lax.axis_index("x") + 2 * lax.axis_index("y") + lax.axis_index("c")
    return lax.dynamic_update_index_in_dim(gathered, v, dev, 0)


def _direct_exchange(operand, out_shape, n_copies, make_copies):
    def phases(in_ref, out_ref, send_sems, recv_sems):
        copies = make_copies(in_ref, out_ref, send_sems, recv_sems)

        def start():
            for cp in copies:
                cp.start()

        def finish():
            for cp in copies:
                cp.wait()

        return start, _nothing, finish

    return _Exchange(operand, out_shape, n_copies, phases)


def _swap_cores_exchange(blocks):
    _, m, n = blocks.shape

    def make_copies(g_ref, out_ref, send_sems, recv_sems):
        x, y, c = _my_place()
        return [pltpu.make_async_remote_copy(
            src_ref=g_ref.at[2 * j + (1 - c)], dst_ref=out_ref.at[j],
            send_sem=send_sems.at[j], recv_sem=recv_sems.at[j],
            device_id=(x, y, 1 - c), device_id_type=MESH_IDS) for j in range(4)]

    return _direct_exchange(blocks, jax.ShapeDtypeStruct((4, m, n), blocks.dtype), 4, make_copies)


def _scatter_chips_exchange(parts):
    _, m, n = parts.shape
    flips = [(1, 0), (0, 1), (1, 1)]

    def make_copies(p_ref, out_ref, send_sems, recv_sems):
        x, y, c = _my_place()
        copies = []
        for k, (fx, fy) in enumerate(flips):
            tx = 1 - x if fx else x
            ty = 1 - y if fy else y
            copies.append(pltpu.make_async_remote_copy(
                src_ref=p_ref.at[2 * tx + ty], dst_ref=out_ref.at[k],
                send_sem=send_sems.at[k], recv_sem=recv_sems.at[k],
                device_id=(tx, ty, c), device_id_type=MESH_IDS))
        return copies

    return _direct_exchange(parts, jax.ShapeDtypeStruct((3, m, n), parts.dtype), 3, make_copies)


def _swap_one_exchange(v):
    def make_copies(v_ref, out_ref, send_sems, recv_sems):
        x, y, c = _my_place()
        return [pltpu.make_async_remote_copy(src_ref=v_ref, dst_ref=out_ref, send_sem=send_sems.at[0],
                                             recv_sem=recv_sems.at[0], device_id=(x, y, 1 - c),
                                             device_id_type=MESH_IDS)]

    return _direct_exchange(v, jax.ShapeDtypeStruct(v.shape, v.dtype), 1, make_copies)


def _gather8(v, name):
    return _with_own(_run_exchange(_gather_exchange(v), name), v)


def _carried(ex, refs, n_in, n_out, n_scratch):
    ins, ex_in = refs[:n_in], refs[n_in]
    outs, ex_out = refs[n_in + 1:n_in + 1 + n_out], refs[n_in + 1 + n_out]
    at = n_in + 2 + n_out
    return ins, outs + refs[at:at + n_scratch], ex.phases(ex_in, ex_out, *refs[at + n_scratch:])


def _ada_fwd(c_all, w_ada_sh, b_ada_sh):
    nb, cols = c_all.shape[0], w_ada_sh.shape[1]
    tn = 512

    def body(c_ref, w_ref, b_ref, o_ref):
        cv = c_ref[...]
        act = (cv * jax.nn.sigmoid(cv)).astype(BF16)
        o_ref[...] = _dot(act, w_ref[...].astype(BF16)) + b_ref[...]

    return pl.pallas_call(
        body, name="ada_fwd", grid=(cols // tn,),
        out_shape=jax.ShapeDtypeStruct((nb, cols), F32),
        in_specs=[pl.BlockSpec((nb, D_MODEL), lambda j: (0, 0)),
                  pl.BlockSpec((D_MODEL, tn), lambda j: (0, j)),
                  pl.BlockSpec((1, tn), lambda j: (0, j))],
        out_specs=pl.BlockSpec((nb, tn), lambda j: (0, j)),
        compiler_params=_params(("arbitrary",)),
    )(c_all, w_ada_sh, b_ada_sh)


def _ada_bwd(c_all, dmod_sh):
    nb, cols = dmod_sh.shape
    tn = 512

    def body(c_ref, d_ref, o_ref):
        cv = c_ref[...]
        act = (cv * jax.nn.sigmoid(cv)).astype(BF16)
        o_ref[...] = _dot(act, d_ref[...].astype(BF16), TN)

    return pl.pallas_call(
        body, name="ada_bwd", grid=(cols // tn,),
        out_shape=jax.ShapeDtypeStruct((D_MODEL, cols), F32),
        in_specs=[pl.BlockSpec((nb, D_MODEL), lambda j: (0, 0)),
                  pl.BlockSpec((nb, tn), lambda j: (0, j))],
        out_specs=pl.BlockSpec((D_MODEL, tn), lambda j: (0, j)),
        compiler_params=_params(("arbitrary",)),
    )(c_all, dmod_sh)


def _sum_lead(v, name):
    k, m, n = v.shape

    def body(v_ref, o_ref):
        acc = v_ref[0]
        for i in range(1, k):
            acc = acc + v_ref[i]
        o_ref[...] = acc

    return pl.pallas_call(
        body, name=name, out_shape=jax.ShapeDtypeStruct((m, n), F32),
        in_specs=[pl.BlockSpec((k, m, n), lambda: (0, 0, 0))],
        out_specs=pl.BlockSpec((m, n), lambda: (0, 0)),
        compiler_params=_params(),
    )(v)


def _adamw_math(w, g, m, v):
    mn = ADAM_B1 * m + (1.0 - ADAM_B1) * g
    vn = ADAM_B2 * v + (1.0 - ADAM_B2) * (g * g)
    m_hat = mn / (1.0 - ADAM_B1 ** ADAM_STEP)
    v_hat = vn / (1.0 - ADAM_B2 ** ADAM_STEP)
    return -ADAM_LR * (m_hat / (jnp.sqrt(v_hat) + ADAM_EPS) + ADAM_WD * w), mn, vn


def _adamw_small(g_sum, g_b_ada, ws, ms, vs):
    n = len(SMALL)

    def body(gs_ref, gb_ref, *refs):
        outs = refs[3 * n:]
        for p in range(n):
            rows_p = SMALL[p][1] // LANES
            g = gb_ref[...] if SMALL[p][0] == "b_ada" else gs_ref[SMALL_AT[p]:SMALL_AT[p] + rows_p, :]
            d, mn, vn = _adamw_math(refs[p][...], g, refs[n + p][...], refs[2 * n + p][...])
            outs[4 * p][...] = g
            outs[4 * p + 1][...] = d
            outs[4 * p + 2][...] = mn
            outs[4 * p + 3][...] = vn

    shapes = [jax.ShapeDtypeStruct((size // LANES, LANES), F32) for _, size in SMALL for _ in range(4)]
    flat = lambda arrs: [a.reshape(-1, LANES) for a in arrs]
    res = pl.pallas_call(body, name="adamw_small", out_shape=tuple(shapes), compiler_params=_params())(
        g_sum, g_b_ada, *flat(ws), *flat(ms), *flat(vs))
    return [tuple(r.reshape(w.shape) for r in res[4 * p:4 * p + 4]) for p, w in enumerate(ws)]


def _adamw_halves(w, g_mine, g_other, m, v, c_idx, name):
    r, cols = w.shape
    half = r // 2
    tr = half
    while tr * cols * 4 > (2 << 20) and tr % 16 == 0:
        tr //= 2

    def body(c_ref, w_ref, mine_ref, other_ref, m_ref, v_ref, g_ref, d_ref, mo_ref, vo_ref):
        g = jnp.where(pl.program_id(0) == c_ref[0], mine_ref[...], other_ref[...])
        g_ref[0] = g
        d_ref[0], mo_ref[0], vo_ref[0] = _adamw_math(w_ref[0], g, m_ref[0], v_ref[0])

    full = pl.BlockSpec((1, tr, cols), lambda h, i, c: (h, i, 0))
    part = pl.BlockSpec((tr, cols), lambda h, i, c: (i, 0))
    shape = jax.ShapeDtypeStruct((2, half, cols), F32)
    grid_spec = pltpu.PrefetchScalarGridSpec(
        num_scalar_prefetch=1, grid=(2, half // tr),
        in_specs=[full, part, part, full, full], out_specs=(full, full, full, full))
    split = lambda a: a.reshape(2, half, cols)
    res = pl.pallas_call(
        body, name=name, grid_spec=grid_spec, out_shape=(shape, shape, shape, shape),
        compiler_params=_params(("arbitrary", "arbitrary")),
    )(c_idx, split(w), g_mine, g_other, split(m), split(v))
    return tuple(a.reshape(r, cols) for a in res)


def _adamw(w, g, m, v, name):
    rows, cols = w.shape
    tr = rows
    while tr * cols * 4 > (2 << 20) and tr % 16 == 0:
        tr //= 2

    def body(w_ref, g_ref, m_ref, v_ref, d_ref, mo_ref, vo_ref):
        d_ref[...], mo_ref[...], vo_ref[...] = _adamw_math(w_ref[...], g_ref[...], m_ref[...], v_ref[...])

    spec = pl.BlockSpec((tr, cols), lambda i: (i, 0))
    shape = jax.ShapeDtypeStruct((rows, cols), F32)
    return pl.pallas_call(
        body, name=name, grid=(rows // tr,), out_shape=(shape, shape, shape),
        in_specs=[spec, spec, spec, spec], out_specs=(spec, spec, spec),
        compiler_params=_params(("arbitrary",)),
    )(w, g, m, v)


def _add_rows(m, n):
    fits = [d for d in range(16, m + 1, 16) if m % d == 0 and d * n * 4 <= (5 << 19)]
    assert fits, (m, n)
    return max(fits)


def _add_pairs(blocks, recv, c_idx, name):
    _, m, n = blocks.shape
    tr = _add_rows(m, n)

    def body(c_ref, a_ref, b_ref, o_ref, ob_ref):
        s = a_ref[...] + b_ref[...]
        o_ref[...] = s
        ob_ref[...] = s.astype(BF16)

    grid_spec = pltpu.PrefetchScalarGridSpec(
        num_scalar_prefetch=1, grid=(4, m // tr),
        in_specs=[pl.BlockSpec((1, tr, n), lambda j, i, c: (2 * j + c[0], i, 0)),
                  pl.BlockSpec((1, tr, n), lambda j, i, c: (j, i, 0))],
        out_specs=(pl.BlockSpec((1, tr, n), lambda j, i, c: (j, i, 0)),
                   pl.BlockSpec((1, tr, n), lambda j, i, c: (j, i, 0))))
    return pl.pallas_call(
        body, name=name, grid_spec=grid_spec,
        out_shape=(jax.ShapeDtypeStruct((4, m, n), F32), jax.ShapeDtypeStruct((4, m, n), BF16)),
        compiler_params=_params(("arbitrary", "arbitrary")),
    )(c_idx, blocks, recv)


def _add_chips(own, recv, name):
    m, n = own.shape
    tr = _add_rows(m, n)

    def body(a_ref, r_ref, o_ref):
        acc = a_ref[...]
        for k in range(3):
            acc = acc + r_ref[k].astype(F32)
        o_ref[...] = acc

    return pl.pallas_call(
        body, name=name, grid=(m // tr,),
        out_shape=jax.ShapeDtypeStruct((m, n), F32),
        in_specs=[pl.BlockSpec((tr, n), lambda i: (i, 0)), pl.BlockSpec((3, tr, n), lambda i: (0, i, 0))],
        out_specs=pl.BlockSpec((tr, n), lambda i: (i, 0)),
        compiler_params=_params(("arbitrary",)),
    )(own, recv)


def _row_spec(cols):
    return pl.BlockSpec((ROW_TILE, cols), lambda i: (i, 0))


def _mod_spec(tiles_per_seq):
    return pl.BlockSpec((1, 8, D_MODEL), lambda i: (i // tiles_per_seq, 0, 0))


def _table_spec(tiles_per_seq):
    return pl.BlockSpec((ROW_TILE, LANES), lambda i: (i % tiles_per_seq, 0))


def _fwd_in(x, mod, ln_g, ln_b, w_in, q_g, kv_g, w_uq, w_ukv, cos_t, sin_a, sin_b, seq):
    rows = x.shape[0]
    tm = ROW_TILE
    tps = seq // tm

    def body(x_ref, mod_ref, g_ref, b_ref, win_ref, qg_ref, kvg_ref, wuq_ref, wukv_ref, cos_ref, sa_ref, sb_ref,
             x0_ref, h_ref, qkv_ref, lat_ref, qp_ref, kp_ref, vm_ref):
        x0, _, _ = _ln_fwd(x_ref[...], g_ref[...], b_ref[...])
        x0_ref[...] = x0
        h = (x0 * (1.0 + mod_ref[0, 1:2, :]) + mod_ref[0, 0:1, :]).astype(BF16)
        h_ref[...] = h
        proj = _dot(h, win_ref[...])
        qkv_ref[:, :SB_W] = (proj[:, :SB_W] * SB_SCALE).astype(BF16)
        qkv_ref[:, SB_W:] = proj[:, SB_W:3 * SB_W].astype(BF16)
        lat_ref[...] = proj[:, 3 * SB_W:3 * SB_W + Q_RANK + KV_RANK]
        cq = proj[:, 3 * SB_W:3 * SB_W + Q_RANK]
        ckv = proj[:, 3 * SB_W + Q_RANK:3 * SB_W + Q_RANK + KV_RANK]
        kr = proj[:, D_IN_PAD - LANES:]
        cos, sa, sb = cos_ref[...], sa_ref[...], sb_ref[...]
        cqn = (cq * lax.rsqrt(_mean(cq * cq) + RMS_EPS) * qg_ref[...]).astype(BF16)
        q_all = _dot(cqn, wuq_ref[...])
        for hd in range(HEADS):
            sl = slice(hd * LANES, (hd + 1) * LANES)
            qp_ref[:, sl] = _rope(q_all[:, sl], cos, sa, sb).astype(BF16)
        ckvn = (ckv * lax.rsqrt(_mean(ckv * ckv) + RMS_EPS) * kvg_ref[...]).astype(BF16)
        kv = _dot(ckvn, wukv_ref[...])
        kr_rot = _rope(kr, cos, sa, sb)
        for hd in range(HEADS):
            sl = slice(hd * LANES, (hd + 1) * LANES)
            kp_ref[:, sl] = (kv[:, sl] + kr_rot).astype(BF16)
        vm_ref[...] = kv[:, HEADS * LANES:].astype(BF16)

    outs = [(D_MODEL, F32), (D_MODEL, BF16), (3 * SB_W, BF16), (Q_RANK + KV_RANK, F32),
            (HEADS * LANES, BF16), (HEADS * LANES, BF16), (MLA_W, BF16)]
    return pl.pallas_call(
        body, name="fwd_in", grid=(rows // tm,),
        out_shape=tuple(jax.ShapeDtypeStruct((rows, n), dt) for n, dt in outs),
        in_specs=[_row_spec(D_MODEL), _mod_spec(tps), _const_spec((1, D_MODEL)), _const_spec((1, D_MODEL)),
                  _const_spec(w_in.shape), _const_spec((1, Q_RANK)), _const_spec((1, KV_RANK)),
                  _const_spec(w_uq.shape), _const_spec(w_ukv.shape),
                  _table_spec(tps), _table_spec(tps), _table_spec(tps)],
        out_specs=tuple(_row_spec(n) for n, _ in outs),
        compiler_params=_params(("arbitrary",)),
    )(x, mod, ln_g, ln_b, w_in, q_g, kv_g, w_uq, w_ukv, cos_t, sin_a, sin_b)


HALF = 512
SHARD = 1024


def _mlp_weight_specs():
    return [pl.BlockSpec((8, HALF, SHARD), lambda i: (0, 0, 0), pipeline_mode=pl.Buffered(1)),
            pl.BlockSpec((8, HALF, SHARD), lambda i: (0, 1, 0), pipeline_mode=pl.Buffered(1))]


def _fwd_out(sb_y, mla_y, x0, mod, w_o, ln_g, ln_b, g_mlp, seq):
    rows = x0.shape[0]
    tm = ROW_TILE
    tps = seq // tm

    def body(sb_ref, ml_ref, x0_ref, mod_ref, wo_ref, g_ref, b_ref, wu_ref, wd_ref,
             mix_ref, y1_ref, h2_ref, u_ref, ff_ref, y2_ref):
        mix = _dot(sb_ref[...], wo_ref[:SB_W, :]) + _dot(ml_ref[...].astype(BF16), wo_ref[SB_W:, :])
        mix_ref[...] = mix
        y1 = ALPHA * x0_ref[...] + (1.0 + mod_ref[0, 2:3, :]) * mix
        y1_ref[...] = y1
        x1, _, _ = _ln_fwd(y1, g_ref[...], b_ref[...])
        h2 = (x1 * (1.0 + mod_ref[0, 4:5, :]) + mod_ref[0, 3:4, :]).astype(BF16)
        h2_ref[...] = h2
        h_lo, h_hi = h2[:, :HALF], h2[:, HALF:]
        ff = jnp.zeros((tm, D_MODEL), F32)
        for chip in range(4):
            u = _dot(h_lo, wu_ref[2 * chip]) + _dot(h_hi, wu_ref[2 * chip + 1])
            u_ref[:, chip * SHARD:(chip + 1) * SHARD] = u.astype(BF16)
            act = jnp.square(jnp.maximum(u, 0.0)).astype(BF16)
            ff = ff + _dot(act[:, :HALF], wd_ref[2 * chip]) + _dot(act[:, HALF:], wd_ref[2 * chip + 1])
        ff_ref[...] = ff
        y2_ref[...] = ALPHA * x1 + (1.0 + mod_ref[0, 5:6, :]) * ff

    outs = [(D_MODEL, F32), (D_MODEL, F32), (D_MODEL, BF16), (D_FF, BF16), (D_MODEL, F32), (D_MODEL, F32)]
    return pl.pallas_call(
        body, name="fwd_out", grid=(rows // tm,),
        out_shape=tuple(jax.ShapeDtypeStruct((rows, n), dt) for n, dt in outs),
        in_specs=[_row_spec(SB_W), _row_spec(MLA_W), _row_spec(D_MODEL), _mod_spec(tps), _const_spec(w_o.shape),
                  _const_spec((1, D_MODEL)), _const_spec((1, D_MODEL))] + _mlp_weight_specs(),
        out_specs=tuple(_row_spec(n) for n, _ in outs),
        compiler_params=_params(("arbitrary",)),
    )(sb_y, mla_y, x0, mod, w_o, ln_g, ln_b, g_mlp, g_mlp)


def _acc_spec(rows=8, cols=D_MODEL):
    return pl.BlockSpec((rows, cols), lambda i: (0, 0))


def _bwd_out(y2, tgt, ff, u, y1, mix, mod, ln2_g, ln2_b, ln1_g, ln1_b, g_mlp, w_o, seq):
    rows = y2.shape[0]
    nb = rows // seq
    tm = ROW_TILE
    tps = seq // tm

    def body(y2_ref, t_ref, ff_ref, u_ref, y1_ref, mix_ref, mod_ref, g2_ref, b2_ref, g_ref, b_ref, wu_ref, wd_ref,
             wo_ref, dy1_ref, dmix_ref, do_ref, dff_ref, du_ref, acc_ref, dmod_ref):
        i = pl.program_id(0)

        @pl.when(i == 0)
        def _():
            acc_ref[...] = jnp.zeros_like(acc_ref)

        @pl.when(i % tps == 0)
        def _():
            dmod_ref[...] = jnp.zeros_like(dmod_ref)

        g2 = g2_ref[...]
        x2, xhat2, rstd2 = _ln_fwd(y2_ref[...], g2, b2_ref[...])
        err = x2 - t_ref[...]
        dx2 = err * (1.0 / D_MODEL)
        acc_ref[0:1, :] += _rowsum(dx2 * xhat2)
        acc_ref[1:2, :] += _rowsum(dx2)
        acc_ref[2:3, :] += _rowsum(err * err) * (0.5 / D_MODEL)
        dy2 = _ln_bwd(dx2, xhat2, rstd2, g2)
        dmod_ref[0, 5:6, :] += _rowsum(dy2 * ff_ref[...])
        dff = ((1.0 + mod_ref[0, 5:6, :]) * dy2).astype(BF16)
        dff_ref[...] = dff
        for blk in range(8):
            cols = slice(blk * HALF, (blk + 1) * HALF)
            da = _dot(dff, wd_ref[blk], NT)
            du_ref[:, cols] = (da * (2.0 * jnp.maximum(u_ref[:, cols].astype(F32), 0.0))).astype(BF16)

        g = g_ref[...]
        x1, xhat, rstd = _ln_fwd(y1_ref[...], g, b_ref[...])
        halves = []
        for half in range(2):
            acc = jnp.zeros((tm, HALF), F32)
            for chip in range(4):
                acc = acc + _dot(du_ref[:, chip * SHARD:(chip + 1) * SHARD], wu_ref[2 * chip + half], NT)
            halves.append(acc)
        dh2 = jnp.concatenate(halves, axis=1)
        dmod_ref[0, 3:4, :] += _rowsum(dh2)
        dmod_ref[0, 4:5, :] += _rowsum(dh2 * x1)
        dx1 = ALPHA * dy2 + dh2 * (1.0 + mod_ref[0, 4:5, :])
        acc_ref[3:4, :] += _rowsum(dx1 * xhat)
        acc_ref[4:5, :] += _rowsum(dx1)
        dy1 = _ln_bwd(dx1, xhat, rstd, g)
        dy1_ref[...] = dy1
        dmod_ref[0, 2:3, :] += _rowsum(dy1 * mix_ref[...])
        dmix = ((1.0 + mod_ref[0, 2:3, :]) * dy1).astype(BF16)
        dmix_ref[...] = dmix
        do_ref[...] = _dot(dmix, wo_ref[...], NT)

    outs = [(D_MODEL, F32), (D_MODEL, BF16), (D_MODEL, F32), (D_MODEL, BF16), (D_FF, BF16)]
    return pl.pallas_call(
        body, name="bwd_out", grid=(rows // tm,),
        out_shape=tuple(jax.ShapeDtypeStruct((rows, n), dt) for n, dt in outs)
        + (jax.ShapeDtypeStruct((8, D_MODEL), F32), jax.ShapeDtypeStruct((nb, 8, D_MODEL), F32)),
        in_specs=[_row_spec(D_MODEL), _row_spec(D_MODEL), _row_spec(D_MODEL), _row_spec(D_FF), _row_spec(D_MODEL),
                  _row_spec(D_MODEL), _mod_spec(tps), _const_spec((1, D_MODEL)), _const_spec((1, D_MODEL)),
                  _const_spec((1, D_MODEL)), _const_spec((1, D_MODEL))] + _mlp_weight_specs()
        + [_const_spec(w_o.shape)],
        out_specs=tuple(_row_spec(n) for n, _ in outs) + (_acc_spec(), _mod_spec(tps)),
        compiler_params=_params(("arbitrary",)),
    )(y2, tgt, ff, u, y1, mix, mod, ln2_g, ln2_b, ln1_g, ln1_b, g_mlp, g_mlp, w_o)


def _bwd_in(dqp, dkp, dvm, dq_sb, dk_sb, dv_sb, lat, x, dy1, mod, ln_g, ln_b, w_in, q_g, kv_g, w_uq, w_ukv,
            cos_t, sin_a, sin_b, seq):
    rows = x.shape[0]
    nb = rows // seq
    tm = ROW_TILE
    tps = seq // tm
    n_lat = Q_RANK + KV_RANK

    def body(dqp_ref, dkp_ref, dvm_ref, dqs_ref, dks_ref, dvs_ref, lat_ref, x_ref, dy1_ref, mod_ref,
             g_ref, b_ref, win_ref, qg_ref, kvg_ref, wuq_ref, wukv_ref, cos_ref, sa_ref, sb_ref,
             dx_ref, dproj_ref, dqall_ref, dkv_ref, latn_ref, acc_ref, accl_ref, dmod_ref):
        i = pl.program_id(0)

        @pl.when(i == 0)
        def _():
            acc_ref[...] = jnp.zeros_like(acc_ref)
            accl_ref[...] = jnp.zeros_like(accl_ref)

        @pl.when(i % tps == 0)
        def _():
            dmod_ref[...] = jnp.zeros_like(dmod_ref)

        cos, sa, sb = cos_ref[...], sa_ref[...], sb_ref[...]
        lane = lax.broadcasted_iota(jnp.int32, (tm, LANES), 1)
        for hd in range(HEADS):
            sl = slice(hd * LANES, (hd + 1) * LANES)
            dqall_ref[:, sl] = _rope_t(dqp_ref[:, sl], cos, sa, sb).astype(BF16)
        dcqn = _dot(dqall_ref[...], wuq_ref[...], NT)
        cq = lat_ref[:, :Q_RANK]
        qg = qg_ref[...]
        rq = lax.rsqrt(_mean(cq * cq) + RMS_EPS)
        cqn = cq * rq
        latn_ref[:, :Q_RANK] = (cqn * qg).astype(BF16)
        accl_ref[0:1, :Q_RANK] += _rowsum(dcqn * cqn)
        dqg = dcqn * qg
        dcq = rq * (dqg - cqn * _mean(dqg * cqn))
        dkr = jnp.zeros((tm, LANES), F32)
        for hd in range(HEADS):
            sl = slice(hd * LANES, (hd + 1) * LANES)
            dk = dkp_ref[:, sl]
            dkr = dkr + dk
            dkv_ref[:, sl] = jnp.where(lane < NOPE, dk, 0.0).astype(BF16)
        dkv_ref[:, HEADS * LANES:] = dvm_ref[...].astype(BF16)
        dckvn = _dot(dkv_ref[...], wukv_ref[...], NT)
        ckv = lat_ref[:, Q_RANK:]
        kvg = kvg_ref[...]
        rkv = lax.rsqrt(_mean(ckv * ckv) + RMS_EPS)
        ckvn = ckv * rkv
        latn_ref[:, Q_RANK:] = (ckvn * kvg).astype(BF16)
        accl_ref[1:2, :KV_RANK] += _rowsum(dckvn * ckvn)
        dkg = dckvn * kvg
        dckv = rkv * (dkg - ckvn * _mean(dkg * ckvn))
        dkr = _rope_t(jnp.where(lane >= NOPE, dkr, 0.0), cos, sa, sb)
        dproj_ref[:, :SB_W] = dqs_ref[...]
        dproj_ref[:, SB_W:2 * SB_W] = dks_ref[...].astype(BF16)
        dproj_ref[:, 2 * SB_W:3 * SB_W] = dvs_ref[...].astype(BF16)
        dproj_ref[:, 3 * SB_W:3 * SB_W + Q_RANK] = dcq.astype(BF16)
        dproj_ref[:, 3 * SB_W + Q_RANK:3 * SB_W + n_lat] = dckv.astype(BF16)
        dproj_ref[:, D_IN_PAD - LANES:] = dkr.astype(BF16)
        dh = _dot(dproj_ref[...], win_ref[...], NT)
        g = g_ref[...]
        x0, xhat, rstd = _ln_fwd(x_ref[...], g, b_ref[...])
        dmod_ref[0, 0:1, :] += _rowsum(dh)
        dmod_ref[0, 1:2, :] += _rowsum(dh * x0)
        dx0 = ALPHA * dy1_ref[...] + dh * (1.0 + mod_ref[0, 1:2, :])
        acc_ref[0:1, :] += _rowsum(dx0 * xhat)
        acc_ref[1:2, :] += _rowsum(dx0)
        dx_ref[...] = _ln_bwd(dx0, xhat, rstd, g)

    outs = [(D_MODEL, F32), (D_IN_PAD, BF16), (HEADS * LANES, BF16), (HEADS * LANES + MLA_W, BF16), (n_lat, BF16)]
    return pl.pallas_call(
        body, name="bwd_in", grid=(rows // tm,),
        out_shape=tuple(jax.ShapeDtypeStruct((rows, n), dt) for n, dt in outs)
        + (jax.ShapeDtypeStruct((8, D_MODEL), F32), jax.ShapeDtypeStruct((8, Q_RANK), F32),
           jax.ShapeDtypeStruct((nb, 8, D_MODEL), F32)),
        in_specs=[_row_spec(HEADS * LANES), _row_spec(HEADS * LANES), _row_spec(MLA_W),
                  _row_spec(SB_W), _row_spec(SB_W), _row_spec(SB_W), _row_spec(n_lat),
                  _row_spec(D_MODEL), _row_spec(D_MODEL), _mod_spec(tps),
                  _const_spec((1, D_MODEL)), _const_spec((1, D_MODEL)), _const_spec(w_in.shape),
                  _const_spec((1, Q_RANK)), _const_spec((1, KV_RANK)), _const_spec(w_uq.shape),
                  _const_spec(w_ukv.shape), _table_spec(tps), _table_spec(tps), _table_spec(tps)],
        out_specs=tuple(_row_spec(n) for n, _ in outs) + (_acc_spec(), _acc_spec(8, Q_RANK), _mod_spec(tps)),
        compiler_params=_params(("arbitrary",)),
    )(dqp, dkp, dvm, dq_sb, dk_sb, dv_sb, lat, x, dy1, mod, ln_g, ln_b, w_in, q_g, kv_g, w_uq, w_ukv,
      cos_t, sin_a, sin_b)


def _wgrad(a, b, name, pre=None, tm=512, tn=1024, tk=2048):
    rows, m = a.shape
    n = b.shape[1]
    tm, tn, tk = min(tm, m), min(tn, n), min(tk, rows)
    if m % tm:
        tm = m
    if n % tn:
        tn = n

    def body(a_ref, b_ref, o_ref):
        @pl.when(pl.program_id(2) == 0)
        def _():
            o_ref[...] = jnp.zeros_like(o_ref)

        av = a_ref[...]
        if pre == "relu2":
            av = jnp.square(jnp.maximum(av.astype(F32), 0.0))
        o_ref[...] += _dot(av.astype(BF16), b_ref[...].astype(BF16), TN)

    return pl.pallas_call(
        body, name=name, grid=(m // tm, n // tn, rows // tk),
        out_shape=jax.ShapeDtypeStruct((m, n), F32),
        in_specs=[pl.BlockSpec((tk, tm), lambda i, j, k: (k, i)), pl.BlockSpec((tk, tn), lambda i, j, k: (k, j))],
        out_specs=pl.BlockSpec((tm, tn), lambda i, j, k: (i, j)),
        compiler_params=_params(("arbitrary", "arbitrary", "arbitrary")),
    )(a, b)


def _wgrad_packed(a, b, name, block_of, row_block, split=1, pre=None, into=None, tk=2048):
    rows, m = a.shape
    n = b.shape[1]
    tm = HALF
    part = tm // split
    tk = min(tk, rows)
    shape = jax.ShapeDtypeStruct((8, GROUP_MLP[0], PACK_COLS), F32)

    def body(a_ref, b_ref, *rest):
        o_ref = rest[-1]

        @pl.when(pl.program_id(2) == 0)
        def _():
            o_ref[...] = jnp.zeros_like(o_ref)

        av = a_ref[...]
        if pre == "relu2":
            av = jnp.square(jnp.maximum(av.astype(F32), 0.0))
        prod = _dot(av.astype(BF16), b_ref[...].astype(BF16), TN)
        for s in range(split):
            o_ref[s] += prod[s * part:(s + 1) * part]

    in_specs = [pl.BlockSpec((tk, tm), lambda i, j, k: (k, i)), pl.BlockSpec((tk, SHARD), lambda i, j, k: (k, j))]
    operands = [a, b]
    if into is not None:
        in_specs.append(pl.BlockSpec(memory_space=pl.ANY))
        operands.append(into)
    return pl.pallas_call(
        body, name=name, grid=(m // tm, n // SHARD, rows // tk), out_shape=shape,
        in_specs=in_specs,
        out_specs=pl.BlockSpec((split, part, SHARD), lambda i, j, k: (block_of(i, j), row_block, 0)),
        input_output_aliases={} if into is None else {2: 0},
        compiler_params=_params(("arbitrary", "arbitrary", "arbitrary")),
    )(*operands)


def _pair(pp):
    return slice(pp * LANES, (pp + 1) * LANES)


def _head_mask(lane, hh):
    return jnp.where((lane >= 64) if hh else (lane < 64), 1.0, 0.0).astype(BF16)


def _tri(t, kind):
    s = lax.broadcasted_iota(jnp.int32, (t, t), 0)
    j = lax.broadcasted_iota(jnp.int32, (t, t), 1)
    one = jnp.where(j > s if kind == "later" else j < s, 1.0, 0.0).astype(BF16)
    return jnp.concatenate([one, one], axis=1)


def _split_dot(tri2, v):
    hi = v.astype(BF16)
    lo = (v - hi.astype(F32)).astype(BF16)
    return _dot(tri2, jnp.concatenate([hi, lo], axis=0))


def _sb_logits(z, valid):
    log_keep = -(jnp.maximum(z, 0.0) + jnp.log(1.0 + jnp.exp(-jnp.abs(z))))
    log_beta = z + log_keep
    if valid is not None:
        log_keep = jnp.where(valid, log_keep, 0.0)
    return log_keep, log_beta


def _attention_call(body, ex, name, grid, operands, in_specs, out_shapes, out_specs, scratch=()):
    n_in, n_out = len(operands), len(out_shapes)
    total = grid[0] * grid[1] * grid[2]
    any_spec = pl.BlockSpec(memory_space=pl.ANY)

    def carrier(*refs):
        ins, outs, (start, middle, finish) = _carried(ex, refs, n_in, n_out, len(scratch))
        step = (pl.program_id(0) * grid[1] + pl.program_id(1)) * grid[2] + pl.program_id(2)
        pl.when(step == 0)(start)
        pl.when(step == total // 2)(middle)
        body(*ins, *outs)
        pl.when(step == total - 1)(finish)

    carried = ex is not None
    return pl.pallas_call(
        carrier if carried else body, name=name, grid=grid,
        out_shape=tuple(out_shapes) + ((ex.out_shape,) if carried else ()),
        in_specs=list(in_specs) + ([any_spec] if carried else []),
        out_specs=tuple(out_specs) + ((any_spec,) if carried else ()),
        scratch_shapes=list(scratch) + (ex.scratch if carried else []),
        compiler_params=_params(("arbitrary", "arbitrary", "arbitrary")),
    )(*operands, *([ex.operand] if carried else []))


def _sb_fwd(qkv, seq, ex=None):
    rows = qkv.shape[0]
    nb = rows // seq
    t = min(ATTN_TILE, seq)
    nq = seq // t
    assert nq <= CAR_SLOTS, (seq, t)
    ap = ATTN_PAIRS
    width = ap * LANES
    groups = SB_W // width
    hds = [(pp, hh) for pp in range(ap) for hh in range(2)]

    def body(q_ref, k_ref, v_ref, tri_ref, o_ref, car_ref, acc_ref):
        i = pl.program_id(2)
        lane = lax.broadcasted_iota(jnp.int32, (t, LANES), 1)
        key = lax.broadcasted_iota(jnp.int32, (t, t), 0)
        qry = lax.broadcasted_iota(jnp.int32, (t, t), 1)
        strict = key < qry
        tri = tri_ref[...]
        masks = [_head_mask(lane, hh) for hh in range(2)]
        qms = [q_ref[:, _pair(pp)] * masks[hh] for pp, hh in hds]
        acc_ref[...] = jnp.zeros_like(acc_ref)
        car_ref[...] = jnp.zeros_like(car_ref)

        def step(kb, c_sums, valid):
            start = pl.multiple_of(kb * t, t)
            kss = [k_ref[pl.ds(start, t), _pair(pp)] for pp in range(ap)]
            vss = [v_ref[pl.ds(start, t), _pair(pp)] for pp in range(ap)]
            zs = [_dot(kss[pp], qms[n], NT) for n, (pp, _) in enumerate(hds)]
            logs = [_sb_logits(z, valid) for z in zs]
            sufs = [_split_dot(tri, lg[0]) for lg in logs]
            new_sums = []
            for n, (pp, hh) in enumerate(hds):
                log_keep, log_beta = logs[n]
                w = jnp.exp(log_beta + sufs[n] + c_sums[n])
                if valid is not None:
                    w = jnp.where(valid, w, 0.0)
                acc_ref[pp] += _dot(vss[pp] * masks[hh], w.astype(BF16), TN)
                car_ref[0, pl.ds(n * CAR_SLOTS + kb, 1), :] = c_sums[n]
                new_sums.append(c_sums[n] + jnp.sum(log_keep, axis=0, keepdims=True))
            return tuple(new_sums)

        c_sums = step(i, tuple(jnp.zeros((1, t), F32) for _ in hds), strict)
        lax.fori_loop(0, i, lambda j, cr: step(i - 1 - j, cr, None), c_sums)
        for pp in range(ap):
            o_ref[:, _pair(pp)] = acc_ref[pp].T.astype(BF16)

    qspec = pl.BlockSpec((t, width), lambda b, p, i: (b * nq + i, p))
    car_rows = len(hds) * CAR_SLOTS
    return _attention_call(
        body, ex, "sb_fwd", (nb, groups, nq),
        [qkv, qkv, qkv, _tri(t, "later")],
        [qspec,
         pl.BlockSpec((seq, width), lambda b, p, i: (b, groups + p)),
         pl.BlockSpec((seq, width), lambda b, p, i: (b, 2 * groups + p)),
         _const_spec((t, 2 * t))],
        [jax.ShapeDtypeStruct((rows, SB_W), BF16), jax.ShapeDtypeStruct((nb * nq, HEADS * CAR_SLOTS, t), F32)],
        [qspec, pl.BlockSpec((1, car_rows, t), lambda b, p, i: (b * nq + i, p, 0))],
        scratch=[pltpu.VMEM((ap, LANES, t), F32)])


def _sb_bwd(qkv, d_out, cars, seq, ex=None):
    rows = qkv.shape[0]
    nb = rows // seq
    t = min(ATTN_TILE, seq)
    nq = seq // t
    ap = ATTN_PAIRS
    width = ap * LANES
    groups = SB_W // width
    hds = [(pp, hh) for pp in range(ap) for hh in range(2)]

    def body(q_ref, k_ref, v_ref, do_ref, car_ref, tri_ref, pre_ref, dq_ref, dk_ref, dv_ref, dq_acc):
        i = pl.program_id(2)

        @pl.when(i == 0)
        def _():
            dk_ref[...] = jnp.zeros_like(dk_ref)
            dv_ref[...] = jnp.zeros_like(dv_ref)

        lane = lax.broadcasted_iota(jnp.int32, (t, LANES), 1)
        key = lax.broadcasted_iota(jnp.int32, (t, t), 0)
        qry = lax.broadcasted_iota(jnp.int32, (t, t), 1)
        strict = key < qry
        tri, pre = tri_ref[...], pre_ref[...]
        masks = [_head_mask(lane, hh) for hh in range(2)]
        qms = [q_ref[:, _pair(pp)] * masks[hh] for pp, hh in hds]
        doms = [do_ref[:, _pair(pp)].astype(BF16) * masks[hh] for pp, hh in hds]
        dq_acc[...] = jnp.zeros_like(dq_acc)

        def step(kb, g_pres, valid):
            start = pl.multiple_of(kb * t, t)
            kss = [k_ref[pl.ds(start, t), _pair(pp)] for pp in range(ap)]
            vss = [v_ref[pl.ds(start, t), _pair(pp)] for pp in range(ap)]
            zs = [_dot(kss[pp], qms[n], NT) for n, (pp, _) in enumerate(hds)]
            dws = [_dot(vss[pp], doms[n], NT) for n, (pp, _) in enumerate(hds)]
            logs = [_sb_logits(z, valid) for z in zs]
            sufs = [_split_dot(tri, lg[0]) for lg in logs]
            ws, gs = [], []
            for n in range(len(hds)):
                c_sum = car_ref[0, pl.ds(n * CAR_SLOTS + kb, 1), :]
                w = jnp.exp(logs[n][1] + sufs[n] + c_sum)
                if valid is not None:
                    w = jnp.where(valid, w, 0.0)
                ws.append(w)
                gs.append(dws[n] * w)
            befores = [g_pres[n] + _split_dot(pre, gs[n]) for n in range(len(hds))]
            for pp in range(ap):
                a, b = 2 * pp, 2 * pp + 1
                dv_ref[pl.ds(start, t), _pair(pp)] += _dot(ws[a].astype(BF16), doms[a]) + _dot(ws[b].astype(BF16), doms[b])
            dzbs = []
            for n in range(len(hds)):
                beta = jnp.exp(logs[n][1])
                dz = gs[n] * (1.0 - beta) - beta * befores[n]
                if valid is not None:
                    dz = jnp.where(valid, dz, 0.0)
                dzbs.append(dz.astype(BF16))
            for pp in range(ap):
                a, b = 2 * pp, 2 * pp + 1
                dq_acc[pp] += _dot(dzbs[a], kss[pp] * masks[0], TN) + _dot(dzbs[b], kss[pp] * masks[1], TN)
                dk_ref[pl.ds(start, t), _pair(pp)] += _dot(dzbs[a], qms[a]) + _dot(dzbs[b], qms[b])
            return tuple(g_pres[n] + jnp.sum(gs[n], axis=0, keepdims=True) for n in range(len(hds)))

        g_pres = lax.fori_loop(0, i, lambda kb, cr: step(kb, cr, None), tuple(jnp.zeros((1, t), F32) for _ in hds))
        step(i, g_pres, strict)
        for pp in range(ap):
            dq_ref[:, _pair(pp)] = (dq_acc[pp] * SB_SCALE).astype(BF16)

    qspec = pl.BlockSpec((t, width), lambda b, p, i: (b * nq + i, p))
    kspec_out = pl.BlockSpec((seq, width), lambda b, p, i: (b, p))
    car_rows = len(hds) * CAR_SLOTS
    return _attention_call(
        body, ex, "sb_bwd", (nb, groups, nq),
        [qkv, qkv, qkv, d_out, cars, _tri(t, "later"), _tri(t, "earlier")],
        [qspec,
         pl.BlockSpec((seq, width), lambda b, p, i: (b, groups + p)),
         pl.BlockSpec((seq, width), lambda b, p, i: (b, 2 * groups + p)),
         qspec, pl.BlockSpec((1, car_rows, t), lambda b, p, i: (b * nq + i, p, 0)),
         _const_spec((t, 2 * t)), _const_spec((t, 2 * t))],
        [jax.ShapeDtypeStruct((rows, SB_W), BF16), jax.ShapeDtypeStruct((rows, SB_W), F32),
         jax.ShapeDtypeStruct((rows, SB_W), F32)],
        [qspec, kspec_out, kspec_out],
        scratch=[pltpu.VMEM((ap, t, LANES), F32)])


def _mla_scores(qh, ks, allowed):
    s = _dot(qh, ks, NT) * MLA_SCALE
    if allowed is not None:
        s = jnp.where(allowed, s, jnp.finfo(F32).min)
    return s


def _mla_fwd(qp, kp, vm, seq, ex=None, chunk=64):
    rows = qp.shape[0]
    nb = rows // seq
    t = min(ATTN_TILE, seq)
    nq = seq // t
    shift = int(math.log2(chunk))
    ap = ATTN_PAIRS
    width = ap * LANES
    groups = MLA_W // width
    hds = [(pp, hh) for pp in range(ap) for hh in range(2)]

    def body(q_ref, k_ref, v_ref, o_ref, lse_ref, acc_ref):
        i = pl.program_id(2)
        lane = lax.broadcasted_iota(jnp.int32, (t, LANES), 1)
        key = lax.broadcasted_iota(jnp.int32, (t, t), 0)
        qry = lax.broadcasted_iota(jnp.int32, (t, t), 1)
        allowed_diag = jnp.right_shift(key, shift) <= jnp.right_shift(qry, shift)
        masks = [_head_mask(lane, hh) for hh in range(2)]
        qhs = [q_ref[:, _pair(n)] for n in range(len(hds))]
        acc_ref[...] = jnp.zeros_like(acc_ref)

        def step(kb, carry, allowed):
            start = pl.multiple_of(kb * t, t)
            vss = [v_ref[pl.ds(start, t), _pair(pp)] for pp in range(ap)]
            scores = [_mla_scores(k_ref[pl.ds(start, t), _pair(n)], qhs[n], allowed) for n in range(len(hds))]
            new = []
            for n, (pp, hh) in enumerate(hds):
                m_run, l_run = carry[n]
                s = scores[n]
                m_new = jnp.maximum(m_run, jnp.max(s, axis=0, keepdims=True))
                p = jnp.exp(s - m_new)
                scale = jnp.exp(m_run - m_new)
                l_run = scale * l_run + jnp.sum(p, axis=0, keepdims=True)
                acc_ref[n] = scale * acc_ref[n] + _dot(vss[pp] * masks[hh], p.astype(BF16), TN)
                new.append((m_new, l_run))
            return tuple(new)

        init = (jnp.full((1, t), jnp.finfo(F32).min, F32), jnp.zeros((1, t), F32))
        carry = step(i, tuple(init for _ in hds), allowed_diag)
        carry = lax.fori_loop(0, i, lambda kb, cr: step(kb, cr, None), carry)
        lse_rows = []
        for pp in range(ap):
            out_t = jnp.zeros((LANES, t), F32)
            for hh in range(2):
                m_run, l_run = carry[2 * pp + hh]
                out_t = out_t + acc_ref[2 * pp + hh] / l_run
                lse_rows.append(m_run + jnp.log(l_run))
            o_ref[:, _pair(pp)] = out_t.T
        lse_t = jnp.concatenate(lse_rows + [jnp.zeros((LANES - len(hds), t), F32)], axis=0)
        lse_ref[...] = jnp.zeros_like(lse_ref)
        lse_ref[:, _pair(0)] = lse_t.T

    ospec = pl.BlockSpec((t, width), lambda b, p, i: (b * nq + i, p))
    return _attention_call(
        body, ex, "mla_fwd", (nb, groups, nq), [qp, kp, vm],
        [pl.BlockSpec((t, 2 * width), lambda b, p, i: (b * nq + i, p)),
         pl.BlockSpec((seq, 2 * width), lambda b, p, i: (b, p)),
         pl.BlockSpec((seq, width), lambda b, p, i: (b, p))],
        [jax.ShapeDtypeStruct((rows, MLA_W), F32), jax.ShapeDtypeStruct((rows, MLA_W), F32)],
        [ospec, ospec], scratch=[pltpu.VMEM((len(hds), LANES, t), F32)])


def _mla_bwd(qp, kp, vm, d_out, out, lse, seq, ex=None, chunk=64):
    rows = qp.shape[0]
    nb = rows // seq
    t = min(ATTN_TILE, seq)
    nq = seq // t
    shift = int(math.log2(chunk))
    ap = ATTN_PAIRS
    width = ap * LANES
    groups = MLA_W // width
    hds = [(pp, hh) for pp in range(ap) for hh in range(2)]
    nh = len(hds)

    def body(q_ref, k_ref, v_ref, do_ref, o_ref, lse_ref, dq_ref, dk_ref, dv_ref):
        i = pl.program_id(2)

        @pl.when(i == 0)
        def _():
            dk_ref[...] = jnp.zeros_like(dk_ref)
            dv_ref[...] = jnp.zeros_like(dv_ref)

        lane = lax.broadcasted_iota(jnp.int32, (t, LANES), 1)
        key = lax.broadcasted_iota(jnp.int32, (t, t), 0)
        qry = lax.broadcasted_iota(jnp.int32, (t, t), 1)
        allowed_diag = jnp.right_shift(key, shift) <= jnp.right_shift(qry, shift)
        qhs = [q_ref[:, _pair(n)] for n in range(nh)]
        lse_t = lse_ref[:, _pair(0)].T
        doms, deltas, lse_hs = [], [], []
        for pp in range(ap):
            do = do_ref[:, _pair(pp)]
            d_o_t = (do * o_ref[:, _pair(pp)]).T
            for hh in range(2):
                doms.append(do.astype(BF16) * _head_mask(lane, hh))
                deltas.append(jnp.sum(d_o_t[hh * 64:(hh + 1) * 64], axis=0, keepdims=True))
                lse_hs.append(lse_t[2 * pp + hh:2 * pp + hh + 1])

        dq_ref[...] = jnp.zeros_like(dq_ref)

        def step(kb, allowed):
            start = pl.multiple_of(kb * t, t)
            vss = [v_ref[pl.ds(start, t), _pair(pp)] for pp in range(ap)]
            kss = [k_ref[pl.ds(start, t), _pair(n)] for n in range(nh)]
            scores = [_mla_scores(kss[n], qhs[n], allowed) for n in range(nh)]
            dps = [_dot(vss[pp], doms[n], NT) for n, (pp, _) in enumerate(hds)]
            ps = [jnp.exp(scores[n] - lse_hs[n]) for n in range(nh)]
            dss = [(ps[n] * (dps[n] - deltas[n]) * MLA_SCALE).astype(BF16) for n in range(nh)]
            for pp in range(ap):
                a, b = 2 * pp, 2 * pp + 1
                dv_ref[pl.ds(start, t), _pair(pp)] += _dot(ps[a].astype(BF16), doms[a]) + _dot(ps[b].astype(BF16), doms[b])
            for n in range(nh):
                dk_ref[pl.ds(start, t), _pair(n)] += _dot(dss[n], qhs[n])
                dq_ref[:, _pair(n)] += _dot(dss[n], kss[n], TN)

        def off_diagonal(kb, nothing):
            step(kb, None)
            return nothing

        lax.fori_loop(0, i, off_diagonal, 0)
        step(i, allowed_diag)

    ospec = pl.BlockSpec((t, width), lambda b, p, i: (b * nq + i, p))
    return _attention_call(
        body, ex, "mla_bwd", (nb, groups, nq), [qp, kp, vm, d_out, out, lse],
        [pl.BlockSpec((t, 2 * width), lambda b, p, i: (b * nq + i, p)),
         pl.BlockSpec((seq, 2 * width), lambda b, p, i: (b, p)),
         pl.BlockSpec((seq, width), lambda b, p, i: (b, p)),
         pl.BlockSpec((t, width), lambda b, p, i: (b * nq + i, groups + p)),
         ospec, ospec],
        [jax.ShapeDtypeStruct((rows, HEADS * LANES), F32), jax.ShapeDtypeStruct((rows, HEADS * LANES), F32),
         jax.ShapeDtypeStruct((rows, MLA_W), F32)],
        [pl.BlockSpec((t, 2 * width), lambda b, p, i: (b * nq + i, p)),
         pl.BlockSpec((seq, 2 * width), lambda b, p, i: (b, p)),
         pl.BlockSpec((seq, width), lambda b, p, i: (b, p))])


PACK_COLS = 1024
PACK_ALIGN = 16
GROUP_IN = (384, ((1024, 552, 1), (384, 192, 1), (256, 256, 1)))
GROUP_MLP = (1152, ((1024, 1024, 1), (1024, 1024, 0), (256, 1024, 0)))


def _pack_rows(r, c):
    return (r // 2) * c // PACK_COLS


def _slot_rows(r, c):
    return -(-_pack_rows(r, c) // PACK_ALIGN) * PACK_ALIGN


def _join_slots(parts, group):
    total, weights = group
    padded = [jnp.pad(p, ((0, 0), (0, _slot_rows(r, c) - p.shape[1]), (0, 0))) for p, (r, c, _) in zip(parts, weights)]
    used = sum(_slot_rows(r, c) for r, c, _ in weights)
    if total > used:
        padded.append(jnp.zeros((parts[0].shape[0], total - used, PACK_COLS), parts[0].dtype))
    return jnp.concatenate(padded, axis=1)


def _split_slots(packed, group):
    out, at = [], 0
    for r, c, _ in group[1]:
        out.append(packed[:, at:at + _pack_rows(r, c), :])
        at += _slot_rows(r, c)
    return out


def _pack_halves(shards, group):
    return _join_slots([s.reshape(2, _pack_rows(r, c), PACK_COLS) for s, (r, c, _) in zip(shards, group[1])], group)


def _unpack_half(packed, group):
    return [p.reshape(r // 2, c) for p, (r, c, _) in zip(_split_slots(packed[None], group), group[1])]


def _unpack_full(gathered, group):
    out = []
    for p, (r, c, axis) in zip(_split_slots(gathered, group), group[1]):
        shards = p.reshape(4, r, c)
        out.append(shards.reshape(4 * r, c) if axis == 0 else jnp.moveaxis(shards, 0, 1).reshape(r, 4 * c))
    return out


def _pack_full(grads, group):
    parts = []
    for gr, (r, c, axis) in zip(grads, group[1]):
        shards = gr.reshape(4, r, c) if axis == 0 else jnp.moveaxis(gr.reshape(r, 4, c), 1, 0)
        parts.append(shards.reshape(8, _pack_rows(r, c), PACK_COLS))
    return _join_slots(parts, group)


def _pad_w_in(w_in):
    z = jnp.zeros((D_MODEL, 1), w_in.dtype)
    return jnp.concatenate([w_in[:, :2176], jnp.tile(z, (1, 64)), w_in[:, 2176:], jnp.tile(z, (1, 32))], axis=1)


def _unpad_w_in(g):
    return jnp.concatenate([g[:, :2176], g[:, 2240:2272]], axis=1)


def _pad_heads(w, used):
    k = w.shape[0]
    w3 = w.reshape(k, HEADS, used)
    return jnp.pad(w3, ((0, 0), (0, 0), (0, LANES - used))).reshape(k, HEADS * LANES)


def _unpad_heads(g, used):
    k = g.shape[0]
    return g.reshape(k, HEADS, LANES)[:, :, :used].reshape(k, HEADS * used)


def _rope_tables(seq):
    inv_freq = 1.0 / (ROPE_BASE ** (jnp.arange(0, ROPE, 2, dtype=F32) / ROPE))
    ang = jnp.arange(seq, dtype=F32)[:, None] * inv_freq[None, :]
    cos, sin = jnp.cos(ang), jnp.sin(ang)
    one, zero = jnp.ones((seq, NOPE), F32), jnp.zeros((seq, NOPE), F32)
    z16, z32 = jnp.zeros((seq, 16), F32), jnp.zeros((seq, 32), F32)
    cos_t = jnp.concatenate([one, cos, cos, jnp.ones((seq, 32), F32)], axis=1)
    sin_a = jnp.concatenate([zero, -sin, z16, z32], axis=1)
    sin_b = jnp.concatenate([zero, z16, sin, z32], axis=1)
    return cos_t, sin_a, sin_b


SMALL = (("ln_in_g", 1024), ("ln_in_b", 1024), ("b_ada", 6144), ("q_norm_g", 384), ("kv_norm_g", 256),
         ("ln1_g", 1024), ("ln1_b", 1024), ("ln2_g", 1024), ("ln2_b", 1024))
SUBLANES = 8
SMALL_SLOTS = [-(-n // LANES // SUBLANES) * SUBLANES for _, n in SMALL]
SMALL_AT = [sum(SMALL_SLOTS[:p]) for p in range(len(SMALL))]
SMALL_ROWS = sum(SMALL_SLOTS)


def _pack_small(vals):
    parts = []
    for v, slot in zip(vals, SMALL_SLOTS):
        rows = v.reshape(-1, LANES)
        parts.append(jnp.pad(rows, ((0, slot - rows.shape[0]), (0, 0))))
    return jnp.concatenate(parts, axis=0)


def kernel(x, c, ln_in_g, ln_in_b, w_ada, b_ada, w_in, q_norm_g, kv_norm_g, w_uq, w_ukv, w_o, ln1_g, ln1_b, w_up, w_down, ln2_g, ln2_b, loss_target, m_ln_in_g, m_ln_in_b, m_w_ada, m_b_ada, m_w_in, m_q_norm_g, m_kv_norm_g, m_w_uq, m_w_ukv, m_w_o, m_ln1_g, m_ln1_b, m_w_up, m_w_down, m_ln2_g, m_ln2_b, v_ln_in_g, v_ln_in_b, v_w_ada, v_b_ada, v_w_in, v_q_norm_g, v_kv_norm_g, v_w_uq, v_w_ukv, v_w_o, v_ln1_g, v_ln1_b, v_w_up, v_w_down, v_ln2_g, v_ln2_b):
    nb, seq, _ = x.shape
    rows = nb * seq
    ix, iy, ic = lax.axis_index("x"), lax.axis_index("y"), lax.axis_index("c")
    chip = 2 * ix + iy
    dev = 2 * chip + ic

    def my_half(shards, group):
        packed = _pack_halves([s.astype(BF16) for s in shards], group)
        return lax.dynamic_index_in_dim(packed, ic, 0, keepdims=False)

    f_in, f_uq, f_ukv = _unpack_full(_gather8(my_half([w_in[0], w_uq[0], w_ukv[0]], GROUP_IN), "gather_w_in"),
                                     GROUP_IN)
    half_mlp = my_half([w_up[0], w_down[0], w_o[0]], GROUP_MLP)
    late_weights = _gather_exchange(half_mlp)
    w_in_p = _pad_w_in(f_in)
    uq3 = f_uq.reshape(Q_RANK, HEADS, NOPE + ROPE)
    w_uq_p = jnp.pad(uq3, ((0, 0), (0, 0), (0, LANES - NOPE - ROPE))).reshape(Q_RANK, HEADS * LANES)
    w_ukv_p = jnp.concatenate([_pad_heads(f_ukv[:, :HEADS * NOPE], NOPE), f_ukv[:, HEADS * NOPE:]], axis=1)

    n_all = 8 * nb
    c_all = _gather8(c.reshape(-1, LANES), "gather_c").reshape(n_all, D_MODEL)
    ada_cols = w_ada.shape[2]
    b_sh = lax.dynamic_slice_in_dim(b_ada, chip * ada_cols, ada_cols, axis=1)
    mod_sh = _ada_fwd(c_all, w_ada[0], b_sh)
    mod_g = _gather8(mod_sh, "gather_mod")[0::2]
    mod_all = jnp.moveaxis(mod_g, 0, 1).reshape(n_all, N_MOD * D_MODEL)
    mod_mine = lax.dynamic_slice_in_dim(mod_all, dev * nb, nb, axis=0).reshape(nb, N_MOD, D_MODEL)
    mod = jnp.pad(mod_mine, ((0, 0), (0, 8 - N_MOD), (0, 0)))

    cos_t, sin_a, sin_b = _rope_tables(seq)
    row2 = lambda v: v.reshape(1, -1)

    x2d = x.reshape(rows, D_MODEL)
    x0, h, qkv, lat, qp, kp, vm = _fwd_in(x2d, mod, row2(ln_in_g), row2(ln_in_b), w_in_p, q_norm_g, kv_norm_g,
                                          w_uq_p, w_ukv_p, cos_t, sin_a, sin_b, seq)
    sb_y, cars, g_mlp = _sb_fwd(qkv, seq, late_weights)
    g_mlp = _with_own(g_mlp, half_mlp)
    f_o = _split_slots(g_mlp, GROUP_MLP)[2].reshape(D_MODEL, D_MODEL)
    mla_y, lse = _mla_fwd(qp, kp, vm, seq)
    mix, y1, h2, u, ff, y2 = _fwd_out(sb_y, mla_y, x0, mod, f_o, ln1_g, ln1_b, g_mlp, seq)

    dy1, dmix, d_attn, dff, du, acc_out, dmod_a = _bwd_out(
        y2, loss_target.reshape(rows, D_MODEL), ff, u, y1, mix, mod, ln2_g, ln2_b, ln1_g, ln1_b, g_mlp, f_o, seq)
    c_idx = ic.reshape(1).astype(jnp.int32)
    blocks_mlp = _wgrad_packed(h2, du, "wgrad_up", lambda i, j: 2 * j + i, 0)
    blocks_mlp = _wgrad_packed(u, dff, "wgrad_down", lambda i, j: i, 1, pre="relu2", into=blocks_mlp)
    blocks_mlp = _wgrad_packed(sb_y, dmix, "wgrad_o_sb", lambda i, j: 0, 8, split=4, into=blocks_mlp)
    blocks_mlp = _wgrad_packed(mla_y, dmix, "wgrad_o_mla", lambda i, j: 1, 8, split=4, into=blocks_mlp)
    dq_sb, dk_sb, dv_sb, sibling_mlp = _sb_bwd(qkv, d_attn, cars, seq, _swap_cores_exchange(blocks_mlp))
    part_mlp, part_mlp_bf = _add_pairs(blocks_mlp, sibling_mlp, c_idx, "grad_add_cores_mlp")
    dqp, dkp, dvm, chips_mlp = _mla_bwd(qp, kp, vm, d_attn, mla_y, lse, seq, _scatter_chips_exchange(part_mlp_bf))
    grad_x, dproj, dqall, dkv, latn, acc0, acc_lat, dmod_c = _bwd_in(
        dqp, dkp, dvm, dq_sb, dk_sb, dv_sb, lat, x2d, dy1, mod, row2(ln_in_g), row2(ln_in_b), w_in_p,
        q_norm_g, kv_norm_g, w_uq_p, w_ukv_p, cos_t, sin_a, sin_b, seq)

    g_in = _unpad_w_in(_wgrad(h, dproj, "wgrad_in", tn=768))
    g_uq = _unpad_heads(_wgrad(latn[:, :Q_RANK], dqall, "wgrad_uq"), NOPE + ROPE)
    g_ukv_p = _wgrad(latn[:, Q_RANK:], dkv, "wgrad_ukv", tn=512)
    g_ukv = jnp.concatenate([_unpad_heads(g_ukv_p[:, :HEADS * LANES], NOPE), g_ukv_p[:, HEADS * LANES:]], axis=1)
    blocks_in = _pack_full([g_in, g_uq, g_ukv], GROUP_IN)
    sibling_in = _run_exchange(_swap_cores_exchange(blocks_in), "grads_in_to_sibling")
    part_in, part_in_bf = _add_pairs(blocks_in, sibling_in, c_idx, "grad_add_cores_in")
    chips_in = _run_exchange(_scatter_chips_exchange(part_in_bf), "grads_in_to_chips")

    def own(part):
        return lax.dynamic_index_in_dim(part, chip, 0, keepdims=False)

    half = jnp.concatenate([_add_chips(own(part_in), chips_in, "grad_add_chips_in"),
                            _add_chips(own(part_mlp), chips_mlp, "grad_add_chips_mlp")], axis=0)
    other = _run_exchange(_swap_one_exchange(half), "grads_halves")
    mine = _unpack_half(half[:GROUP_IN[0]], GROUP_IN) + _unpack_half(half[GROUP_IN[0]:], GROUP_MLP)
    theirs = _unpack_half(other[:GROUP_IN[0]], GROUP_IN) + _unpack_half(other[GROUP_IN[0]:], GROUP_MLP)

    dmod = (dmod_a + dmod_c)[:, :N_MOD, :]
    small_part = _pack_small([acc0[0], acc0[1], jnp.zeros((N_MOD * D_MODEL,), F32), acc_lat[0, :Q_RANK],
                              acc_lat[1, :KV_RANK], acc_out[3], acc_out[4], acc_out[0], acc_out[1]])
    n_sum = SMALL_ROWS + D_MODEL // LANES
    payload = jnp.concatenate([small_part, acc_out[2].reshape(-1, LANES), dmod.reshape(-1, LANES)], axis=0)
    gathered = _gather8(payload, "gather_small")
    small_sum = _sum_lead(gathered[:, :n_sum, :], "sum_small")
    loss = jnp.sum(small_sum[SMALL_ROWS:])
    dmod_all = gathered[:, n_sum:, :].reshape(n_all, N_MOD * D_MODEL)
    g_b_ada = _sum_lead(dmod_all.reshape(n_all, N_MOD * D_MODEL // LANES, LANES), "sum_b_ada")
    dmod_sh = lax.dynamic_slice_in_dim(dmod_all, chip * ada_cols, ada_cols, axis=1)
    g_w_ada = _ada_bwd(c_all, dmod_sh)

    res = {}
    d_ada, m_ada, v_ada = _adamw(w_ada[0], g_w_ada, m_w_ada[0], v_w_ada[0], "adamw_w_ada")
    res["w_ada"] = (g_w_ada[None], d_ada[None], m_ada[None], v_ada[None])
    sharded = {"w_in": (w_in, m_w_in, v_w_in), "w_uq": (w_uq, m_w_uq, v_w_uq), "w_ukv": (w_ukv, m_w_ukv, v_w_ukv),
               "w_up": (w_up, m_w_up, v_w_up), "w_down": (w_down, m_w_down, v_w_down), "w_o": (w_o, m_w_o, v_w_o)}
    for (name, (w, m, v)), g_mine, g_other in zip(sharded.items(), mine, theirs):
        quad = _adamw_halves(w[0], g_mine, g_other, m[0], v[0], c_idx, "adamw_" + name)
        res[name] = tuple(a[None] for a in quad)
    small_w = [ln_in_g, ln_in_b, b_ada, q_norm_g, kv_norm_g, ln1_g, ln1_b, ln2_g, ln2_b]
    small_m = [m_ln_in_g, m_ln_in_b, m_b_ada, m_q_norm_g, m_kv_norm_g, m_ln1_g, m_ln1_b, m_ln2_g, m_ln2_b]
    small_v = [v_ln_in_g, v_ln_in_b, v_b_ada, v_q_norm_g, v_kv_norm_g, v_ln1_g, v_ln1_b, v_ln2_g, v_ln2_b]
    for (name, _), quad in zip(SMALL, _adamw_small(small_sum, g_b_ada, small_w, small_m, small_v)):
        res[name] = quad

    order = ["ln_in_g", "ln_in_b", "w_ada", "b_ada", "w_in", "q_norm_g", "kv_norm_g", "w_uq", "w_ukv", "w_o",
             "ln1_g", "ln1_b", "w_up", "w_down", "ln2_g", "ln2_b"]
    outs = [loss, grad_x.reshape(nb, seq, D_MODEL)]
    for k in range(4):
        outs += [res[name][k] for name in order]
    return tuple(outs)
```

```python
import functools
import math

import jax
import jax.numpy as jnp
from jax import lax
from jax.experimental import pallas as pl
from jax.experimental.pallas import tpu as pltpu

F32 = jnp.float32
BF16 = jnp.bfloat16
MESH_IDS = pl.DeviceIdType.MESH

D_MODEL = 1024
HEADS = 8
HEAD_PAIRS = HEADS // 2
SB_W = 512
MLA_W = 512
NOPE = 64
ROPE = 32
Q_RANK = 384
KV_RANK = 256
D_IN = 2208
D_IN_PAD = 2304
D_FF = 4096
N_MOD = 6
LN_EPS = 1e-5
RMS_EPS = 1e-6
ALPHA = 2.0 ** 0.25
ROPE_BASE = 10000.0
SB_SCALE = 64 ** -0.5
NEG_LOG2E = -math.log2(math.e)
MLA_SCALE = 96 ** -0.5
ADAM_LR = 0.001
ADAM_B1 = 0.9
ADAM_B2 = 0.999
ADAM_EPS = 1e-08
ADAM_WD = 0.01
ADAM_STEP = 10

LANES = 128
ROW_TILE = 256
ATTN_TILE = 256
CAR_SLOTS = 8
ATTN_PAIRS = 4
VMEM_LIMIT = 56 << 20

NT = (((1,), (1,)), ((), ()))
TN = (((0,), (0,)), ((), ()))


def _params(sem=None):
    return pltpu.CompilerParams(vmem_limit_bytes=VMEM_LIMIT, dimension_semantics=sem)


def _const_spec(shape):
    zeros = (0,) * len(shape)
    return pl.BlockSpec(shape, lambda *_: zeros, pipeline_mode=pl.Buffered(1))


def _dot(a, b, dims=None):
    if dims is None:
        return jnp.dot(a, b, preferred_element_type=F32)
    return lax.dot_general(a, b, dims, preferred_element_type=F32)


def _mean(v):
    return jnp.mean(v, axis=-1, keepdims=True)


def _rowsum(v):
    return jnp.sum(v, axis=0, keepdims=True)


def _ln_fwd(y, g, b):
    mu = _mean(y)
    yc = y - mu
    rstd = lax.rsqrt(_mean(yc * yc) + LN_EPS)
    xhat = yc * rstd
    return xhat * g + b, xhat, rstd


def _ln_bwd(dx, xhat, rstd, g):
    dxh = dx * g
    return rstd * (dxh - _mean(dxh) - xhat * _mean(dxh * xhat))


def _rope(v, cos, sin_a, sin_b):
    return v * cos + pltpu.roll(v, 112, 1) * sin_a + pltpu.roll(v, 16, 1) * sin_b


def _rope_t(dv, cos, sin_a, sin_b):
    return dv * cos + pltpu.roll(dv * sin_a, 16, 1) + pltpu.roll(dv * sin_b, 112, 1)


def _my_place():
    return lax.axis_index("x"), lax.axis_index("y"), lax.axis_index("c")


class _Exchange:
    def __init__(self, operand, out_shape, n_copies, phases):
        self.operand = operand
        self.out_shape = out_shape
        self.phases = phases
        self.scratch = [pltpu.SemaphoreType.DMA((n_copies,)), pltpu.SemaphoreType.DMA((n_copies,))]


def _run_exchange(ex, name):
    def body(in_ref, out_ref, send_sems, recv_sems):
        for phase in ex.phases(in_ref, out_ref, send_sems, recv_sems):
            phase()

    return pl.pallas_call(
        body, name=name, out_shape=ex.out_shape,
        in_specs=[pl.BlockSpec(memory_space=pl.ANY)], out_specs=pl.BlockSpec(memory_space=pl.ANY),
        scratch_shapes=ex.scratch,
    )(ex.operand)


def _nothing():
    pass


def _gather_exchange(v):
    m, n = v.shape

    def phases(v_ref, out_ref, send_sems, recv_sems):
        x, y, c = _my_place()
        me, sibling = (x, y, c), (x, y, 1 - c)
        chips = [(1 - x, y), (x, 1 - y), (1 - x, 1 - y)]

        def rows(px, py, pc):
            return out_ref.at[4 * px + 2 * py + pc]

        def copy(k, block, to, src=None):
            return pltpu.make_async_remote_copy(
                src_ref=rows(*block) if src is None else src, dst_ref=rows(*block),
                send_sem=send_sems.at[k], recv_sem=recv_sems.at[k], device_id=to, device_id_type=MESH_IDS)

        first = [copy(0, me, sibling, src=v_ref)]
        first += [copy(1 + j, me, (*chip, c), src=v_ref) for j, chip in enumerate(chips)]
        passed = [copy(4 + j, (*chip, c), sibling) for j, chip in enumerate(chips)]

        def start():
            for cp in first:
                cp.start()

        def middle():
            for j, chip in enumerate(chips):
                copy(1 + j, (*chip, c), me).wait_recv()
                passed[j].start()

        def finish():
            copy(0, sibling, me).wait_recv()
            for j, chip in enumerate(chips):
                copy(4 + j, (*chip, 1 - c), me).wait_recv()
            for cp in first + passed:
                cp.wait_send()

        return start, middle, finish

    return _Exchange(v, jax.ShapeDtypeStruct((8, m, n), v.dtype), 7, phases)


def _with_own(gathered, v):
    dev = 4 * lax.axis_index("x") + 2 * lax.axis_index("y") + lax.axis_index("c")
    return lax.dynamic_update_index_in_dim(gathered, v, dev, 0)


def _direct_exchange(operand, out_shape, n_copies, make_copies):
    def phases(in_ref, out_ref, send_sems, recv_sems):
        copies = make_copies(in_ref, out_ref, send_sems, recv_sems)

        def start():
            for cp in copies:
                cp.start()

        def finish():
            for cp in copies:
                cp.wait()

        return start, _nothing, finish

    return _Exchange(operand, out_shape, n_copies, phases)


def _swap_cores_exchange(blocks):
    _, m, n = blocks.shape

    def make_copies(g_ref, out_ref, send_sems, recv_sems):
        x, y, c = _my_place()
        return [pltpu.make_async_remote_copy(
            src_ref=g_ref.at[2 * j + (1 - c)], dst_ref=out_ref.at[j],
            send_sem=send_sems.at[j], recv_sem=recv_sems.at[j],
            device_id=(x, y, 1 - c), device_id_type=MESH_IDS) for j in range(4)]

    return _direct_exchange(blocks, jax.ShapeDtypeStruct((4, m, n), blocks.dtype), 4, make_copies)


def _scatter_chips_exchange(parts):
    _, m, n = parts.shape
    flips = [(1, 0), (0, 1), (1, 1)]

    def make_copies(p_ref, out_ref, send_sems, recv_sems):
        x, y, c = _my_place()
        copies = []
        for k, (fx, fy) in enumerate(flips):
            tx = 1 - x if fx else x
            ty = 1 - y if fy else y
            copies.append(pltpu.make_async_remote_copy(
                src_ref=p_ref.at[2 * tx + ty], dst_ref=out_ref.at[k],
                send_sem=send_sems.at[k], recv_sem=recv_sems.at[k],
                device_id=(tx, ty, c), device_id_type=MESH_IDS))
        return copies

    return _direct_exchange(parts, jax.ShapeDtypeStruct((3, m, n), parts.dtype), 3, make_copies)


def _swap_one_exchange(v):
    def make_copies(v_ref, out_ref, send_sems, recv_sems):
        x, y, c = _my_place()
        return [pltpu.make_async_remote_copy(src_ref=v_ref, dst_ref=out_ref, send_sem=send_sems.at[0],
                                             recv_sem=recv_sems.at[0], device_id=(x, y, 1 - c),
                                             device_id_type=MESH_IDS)]

    return _direct_exchange(v, jax.ShapeDtypeStruct(v.shape, v.dtype), 1, make_copies)


def _gather8(v, name):
    return _with_own(_run_exchange(_gather_exchange(v), name), v)


def _carried(ex, refs, n_in, n_out, n_scratch):
    ins, ex_in = refs[:n_in], refs[n_in]
    outs, ex_out = refs[n_in + 1:n_in + 1 + n_out], refs[n_in + 1 + n_out]
    at = n_in + 2 + n_out
    return ins, outs + refs[at:at + n_scratch], ex.phases(ex_in, ex_out, *refs[at + n_scratch:])


def _ada_fwd(c_all, w_ada_sh, b_ada_sh):
    nb, cols = c_all.shape[0], w_ada_sh.shape[1]
    tn = 512

    def body(c_ref, w_ref, b_ref, o_ref):
        cv = c_ref[...]
        act = (cv * jax.nn.sigmoid(cv)).astype(BF16)
        o_ref[...] = _dot(act, w_ref[...].astype(BF16)) + b_ref[...]

    return pl.pallas_call(
        body, name="ada_fwd", grid=(cols // tn,),
        out_shape=jax.ShapeDtypeStruct((nb, cols), F32),
        in_specs=[pl.BlockSpec((nb, D_MODEL), lambda j: (0, 0)),
                  pl.BlockSpec((D_MODEL, tn), lambda j: (0, j)),
                  pl.BlockSpec((1, tn), lambda j: (0, j))],
        out_specs=pl.BlockSpec((nb, tn), lambda j: (0, j)),
        compiler_params=_params(("arbitrary",)),
    )(c_all, w_ada_sh, b_ada_sh)


def _ada_bwd(c_all, dmod_sh):
    nb, cols = dmod_sh.shape
    tn = 512

    def body(c_ref, d_ref, o_ref):
        cv = c_ref[...]
        act = (cv * jax.nn.sigmoid(cv)).astype(BF16)
        o_ref[...] = _dot(act, d_ref[...].astype(BF16), TN)

    return pl.pallas_call(
        body, name="ada_bwd", grid=(cols // tn,),
        out_shape=jax.ShapeDtypeStruct((D_MODEL, cols), F32),
        in_specs=[pl.BlockSpec((nb, D_MODEL), lambda j: (0, 0)),
                  pl.BlockSpec((nb, tn), lambda j: (0, j))],
        out_specs=pl.BlockSpec((D_MODEL, tn), lambda j: (0, j)),
        compiler_params=_params(("arbitrary",)),
    )(c_all, dmod_sh)


def _sum_lead(v, name):
    k, m, n = v.shape

    def body(v_ref, o_ref):
        acc = v_ref[0]
        for i in range(1, k):
            acc = acc + v_ref[i]
        o_ref[...] = acc

    return pl.pallas_call(
        body, name=name, out_shape=jax.ShapeDtypeStruct((m, n), F32),
        in_specs=[pl.BlockSpec((k, m, n), lambda: (0, 0, 0))],
        out_specs=pl.BlockSpec((m, n), lambda: (0, 0)),
        compiler_params=_params(),
    )(v)


def _adamw_math(w, g, m, v):
    mn = ADAM_B1 * m + (1.0 - ADAM_B1) * g
    vn = ADAM_B2 * v + (1.0 - ADAM_B2) * (g * g)
    m_hat = mn / (1.0 - ADAM_B1 ** ADAM_STEP)
    v_hat = vn / (1.0 - ADAM_B2 ** ADAM_STEP)
    return -ADAM_LR * (m_hat / (jnp.sqrt(v_hat) + ADAM_EPS) + ADAM_WD * w), mn, vn


def _adamw_small(g_sum, g_b_ada, ws, ms, vs):
    n = len(SMALL)

    def body(gs_ref, gb_ref, *refs):
        outs = refs[3 * n:]
        for p in range(n):
            rows_p = SMALL[p][1] // LANES
            g = gb_ref[...] if SMALL[p][0] == "b_ada" else gs_ref[SMALL_AT[p]:SMALL_AT[p] + rows_p, :]
            d, mn, vn = _adamw_math(refs[p][...], g, refs[n + p][...], refs[2 * n + p][...])
            outs[4 * p][...] = g
            outs[4 * p + 1][...] = d
            outs[4 * p + 2][...] = mn
            outs[4 * p + 3][...] = vn

    shapes = [jax.ShapeDtypeStruct((size // LANES, LANES), F32) for _, size in SMALL for _ in range(4)]
    flat = lambda arrs: [a.reshape(-1, LANES) for a in arrs]
    res = pl.pallas_call(body, name="adamw_small", out_shape=tuple(shapes), compiler_params=_params())(
        g_sum, g_b_ada, *flat(ws), *flat(ms), *flat(vs))
    return [tuple(r.reshape(w.shape) for r in res[4 * p:4 * p + 4]) for p, w in enumerate(ws)]


def _adamw_halves(w, g_mine, g_other, m, v, c_idx, name):
    r, cols = w.shape
    half = r // 2
    tr = half
    while tr * cols * 4 > (2 << 20) and tr % 16 == 0:
        tr //= 2

    def body(c_ref, w_ref, mine_ref, other_ref, m_ref, v_ref, g_ref, d_ref, mo_ref, vo_ref):
        g = jnp.where(pl.program_id(0) == c_ref[0], mine_ref[...], other_ref[...])
        g_ref[0] = g
        d_ref[0], mo_ref[0], vo_ref[0] = _adamw_math(w_ref[0], g, m_ref[0], v_ref[0])

    full = pl.BlockSpec((1, tr, cols), lambda h, i, c: (h, i, 0))
    part = pl.BlockSpec((tr, cols), lambda h, i, c: (i, 0))
    shape = jax.ShapeDtypeStruct((2, half, cols), F32)
    grid_spec = pltpu.PrefetchScalarGridSpec(
        num_scalar_prefetch=1, grid=(2, half // tr),
        in_specs=[full, part, part, full, full], out_specs=(full, full, full, full))
    split = lambda a: a.reshape(2, half, cols)
    res = pl.pallas_call(
        body, name=name, grid_spec=grid_spec, out_shape=(shape, shape, shape, shape),
        compiler_params=_params(("arbitrary", "arbitrary")),
    )(c_idx, split(w), g_mine, g_other, split(m), split(v))
    return tuple(a.reshape(r, cols) for a in res)


def _adamw(w, g, m, v, name):
    rows, cols = w.shape
    tr = rows
    while tr * cols * 4 > (2 << 20) and tr % 16 == 0:
        tr //= 2

    def body(w_ref, g_ref, m_ref, v_ref, d_ref, mo_ref, vo_ref):
        d_ref[...], mo_ref[...], vo_ref[...] = _adamw_math(w_ref[...], g_ref[...], m_ref[...], v_ref[...])

    spec = pl.BlockSpec((tr, cols), lambda i: (i, 0))
    shape = jax.ShapeDtypeStruct((rows, cols), F32)
    return pl.pallas_call(
        body, name=name, grid=(rows // tr,), out_shape=(shape, shape, shape),
        in_specs=[spec, spec, spec, spec], out_specs=(spec, spec, spec),
        compiler_params=_params(("arbitrary",)),
    )(w, g, m, v)


def _add_rows(m, n):
    fits = [d for d in range(16, m + 1, 16) if m % d == 0 and d * n * 4 <= (5 << 19)]
    assert fits, (m, n)
    return max(fits)


def _add_pairs(blocks, recv, c_idx, name):
    _, m, n = blocks.shape
    tr = _add_rows(m, n)

    def body(c_ref, a_ref, b_ref, o_ref, ob_ref):
        s = a_ref[...] + b_ref[...]
        o_ref[...] = s
        ob_ref[...] = s.astype(BF16)

    grid_spec = pltpu.PrefetchScalarGridSpec(
        num_scalar_prefetch=1, grid=(4, m // tr),
        in_specs=[pl.BlockSpec((1, tr, n), lambda j, i, c: (2 * j + c[0], i, 0)),
                  pl.BlockSpec((1, tr, n), lambda j, i, c: (j, i, 0))],
        out_specs=(pl.BlockSpec((1, tr, n), lambda j, i, c: (j, i, 0)),
                   pl.BlockSpec((1, tr, n), lambda j, i, c: (j, i, 0))))
    return pl.pallas_call(
        body, name=name, grid_spec=grid_spec,
        out_shape=(jax.ShapeDtypeStruct((4, m, n), F32), jax.ShapeDtypeStruct((4, m, n), BF16)),
        compiler_params=_params(("arbitrary", "arbitrary")),
    )(c_idx, blocks, recv)


def _add_chips(own, recv, name):
    m, n = own.shape
    tr = _add_rows(m, n)

    def body(a_ref, r_ref, o_ref):
        acc = a_ref[...]
        for k in range(3):
            acc = acc + r_ref[k].astype(F32)
        o_ref[...] = acc

    return pl.pallas_call(
        body, name=name, grid=(m // tr,),
        out_shape=jax.ShapeDtypeStruct((m, n), F32),
        in_specs=[pl.BlockSpec((tr, n), lambda i: (i, 0)), pl.BlockSpec((3, tr, n), lambda i: (0, i, 0))],
        out_specs=pl.BlockSpec((tr, n), lambda i: (i, 0)),
        compiler_params=_params(("arbitrary",)),
    )(own, recv)


def _row_spec(cols):
    return pl.BlockSpec((ROW_TILE, cols), lambda i: (i, 0))


def _mod_spec(tiles_per_seq):
    return pl.BlockSpec((1, 8, D_MODEL), lambda i: (i // tiles_per_seq, 0, 0))


def _table_spec(tiles_per_seq):
    return pl.BlockSpec((ROW_TILE, LANES), lambda i: (i % tiles_per_seq, 0))


def _fwd_in(x, mod, ln_g, ln_b, w_in, q_g, kv_g, w_uq, w_ukv, cos_t, sin_a, sin_b, seq):
    rows = x.shape[0]
    tm = ROW_TILE
    tps = seq // tm

    def body(x_ref, mod_ref, g_ref, b_ref, win_ref, qg_ref, kvg_ref, wuq_ref, wukv_ref, cos_ref, sa_ref, sb_ref,
             x0_ref, h_ref, qkv_ref, lat_ref, qp_ref, kp_ref, vm_ref):
        x0, _, _ = _ln_fwd(x_ref[...], g_ref[...], b_ref[...])
        x0_ref[...] = x0
        h = (x0 * (1.0 + mod_ref[0, 1:2, :]) + mod_ref[0, 0:1, :]).astype(BF16)
        h_ref[...] = h
        proj = _dot(h, win_ref[...])
        qkv_ref[:, :SB_W] = (proj[:, :SB_W] * SB_SCALE).astype(BF16)
        qkv_ref[:, SB_W:] = proj[:, SB_W:3 * SB_W].astype(BF16)
        lat_ref[...] = proj[:, 3 * SB_W:3 * SB_W + Q_RANK + KV_RANK]
        cq = proj[:, 3 * SB_W:3 * SB_W + Q_RANK]
        ckv = proj[:, 3 * SB_W + Q_RANK:3 * SB_W + Q_RANK + KV_RANK]
        kr = proj[:, D_IN_PAD - LANES:]
        cos, sa, sb = cos_ref[...], sa_ref[...], sb_ref[...]
        cqn = (cq * lax.rsqrt(_mean(cq * cq) + RMS_EPS) * qg_ref[...]).astype(BF16)
        q_all = _dot(cqn, wuq_ref[...])
        for hd in range(HEADS):
            sl = slice(hd * LANES, (hd + 1) * LANES)
            qp_ref[:, sl] = _rope(q_all[:, sl], cos, sa, sb).astype(BF16)
        ckvn = (ckv * lax.rsqrt(_mean(ckv * ckv) + RMS_EPS) * kvg_ref[...]).astype(BF16)
        kv = _dot(ckvn, wukv_ref[...])
        kr_rot = _rope(kr, cos, sa, sb)
        for hd in range(HEADS):
            sl = slice(hd * LANES, (hd + 1) * LANES)
            kp_ref[:, sl] = (kv[:, sl] + kr_rot).astype(BF16)
        vm_ref[...] = kv[:, HEADS * LANES:].astype(BF16)

    outs = [(D_MODEL, F32), (D_MODEL, BF16), (3 * SB_W, BF16), (Q_RANK + KV_RANK, F32),
            (HEADS * LANES, BF16), (HEADS * LANES, BF16), (MLA_W, BF16)]
    return pl.pallas_call(
        body, name="fwd_in", grid=(rows // tm,),
        out_shape=tuple(jax.ShapeDtypeStruct((rows, n), dt) for n, dt in outs),
        in_specs=[_row_spec(D_MODEL), _mod_spec(tps), _const_spec((1, D_MODEL)), _const_spec((1, D_MODEL)),
                  _const_spec(w_in.shape), _const_spec((1, Q_RANK)), _const_spec((1, KV_RANK)),
                  _const_spec(w_uq.shape), _const_spec(w_ukv.shape),
                  _table_spec(tps), _table_spec(tps), _table_spec(tps)],
        out_specs=tuple(_row_spec(n) for n, _ in outs),
        compiler_params=_params(("arbitrary",)),
    )(x, mod, ln_g, ln_b, w_in, q_g, kv_g, w_uq, w_ukv, cos_t, sin_a, sin_b)


HALF = 512
SHARD = 1024


def _mlp_weight_specs():
    return [pl.BlockSpec((8, HALF, SHARD), lambda i: (0, 0, 0), pipeline_mode=pl.Buffered(1)),
            pl.BlockSpec((8, HALF, SHARD), lambda i: (0, 1, 0), pipeline_mode=pl.Buffered(1))]


def _fwd_out(sb_y, mla_y, x0, mod, w_o, ln_g, ln_b, g_mlp, seq):
    rows = x0.shape[0]
    tm = ROW_TILE
    tps = seq // tm

    def body(sb_ref, ml_ref, x0_ref, mod_ref, wo_ref, g_ref, b_ref, wu_ref, wd_ref,
             mix_ref, y1_ref, h2_ref, u_ref, ff_ref, y2_ref):
        mix = _dot(sb_ref[...], wo_ref[:SB_W, :]) + _dot(ml_ref[...].astype(BF16), wo_ref[SB_W:, :])
        mix_ref[...] = mix
        y1 = ALPHA * x0_ref[...] + (1.0 + mod_ref[0, 2:3, :]) * mix
        y1_ref[...] = y1
        x1, _, _ = _ln_fwd(y1, g_ref[...], b_ref[...])
        h2 = (x1 * (1.0 + mod_ref[0, 4:5, :]) + mod_ref[0, 3:4, :]).astype(BF16)
        h2_ref[...] = h2
        h_lo, h_hi = h2[:, :HALF], h2[:, HALF:]
        ff = jnp.zeros((tm, D_MODEL), F32)
        for chip in range(4):
            u = _dot(h_lo, wu_ref[2 * chip]) + _dot(h_hi, wu_ref[2 * chip + 1])
            u_ref[:, chip * SHARD:(chip + 1) * SHARD] = u.astype(BF16)
            act = jnp.square(jnp.maximum(u, 0.0)).astype(BF16)
            ff = ff + _dot(act[:, :HALF], wd_ref[2 * chip]) + _dot(act[:, HALF:], wd_ref[2 * chip + 1])
        ff_ref[...] = ff
        y2_ref[...] = ALPHA * x1 + (1.0 + mod_ref[0, 5:6, :]) * ff

    outs = [(D_MODEL, F32), (D_MODEL, F32), (D_MODEL, BF16), (D_FF, BF16), (D_MODEL, F32), (D_MODEL, F32)]
    return pl.pallas_call(
        body, name="fwd_out", grid=(rows // tm,),
        out_shape=tuple(jax.ShapeDtypeStruct((rows, n), dt) for n, dt in outs),
        in_specs=[_row_spec(SB_W), _row_spec(MLA_W), _row_spec(D_MODEL), _mod_spec(tps), _const_spec(w_o.shape),
                  _const_spec((1, D_MODEL)), _const_spec((1, D_MODEL))] + _mlp_weight_specs(),
        out_specs=tuple(_row_spec(n) for n, _ in outs),
        compiler_params=_params(("arbitrary",)),
    )(sb_y, mla_y, x0, mod, w_o, ln_g, ln_b, g_mlp, g_mlp)


def _acc_spec(rows=8, cols=D_MODEL):
    return pl.BlockSpec((rows, cols), lambda i: (0, 0))


def _bwd_out(y2, tgt, ff, u, y1, mix, mod, ln2_g, ln2_b, ln1_g, ln1_b, g_mlp, w_o, seq):
    rows = y2.shape[0]
    nb = rows // seq
    tm = ROW_TILE
    tps = seq // tm

    def body(y2_ref, t_ref, ff_ref, u_ref, y1_ref, mix_ref, mod_ref, g2_ref, b2_ref, g_ref, b_ref, wu_ref, wd_ref,
             wo_ref, dy1_ref, dmix_ref, do_ref, dff_ref, du_ref, acc_ref, dmod_ref):
        i = pl.program_id(0)

        @pl.when(i == 0)
        def _():
            acc_ref[...] = jnp.zeros_like(acc_ref)

        @pl.when(i % tps == 0)
        def _():
            dmod_ref[...] = jnp.zeros_like(dmod_ref)

        g2 = g2_ref[...]
        x2, xhat2, rstd2 = _ln_fwd(y2_ref[...], g2, b2_ref[...])
        err = x2 - t_ref[...]
        dx2 = err * (1.0 / D_MODEL)
        acc_ref[0:1, :] += _rowsum(dx2 * xhat2)
        acc_ref[1:2, :] += _rowsum(dx2)
        acc_ref[2:3, :] += _rowsum(err * err) * (0.5 / D_MODEL)
        dy2 = _ln_bwd(dx2, xhat2, rstd2, g2)
        dmod_ref[0, 5:6, :] += _rowsum(dy2 * ff_ref[...])
        dff = ((1.0 + mod_ref[0, 5:6, :]) * dy2).astype(BF16)
        dff_ref[...] = dff
        for blk in range(8):
            cols = slice(blk * HALF, (blk + 1) * HALF)
            da = _dot(dff, wd_ref[blk], NT)
            du_ref[:, cols] = (da * (2.0 * jnp.maximum(u_ref[:, cols].astype(F32), 0.0))).astype(BF16)

        g = g_ref[...]
        x1, xhat, rstd = _ln_fwd(y1_ref[...], g, b_ref[...])
        halves = []
        for half in range(2):
            acc = jnp.zeros((tm, HALF), F32)
            for chip in range(4):
                acc = acc + _dot(du_ref[:, chip * SHARD:(chip + 1) * SHARD], wu_ref[2 * chip + half], NT)
            halves.append(acc)
        dh2 = jnp.concatenate(halves, axis=1)
        dmod_ref[0, 3:4, :] += _rowsum(dh2)
        dmod_ref[0, 4:5, :] += _rowsum(dh2 * x1)
        dx1 = ALPHA * dy2 + dh2 * (1.0 + mod_ref[0, 4:5, :])
        acc_ref[3:4, :] += _rowsum(dx1 * xhat)
        acc_ref[4:5, :] += _rowsum(dx1)
        dy1 = _ln_bwd(dx1, xhat, rstd, g)
        dy1_ref[...] = dy1
        dmod_ref[0, 2:3, :] += _rowsum(dy1 * mix_ref[...])
        dmix = ((1.0 + mod_ref[0, 2:3, :]) * dy1).astype(BF16)
        dmix_ref[...] = dmix
        do_ref[...] = _dot(dmix, wo_ref[...], NT)

    outs = [(D_MODEL, F32), (D_MODEL, BF16), (D_MODEL, F32), (D_MODEL, BF16), (D_FF, BF16)]
    return pl.pallas_call(
        body, name="bwd_out", grid=(rows // tm,),
        out_shape=tuple(jax.ShapeDtypeStruct((rows, n), dt) for n, dt in outs)
        + (jax.ShapeDtypeStruct((8, D_MODEL), F32), jax.ShapeDtypeStruct((nb, 8, D_MODEL), F32)),
        in_specs=[_row_spec(D_MODEL), _row_spec(D_MODEL), _row_spec(D_MODEL), _row_spec(D_FF), _row_spec(D_MODEL),
                  _row_spec(D_MODEL), _mod_spec(tps), _const_spec((1, D_MODEL)), _const_spec((1, D_MODEL)),
                  _const_spec((1, D_MODEL)), _const_spec((1, D_MODEL))] + _mlp_weight_specs()
        + [_const_spec(w_o.shape)],
        out_specs=tuple(_row_spec(n) for n, _ in outs) + (_acc_spec(), _mod_spec(tps)),
        compiler_params=_params(("arbitrary",)),
    )(y2, tgt, ff, u, y1, mix, mod, ln2_g, ln2_b, ln1_g, ln1_b, g_mlp, g_mlp, w_o)


def _bwd_in(dqp, dkp, dvm, dq_sb, dk_sb, dv_sb, lat, x, dy1, mod, ln_g, ln_b, w_in, q_g, kv_g, w_uq, w_ukv,
            cos_t, sin_a, sin_b, seq):
    rows = x.shape[0]
    nb = rows // seq
    tm = ROW_TILE
    tps = seq // tm
    n_lat = Q_RANK + KV_RANK

    def body(dqp_ref, dkp_ref, dvm_ref, dqs_ref, dks_ref, dvs_ref, lat_ref, x_ref, dy1_ref, mod_ref,
             g_ref, b_ref, win_ref, qg_ref, kvg_ref, wuq_ref, wukv_ref, cos_ref, sa_ref, sb_ref,
             dx_ref, dproj_ref, dqall_ref, dkv_ref, latn_ref, acc_ref, accl_ref, dmod_ref):
        i = pl.program_id(0)

        @pl.when(i == 0)
        def _():
            acc_ref[...] = jnp.zeros_like(acc_ref)
            accl_ref[...] = jnp.zeros_like(accl_ref)

        @pl.when(i % tps == 0)
        def _():
            dmod_ref[...] = jnp.zeros_like(dmod_ref)

        cos, sa, sb = cos_ref[...], sa_ref[...], sb_ref[...]
        lane = lax.broadcasted_iota(jnp.int32, (tm, LANES), 1)
        for hd in range(HEADS):
            sl = slice(hd * LANES, (hd + 1) * LANES)
            dqall_ref[:, sl] = _rope_t(dqp_ref[:, sl], cos, sa, sb).astype(BF16)
        dcqn = _dot(dqall_ref[...], wuq_ref[...], NT)
        cq = lat_ref[:, :Q_RANK]
        qg = qg_ref[...]
        rq = lax.rsqrt(_mean(cq * cq) + RMS_EPS)
        cqn = cq * rq
        latn_ref[:, :Q_RANK] = (cqn * qg).astype(BF16)
        accl_ref[0:1, :Q_RANK] += _rowsum(dcqn * cqn)
        dqg = dcqn * qg
        dcq = rq * (dqg - cqn * _mean(dqg * cqn))
        dkr = jnp.zeros((tm, LANES), F32)
        for hd in range(HEADS):
            sl = slice(hd * LANES, (hd + 1) * LANES)
            dk = dkp_ref[:, sl]
            dkr = dkr + dk
            dkv_ref[:, sl] = jnp.where(lane < NOPE, dk, 0.0).astype(BF16)
        dkv_ref[:, HEADS * LANES:] = dvm_ref[...].astype(BF16)
        dckvn = _dot(dkv_ref[...], wukv_ref[...], NT)
        ckv = lat_ref[:, Q_RANK:]
        kvg = kvg_ref[...]
        rkv = lax.rsqrt(_mean(ckv * ckv) + RMS_EPS)
        ckvn = ckv * rkv
        latn_ref[:, Q_RANK:] = (ckvn * kvg).astype(BF16)
        accl_ref[1:2, :KV_RANK] += _rowsum(dckvn * ckvn)
        dkg = dckvn * kvg
        dckv = rkv * (dkg - ckvn * _mean(dkg * ckvn))
        dkr = _rope_t(jnp.where(lane >= NOPE, dkr, 0.0), cos, sa, sb)
        dproj_ref[:, :SB_W] = dqs_ref[...]
        dproj_ref[:, SB_W:2 * SB_W] = dks_ref[...].astype(BF16)
        dproj_ref[:, 2 * SB_W:3 * SB_W] = dvs_ref[...].astype(BF16)
        dproj_ref[:, 3 * SB_W:3 * SB_W + Q_RANK] = dcq.astype(BF16)
        dproj_ref[:, 3 * SB_W + Q_RANK:3 * SB_W + n_lat] = dckv.astype(BF16)
        dproj_ref[:, D_IN_PAD - LANES:] = dkr.astype(BF16)
        dh = _dot(dproj_ref[...], win_ref[...], NT)
        g = g_ref[...]
        x0, xhat, rstd = _ln_fwd(x_ref[...], g, b_ref[...])
        dmod_ref[0, 0:1, :] += _rowsum(dh)
        dmod_ref[0, 1:2, :] += _rowsum(dh * x0)
        dx0 = ALPHA * dy1_ref[...] + dh * (1.0 + mod_ref[0, 1:2, :])
        acc_ref[0:1, :] += _rowsum(dx0 * xhat)
        acc_ref[1:2, :] += _rowsum(dx0)
        dx_ref[...] = _ln_bwd(dx0, xhat, rstd, g)

    outs = [(D_MODEL, F32), (D_IN_PAD, BF16), (HEADS * LANES, BF16), (HEADS * LANES + MLA_W, BF16), (n_lat, BF16)]
    return pl.pallas_call(
        body, name="bwd_in", grid=(rows // tm,),
        out_shape=tuple(jax.ShapeDtypeStruct((rows, n), dt) for n, dt in outs)
        + (jax.ShapeDtypeStruct((8, D_MODEL), F32), jax.ShapeDtypeStruct((8, Q_RANK), F32),
           jax.ShapeDtypeStruct((nb, 8, D_MODEL), F32)),
        in_specs=[_row_spec(HEADS * LANES), _row_spec(HEADS * LANES), _row_spec(MLA_W),
                  _row_spec(SB_W), _row_spec(SB_W), _row_spec(SB_W), _row_spec(n_lat),
                  _row_spec(D_MODEL), _row_spec(D_MODEL), _mod_spec(tps),
                  _const_spec((1, D_MODEL)), _const_spec((1, D_MODEL)), _const_spec(w_in.shape),
                  _const_spec((1, Q_RANK)), _const_spec((1, KV_RANK)), _const_spec(w_uq.shape),
                  _const_spec(w_ukv.shape), _table_spec(tps), _table_spec(tps), _table_spec(tps)],
        out_specs=tuple(_row_spec(n) for n, _ in outs) + (_acc_spec(), _acc_spec(8, Q_RANK), _mod_spec(tps)),
        compiler_params=_params(("arbitrary",)),
    )(dqp, dkp, dvm, dq_sb, dk_sb, dv_sb, lat, x, dy1, mod, ln_g, ln_b, w_in, q_g, kv_g, w_uq, w_ukv,
      cos_t, sin_a, sin_b)


def _wgrad(a, b, name, pre=None, tm=512, tn=1024, tk=2048):
    rows, m = a.shape
    n = b.shape[1]
    tm, tn, tk = min(tm, m), min(tn, n), min(tk, rows)
    if m % tm:
        tm = m
    if n % tn:
        tn = n

    def body(a_ref, b_ref, o_ref):
        @pl.when(pl.program_id(2) == 0)
        def _():
            o_ref[...] = jnp.zeros_like(o_ref)

        av = a_ref[...]
        if pre == "relu2":
            av = jnp.square(jnp.maximum(av.astype(F32), 0.0))
        o_ref[...] += _dot(av.astype(BF16), b_ref[...].astype(BF16), TN)

    return pl.pallas_call(
        body, name=name, grid=(m // tm, n // tn, rows // tk),
        out_shape=jax.ShapeDtypeStruct((m, n), F32),
        in_specs=[pl.BlockSpec((tk, tm), lambda i, j, k: (k, i)), pl.BlockSpec((tk, tn), lambda i, j, k: (k, j))],
        out_specs=pl.BlockSpec((tm, tn), lambda i, j, k: (i, j)),
        compiler_params=_params(("arbitrary", "arbitrary", "arbitrary")),
    )(a, b)


def _wgrad_packed(a, b, name, block_of, row_block, split=1, pre=None, into=None, tk=2048):
    rows, m = a.shape
    n = b.shape[1]
    tm = HALF
    part = tm // split
    tk = min(tk, rows)
    shape = jax.ShapeDtypeStruct((8, GROUP_MLP[0], PACK_COLS), F32)

    def body(a_ref, b_ref, *rest):
        o_ref = rest[-1]

        @pl.when(pl.program_id(2) == 0)
        def _():
            o_ref[...] = jnp.zeros_like(o_ref)

        av = a_ref[...]
        if pre == "relu2":
            av = jnp.square(jnp.maximum(av.astype(F32), 0.0))
        prod = _dot(av.astype(BF16), b_ref[...].astype(BF16), TN)
        for s in range(split):
            o_ref[s] += prod[s * part:(s + 1) * part]

    in_specs = [pl.BlockSpec((tk, tm), lambda i, j, k: (k, i)), pl.BlockSpec((tk, SHARD), lambda i, j, k: (k, j))]
    operands = [a, b]
    if into is not None:
        in_specs.append(pl.BlockSpec(memory_space=pl.ANY))
        operands.append(into)
    return pl.pallas_call(
        body, name=name, grid=(m // tm, n // SHARD, rows // tk), out_shape=shape,
        in_specs=in_specs,
        out_specs=pl.BlockSpec((split, part, SHARD), lambda i, j, k: (block_of(i, j), row_block, 0)),
        input_output_aliases={} if into is None else {2: 0},
        compiler_params=_params(("arbitrary", "arbitrary", "arbitrary")),
    )(*operands)


def _pair(pp):
    return slice(pp * LANES, (pp + 1) * LANES)


def _head_mask(lane, hh):
    return jnp.where((lane >= 64) if hh else (lane < 64), 1.0, 0.0).astype(BF16)


def _tri(t, kind):
    s = lax.broadcasted_iota(jnp.int32, (t, t), 0)
    j = lax.broadcasted_iota(jnp.int32, (t, t), 1)
    one = jnp.where(j > s if kind == "later" else j < s, 1.0, 0.0).astype(BF16)
    return jnp.concatenate([one, one], axis=1)


def _split_dot(tri2, v):
    hi = v.astype(BF16)
    lo = (v - hi.astype(F32)).astype(BF16)
    return _dot(tri2, jnp.concatenate([hi, lo], axis=0))


def _sb_logits(z, valid):
    log_keep = -(jnp.maximum(z, 0.0) + jnp.log(1.0 + jnp.exp2(jnp.abs(z) * NEG_LOG2E)))
    log_beta = z + log_keep
    if valid is not None:
        log_keep = jnp.where(valid, log_keep, 0.0)
    return log_keep, log_beta


def _attention_call(body, ex, name, grid, operands, in_specs, out_shapes, out_specs, scratch=()):
    n_in, n_out = len(operands), len(out_shapes)
    total = grid[0] * grid[1] * grid[2]
    any_spec = pl.BlockSpec(memory_space=pl.ANY)

    def carrier(*refs):
        ins, outs, (start, middle, finish) = _carried(ex, refs, n_in, n_out, len(scratch))
        step = (pl.program_id(0) * grid[1] + pl.program_id(1)) * grid[2] + pl.program_id(2)
        pl.when(step == 0)(start)
        pl.when(step == total // 2)(middle)
        body(*ins, *outs)
        pl.when(step == total - 1)(finish)

    carried = ex is not None
    return pl.pallas_call(
        carrier if carried else body, name=name, grid=grid,
        out_shape=tuple(out_shapes) + ((ex.out_shape,) if carried else ()),
        in_specs=list(in_specs) + ([any_spec] if carried else []),
        out_specs=tuple(out_specs) + ((any_spec,) if carried else ()),
        scratch_shapes=list(scratch) + (ex.scratch if carried else []),
        compiler_params=_params(("arbitrary", "arbitrary", "arbitrary")),
    )(*operands, *([ex.operand] if carried else []))


def _sb_fwd(qkv, seq, ex=None):
    rows = qkv.shape[0]
    nb = rows // seq
    t = min(ATTN_TILE, seq)
    nq = seq // t
    assert nq <= CAR_SLOTS, (seq, t)
    ap = ATTN_PAIRS
    width = ap * LANES
    groups = SB_W // width
    hds = [(pp, hh) for pp in range(ap) for hh in range(2)]

    def body(q_ref, k_ref, v_ref, tri_ref, o_ref, car_ref, acc_ref):
        i = pl.program_id(2)
        lane = lax.broadcasted_iota(jnp.int32, (t, LANES), 1)
        key = lax.broadcasted_iota(jnp.int32, (t, t), 0)
        qry = lax.broadcasted_iota(jnp.int32, (t, t), 1)
        strict = key < qry
        tri = tri_ref[...]
        masks = [_head_mask(lane, hh) for hh in range(2)]
        qms = [q_ref[:, _pair(pp)] * masks[hh] for pp, hh in hds]
        acc_ref[...] = jnp.zeros_like(acc_ref)
        car_ref[...] = jnp.zeros_like(car_ref)

        def step(kb, c_sums, valid):
            start = pl.multiple_of(kb * t, t)
            kss = [k_ref[pl.ds(start, t), _pair(pp)] for pp in range(ap)]
            vss = [v_ref[pl.ds(start, t), _pair(pp)] for pp in range(ap)]
            zs = [_dot(kss[pp], qms[n], NT) for n, (pp, _) in enumerate(hds)]
            logs = [_sb_logits(z, valid) for z in zs]
            sufs = [_split_dot(tri, lg[0]) for lg in logs]
            new_sums = []
            for n, (pp, hh) in enumerate(hds):
                log_keep, log_beta = logs[n]
                w = jnp.exp(log_beta + sufs[n] + c_sums[n])
                if valid is not None:
                    w = jnp.where(valid, w, 0.0)
                acc_ref[pp] += _dot(vss[pp] * masks[hh], w.astype(BF16), TN)
                car_ref[0, pl.ds(n * CAR_SLOTS + kb, 1), :] = c_sums[n]
                new_sums.append(c_sums[n] + sufs[n][0:1, :] + log_keep[0:1, :])
            return tuple(new_sums)

        c_sums = step(i, tuple(jnp.zeros((1, t), F32) for _ in hds), strict)
        lax.fori_loop(0, i, lambda j, cr: step(i - 1 - j, cr, None), c_sums)
        for pp in range(ap):
            o_ref[:, _pair(pp)] = acc_ref[pp].T.astype(BF16)

    qspec = pl.BlockSpec((t, width), lambda b, p, i: (b * nq + i, p))
    car_rows = len(hds) * CAR_SLOTS
    return _attention_call(
        body, ex, "sb_fwd", (nb, groups, nq),
        [qkv, qkv, qkv, _tri(t, "later")],
        [qspec,
         pl.BlockSpec((seq, width), lambda b, p, i: (b, groups + p)),
         pl.BlockSpec((seq, width), lambda b, p, i: (b, 2 * groups + p)),
         _const_spec((t, 2 * t))],
        [jax.ShapeDtypeStruct((rows, SB_W), BF16), jax.ShapeDtypeStruct((nb * nq, HEADS * CAR_SLOTS, t), F32)],
        [qspec, pl.BlockSpec((1, car_rows, t), lambda b, p, i: (b * nq + i, p, 0))],
        scratch=[pltpu.VMEM((ap, LANES, t), F32)])


def _sb_bwd(qkv, d_out, cars, seq, ex=None):
    rows = qkv.shape[0]
    nb = rows // seq
    t = min(ATTN_TILE, seq)
    nq = seq // t
    ap = ATTN_PAIRS
    width = ap * LANES
    groups = SB_W // width
    hds = [(pp, hh) for pp in range(ap) for hh in range(2)]

    def body(q_ref, k_ref, v_ref, do_ref, car_ref, tri_ref, pre_ref, dq_ref, dk_ref, dv_ref, dq_acc):
        i = pl.program_id(2)

        @pl.when(i == 0)
        def _():
            dk_ref[...] = jnp.zeros_like(dk_ref)
            dv_ref[...] = jnp.zeros_like(dv_ref)

        lane = lax.broadcasted_iota(jnp.int32, (t, LANES), 1)
        key = lax.broadcasted_iota(jnp.int32, (t, t), 0)
        qry = lax.broadcasted_iota(jnp.int32, (t, t), 1)
        strict = key < qry
        tri, pre = tri_ref[...], pre_ref[...]
        masks = [_head_mask(lane, hh) for hh in range(2)]
        qms = [q_ref[:, _pair(pp)] * masks[hh] for pp, hh in hds]
        doms = [do_ref[:, _pair(pp)].astype(BF16) * masks[hh] for pp, hh in hds]
        dq_acc[...] = jnp.zeros_like(dq_acc)

        def step(kb, g_pres, valid):
            start = pl.multiple_of(kb * t, t)
            kss = [k_ref[pl.ds(start, t), _pair(pp)] for pp in range(ap)]
            vss = [v_ref[pl.ds(start, t), _pair(pp)] for pp in range(ap)]
            zs = [_dot(kss[pp], qms[n], NT) for n, (pp, _) in enumerate(hds)]
            dws = [_dot(vss[pp], doms[n], NT) for n, (pp, _) in enumerate(hds)]
            logs = [_sb_logits(z, valid) for z in zs]
            sufs = [_split_dot(tri, lg[0]) for lg in logs]
            ws, gs = [], []
            for n in range(len(hds)):
                c_sum = car_ref[0, pl.ds(n * CAR_SLOTS + kb, 1), :]
                w = jnp.exp(logs[n][1] + sufs[n] + c_sum)
                if valid is not None:
                    w = jnp.where(valid, w, 0.0)
                ws.append(w)
                gs.append(dws[n] * w)
            pres = [_split_dot(pre, gs[n]) for n in range(len(hds))]
            befores = [g_pres[n] + pres[n] for n in range(len(hds))]
            for pp in range(ap):
                a, b = 2 * pp, 2 * pp + 1
                dv_ref[pl.ds(start, t), _pair(pp)] += _dot(ws[a].astype(BF16), doms[a]) + _dot(ws[b].astype(BF16), doms[b])
            dzbs = []
            for n in range(len(hds)):
                beta = jnp.exp(logs[n][1])
                dz = gs[n] * (1.0 - beta) - beta * befores[n]
                if valid is not None:
                    dz = jnp.where(valid, dz, 0.0)
                dzbs.append(dz.astype(BF16))
            for pp in range(ap):
                a, b = 2 * pp, 2 * pp + 1
                dq_acc[pp] += _dot(dzbs[a], kss[pp] * masks[0], TN) + _dot(dzbs[b], kss[pp] * masks[1], TN)
                dk_ref[pl.ds(start, t), _pair(pp)] += _dot(dzbs[a], qms[a]) + _dot(dzbs[b], qms[b])
            return tuple(g_pres[n] + pres[n][t - 1:t, :] + gs[n][t - 1:t, :] for n in range(len(hds)))

        g_pres = lax.fori_loop(0, i, lambda kb, cr: step(kb, cr, None), tuple(jnp.zeros((1, t), F32) for _ in hds))
        step(i, g_pres, strict)
        for pp in range(ap):
            dq_ref[:, _pair(pp)] = (dq_acc[pp] * SB_SCALE).astype(BF16)

    qspec = pl.BlockSpec((t, width), lambda b, p, i: (b * nq + i, p))
    kspec_out = pl.BlockSpec((seq, width), lambda b, p, i: (b, p))
    car_rows = len(hds) * CAR_SLOTS
    return _attention_call(
        body, ex, "sb_bwd", (nb, groups, nq),
        [qkv, qkv, qkv, d_out, cars, _tri(t, "later"), _tri(t, "earlier")],
        [qspec,
         pl.BlockSpec((seq, width), lambda b, p, i: (b, groups + p)),
         pl.BlockSpec((seq, width), lambda b, p, i: (b, 2 * groups + p)),
         qspec, pl.BlockSpec((1, car_rows, t), lambda b, p, i: (b * nq + i, p, 0)),
         _const_spec((t, 2 * t)), _const_spec((t, 2 * t))],
        [jax.ShapeDtypeStruct((rows, SB_W), BF16), jax.ShapeDtypeStruct((rows, SB_W), F32),
         jax.ShapeDtypeStruct((rows, SB_W), F32)],
        [qspec, kspec_out, kspec_out],
        scratch=[pltpu.VMEM((ap, t, LANES), F32)])


def _mla_scores(qh, ks, allowed):
    s = _dot(qh, ks, NT) * MLA_SCALE
    if allowed is not None:
        s = jnp.where(allowed, s, jnp.finfo(F32).min)
    return s


def _mla_fwd(qp, kp, vm, seq, ex=None, chunk=64):
    rows = qp.shape[0]
    nb = rows // seq
    t = min(ATTN_TILE, seq)
    nq = seq // t
    shift = int(math.log2(chunk))
    ap = ATTN_PAIRS
    width = ap * LANES
    groups = MLA_W // width
    hds = [(pp, hh) for pp in range(ap) for hh in range(2)]

    def body(q_ref, k_ref, v_ref, o_ref, lse_ref, acc_ref):
        i = pl.program_id(2)
        lane = lax.broadcasted_iota(jnp.int32, (t, LANES), 1)
        key = lax.broadcasted_iota(jnp.int32, (t, t), 0)
        qry = lax.broadcasted_iota(jnp.int32, (t, t), 1)
        allowed_diag = jnp.right_shift(key, shift) <= jnp.right_shift(qry, shift)
        masks = [_head_mask(lane, hh) for hh in range(2)]
        qhs = [q_ref[:, _pair(n)] for n in range(len(hds))]
        acc_ref[...] = jnp.zeros_like(acc_ref)

        def step(kb, carry, allowed):
            start = pl.multiple_of(kb * t, t)
            vss = [v_ref[pl.ds(start, t), _pair(pp)] for pp in range(ap)]
            scores = [_mla_scores(k_ref[pl.ds(start, t), _pair(n)], qhs[n], allowed) for n in range(len(hds))]
            new = []
            for n, (pp, hh) in enumerate(hds):
                m_run, l_run = carry[n]
                s = scores[n]
                m_new = jnp.maximum(m_run, jnp.max(s, axis=0, keepdims=True))
                p = jnp.exp(s - m_new)
                scale = jnp.exp(m_run - m_new)
                l_run = scale * l_run + jnp.sum(p, axis=0, keepdims=True)
                acc_ref[n] = scale * acc_ref[n] + _dot(vss[pp] * masks[hh], p.astype(BF16), TN)
                new.append((m_new, l_run))
            return tuple(new)

        init = (jnp.full((1, t), jnp.finfo(F32).min, F32), jnp.zeros((1, t), F32))
        carry = step(i, tuple(init for _ in hds), allowed_diag)
        carry = lax.fori_loop(0, i, lambda kb, cr: step(kb, cr, None), carry)
        lse_rows = []
        for pp in range(ap):
            out_t = jnp.zeros((LANES, t), F32)
            for hh in range(2):
                m_run, l_run = carry[2 * pp + hh]
                out_t = out_t + acc_ref[2 * pp + hh] / l_run
                lse_rows.append(m_run + jnp.log(l_run))
            o_ref[:, _pair(pp)] = out_t.T
        lse_t = jnp.concatenate(lse_rows + [jnp.zeros((LANES - len(hds), t), F32)], axis=0)
        lse_ref[...] = jnp.zeros_like(lse_ref)
        lse_ref[:, _pair(0)] = lse_t.T

    ospec = pl.BlockSpec((t, width), lambda b, p, i: (b * nq + i, p))
    return _attention_call(
        body, ex, "mla_fwd", (nb, groups, nq), [qp, kp, vm],
        [pl.BlockSpec((t, 2 * width), lambda b, p, i: (b * nq + i, p)),
         pl.BlockSpec((seq, 2 * width), lambda b, p, i: (b, p)),
         pl.BlockSpec((seq, width), lambda b, p, i: (b, p))],
        [jax.ShapeDtypeStruct((rows, MLA_W), F32), jax.ShapeDtypeStruct((rows, MLA_W), F32)],
        [ospec, ospec], scratch=[pltpu.VMEM((len(hds), LANES, t), F32)])


def _mla_bwd(qp, kp, vm, d_out, out, lse, seq, ex=None, chunk=64):
    rows = qp.shape[0]
    nb = rows // seq
    t = min(ATTN_TILE, seq)
    nq = seq // t
    shift = int(math.log2(chunk))
    ap = ATTN_PAIRS
    width = ap * LANES
    groups = MLA_W // width
    hds = [(pp, hh) for pp in range(ap) for hh in range(2)]
    nh = len(hds)

    def body(q_ref, k_ref, v_ref, do_ref, o_ref, lse_ref, dq_ref, dk_ref, dv_ref):
        i = pl.program_id(2)

        @pl.when(i == 0)
        def _():
            dk_ref[...] = jnp.zeros_like(dk_ref)
            dv_ref[...] = jnp.zeros_like(dv_ref)

        lane = lax.broadcasted_iota(jnp.int32, (t, LANES), 1)
        key = lax.broadcasted_iota(jnp.int32, (t, t), 0)
        qry = lax.broadcasted_iota(jnp.int32, (t, t), 1)
        allowed_diag = jnp.right_shift(key, shift) <= jnp.right_shift(qry, shift)
        qhs = [q_ref[:, _pair(n)] for n in range(nh)]
        lse_t = lse_ref[:, _pair(0)].T
        doms, deltas, lse_hs = [], [], []
        for pp in range(ap):
            do = do_ref[:, _pair(pp)]
            d_o_t = (do * o_ref[:, _pair(pp)]).T
            for hh in range(2):
                doms.append(do.astype(BF16) * _head_mask(lane, hh))
                deltas.append(jnp.sum(d_o_t[hh * 64:(hh + 1) * 64], axis=0, keepdims=True))
                lse_hs.append(lse_t[2 * pp + hh:2 * pp + hh + 1])

        dq_ref[...] = jnp.zeros_like(dq_ref)

        def step(kb, allowed):
            start = pl.multiple_of(kb * t, t)
            vss = [v_ref[pl.ds(start, t), _pair(pp)] for pp in range(ap)]
            kss = [k_ref[pl.ds(start, t), _pair(n)] for n in range(nh)]
            scores = [_mla_scores(kss[n], qhs[n], allowed) for n in range(nh)]
            dps = [_dot(vss[pp], doms[n], NT) for n, (pp, _) in enumerate(hds)]
            ps = [jnp.exp(scores[n] - lse_hs[n]) for n in range(nh)]
            dss = [(ps[n] * (dps[n] - deltas[n]) * MLA_SCALE).astype(BF16) for n in range(nh)]
            for pp in range(ap):
                a, b = 2 * pp, 2 * pp + 1
                dv_ref[pl.ds(start, t), _pair(pp)] += _dot(ps[a].astype(BF16), doms[a]) + _dot(ps[b].astype(BF16), doms[b])
            for n in range(nh):
                dk_ref[pl.ds(start, t), _pair(n)] += _dot(dss[n], qhs[n])
                dq_ref[:, _pair(n)] += _dot(dss[n], kss[n], TN)

        def off_diagonal(kb, nothing):
            step(kb, None)
            return nothing

        lax.fori_loop(0, i, off_diagonal, 0)
        step(i, allowed_diag)

    ospec = pl.BlockSpec((t, width), lambda b, p, i: (b * nq + i, p))
    return _attention_call(
        body, ex, "mla_bwd", (nb, groups, nq), [qp, kp, vm, d_out, out, lse],
        [pl.BlockSpec((t, 2 * width), lambda b, p, i: (b * nq + i, p)),
         pl.BlockSpec((seq, 2 * width), lambda b, p, i: (b, p)),
         pl.BlockSpec((seq, width), lambda b, p, i: (b, p)),
         pl.BlockSpec((t, width), lambda b, p, i: (b * nq + i, groups + p)),
         ospec, ospec],
        [jax.ShapeDtypeStruct((rows, HEADS * LANES), F32), jax.ShapeDtypeStruct((rows, HEADS * LANES), F32),
         jax.ShapeDtypeStruct((rows, MLA_W), F32)],
        [pl.BlockSpec((t, 2 * width), lambda b, p, i: (b * nq + i, p)),
         pl.BlockSpec((seq, 2 * width), lambda b, p, i: (b, p)),
         pl.BlockSpec((seq, width), lambda b, p, i: (b, p))])


PACK_COLS = 1024
PACK_ALIGN = 16
GROUP_IN = (384, ((1024, 552, 1), (384, 192, 1), (256, 256, 1)))
GROUP_MLP = (1152, ((1024, 1024, 1), (1024, 1024, 0), (256, 1024, 0)))


def _pack_rows(r, c):
    return (r // 2) * c // PACK_COLS


def _slot_rows(r, c):
    return -(-_pack_rows(r, c) // PACK_ALIGN) * PACK_ALIGN


def _join_slots(parts, group):
    total, weights = group
    padded = [jnp.pad(p, ((0, 0), (0, _slot_rows(r, c) - p.shape[1]), (0, 0))) for p, (r, c, _) in zip(parts, weights)]
    used = sum(_slot_rows(r, c) for r, c, _ in weights)
    if total > used:
        padded.append(jnp.zeros((parts[0].shape[0], total - used, PACK_COLS), parts[0].dtype))
    return jnp.concatenate(padded, axis=1)


def _split_slots(packed, group):
    out, at = [], 0
    for r, c, _ in group[1]:
        out.append(packed[:, at:at + _pack_rows(r, c), :])
        at += _slot_rows(r, c)
    return out


def _pack_halves(shards, group):
    return _join_slots([s.reshape(2, _pack_rows(r, c), PACK_COLS) for s, (r, c, _) in zip(shards, group[1])], group)


def _unpack_half(packed, group):
    return [p.reshape(r // 2, c) for p, (r, c, _) in zip(_split_slots(packed[None], group), group[1])]


def _unpack_full(gathered, group):
    out = []
    for p, (r, c, axis) in zip(_split_slots(gathered, group), group[1]):
        shards = p.reshape(4, r, c)
        out.append(shards.reshape(4 * r, c) if axis == 0 else jnp.moveaxis(shards, 0, 1).reshape(r, 4 * c))
    return out


def _pack_full(grads, group):
    parts = []
    for gr, (r, c, axis) in zip(grads, group[1]):
        shards = gr.reshape(4, r, c) if axis == 0 else jnp.moveaxis(gr.reshape(r, 4, c), 1, 0)
        parts.append(shards.reshape(8, _pack_rows(r, c), PACK_COLS))
    return _join_slots(parts, group)


def _pad_w_in(w_in):
    z = jnp.zeros((D_MODEL, 1), w_in.dtype)
    return jnp.concatenate([w_in[:, :2176], jnp.tile(z, (1, 64)), w_in[:, 2176:], jnp.tile(z, (1, 32))], axis=1)


def _unpad_w_in(g):
    return jnp.concatenate([g[:, :2176], g[:, 2240:2272]], axis=1)


def _pad_heads(w, used):
    k = w.shape[0]
    w3 = w.reshape(k, HEADS, used)
    return jnp.pad(w3, ((0, 0), (0, 0), (0, LANES - used))).reshape(k, HEADS * LANES)


def _unpad_heads(g, used):
    k = g.shape[0]
    return g.reshape(k, HEADS, LANES)[:, :, :used].reshape(k, HEADS * used)


def _rope_tables(seq):
    inv_freq = 1.0 / (ROPE_BASE ** (jnp.arange(0, ROPE, 2, dtype=F32) / ROPE))
    ang = jnp.arange(seq, dtype=F32)[:, None] * inv_freq[None, :]
    cos, sin = jnp.cos(ang), jnp.sin(ang)
    one, zero = jnp.ones((seq, NOPE), F32), jnp.zeros((seq, NOPE), F32)
    z16, z32 = jnp.zeros((seq, 16), F32), jnp.zeros((seq, 32), F32)
    cos_t = jnp.concatenate([one, cos, cos, jnp.ones((seq, 32), F32)], axis=1)
    sin_a = jnp.concatenate([zero, -sin, z16, z32], axis=1)
    sin_b = jnp.concatenate([zero, z16, sin, z32], axis=1)
    return cos_t, sin_a, sin_b


SMALL = (("ln_in_g", 1024), ("ln_in_b", 1024), ("b_ada", 6144), ("q_norm_g", 384), ("kv_norm_g", 256),
         ("ln1_g", 1024), ("ln1_b", 1024), ("ln2_g", 1024), ("ln2_b", 1024))
SUBLANES = 8
SMALL_SLOTS = [-(-n // LANES // SUBLANES) * SUBLANES for _, n in SMALL]
SMALL_AT = [sum(SMALL_SLOTS[:p]) for p in range(len(SMALL))]
SMALL_ROWS = sum(SMALL_SLOTS)


def _pack_small(vals):
    parts = []
    for v, slot in zip(vals, SMALL_SLOTS):
        rows = v.reshape(-1, LANES)
        parts.append(jnp.pad(rows, ((0, slot - rows.shape[0]), (0, 0))))
    return jnp.concatenate(parts, axis=0)


def kernel(x, c, ln_in_g, ln_in_b, w_ada, b_ada, w_in, q_norm_g, kv_norm_g, w_uq, w_ukv, w_o, ln1_g, ln1_b, w_up, w_down, ln2_g, ln2_b, loss_target, m_ln_in_g, m_ln_in_b, m_w_ada, m_b_ada, m_w_in, m_q_norm_g, m_kv_norm_g, m_w_uq, m_w_ukv, m_w_o, m_ln1_g, m_ln1_b, m_w_up, m_w_down, m_ln2_g, m_ln2_b, v_ln_in_g, v_ln_in_b, v_w_ada, v_b_ada, v_w_in, v_q_norm_g, v_kv_norm_g, v_w_uq, v_w_ukv, v_w_o, v_ln1_g, v_ln1_b, v_w_up, v_w_down, v_ln2_g, v_ln2_b):
    nb, seq, _ = x.shape
    rows = nb * seq
    ix, iy, ic = lax.axis_index("x"), lax.axis_index("y"), lax.axis_index("c")
    chip = 2 * ix + iy
    dev = 2 * chip + ic

    def my_half(shards, group):
        packed = _pack_halves([s.astype(BF16) for s in shards], group)
        return lax.dynamic_index_in_dim(packed, ic, 0, keepdims=False)

    f_in, f_uq, f_ukv = _unpack_full(_gather8(my_half([w_in[0], w_uq[0], w_ukv[0]], GROUP_IN), "gather_w_in"),
                                     GROUP_IN)
    half_mlp = my_half([w_up[0], w_down[0], w_o[0]], GROUP_MLP)
    late_weights = _gather_exchange(half_mlp)
    w_in_p = _pad_w_in(f_in)
    uq3 = f_uq.reshape(Q_RANK, HEADS, NOPE + ROPE)
    w_uq_p = jnp.pad(uq3, ((0, 0), (0, 0), (0, LANES - NOPE - ROPE))).reshape(Q_RANK, HEADS * LANES)
    w_ukv_p = jnp.concatenate([_pad_heads(f_ukv[:, :HEADS * NOPE], NOPE), f_ukv[:, HEADS * NOPE:]], axis=1)

    n_all = 8 * nb
    c_all = _gather8(c.reshape(-1, LANES), "gather_c").reshape(n_all, D_MODEL)
    ada_cols = w_ada.shape[2]
    b_sh = lax.dynamic_slice_in_dim(b_ada, chip * ada_cols, ada_cols, axis=1)
    mod_sh = _ada_fwd(c_all, w_ada[0], b_sh)
    mod_g = _gather8(mod_sh, "gather_mod")[0::2]
    mod_all = jnp.moveaxis(mod_g, 0, 1).reshape(n_all, N_MOD * D_MODEL)
    mod_mine = lax.dynamic_slice_in_dim(mod_all, dev * nb, nb, axis=0).reshape(nb, N_MOD, D_MODEL)
    mod = jnp.pad(mod_mine, ((0, 0), (0, 8 - N_MOD), (0, 0)))

    cos_t, sin_a, sin_b = _rope_tables(seq)
    row2 = lambda v: v.reshape(1, -1)

    x2d = x.reshape(rows, D_MODEL)
    x0, h, qkv, lat, qp, kp, vm = _fwd_in(x2d, mod, row2(ln_in_g), row2(ln_in_b), w_in_p, q_norm_g, kv_norm_g,
                                          w_uq_p, w_ukv_p, cos_t, sin_a, sin_b, seq)
    sb_y, cars, g_mlp = _sb_fwd(qkv, seq, late_weights)
    g_mlp = _with_own(g_mlp, half_mlp)
    f_o = _split_slots(g_mlp, GROUP_MLP)[2].reshape(D_MODEL, D_MODEL)
    mla_y, lse = _mla_fwd(qp, kp, vm, seq)
    mix, y1, h2, u, ff, y2 = _fwd_out(sb_y, mla_y, x0, mod, f_o, ln1_g, ln1_b, g_mlp, seq)

    dy1, dmix, d_attn, dff, du, acc_out, dmod_a = _bwd_out(
        y2, loss_target.reshape(rows, D_MODEL), ff, u, y1, mix, mod, ln2_g, ln2_b, ln1_g, ln1_b, g_mlp, f_o, seq)
    c_idx = ic.reshape(1).astype(jnp.int32)
    blocks_mlp = _wgrad_packed(h2, du, "wgrad_up", lambda i, j: 2 * j + i, 0)
    blocks_mlp = _wgrad_packed(u, dff, "wgrad_down", lambda i, j: i, 1, pre="relu2", into=blocks_mlp)
    blocks_mlp = _wgrad_packed(sb_y, dmix, "wgrad_o_sb", lambda i, j: 0, 8, split=4, into=blocks_mlp)
    blocks_mlp = _wgrad_packed(mla_y, dmix, "wgrad_o_mla", lambda i, j: 1, 8, split=4, into=blocks_mlp)
    dq_sb, dk_sb, dv_sb, sibling_mlp = _sb_bwd(qkv, d_attn, cars, seq, _swap_cores_exchange(blocks_mlp))
    part_mlp, part_mlp_bf = _add_pairs(blocks_mlp, sibling_mlp, c_idx, "grad_add_cores_mlp")
    dqp, dkp, dvm, chips_mlp = _mla_bwd(qp, kp, vm, d_attn, mla_y, lse, seq, _scatter_chips_exchange(part_mlp_bf))
    grad_x, dproj, dqall, dkv, latn, acc0, acc_lat, dmod_c = _bwd_in(
        dqp, dkp, dvm, dq_sb, dk_sb, dv_sb, lat, x2d, dy1, mod, row2(ln_in_g), row2(ln_in_b), w_in_p,
        q_norm_g, kv_norm_g, w_uq_p, w_ukv_p, cos_t, sin_a, sin_b, seq)

    g_in = _unpad_w_in(_wgrad(h, dproj, "wgrad_in", tn=768))
    g_uq = _unpad_heads(_wgrad(latn[:, :Q_RANK], dqall, "wgrad_uq"), NOPE + ROPE)
    g_ukv_p = _wgrad(latn[:, Q_RANK:], dkv, "wgrad_ukv", tn=512)
    g_ukv = jnp.concatenate([_unpad_heads(g_ukv_p[:, :HEADS * LANES], NOPE), g_ukv_p[:, HEADS * LANES:]], axis=1)
    blocks_in = _pack_full([g_in, g_uq, g_ukv], GROUP_IN)
    sibling_in = _run_exchange(_swap_cores_exchange(blocks_in), "grads_in_to_sibling")
    part_in, part_in_bf = _add_pairs(blocks_in, sibling_in, c_idx, "grad_add_cores_in")
    chips_in = _run_exchange(_scatter_chips_exchange(part_in_bf), "grads_in_to_chips")

    def own(part):
        return lax.dynamic_index_in_dim(part, chip, 0, keepdims=False)

    half = jnp.concatenate([_add_chips(own(part_in), chips_in, "grad_add_chips_in"),
                            _add_chips(own(part_mlp), chips_mlp, "grad_add_chips_mlp")], axis=0)
    other = _run_exchange(_swap_one_exchange(half), "grads_halves")
    mine = _unpack_half(half[:GROUP_IN[0]], GROUP_IN) + _unpack_half(half[GROUP_IN[0]:], GROUP_MLP)
    theirs = _unpack_half(other[:GROUP_IN[0]], GROUP_IN) + _unpack_half(other[GROUP_IN[0]:], GROUP_MLP)

    dmod = (dmod_a + dmod_c)[:, :N_MOD, :]
    small_part = _pack_small([acc0[0], acc0[1], jnp.zeros((N_MOD * D_MODEL,), F32), acc_lat[0, :Q_RANK],
                              acc_lat[1, :KV_RANK], acc_out[3], acc_out[4], acc_out[0], acc_out[1]])
    n_sum = SMALL_ROWS + D_MODEL // LANES
    payload = jnp.concatenate([small_part, acc_out[2].reshape(-1, LANES), dmod.reshape(-1, LANES)], axis=0)
    gathered = _gather8(payload, "gather_small")
    small_sum = _sum_lead(gathered[:, :n_sum, :], "sum_small")
    loss = jnp.sum(small_sum[SMALL_ROWS:])
    dmod_all = gathered[:, n_sum:, :].reshape(n_all, N_MOD * D_MODEL)
    g_b_ada = _sum_lead(dmod_all.reshape(n_all, N_MOD * D_MODEL // LANES, LANES), "sum_b_ada")
    dmod_sh = lax.dynamic_slice_in_dim(dmod_all, chip * ada_cols, ada_cols, axis=1)
    g_w_ada = _ada_bwd(c_all, dmod_sh)

    res = {}
    d_ada, m_ada, v_ada = _adamw(w_ada[0], g_w_ada, m_w_ada[0], v_w_ada[0], "adamw_w_ada")
    res["w_ada"] = (g_w_ada[None], d_ada[None], m_ada[None], v_ada[None])
    sharded = {"w_in": (w_in, m_w_in, v_w_in), "w_uq": (w_uq, m_w_uq, v_w_uq), "w_ukv": (w_ukv, m_w_ukv, v_w_ukv),
               "w_up": (w_up, m_w_up, v_w_up), "w_down": (w_down, m_w_down, v_w_down), "w_o": (w_o, m_w_o, v_w_o)}
    for (name, (w, m, v)), g_mine, g_other in zip(sharded.items(), mine, theirs):
        quad = _adamw_halves(w[0], g_mine, g_other, m[0], v[0], c_idx, "adamw_" + name)
        res[name] = tuple(a[None] for a in quad)
    small_w = [ln_in_g, ln_in_b, b_ada, q_norm_g, kv_norm_g, ln1_g, ln1_b, ln2_g, ln2_b]
    small_m = [m_ln_in_g, m_ln_in_b, m_b_ada, m_q_norm_g, m_kv_norm_g, m_ln1_g, m_ln1_b, m_ln2_g, m_ln2_b]
    small_v = [v_ln_in_g, v_ln_in_b, v_b_ada, v_q_norm_g, v_kv_norm_g, v_ln1_g, v_ln1_b, v_ln2_g, v_ln2_b]
    for (name, _), quad in zip(SMALL, _adamw_small(small_sum, g_b_ada, small_w, small_m, small_v)):
        res[name] = quad

    order = ["ln_in_g", "ln_in_b", "w_ada", "b_ada", "w_in", "q_norm_g", "kv_norm_g", "w_uq", "w_ukv", "w_o",
             "ln1_g", "ln1_b", "w_up", "w_down", "ln2_g", "ln2_b"]
    outs = [loss, grad_x.reshape(nb, seq, D_MODEL)]
    for k in range(4):
        outs += [res[name][k] for name in order]
    return tuple(outs)
```

```python
import functools
import math

import jax
import jax.numpy as jnp
from jax import lax
from jax.experimental import pallas as pl
from jax.experimental.pallas import tpu as pltpu

F32 = jnp.float32
BF16 = jnp.bfloat16
MESH_IDS = pl.DeviceIdType.MESH

D_MODEL = 1024
HEADS = 8
HEAD_PAIRS = HEADS // 2
SB_W = 512
MLA_W = 512
NOPE = 64
ROPE = 32
Q_RANK = 384
KV_RANK = 256
D_IN = 2208
D_IN_PAD = 2304
D_FF = 4096
N_MOD = 6
LN_EPS = 1e-5
RMS_EPS = 1e-6
ALPHA = 2.0 ** 0.25
ROPE_BASE = 10000.0
SB_SCALE = 64 ** -0.5
NEG_LOG2E = -math.log2(math.e)
MLA_SCALE = 96 ** -0.5
ADAM_LR = 0.001
ADAM_B1 = 0.9
ADAM_B2 = 0.999
ADAM_EPS = 1e-08
ADAM_WD = 0.01
ADAM_STEP = 10

LANES = 128
ROW_TILE = 256
ATTN_TILE = 256
CAR_SLOTS = 8
ATTN_PAIRS = 4
VMEM_LIMIT = 56 << 20

NT = (((1,), (1,)), ((), ()))
TN = (((0,), (0,)), ((), ()))


def _params(sem=None):
    return pltpu.CompilerParams(vmem_limit_bytes=VMEM_LIMIT, dimension_semantics=sem)


def _const_spec(shape):
    zeros = (0,) * len(shape)
    return pl.BlockSpec(shape, lambda *_: zeros, pipeline_mode=pl.Buffered(1))


def _dot(a, b, dims=None):
    if dims is None:
        return jnp.dot(a, b, preferred_element_type=F32)
    return lax.dot_general(a, b, dims, preferred_element_type=F32)


def _mean(v):
    return jnp.mean(v, axis=-1, keepdims=True)


def _rowsum(v):
    return jnp.sum(v, axis=0, keepdims=True)


def _ln_fwd(y, g, b):
    mu = _mean(y)
    yc = y - mu
    rstd = lax.rsqrt(_mean(yc * yc) + LN_EPS)
    xhat = yc * rstd
    return xhat * g + b, xhat, rstd


def _ln_bwd(dx, xhat, rstd, g):
    dxh = dx * g
    return rstd * (dxh - _mean(dxh) - xhat * _mean(dxh * xhat))


def _rope(v, cos, sin_a, sin_b):
    return v * cos + pltpu.roll(v, 112, 1) * sin_a + pltpu.roll(v, 16, 1) * sin_b


def _rope_t(dv, cos, sin_a, sin_b):
    return dv * cos + pltpu.roll(dv * sin_a, 16, 1) + pltpu.roll(dv * sin_b, 112, 1)


def _my_place():
    return lax.axis_index("x"), lax.axis_index("y"), lax.axis_index("c")


class _Exchange:
    def __init__(self, operand, out_shape, n_copies, phases):
        self.operand = operand
        self.out_shape = out_shape
        self.phases = phases
        self.scratch = [pltpu.SemaphoreType.DMA((n_copies,)), pltpu.SemaphoreType.DMA((n_copies,))]


def _run_exchange(ex, name):
    def body(in_ref, out_ref, send_sems, recv_sems):
        for phase in ex.phases(in_ref, out_ref, send_sems, recv_sems):
            phase()

    return pl.pallas_call(
        body, name=name, out_shape=ex.out_shape,
        in_specs=[pl.BlockSpec(memory_space=pl.ANY)], out_specs=pl.BlockSpec(memory_space=pl.ANY),
        scratch_shapes=ex.scratch,
    )(ex.operand)


def _nothing():
    pass


def _gather_exchange(v):
    m, n = v.shape

    def phases(v_ref, out_ref, send_sems, recv_sems):
        x, y, c = _my_place()
        me, sibling = (x, y, c), (x, y, 1 - c)
        chips = [(1 - x, y), (x, 1 - y), (1 - x, 1 - y)]

        def rows(px, py, pc):
            return out_ref.at[4 * px + 2 * py + pc]

        def copy(k, block, to, src=None):
            return pltpu.make_async_remote_copy(
                src_ref=rows(*block) if src is None else src, dst_ref=rows(*block),
                send_sem=send_sems.at[k], recv_sem=recv_sems.at[k], device_id=to, device_id_type=MESH_IDS)

        first = [copy(0, me, sibling, src=v_ref)]
        first += [copy(1 + j, me, (*chip, c), src=v_ref) for j, chip in enumerate(chips)]
        passed = [copy(4 + j, (*chip, c), sibling) for j, chip in enumerate(chips)]

        def start():
            for cp in first:
                cp.start()

        def middle():
            for j, chip in enumerate(chips):
                copy(1 + j, (*chip, c), me).wait_recv()
                passed[j].start()

        def finish():
            copy(0, sibling, me).wait_recv()
            for j, chip in enumerate(chips):
                copy(4 + j, (*chip, 1 - c), me).wait_recv()
            for cp in first + passed:
                cp.wait_send()

        return start, middle, finish

    return _Exchange(v, jax.ShapeDtypeStruct((8, m, n), v.dtype), 7, phases)


def _with_own(gathered, v):
    dev = 4 * lax.axis_index("x") + 2 * lax.axis_index("y") + lax.axis_index("c")
    return lax.dynamic_update_index_in_dim(gathered, v, dev, 0)


def _direct_exchange(operand, out_shape, n_copies, make_copies):
    def phases(in_ref, out_ref, send_sems, recv_sems):
        copies = make_copies(in_ref, out_ref, send_sems, recv_sems)

        def start():
            for cp in copies:
                cp.start()

        def finish():
            for cp in copies:
                cp.wait()

        return start, _nothing, finish

    return _Exchange(operand, out_shape, n_copies, phases)


def _swap_cores_exchange(blocks):
    _, m, n = blocks.shape

    def make_copies(g_ref, out_ref, send_sems, recv_sems):
        x, y, c = _my_place()
        return [pltpu.make_async_remote_copy(
            src_ref=g_ref.at[2 * j + (1 - c)], dst_ref=out_ref.at[j],
            send_sem=send_sems.at[j], recv_sem=recv_sems.at[j],
            device_id=(x, y, 1 - c), device_id_type=MESH_IDS) for j in range(4)]

    return _direct_exchange(blocks, jax.ShapeDtypeStruct((4, m, n), blocks.dtype), 4, make_copies)


def _scatter_chips_exchange(parts):
    _, m, n = parts.shape
    flips = [(1, 0), (0, 1), (1, 1)]

    def make_copies(p_ref, out_ref, send_sems, recv_sems):
        x, y, c = _my_place()
        copies = []
        for k, (fx, fy) in enumerate(flips):
            tx = 1 - x if fx else x
            ty = 1 - y if fy else y
            copies.append(pltpu.make_async_remote_copy(
                src_ref=p_ref.at[2 * tx + ty], dst_ref=out_ref.at[k],
                send_sem=send_sems.at[k], recv_sem=recv_sems.at[k],
                device_id=(tx, ty, c), device_id_type=MESH_IDS))
        return copies

    return _direct_exchange(parts, jax.ShapeDtypeStruct((3, m, n), parts.dtype), 3, make_copies)


def _swap_one_exchange(v):
    def make_copies(v_ref, out_ref, send_sems, recv_sems):
        x, y, c = _my_place()
        return [pltpu.make_async_remote_copy(src_ref=v_ref, dst_ref=out_ref, send_sem=send_sems.at[0],
                                             recv_sem=recv_sems.at[0], device_id=(x, y, 1 - c),
                                             device_id_type=MESH_IDS)]

    return _direct_exchange(v, jax.ShapeDtypeStruct(v.shape, v.dtype), 1, make_copies)


def _gather8(v, name):
    return _with_own(_run_exchange(_gather_exchange(v), name), v)


def _carried(ex, refs, n_in, n_out, n_scratch):
    ins, ex_in = refs[:n_in], refs[n_in]
    outs, ex_out = refs[n_in + 1:n_in + 1 + n_out], refs[n_in + 1 + n_out]
    at = n_in + 2 + n_out
    return ins, outs + refs[at:at + n_scratch], ex.phases(ex_in, ex_out, *refs[at + n_scratch:])


def _ada_fwd(c_all, w_ada_sh, b_ada_sh):
    nb, cols = c_all.shape[0], w_ada_sh.shape[1]
    tn = 512

    def body(c_ref, w_ref, b_ref, o_ref):
        cv = c_ref[...]
        act = (cv * jax.nn.sigmoid(cv)).astype(BF16)
        o_ref[...] = _dot(act, w_ref[...].astype(BF16)) + b_ref[...]

    return pl.pallas_call(
        body, name="ada_fwd", grid=(cols // tn,),
        out_shape=jax.ShapeDtypeStruct((nb, cols), F32),
        in_specs=[pl.BlockSpec((nb, D_MODEL), lambda j: (0, 0)),
                  pl.BlockSpec((D_MODEL, tn), lambda j: (0, j)),
                  pl.BlockSpec((1, tn), lambda j: (0, j))],
        out_specs=pl.BlockSpec((nb, tn), lambda j: (0, j)),
        compiler_params=_params(("arbitrary",)),
    )(c_all, w_ada_sh, b_ada_sh)


def _ada_bwd(c_all, dmod_sh):
    nb, cols = dmod_sh.shape
    tn = 512

    def body(c_ref, d_ref, o_ref):
        cv = c_ref[...]
        act = (cv * jax.nn.sigmoid(cv)).astype(BF16)
        o_ref[...] = _dot(act, d_ref[...].astype(BF16), TN)

    return pl.pallas_call(
        body, name="ada_bwd", grid=(cols // tn,),
        out_shape=jax.ShapeDtypeStruct((D_MODEL, cols), F32),
        in_specs=[pl.BlockSpec((nb, D_MODEL), lambda j: (0, 0)),
                  pl.BlockSpec((nb, tn), lambda j: (0, j))],
        out_specs=pl.BlockSpec((D_MODEL, tn), lambda j: (0, j)),
        compiler_params=_params(("arbitrary",)),
    )(c_all, dmod_sh)


def _sum_lead(v, name):
    k, m, n = v.shape

    def body(v_ref, o_ref):
        acc = v_ref[0]
        for i in range(1, k):
            acc = acc + v_ref[i]
        o_ref[...] = acc

    return pl.pallas_call(
        body, name=name, out_shape=jax.ShapeDtypeStruct((m, n), F32),
        in_specs=[pl.BlockSpec((k, m, n), lambda: (0, 0, 0))],
        out_specs=pl.BlockSpec((m, n), lambda: (0, 0)),
        compiler_params=_params(),
    )(v)


def _adamw_math(w, g, m, v):
    mn = ADAM_B1 * m + (1.0 - ADAM_B1) * g
    vn = ADAM_B2 * v + (1.0 - ADAM_B2) * (g * g)
    m_hat = mn / (1.0 - ADAM_B1 ** ADAM_STEP)
    v_hat = vn / (1.0 - ADAM_B2 ** ADAM_STEP)
    return -ADAM_LR * (m_hat / (jnp.sqrt(v_hat) + ADAM_EPS) + ADAM_WD * w), mn, vn


def _adamw_small(g_sum, g_b_ada, ws, ms, vs):
    n = len(SMALL)

    def body(gs_ref, gb_ref, *refs):
        outs = refs[3 * n:]
        for p in range(n):
            rows_p = SMALL[p][1] // LANES
            g = gb_ref[...] if SMALL[p][0] == "b_ada" else gs_ref[SMALL_AT[p]:SMALL_AT[p] + rows_p, :]
            d, mn, vn = _adamw_math(refs[p][...], g, refs[n + p][...], refs[2 * n + p][...])
            outs[4 * p][...] = g
            outs[4 * p + 1][...] = d
            outs[4 * p + 2][...] = mn
            outs[4 * p + 3][...] = vn

    shapes = [jax.ShapeDtypeStruct((size // LANES, LANES), F32) for _, size in SMALL for _ in range(4)]
    flat = lambda arrs: [a.reshape(-1, LANES) for a in arrs]
    res = pl.pallas_call(body, name="adamw_small", out_shape=tuple(shapes), compiler_params=_params())(
        g_sum, g_b_ada, *flat(ws), *flat(ms), *flat(vs))
    return [tuple(r.reshape(w.shape) for r in res[4 * p:4 * p + 4]) for p, w in enumerate(ws)]


def _adamw_halves(w, g_mine, g_other, m, v, c_idx, name):
    r, cols = w.shape
    half = r // 2
    tr = half
    while tr * cols * 4 > (2 << 20) and tr % 16 == 0:
        tr //= 2

    def body(c_ref, w_ref, mine_ref, other_ref, m_ref, v_ref, g_ref, d_ref, mo_ref, vo_ref):
        g = jnp.where(pl.program_id(0) == c_ref[0], mine_ref[...], other_ref[...])
        g_ref[0] = g
        d_ref[0], mo_ref[0], vo_ref[0] = _adamw_math(w_ref[0], g, m_ref[0], v_ref[0])

    full = pl.BlockSpec((1, tr, cols), lambda h, i, c: (h, i, 0))
    part = pl.BlockSpec((tr, cols), lambda h, i, c: (i, 0))
    shape = jax.ShapeDtypeStruct((2, half, cols), F32)
    grid_spec = pltpu.PrefetchScalarGridSpec(
        num_scalar_prefetch=1, grid=(2, half // tr),
        in_specs=[full, part, part, full, full], out_specs=(full, full, full, full))
    split = lambda a: a.reshape(2, half, cols)
    res = pl.pallas_call(
        body, name=name, grid_spec=grid_spec, out_shape=(shape, shape, shape, shape),
        compiler_params=_params(("arbitrary", "arbitrary")),
    )(c_idx, split(w), g_mine, g_other, split(m), split(v))
    return tuple(a.reshape(r, cols) for a in res)


def _adamw(w, g, m, v, name):
    rows, cols = w.shape
    tr = rows
    while tr * cols * 4 > (2 << 20) and tr % 16 == 0:
        tr //= 2

    def body(w_ref, g_ref, m_ref, v_ref, d_ref, mo_ref, vo_ref):
        d_ref[...], mo_ref[...], vo_ref[...] = _adamw_math(w_ref[...], g_ref[...], m_ref[...], v_ref[...])

    spec = pl.BlockSpec((tr, cols), lambda i: (i, 0))
    shape = jax.ShapeDtypeStruct((rows, cols), F32)
    return pl.pallas_call(
        body, name=name, grid=(rows // tr,), out_shape=(shape, shape, shape),
        in_specs=[spec, spec, spec, spec], out_specs=(spec, spec, spec),
        compiler_params=_params(("arbitrary",)),
    )(w, g, m, v)


def _add_rows(m, n):
    fits = [d for d in range(16, m + 1, 16) if m % d == 0 and d * n * 4 <= (5 << 19)]
    assert fits, (m, n)
    return max(fits)


def _add_pairs(blocks, recv, c_idx, name):
    _, m, n = blocks.shape
    tr = _add_rows(m, n)

    def body(c_ref, a_ref, b_ref, o_ref, ob_ref):
        s = a_ref[...] + b_ref[...]
        o_ref[...] = s
        ob_ref[...] = s.astype(BF16)

    grid_spec = pltpu.PrefetchScalarGridSpec(
        num_scalar_prefetch=1, grid=(4, m // tr),
        in_specs=[pl.BlockSpec((1, tr, n), lambda j, i, c: (2 * j + c[0], i, 0)),
                  pl.BlockSpec((1, tr, n), lambda j, i, c: (j, i, 0))],
        out_specs=(pl.BlockSpec((1, tr, n), lambda j, i, c: (j, i, 0)),
                   pl.BlockSpec((1, tr, n), lambda j, i, c: (j, i, 0))))
    return pl.pallas_call(
        body, name=name, grid_spec=grid_spec,
        out_shape=(jax.ShapeDtypeStruct((4, m, n), F32), jax.ShapeDtypeStruct((4, m, n), BF16)),
        compiler_params=_params(("arbitrary", "arbitrary")),
    )(c_idx, blocks, recv)


def _add_chips(own, recv, name):
    m, n = own.shape
    tr = _add_rows(m, n)

    def body(a_ref, r_ref, o_ref):
        acc = a_ref[...]
        for k in range(3):
            acc = acc + r_ref[k].astype(F32)
        o_ref[...] = acc

    return pl.pallas_call(
        body, name=name, grid=(m // tr,),
        out_shape=jax.ShapeDtypeStruct((m, n), F32),
        in_specs=[pl.BlockSpec((tr, n), lambda i: (i, 0)), pl.BlockSpec((3, tr, n), lambda i: (0, i, 0))],
        out_specs=pl.BlockSpec((tr, n), lambda i: (i, 0)),
        compiler_params=_params(("arbitrary",)),
    )(own, recv)


def _row_spec(cols):
    return pl.BlockSpec((ROW_TILE, cols), lambda i: (i, 0))


def _mod_spec(tiles_per_seq):
    return pl.BlockSpec((1, 8, D_MODEL), lambda i: (i // tiles_per_seq, 0, 0))


def _table_spec(tiles_per_seq):
    return pl.BlockSpec((ROW_TILE, LANES), lambda i: (i % tiles_per_seq, 0))


def _fwd_in(x, mod, ln_g, ln_b, w_in, q_g, kv_g, w_uq, w_ukv, cos_t, sin_a, sin_b, seq):
    rows = x.shape[0]
    tm = ROW_TILE
    tps = seq // tm

    def body(x_ref, mod_ref, g_ref, b_ref, win_ref, qg_ref, kvg_ref, wuq_ref, wukv_ref, cos_ref, sa_ref, sb_ref,
             x0_ref, h_ref, qkv_ref, lat_ref, qp_ref, kp_ref, vm_ref):
        x0, _, _ = _ln_fwd(x_ref[...], g_ref[...], b_ref[...])
        x0_ref[...] = x0
        h = (x0 * (1.0 + mod_ref[0, 1:2, :]) + mod_ref[0, 0:1, :]).astype(BF16)
        h_ref[...] = h
        proj = _dot(h, win_ref[...])
        qkv_ref[:, :SB_W] = (proj[:, :SB_W] * SB_SCALE).astype(BF16)
        qkv_ref[:, SB_W:] = proj[:, SB_W:3 * SB_W].astype(BF16)
        lat_ref[...] = proj[:, 3 * SB_W:3 * SB_W + Q_RANK + KV_RANK]
        cq = proj[:, 3 * SB_W:3 * SB_W + Q_RANK]
        ckv = proj[:, 3 * SB_W + Q_RANK:3 * SB_W + Q_RANK + KV_RANK]
        kr = proj[:, D_IN_PAD - LANES:]
        cos, sa, sb = cos_ref[...], sa_ref[...], sb_ref[...]
        cqn = (cq * lax.rsqrt(_mean(cq * cq) + RMS_EPS) * qg_ref[...]).astype(BF16)
        q_all = _dot(cqn, wuq_ref[...])
        for hd in range(HEADS):
            sl = slice(hd * LANES, (hd + 1) * LANES)
            qp_ref[:, sl] = _rope(q_all[:, sl], cos, sa, sb).astype(BF16)
        ckvn = (ckv * lax.rsqrt(_mean(ckv * ckv) + RMS_EPS) * kvg_ref[...]).astype(BF16)
        kv = _dot(ckvn, wukv_ref[...])
        kr_rot = _rope(kr, cos, sa, sb)
        for hd in range(HEADS):
            sl = slice(hd * LANES, (hd + 1) * LANES)
            kp_ref[:, sl] = (kv[:, sl] + kr_rot).astype(BF16)
        vm_ref[...] = kv[:, HEADS * LANES:].astype(BF16)

    outs = [(D_MODEL, F32), (D_MODEL, BF16), (3 * SB_W, BF16), (Q_RANK + KV_RANK, F32),
            (HEADS * LANES, BF16), (HEADS * LANES, BF16), (MLA_W, BF16)]
    return pl.pallas_call(
        body, name="fwd_in", grid=(rows // tm,),
        out_shape=tuple(jax.ShapeDtypeStruct((rows, n), dt) for n, dt in outs),
        in_specs=[_row_spec(D_MODEL), _mod_spec(tps), _const_spec((1, D_MODEL)), _const_spec((1, D_MODEL)),
                  _const_spec(w_in.shape), _const_spec((1, Q_RANK)), _const_spec((1, KV_RANK)),
                  _const_spec(w_uq.shape), _const_spec(w_ukv.shape),
                  _table_spec(tps), _table_spec(tps), _table_spec(tps)],
        out_specs=tuple(_row_spec(n) for n, _ in outs),
        compiler_params=_params(("arbitrary",)),
    )(x, mod, ln_g, ln_b, w_in, q_g, kv_g, w_uq, w_ukv, cos_t, sin_a, sin_b)


HALF = 512
SHARD = 1024


def _mlp_weight_specs():
    return [pl.BlockSpec((8, HALF, SHARD), lambda i: (0, 0, 0), pipeline_mode=pl.Buffered(1)),
            pl.BlockSpec((8, HALF, SHARD), lambda i: (0, 1, 0), pipeline_mode=pl.Buffered(1))]


def _fwd_out(sb_y, mla_y, x0, mod, w_o, ln_g, ln_b, g_mlp, seq):
    rows = x0.shape[0]
    tm = ROW_TILE
    tps = seq // tm

    def body(sb_ref, ml_ref, x0_ref, mod_ref, wo_ref, g_ref, b_ref, wu_ref, wd_ref,
             mix_ref, y1_ref, h2_ref, u_ref, ff_ref, y2_ref):
        mix = _dot(sb_ref[...], wo_ref[:SB_W, :]) + _dot(ml_ref[...].astype(BF16), wo_ref[SB_W:, :])
        mix_ref[...] = mix
        y1 = ALPHA * x0_ref[...] + (1.0 + mod_ref[0, 2:3, :]) * mix
        y1_ref[...] = y1
        x1, _, _ = _ln_fwd(y1, g_ref[...], b_ref[...])
        h2 = (x1 * (1.0 + mod_ref[0, 4:5, :]) + mod_ref[0, 3:4, :]).astype(BF16)
        h2_ref[...] = h2
        h_lo, h_hi = h2[:, :HALF], h2[:, HALF:]
        ff = jnp.zeros((tm, D_MODEL), F32)
        for chip in range(4):
            u = _dot(h_lo, wu_ref[2 * chip]) + _dot(h_hi, wu_ref[2 * chip + 1])
            u_ref[:, chip * SHARD:(chip + 1) * SHARD] = u.astype(BF16)
            act = jnp.square(jnp.maximum(u, 0.0)).astype(BF16)
            ff = ff + _dot(act[:, :HALF], wd_ref[2 * chip]) + _dot(act[:, HALF:], wd_ref[2 * chip + 1])
        ff_ref[...] = ff
        y2_ref[...] = ALPHA * x1 + (1.0 + mod_ref[0, 5:6, :]) * ff

    outs = [(D_MODEL, F32), (D_MODEL, F32), (D_MODEL, BF16), (D_FF, BF16), (D_MODEL, F32), (D_MODEL, F32)]
    return pl.pallas_call(
        body, name="fwd_out", grid=(rows // tm,),
        out_shape=tuple(jax.ShapeDtypeStruct((rows, n), dt) for n, dt in outs),
        in_specs=[_row_spec(SB_W), _row_spec(MLA_W), _row_spec(D_MODEL), _mod_spec(tps), _const_spec(w_o.shape),
                  _const_spec((1, D_MODEL)), _const_spec((1, D_MODEL))] + _mlp_weight_specs(),
        out_specs=tuple(_row_spec(n) for n, _ in outs),
        compiler_params=_params(("arbitrary",)),
    )(sb_y, mla_y, x0, mod, w_o, ln_g, ln_b, g_mlp, g_mlp)


def _acc_spec(rows=8, cols=D_MODEL):
    return pl.BlockSpec((rows, cols), lambda i: (0, 0))


def _bwd_out(y2, tgt, ff, u, y1, mix, mod, ln2_g, ln2_b, ln1_g, ln1_b, g_mlp, w_o, seq):
    rows = y2.shape[0]
    nb = rows // seq
    tm = ROW_TILE
    tps = seq // tm

    def body(y2_ref, t_ref, ff_ref, u_ref, y1_ref, mix_ref, mod_ref, g2_ref, b2_ref, g_ref, b_ref, wu_ref, wd_ref,
             wo_ref, dy1_ref, dmix_ref, do_ref, dff_ref, du_ref, acc_ref, dmod_ref):
        i = pl.program_id(0)

        @pl.when(i == 0)
        def _():
            acc_ref[...] = jnp.zeros_like(acc_ref)

        @pl.when(i % tps == 0)
        def _():
            dmod_ref[...] = jnp.zeros_like(dmod_ref)

        g2 = g2_ref[...]
        x2, xhat2, rstd2 = _ln_fwd(y2_ref[...], g2, b2_ref[...])
        err = x2 - t_ref[...]
        dx2 = err * (1.0 / D_MODEL)
        acc_ref[0:1, :] += _rowsum(dx2 * xhat2)
        acc_ref[1:2, :] += _rowsum(dx2)
        acc_ref[2:3, :] += _rowsum(err * err) * (0.5 / D_MODEL)
        dy2 = _ln_bwd(dx2, xhat2, rstd2, g2)
        dmod_ref[0, 5:6, :] += _rowsum(dy2 * ff_ref[...])
        dff = ((1.0 + mod_ref[0, 5:6, :]) * dy2).astype(BF16)
        dff_ref[...] = dff
        for blk in range(8):
            cols = slice(blk * HALF, (blk + 1) * HALF)
            da = _dot(dff, wd_ref[blk], NT)
            du_ref[:, cols] = (da * (2.0 * jnp.maximum(u_ref[:, cols].astype(F32), 0.0))).astype(BF16)

        g = g_ref[...]
        x1, xhat, rstd = _ln_fwd(y1_ref[...], g, b_ref[...])
        halves = []
        for half in range(2):
            acc = jnp.zeros((tm, HALF), F32)
            for chip in range(4):
                acc = acc + _dot(du_ref[:, chip * SHARD:(chip + 1) * SHARD], wu_ref[2 * chip + half], NT)
            halves.append(acc)
        dh2 = jnp.concatenate(halves, axis=1)
        dmod_ref[0, 3:4, :] += _rowsum(dh2)
        dmod_ref[0, 4:5, :] += _rowsum(dh2 * x1)
        dx1 = ALPHA * dy2 + dh2 * (1.0 + mod_ref[0, 4:5, :])
        acc_ref[3:4, :] += _rowsum(dx1 * xhat)
        acc_ref[4:5, :] += _rowsum(dx1)
        dy1 = _ln_bwd(dx1, xhat, rstd, g)
        dy1_ref[...] = dy1
        dmod_ref[0, 2:3, :] += _rowsum(dy1 * mix_ref[...])
        dmix = ((1.0 + mod_ref[0, 2:3, :]) * dy1).astype(BF16)
        dmix_ref[...] = dmix
        do_ref[...] = _dot(dmix, wo_ref[...], NT)

    outs = [(D_MODEL, F32), (D_MODEL, BF16), (D_MODEL, F32), (D_MODEL, BF16), (D_FF, BF16)]
    return pl.pallas_call(
        body, name="bwd_out", grid=(rows // tm,),
        out_shape=tuple(jax.ShapeDtypeStruct((rows, n), dt) for n, dt in outs)
        + (jax.ShapeDtypeStruct((8, D_MODEL), F32), jax.ShapeDtypeStruct((nb, 8, D_MODEL), F32)),
        in_specs=[_row_spec(D_MODEL), _row_spec(D_MODEL), _row_spec(D_MODEL), _row_spec(D_FF), _row_spec(D_MODEL),
                  _row_spec(D_MODEL), _mod_spec(tps), _const_spec((1, D_MODEL)), _const_spec((1, D_MODEL)),
                  _const_spec((1, D_MODEL)), _const_spec((1, D_MODEL))] + _mlp_weight_specs()
        + [_const_spec(w_o.shape)],
        out_specs=tuple(_row_spec(n) for n, _ in outs) + (_acc_spec(), _mod_spec(tps)),
        compiler_params=_params(("arbitrary",)),
    )(y2, tgt, ff, u, y1, mix, mod, ln2_g, ln2_b, ln1_g, ln1_b, g_mlp, g_mlp, w_o)


def _bwd_in(dqp, dkp, dvm, dq_sb, dk_sb, dv_sb, lat, x, dy1, mod, ln_g, ln_b, w_in, q_g, kv_g, w_uq, w_ukv,
            cos_t, sin_a, sin_b, seq):
    rows = x.shape[0]
    nb = rows // seq
    tm = ROW_TILE
    tps = seq // tm
    n_lat = Q_RANK + KV_RANK

    def body(dqp_ref, dkp_ref, dvm_ref, dqs_ref, dks_ref, dvs_ref, lat_ref, x_ref, dy1_ref, mod_ref,
             g_ref, b_ref, win_ref, qg_ref, kvg_ref, wuq_ref, wukv_ref, cos_ref, sa_ref, sb_ref,
             dx_ref, dproj_ref, dqall_ref, dkv_ref, latn_ref, acc_ref, accl_ref, dmod_ref):
        i = pl.program_id(0)

        @pl.when(i == 0)
        def _():
            acc_ref[...] = jnp.zeros_like(acc_ref)
            accl_ref[...] = jnp.zeros_like(accl_ref)

        @pl.when(i % tps == 0)
        def _():
            dmod_ref[...] = jnp.zeros_like(dmod_ref)

        cos, sa, sb = cos_ref[...], sa_ref[...], sb_ref[...]
        lane = lax.broadcasted_iota(jnp.int32, (tm, LANES), 1)
        for hd in range(HEADS):
            sl = slice(hd * LANES, (hd + 1) * LANES)
            dqall_ref[:, sl] = _rope_t(dqp_ref[:, sl], cos, sa, sb).astype(BF16)
        dcqn = _dot(dqall_ref[...], wuq_ref[...], NT)
        cq = lat_ref[:, :Q_RANK]
        qg = qg_ref[...]
        rq = lax.rsqrt(_mean(cq * cq) + RMS_EPS)
        cqn = cq * rq
        latn_ref[:, :Q_RANK] = (cqn * qg).astype(BF16)
        accl_ref[0:1, :Q_RANK] += _rowsum(dcqn * cqn)
        dqg = dcqn * qg
        dcq = rq * (dqg - cqn * _mean(dqg * cqn))
        dkr = jnp.zeros((tm, LANES), F32)
        for hd in range(HEADS):
            sl = slice(hd * LANES, (hd + 1) * LANES)
            dk = dkp_ref[:, sl]
            dkr = dkr + dk
            dkv_ref[:, sl] = jnp.where(lane < NOPE, dk, 0.0).astype(BF16)
        dkv_ref[:, HEADS * LANES:] = dvm_ref[...].astype(BF16)
        dckvn = _dot(dkv_ref[...], wukv_ref[...], NT)
        ckv = lat_ref[:, Q_RANK:]
        kvg = kvg_ref[...]
        rkv = lax.rsqrt(_mean(ckv * ckv) + RMS_EPS)
        ckvn = ckv * rkv
        latn_ref[:, Q_RANK:] = (ckvn * kvg).astype(BF16)
        accl_ref[1:2, :KV_RANK] += _rowsum(dckvn * ckvn)
        dkg = dckvn * kvg
        dckv = rkv * (dkg - ckvn * _mean(dkg * ckvn))
        dkr = _rope_t(jnp.where(lane >= NOPE, dkr, 0.0), cos, sa, sb)
        dproj_ref[:, :SB_W] = dqs_ref[...]
        dproj_ref[:, SB_W:2 * SB_W] = dks_ref[...].astype(BF16)
        dproj_ref[:, 2 * SB_W:3 * SB_W] = dvs_ref[...].astype(BF16)
        dproj_ref[:, 3 * SB_W:3 * SB_W + Q_RANK] = dcq.astype(BF16)
        dproj_ref[:, 3 * SB_W + Q_RANK:3 * SB_W + n_lat] = dckv.astype(BF16)
        dproj_ref[:, D_IN_PAD - LANES:] = dkr.astype(BF16)
        dh = _dot(dproj_ref[...], win_ref[...], NT)
        g = g_ref[...]
        x0, xhat, rstd = _ln_fwd(x_ref[...], g, b_ref[...])
        dmod_ref[0, 0:1, :] += _rowsum(dh)
        dmod_ref[0, 1:2, :] += _rowsum(dh * x0)
        dx0 = ALPHA * dy1_ref[...] + dh * (1.0 + mod_ref[0, 1:2, :])
        acc_ref[0:1, :] += _rowsum(dx0 * xhat)
        acc_ref[1:2, :] += _rowsum(dx0)
        dx_ref[...] = _ln_bwd(dx0, xhat, rstd, g)

    outs = [(D_MODEL, F32), (D_IN_PAD, BF16), (HEADS * LANES, BF16), (HEADS * LANES + MLA_W, BF16), (n_lat, BF16)]
    return pl.pallas_call(
        body, name="bwd_in", grid=(rows // tm,),
        out_shape=tuple(jax.ShapeDtypeStruct((rows, n), dt) for n, dt in outs)
        + (jax.ShapeDtypeStruct((8, D_MODEL), F32), jax.ShapeDtypeStruct((8, Q_RANK), F32),
           jax.ShapeDtypeStruct((nb, 8, D_MODEL), F32)),
        in_specs=[_row_spec(HEADS * LANES), _row_spec(HEADS * LANES), _row_spec(MLA_W),
                  _row_spec(SB_W), _row_spec(SB_W), _row_spec(SB_W), _row_spec(n_lat),
                  _row_spec(D_MODEL), _row_spec(D_MODEL), _mod_spec(tps),
                  _const_spec((1, D_MODEL)), _const_spec((1, D_MODEL)), _const_spec(w_in.shape),
                  _const_spec((1, Q_RANK)), _const_spec((1, KV_RANK)), _const_spec(w_uq.shape),
                  _const_spec(w_ukv.shape), _table_spec(tps), _table_spec(tps), _table_spec(tps)],
        out_specs=tuple(_row_spec(n) for n, _ in outs) + (_acc_spec(), _acc_spec(8, Q_RANK), _mod_spec(tps)),
        compiler_params=_params(("arbitrary",)),
    )(dqp, dkp, dvm, dq_sb, dk_sb, dv_sb, lat, x, dy1, mod, ln_g, ln_b, w_in, q_g, kv_g, w_uq, w_ukv,
      cos_t, sin_a, sin_b)


def _wgrad(a, b, name, pre=None, tm=512, tn=1024, tk=2048):
    rows, m = a.shape
    n = b.shape[1]
    tm, tn, tk = min(tm, m), min(tn, n), min(tk, rows)
    if m % tm:
        tm = m
    if n % tn:
        tn = n

    def body(a_ref, b_ref, o_ref):
        @pl.when(pl.program_id(2) == 0)
        def _():
            o_ref[...] = jnp.zeros_like(o_ref)

        av = a_ref[...]
        if pre == "relu2":
            av = jnp.square(jnp.maximum(av.astype(F32), 0.0))
        o_ref[...] += _dot(av.astype(BF16), b_ref[...].astype(BF16), TN)

    return pl.pallas_call(
        body, name=name, grid=(m // tm, n // tn, rows // tk),
        out_shape=jax.ShapeDtypeStruct((m, n), F32),
        in_specs=[pl.BlockSpec((tk, tm), lambda i, j, k: (k, i)), pl.BlockSpec((tk, tn), lambda i, j, k: (k, j))],
        out_specs=pl.BlockSpec((tm, tn), lambda i, j, k: (i, j)),
        compiler_params=_params(("arbitrary", "arbitrary", "arbitrary")),
    )(a, b)


def _wgrad_packed(a, b, name, block_of, row_block, split=1, pre=None, into=None, tk=2048):
    rows, m = a.shape
    n = b.shape[1]
    tm = HALF
    part = tm // split
    tk = min(tk, rows)
    shape = jax.ShapeDtypeStruct((8, GROUP_MLP[0], PACK_COLS), F32)

    def body(a_ref, b_ref, *rest):
        o_ref = rest[-1]

        @pl.when(pl.program_id(2) == 0)
        def _():
            o_ref[...] = jnp.zeros_like(o_ref)

        av = a_ref[...]
        if pre == "relu2":
            av = jnp.square(jnp.maximum(av.astype(F32), 0.0))
        prod = _dot(av.astype(BF16), b_ref[...].astype(BF16), TN)
        for s in range(split):
            o_ref[s] += prod[s * part:(s + 1) * part]

    in_specs = [pl.BlockSpec((tk, tm), lambda i, j, k: (k, i)), pl.BlockSpec((tk, SHARD), lambda i, j, k: (k, j))]
    operands = [a, b]
    if into is not None:
        in_specs.append(pl.BlockSpec(memory_space=pl.ANY))
        operands.append(into)
    return pl.pallas_call(
        body, name=name, grid=(m // tm, n // SHARD, rows // tk), out_shape=shape,
        in_specs=in_specs,
        out_specs=pl.BlockSpec((split, part, SHARD), lambda i, j, k: (block_of(i, j), row_block, 0)),
        input_output_aliases={} if into is None else {2: 0},
        compiler_params=_params(("arbitrary", "arbitrary", "arbitrary")),
    )(*operands)


def _pair(pp):
    return slice(pp * LANES, (pp + 1) * LANES)


def _head_mask(lane, hh):
    return jnp.where((lane >= 64) if hh else (lane < 64), 1.0, 0.0).astype(BF16)


def _tri(t, kind):
    s = lax.broadcasted_iota(jnp.int32, (t, t), 0)
    j = lax.broadcasted_iota(jnp.int32, (t, t), 1)
    one = jnp.where(j > s if kind == "later" else j < s, 1.0, 0.0).astype(BF16)
    return jnp.concatenate([one, one], axis=1)


def _split_dot(tri2, v):
    hi = v.astype(BF16)
    lo = (v - hi.astype(F32)).astype(BF16)
    return _dot(tri2, jnp.concatenate([hi, lo], axis=0))


def _sb_logits(z, valid):
    log_keep = -(jnp.maximum(z, 0.0) + jnp.log(1.0 + jnp.exp2(jnp.abs(z) * NEG_LOG2E)))
    log_beta = z + log_keep
    if valid is not None:
        log_keep = jnp.where(valid, log_keep, 0.0)
    return log_keep, log_beta


def _attention_call(body, ex, name, grid, operands, in_specs, out_shapes, out_specs, scratch=()):
    n_in, n_out = len(operands), len(out_shapes)
    total = grid[0] * grid[1] * grid[2]
    any_spec = pl.BlockSpec(memory_space=pl.ANY)

    def carrier(*refs):
        ins, outs, (start, middle, finish) = _carried(ex, refs, n_in, n_out, len(scratch))
        step = (pl.program_id(0) * grid[1] + pl.program_id(1)) * grid[2] + pl.program_id(2)
        pl.when(step == 0)(start)
        pl.when(step == total // 2)(middle)
        body(*ins, *outs)
        pl.when(step == total - 1)(finish)

    carried = ex is not None
    return pl.pallas_call(
        carrier if carried else body, name=name, grid=grid,
        out_shape=tuple(out_shapes) + ((ex.out_shape,) if carried else ()),
        in_specs=list(in_specs) + ([any_spec] if carried else []),
        out_specs=tuple(out_specs) + ((any_spec,) if carried else ()),
        scratch_shapes=list(scratch) + (ex.scratch if carried else []),
        compiler_params=_params(("arbitrary", "arbitrary", "arbitrary")),
    )(*operands, *([ex.operand] if carried else []))


def _sb_fwd(qkv, seq, ex=None):
    rows = qkv.shape[0]
    nb = rows // seq
    t = min(ATTN_TILE, seq)
    nq = seq // t
    assert nq <= CAR_SLOTS, (seq, t)
    ap = ATTN_PAIRS
    width = ap * LANES
    groups = SB_W // width
    hds = [(pp, hh) for pp in range(ap) for hh in range(2)]

    def body(q_ref, k_ref, v_ref, tri_ref, o_ref, car_ref, acc_ref):
        i = pl.program_id(2)
        lane = lax.broadcasted_iota(jnp.int32, (t, LANES), 1)
        key = lax.broadcasted_iota(jnp.int32, (t, t), 0)
        qry = lax.broadcasted_iota(jnp.int32, (t, t), 1)
        strict = key < qry
        tri = tri_ref[...]
        masks = [_head_mask(lane, hh) for hh in range(2)]
        qms = [q_ref[:, _pair(pp)] * masks[hh] for pp, hh in hds]
        acc_ref[...] = jnp.zeros_like(acc_ref)
        car_ref[...] = jnp.zeros_like(car_ref)

        def step(kb, c_sums, valid):
            start = pl.multiple_of(kb * t, t)
            kss = [k_ref[pl.ds(start, t), _pair(pp)] for pp in range(ap)]
            vss = [v_ref[pl.ds(start, t), _pair(pp)] for pp in range(ap)]
            zs = [_dot(kss[pp], qms[n], NT) for n, (pp, _) in enumerate(hds)]
            logs = [_sb_logits(z, valid) for z in zs]
            sufs = [_split_dot(tri, lg[0]) for lg in logs]
            new_sums = []
            for n, (pp, hh) in enumerate(hds):
                log_keep, log_beta = logs[n]
                w = jnp.exp(log_beta + sufs[n] + c_sums[n])
                if valid is not None:
                    w = jnp.where(valid, w, 0.0)
                acc_ref[pp] += _dot(vss[pp] * masks[hh], w.astype(BF16), TN)
                car_ref[0, pl.ds(n * CAR_SLOTS + kb, 1), :] = c_sums[n]
                new_sums.append(c_sums[n] + sufs[n][0:1, :] + log_keep[0:1, :])
            return tuple(new_sums)

        c_sums = step(i, tuple(jnp.zeros((1, t), F32) for _ in hds), strict)
        lax.fori_loop(0, i, lambda j, cr: step(i - 1 - j, cr, None), c_sums)
        for pp in range(ap):
            o_ref[:, _pair(pp)] = acc_ref[pp].T.astype(BF16)

    qspec = pl.BlockSpec((t, width), lambda b, p, i: (b * nq + i, p))
    car_rows = len(hds) * CAR_SLOTS
    return _attention_call(
        body, ex, "sb_fwd", (nb, groups, nq),
        [qkv, qkv, qkv, _tri(t, "later")],
        [qspec,
         pl.BlockSpec((seq, width), lambda b, p, i: (b, groups + p)),
         pl.BlockSpec((seq, width), lambda b, p, i: (b, 2 * groups + p)),
         _const_spec((t, 2 * t))],
        [jax.ShapeDtypeStruct((rows, SB_W), BF16), jax.ShapeDtypeStruct((nb * nq, HEADS * CAR_SLOTS, t), F32)],
        [qspec, pl.BlockSpec((1, car_rows, t), lambda b, p, i: (b * nq + i, p, 0))],
        scratch=[pltpu.VMEM((ap, LANES, t), F32)])


def _sb_bwd(qkv, d_out, cars, seq, ex=None):
    rows = qkv.shape[0]
    nb = rows // seq
    t = min(ATTN_TILE, seq)
    nq = seq // t
    ap = ATTN_PAIRS
    width = ap * LANES
    groups = SB_W // width
    hds = [(pp, hh) for pp in range(ap) for hh in range(2)]

    def body(q_ref, k_ref, v_ref, do_ref, car_ref, tri_ref, pre_ref, dq_ref, dk_ref, dv_ref, dq_acc):
        i = pl.program_id(2)

        @pl.when(i == 0)
        def _():
            dk_ref[...] = jnp.zeros_like(dk_ref)
            dv_ref[...] = jnp.zeros_like(dv_ref)

        lane = lax.broadcasted_iota(jnp.int32, (t, LANES), 1)
        key = lax.broadcasted_iota(jnp.int32, (t, t), 0)
        qry = lax.broadcasted_iota(jnp.int32, (t, t), 1)
        strict = key < qry
        tri, pre = tri_ref[...], pre_ref[...]
        masks = [_head_mask(lane, hh) for hh in range(2)]
        qms = [q_ref[:, _pair(pp)] * masks[hh] for pp, hh in hds]
        doms = [do_ref[:, _pair(pp)].astype(BF16) * masks[hh] for pp, hh in hds]
        dq_acc[...] = jnp.zeros_like(dq_acc)

        def step(kb, g_pres, valid):
            start = pl.multiple_of(kb * t, t)
            kss = [k_ref[pl.ds(start, t), _pair(pp)] for pp in range(ap)]
            vss = [v_ref[pl.ds(start, t), _pair(pp)] for pp in range(ap)]
            zs = [_dot(kss[pp], qms[n], NT) for n, (pp, _) in enumerate(hds)]
            dws = [_dot(vss[pp], doms[n], NT) for n, (pp, _) in enumerate(hds)]
            logs = [_sb_logits(z, valid) for z in zs]
            sufs = [_split_dot(tri, lg[0]) for lg in logs]
            ws, gs = [], []
            for n in range(len(hds)):
                c_sum = car_ref[0, pl.ds(n * CAR_SLOTS + kb, 1), :]
                w = jnp.exp(logs[n][1] + sufs[n] + c_sum)
                if valid is not None:
                    w = jnp.where(valid, w, 0.0)
                ws.append(w)
                gs.append(dws[n] * w)
            pres = [_split_dot(pre, gs[n]) for n in range(len(hds))]
            befores = [g_pres[n] + pres[n] for n in range(len(hds))]
            for pp in range(ap):
                a, b = 2 * pp, 2 * pp + 1
                dv_ref[pl.ds(start, t), _pair(pp)] += _dot(ws[a].astype(BF16), doms[a]) + _dot(ws[b].astype(BF16), doms[b])
            dzbs = []
            for n in range(len(hds)):
                beta = jnp.exp(logs[n][1])
                dz = gs[n] * (1.0 - beta) - beta * befores[n]
                if valid is not None:
                    dz = jnp.where(valid, dz, 0.0)
                dzbs.append(dz.astype(BF16))
            for pp in range(ap):
                a, b = 2 * pp, 2 * pp + 1
                dq_acc[pp] += _dot(dzbs[a], kss[pp] * masks[0], TN) + _dot(dzbs[b], kss[pp] * masks[1], TN)
                dk_ref[pl.ds(start, t), _pair(pp)] += _dot(dzbs[a], qms[a]) + _dot(dzbs[b], qms[b])
            return tuple(g_pres[n] + pres[n][t - 1:t, :] + gs[n][t - 1:t, :] for n in range(len(hds)))

        g_pres = lax.fori_loop(0, i, lambda kb, cr: step(kb, cr, None), tuple(jnp.zeros((1, t), F32) for _ in hds))
        step(i, g_pres, strict)
        for pp in range(ap):
            dq_ref[:, _pair(pp)] = (dq_acc[pp] * SB_SCALE).astype(BF16)

    qspec = pl.BlockSpec((t, width), lambda b, p, i: (b * nq + i, p))
    kspec_out = pl.BlockSpec((seq, width), lambda b, p, i: (b, p))
    car_rows = len(hds) * CAR_SLOTS
    return _attention_call(
        body, ex, "sb_bwd", (nb, groups, nq),
        [qkv, qkv, qkv, d_out, cars, _tri(t, "later"), _tri(t, "earlier")],
        [qspec,
         pl.BlockSpec((seq, width), lambda b, p, i: (b, groups + p)),
         pl.BlockSpec((seq, width), lambda b, p, i: (b, 2 * groups + p)),
         qspec, pl.BlockSpec((1, car_rows, t), lambda b, p, i: (b * nq + i, p, 0)),
         _const_spec((t, 2 * t)), _const_spec((t, 2 * t))],
        [jax.ShapeDtypeStruct((rows, SB_W), BF16), jax.ShapeDtypeStruct((rows, SB_W), F32),
         jax.ShapeDtypeStruct((rows, SB_W), F32)],
        [qspec, kspec_out, kspec_out],
        scratch=[pltpu.VMEM((ap, t, LANES), F32)])


def _mla_scores(ks, qh, allowed):
    s = _dot(ks, qh, NT) * (MLA_SCALE * -NEG_LOG2E)
    if allowed is not None:
        s = jnp.where(allowed, s, jnp.finfo(F32).min)
    return s


def _mla_fwd(qp, kp, vm, seq, ex=None, chunk=64):
    rows = qp.shape[0]
    nb = rows // seq
    t = min(ATTN_TILE, seq)
    nq = seq // t
    shift = int(math.log2(chunk))
    ap = ATTN_PAIRS
    width = ap * LANES
    groups = MLA_W // width
    hds = [(pp, hh) for pp in range(ap) for hh in range(2)]

    def body(q_ref, k_ref, v_ref, o_ref, lse_ref, acc_ref):
        i = pl.program_id(2)
        lane = lax.broadcasted_iota(jnp.int32, (t, LANES), 1)
        key = lax.broadcasted_iota(jnp.int32, (t, t), 0)
        qry = lax.broadcasted_iota(jnp.int32, (t, t), 1)
        allowed_diag = jnp.right_shift(key, shift) <= jnp.right_shift(qry, shift)
        masks = [_head_mask(lane, hh) for hh in range(2)]
        qhs = [q_ref[:, _pair(n)] for n in range(len(hds))]
        acc_ref[...] = jnp.zeros_like(acc_ref)

        def step(kb, carry, allowed):
            start = pl.multiple_of(kb * t, t)
            vss = [v_ref[pl.ds(start, t), _pair(pp)] for pp in range(ap)]
            scores = [_mla_scores(k_ref[pl.ds(start, t), _pair(n)], qhs[n], allowed) for n in range(len(hds))]
            new = []
            for n, (pp, hh) in enumerate(hds):
                m_run, l_run = carry[n]
                s = scores[n]
                m_new = jnp.maximum(m_run, jnp.max(s, axis=0, keepdims=True))
                p = jnp.exp2(s - m_new)
                scale = jnp.exp2(m_run - m_new)
                l_run = scale * l_run + jnp.sum(p, axis=0, keepdims=True)
                acc_ref[n] = scale * acc_ref[n] + _dot(vss[pp] * masks[hh], p.astype(BF16), TN)
                new.append((m_new, l_run))
            return tuple(new)

        init = (jnp.full((1, t), jnp.finfo(F32).min, F32), jnp.zeros((1, t), F32))
        carry = step(i, tuple(init for _ in hds), allowed_diag)
        carry = lax.fori_loop(0, i, lambda kb, cr: step(kb, cr, None), carry)
        lse_rows = []
        for pp in range(ap):
            out_t = jnp.zeros((LANES, t), F32)
            for hh in range(2):
                m_run, l_run = carry[2 * pp + hh]
                out_t = out_t + acc_ref[2 * pp + hh] / l_run
                lse_rows.append(m_run + jnp.log(l_run) * -NEG_LOG2E)
            o_ref[:, _pair(pp)] = out_t.T
        lse_t = jnp.concatenate(lse_rows + [jnp.zeros((LANES - len(hds), t), F32)], axis=0)
        lse_ref[...] = jnp.zeros_like(lse_ref)
        lse_ref[:, _pair(0)] = lse_t.T

    ospec = pl.BlockSpec((t, width), lambda b, p, i: (b * nq + i, p))
    return _attention_call(
        body, ex, "mla_fwd", (nb, groups, nq), [qp, kp, vm],
        [pl.BlockSpec((t, 2 * width), lambda b, p, i: (b * nq + i, p)),
         pl.BlockSpec((seq, 2 * width), lambda b, p, i: (b, p)),
         pl.BlockSpec((seq, width), lambda b, p, i: (b, p))],
        [jax.ShapeDtypeStruct((rows, MLA_W), F32), jax.ShapeDtypeStruct((rows, MLA_W), F32)],
        [ospec, ospec], scratch=[pltpu.VMEM((len(hds), LANES, t), F32)])


def _mla_bwd(qp, kp, vm, d_out, out, lse, seq, ex=None, chunk=64):
    rows = qp.shape[0]
    nb = rows // seq
    t = min(ATTN_TILE, seq)
    nq = seq // t
    shift = int(math.log2(chunk))
    ap = ATTN_PAIRS
    width = ap * LANES
    groups = MLA_W // width
    hds = [(pp, hh) for pp in range(ap) for hh in range(2)]
    nh = len(hds)

    def body(q_ref, k_ref, v_ref, do_ref, o_ref, lse_ref, dq_ref, dk_ref, dv_ref):
        i = pl.program_id(2)

        @pl.when(i == 0)
        def _():
            dk_ref[...] = jnp.zeros_like(dk_ref)
            dv_ref[...] = jnp.zeros_like(dv_ref)

        lane = lax.broadcasted_iota(jnp.int32, (t, LANES), 1)
        key = lax.broadcasted_iota(jnp.int32, (t, t), 0)
        qry = lax.broadcasted_iota(jnp.int32, (t, t), 1)
        allowed_diag = jnp.right_shift(key, shift) <= jnp.right_shift(qry, shift)
        qhs = [q_ref[:, _pair(n)] for n in range(nh)]
        lse_t = lse_ref[:, _pair(0)].T
        doms, deltas, lse_hs = [], [], []
        for pp in range(ap):
            do = do_ref[:, _pair(pp)]
            d_o_t = (do * o_ref[:, _pair(pp)]).T
            for hh in range(2):
                doms.append(do.astype(BF16) * _head_mask(lane, hh))
                deltas.append(jnp.sum(d_o_t[hh * 64:(hh + 1) * 64], axis=0, keepdims=True))
                lse_hs.append(lse_t[2 * pp + hh:2 * pp + hh + 1])

        dq_ref[...] = jnp.zeros_like(dq_ref)

        def step(kb, allowed):
            start = pl.multiple_of(kb * t, t)
            vss = [v_ref[pl.ds(start, t), _pair(pp)] for pp in range(ap)]
            kss = [k_ref[pl.ds(start, t), _pair(n)] for n in range(nh)]
            scores = [_mla_scores(kss[n], qhs[n], allowed) for n in range(nh)]
            dps = [_dot(vss[pp], doms[n], NT) for n, (pp, _) in enumerate(hds)]
            ps = [jnp.exp2(scores[n] - lse_hs[n]) for n in range(nh)]
            dss = [(ps[n] * (dps[n] - deltas[n]) * MLA_SCALE).astype(BF16) for n in range(nh)]
            for pp in range(ap):
                a, b = 2 * pp, 2 * pp + 1
                dv_ref[pl.ds(start, t), _pair(pp)] += _dot(ps[a].astype(BF16), doms[a]) + _dot(ps[b].astype(BF16), doms[b])
            for n in range(nh):
                dk_ref[pl.ds(start, t), _pair(n)] += _dot(dss[n], qhs[n])
                dq_ref[:, _pair(n)] += _dot(dss[n], kss[n], TN)

        def off_diagonal(kb, nothing):
            step(kb, None)
            return nothing

        lax.fori_loop(0, i, off_diagonal, 0)
        step(i, allowed_diag)

    ospec = pl.BlockSpec((t, width), lambda b, p, i: (b * nq + i, p))
    return _attention_call(
        body, ex, "mla_bwd", (nb, groups, nq), [qp, kp, vm, d_out, out, lse],
        [pl.BlockSpec((t, 2 * width), lambda b, p, i: (b * nq + i, p)),
         pl.BlockSpec((seq, 2 * width), lambda b, p, i: (b, p)),
         pl.BlockSpec((seq, width), lambda b, p, i: (b, p)),
         pl.BlockSpec((t, width), lambda b, p, i: (b * nq + i, groups + p)),
         ospec, ospec],
        [jax.ShapeDtypeStruct((rows, HEADS * LANES), F32), jax.ShapeDtypeStruct((rows, HEADS * LANES), F32),
         jax.ShapeDtypeStruct((rows, MLA_W), F32)],
        [pl.BlockSpec((t, 2 * width), lambda b, p, i: (b * nq + i, p)),
         pl.BlockSpec((seq, 2 * width), lambda b, p, i: (b, p)),
         pl.BlockSpec((seq, width), lambda b, p, i: (b, p))])


PACK_COLS = 1024
PACK_ALIGN = 16
GROUP_IN = (384, ((1024, 552, 1), (384, 192, 1), (256, 256, 1)))
GROUP_MLP = (1152, ((1024, 1024, 1), (1024, 1024, 0), (256, 1024, 0)))


def _pack_rows(r, c):
    return (r // 2) * c // PACK_COLS


def _slot_rows(r, c):
    return -(-_pack_rows(r, c) // PACK_ALIGN) * PACK_ALIGN


def _join_slots(parts, group):
    total, weights = group
    padded = [jnp.pad(p, ((0, 0), (0, _slot_rows(r, c) - p.shape[1]), (0, 0))) for p, (r, c, _) in zip(parts, weights)]
    used = sum(_slot_rows(r, c) for r, c, _ in weights)
    if total > used:
        padded.append(jnp.zeros((parts[0].shape[0], total - used, PACK_COLS), parts[0].dtype))
    return jnp.concatenate(padded, axis=1)


def _split_slots(packed, group):
    out, at = [], 0
    for r, c, _ in group[1]:
        out.append(packed[:, at:at + _pack_rows(r, c), :])
        at += _slot_rows(r, c)
    return out


def _pack_halves(shards, group):
    return _join_slots([s.reshape(2, _pack_rows(r, c), PACK_COLS) for s, (r, c, _) in zip(shards, group[1])], group)


def _unpack_half(packed, group):
    return [p.reshape(r // 2, c) for p, (r, c, _) in zip(_split_slots(packed[None], group), group[1])]


def _unpack_full(gathered, group):
    out = []
    for p, (r, c, axis) in zip(_split_slots(gathered, group), group[1]):
        shards = p.reshape(4, r, c)
        out.append(shards.reshape(4 * r, c) if axis == 0 else jnp.moveaxis(shards, 0, 1).reshape(r, 4 * c))
    return out


def _pack_full(grads, group):
    parts = []
    for gr, (r, c, axis) in zip(grads, group[1]):
        shards = gr.reshape(4, r, c) if axis == 0 else jnp.moveaxis(gr.reshape(r, 4, c), 1, 0)
        parts.append(shards.reshape(8, _pack_rows(r, c), PACK_COLS))
    return _join_slots(parts, group)


def _pad_w_in(w_in):
    z = jnp.zeros((D_MODEL, 1), w_in.dtype)
    return jnp.concatenate([w_in[:, :2176], jnp.tile(z, (1, 64)), w_in[:, 2176:], jnp.tile(z, (1, 32))], axis=1)


def _unpad_w_in(g):
    return jnp.concatenate([g[:, :2176], g[:, 2240:2272]], axis=1)


def _pad_heads(w, used):
    k = w.shape[0]
    w3 = w.reshape(k, HEADS, used)
    return jnp.pad(w3, ((0, 0), (0, 0), (0, LANES - used))).reshape(k, HEADS * LANES)


def _unpad_heads(g, used):
    k = g.shape[0]
    return g.reshape(k, HEADS, LANES)[:, :, :used].reshape(k, HEADS * used)


def _rope_tables(seq):
    inv_freq = 1.0 / (ROPE_BASE ** (jnp.arange(0, ROPE, 2, dtype=F32) / ROPE))
    ang = jnp.arange(seq, dtype=F32)[:, None] * inv_freq[None, :]
    cos, sin = jnp.cos(ang), jnp.sin(ang)
    one, zero = jnp.ones((seq, NOPE), F32), jnp.zeros((seq, NOPE), F32)
    z16, z32 = jnp.zeros((seq, 16), F32), jnp.zeros((seq, 32), F32)
    cos_t = jnp.concatenate([one, cos, cos, jnp.ones((seq, 32), F32)], axis=1)
    sin_a = jnp.concatenate([zero, -sin, z16, z32], axis=1)
    sin_b = jnp.concatenate([zero, z16, sin, z32], axis=1)
    return cos_t, sin_a, sin_b


SMALL = (("ln_in_g", 1024), ("ln_in_b", 1024), ("b_ada", 6144), ("q_norm_g", 384), ("kv_norm_g", 256),
         ("ln1_g", 1024), ("ln1_b", 1024), ("ln2_g", 1024), ("ln2_b", 1024))
SUBLANES = 8
SMALL_SLOTS = [-(-n // LANES // SUBLANES) * SUBLANES for _, n in SMALL]
SMALL_AT = [sum(SMALL_SLOTS[:p]) for p in range(len(SMALL))]
SMALL_ROWS = sum(SMALL_SLOTS)


def _pack_small(vals):
    parts = []
    for v, slot in zip(vals, SMALL_SLOTS):
        rows = v.reshape(-1, LANES)
        parts.append(jnp.pad(rows, ((0, slot - rows.shape[0]), (0, 0))))
    return jnp.concatenate(parts, axis=0)


def kernel(x, c, ln_in_g, ln_in_b, w_ada, b_ada, w_in, q_norm_g, kv_norm_g, w_uq, w_ukv, w_o, ln1_g, ln1_b, w_up, w_down, ln2_g, ln2_b, loss_target, m_ln_in_g, m_ln_in_b, m_w_ada, m_b_ada, m_w_in, m_q_norm_g, m_kv_norm_g, m_w_uq, m_w_ukv, m_w_o, m_ln1_g, m_ln1_b, m_w_up, m_w_down, m_ln2_g, m_ln2_b, v_ln_in_g, v_ln_in_b, v_w_ada, v_b_ada, v_w_in, v_q_norm_g, v_kv_norm_g, v_w_uq, v_w_ukv, v_w_o, v_ln1_g, v_ln1_b, v_w_up, v_w_down, v_ln2_g, v_ln2_b):
    nb, seq, _ = x.shape
    rows = nb * seq
    ix, iy, ic = lax.axis_index("x"), lax.axis_index("y"), lax.axis_index("c")
    chip = 2 * ix + iy
    dev = 2 * chip + ic

    def my_half(shards, group):
        packed = _pack_halves([s.astype(BF16) for s in shards], group)
        return lax.dynamic_index_in_dim(packed, ic, 0, keepdims=False)

    f_in, f_uq, f_ukv = _unpack_full(_gather8(my_half([w_in[0], w_uq[0], w_ukv[0]], GROUP_IN), "gather_w_in"),
                                     GROUP_IN)
    half_mlp = my_half([w_up[0], w_down[0], w_o[0]], GROUP_MLP)
    late_weights = _gather_exchange(half_mlp)
    w_in_p = _pad_w_in(f_in)
    uq3 = f_uq.reshape(Q_RANK, HEADS, NOPE + ROPE)
    w_uq_p = jnp.pad(uq3, ((0, 0), (0, 0), (0, LANES - NOPE - ROPE))).reshape(Q_RANK, HEADS * LANES)
    w_ukv_p = jnp.concatenate([_pad_heads(f_ukv[:, :HEADS * NOPE], NOPE), f_ukv[:, HEADS * NOPE:]], axis=1)

    n_all = 8 * nb
    c_all = _gather8(c.reshape(-1, LANES), "gather_c").reshape(n_all, D_MODEL)
    ada_cols = w_ada.shape[2]
    b_sh = lax.dynamic_slice_in_dim(b_ada, chip * ada_cols, ada_cols, axis=1)
    mod_sh = _ada_fwd(c_all, w_ada[0], b_sh)
    mod_g = _gather8(mod_sh, "gather_mod")[0::2]
    mod_all = jnp.moveaxis(mod_g, 0, 1).reshape(n_all, N_MOD * D_MODEL)
    mod_mine = lax.dynamic_slice_in_dim(mod_all, dev * nb, nb, axis=0).reshape(nb, N_MOD, D_MODEL)
    mod = jnp.pad(mod_mine, ((0, 0), (0, 8 - N_MOD), (0, 0)))

    cos_t, sin_a, sin_b = _rope_tables(seq)
    row2 = lambda v: v.reshape(1, -1)

    x2d = x.reshape(rows, D_MODEL)
    x0, h, qkv, lat, qp, kp, vm = _fwd_in(x2d, mod, row2(ln_in_g), row2(ln_in_b), w_in_p, q_norm_g, kv_norm_g,
                                          w_uq_p, w_ukv_p, cos_t, sin_a, sin_b, seq)
    sb_y, cars, g_mlp = _sb_fwd(qkv, seq, late_weights)
    g_mlp = _with_own(g_mlp, half_mlp)
    f_o = _split_slots(g_mlp, GROUP_MLP)[2].reshape(D_MODEL, D_MODEL)
    mla_y, lse = _mla_fwd(qp, kp, vm, seq)
    mix, y1, h2, u, ff, y2 = _fwd_out(sb_y, mla_y, x0, mod, f_o, ln1_g, ln1_b, g_mlp, seq)

    dy1, dmix, d_attn, dff, du, acc_out, dmod_a = _bwd_out(
        y2, loss_target.reshape(rows, D_MODEL), ff, u, y1, mix, mod, ln2_g, ln2_b, ln1_g, ln1_b, g_mlp, f_o, seq)
    c_idx = ic.reshape(1).astype(jnp.int32)
    blocks_mlp = _wgrad_packed(h2, du, "wgrad_up", lambda i, j: 2 * j + i, 0)
    blocks_mlp = _wgrad_packed(u, dff, "wgrad_down", lambda i, j: i, 1, pre="relu2", into=blocks_mlp)
    blocks_mlp = _wgrad_packed(sb_y, dmix, "wgrad_o_sb", lambda i, j: 0, 8, split=4, into=blocks_mlp)
    blocks_mlp = _wgrad_packed(mla_y, dmix, "wgrad_o_mla", lambda i, j: 1, 8, split=4, into=blocks_mlp)
    dq_sb, dk_sb, dv_sb, sibling_mlp = _sb_bwd(qkv, d_attn, cars, seq, _swap_cores_exchange(blocks_mlp))
    part_mlp, part_mlp_bf = _add_pairs(blocks_mlp, sibling_mlp, c_idx, "grad_add_cores_mlp")
    dqp, dkp, dvm, chips_mlp = _mla_bwd(qp, kp, vm, d_attn, mla_y, lse, seq, _scatter_chips_exchange(part_mlp_bf))
    grad_x, dproj, dqall, dkv, latn, acc0, acc_lat, dmod_c = _bwd_in(
        dqp, dkp, dvm, dq_sb, dk_sb, dv_sb, lat, x2d, dy1, mod, row2(ln_in_g), row2(ln_in_b), w_in_p,
        q_norm_g, kv_norm_g, w_uq_p, w_ukv_p, cos_t, sin_a, sin_b, seq)

    g_in = _unpad_w_in(_wgrad(h, dproj, "wgrad_in", tn=768))
    g_uq = _unpad_heads(_wgrad(latn[:, :Q_RANK], dqall, "wgrad_uq"), NOPE + ROPE)
    g_ukv_p = _wgrad(latn[:, Q_RANK:], dkv, "wgrad_ukv", tn=512)
    g_ukv = jnp.concatenate([_unpad_heads(g_ukv_p[:, :HEADS * LANES], NOPE), g_ukv_p[:, HEADS * LANES:]], axis=1)
    blocks_in = _pack_full([g_in, g_uq, g_ukv], GROUP_IN)
    sibling_in = _run_exchange(_swap_cores_exchange(blocks_in), "grads_in_to_sibling")
    part_in, part_in_bf = _add_pairs(blocks_in, sibling_in, c_idx, "grad_add_cores_in")
    chips_in = _run_exchange(_scatter_chips_exchange(part_in_bf), "grads_in_to_chips")

    def own(part):
        return lax.dynamic_index_in_dim(part, chip, 0, keepdims=False)

    half = jnp.concatenate([_add_chips(own(part_in), chips_in, "grad_add_chips_in"),
                            _add_chips(own(part_mlp), chips_mlp, "grad_add_chips_mlp")], axis=0)
    other = _run_exchange(_swap_one_exchange(half), "grads_halves")
    mine = _unpack_half(half[:GROUP_IN[0]], GROUP_IN) + _unpack_half(half[GROUP_IN[0]:], GROUP_MLP)
    theirs = _unpack_half(other[:GROUP_IN[0]], GROUP_IN) + _unpack_half(other[GROUP_IN[0]:], GROUP_MLP)

    dmod = (dmod_a + dmod_c)[:, :N_MOD, :]
    small_part = _pack_small([acc0[0], acc0[1], jnp.zeros((N_MOD * D_MODEL,), F32), acc_lat[0, :Q_RANK],
                              acc_lat[1, :KV_RANK], acc_out[3], acc_out[4], acc_out[0], acc_out[1]])
    n_sum = SMALL_ROWS + D_MODEL // LANES
    payload = jnp.concatenate([small_part, acc_out[2].reshape(-1, LANES), dmod.reshape(-1, LANES)], axis=0)
    gathered = _gather8(payload, "gather_small")
    small_sum = _sum_lead(gathered[:, :n_sum, :], "sum_small")
    loss = jnp.sum(small_sum[SMALL_ROWS:])
    dmod_all = gathered[:, n_sum:, :].reshape(n_all, N_MOD * D_MODEL)
    g_b_ada = _sum_lead(dmod_all.reshape(n_all, N_MOD * D_MODEL // LANES, LANES), "sum_b_ada")
    dmod_sh = lax.dynamic_slice_in_dim(dmod_all, chip * ada_cols, ada_cols, axis=1)
    g_w_ada = _ada_bwd(c_all, dmod_sh)

    res = {}
    d_ada, m_ada, v_ada = _adamw(w_ada[0], g_w_ada, m_w_ada[0], v_w_ada[0], "adamw_w_ada")
    res["w_ada"] = (g_w_ada[None], d_ada[None], m_ada[None], v_ada[None])
    sharded = {"w_in": (w_in, m_w_in, v_w_in), "w_uq": (w_uq, m_w_uq, v_w_uq), "w_ukv": (w_ukv, m_w_ukv, v_w_ukv),
               "w_up": (w_up, m_w_up, v_w_up), "w_down": (w_down, m_w_down, v_w_down), "w_o": (w_o, m_w_o, v_w_o)}
    for (name, (w, m, v)), g_mine, g_other in zip(sharded.items(), mine, theirs):
        quad = _adamw_halves(w[0], g_mine, g_other, m[0], v[0], c_idx, "adamw_" + name)
        res[name] = tuple(a[None] for a in quad)
    small_w = [ln_in_g, ln_in_b, b_ada, q_norm_g, kv_norm_g, ln1_g, ln1_b, ln2_g, ln2_b]
    small_m = [m_ln_in_g, m_ln_in_b, m_b_ada, m_q_norm_g, m_kv_norm_g, m_ln1_g, m_ln1_b, m_ln2_g, m_ln2_b]
    small_v = [v_ln_in_g, v_ln_in_b, v_b_ada, v_q_norm_g, v_kv_norm_g, v_ln1_g, v_ln1_b, v_ln2_g, v_ln2_b]
    for (name, _), quad in zip(SMALL, _adamw_small(small_sum, g_b_ada, small_w, small_m, small_v)):
        res[name] = quad

    order = ["ln_in_g", "ln_in_b", "w_ada", "b_ada", "w_in", "q_norm_g", "kv_norm_g", "w_uq", "w_ukv", "w_o",
             "ln1_g", "ln1_b", "w_up", "w_down", "ln2_g", "ln2_b"]
    outs = [loss, grad_x.reshape(nb, seq, D_MODEL)]
    for k in range(4):
        outs += [res[name][k] for name in order]
    return tuple(outs)
```

```python
import functools
import math

import jax
import jax.numpy as jnp
from jax import lax
from jax.experimental import pallas as pl
from jax.experimental.pallas import tpu as pltpu

F32 = jnp.float32
BF16 = jnp.bfloat16
MESH_IDS = pl.DeviceIdType.MESH

D_MODEL = 1024
HEADS = 8
HEAD_PAIRS = HEADS // 2
SB_W = 512
MLA_W = 512
NOPE = 64
ROPE = 32
Q_RANK = 384
KV_RANK = 256
D_IN = 2208
D_IN_PAD = 2304
D_FF = 4096
N_MOD = 6
LN_EPS = 1e-5
RMS_EPS = 1e-6
ALPHA = 2.0 ** 0.25
ROPE_BASE = 10000.0
SB_SCALE = 64 ** -0.5
NEG_LOG2E = -math.log2(math.e)
MLA_SCALE = 96 ** -0.5
ADAM_LR = 0.001
ADAM_B1 = 0.9
ADAM_B2 = 0.999
ADAM_EPS = 1e-08
ADAM_WD = 0.01
ADAM_STEP = 10

LANES = 128
ROW_TILE = 256
ATTN_TILE = 256
CAR_SLOTS = 8
ATTN_PAIRS = 4
VMEM_LIMIT = 56 << 20

NT = (((1,), (1,)), ((), ()))
TN = (((0,), (0,)), ((), ()))


def _params(sem=None):
    return pltpu.CompilerParams(vmem_limit_bytes=VMEM_LIMIT, dimension_semantics=sem)


def _const_spec(shape):
    zeros = (0,) * len(shape)
    return pl.BlockSpec(shape, lambda *_: zeros, pipeline_mode=pl.Buffered(1))


def _dot(a, b, dims=None):
    if dims is None:
        return jnp.dot(a, b, preferred_element_type=F32)
    return lax.dot_general(a, b, dims, preferred_element_type=F32)


def _mean(v):
    return jnp.mean(v, axis=-1, keepdims=True)


def _rowsum(v):
    return jnp.sum(v, axis=0, keepdims=True)


def _ln_fwd(y, g, b):
    mu = _mean(y)
    yc = y - mu
    rstd = lax.rsqrt(_mean(yc * yc) + LN_EPS)
    xhat = yc * rstd
    return xhat * g + b, xhat, rstd


def _ln_bwd(dx, xhat, rstd, g):
    dxh = dx * g
    return rstd * (dxh - _mean(dxh) - xhat * _mean(dxh * xhat))


def _rope(v, cos, sin_a, sin_b):
    return v * cos + pltpu.roll(v, 112, 1) * sin_a + pltpu.roll(v, 16, 1) * sin_b


def _rope_t(dv, cos, sin_a, sin_b):
    return dv * cos + pltpu.roll(dv * sin_a, 16, 1) + pltpu.roll(dv * sin_b, 112, 1)


def _my_place():
    return lax.axis_index("x"), lax.axis_index("y"), lax.axis_index("c")


class _Exchange:
    def __init__(self, operand, out_shape, n_copies, phases):
        self.operand = operand
        self.out_shape = out_shape
        self.phases = phases
        self.scratch = [pltpu.SemaphoreType.DMA((n_copies,)), pltpu.SemaphoreType.DMA((n_copies,))]


def _run_exchange(ex, name):
    def body(in_ref, out_ref, send_sems, recv_sems):
        for phase in ex.phases(in_ref, out_ref, send_sems, recv_sems):
            phase()

    return pl.pallas_call(
        body, name=name, out_shape=ex.out_shape,
        in_specs=[pl.BlockSpec(memory_space=pl.ANY)], out_specs=pl.BlockSpec(memory_space=pl.ANY),
        scratch_shapes=ex.scratch,
    )(ex.operand)


def _nothing():
    pass


def _gather_exchange(v):
    m, n = v.shape

    def phases(v_ref, out_ref, send_sems, recv_sems):
        x, y, c = _my_place()
        me, sibling = (x, y, c), (x, y, 1 - c)
        chips = [(1 - x, y), (x, 1 - y), (1 - x, 1 - y)]

        def rows(px, py, pc):
            return out_ref.at[4 * px + 2 * py + pc]

        def copy(k, block, to, src=None):
            return pltpu.make_async_remote_copy(
                src_ref=rows(*block) if src is None else src, dst_ref=rows(*block),
                send_sem=send_sems.at[k], recv_sem=recv_sems.at[k], device_id=to, device_id_type=MESH_IDS)

        first = [copy(0, me, sibling, src=v_ref)]
        first += [copy(1 + j, me, (*chip, c), src=v_ref) for j, chip in enumerate(chips)]
        passed = [copy(4 + j, (*chip, c), sibling) for j, chip in enumerate(chips)]

        def start():
            for cp in first:
                cp.start()

        def middle():
            for j, chip in enumerate(chips):
                copy(1 + j, (*chip, c), me).wait_recv()
                passed[j].start()

        def finish():
            copy(0, sibling, me).wait_recv()
            for j, chip in enumerate(chips):
                copy(4 + j, (*chip, 1 - c), me).wait_recv()
            for cp in first + passed:
                cp.wait_send()

        return start, middle, finish

    return _Exchange(v, jax.ShapeDtypeStruct((8, m, n), v.dtype), 7, phases)


def _with_own(gathered, v):
    dev = 4 * lax.axis_index("x") + 2 * lax.axis_index("y") + lax.axis_index("c")
    return lax.dynamic_update_index_in_dim(gathered, v, dev, 0)


def _direct_exchange(operand, out_shape, n_copies, make_copies):
    def phases(in_ref, out_ref, send_sems, recv_sems):
        copies = make_copies(in_ref, out_ref, send_sems, recv_sems)

        def start():
            for cp in copies:
                cp.start()

        def finish():
            for cp in copies:
                cp.wait()

        return start, _nothing, finish

    return _Exchange(operand, out_shape, n_copies, phases)


def _swap_cores_exchange(blocks):
    _, m, n = blocks.shape

    def make_copies(g_ref, out_ref, send_sems, recv_sems):
        x, y, c = _my_place()
        return [pltpu.make_async_remote_copy(
            src_ref=g_ref.at[2 * j + (1 - c)], dst_ref=out_ref.at[j],
            send_sem=send_sems.at[j], recv_sem=recv_sems.at[j],
            device_id=(x, y, 1 - c), device_id_type=MESH_IDS) for j in range(4)]

    return _direct_exchange(blocks, jax.ShapeDtypeStruct((4, m, n), blocks.dtype), 4, make_copies)


def _scatter_chips_exchange(parts):
    _, m, n = parts.shape
    flips = [(1, 0), (0, 1), (1, 1)]

    def make_copies(p_ref, out_ref, send_sems, recv_sems):
        x, y, c = _my_place()
        copies = []
        for k, (fx, fy) in enumerate(flips):
            tx = 1 - x if fx else x
            ty = 1 - y if fy else y
            copies.append(pltpu.make_async_remote_copy(
                src_ref=p_ref.at[2 * tx + ty], dst_ref=out_ref.at[k],
                send_sem=send_sems.at[k], recv_sem=recv_sems.at[k],
                device_id=(tx, ty, c), device_id_type=MESH_IDS))
        return copies

    return _direct_exchange(parts, jax.ShapeDtypeStruct((3, m, n), parts.dtype), 3, make_copies)


def _swap_one_exchange(v):
    def make_copies(v_ref, out_ref, send_sems, recv_sems):
        x, y, c = _my_place()
        return [pltpu.make_async_remote_copy(src_ref=v_ref, dst_ref=out_ref, send_sem=send_sems.at[0],
                                             recv_sem=recv_sems.at[0], device_id=(x, y, 1 - c),
                                             device_id_type=MESH_IDS)]

    return _direct_exchange(v, jax.ShapeDtypeStruct(v.shape, v.dtype), 1, make_copies)


def _gather8(v, name):
    return _with_own(_run_exchange(_gather_exchange(v), name), v)


def _carried(ex, refs, n_in, n_out, n_scratch):
    ins, ex_in = refs[:n_in], refs[n_in]
    outs, ex_out = refs[n_in + 1:n_in + 1 + n_out], refs[n_in + 1 + n_out]
    at = n_in + 2 + n_out
    return ins, outs + refs[at:at + n_scratch], ex.phases(ex_in, ex_out, *refs[at + n_scratch:])


def _ada_fwd(c_all, w_ada_sh, b_ada_sh):
    nb, cols = c_all.shape[0], w_ada_sh.shape[1]
    tn = 512

    def body(c_ref, w_ref, b_ref, o_ref):
        cv = c_ref[...]
        act = (cv * jax.nn.sigmoid(cv)).astype(BF16)
        o_ref[...] = _dot(act, w_ref[...].astype(BF16)) + b_ref[...]

    return pl.pallas_call(
        body, name="ada_fwd", grid=(cols // tn,),
        out_shape=jax.ShapeDtypeStruct((nb, cols), F32),
        in_specs=[pl.BlockSpec((nb, D_MODEL), lambda j: (0, 0)),
                  pl.BlockSpec((D_MODEL, tn), lambda j: (0, j)),
                  pl.BlockSpec((1, tn), lambda j: (0, j))],
        out_specs=pl.BlockSpec((nb, tn), lambda j: (0, j)),
        compiler_params=_params(("arbitrary",)),
    )(c_all, w_ada_sh, b_ada_sh)


def _ada_bwd(c_all, dmod_sh):
    nb, cols = dmod_sh.shape
    tn = 512

    def body(c_ref, d_ref, o_ref):
        cv = c_ref[...]
        act = (cv * jax.nn.sigmoid(cv)).astype(BF16)
        o_ref[...] = _dot(act, d_ref[...].astype(BF16), TN)

    return pl.pallas_call(
        body, name="ada_bwd", grid=(cols // tn,),
        out_shape=jax.ShapeDtypeStruct((D_MODEL, cols), F32),
        in_specs=[pl.BlockSpec((nb, D_MODEL), lambda j: (0, 0)),
                  pl.BlockSpec((nb, tn), lambda j: (0, j))],
        out_specs=pl.BlockSpec((D_MODEL, tn), lambda j: (0, j)),
        compiler_params=_params(("arbitrary",)),
    )(c_all, dmod_sh)


def _sum_lead(v, name):
    k, m, n = v.shape

    def body(v_ref, o_ref):
        acc = v_ref[0]
        for i in range(1, k):
            acc = acc + v_ref[i]
        o_ref[...] = acc

    return pl.pallas_call(
        body, name=name, out_shape=jax.ShapeDtypeStruct((m, n), F32),
        in_specs=[pl.BlockSpec((k, m, n), lambda: (0, 0, 0))],
        out_specs=pl.BlockSpec((m, n), lambda: (0, 0)),
        compiler_params=_params(),
    )(v)


def _adamw_math(w, g, m, v):
    mn = ADAM_B1 * m + (1.0 - ADAM_B1) * g
    vn = ADAM_B2 * v + (1.0 - ADAM_B2) * (g * g)
    m_hat = mn / (1.0 - ADAM_B1 ** ADAM_STEP)
    v_hat = vn / (1.0 - ADAM_B2 ** ADAM_STEP)
    return -ADAM_LR * (m_hat / (jnp.sqrt(v_hat) + ADAM_EPS) + ADAM_WD * w), mn, vn


def _adamw_small(g_sum, g_b_ada, ws, ms, vs):
    n = len(SMALL)

    def body(gs_ref, gb_ref, *refs):
        outs = refs[3 * n:]
        for p in range(n):
            rows_p = SMALL[p][1] // LANES
            g = gb_ref[...] if SMALL[p][0] == "b_ada" else gs_ref[SMALL_AT[p]:SMALL_AT[p] + rows_p, :]
            d, mn, vn = _adamw_math(refs[p][...], g, refs[n + p][...], refs[2 * n + p][...])
            outs[4 * p][...] = g
            outs[4 * p + 1][...] = d
            outs[4 * p + 2][...] = mn
            outs[4 * p + 3][...] = vn

    shapes = [jax.ShapeDtypeStruct((size // LANES, LANES), F32) for _, size in SMALL for _ in range(4)]
    flat = lambda arrs: [a.reshape(-1, LANES) for a in arrs]
    res = pl.pallas_call(body, name="adamw_small", out_shape=tuple(shapes), compiler_params=_params())(
        g_sum, g_b_ada, *flat(ws), *flat(ms), *flat(vs))
    return [tuple(r.reshape(w.shape) for r in res[4 * p:4 * p + 4]) for p, w in enumerate(ws)]


def _adamw_halves(w, g_mine, g_other, m, v, c_idx, name):
    r, cols = w.shape
    half = r // 2
    tr = half
    while tr * cols * 4 > (2 << 20) and tr % 16 == 0:
        tr //= 2

    def body(c_ref, w_ref, mine_ref, other_ref, m_ref, v_ref, g_ref, d_ref, mo_ref, vo_ref):
        g = jnp.where(pl.program_id(0) == c_ref[0], mine_ref[...], other_ref[...])
        g_ref[0] = g
        d_ref[0], mo_ref[0], vo_ref[0] = _adamw_math(w_ref[0], g, m_ref[0], v_ref[0])

    full = pl.BlockSpec((1, tr, cols), lambda h, i, c: (h, i, 0))
    part = pl.BlockSpec((tr, cols), lambda h, i, c: (i, 0))
    shape = jax.ShapeDtypeStruct((2, half, cols), F32)
    grid_spec = pltpu.PrefetchScalarGridSpec(
        num_scalar_prefetch=1, grid=(2, half // tr),
        in_specs=[full, part, part, full, full], out_specs=(full, full, full, full))
    split = lambda a: a.reshape(2, half, cols)
    res = pl.pallas_call(
        body, name=name, grid_spec=grid_spec, out_shape=(shape, shape, shape, shape),
        compiler_params=_params(("arbitrary", "arbitrary")),
    )(c_idx, split(w), g_mine, g_other, split(m), split(v))
    return tuple(a.reshape(r, cols) for a in res)


def _adamw(w, g, m, v, name):
    rows, cols = w.shape
    tr = rows
    while tr * cols * 4 > (2 << 20) and tr % 16 == 0:
        tr //= 2

    def body(w_ref, g_ref, m_ref, v_ref, d_ref, mo_ref, vo_ref):
        d_ref[...], mo_ref[...], vo_ref[...] = _adamw_math(w_ref[...], g_ref[...], m_ref[...], v_ref[...])

    spec = pl.BlockSpec((tr, cols), lambda i: (i, 0))
    shape = jax.ShapeDtypeStruct((rows, cols), F32)
    return pl.pallas_call(
        body, name=name, grid=(rows // tr,), out_shape=(shape, shape, shape),
        in_specs=[spec, spec, spec, spec], out_specs=(spec, spec, spec),
        compiler_params=_params(("arbitrary",)),
    )(w, g, m, v)


def _add_rows(m, n):
    fits = [d for d in range(16, m + 1, 16) if m % d == 0 and d * n * 4 <= (5 << 19)]
    assert fits, (m, n)
    return max(fits)


def _add_pairs(blocks, recv, c_idx, name):
    _, m, n = blocks.shape
    tr = _add_rows(m, n)

    def body(c_ref, a_ref, b_ref, o_ref, ob_ref):
        s = a_ref[...] + b_ref[...]
        o_ref[...] = s
        ob_ref[...] = s.astype(BF16)

    grid_spec = pltpu.PrefetchScalarGridSpec(
        num_scalar_prefetch=1, grid=(4, m // tr),
        in_specs=[pl.BlockSpec((1, tr, n), lambda j, i, c: (2 * j + c[0], i, 0)),
                  pl.BlockSpec((1, tr, n), lambda j, i, c: (j, i, 0))],
        out_specs=(pl.BlockSpec((1, tr, n), lambda j, i, c: (j, i, 0)),
                   pl.BlockSpec((1, tr, n), lambda j, i, c: (j, i, 0))))
    return pl.pallas_call(
        body, name=name, grid_spec=grid_spec,
        out_shape=(jax.ShapeDtypeStruct((4, m, n), F32), jax.ShapeDtypeStruct((4, m, n), BF16)),
        compiler_params=_params(("arbitrary", "arbitrary")),
    )(c_idx, blocks, recv)


def _add_chips(own, recv, name):
    m, n = own.shape
    tr = _add_rows(m, n)

    def body(a_ref, r_ref, o_ref):
        acc = a_ref[...]
        for k in range(3):
            acc = acc + r_ref[k].astype(F32)
        o_ref[...] = acc

    return pl.pallas_call(
        body, name=name, grid=(m // tr,),
        out_shape=jax.ShapeDtypeStruct((m, n), F32),
        in_specs=[pl.BlockSpec((tr, n), lambda i: (i, 0)), pl.BlockSpec((3, tr, n), lambda i: (0, i, 0))],
        out_specs=pl.BlockSpec((tr, n), lambda i: (i, 0)),
        compiler_params=_params(("arbitrary",)),
    )(own, recv)


def _row_spec(cols):
    return pl.BlockSpec((ROW_TILE, cols), lambda i: (i, 0))


def _mod_spec(tiles_per_seq):
    return pl.BlockSpec((1, 8, D_MODEL), lambda i: (i // tiles_per_seq, 0, 0))


def _table_spec(tiles_per_seq):
    return pl.BlockSpec((ROW_TILE, LANES), lambda i: (i % tiles_per_seq, 0))


def _fwd_in(x, mod, ln_g, ln_b, w_in, q_g, kv_g, w_uq, w_ukv, cos_t, sin_a, sin_b, seq):
    rows = x.shape[0]
    tm = min(2 * ROW_TILE, seq)
    tps = seq // tm

    def body(x_ref, mod_ref, g_ref, b_ref, win_ref, qg_ref, kvg_ref, wuq_ref, wukv_ref, cos_ref, sa_ref, sb_ref,
             x0_ref, h_ref, qkv_ref, lat_ref, qp_ref, kp_ref, vm_ref):
        def chain(rs):
            x0, _, _ = _ln_fwd(x_ref[rs, :], g_ref[...], b_ref[...])
            x0_ref[rs, :] = x0
            h = (x0 * (1.0 + mod_ref[0, 1:2, :]) + mod_ref[0, 0:1, :]).astype(BF16)
            h_ref[rs, :] = h
            yield
            proj = _dot(h, win_ref[...])
            yield
            qkv_ref[rs, :SB_W] = (proj[:, :SB_W] * SB_SCALE).astype(BF16)
            qkv_ref[rs, SB_W:] = proj[:, SB_W:3 * SB_W].astype(BF16)
            lat_ref[rs, :] = proj[:, 3 * SB_W:3 * SB_W + Q_RANK + KV_RANK]
            cq = proj[:, 3 * SB_W:3 * SB_W + Q_RANK]
            ckv = proj[:, 3 * SB_W + Q_RANK:3 * SB_W + Q_RANK + KV_RANK]
            kr = proj[:, D_IN_PAD - LANES:]
            cos, sa, sb = cos_ref[rs, :], sa_ref[rs, :], sb_ref[rs, :]
            cqn = (cq * lax.rsqrt(_mean(cq * cq) + RMS_EPS) * qg_ref[...]).astype(BF16)
            q_all = _dot(cqn, wuq_ref[...])
            ckvn = (ckv * lax.rsqrt(_mean(ckv * ckv) + RMS_EPS) * kvg_ref[...]).astype(BF16)
            kv = _dot(ckvn, wukv_ref[...])
            yield
            for hd in range(HEADS):
                sl = slice(hd * LANES, (hd + 1) * LANES)
                qp_ref[rs, sl] = _rope(q_all[:, sl], cos, sa, sb).astype(BF16)
            kr_rot = _rope(kr, cos, sa, sb)
            for hd in range(HEADS):
                sl = slice(hd * LANES, (hd + 1) * LANES)
                kp_ref[rs, sl] = (kv[:, sl] + kr_rot).astype(BF16)
            vm_ref[rs, :] = kv[:, HEADS * LANES:].astype(BF16)

        half = tm // 2
        _staggered([chain(slice(0, half)), chain(slice(half, tm))])

    row_spec = lambda cols: pl.BlockSpec((tm, cols), lambda i: (i, 0))
    table_spec = pl.BlockSpec((tm, LANES), lambda i: (i % tps, 0))
    outs = [(D_MODEL, F32), (D_MODEL, BF16), (3 * SB_W, BF16), (Q_RANK + KV_RANK, F32),
            (HEADS * LANES, BF16), (HEADS * LANES, BF16), (MLA_W, BF16)]
    return pl.pallas_call(
        body, name="fwd_in", grid=(rows // tm,),
        out_shape=tuple(jax.ShapeDtypeStruct((rows, n), dt) for n, dt in outs),
        in_specs=[row_spec(D_MODEL), pl.BlockSpec((1, 8, D_MODEL), lambda i: (i // tps, 0, 0)),
                  _const_spec((1, D_MODEL)), _const_spec((1, D_MODEL)),
                  _const_spec(w_in.shape), _const_spec((1, Q_RANK)), _const_spec((1, KV_RANK)),
                  _const_spec(w_uq.shape), _const_spec(w_ukv.shape), table_spec, table_spec, table_spec],
        out_specs=tuple(row_spec(n) for n, _ in outs),
        compiler_params=_params(("arbitrary",)),
    )(x, mod, ln_g, ln_b, w_in, q_g, kv_g, w_uq, w_ukv, cos_t, sin_a, sin_b)


HALF = 512
SHARD = 1024


def _mlp_weight_specs():
    return [pl.BlockSpec((8, HALF, SHARD), lambda i: (0, 0, 0), pipeline_mode=pl.Buffered(1)),
            pl.BlockSpec((8, HALF, SHARD), lambda i: (0, 1, 0), pipeline_mode=pl.Buffered(1))]


def _fwd_out(sb_y, mla_y, x0, mod, w_o, ln_g, ln_b, g_mlp, seq):
    rows = x0.shape[0]
    tm = ROW_TILE
    tps = seq // tm

    def body(sb_ref, ml_ref, x0_ref, mod_ref, wo_ref, g_ref, b_ref, wu_ref, wd_ref,
             mix_ref, y1_ref, h2_ref, u_ref, ff_ref, y2_ref):
        mix = _dot(sb_ref[...], wo_ref[:SB_W, :]) + _dot(ml_ref[...].astype(BF16), wo_ref[SB_W:, :])
        mix_ref[...] = mix
        y1 = ALPHA * x0_ref[...] + (1.0 + mod_ref[0, 2:3, :]) * mix
        y1_ref[...] = y1
        x1, _, _ = _ln_fwd(y1, g_ref[...], b_ref[...])
        h2 = (x1 * (1.0 + mod_ref[0, 4:5, :]) + mod_ref[0, 3:4, :]).astype(BF16)
        h2_ref[...] = h2
        h_lo, h_hi = h2[:, :HALF], h2[:, HALF:]
        ff = jnp.zeros((tm, D_MODEL), F32)
        for chip in range(4):
            u = _dot(h_lo, wu_ref[2 * chip]) + _dot(h_hi, wu_ref[2 * chip + 1])
            u_ref[:, chip * SHARD:(chip + 1) * SHARD] = u.astype(BF16)
            act = jnp.square(jnp.maximum(u, 0.0)).astype(BF16)
            ff = ff + _dot(act[:, :HALF], wd_ref[2 * chip]) + _dot(act[:, HALF:], wd_ref[2 * chip + 1])
        ff_ref[...] = ff
        y2_ref[...] = ALPHA * x1 + (1.0 + mod_ref[0, 5:6, :]) * ff

    outs = [(D_MODEL, F32), (D_MODEL, F32), (D_MODEL, BF16), (D_FF, BF16), (D_MODEL, F32), (D_MODEL, F32)]
    return pl.pallas_call(
        body, name="fwd_out", grid=(rows // tm,),
        out_shape=tuple(jax.ShapeDtypeStruct((rows, n), dt) for n, dt in outs),
        in_specs=[_row_spec(SB_W), _row_spec(MLA_W), _row_spec(D_MODEL), _mod_spec(tps), _const_spec(w_o.shape),
                  _const_spec((1, D_MODEL)), _const_spec((1, D_MODEL))] + _mlp_weight_specs(),
        out_specs=tuple(_row_spec(n) for n, _ in outs),
        compiler_params=_params(("arbitrary",)),
    )(sb_y, mla_y, x0, mod, w_o, ln_g, ln_b, g_mlp, g_mlp)


def _staggered(chains):
    live = []
    for chain in chains:
        live.append(chain)
        live = [c for c in live if next(c, StopIteration) is not StopIteration]
    while live:
        live = [c for c in live if next(c, StopIteration) is not StopIteration]


def _acc_spec(rows=8, cols=D_MODEL):
    return pl.BlockSpec((rows, cols), lambda i: (0, 0))


def _bwd_out(y2, tgt, ff, u, y1, mix, mod, ln2_g, ln2_b, ln1_g, ln1_b, g_mlp, w_o, seq):
    rows = y2.shape[0]
    nb = rows // seq
    tm = ROW_TILE
    tps = seq // tm

    def body(y2_ref, t_ref, ff_ref, u_ref, y1_ref, mix_ref, mod_ref, g2_ref, b2_ref, g_ref, b_ref, wu_ref, wd_ref,
             wo_ref, dy1_ref, dmix_ref, do_ref, dff_ref, du_ref, acc_ref, dmod_ref):
        i = pl.program_id(0)

        @pl.when(i == 0)
        def _():
            acc_ref[...] = jnp.zeros_like(acc_ref)

        @pl.when(i % tps == 0)
        def _():
            dmod_ref[...] = jnp.zeros_like(dmod_ref)

        g2 = g2_ref[...]
        x2, xhat2, rstd2 = _ln_fwd(y2_ref[...], g2, b2_ref[...])
        err = x2 - t_ref[...]
        dx2 = err * (1.0 / D_MODEL)
        acc_ref[0:1, :] += _rowsum(dx2 * xhat2)
        acc_ref[1:2, :] += _rowsum(dx2)
        acc_ref[2:3, :] += _rowsum(err * err) * (0.5 / D_MODEL)
        dy2 = _ln_bwd(dx2, xhat2, rstd2, g2)
        dmod_ref[0, 5:6, :] += _rowsum(dy2 * ff_ref[...])
        dff = ((1.0 + mod_ref[0, 5:6, :]) * dy2).astype(BF16)
        dff_ref[...] = dff
        for blk in range(8):
            cols = slice(blk * HALF, (blk + 1) * HALF)
            da = _dot(dff, wd_ref[blk], NT)
            du_ref[:, cols] = (da * (2.0 * jnp.maximum(u_ref[:, cols].astype(F32), 0.0))).astype(BF16)

        g = g_ref[...]
        x1, xhat, rstd = _ln_fwd(y1_ref[...], g, b_ref[...])
        halves = []
        for half in range(2):
            acc = jnp.zeros((tm, HALF), F32)
            for chip in range(4):
                acc = acc + _dot(du_ref[:, chip * SHARD:(chip + 1) * SHARD], wu_ref[2 * chip + half], NT)
            halves.append(acc)
        dh2 = jnp.concatenate(halves, axis=1)
        dmod_ref[0, 3:4, :] += _rowsum(dh2)
        dmod_ref[0, 4:5, :] += _rowsum(dh2 * x1)
        dx1 = ALPHA * dy2 + dh2 * (1.0 + mod_ref[0, 4:5, :])
        acc_ref[3:4, :] += _rowsum(dx1 * xhat)
        acc_ref[4:5, :] += _rowsum(dx1)
        dy1 = _ln_bwd(dx1, xhat, rstd, g)
        dy1_ref[...] = dy1
        dmod_ref[0, 2:3, :] += _rowsum(dy1 * mix_ref[...])
        dmix = ((1.0 + mod_ref[0, 2:3, :]) * dy1).astype(BF16)
        dmix_ref[...] = dmix
        do_ref[...] = _dot(dmix, wo_ref[...], NT)

    outs = [(D_MODEL, F32), (D_MODEL, BF16), (D_MODEL, F32), (D_MODEL, BF16), (D_FF, BF16)]
    return pl.pallas_call(
        body, name="bwd_out", grid=(rows // tm,),
        out_shape=tuple(jax.ShapeDtypeStruct((rows, n), dt) for n, dt in outs)
        + (jax.ShapeDtypeStruct((8, D_MODEL), F32), jax.ShapeDtypeStruct((nb, 8, D_MODEL), F32)),
        in_specs=[_row_spec(D_MODEL), _row_spec(D_MODEL), _row_spec(D_MODEL), _row_spec(D_FF), _row_spec(D_MODEL),
                  _row_spec(D_MODEL), _mod_spec(tps), _const_spec((1, D_MODEL)), _const_spec((1, D_MODEL)),
                  _const_spec((1, D_MODEL)), _const_spec((1, D_MODEL))] + _mlp_weight_specs()
        + [_const_spec(w_o.shape)],
        out_specs=tuple(_row_spec(n) for n, _ in outs) + (_acc_spec(), _mod_spec(tps)),
        compiler_params=_params(("arbitrary",)),
    )(y2, tgt, ff, u, y1, mix, mod, ln2_g, ln2_b, ln1_g, ln1_b, g_mlp, g_mlp, w_o)


def _bwd_in(dqp, dkp, dvm, dq_sb, dk_sb, dv_sb, lat, x, dy1, mod, ln_g, ln_b, w_in, q_g, kv_g, w_uq, w_ukv,
            cos_t, sin_a, sin_b, seq):
    rows = x.shape[0]
    nb = rows // seq
    tm = ROW_TILE
    tps = seq // tm
    n_lat = Q_RANK + KV_RANK

    def body(dqp_ref, dkp_ref, dvm_ref, dqs_ref, dks_ref, dvs_ref, lat_ref, x_ref, dy1_ref, mod_ref,
             g_ref, b_ref, win_ref, qg_ref, kvg_ref, wuq_ref, wukv_ref, cos_ref, sa_ref, sb_ref,
             dx_ref, dproj_ref, dqall_ref, dkv_ref, latn_ref, acc_ref, accl_ref, dmod_ref):
        i = pl.program_id(0)

        @pl.when(i == 0)
        def _():
            acc_ref[...] = jnp.zeros_like(acc_ref)
            accl_ref[...] = jnp.zeros_like(accl_ref)

        @pl.when(i % tps == 0)
        def _():
            dmod_ref[...] = jnp.zeros_like(dmod_ref)

        cos, sa, sb = cos_ref[...], sa_ref[...], sb_ref[...]
        lane = lax.broadcasted_iota(jnp.int32, (tm, LANES), 1)
        for hd in range(HEADS):
            sl = slice(hd * LANES, (hd + 1) * LANES)
            dqall_ref[:, sl] = _rope_t(dqp_ref[:, sl], cos, sa, sb).astype(BF16)
        dcqn = _dot(dqall_ref[...], wuq_ref[...], NT)
        cq = lat_ref[:, :Q_RANK]
        qg = qg_ref[...]
        rq = lax.rsqrt(_mean(cq * cq) + RMS_EPS)
        cqn = cq * rq
        latn_ref[:, :Q_RANK] = (cqn * qg).astype(BF16)
        accl_ref[0:1, :Q_RANK] += _rowsum(dcqn * cqn)
        dqg = dcqn * qg
        dcq = rq * (dqg - cqn * _mean(dqg * cqn))
        dkr = jnp.zeros((tm, LANES), F32)
        for hd in range(HEADS):
            sl = slice(hd * LANES, (hd + 1) * LANES)
            dk = dkp_ref[:, sl]
            dkr = dkr + dk
            dkv_ref[:, sl] = jnp.where(lane < NOPE, dk, 0.0).astype(BF16)
        dkv_ref[:, HEADS * LANES:] = dvm_ref[...].astype(BF16)
        dckvn = _dot(dkv_ref[...], wukv_ref[...], NT)
        ckv = lat_ref[:, Q_RANK:]
        kvg = kvg_ref[...]
        rkv = lax.rsqrt(_mean(ckv * ckv) + RMS_EPS)
        ckvn = ckv * rkv
        latn_ref[:, Q_RANK:] = (ckvn * kvg).astype(BF16)
        accl_ref[1:2, :KV_RANK] += _rowsum(dckvn * ckvn)
        dkg = dckvn * kvg
        dckv = rkv * (dkg - ckvn * _mean(dkg * ckvn))
        dkr = _rope_t(jnp.where(lane >= NOPE, dkr, 0.0), cos, sa, sb)
        dproj_ref[:, :SB_W] = dqs_ref[...]
        dproj_ref[:, SB_W:2 * SB_W] = dks_ref[...].astype(BF16)
        dproj_ref[:, 2 * SB_W:3 * SB_W] = dvs_ref[...].astype(BF16)
        dproj_ref[:, 3 * SB_W:3 * SB_W + Q_RANK] = dcq.astype(BF16)
        dproj_ref[:, 3 * SB_W + Q_RANK:3 * SB_W + n_lat] = dckv.astype(BF16)
        dproj_ref[:, D_IN_PAD - LANES:] = dkr.astype(BF16)
        dh = _dot(dproj_ref[...], win_ref[...], NT)
        g = g_ref[...]
        x0, xhat, rstd = _ln_fwd(x_ref[...], g, b_ref[...])
        dmod_ref[0, 0:1, :] += _rowsum(dh)
        dmod_ref[0, 1:2, :] += _rowsum(dh * x0)
        dx0 = ALPHA * dy1_ref[...] + dh * (1.0 + mod_ref[0, 1:2, :])
        acc_ref[0:1, :] += _rowsum(dx0 * xhat)
        acc_ref[1:2, :] += _rowsum(dx0)
        dx_ref[...] = _ln_bwd(dx0, xhat, rstd, g)

    outs = [(D_MODEL, F32), (D_IN_PAD, BF16), (HEADS * LANES, BF16), (HEADS * LANES + MLA_W, BF16), (n_lat, BF16)]
    return pl.pallas_call(
        body, name="bwd_in", grid=(rows // tm,),
        out_shape=tuple(jax.ShapeDtypeStruct((rows, n), dt) for n, dt in outs)
        + (jax.ShapeDtypeStruct((8, D_MODEL), F32), jax.ShapeDtypeStruct((8, Q_RANK), F32),
           jax.ShapeDtypeStruct((nb, 8, D_MODEL), F32)),
        in_specs=[_row_spec(HEADS * LANES), _row_spec(HEADS * LANES), _row_spec(MLA_W),
                  _row_spec(SB_W), _row_spec(SB_W), _row_spec(SB_W), _row_spec(n_lat),
                  _row_spec(D_MODEL), _row_spec(D_MODEL), _mod_spec(tps),
                  _const_spec((1, D_MODEL)), _const_spec((1, D_MODEL)), _const_spec(w_in.shape),
                  _const_spec((1, Q_RANK)), _const_spec((1, KV_RANK)), _const_spec(w_uq.shape),
                  _const_spec(w_ukv.shape), _table_spec(tps), _table_spec(tps), _table_spec(tps)],
        out_specs=tuple(_row_spec(n) for n, _ in outs) + (_acc_spec(), _acc_spec(8, Q_RANK), _mod_spec(tps)),
        compiler_params=_params(("arbitrary",)),
    )(dqp, dkp, dvm, dq_sb, dk_sb, dv_sb, lat, x, dy1, mod, ln_g, ln_b, w_in, q_g, kv_g, w_uq, w_ukv,
      cos_t, sin_a, sin_b)


def _wgrad(a, b, name, pre=None, tm=512, tn=1024, tk=2048):
    rows, m = a.shape
    n = b.shape[1]
    tm, tn, tk = min(tm, m), min(tn, n), min(tk, rows)
    if m % tm:
        tm = m
    if n % tn:
        tn = n

    def body(a_ref, b_ref, o_ref):
        @pl.when(pl.program_id(2) == 0)
        def _():
            o_ref[...] = jnp.zeros_like(o_ref)

        av = a_ref[...]
        if pre == "relu2":
            av = jnp.square(jnp.maximum(av.astype(F32), 0.0))
        o_ref[...] += _dot(av.astype(BF16), b_ref[...].astype(BF16), TN)

    return pl.pallas_call(
        body, name=name, grid=(m // tm, n // tn, rows // tk),
        out_shape=jax.ShapeDtypeStruct((m, n), F32),
        in_specs=[pl.BlockSpec((tk, tm), lambda i, j, k: (k, i)), pl.BlockSpec((tk, tn), lambda i, j, k: (k, j))],
        out_specs=pl.BlockSpec((tm, tn), lambda i, j, k: (i, j)),
        compiler_params=_params(("arbitrary", "arbitrary", "arbitrary")),
    )(a, b)


def _wgrad_packed(a, b, name, block_of, row_block, split=1, pre=None, into=None, tk=2048):
    rows, m = a.shape
    n = b.shape[1]
    tm = HALF
    part = tm // split
    tk = min(tk, rows)
    shape = jax.ShapeDtypeStruct((8, GROUP_MLP[0], PACK_COLS), F32)

    def body(a_ref, b_ref, *rest):
        o_ref = rest[-1]

        @pl.when(pl.program_id(2) == 0)
        def _():
            o_ref[...] = jnp.zeros_like(o_ref)

        av = a_ref[...]
        if pre == "relu2":
            av = jnp.square(jnp.maximum(av.astype(F32), 0.0))
        prod = _dot(av.astype(BF16), b_ref[...].astype(BF16), TN)
        for s in range(split):
            o_ref[s] += prod[s * part:(s + 1) * part]

    in_specs = [pl.BlockSpec((tk, tm), lambda i, j, k: (k, i)), pl.BlockSpec((tk, SHARD), lambda i, j, k: (k, j))]
    operands = [a, b]
    if into is not None:
        in_specs.append(pl.BlockSpec(memory_space=pl.ANY))
        operands.append(into)
    return pl.pallas_call(
        body, name=name, grid=(m // tm, n // SHARD, rows // tk), out_shape=shape,
        in_specs=in_specs,
        out_specs=pl.BlockSpec((split, part, SHARD), lambda i, j, k: (block_of(i, j), row_block, 0)),
        input_output_aliases={} if into is None else {2: 0},
        compiler_params=_params(("arbitrary", "arbitrary", "arbitrary")),
    )(*operands)


def _pair(pp):
    return slice(pp * LANES, (pp + 1) * LANES)


def _head_mask(lane, hh):
    return jnp.where((lane >= 64) if hh else (lane < 64), 1.0, 0.0).astype(BF16)


def _tri(t, kind):
    s = lax.broadcasted_iota(jnp.int32, (t, t), 0)
    j = lax.broadcasted_iota(jnp.int32, (t, t), 1)
    one = jnp.where(j > s if kind == "later" else j < s, 1.0, 0.0).astype(BF16)
    return jnp.concatenate([one, one], axis=1)


def _split_dot(tri2, v):
    hi = v.astype(BF16)
    lo = (v - hi.astype(F32)).astype(BF16)
    return _dot(tri2, jnp.concatenate([hi, lo], axis=0))


def _sb_logits(z, valid):
    log_keep = -(jnp.maximum(z, 0.0) + jnp.log(1.0 + jnp.exp2(jnp.abs(z) * NEG_LOG2E)))
    log_beta = z + log_keep
    if valid is not None:
        log_keep = jnp.where(valid, log_keep, 0.0)
    return log_keep, log_beta


def _attention_call(body, ex, name, grid, operands, in_specs, out_shapes, out_specs, scratch=()):
    n_in, n_out = len(operands), len(out_shapes)
    total = grid[0] * grid[1] * grid[2]
    any_spec = pl.BlockSpec(memory_space=pl.ANY)

    def carrier(*refs):
        ins, outs, (start, middle, finish) = _carried(ex, refs, n_in, n_out, len(scratch))
        step = (pl.program_id(0) * grid[1] + pl.program_id(1)) * grid[2] + pl.program_id(2)
        pl.when(step == 0)(start)
        pl.when(step == total // 2)(middle)
        body(*ins, *outs)
        pl.when(step == total - 1)(finish)

    carried = ex is not None
    return pl.pallas_call(
        carrier if carried else body, name=name, grid=grid,
        out_shape=tuple(out_shapes) + ((ex.out_shape,) if carried else ()),
        in_specs=list(in_specs) + ([any_spec] if carried else []),
        out_specs=tuple(out_specs) + ((any_spec,) if carried else ()),
        scratch_shapes=list(scratch) + (ex.scratch if carried else []),
        compiler_params=_params(("arbitrary", "arbitrary", "arbitrary")),
    )(*operands, *([ex.operand] if carried else []))


def _sb_fwd(qkv, seq, ex=None):
    rows = qkv.shape[0]
    nb = rows // seq
    t = min(ATTN_TILE, seq)
    nq = seq // t
    assert nq <= CAR_SLOTS, (seq, t)
    ap = ATTN_PAIRS
    width = ap * LANES
    groups = SB_W // width
    hds = [(pp, hh) for pp in range(ap) for hh in range(2)]

    def body(q_ref, k_ref, v_ref, tri_ref, o_ref, car_ref, acc_ref):
        i = pl.program_id(2)
        lane = lax.broadcasted_iota(jnp.int32, (t, LANES), 1)
        key = lax.broadcasted_iota(jnp.int32, (t, t), 0)
        qry = lax.broadcasted_iota(jnp.int32, (t, t), 1)
        strict = key < qry
        tri = tri_ref[...]
        masks = [_head_mask(lane, hh) for hh in range(2)]
        qms = [q_ref[:, _pair(pp)] * masks[hh] for pp, hh in hds]
        acc_ref[...] = jnp.zeros_like(acc_ref)
        car_ref[...] = jnp.zeros_like(car_ref)

        def step(kb, c_sums, valid):
            start = pl.multiple_of(kb * t, t)
            kss = [k_ref[pl.ds(start, t), _pair(pp)] for pp in range(ap)]
            vss = [v_ref[pl.ds(start, t), _pair(pp)] for pp in range(ap)]
            zs = [_dot(kss[pp], qms[n], NT) for n, (pp, _) in enumerate(hds)]
            logs = [_sb_logits(z, valid) for z in zs]
            sufs = [_split_dot(tri, lg[0]) for lg in logs]
            new_sums = []
            for n, (pp, hh) in enumerate(hds):
                log_keep, log_beta = logs[n]
                w = jnp.exp(log_beta + sufs[n] + c_sums[n])
                if valid is not None:
                    w = jnp.where(valid, w, 0.0)
                acc_ref[pp] += _dot(vss[pp] * masks[hh], w.astype(BF16), TN)
                car_ref[0, pl.ds(n * CAR_SLOTS + kb, 1), :] = c_sums[n]
                new_sums.append(c_sums[n] + sufs[n][0:1, :] + log_keep[0:1, :])
            return tuple(new_sums)

        c_sums = step(i, tuple(jnp.zeros((1, t), F32) for _ in hds), strict)
        lax.fori_loop(0, i, lambda j, cr: step(i - 1 - j, cr, None), c_sums)
        for pp in range(ap):
            o_ref[:, _pair(pp)] = acc_ref[pp].T.astype(BF16)

    qspec = pl.BlockSpec((t, width), lambda b, p, i: (b * nq + i, p))
    car_rows = len(hds) * CAR_SLOTS
    return _attention_call(
        body, ex, "sb_fwd", (nb, groups, nq),
        [qkv, qkv, qkv, _tri(t, "later")],
        [qspec,
         pl.BlockSpec((seq, width), lambda b, p, i: (b, groups + p)),
         pl.BlockSpec((seq, width), lambda b, p, i: (b, 2 * groups + p)),
         _const_spec((t, 2 * t))],
        [jax.ShapeDtypeStruct((rows, SB_W), BF16), jax.ShapeDtypeStruct((nb * nq, HEADS * CAR_SLOTS, t), F32)],
        [qspec, pl.BlockSpec((1, car_rows, t), lambda b, p, i: (b * nq + i, p, 0))],
        scratch=[pltpu.VMEM((ap, LANES, t), F32)])


def _sb_bwd(qkv, d_out, cars, seq, ex=None):
    rows = qkv.shape[0]
    nb = rows // seq
    t = min(ATTN_TILE, seq)
    nq = seq // t
    ap = ATTN_PAIRS
    width = ap * LANES
    groups = SB_W // width
    hds = [(pp, hh) for pp in range(ap) for hh in range(2)]

    def body(q_ref, k_ref, v_ref, do_ref, car_ref, tri_ref, pre_ref, dq_ref, dk_ref, dv_ref, dq_acc):
        i = pl.program_id(2)

        @pl.when(i == 0)
        def _():
            dk_ref[...] = jnp.zeros_like(dk_ref)
            dv_ref[...] = jnp.zeros_like(dv_ref)

        lane = lax.broadcasted_iota(jnp.int32, (t, LANES), 1)
        key = lax.broadcasted_iota(jnp.int32, (t, t), 0)
        qry = lax.broadcasted_iota(jnp.int32, (t, t), 1)
        strict = key < qry
        tri, pre = tri_ref[...], pre_ref[...]
        masks = [_head_mask(lane, hh) for hh in range(2)]
        qms = [q_ref[:, _pair(pp)] * masks[hh] for pp, hh in hds]
        doms = [do_ref[:, _pair(pp)].astype(BF16) * masks[hh] for pp, hh in hds]
        dq_acc[...] = jnp.zeros_like(dq_acc)

        def step(kb, g_pres, valid):
            start = pl.multiple_of(kb * t, t)
            kss = [k_ref[pl.ds(start, t), _pair(pp)] for pp in range(ap)]
            vss = [v_ref[pl.ds(start, t), _pair(pp)] for pp in range(ap)]
            zs = [_dot(kss[pp], qms[n], NT) for n, (pp, _) in enumerate(hds)]
            dws = [_dot(vss[pp], doms[n], NT) for n, (pp, _) in enumerate(hds)]
            logs = [_sb_logits(z, valid) for z in zs]
            sufs = [_split_dot(tri, lg[0]) for lg in logs]
            ws, gs = [], []
            for n in range(len(hds)):
                c_sum = car_ref[0, pl.ds(n * CAR_SLOTS + kb, 1), :]
                w = jnp.exp(logs[n][1] + sufs[n] + c_sum)
                if valid is not None:
                    w = jnp.where(valid, w, 0.0)
                ws.append(w)
                gs.append(dws[n] * w)
            pres = [_split_dot(pre, gs[n]) for n in range(len(hds))]
            befores = [g_pres[n] + pres[n] for n in range(len(hds))]
            for pp in range(ap):
                a, b = 2 * pp, 2 * pp + 1
                dv_ref[pl.ds(start, t), _pair(pp)] += _dot(ws[a].astype(BF16), doms[a]) + _dot(ws[b].astype(BF16), doms[b])
            dzbs = []
            for n in range(len(hds)):
                beta = jnp.exp(logs[n][1])
                dz = gs[n] * (1.0 - beta) - beta * befores[n]
                if valid is not None:
                    dz = jnp.where(valid, dz, 0.0)
                dzbs.append(dz.astype(BF16))
            for pp in range(ap):
                a, b = 2 * pp, 2 * pp + 1
                dq_acc[pp] += _dot(dzbs[a], kss[pp] * masks[0], TN) + _dot(dzbs[b], kss[pp] * masks[1], TN)
                dk_ref[pl.ds(start, t), _pair(pp)] += _dot(dzbs[a], qms[a]) + _dot(dzbs[b], qms[b])
            return tuple(g_pres[n] + pres[n][t - 1:t, :] + gs[n][t - 1:t, :] for n in range(len(hds)))

        g_pres = lax.fori_loop(0, i, lambda kb, cr: step(kb, cr, None), tuple(jnp.zeros((1, t), F32) for _ in hds))
        step(i, g_pres, strict)
        for pp in range(ap):
            dq_ref[:, _pair(pp)] = (dq_acc[pp] * SB_SCALE).astype(BF16)

    qspec = pl.BlockSpec((t, width), lambda b, p, i: (b * nq + i, p))
    kspec_out = pl.BlockSpec((seq, width), lambda b, p, i: (b, p))
    car_rows = len(hds) * CAR_SLOTS
    return _attention_call(
        body, ex, "sb_bwd", (nb, groups, nq),
        [qkv, qkv, qkv, d_out, cars, _tri(t, "later"), _tri(t, "earlier")],
        [qspec,
         pl.BlockSpec((seq, width), lambda b, p, i: (b, groups + p)),
         pl.BlockSpec((seq, width), lambda b, p, i: (b, 2 * groups + p)),
         qspec, pl.BlockSpec((1, car_rows, t), lambda b, p, i: (b * nq + i, p, 0)),
         _const_spec((t, 2 * t)), _const_spec((t, 2 * t))],
        [jax.ShapeDtypeStruct((rows, SB_W), BF16), jax.ShapeDtypeStruct((rows, SB_W), F32),
         jax.ShapeDtypeStruct((rows, SB_W), F32)],
        [qspec, kspec_out, kspec_out],
        scratch=[pltpu.VMEM((ap, t, LANES), F32)])


def _mla_scores(ks, qh, allowed):
    s = _dot(ks, qh, NT) * (MLA_SCALE * -NEG_LOG2E)
    if allowed is not None:
        s = jnp.where(allowed, s, jnp.finfo(F32).min)
    return s


def _mla_fwd(qp, kp, vm, seq, ex=None, chunk=64):
    rows = qp.shape[0]
    nb = rows // seq
    t = min(ATTN_TILE, seq)
    nq = seq // t
    shift = int(math.log2(chunk))
    ap = ATTN_PAIRS
    width = ap * LANES
    groups = MLA_W // width
    hds = [(pp, hh) for pp in range(ap) for hh in range(2)]

    def body(q_ref, k_ref, v_ref, o_ref, lse_ref, acc_ref):
        i = pl.program_id(2)
        lane = lax.broadcasted_iota(jnp.int32, (t, LANES), 1)
        key = lax.broadcasted_iota(jnp.int32, (t, t), 0)
        qry = lax.broadcasted_iota(jnp.int32, (t, t), 1)
        allowed_diag = jnp.right_shift(key, shift) <= jnp.right_shift(qry, shift)
        masks = [_head_mask(lane, hh) for hh in range(2)]
        qhs = [q_ref[:, _pair(n)] for n in range(len(hds))]
        acc_ref[...] = jnp.zeros_like(acc_ref)

        def step(kb, carry, allowed):
            start = pl.multiple_of(kb * t, t)
            vss = [v_ref[pl.ds(start, t), _pair(pp)] for pp in range(ap)]
            scores = [_mla_scores(k_ref[pl.ds(start, t), _pair(n)], qhs[n], allowed) for n in range(len(hds))]
            new = []
            for n, (pp, hh) in enumerate(hds):
                m_run, l_run = carry[n]
                s = scores[n]
                m_new = jnp.maximum(m_run, jnp.max(s, axis=0, keepdims=True))
                p = jnp.exp2(s - m_new)
                scale = jnp.exp2(m_run - m_new)
                l_run = scale * l_run + jnp.sum(p, axis=0, keepdims=True)
                acc_ref[n] = scale * acc_ref[n] + _dot(vss[pp] * masks[hh], p.astype(BF16), TN)
                new.append((m_new, l_run))
            return tuple(new)

        init = (jnp.full((1, t), jnp.finfo(F32).min, F32), jnp.zeros((1, t), F32))
        carry = step(i, tuple(init for _ in hds), allowed_diag)
        carry = lax.fori_loop(0, i, lambda kb, cr: step(kb, cr, None), carry)
        lse_rows = []
        for pp in range(ap):
            out_t = jnp.zeros((LANES, t), F32)
            for hh in range(2):
                m_run, l_run = carry[2 * pp + hh]
                out_t = out_t + acc_ref[2 * pp + hh] / l_run
                lse_rows.append(m_run + jnp.log(l_run) * -NEG_LOG2E)
            o_ref[:, _pair(pp)] = out_t.T
        lse_t = jnp.concatenate(lse_rows + [jnp.zeros((LANES - len(hds), t), F32)], axis=0)
        lse_ref[...] = jnp.zeros_like(lse_ref)
        lse_ref[:, _pair(0)] = lse_t.T

    ospec = pl.BlockSpec((t, width), lambda b, p, i: (b * nq + i, p))
    return _attention_call(
        body, ex, "mla_fwd", (nb, groups, nq), [qp, kp, vm],
        [pl.BlockSpec((t, 2 * width), lambda b, p, i: (b * nq + i, p)),
         pl.BlockSpec((seq, 2 * width), lambda b, p, i: (b, p)),
         pl.BlockSpec((seq, width), lambda b, p, i: (b, p))],
        [jax.ShapeDtypeStruct((rows, MLA_W), F32), jax.ShapeDtypeStruct((rows, MLA_W), F32)],
        [ospec, ospec], scratch=[pltpu.VMEM((len(hds), LANES, t), F32)])


def _mla_bwd(qp, kp, vm, d_out, out, lse, seq, ex=None, chunk=64):
    rows = qp.shape[0]
    nb = rows // seq
    t = min(ATTN_TILE, seq)
    nq = seq // t
    shift = int(math.log2(chunk))
    ap = ATTN_PAIRS
    width = ap * LANES
    groups = MLA_W // width
    hds = [(pp, hh) for pp in range(ap) for hh in range(2)]
    nh = len(hds)

    def body(q_ref, k_ref, v_ref, do_ref, o_ref, lse_ref, dq_ref, dk_ref, dv_ref):
        i = pl.program_id(2)

        @pl.when(i == 0)
        def _():
            dk_ref[...] = jnp.zeros_like(dk_ref)
            dv_ref[...] = jnp.zeros_like(dv_ref)

        lane = lax.broadcasted_iota(jnp.int32, (t, LANES), 1)
        key = lax.broadcasted_iota(jnp.int32, (t, t), 0)
        qry = lax.broadcasted_iota(jnp.int32, (t, t), 1)
        allowed_diag = jnp.right_shift(key, shift) <= jnp.right_shift(qry, shift)
        qhs = [q_ref[:, _pair(n)] for n in range(nh)]
        lse_t = lse_ref[:, _pair(0)].T
        doms, deltas, lse_hs = [], [], []
        for pp in range(ap):
            do = do_ref[:, _pair(pp)]
            d_o_t = (do * o_ref[:, _pair(pp)]).T
            for hh in range(2):
                doms.append(do.astype(BF16) * _head_mask(lane, hh))
                deltas.append(jnp.sum(d_o_t[hh * 64:(hh + 1) * 64], axis=0, keepdims=True))
                lse_hs.append(lse_t[2 * pp + hh:2 * pp + hh + 1])

        dq_ref[...] = jnp.zeros_like(dq_ref)

        def step(kb, allowed):
            start = pl.multiple_of(kb * t, t)
            vss = [v_ref[pl.ds(start, t), _pair(pp)] for pp in range(ap)]
            kss = [k_ref[pl.ds(start, t), _pair(n)] for n in range(nh)]
            scores = [_mla_scores(kss[n], qhs[n], allowed) for n in range(nh)]
            dps = [_dot(vss[pp], doms[n], NT) for n, (pp, _) in enumerate(hds)]
            ps = [jnp.exp2(scores[n] - lse_hs[n]) for n in range(nh)]
            dss = [(ps[n] * (dps[n] - deltas[n]) * MLA_SCALE).astype(BF16) for n in range(nh)]
            for pp in range(ap):
                a, b = 2 * pp, 2 * pp + 1
                dv_ref[pl.ds(start, t), _pair(pp)] += _dot(ps[a].astype(BF16), doms[a]) + _dot(ps[b].astype(BF16), doms[b])
            for n in range(nh):
                dk_ref[pl.ds(start, t), _pair(n)] += _dot(dss[n], qhs[n])
                dq_ref[:, _pair(n)] += _dot(dss[n], kss[n], TN)

        def off_diagonal(kb, nothing):
            step(kb, None)
            return nothing

        lax.fori_loop(0, i, off_diagonal, 0)
        step(i, allowed_diag)

    ospec = pl.BlockSpec((t, width), lambda b, p, i: (b * nq + i, p))
    return _attention_call(
        body, ex, "mla_bwd", (nb, groups, nq), [qp, kp, vm, d_out, out, lse],
        [pl.BlockSpec((t, 2 * width), lambda b, p, i: (b * nq + i, p)),
         pl.BlockSpec((seq, 2 * width), lambda b, p, i: (b, p)),
         pl.BlockSpec((seq, width), lambda b, p, i: (b, p)),
         pl.BlockSpec((t, width), lambda b, p, i: (b * nq + i, groups + p)),
         ospec, ospec],
        [jax.ShapeDtypeStruct((rows, HEADS * LANES), F32), jax.ShapeDtypeStruct((rows, HEADS * LANES), F32),
         jax.ShapeDtypeStruct((rows, MLA_W), F32)],
        [pl.BlockSpec((t, 2 * width), lambda b, p, i: (b * nq + i, p)),
         pl.BlockSpec((seq, 2 * width), lambda b, p, i: (b, p)),
         pl.BlockSpec((seq, width), lambda b, p, i: (b, p))])


PACK_COLS = 1024
PACK_ALIGN = 16
GROUP_IN = (384, ((1024, 552, 1), (384, 192, 1), (256, 256, 1)))
GROUP_MLP = (1152, ((1024, 1024, 1), (1024, 1024, 0), (256, 1024, 0)))


def _pack_rows(r, c):
    return (r // 2) * c // PACK_COLS


def _slot_rows(r, c):
    return -(-_pack_rows(r, c) // PACK_ALIGN) * PACK_ALIGN


def _join_slots(parts, group):
    total, weights = group
    padded = [jnp.pad(p, ((0, 0), (0, _slot_rows(r, c) - p.shape[1]), (0, 0))) for p, (r, c, _) in zip(parts, weights)]
    used = sum(_slot_rows(r, c) for r, c, _ in weights)
    if total > used:
        padded.append(jnp.zeros((parts[0].shape[0], total - used, PACK_COLS), parts[0].dtype))
    return jnp.concatenate(padded, axis=1)


def _split_slots(packed, group):
    out, at = [], 0
    for r, c, _ in group[1]:
        out.append(packed[:, at:at + _pack_rows(r, c), :])
        at += _slot_rows(r, c)
    return out


def _pack_halves(shards, group):
    return _join_slots([s.reshape(2, _pack_rows(r, c), PACK_COLS) for s, (r, c, _) in zip(shards, group[1])], group)


def _unpack_half(packed, group):
    return [p.reshape(r // 2, c) for p, (r, c, _) in zip(_split_slots(packed[None], group), group[1])]


def _unpack_full(gathered, group):
    out = []
    for p, (r, c, axis) in zip(_split_slots(gathered, group), group[1]):
        shards = p.reshape(4, r, c)
        out.append(shards.reshape(4 * r, c) if axis == 0 else jnp.moveaxis(shards, 0, 1).reshape(r, 4 * c))
    return out


def _pack_full(grads, group):
    parts = []
    for gr, (r, c, axis) in zip(grads, group[1]):
        shards = gr.reshape(4, r, c) if axis == 0 else jnp.moveaxis(gr.reshape(r, 4, c), 1, 0)
        parts.append(shards.reshape(8, _pack_rows(r, c), PACK_COLS))
    return _join_slots(parts, group)


def _pad_w_in(w_in):
    z = jnp.zeros((D_MODEL, 1), w_in.dtype)
    return jnp.concatenate([w_in[:, :2176], jnp.tile(z, (1, 64)), w_in[:, 2176:], jnp.tile(z, (1, 32))], axis=1)


def _unpad_w_in(g):
    return jnp.concatenate([g[:, :2176], g[:, 2240:2272]], axis=1)


def _pad_heads(w, used):
    k = w.shape[0]
    w3 = w.reshape(k, HEADS, used)
    return jnp.pad(w3, ((0, 0), (0, 0), (0, LANES - used))).reshape(k, HEADS * LANES)


def _unpad_heads(g, used):
    k = g.shape[0]
    return g.reshape(k, HEADS, LANES)[:, :, :used].reshape(k, HEADS * used)


def _rope_tables(seq):
    inv_freq = 1.0 / (ROPE_BASE ** (jnp.arange(0, ROPE, 2, dtype=F32) / ROPE))
    ang = jnp.arange(seq, dtype=F32)[:, None] * inv_freq[None, :]
    cos, sin = jnp.cos(ang), jnp.sin(ang)
    one, zero = jnp.ones((seq, NOPE), F32), jnp.zeros((seq, NOPE), F32)
    z16, z32 = jnp.zeros((seq, 16), F32), jnp.zeros((seq, 32), F32)
    cos_t = jnp.concatenate([one, cos, cos, jnp.ones((seq, 32), F32)], axis=1)
    sin_a = jnp.concatenate([zero, -sin, z16, z32], axis=1)
    sin_b = jnp.concatenate([zero, z16, sin, z32], axis=1)
    return cos_t, sin_a, sin_b


SMALL = (("ln_in_g", 1024), ("ln_in_b", 1024), ("b_ada", 6144), ("q_norm_g", 384), ("kv_norm_g", 256),
         ("ln1_g", 1024), ("ln1_b", 1024), ("ln2_g", 1024), ("ln2_b", 1024))
SUBLANES = 8
SMALL_SLOTS = [-(-n // LANES // SUBLANES) * SUBLANES for _, n in SMALL]
SMALL_AT = [sum(SMALL_SLOTS[:p]) for p in range(len(SMALL))]
SMALL_ROWS = sum(SMALL_SLOTS)


def _pack_small(vals):
    parts = []
    for v, slot in zip(vals, SMALL_SLOTS):
        rows = v.reshape(-1, LANES)
        parts.append(jnp.pad(rows, ((0, slot - rows.shape[0]), (0, 0))))
    return jnp.concatenate(parts, axis=0)


def kernel(x, c, ln_in_g, ln_in_b, w_ada, b_ada, w_in, q_norm_g, kv_norm_g, w_uq, w_ukv, w_o, ln1_g, ln1_b, w_up, w_down, ln2_g, ln2_b, loss_target, m_ln_in_g, m_ln_in_b, m_w_ada, m_b_ada, m_w_in, m_q_norm_g, m_kv_norm_g, m_w_uq, m_w_ukv, m_w_o, m_ln1_g, m_ln1_b, m_w_up, m_w_down, m_ln2_g, m_ln2_b, v_ln_in_g, v_ln_in_b, v_w_ada, v_b_ada, v_w_in, v_q_norm_g, v_kv_norm_g, v_w_uq, v_w_ukv, v_w_o, v_ln1_g, v_ln1_b, v_w_up, v_w_down, v_ln2_g, v_ln2_b):
    nb, seq, _ = x.shape
    rows = nb * seq
    ix, iy, ic = lax.axis_index("x"), lax.axis_index("y"), lax.axis_index("c")
    chip = 2 * ix + iy
    dev = 2 * chip + ic

    def my_half(shards, group):
        packed = _pack_halves([s.astype(BF16) for s in shards], group)
        return lax.dynamic_index_in_dim(packed, ic, 0, keepdims=False)

    f_in, f_uq, f_ukv = _unpack_full(_gather8(my_half([w_in[0], w_uq[0], w_ukv[0]], GROUP_IN), "gather_w_in"),
                                     GROUP_IN)
    half_mlp = my_half([w_up[0], w_down[0], w_o[0]], GROUP_MLP)
    late_weights = _gather_exchange(half_mlp)
    w_in_p = _pad_w_in(f_in)
    uq3 = f_uq.reshape(Q_RANK, HEADS, NOPE + ROPE)
    w_uq_p = jnp.pad(uq3, ((0, 0), (0, 0), (0, LANES - NOPE - ROPE))).reshape(Q_RANK, HEADS * LANES)
    w_ukv_p = jnp.concatenate([_pad_heads(f_ukv[:, :HEADS * NOPE], NOPE), f_ukv[:, HEADS * NOPE:]], axis=1)

    n_all = 8 * nb
    c_all = _gather8(c.reshape(-1, LANES), "gather_c").reshape(n_all, D_MODEL)
    ada_cols = w_ada.shape[2]
    b_sh = lax.dynamic_slice_in_dim(b_ada, chip * ada_cols, ada_cols, axis=1)
    mod_sh = _ada_fwd(c_all, w_ada[0], b_sh)
    mod_g = _gather8(mod_sh, "gather_mod")[0::2]
    mod_all = jnp.moveaxis(mod_g, 0, 1).reshape(n_all, N_MOD * D_MODEL)
    mod_mine = lax.dynamic_slice_in_dim(mod_all, dev * nb, nb, axis=0).reshape(nb, N_MOD, D_MODEL)
    mod = jnp.pad(mod_mine, ((0, 0), (0, 8 - N_MOD), (0, 0)))

    cos_t, sin_a, sin_b = _rope_tables(seq)
    row2 = lambda v: v.reshape(1, -1)

    x2d = x.reshape(rows, D_MODEL)
    x0, h, qkv, lat, qp, kp, vm = _fwd_in(x2d, mod, row2(ln_in_g), row2(ln_in_b), w_in_p, q_norm_g, kv_norm_g,
                                          w_uq_p, w_ukv_p, cos_t, sin_a, sin_b, seq)
    sb_y, cars, g_mlp = _sb_fwd(qkv, seq, late_weights)
    g_mlp = _with_own(g_mlp, half_mlp)
    f_o = _split_slots(g_mlp, GROUP_MLP)[2].reshape(D_MODEL, D_MODEL)
    mla_y, lse = _mla_fwd(qp, kp, vm, seq)
    mix, y1, h2, u, ff, y2 = _fwd_out(sb_y, mla_y, x0, mod, f_o, ln1_g, ln1_b, g_mlp, seq)

    dy1, dmix, d_attn, dff, du, acc_out, dmod_a = _bwd_out(
        y2, loss_target.reshape(rows, D_MODEL), ff, u, y1, mix, mod, ln2_g, ln2_b, ln1_g, ln1_b, g_mlp, f_o, seq)
    c_idx = ic.reshape(1).astype(jnp.int32)
    blocks_mlp = _wgrad_packed(h2, du, "wgrad_up", lambda i, j: 2 * j + i, 0)
    blocks_mlp = _wgrad_packed(u, dff, "wgrad_down", lambda i, j: i, 1, pre="relu2", into=blocks_mlp)
    blocks_mlp = _wgrad_packed(sb_y, dmix, "wgrad_o_sb", lambda i, j: 0, 8, split=4, into=blocks_mlp)
    blocks_mlp = _wgrad_packed(mla_y, dmix, "wgrad_o_mla", lambda i, j: 1, 8, split=4, into=blocks_mlp)
    dq_sb, dk_sb, dv_sb, sibling_mlp = _sb_bwd(qkv, d_attn, cars, seq, _swap_cores_exchange(blocks_mlp))
    part_mlp, part_mlp_bf = _add_pairs(blocks_mlp, sibling_mlp, c_idx, "grad_add_cores_mlp")
    dqp, dkp, dvm, chips_mlp = _mla_bwd(qp, kp, vm, d_attn, mla_y, lse, seq, _scatter_chips_exchange(part_mlp_bf))
    grad_x, dproj, dqall, dkv, latn, acc0, acc_lat, dmod_c = _bwd_in(
        dqp, dkp, dvm, dq_sb, dk_sb, dv_sb, lat, x2d, dy1, mod, row2(ln_in_g), row2(ln_in_b), w_in_p,
        q_norm_g, kv_norm_g, w_uq_p, w_ukv_p, cos_t, sin_a, sin_b, seq)

    g_in = _unpad_w_in(_wgrad(h, dproj, "wgrad_in", tn=768))
    g_uq = _unpad_heads(_wgrad(latn[:, :Q_RANK], dqall, "wgrad_uq"), NOPE + ROPE)
    g_ukv_p = _wgrad(latn[:, Q_RANK:], dkv, "wgrad_ukv", tn=512)
    g_ukv = jnp.concatenate([_unpad_heads(g_ukv_p[:, :HEADS * LANES], NOPE), g_ukv_p[:, HEADS * LANES:]], axis=1)
    blocks_in = _pack_full([g_in, g_uq, g_ukv], GROUP_IN)
    sibling_in = _run_exchange(_swap_cores_exchange(blocks_in), "grads_in_to_sibling")
    part_in, part_in_bf = _add_pairs(blocks_in, sibling_in, c_idx, "grad_add_cores_in")
    chips_in = _run_exchange(_scatter_chips_exchange(part_in_bf), "grads_in_to_chips")

    def own(part):
        return lax.dynamic_index_in_dim(part, chip, 0, keepdims=False)

    half = jnp.concatenate([_add_chips(own(part_in), chips_in, "grad_add_chips_in"),
                            _add_chips(own(part_mlp), chips_mlp, "grad_add_chips_mlp")], axis=0)
    other = _run_exchange(_swap_one_exchange(half), "grads_halves")
    mine = _unpack_half(half[:GROUP_IN[0]], GROUP_IN) + _unpack_half(half[GROUP_IN[0]:], GROUP_MLP)
    theirs = _unpack_half(other[:GROUP_IN[0]], GROUP_IN) + _unpack_half(other[GROUP_IN[0]:], GROUP_MLP)

    dmod = (dmod_a + dmod_c)[:, :N_MOD, :]
    small_part = _pack_small([acc0[0], acc0[1], jnp.zeros((N_MOD * D_MODEL,), F32), acc_lat[0, :Q_RANK],
                              acc_lat[1, :KV_RANK], acc_out[3], acc_out[4], acc_out[0], acc_out[1]])
    n_sum = SMALL_ROWS + D_MODEL // LANES
    payload = jnp.concatenate([small_part, acc_out[2].reshape(-1, LANES), dmod.reshape(-1, LANES)], axis=0)
    gathered = _gather8(payload, "gather_small")
    small_sum = _sum_lead(gathered[:, :n_sum, :], "sum_small")
    loss = jnp.sum(small_sum[SMALL_ROWS:])
    dmod_all = gathered[:, n_sum:, :].reshape(n_all, N_MOD * D_MODEL)
    g_b_ada = _sum_lead(dmod_all.reshape(n_all, N_MOD * D_MODEL // LANES, LANES), "sum_b_ada")
    dmod_sh = lax.dynamic_slice_in_dim(dmod_all, chip * ada_cols, ada_cols, axis=1)
    g_w_ada = _ada_bwd(c_all, dmod_sh)

    res = {}
    d_ada, m_ada, v_ada = _adamw(w_ada[0], g_w_ada, m_w_ada[0], v_w_ada[0], "adamw_w_ada")
    res["w_ada"] = (g_w_ada[None], d_ada[None], m_ada[None], v_ada[None])
    sharded = {"w_in": (w_in, m_w_in, v_w_in), "w_uq": (w_uq, m_w_uq, v_w_uq), "w_ukv": (w_ukv, m_w_ukv, v_w_ukv),
               "w_up": (w_up, m_w_up, v_w_up), "w_down": (w_down, m_w_down, v_w_down), "w_o": (w_o, m_w_o, v_w_o)}
    for (name, (w, m, v)), g_mine, g_other in zip(sharded.items(), mine, theirs):
        quad = _adamw_halves(w[0], g_mine, g_other, m[0], v[0], c_idx, "adamw_" + name)
        res[name] = tuple(a[None] for a in quad)
    small_w = [ln_in_g, ln_in_b, b_ada, q_norm_g, kv_norm_g, ln1_g, ln1_b, ln2_g, ln2_b]
    small_m = [m_ln_in_g, m_ln_in_b, m_b_ada, m_q_norm_g, m_kv_norm_g, m_ln1_g, m_ln1_b, m_ln2_g, m_ln2_b]
    small_v = [v_ln_in_g, v_ln_in_b, v_b_ada, v_q_norm_g, v_kv_norm_g, v_ln1_g, v_ln1_b, v_ln2_g, v_ln2_b]
    for (name, _), quad in zip(SMALL, _adamw_small(small_sum, g_b_ada, small_w, small_m, small_v)):
        res[name] = quad

    order = ["ln_in_g", "ln_in_b", "w_ada", "b_ada", "w_in", "q_norm_g", "kv_norm_g", "w_uq", "w_ukv", "w_o",
             "ln1_g", "ln1_b", "w_up", "w_down", "ln2_g", "ln2_b"]
    outs = [loss, grad_x.reshape(nb, seq, D_MODEL)]
    for k in range(4):
        outs += [res[name][k] for name in order]
    return tuple(outs)
```

```python
import math

import jax
import jax.numpy as jnp
from jax import lax
from jax.experimental import pallas as pl
from jax.experimental.pallas import tpu as pltpu

F32 = jnp.float32
BF16 = jnp.bfloat16
MESH_IDS = pl.DeviceIdType.MESH

D_MODEL = 1024
HEADS = 8
SB_W = 512
MLA_W = 512
NOPE = 64
ROPE = 32
Q_RANK = 384
KV_RANK = 256
D_IN_PAD = 2304
D_FF = 4096
N_MOD = 6
LN_EPS = 1e-5
RMS_EPS = 1e-6
ALPHA = 2.0 ** 0.25
ROPE_BASE = 10000.0
SB_SCALE = 64 ** -0.5
NEG_LOG2E = -math.log2(math.e)
MLA_SCALE = 96 ** -0.5
ADAM_LR = 0.001
ADAM_B1 = 0.9
ADAM_B2 = 0.999
ADAM_EPS = 1e-08
ADAM_WD = 0.01
ADAM_STEP = 10

LANES = 128
ROW_TILE = 256
ATTN_TILE = 256
CAR_SLOTS = 8
ATTN_PAIRS = 4
VMEM_LIMIT = 56 << 20

NT = (((1,), (1,)), ((), ()))
TN = (((0,), (0,)), ((), ()))


def _params(sem=None):
    return pltpu.CompilerParams(vmem_limit_bytes=VMEM_LIMIT, dimension_semantics=sem)


def _const_spec(shape):
    zeros = (0,) * len(shape)
    return pl.BlockSpec(shape, lambda *_: zeros, pipeline_mode=pl.Buffered(1))


def _dot(a, b, dims=None):
    if dims is None:
        return jnp.dot(a, b, preferred_element_type=F32)
    return lax.dot_general(a, b, dims, preferred_element_type=F32)


def _mean(v):
    return jnp.mean(v, axis=-1, keepdims=True)


def _rowsum(v):
    return jnp.sum(v, axis=0, keepdims=True)


def _ln_fwd(y, g, b):
    mu = _mean(y)
    yc = y - mu
    rstd = lax.rsqrt(_mean(yc * yc) + LN_EPS)
    xhat = yc * rstd
    return xhat * g + b, xhat, rstd


def _ln_bwd(dx, xhat, rstd, g):
    dxh = dx * g
    return rstd * (dxh - _mean(dxh) - xhat * _mean(dxh * xhat))


def _rope(v, cos, sin_a, sin_b):
    return v * cos + pltpu.roll(v, 112, 1) * sin_a + pltpu.roll(v, 16, 1) * sin_b


def _rope_t(dv, cos, sin_a, sin_b):
    return dv * cos + pltpu.roll(dv * sin_a, 16, 1) + pltpu.roll(dv * sin_b, 112, 1)


def _my_place():
    return lax.axis_index("x"), lax.axis_index("y"), lax.axis_index("c")


class _Exchange:
    def __init__(self, operand, out_shape, n_copies, phases):
        self.operand = operand
        self.out_shape = out_shape
        self.phases = phases
        self.scratch = [pltpu.SemaphoreType.DMA((n_copies,)), pltpu.SemaphoreType.DMA((n_copies,))]


def _run_exchange(ex, name):
    def body(in_ref, out_ref, send_sems, recv_sems):
        for phase in ex.phases(in_ref, out_ref, send_sems, recv_sems):
            phase()

    return pl.pallas_call(
        body, name=name, out_shape=ex.out_shape,
        in_specs=[pl.BlockSpec(memory_space=pl.ANY)], out_specs=pl.BlockSpec(memory_space=pl.ANY),
        scratch_shapes=ex.scratch,
    )(ex.operand)


def _nothing():
    pass


def _gather_exchange(v):
    m, n = v.shape

    def phases(v_ref, out_ref, send_sems, recv_sems):
        x, y, c = _my_place()
        me, sibling = (x, y, c), (x, y, 1 - c)
        chips = [(1 - x, y), (x, 1 - y), (1 - x, 1 - y)]

        def rows(px, py, pc):
            return out_ref.at[4 * px + 2 * py + pc]

        def copy(k, block, to, src=None):
            return pltpu.make_async_remote_copy(
                src_ref=rows(*block) if src is None else src, dst_ref=rows(*block),
                send_sem=send_sems.at[k], recv_sem=recv_sems.at[k], device_id=to, device_id_type=MESH_IDS)

        first = [copy(0, me, sibling, src=v_ref)]
        first += [copy(1 + j, me, (*chip, c), src=v_ref) for j, chip in enumerate(chips)]
        passed = [copy(4 + j, (*chip, c), sibling) for j, chip in enumerate(chips)]

        def start():
            for cp in first:
                cp.start()

        def middle():
            for j, chip in enumerate(chips):
                copy(1 + j, (*chip, c), me).wait_recv()
                passed[j].start()

        def finish():
            copy(0, sibling, me).wait_recv()
            for j, chip in enumerate(chips):
                copy(4 + j, (*chip, 1 - c), me).wait_recv()
            for cp in first + passed:
                cp.wait_send()

        return start, middle, finish

    return _Exchange(v, jax.ShapeDtypeStruct((8, m, n), v.dtype), 7, phases)


def _with_own(gathered, v):
    dev = 4 * lax.axis_index("x") + 2 * lax.axis_index("y") + lax.axis_index("c")
    return lax.dynamic_update_index_in_dim(gathered, v, dev, 0)


def _direct_exchange(operand, out_shape, n_copies, make_copies):
    def phases(in_ref, out_ref, send_sems, recv_sems):
        copies = make_copies(in_ref, out_ref, send_sems, recv_sems)

        def start():
            for cp in copies:
                cp.start()

        def finish():
            for cp in copies:
                cp.wait()

        return start, _nothing, finish

    return _Exchange(operand, out_shape, n_copies, phases)


def _swap_cores_exchange(blocks):
    _, m, n = blocks.shape

    def make_copies(g_ref, out_ref, send_sems, recv_sems):
        x, y, c = _my_place()
        return [pltpu.make_async_remote_copy(
            src_ref=g_ref.at[2 * j + (1 - c)], dst_ref=out_ref.at[j],
            send_sem=send_sems.at[j], recv_sem=recv_sems.at[j],
            device_id=(x, y, 1 - c), device_id_type=MESH_IDS) for j in range(4)]

    return _direct_exchange(blocks, jax.ShapeDtypeStruct((4, m, n), blocks.dtype), 4, make_copies)


def _scatter_chips_exchange(parts):
    _, m, n = parts.shape
    flips = [(1, 0), (0, 1), (1, 1)]

    def make_copies(p_ref, out_ref, send_sems, recv_sems):
        x, y, c = _my_place()
        copies = []
        for k, (fx, fy) in enumerate(flips):
            tx = 1 - x if fx else x
            ty = 1 - y if fy else y
            copies.append(pltpu.make_async_remote_copy(
                src_ref=p_ref.at[2 * tx + ty], dst_ref=out_ref.at[k],
                send_sem=send_sems.at[k], recv_sem=recv_sems.at[k],
                device_id=(tx, ty, c), device_id_type=MESH_IDS))
        return copies

    return _direct_exchange(parts, jax.ShapeDtypeStruct((3, m, n), parts.dtype), 3, make_copies)


def _swap_one_exchange(v):
    def make_copies(v_ref, out_ref, send_sems, recv_sems):
        x, y, c = _my_place()
        return [pltpu.make_async_remote_copy(src_ref=v_ref, dst_ref=out_ref, send_sem=send_sems.at[0],
                                             recv_sem=recv_sems.at[0], device_id=(x, y, 1 - c),
                                             device_id_type=MESH_IDS)]

    return _direct_exchange(v, jax.ShapeDtypeStruct(v.shape, v.dtype), 1, make_copies)


def _gather8(v, name):
    return _with_own(_run_exchange(_gather_exchange(v), name), v)


def _carried(ex, refs, n_in, n_out, n_scratch):
    ins, ex_in = refs[:n_in], refs[n_in]
    outs, ex_out = refs[n_in + 1:n_in + 1 + n_out], refs[n_in + 1 + n_out]
    at = n_in + 2 + n_out
    return ins, outs + refs[at:at + n_scratch], ex.phases(ex_in, ex_out, *refs[at + n_scratch:])


def _ada_fwd(c_all, w_ada_sh, b_ada_sh):
    nb, cols = c_all.shape[0], w_ada_sh.shape[1]
    tn = 512

    def body(c_ref, w_ref, b_ref, o_ref):
        cv = c_ref[...]
        act = (cv * jax.nn.sigmoid(cv)).astype(BF16)
        o_ref[...] = _dot(act, w_ref[...].astype(BF16)) + b_ref[...]

    return pl.pallas_call(
        body, name="ada_fwd", grid=(cols // tn,),
        out_shape=jax.ShapeDtypeStruct((nb, cols), F32),
        in_specs=[pl.BlockSpec((nb, D_MODEL), lambda j: (0, 0)),
                  pl.BlockSpec((D_MODEL, tn), lambda j: (0, j)),
                  pl.BlockSpec((1, tn), lambda j: (0, j))],
        out_specs=pl.BlockSpec((nb, tn), lambda j: (0, j)),
        compiler_params=_params(("arbitrary",)),
    )(c_all, w_ada_sh, b_ada_sh)


def _ada_bwd(c_all, dmod_sh):
    nb, cols = dmod_sh.shape
    tn = 512

    def body(c_ref, d_ref, o_ref):
        cv = c_ref[...]
        act = (cv * jax.nn.sigmoid(cv)).astype(BF16)
        o_ref[...] = _dot(act, d_ref[...].astype(BF16), TN)

    return pl.pallas_call(
        body, name="ada_bwd", grid=(cols // tn,),
        out_shape=jax.ShapeDtypeStruct((D_MODEL, cols), F32),
        in_specs=[pl.BlockSpec((nb, D_MODEL), lambda j: (0, 0)),
                  pl.BlockSpec((nb, tn), lambda j: (0, j))],
        out_specs=pl.BlockSpec((D_MODEL, tn), lambda j: (0, j)),
        compiler_params=_params(("arbitrary",)),
    )(c_all, dmod_sh)


def _sum_lead(v, name):
    k, m, n = v.shape

    def body(v_ref, o_ref):
        acc = v_ref[0]
        for i in range(1, k):
            acc = acc + v_ref[i]
        o_ref[...] = acc

    return pl.pallas_call(
        body, name=name, out_shape=jax.ShapeDtypeStruct((m, n), F32),
        in_specs=[pl.BlockSpec((k, m, n), lambda: (0, 0, 0))],
        out_specs=pl.BlockSpec((m, n), lambda: (0, 0)),
        compiler_params=_params(),
    )(v)


def _adamw_math(w, g, m, v):
    mn = ADAM_B1 * m + (1.0 - ADAM_B1) * g
    vn = ADAM_B2 * v + (1.0 - ADAM_B2) * (g * g)
    m_hat = mn / (1.0 - ADAM_B1 ** ADAM_STEP)
    v_hat = vn / (1.0 - ADAM_B2 ** ADAM_STEP)
    return -ADAM_LR * (m_hat / (jnp.sqrt(v_hat) + ADAM_EPS) + ADAM_WD * w), mn, vn


def _adamw_small(g_sum, g_b_ada, ws, ms, vs):
    n = len(SMALL)

    def body(gs_ref, gb_ref, *refs):
        outs = refs[3 * n:]
        for p in range(n):
            rows_p = SMALL[p][1] // LANES
            g = gb_ref[...] if SMALL[p][0] == "b_ada" else gs_ref[SMALL_AT[p]:SMALL_AT[p] + rows_p, :]
            d, mn, vn = _adamw_math(refs[p][...], g, refs[n + p][...], refs[2 * n + p][...])
            outs[4 * p][...] = g
            outs[4 * p + 1][...] = d
            outs[4 * p + 2][...] = mn
            outs[4 * p + 3][...] = vn

    shapes = [jax.ShapeDtypeStruct((size // LANES, LANES), F32) for _, size in SMALL for _ in range(4)]
    flat = lambda arrs: [a.reshape(-1, LANES) for a in arrs]
    res = pl.pallas_call(body, name="adamw_small", out_shape=tuple(shapes), compiler_params=_params())(
        g_sum, g_b_ada, *flat(ws), *flat(ms), *flat(vs))
    return [tuple(r.reshape(w.shape) for r in res[4 * p:4 * p + 4]) for p, w in enumerate(ws)]


def _adamw_halves(w, g_mine, g_other, m, v, c_idx, name):
    r, cols = w.shape
    half = r // 2
    tr = half
    while tr * cols * 4 > (2 << 20) and tr % 16 == 0:
        tr //= 2

    def body(c_ref, w_ref, mine_ref, other_ref, m_ref, v_ref, g_ref, d_ref, mo_ref, vo_ref):
        g = jnp.where(pl.program_id(0) == c_ref[0], mine_ref[...], other_ref[...])
        g_ref[0] = g
        d_ref[0], mo_ref[0], vo_ref[0] = _adamw_math(w_ref[0], g, m_ref[0], v_ref[0])

    full = pl.BlockSpec((1, tr, cols), lambda h, i, c: (h, i, 0))
    part = pl.BlockSpec((tr, cols), lambda h, i, c: (i, 0))
    shape = jax.ShapeDtypeStruct((2, half, cols), F32)
    grid_spec = pltpu.PrefetchScalarGridSpec(
        num_scalar_prefetch=1, grid=(2, half // tr),
        in_specs=[full, part, part, full, full], out_specs=(full, full, full, full))
    split = lambda a: a.reshape(2, half, cols)
    res = pl.pallas_call(
        body, name=name, grid_spec=grid_spec, out_shape=(shape, shape, shape, shape),
        compiler_params=_params(("arbitrary", "arbitrary")),
    )(c_idx, split(w), g_mine, g_other, split(m), split(v))
    return tuple(a.reshape(r, cols) for a in res)


def _adamw(w, g, m, v, name):
    rows, cols = w.shape
    tr = rows
    while tr * cols * 4 > (2 << 20) and tr % 16 == 0:
        tr //= 2

    def body(w_ref, g_ref, m_ref, v_ref, d_ref, mo_ref, vo_ref):
        d_ref[...], mo_ref[...], vo_ref[...] = _adamw_math(w_ref[...], g_ref[...], m_ref[...], v_ref[...])

    spec = pl.BlockSpec((tr, cols), lambda i: (i, 0))
    shape = jax.ShapeDtypeStruct((rows, cols), F32)
    return pl.pallas_call(
        body, name=name, grid=(rows // tr,), out_shape=(shape, shape, shape),
        in_specs=[spec, spec, spec, spec], out_specs=(spec, spec, spec),
        compiler_params=_params(("arbitrary",)),
    )(w, g, m, v)


def _add_rows(m, n):
    fits = [d for d in range(16, m + 1, 16) if m % d == 0 and d * n * 4 <= (5 << 19)]
    assert fits, (m, n)
    return max(fits)


def _add_pairs(blocks, recv, c_idx, name):
    _, m, n = blocks.shape
    tr = _add_rows(m, n)

    def body(c_ref, a_ref, b_ref, o_ref, ob_ref):
        s = a_ref[...] + b_ref[...]
        o_ref[...] = s
        ob_ref[...] = s.astype(BF16)

    grid_spec = pltpu.PrefetchScalarGridSpec(
        num_scalar_prefetch=1, grid=(4, m // tr),
        in_specs=[pl.BlockSpec((1, tr, n), lambda j, i, c: (2 * j + c[0], i, 0)),
                  pl.BlockSpec((1, tr, n), lambda j, i, c: (j, i, 0))],
        out_specs=(pl.BlockSpec((1, tr, n), lambda j, i, c: (j, i, 0)),
                   pl.BlockSpec((1, tr, n), lambda j, i, c: (j, i, 0))))
    return pl.pallas_call(
        body, name=name, grid_spec=grid_spec,
        out_shape=(jax.ShapeDtypeStruct((4, m, n), F32), jax.ShapeDtypeStruct((4, m, n), BF16)),
        compiler_params=_params(("arbitrary", "arbitrary")),
    )(c_idx, blocks, recv)


def _add_chips(own, recv, name):
    m, n = own.shape
    tr = _add_rows(m, n)

    def body(a_ref, r_ref, o_ref):
        acc = a_ref[...]
        for k in range(3):
            acc = acc + r_ref[k].astype(F32)
        o_ref[...] = acc

    return pl.pallas_call(
        body, name=name, grid=(m // tr,),
        out_shape=jax.ShapeDtypeStruct((m, n), F32),
        in_specs=[pl.BlockSpec((tr, n), lambda i: (i, 0)), pl.BlockSpec((3, tr, n), lambda i: (0, i, 0))],
        out_specs=pl.BlockSpec((tr, n), lambda i: (i, 0)),
        compiler_params=_params(("arbitrary",)),
    )(own, recv)


def _row_spec(cols):
    return pl.BlockSpec((ROW_TILE, cols), lambda i: (i, 0))


def _mod_spec(tiles_per_seq):
    return pl.BlockSpec((1, 8, D_MODEL), lambda i: (i // tiles_per_seq, 0, 0))


def _table_spec(tiles_per_seq):
    return pl.BlockSpec((ROW_TILE, LANES), lambda i: (i % tiles_per_seq, 0))


def _fwd_in(x, mod, ln_g, ln_b, w_in, q_g, kv_g, w_uq, w_ukv, cos_t, sin_a, sin_b, seq):
    rows = x.shape[0]
    tm = min(2 * ROW_TILE, seq)
    tps = seq // tm

    def body(x_ref, mod_ref, g_ref, b_ref, win_ref, qg_ref, kvg_ref, wuq_ref, wukv_ref, cos_ref, sa_ref, sb_ref,
             x0_ref, h_ref, qkv_ref, lat_ref, qp_ref, kp_ref, vm_ref):
        def chain(rs):
            x0, _, _ = _ln_fwd(x_ref[rs, :], g_ref[...], b_ref[...])
            x0_ref[rs, :] = x0
            h = (x0 * (1.0 + mod_ref[0, 1:2, :]) + mod_ref[0, 0:1, :]).astype(BF16)
            h_ref[rs, :] = h
            yield
            proj = _dot(h, win_ref[...])
            yield
            qkv_ref[rs, :SB_W] = (proj[:, :SB_W] * SB_SCALE).astype(BF16)
            qkv_ref[rs, SB_W:] = proj[:, SB_W:3 * SB_W].astype(BF16)
            lat_ref[rs, :] = proj[:, 3 * SB_W:3 * SB_W + Q_RANK + KV_RANK]
            cq = proj[:, 3 * SB_W:3 * SB_W + Q_RANK]
            ckv = proj[:, 3 * SB_W + Q_RANK:3 * SB_W + Q_RANK + KV_RANK]
            kr = proj[:, D_IN_PAD - LANES:]
            cos, sa, sb = cos_ref[rs, :], sa_ref[rs, :], sb_ref[rs, :]
            cqn = (cq * lax.rsqrt(_mean(cq * cq) + RMS_EPS) * qg_ref[...]).astype(BF16)
            q_all = _dot(cqn, wuq_ref[...])
            ckvn = (ckv * lax.rsqrt(_mean(ckv * ckv) + RMS_EPS) * kvg_ref[...]).astype(BF16)
            kv = _dot(ckvn, wukv_ref[...])
            yield
            for hd in range(HEADS):
                sl = slice(hd * LANES, (hd + 1) * LANES)
                qp_ref[rs, sl] = _rope(q_all[:, sl], cos, sa, sb).astype(BF16)
            kr_rot = _rope(kr, cos, sa, sb)
            for hd in range(HEADS):
                sl = slice(hd * LANES, (hd + 1) * LANES)
                kp_ref[rs, sl] = (kv[:, sl] + kr_rot).astype(BF16)
            vm_ref[rs, :] = kv[:, HEADS * LANES:].astype(BF16)

        half = tm // 2
        _staggered([chain(slice(0, half)), chain(slice(half, tm))])

    row_spec = lambda cols: pl.BlockSpec((tm, cols), lambda i: (i, 0))
    table_spec = pl.BlockSpec((tm, LANES), lambda i: (i % tps, 0))
    outs = [(D_MODEL, F32), (D_MODEL, BF16), (3 * SB_W, BF16), (Q_RANK + KV_RANK, F32),
            (HEADS * LANES, BF16), (HEADS * LANES, BF16), (MLA_W, BF16)]
    return pl.pallas_call(
        body, name="fwd_in", grid=(rows // tm,),
        out_shape=tuple(jax.ShapeDtypeStruct((rows, n), dt) for n, dt in outs),
        in_specs=[row_spec(D_MODEL), pl.BlockSpec((1, 8, D_MODEL), lambda i: (i // tps, 0, 0)),
                  _const_spec((1, D_MODEL)), _const_spec((1, D_MODEL)),
                  _const_spec(w_in.shape), _const_spec((1, Q_RANK)), _const_spec((1, KV_RANK)),
                  _const_spec(w_uq.shape), _const_spec(w_ukv.shape), table_spec, table_spec, table_spec],
        out_specs=tuple(row_spec(n) for n, _ in outs),
        compiler_params=_params(("arbitrary",)),
    )(x, mod, ln_g, ln_b, w_in, q_g, kv_g, w_uq, w_ukv, cos_t, sin_a, sin_b)


HALF = 512
SHARD = 1024


def _mlp_weight_specs():
    return [pl.BlockSpec((8, HALF, SHARD), lambda i: (0, 0, 0), pipeline_mode=pl.Buffered(1)),
            pl.BlockSpec((8, HALF, SHARD), lambda i: (0, 1, 0), pipeline_mode=pl.Buffered(1))]


def _fwd_out(sb_y, mla_y, x0, mod, w_o, ln_g, ln_b, g_mlp, seq):
    rows = x0.shape[0]
    tm = ROW_TILE
    tps = seq // tm

    def body(sb_ref, ml_ref, x0_ref, mod_ref, wo_ref, g_ref, b_ref, wu_ref, wd_ref,
             mix_ref, y1_ref, h2_ref, u_ref, ff_ref, y2_ref):
        mix = _dot(sb_ref[...], wo_ref[:SB_W, :]) + _dot(ml_ref[...].astype(BF16), wo_ref[SB_W:, :])
        mix_ref[...] = mix
        y1 = ALPHA * x0_ref[...] + (1.0 + mod_ref[0, 2:3, :]) * mix
        y1_ref[...] = y1
        x1, _, _ = _ln_fwd(y1, g_ref[...], b_ref[...])
        h2 = (x1 * (1.0 + mod_ref[0, 4:5, :]) + mod_ref[0, 3:4, :]).astype(BF16)
        h2_ref[...] = h2
        h_lo, h_hi = h2[:, :HALF], h2[:, HALF:]
        ff = jnp.zeros((tm, D_MODEL), F32)
        for chip in range(4):
            u = _dot(h_lo, wu_ref[2 * chip]) + _dot(h_hi, wu_ref[2 * chip + 1])
            u_ref[:, chip * SHARD:(chip + 1) * SHARD] = u.astype(BF16)
            act = jnp.square(jnp.maximum(u, 0.0)).astype(BF16)
            ff = ff + _dot(act[:, :HALF], wd_ref[2 * chip]) + _dot(act[:, HALF:], wd_ref[2 * chip + 1])
        ff_ref[...] = ff
        y2_ref[...] = ALPHA * x1 + (1.0 + mod_ref[0, 5:6, :]) * ff

    outs = [(D_MODEL, F32), (D_MODEL, F32), (D_MODEL, BF16), (D_FF, BF16), (D_MODEL, F32), (D_MODEL, F32)]
    return pl.pallas_call(
        body, name="fwd_out", grid=(rows // tm,),
        out_shape=tuple(jax.ShapeDtypeStruct((rows, n), dt) for n, dt in outs),
        in_specs=[_row_spec(SB_W), _row_spec(MLA_W), _row_spec(D_MODEL), _mod_spec(tps), _const_spec(w_o.shape),
                  _const_spec((1, D_MODEL)), _const_spec((1, D_MODEL))] + _mlp_weight_specs(),
        out_specs=tuple(_row_spec(n) for n, _ in outs),
        compiler_params=_params(("arbitrary",)),
    )(sb_y, mla_y, x0, mod, w_o, ln_g, ln_b, g_mlp, g_mlp)


def _staggered(chains):
    live = []
    for chain in chains:
        live.append(chain)
        live = [c for c in live if next(c, StopIteration) is not StopIteration]
    while live:
        live = [c for c in live if next(c, StopIteration) is not StopIteration]


def _acc_spec(rows=8, cols=D_MODEL):
    return pl.BlockSpec((rows, cols), lambda i: (0, 0))


def _bwd_out(y2, tgt, ff, u, y1, mix, mod, ln2_g, ln2_b, ln1_g, ln1_b, g_mlp, w_o, seq):
    rows = y2.shape[0]
    nb = rows // seq
    tm = ROW_TILE
    tps = seq // tm

    def body(y2_ref, t_ref, ff_ref, u_ref, y1_ref, mix_ref, mod_ref, g2_ref, b2_ref, g_ref, b_ref, wu_ref, wd_ref,
             wo_ref, dy1_ref, dmix_ref, do_ref, dff_ref, du_ref, acc_ref, dmod_ref):
        i = pl.program_id(0)

        @pl.when(i == 0)
        def _():
            acc_ref[...] = jnp.zeros_like(acc_ref)

        @pl.when(i % tps == 0)
        def _():
            dmod_ref[...] = jnp.zeros_like(dmod_ref)

        g2 = g2_ref[...]
        x2, xhat2, rstd2 = _ln_fwd(y2_ref[...], g2, b2_ref[...])
        err = x2 - t_ref[...]
        dx2 = err * (1.0 / D_MODEL)
        acc_ref[0:1, :] += _rowsum(dx2 * xhat2)
        acc_ref[1:2, :] += _rowsum(dx2)
        acc_ref[2:3, :] += _rowsum(err * err) * (0.5 / D_MODEL)
        dy2 = _ln_bwd(dx2, xhat2, rstd2, g2)
        dmod_ref[0, 5:6, :] += _rowsum(dy2 * ff_ref[...])
        dff = ((1.0 + mod_ref[0, 5:6, :]) * dy2).astype(BF16)
        dff_ref[...] = dff
        for blk in range(8):
            cols = slice(blk * HALF, (blk + 1) * HALF)
            da = _dot(dff, wd_ref[blk], NT)
            du_ref[:, cols] = (da * (2.0 * jnp.maximum(u_ref[:, cols].astype(F32), 0.0))).astype(BF16)

        g = g_ref[...]
        x1, xhat, rstd = _ln_fwd(y1_ref[...], g, b_ref[...])
        halves = []
        for half in range(2):
            acc = jnp.zeros((tm, HALF), F32)
            for chip in range(4):
                acc = acc + _dot(du_ref[:, chip * SHARD:(chip + 1) * SHARD], wu_ref[2 * chip + half], NT)
            halves.append(acc)
        dh2 = jnp.concatenate(halves, axis=1)
        dmod_ref[0, 3:4, :] += _rowsum(dh2)
        dmod_ref[0, 4:5, :] += _rowsum(dh2 * x1)
        dx1 = ALPHA * dy2 + dh2 * (1.0 + mod_ref[0, 4:5, :])
        acc_ref[3:4, :] += _rowsum(dx1 * xhat)
        acc_ref[4:5, :] += _rowsum(dx1)
        dy1 = _ln_bwd(dx1, xhat, rstd, g)
        dy1_ref[...] = dy1
        dmod_ref[0, 2:3, :] += _rowsum(dy1 * mix_ref[...])
        dmix = ((1.0 + mod_ref[0, 2:3, :]) * dy1).astype(BF16)
        dmix_ref[...] = dmix
        do_ref[...] = _dot(dmix, wo_ref[...], NT)

    outs = [(D_MODEL, F32), (D_MODEL, BF16), (D_MODEL, F32), (D_MODEL, BF16), (D_FF, BF16)]
    return pl.pallas_call(
        body, name="bwd_out", grid=(rows // tm,),
        out_shape=tuple(jax.ShapeDtypeStruct((rows, n), dt) for n, dt in outs)
        + (jax.ShapeDtypeStruct((8, D_MODEL), F32), jax.ShapeDtypeStruct((nb, 8, D_MODEL), F32)),
        in_specs=[_row_spec(D_MODEL), _row_spec(D_MODEL), _row_spec(D_MODEL), _row_spec(D_FF), _row_spec(D_MODEL),
                  _row_spec(D_MODEL), _mod_spec(tps), _const_spec((1, D_MODEL)), _const_spec((1, D_MODEL)),
                  _const_spec((1, D_MODEL)), _const_spec((1, D_MODEL))] + _mlp_weight_specs()
        + [_const_spec(w_o.shape)],
        out_specs=tuple(_row_spec(n) for n, _ in outs) + (_acc_spec(), _mod_spec(tps)),
        compiler_params=_params(("arbitrary",)),
    )(y2, tgt, ff, u, y1, mix, mod, ln2_g, ln2_b, ln1_g, ln1_b, g_mlp, g_mlp, w_o)


def _bwd_in(dqp, dkp, dvm, dq_sb, dk_sb, dv_sb, lat, x, dy1, mod, ln_g, ln_b, w_in, q_g, kv_g, w_uq, w_ukv,
            cos_t, sin_a, sin_b, seq):
    rows = x.shape[0]
    nb = rows // seq
    tm = ROW_TILE
    tps = seq // tm
    n_lat = Q_RANK + KV_RANK

    def body(dqp_ref, dkp_ref, dvm_ref, dqs_ref, dks_ref, dvs_ref, lat_ref, x_ref, dy1_ref, mod_ref,
             g_ref, b_ref, win_ref, qg_ref, kvg_ref, wuq_ref, wukv_ref, cos_ref, sa_ref, sb_ref,
             dx_ref, dproj_ref, dqall_ref, dkv_ref, latn_ref, acc_ref, accl_ref, dmod_ref):
        i = pl.program_id(0)

        @pl.when(i == 0)
        def _():
            acc_ref[...] = jnp.zeros_like(acc_ref)
            accl_ref[...] = jnp.zeros_like(accl_ref)

        @pl.when(i % tps == 0)
        def _():
            dmod_ref[...] = jnp.zeros_like(dmod_ref)

        cos, sa, sb = cos_ref[...], sa_ref[...], sb_ref[...]
        lane = lax.broadcasted_iota(jnp.int32, (tm, LANES), 1)
        for hd in range(HEADS):
            sl = slice(hd * LANES, (hd + 1) * LANES)
            dqall_ref[:, sl] = _rope_t(dqp_ref[:, sl], cos, sa, sb).astype(BF16)
        dcqn = _dot(dqall_ref[...], wuq_ref[...], NT)
        cq = lat_ref[:, :Q_RANK]
        qg = qg_ref[...]
        rq = lax.rsqrt(_mean(cq * cq) + RMS_EPS)
        cqn = cq * rq
        latn_ref[:, :Q_RANK] = (cqn * qg).astype(BF16)
        accl_ref[0:1, :Q_RANK] += _rowsum(dcqn * cqn)
        dqg = dcqn * qg
        dcq = rq * (dqg - cqn * _mean(dqg * cqn))
        dkr = jnp.zeros((tm, LANES), F32)
        for hd in range(HEADS):
            sl = slice(hd * LANES, (hd + 1) * LANES)
            dk = dkp_ref[:, sl]
            dkr = dkr + dk
            dkv_ref[:, sl] = jnp.where(lane < NOPE, dk, 0.0).astype(BF16)
        dkv_ref[:, HEADS * LANES:] = dvm_ref[...].astype(BF16)
        dckvn = _dot(dkv_ref[...], wukv_ref[...], NT)
        ckv = lat_ref[:, Q_RANK:]
        kvg = kvg_ref[...]
        rkv = lax.rsqrt(_mean(ckv * ckv) + RMS_EPS)
        ckvn = ckv * rkv
        latn_ref[:, Q_RANK:] = (ckvn * kvg).astype(BF16)
        accl_ref[1:2, :KV_RANK] += _rowsum(dckvn * ckvn)
        dkg = dckvn * kvg
        dckv = rkv * (dkg - ckvn * _mean(dkg * ckvn))
        dkr = _rope_t(jnp.where(lane >= NOPE, dkr, 0.0), cos, sa, sb)
        dproj_ref[:, :SB_W] = dqs_ref[...]
        dproj_ref[:, SB_W:2 * SB_W] = dks_ref[...].astype(BF16)
        dproj_ref[:, 2 * SB_W:3 * SB_W] = dvs_ref[...].astype(BF16)
        dproj_ref[:, 3 * SB_W:3 * SB_W + Q_RANK] = dcq.astype(BF16)
        dproj_ref[:, 3 * SB_W + Q_RANK:3 * SB_W + n_lat] = dckv.astype(BF16)
        dproj_ref[:, D_IN_PAD - LANES:] = dkr.astype(BF16)
        dh = _dot(dproj_ref[...], win_ref[...], NT)
        g = g_ref[...]
        x0, xhat, rstd = _ln_fwd(x_ref[...], g, b_ref[...])
        dmod_ref[0, 0:1, :] += _rowsum(dh)
        dmod_ref[0, 1:2, :] += _rowsum(dh * x0)
        dx0 = ALPHA * dy1_ref[...] + dh * (1.0 + mod_ref[0, 1:2, :])
        acc_ref[0:1, :] += _rowsum(dx0 * xhat)
        acc_ref[1:2, :] += _rowsum(dx0)
        dx_ref[...] = _ln_bwd(dx0, xhat, rstd, g)

    outs = [(D_MODEL, F32), (D_IN_PAD, BF16), (HEADS * LANES, BF16), (HEADS * LANES + MLA_W, BF16), (n_lat, BF16)]
    return pl.pallas_call(
        body, name="bwd_in", grid=(rows // tm,),
        out_shape=tuple(jax.ShapeDtypeStruct((rows, n), dt) for n, dt in outs)
        + (jax.ShapeDtypeStruct((8, D_MODEL), F32), jax.ShapeDtypeStruct((8, Q_RANK), F32),
           jax.ShapeDtypeStruct((nb, 8, D_MODEL), F32)),
        in_specs=[_row_spec(HEADS * LANES), _row_spec(HEADS * LANES), _row_spec(MLA_W),
                  _row_spec(SB_W), _row_spec(SB_W), _row_spec(SB_W), _row_spec(n_lat),
                  _row_spec(D_MODEL), _row_spec(D_MODEL), _mod_spec(tps),
                  _const_spec((1, D_MODEL)), _const_spec((1, D_MODEL)), _const_spec(w_in.shape),
                  _const_spec((1, Q_RANK)), _const_spec((1, KV_RANK)), _const_spec(w_uq.shape),
                  _const_spec(w_ukv.shape), _table_spec(tps), _table_spec(tps), _table_spec(tps)],
        out_specs=tuple(_row_spec(n) for n, _ in outs) + (_acc_spec(), _acc_spec(8, Q_RANK), _mod_spec(tps)),
        compiler_params=_params(("arbitrary",)),
    )(dqp, dkp, dvm, dq_sb, dk_sb, dv_sb, lat, x, dy1, mod, ln_g, ln_b, w_in, q_g, kv_g, w_uq, w_ukv,
      cos_t, sin_a, sin_b)


def _wgrad(a, b, name, tm=512, tn=1024, tk=2048, ex=None):
    rows, m = a.shape
    n = b.shape[1]
    tm, tn, tk = min(tm, m), min(tn, n), min(tk, rows)
    if m % tm:
        tm = m
    if n % tn:
        tn = n

    def body(a_ref, b_ref, o_ref):
        @pl.when(pl.program_id(2) == 0)
        def _():
            o_ref[...] = jnp.zeros_like(o_ref)

        o_ref[...] += _dot(a_ref[...].astype(BF16), b_ref[...].astype(BF16), TN)

    res = _carrier_call(
        body, ex, name, (m // tm, n // tn, rows // tk), [a, b],
        [pl.BlockSpec((tk, tm), lambda i, j, k: (k, i)), pl.BlockSpec((tk, tn), lambda i, j, k: (k, j))],
        [jax.ShapeDtypeStruct((m, n), F32)], [pl.BlockSpec((tm, tn), lambda i, j, k: (i, j))])
    return res[0] if ex is None else res


def _wgrad_packed(a, b, name, block_of, row_block, split=1, pre=None, into=None, tk=2048):
    rows, m = a.shape
    n = b.shape[1]
    tm = HALF
    part = tm // split
    tk = min(tk, rows)
    shape = jax.ShapeDtypeStruct((8, GROUP_MLP[0], PACK_COLS), F32)

    def body(a_ref, b_ref, *rest):
        o_ref = rest[-1]

        @pl.when(pl.program_id(2) == 0)
        def _():
            o_ref[...] = jnp.zeros_like(o_ref)

        av = a_ref[...]
        if pre == "relu2":
            av = jnp.square(jnp.maximum(av.astype(F32), 0.0))
        prod = _dot(av.astype(BF16), b_ref[...].astype(BF16), TN)
        for s in range(split):
            o_ref[s] += prod[s * part:(s + 1) * part]

    in_specs = [pl.BlockSpec((tk, tm), lambda i, j, k: (k, i)), pl.BlockSpec((tk, SHARD), lambda i, j, k: (k, j))]
    operands = [a, b]
    if into is not None:
        in_specs.append(pl.BlockSpec(memory_space=pl.ANY))
        operands.append(into)
    return pl.pallas_call(
        body, name=name, grid=(m // tm, n // SHARD, rows // tk), out_shape=shape,
        in_specs=in_specs,
        out_specs=pl.BlockSpec((split, part, SHARD), lambda i, j, k: (block_of(i, j), row_block, 0)),
        input_output_aliases={} if into is None else {2: 0},
        compiler_params=_params(("arbitrary", "arbitrary", "arbitrary")),
    )(*operands)


def _pair(pp):
    return slice(pp * LANES, (pp + 1) * LANES)


def _head_mask(lane, hh):
    return jnp.where((lane >= 64) if hh else (lane < 64), 1.0, 0.0).astype(BF16)


def _tri(t, kind):
    s = lax.broadcasted_iota(jnp.int32, (t, t), 0)
    j = lax.broadcasted_iota(jnp.int32, (t, t), 1)
    one = jnp.where(j > s if kind == "later" else j < s, 1.0, 0.0).astype(BF16)
    return jnp.concatenate([one, one], axis=1)


def _split_dot(tri2, v):
    hi = v.astype(BF16)
    lo = (v - hi.astype(F32)).astype(BF16)
    return _dot(tri2, jnp.concatenate([hi, lo], axis=0))


def _sb_logits(z, valid):
    log_keep = -(jnp.maximum(z, 0.0) + jnp.log(1.0 + jnp.exp2(jnp.abs(z) * NEG_LOG2E)))
    log_beta = z + log_keep
    if valid is not None:
        log_keep = jnp.where(valid, log_keep, 0.0)
    return log_keep, log_beta


def _carrier_call(body, ex, name, grid, operands, in_specs, out_shapes, out_specs, scratch=()):
    n_in, n_out = len(operands), len(out_shapes)
    total = grid[0] * grid[1] * grid[2]
    any_spec = pl.BlockSpec(memory_space=pl.ANY)

    def carrier(*refs):
        ins, outs, (start, middle, finish) = _carried(ex, refs, n_in, n_out, len(scratch))
        step = (pl.program_id(0) * grid[1] + pl.program_id(1)) * grid[2] + pl.program_id(2)
        pl.when(step == 0)(start)
        pl.when(step == total // 2)(middle)
        body(*ins, *outs)
        pl.when(step == total - 1)(finish)

    carried = ex is not None
    return pl.pallas_call(
        carrier if carried else body, name=name, grid=grid,
        out_shape=tuple(out_shapes) + ((ex.out_shape,) if carried else ()),
        in_specs=list(in_specs) + ([any_spec] if carried else []),
        out_specs=tuple(out_specs) + ((any_spec,) if carried else ()),
        scratch_shapes=list(scratch) + (ex.scratch if carried else []),
        compiler_params=_params(("arbitrary", "arbitrary", "arbitrary")),
    )(*operands, *([ex.operand] if carried else []))


def _sb_fwd(qkv, seq, ex=None):
    rows = qkv.shape[0]
    nb = rows // seq
    t = min(ATTN_TILE, seq)
    nq = seq // t
    assert nq <= CAR_SLOTS, (seq, t)
    ap = ATTN_PAIRS
    width = ap * LANES
    groups = SB_W // width
    hds = [(pp, hh) for pp in range(ap) for hh in range(2)]

    def body(q_ref, k_ref, v_ref, tri_ref, o_ref, car_ref, acc_ref):
        i = pl.program_id(2)
        lane = lax.broadcasted_iota(jnp.int32, (t, LANES), 1)
        key = lax.broadcasted_iota(jnp.int32, (t, t), 0)
        qry = lax.broadcasted_iota(jnp.int32, (t, t), 1)
        strict = key < qry
        tri = tri_ref[...]
        masks = [_head_mask(lane, hh) for hh in range(2)]
        qms = [q_ref[:, _pair(pp)] * masks[hh] for pp, hh in hds]
        acc_ref[...] = jnp.zeros_like(acc_ref)
        car_ref[...] = jnp.zeros_like(car_ref)

        def step(kb, c_sums, valid):
            start = pl.multiple_of(kb * t, t)
            kss = [k_ref[pl.ds(start, t), _pair(pp)] for pp in range(ap)]
            vss = [v_ref[pl.ds(start, t), _pair(pp)] for pp in range(ap)]
            zs = [_dot(kss[pp], qms[n], NT) for n, (pp, _) in enumerate(hds)]
            logs = [_sb_logits(z, valid) for z in zs]
            sufs = [_split_dot(tri, lg[0]) for lg in logs]
            new_sums = []
            for n, (pp, hh) in enumerate(hds):
                log_keep, log_beta = logs[n]
                w = jnp.exp(log_beta + sufs[n] + c_sums[n])
                if valid is not None:
                    w = jnp.where(valid, w, 0.0)
                acc_ref[pp] += _dot(vss[pp] * masks[hh], w.astype(BF16), TN)
                car_ref[0, pl.ds(n * CAR_SLOTS + kb, 1), :] = c_sums[n]
                new_sums.append(c_sums[n] + sufs[n][0:1, :] + log_keep[0:1, :])
            return tuple(new_sums)

        c_sums = step(i, tuple(jnp.zeros((1, t), F32) for _ in hds), strict)
        lax.fori_loop(0, i, lambda j, cr: step(i - 1 - j, cr, None), c_sums)
        for pp in range(ap):
            o_ref[:, _pair(pp)] = acc_ref[pp].T.astype(BF16)

    qspec = pl.BlockSpec((t, width), lambda b, p, i: (b * nq + i, p))
    car_rows = len(hds) * CAR_SLOTS
    return _carrier_call(
        body, ex, "sb_fwd", (nb, groups, nq),
        [qkv, qkv, qkv, _tri(t, "later")],
        [qspec,
         pl.BlockSpec((seq, width), lambda b, p, i: (b, groups + p)),
         pl.BlockSpec((seq, width), lambda b, p, i: (b, 2 * groups + p)),
         _const_spec((t, 2 * t))],
        [jax.ShapeDtypeStruct((rows, SB_W), BF16), jax.ShapeDtypeStruct((nb * nq, HEADS * CAR_SLOTS, t), F32)],
        [qspec, pl.BlockSpec((1, car_rows, t), lambda b, p, i: (b * nq + i, p, 0))],
        scratch=[pltpu.VMEM((ap, LANES, t), F32)])


def _sb_bwd(qkv, d_out, cars, seq, ex=None):
    rows = qkv.shape[0]
    nb = rows // seq
    t = min(ATTN_TILE, seq)
    nq = seq // t
    ap = ATTN_PAIRS
    width = ap * LANES
    groups = SB_W // width
    hds = [(pp, hh) for pp in range(ap) for hh in range(2)]

    def body(q_ref, k_ref, v_ref, do_ref, car_ref, tri_ref, pre_ref, dq_ref, dk_ref, dv_ref, dq_acc):
        i = pl.program_id(2)

        @pl.when(i == 0)
        def _():
            dk_ref[...] = jnp.zeros_like(dk_ref)
            dv_ref[...] = jnp.zeros_like(dv_ref)

        lane = lax.broadcasted_iota(jnp.int32, (t, LANES), 1)
        key = lax.broadcasted_iota(jnp.int32, (t, t), 0)
        qry = lax.broadcasted_iota(jnp.int32, (t, t), 1)
        strict = key < qry
        tri, pre = tri_ref[...], pre_ref[...]
        masks = [_head_mask(lane, hh) for hh in range(2)]
        qms = [q_ref[:, _pair(pp)] * masks[hh] for pp, hh in hds]
        doms = [do_ref[:, _pair(pp)].astype(BF16) * masks[hh] for pp, hh in hds]
        dq_acc[...] = jnp.zeros_like(dq_acc)

        def step(kb, g_pres, valid):
            start = pl.multiple_of(kb * t, t)
            kss = [k_ref[pl.ds(start, t), _pair(pp)] for pp in range(ap)]
            vss = [v_ref[pl.ds(start, t), _pair(pp)] for pp in range(ap)]
            zs = [_dot(kss[pp], qms[n], NT) for n, (pp, _) in enumerate(hds)]
            dws = [_dot(vss[pp], doms[n], NT) for n, (pp, _) in enumerate(hds)]
            logs = [_sb_logits(z, valid) for z in zs]
            sufs = [_split_dot(tri, lg[0]) for lg in logs]
            ws, gs = [], []
            for n in range(len(hds)):
                c_sum = car_ref[0, pl.ds(n * CAR_SLOTS + kb, 1), :]
                w = jnp.exp(logs[n][1] + sufs[n] + c_sum)
                if valid is not None:
                    w = jnp.where(valid, w, 0.0)
                ws.append(w)
                gs.append(dws[n] * w)
            pres = [_split_dot(pre, gs[n]) for n in range(len(hds))]
            befores = [g_pres[n] + pres[n] for n in range(len(hds))]
            for pp in range(ap):
                a, b = 2 * pp, 2 * pp + 1
                dv_ref[pl.ds(start, t), _pair(pp)] += _dot(ws[a].astype(BF16), doms[a]) + _dot(ws[b].astype(BF16), doms[b])
            dzbs = []
            for n in range(len(hds)):
                beta = jnp.exp(logs[n][1])
                dz = gs[n] * (1.0 - beta) - beta * befores[n]
                if valid is not None:
                    dz = jnp.where(valid, dz, 0.0)
                dzbs.append(dz.astype(BF16))
            for pp in range(ap):
                a, b = 2 * pp, 2 * pp + 1
                dq_acc[pp] += _dot(dzbs[a], kss[pp] * masks[0], TN) + _dot(dzbs[b], kss[pp] * masks[1], TN)
                dk_ref[pl.ds(start, t), _pair(pp)] += _dot(dzbs[a], qms[a]) + _dot(dzbs[b], qms[b])
            return tuple(g_pres[n] + pres[n][t - 1:t, :] + gs[n][t - 1:t, :] for n in range(len(hds)))

        g_pres = lax.fori_loop(0, i, lambda kb, cr: step(kb, cr, None), tuple(jnp.zeros((1, t), F32) for _ in hds))
        step(i, g_pres, strict)
        for pp in range(ap):
            dq_ref[:, _pair(pp)] = (dq_acc[pp] * SB_SCALE).astype(BF16)

    qspec = pl.BlockSpec((t, width), lambda b, p, i: (b * nq + i, p))
    kspec_out = pl.BlockSpec((seq, width), lambda b, p, i: (b, p))
    car_rows = len(hds) * CAR_SLOTS
    return _carrier_call(
        body, ex, "sb_bwd", (nb, groups, nq),
        [qkv, qkv, qkv, d_out, cars, _tri(t, "later"), _tri(t, "earlier")],
        [qspec,
         pl.BlockSpec((seq, width), lambda b, p, i: (b, groups + p)),
         pl.BlockSpec((seq, width), lambda b, p, i: (b, 2 * groups + p)),
         qspec, pl.BlockSpec((1, car_rows, t), lambda b, p, i: (b * nq + i, p, 0)),
         _const_spec((t, 2 * t)), _const_spec((t, 2 * t))],
        [jax.ShapeDtypeStruct((rows, SB_W), BF16), jax.ShapeDtypeStruct((rows, SB_W), F32),
         jax.ShapeDtypeStruct((rows, SB_W), F32)],
        [qspec, kspec_out, kspec_out],
        scratch=[pltpu.VMEM((ap, t, LANES), F32)])


def _mla_scores(ks, qh, allowed):
    s = _dot(ks, qh, NT) * (MLA_SCALE * -NEG_LOG2E)
    if allowed is not None:
        s = jnp.where(allowed, s, jnp.finfo(F32).min)
    return s


def _mla_fwd(qp, kp, vm, seq, ex=None, chunk=64):
    rows = qp.shape[0]
    nb = rows // seq
    t = min(ATTN_TILE, seq)
    nq = seq // t
    shift = int(math.log2(chunk))
    ap = ATTN_PAIRS
    width = ap * LANES
    groups = MLA_W // width
    hds = [(pp, hh) for pp in range(ap) for hh in range(2)]

    def body(q_ref, k_ref, v_ref, o_ref, lse_ref, acc_ref):
        i = pl.program_id(2)
        lane = lax.broadcasted_iota(jnp.int32, (t, LANES), 1)
        key = lax.broadcasted_iota(jnp.int32, (t, t), 0)
        qry = lax.broadcasted_iota(jnp.int32, (t, t), 1)
        allowed_diag = jnp.right_shift(key, shift) <= jnp.right_shift(qry, shift)
        masks = [_head_mask(lane, hh) for hh in range(2)]
        qhs = [q_ref[:, _pair(n)] for n in range(len(hds))]
        acc_ref[...] = jnp.zeros_like(acc_ref)

        def step(kb, carry, allowed):
            start = pl.multiple_of(kb * t, t)
            vss = [v_ref[pl.ds(start, t), _pair(pp)] for pp in range(ap)]
            scores = [_mla_scores(k_ref[pl.ds(start, t), _pair(n)], qhs[n], allowed) for n in range(len(hds))]
            new = []
            for n, (pp, hh) in enumerate(hds):
                m_run, l_run = carry[n]
                s = scores[n]
                m_new = jnp.maximum(m_run, jnp.max(s, axis=0, keepdims=True))
                p = jnp.exp2(s - m_new)
                scale = jnp.exp2(m_run - m_new)
                l_run = scale * l_run + jnp.sum(p, axis=0, keepdims=True)
                acc_ref[n] = scale * acc_ref[n] + _dot(vss[pp] * masks[hh], p.astype(BF16), TN)
                new.append((m_new, l_run))
            return tuple(new)

        init = (jnp.full((1, t), jnp.finfo(F32).min, F32), jnp.zeros((1, t), F32))
        carry = step(i, tuple(init for _ in hds), allowed_diag)
        carry = lax.fori_loop(0, i, lambda kb, cr: step(kb, cr, None), carry)
        lse_rows = []
        for pp in range(ap):
            out_t = jnp.zeros((LANES, t), F32)
            for hh in range(2):
                m_run, l_run = carry[2 * pp + hh]
                out_t = out_t + acc_ref[2 * pp + hh] / l_run
                lse_rows.append(m_run + jnp.log(l_run) * -NEG_LOG2E)
            o_ref[:, _pair(pp)] = out_t.T
        lse_t = jnp.concatenate(lse_rows + [jnp.zeros((LANES - len(hds), t), F32)], axis=0)
        lse_ref[...] = jnp.zeros_like(lse_ref)
        lse_ref[:, _pair(0)] = lse_t.T

    ospec = pl.BlockSpec((t, width), lambda b, p, i: (b * nq + i, p))
    return _carrier_call(
        body, ex, "mla_fwd", (nb, groups, nq), [qp, kp, vm],
        [pl.BlockSpec((t, 2 * width), lambda b, p, i: (b * nq + i, p)),
         pl.BlockSpec((seq, 2 * width), lambda b, p, i: (b, p)),
         pl.BlockSpec((seq, width), lambda b, p, i: (b, p))],
        [jax.ShapeDtypeStruct((rows, MLA_W), F32), jax.ShapeDtypeStruct((rows, MLA_W), F32)],
        [ospec, ospec], scratch=[pltpu.VMEM((len(hds), LANES, t), F32)])


def _mla_bwd(qp, kp, vm, d_out, out, lse, seq, ex=None, chunk=64):
    rows = qp.shape[0]
    nb = rows // seq
    t = min(ATTN_TILE, seq)
    nq = seq // t
    shift = int(math.log2(chunk))
    ap = ATTN_PAIRS
    width = ap * LANES
    groups = MLA_W // width
    hds = [(pp, hh) for pp in range(ap) for hh in range(2)]
    nh = len(hds)

    def body(q_ref, k_ref, v_ref, do_ref, o_ref, lse_ref, dq_ref, dk_ref, dv_ref):
        i = pl.program_id(2)

        @pl.when(i == 0)
        def _():
            dk_ref[...] = jnp.zeros_like(dk_ref)
            dv_ref[...] = jnp.zeros_like(dv_ref)

        lane = lax.broadcasted_iota(jnp.int32, (t, LANES), 1)
        key = lax.broadcasted_iota(jnp.int32, (t, t), 0)
        qry = lax.broadcasted_iota(jnp.int32, (t, t), 1)
        allowed_diag = jnp.right_shift(key, shift) <= jnp.right_shift(qry, shift)
        qhs = [q_ref[:, _pair(n)] for n in range(nh)]
        lse_t = lse_ref[:, _pair(0)].T
        doms, deltas, lse_hs = [], [], []
        for pp in range(ap):
            do = do_ref[:, _pair(pp)]
            d_o_t = (do * o_ref[:, _pair(pp)]).T
            for hh in range(2):
                doms.append(do.astype(BF16) * _head_mask(lane, hh))
                deltas.append(jnp.sum(d_o_t[hh * 64:(hh + 1) * 64], axis=0, keepdims=True))
                lse_hs.append(lse_t[2 * pp + hh:2 * pp + hh + 1])

        dq_ref[...] = jnp.zeros_like(dq_ref)

        def step(kb, allowed):
            start = pl.multiple_of(kb * t, t)
            vss = [v_ref[pl.ds(start, t), _pair(pp)] for pp in range(ap)]
            kss = [k_ref[pl.ds(start, t), _pair(n)] for n in range(nh)]
            scores = [_mla_scores(kss[n], qhs[n], allowed) for n in range(nh)]
            dps = [_dot(vss[pp], doms[n], NT) for n, (pp, _) in enumerate(hds)]
            ps = [jnp.exp2(scores[n] - lse_hs[n]) for n in range(nh)]
            dss = [(ps[n] * (dps[n] - deltas[n]) * MLA_SCALE).astype(BF16) for n in range(nh)]
            for pp in range(ap):
                a, b = 2 * pp, 2 * pp + 1
                dv_ref[pl.ds(start, t), _pair(pp)] += _dot(ps[a].astype(BF16), doms[a]) + _dot(ps[b].astype(BF16), doms[b])
            for n in range(nh):
                dk_ref[pl.ds(start, t), _pair(n)] += _dot(dss[n], qhs[n])
                dq_ref[:, _pair(n)] += _dot(dss[n], kss[n], TN)

        def off_diagonal(kb, nothing):
            step(kb, None)
            return nothing

        lax.fori_loop(0, i, off_diagonal, 0)
        step(i, allowed_diag)

    ospec = pl.BlockSpec((t, width), lambda b, p, i: (b * nq + i, p))
    return _carrier_call(
        body, ex, "mla_bwd", (nb, groups, nq), [qp, kp, vm, d_out, out, lse],
        [pl.BlockSpec((t, 2 * width), lambda b, p, i: (b * nq + i, p)),
         pl.BlockSpec((seq, 2 * width), lambda b, p, i: (b, p)),
         pl.BlockSpec((seq, width), lambda b, p, i: (b, p)),
         pl.BlockSpec((t, width), lambda b, p, i: (b * nq + i, groups + p)),
         ospec, ospec],
        [jax.ShapeDtypeStruct((rows, HEADS * LANES), F32), jax.ShapeDtypeStruct((rows, HEADS * LANES), F32),
         jax.ShapeDtypeStruct((rows, MLA_W), F32)],
        [pl.BlockSpec((t, 2 * width), lambda b, p, i: (b * nq + i, p)),
         pl.BlockSpec((seq, 2 * width), lambda b, p, i: (b, p)),
         pl.BlockSpec((seq, width), lambda b, p, i: (b, p))])


PACK_COLS = 1024
PACK_ALIGN = 16
GROUP_IN = (384, ((1024, 552, 1), (384, 192, 1), (256, 256, 1)))
GROUP_MLP = (1152, ((1024, 1024, 1), (1024, 1024, 0), (256, 1024, 0)))


def _pack_rows(r, c):
    return (r // 2) * c // PACK_COLS


def _slot_rows(r, c):
    return -(-_pack_rows(r, c) // PACK_ALIGN) * PACK_ALIGN


def _join_slots(parts, group):
    total, weights = group
    padded = [jnp.pad(p, ((0, 0), (0, _slot_rows(r, c) - p.shape[1]), (0, 0))) for p, (r, c, _) in zip(parts, weights)]
    used = sum(_slot_rows(r, c) for r, c, _ in weights)
    if total > used:
        padded.append(jnp.zeros((parts[0].shape[0], total - used, PACK_COLS), parts[0].dtype))
    return jnp.concatenate(padded, axis=1)


def _split_slots(packed, group):
    out, at = [], 0
    for r, c, _ in group[1]:
        out.append(packed[:, at:at + _pack_rows(r, c), :])
        at += _slot_rows(r, c)
    return out


def _pack_halves(shards, group):
    return _join_slots([s.reshape(2, _pack_rows(r, c), PACK_COLS) for s, (r, c, _) in zip(shards, group[1])], group)


def _unpack_half(packed, group):
    return [p.reshape(r // 2, c) for p, (r, c, _) in zip(_split_slots(packed[None], group), group[1])]


def _unpack_full(gathered, group):
    out = []
    for p, (r, c, axis) in zip(_split_slots(gathered, group), group[1]):
        shards = p.reshape(4, r, c)
        out.append(shards.reshape(4 * r, c) if axis == 0 else jnp.moveaxis(shards, 0, 1).reshape(r, 4 * c))
    return out


def _pack_full(grads, group):
    parts = []
    for gr, (r, c, axis) in zip(grads, group[1]):
        shards = gr.reshape(4, r, c) if axis == 0 else jnp.moveaxis(gr.reshape(r, 4, c), 1, 0)
        parts.append(shards.reshape(8, _pack_rows(r, c), PACK_COLS))
    return _join_slots(parts, group)


def _pad_w_in(w_in):
    z = jnp.zeros((D_MODEL, 1), w_in.dtype)
    return jnp.concatenate([w_in[:, :2176], jnp.tile(z, (1, 64)), w_in[:, 2176:], jnp.tile(z, (1, 32))], axis=1)


def _unpad_w_in(g):
    return jnp.concatenate([g[:, :2176], g[:, 2240:2272]], axis=1)


def _pad_heads(w, used):
    k = w.shape[0]
    w3 = w.reshape(k, HEADS, used)
    return jnp.pad(w3, ((0, 0), (0, 0), (0, LANES - used))).reshape(k, HEADS * LANES)


def _unpad_heads(g, used):
    k = g.shape[0]
    return g.reshape(k, HEADS, LANES)[:, :, :used].reshape(k, HEADS * used)


def _rope_tables(seq):
    inv_freq = 1.0 / (ROPE_BASE ** (jnp.arange(0, ROPE, 2, dtype=F32) / ROPE))
    ang = jnp.arange(seq, dtype=F32)[:, None] * inv_freq[None, :]
    cos, sin = jnp.cos(ang), jnp.sin(ang)
    one, zero = jnp.ones((seq, NOPE), F32), jnp.zeros((seq, NOPE), F32)
    z16, z32 = jnp.zeros((seq, 16), F32), jnp.zeros((seq, 32), F32)
    cos_t = jnp.concatenate([one, cos, cos, jnp.ones((seq, 32), F32)], axis=1)
    sin_a = jnp.concatenate([zero, -sin, z16, z32], axis=1)
    sin_b = jnp.concatenate([zero, z16, sin, z32], axis=1)
    return cos_t, sin_a, sin_b


SMALL = (("ln_in_g", 1024), ("ln_in_b", 1024), ("b_ada", 6144), ("q_norm_g", 384), ("kv_norm_g", 256),
         ("ln1_g", 1024), ("ln1_b", 1024), ("ln2_g", 1024), ("ln2_b", 1024))
SUBLANES = 8
SMALL_SLOTS = [-(-n // LANES // SUBLANES) * SUBLANES for _, n in SMALL]
SMALL_AT = [sum(SMALL_SLOTS[:p]) for p in range(len(SMALL))]
SMALL_ROWS = sum(SMALL_SLOTS)


def _pack_small(vals):
    parts = []
    for v, slot in zip(vals, SMALL_SLOTS):
        rows = v.reshape(-1, LANES)
        parts.append(jnp.pad(rows, ((0, slot - rows.shape[0]), (0, 0))))
    return jnp.concatenate(parts, axis=0)


def kernel(x, c, ln_in_g, ln_in_b, w_ada, b_ada, w_in, q_norm_g, kv_norm_g, w_uq, w_ukv, w_o, ln1_g, ln1_b, w_up, w_down, ln2_g, ln2_b, loss_target, m_ln_in_g, m_ln_in_b, m_w_ada, m_b_ada, m_w_in, m_q_norm_g, m_kv_norm_g, m_w_uq, m_w_ukv, m_w_o, m_ln1_g, m_ln1_b, m_w_up, m_w_down, m_ln2_g, m_ln2_b, v_ln_in_g, v_ln_in_b, v_w_ada, v_b_ada, v_w_in, v_q_norm_g, v_kv_norm_g, v_w_uq, v_w_ukv, v_w_o, v_ln1_g, v_ln1_b, v_w_up, v_w_down, v_ln2_g, v_ln2_b):
    nb, seq, _ = x.shape
    rows = nb * seq
    ix, iy, ic = lax.axis_index("x"), lax.axis_index("y"), lax.axis_index("c")
    chip = 2 * ix + iy
    dev = 2 * chip + ic

    def my_half(shards, group):
        packed = _pack_halves([s.astype(BF16) for s in shards], group)
        return lax.dynamic_index_in_dim(packed, ic, 0, keepdims=False)

    f_in, f_uq, f_ukv = _unpack_full(_gather8(my_half([w_in[0], w_uq[0], w_ukv[0]], GROUP_IN), "gather_w_in"),
                                     GROUP_IN)
    half_mlp = my_half([w_up[0], w_down[0], w_o[0]], GROUP_MLP)
    late_weights = _gather_exchange(half_mlp)
    w_in_p = _pad_w_in(f_in)
    uq3 = f_uq.reshape(Q_RANK, HEADS, NOPE + ROPE)
    w_uq_p = jnp.pad(uq3, ((0, 0), (0, 0), (0, LANES - NOPE - ROPE))).reshape(Q_RANK, HEADS * LANES)
    w_ukv_p = jnp.concatenate([_pad_heads(f_ukv[:, :HEADS * NOPE], NOPE), f_ukv[:, HEADS * NOPE:]], axis=1)

    n_all = 8 * nb
    c_all = _gather8(c.reshape(-1, LANES), "gather_c").reshape(n_all, D_MODEL)
    ada_cols = w_ada.shape[2]
    b_sh = lax.dynamic_slice_in_dim(b_ada, chip * ada_cols, ada_cols, axis=1)
    mod_sh = _ada_fwd(c_all, w_ada[0], b_sh)
    mod_g = _gather8(mod_sh, "gather_mod")[0::2]
    mod_all = jnp.moveaxis(mod_g, 0, 1).reshape(n_all, N_MOD * D_MODEL)
    mod_mine = lax.dynamic_slice_in_dim(mod_all, dev * nb, nb, axis=0).reshape(nb, N_MOD, D_MODEL)
    mod = jnp.pad(mod_mine, ((0, 0), (0, 8 - N_MOD), (0, 0)))

    cos_t, sin_a, sin_b = _rope_tables(seq)
    row2 = lambda v: v.reshape(1, -1)

    x2d = x.reshape(rows, D_MODEL)
    x0, h, qkv, lat, qp, kp, vm = _fwd_in(x2d, mod, row2(ln_in_g), row2(ln_in_b), w_in_p, q_norm_g, kv_norm_g,
                                          w_uq_p, w_ukv_p, cos_t, sin_a, sin_b, seq)
    sb_y, cars, g_mlp = _sb_fwd(qkv, seq, late_weights)
    g_mlp = _with_own(g_mlp, half_mlp)
    f_o = _split_slots(g_mlp, GROUP_MLP)[2].reshape(D_MODEL, D_MODEL)
    mla_y, lse = _mla_fwd(qp, kp, vm, seq)
    mix, y1, h2, u, ff, y2 = _fwd_out(sb_y, mla_y, x0, mod, f_o, ln1_g, ln1_b, g_mlp, seq)

    dy1, dmix, d_attn, dff, du, acc_out, dmod_a = _bwd_out(
        y2, loss_target.reshape(rows, D_MODEL), ff, u, y1, mix, mod, ln2_g, ln2_b, ln1_g, ln1_b, g_mlp, f_o, seq)
    c_idx = ic.reshape(1).astype(jnp.int32)
    blocks_mlp = _wgrad_packed(h2, du, "wgrad_up", lambda i, j: 2 * j + i, 0)
    blocks_mlp = _wgrad_packed(u, dff, "wgrad_down", lambda i, j: i, 1, pre="relu2", into=blocks_mlp)
    blocks_mlp = _wgrad_packed(sb_y, dmix, "wgrad_o_sb", lambda i, j: 0, 8, split=4, into=blocks_mlp)
    blocks_mlp = _wgrad_packed(mla_y, dmix, "wgrad_o_mla", lambda i, j: 1, 8, split=4, into=blocks_mlp)
    dq_sb, dk_sb, dv_sb, sibling_mlp = _sb_bwd(qkv, d_attn, cars, seq, _swap_cores_exchange(blocks_mlp))
    part_mlp, part_mlp_bf = _add_pairs(blocks_mlp, sibling_mlp, c_idx, "grad_add_cores_mlp")
    dqp, dkp, dvm, chips_mlp = _mla_bwd(qp, kp, vm, d_attn, mla_y, lse, seq, _scatter_chips_exchange(part_mlp_bf))
    grad_x, dproj, dqall, dkv, latn, acc0, acc_lat, dmod_c = _bwd_in(
        dqp, dkp, dvm, dq_sb, dk_sb, dv_sb, lat, x2d, dy1, mod, row2(ln_in_g), row2(ln_in_b), w_in_p,
        q_norm_g, kv_norm_g, w_uq_p, w_ukv_p, cos_t, sin_a, sin_b, seq)

    dmod = (dmod_a + dmod_c)[:, :N_MOD, :]
    small_part = _pack_small([acc0[0], acc0[1], jnp.zeros((N_MOD * D_MODEL,), F32), acc_lat[0, :Q_RANK],
                              acc_lat[1, :KV_RANK], acc_out[3], acc_out[4], acc_out[0], acc_out[1]])
    n_sum = SMALL_ROWS + D_MODEL // LANES
    payload = jnp.concatenate([small_part, acc_out[2].reshape(-1, LANES), dmod.reshape(-1, LANES)], axis=0)
    g_in_p, gathered = _wgrad(h, dproj, "wgrad_in", tn=768, ex=_gather_exchange(payload))
    gathered = _with_own(gathered, payload)
    g_in = _unpad_w_in(g_in_p)
    g_uq = _unpad_heads(_wgrad(latn[:, :Q_RANK], dqall, "wgrad_uq"), NOPE + ROPE)
    g_ukv_p = _wgrad(latn[:, Q_RANK:], dkv, "wgrad_ukv", tn=512)
    g_ukv = jnp.concatenate([_unpad_heads(g_ukv_p[:, :HEADS * LANES], NOPE), g_ukv_p[:, HEADS * LANES:]], axis=1)
    blocks_in = _pack_full([g_in, g_uq, g_ukv], GROUP_IN)
    sibling_in = _run_exchange(_swap_cores_exchange(blocks_in), "grads_in_to_sibling")
    part_in, part_in_bf = _add_pairs(blocks_in, sibling_in, c_idx, "grad_add_cores_in")
    chips_in = _run_exchange(_scatter_chips_exchange(part_in_bf), "grads_in_to_chips")

    def own(part):
        return lax.dynamic_index_in_dim(part, chip, 0, keepdims=False)

    half = jnp.concatenate([_add_chips(own(part_in), chips_in, "grad_add_chips_in"),
                            _add_chips(own(part_mlp), chips_mlp, "grad_add_chips_mlp")], axis=0)
    other = _run_exchange(_swap_one_exchange(half), "grads_halves")
    mine = _unpack_half(half[:GROUP_IN[0]], GROUP_IN) + _unpack_half(half[GROUP_IN[0]:], GROUP_MLP)
    theirs = _unpack_half(other[:GROUP_IN[0]], GROUP_IN) + _unpack_half(other[GROUP_IN[0]:], GROUP_MLP)

    small_sum = _sum_lead(gathered[:, :n_sum, :], "sum_small")
    loss = jnp.sum(small_sum[SMALL_ROWS:])
    dmod_all = gathered[:, n_sum:, :].reshape(n_all, N_MOD * D_MODEL)
    g_b_ada = _sum_lead(dmod_all.reshape(n_all, N_MOD * D_MODEL // LANES, LANES), "sum_b_ada")
    dmod_sh = lax.dynamic_slice_in_dim(dmod_all, chip * ada_cols, ada_cols, axis=1)
    g_w_ada = _ada_bwd(c_all, dmod_sh)

    res = {}
    d_ada, m_ada, v_ada = _adamw(w_ada[0], g_w_ada, m_w_ada[0], v_w_ada[0], "adamw_w_ada")
    res["w_ada"] = (g_w_ada[None], d_ada[None], m_ada[None], v_ada[None])
    sharded = {"w_in": (w_in, m_w_in, v_w_in), "w_uq": (w_uq, m_w_uq, v_w_uq), "w_ukv": (w_ukv, m_w_ukv, v_w_ukv),
               "w_up": (w_up, m_w_up, v_w_up), "w_down": (w_down, m_w_down, v_w_down), "w_o": (w_o, m_w_o, v_w_o)}
    for (name, (w, m, v)), g_mine, g_other in zip(sharded.items(), mine, theirs):
        quad = _adamw_halves(w[0], g_mine, g_other, m[0], v[0], c_idx, "adamw_" + name)
        res[name] = tuple(a[None] for a in quad)
    small_w = [ln_in_g, ln_in_b, b_ada, q_norm_g, kv_norm_g, ln1_g, ln1_b, ln2_g, ln2_b]
    small_m = [m_ln_in_g, m_ln_in_b, m_b_ada, m_q_norm_g, m_kv_norm_g, m_ln1_g, m_ln1_b, m_ln2_g, m_ln2_b]
    small_v = [v_ln_in_g, v_ln_in_b, v_b_ada, v_q_norm_g, v_kv_norm_g, v_ln1_g, v_ln1_b, v_ln2_g, v_ln2_b]
    for (name, _), quad in zip(SMALL, _adamw_small(small_sum, g_b_ada, small_w, small_m, small_v)):
        res[name] = quad

    order = ["ln_in_g", "ln_in_b", "w_ada", "b_ada", "w_in", "q_norm_g", "kv_norm_g", "w_uq", "w_ukv", "w_o",
             "ln1_g", "ln1_b", "w_up", "w_down", "ln2_g", "ln2_b"]
    outs = [loss, grad_x.reshape(nb, seq, D_MODEL)]
    for k in range(4):
        outs += [res[name][k] for name in order]
    return tuple(outs)
```

```python
import math

import jax
import jax.numpy as jnp
from jax import lax
from jax.experimental import pallas as pl
from jax.experimental.pallas import tpu as pltpu

F32 = jnp.float32
BF16 = jnp.bfloat16
MESH_IDS = pl.DeviceIdType.MESH

D_MODEL = 1024
HEADS = 8
SB_W = 512
MLA_W = 512
NOPE = 64
ROPE = 32
Q_RANK = 384
KV_RANK = 256
D_IN_PAD = 2304
D_FF = 4096
N_MOD = 6
LN_EPS = 1e-5
RMS_EPS = 1e-6
ALPHA = 2.0 ** 0.25
ROPE_BASE = 10000.0
SB_SCALE = 64 ** -0.5
NEG_LOG2E = -math.log2(math.e)
MLA_SCALE = 96 ** -0.5
ADAM_LR = 0.001
ADAM_B1 = 0.9
ADAM_B2 = 0.999
ADAM_EPS = 1e-08
ADAM_WD = 0.01
ADAM_STEP = 10

LANES = 128
ROW_TILE = 256
ATTN_TILE = 256
CAR_SLOTS = 8
ATTN_PAIRS = 4
VMEM_LIMIT = 56 << 20

NT = (((1,), (1,)), ((), ()))
TN = (((0,), (0,)), ((), ()))


def _params(sem=None):
    return pltpu.CompilerParams(vmem_limit_bytes=VMEM_LIMIT, dimension_semantics=sem)


def _const_spec(shape):
    zeros = (0,) * len(shape)
    return pl.BlockSpec(shape, lambda *_: zeros, pipeline_mode=pl.Buffered(1))


def _dot(a, b, dims=None):
    if dims is None:
        return jnp.dot(a, b, preferred_element_type=F32)
    return lax.dot_general(a, b, dims, preferred_element_type=F32)


def _mean(v):
    return jnp.mean(v, axis=-1, keepdims=True)


def _rowsum(v):
    return jnp.sum(v, axis=0, keepdims=True)


def _ln_fwd(y, g, b):
    mu = _mean(y)
    yc = y - mu
    rstd = lax.rsqrt(_mean(yc * yc) + LN_EPS)
    xhat = yc * rstd
    return xhat * g + b, xhat, rstd


def _ln_bwd(dx, xhat, rstd, g):
    dxh = dx * g
    return rstd * (dxh - _mean(dxh) - xhat * _mean(dxh * xhat))


def _rope(v, cos, sin_a, sin_b):
    return v * cos + pltpu.roll(v, 112, 1) * sin_a + pltpu.roll(v, 16, 1) * sin_b


def _rope_t(dv, cos, sin_a, sin_b):
    return dv * cos + pltpu.roll(dv * sin_a, 16, 1) + pltpu.roll(dv * sin_b, 112, 1)


def _my_place():
    return lax.axis_index("x"), lax.axis_index("y"), lax.axis_index("c")


class _Exchange:
    def __init__(self, operand, out_shape, n_copies, phases):
        self.operand = operand
        self.out_shape = out_shape
        self.phases = phases
        self.scratch = [pltpu.SemaphoreType.DMA((n_copies,)), pltpu.SemaphoreType.DMA((n_copies,))]


def _run_exchange(ex, name):
    def body(in_ref, out_ref, send_sems, recv_sems):
        for phase in ex.phases(in_ref, out_ref, send_sems, recv_sems):
            phase()

    return pl.pallas_call(
        body, name=name, out_shape=ex.out_shape,
        in_specs=[pl.BlockSpec(memory_space=pl.ANY)], out_specs=pl.BlockSpec(memory_space=pl.ANY),
        scratch_shapes=ex.scratch,
    )(ex.operand)


def _nothing():
    pass


def _gather_exchange(v):
    m, n = v.shape

    def phases(v_ref, out_ref, send_sems, recv_sems):
        x, y, c = _my_place()
        me, sibling = (x, y, c), (x, y, 1 - c)
        chips = [(1 - x, y), (x, 1 - y), (1 - x, 1 - y)]

        def rows(px, py, pc):
            return out_ref.at[4 * px + 2 * py + pc]

        def copy(k, block, to, src=None):
            return pltpu.make_async_remote_copy(
                src_ref=rows(*block) if src is None else src, dst_ref=rows(*block),
                send_sem=send_sems.at[k], recv_sem=recv_sems.at[k], device_id=to, device_id_type=MESH_IDS)

        first = [copy(0, me, sibling, src=v_ref)]
        first += [copy(1 + j, me, (*chip, c), src=v_ref) for j, chip in enumerate(chips)]
        passed = [copy(4 + j, (*chip, c), sibling) for j, chip in enumerate(chips)]

        def start():
            for cp in first:
                cp.start()

        def middle():
            for j, chip in enumerate(chips):
                copy(1 + j, (*chip, c), me).wait_recv()
                passed[j].start()

        def finish():
            copy(0, sibling, me).wait_recv()
            for j, chip in enumerate(chips):
                copy(4 + j, (*chip, 1 - c), me).wait_recv()
            for cp in first + passed:
                cp.wait_send()

        return start, middle, finish

    return _Exchange(v, jax.ShapeDtypeStruct((8, m, n), v.dtype), 7, phases)


def _with_own(gathered, v):
    dev = 4 * lax.axis_index("x") + 2 * lax.axis_index("y") + lax.axis_index("c")
    return lax.dynamic_update_index_in_dim(gathered, v, dev, 0)


def _direct_exchange(operand, out_shape, n_copies, make_copies):
    def phases(in_ref, out_ref, send_sems, recv_sems):
        copies = make_copies(in_ref, out_ref, send_sems, recv_sems)

        def start():
            for cp in copies:
                cp.start()

        def finish():
            for cp in copies:
                cp.wait()

        return start, _nothing, finish

    return _Exchange(operand, out_shape, n_copies, phases)


def _swap_cores_exchange(blocks):
    _, m, n = blocks.shape

    def make_copies(g_ref, out_ref, send_sems, recv_sems):
        x, y, c = _my_place()
        return [pltpu.make_async_remote_copy(
            src_ref=g_ref.at[2 * j + (1 - c)], dst_ref=out_ref.at[j],
            send_sem=send_sems.at[j], recv_sem=recv_sems.at[j],
            device_id=(x, y, 1 - c), device_id_type=MESH_IDS) for j in range(4)]

    return _direct_exchange(blocks, jax.ShapeDtypeStruct((4, m, n), blocks.dtype), 4, make_copies)


def _scatter_chips_exchange(parts):
    _, m, n = parts.shape
    flips = [(1, 0), (0, 1), (1, 1)]

    def make_copies(p_ref, out_ref, send_sems, recv_sems):
        x, y, c = _my_place()
        copies = []
        for k, (fx, fy) in enumerate(flips):
            tx = 1 - x if fx else x
            ty = 1 - y if fy else y
            copies.append(pltpu.make_async_remote_copy(
                src_ref=p_ref.at[2 * tx + ty], dst_ref=out_ref.at[k],
                send_sem=send_sems.at[k], recv_sem=recv_sems.at[k],
                device_id=(tx, ty, c), device_id_type=MESH_IDS))
        return copies

    return _direct_exchange(parts, jax.ShapeDtypeStruct((3, m, n), parts.dtype), 3, make_copies)


def _swap_one_exchange(v):
    def make_copies(v_ref, out_ref, send_sems, recv_sems):
        x, y, c = _my_place()
        return [pltpu.make_async_remote_copy(src_ref=v_ref, dst_ref=out_ref, send_sem=send_sems.at[0],
                                             recv_sem=recv_sems.at[0], device_id=(x, y, 1 - c),
                                             device_id_type=MESH_IDS)]

    return _direct_exchange(v, jax.ShapeDtypeStruct(v.shape, v.dtype), 1, make_copies)


def _gather8(v, name):
    return _with_own(_run_exchange(_gather_exchange(v), name), v)


def _carried(ex, refs, n_in, n_out, n_scratch):
    ins, ex_in = refs[:n_in], refs[n_in]
    outs, ex_out = refs[n_in + 1:n_in + 1 + n_out], refs[n_in + 1 + n_out]
    at = n_in + 2 + n_out
    return ins, outs + refs[at:at + n_scratch], ex.phases(ex_in, ex_out, *refs[at + n_scratch:])


def _ada_fwd(c_all, w_ada_sh, b_ada_sh):
    nb, cols = c_all.shape[0], w_ada_sh.shape[1]
    tn = 512

    def body(c_ref, w_ref, b_ref, o_ref):
        cv = c_ref[...]
        act = (cv * jax.nn.sigmoid(cv)).astype(BF16)
        o_ref[...] = _dot(act, w_ref[...].astype(BF16)) + b_ref[...]

    return pl.pallas_call(
        body, name="ada_fwd", grid=(cols // tn,),
        out_shape=jax.ShapeDtypeStruct((nb, cols), F32),
        in_specs=[pl.BlockSpec((nb, D_MODEL), lambda j: (0, 0)),
                  pl.BlockSpec((D_MODEL, tn), lambda j: (0, j)),
                  pl.BlockSpec((1, tn), lambda j: (0, j))],
        out_specs=pl.BlockSpec((nb, tn), lambda j: (0, j)),
        compiler_params=_params(("arbitrary",)),
    )(c_all, w_ada_sh, b_ada_sh)


def _ada_bwd(c_all, dmod_sh):
    nb, cols = dmod_sh.shape
    tn = 512

    def body(c_ref, d_ref, o_ref):
        cv = c_ref[...]
        act = (cv * jax.nn.sigmoid(cv)).astype(BF16)
        o_ref[...] = _dot(act, d_ref[...].astype(BF16), TN)

    return pl.pallas_call(
        body, name="ada_bwd", grid=(cols // tn,),
        out_shape=jax.ShapeDtypeStruct((D_MODEL, cols), F32),
        in_specs=[pl.BlockSpec((nb, D_MODEL), lambda j: (0, 0)),
                  pl.BlockSpec((nb, tn), lambda j: (0, j))],
        out_specs=pl.BlockSpec((D_MODEL, tn), lambda j: (0, j)),
        compiler_params=_params(("arbitrary",)),
    )(c_all, dmod_sh)


def _sum_lead(v, name):
    k, m, n = v.shape

    def body(v_ref, o_ref):
        acc = v_ref[0]
        for i in range(1, k):
            acc = acc + v_ref[i]
        o_ref[...] = acc

    return pl.pallas_call(
        body, name=name, out_shape=jax.ShapeDtypeStruct((m, n), F32),
        in_specs=[pl.BlockSpec((k, m, n), lambda: (0, 0, 0))],
        out_specs=pl.BlockSpec((m, n), lambda: (0, 0)),
        compiler_params=_params(),
    )(v)


def _adamw_math(w, g, m, v):
    mn = ADAM_B1 * m + (1.0 - ADAM_B1) * g
    vn = ADAM_B2 * v + (1.0 - ADAM_B2) * (g * g)
    m_hat = mn / (1.0 - ADAM_B1 ** ADAM_STEP)
    v_hat = vn / (1.0 - ADAM_B2 ** ADAM_STEP)
    return -ADAM_LR * (m_hat / (jnp.sqrt(v_hat) + ADAM_EPS) + ADAM_WD * w), mn, vn


def _adamw_small(g_sum, g_b_ada, ws, ms, vs):
    n = len(SMALL)

    def body(gs_ref, gb_ref, *refs):
        outs = refs[3 * n:]
        for p in range(n):
            rows_p = SMALL[p][1] // LANES
            g = gb_ref[...] if SMALL[p][0] == "b_ada" else gs_ref[SMALL_AT[p]:SMALL_AT[p] + rows_p, :]
            d, mn, vn = _adamw_math(refs[p][...], g, refs[n + p][...], refs[2 * n + p][...])
            outs[4 * p][...] = g
            outs[4 * p + 1][...] = d
            outs[4 * p + 2][...] = mn
            outs[4 * p + 3][...] = vn

    shapes = [jax.ShapeDtypeStruct((size // LANES, LANES), F32) for _, size in SMALL for _ in range(4)]
    flat = lambda arrs: [a.reshape(-1, LANES) for a in arrs]
    res = pl.pallas_call(body, name="adamw_small", out_shape=tuple(shapes), compiler_params=_params())(
        g_sum, g_b_ada, *flat(ws), *flat(ms), *flat(vs))
    return [tuple(r.reshape(w.shape) for r in res[4 * p:4 * p + 4]) for p, w in enumerate(ws)]


def _adamw_halves(w, g_mine, g_other, m, v, c_idx, name):
    r, cols = w.shape
    half = r // 2
    tr = half
    while tr * cols * 4 > (2 << 20) and tr % 16 == 0:
        tr //= 2

    def body(c_ref, w_ref, mine_ref, other_ref, m_ref, v_ref, g_ref, d_ref, mo_ref, vo_ref):
        g = jnp.where(pl.program_id(0) == c_ref[0], mine_ref[...], other_ref[...])
        g_ref[0] = g
        d_ref[0], mo_ref[0], vo_ref[0] = _adamw_math(w_ref[0], g, m_ref[0], v_ref[0])

    full = pl.BlockSpec((1, tr, cols), lambda h, i, c: (h, i, 0))
    part = pl.BlockSpec((tr, cols), lambda h, i, c: (i, 0))
    shape = jax.ShapeDtypeStruct((2, half, cols), F32)
    grid_spec = pltpu.PrefetchScalarGridSpec(
        num_scalar_prefetch=1, grid=(2, half // tr),
        in_specs=[full, part, part, full, full], out_specs=(full, full, full, full))
    split = lambda a: a.reshape(2, half, cols)
    res = pl.pallas_call(
        body, name=name, grid_spec=grid_spec, out_shape=(shape, shape, shape, shape),
        compiler_params=_params(("arbitrary", "arbitrary")),
    )(c_idx, split(w), g_mine, g_other, split(m), split(v))
    return tuple(a.reshape(r, cols) for a in res)


def _adamw(w, g, m, v, name):
    rows, cols = w.shape
    tr = rows
    while tr * cols * 4 > (2 << 20) and tr % 16 == 0:
        tr //= 2

    def body(w_ref, g_ref, m_ref, v_ref, d_ref, mo_ref, vo_ref):
        d_ref[...], mo_ref[...], vo_ref[...] = _adamw_math(w_ref[...], g_ref[...], m_ref[...], v_ref[...])

    spec = pl.BlockSpec((tr, cols), lambda i: (i, 0))
    shape = jax.ShapeDtypeStruct((rows, cols), F32)
    return pl.pallas_call(
        body, name=name, grid=(rows // tr,), out_shape=(shape, shape, shape),
        in_specs=[spec, spec, spec, spec], out_specs=(spec, spec, spec),
        compiler_params=_params(("arbitrary",)),
    )(w, g, m, v)


def _add_rows(m, n):
    fits = [d for d in range(16, m + 1, 16) if m % d == 0 and d * n * 4 <= (5 << 19)]
    assert fits, (m, n)
    return max(fits)


def _add_pairs(blocks, recv, c_idx, name):
    _, m, n = blocks.shape
    tr = _add_rows(m, n)

    def body(c_ref, a_ref, b_ref, o_ref, ob_ref):
        s = a_ref[...] + b_ref[...]
        o_ref[...] = s
        ob_ref[...] = s.astype(BF16)

    grid_spec = pltpu.PrefetchScalarGridSpec(
        num_scalar_prefetch=1, grid=(4, m // tr),
        in_specs=[pl.BlockSpec((1, tr, n), lambda j, i, c: (2 * j + c[0], i, 0)),
                  pl.BlockSpec((1, tr, n), lambda j, i, c: (j, i, 0))],
        out_specs=(pl.BlockSpec((1, tr, n), lambda j, i, c: (j, i, 0)),
                   pl.BlockSpec((1, tr, n), lambda j, i, c: (j, i, 0))))
    return pl.pallas_call(
        body, name=name, grid_spec=grid_spec,
        out_shape=(jax.ShapeDtypeStruct((4, m, n), F32), jax.ShapeDtypeStruct((4, m, n), BF16)),
        compiler_params=_params(("arbitrary", "arbitrary")),
    )(c_idx, blocks, recv)


def _add_chips(own, recv, name):
    m, n = own.shape
    tr = _add_rows(m, n)

    def body(a_ref, r_ref, o_ref):
        acc = a_ref[...]
        for k in range(3):
            acc = acc + r_ref[k].astype(F32)
        o_ref[...] = acc

    return pl.pallas_call(
        body, name=name, grid=(m // tr,),
        out_shape=jax.ShapeDtypeStruct((m, n), F32),
        in_specs=[pl.BlockSpec((tr, n), lambda i: (i, 0)), pl.BlockSpec((3, tr, n), lambda i: (0, i, 0))],
        out_specs=pl.BlockSpec((tr, n), lambda i: (i, 0)),
        compiler_params=_params(("arbitrary",)),
    )(own, recv)


def _row_spec(cols):
    return pl.BlockSpec((ROW_TILE, cols), lambda i: (i, 0))


def _mod_spec(tiles_per_seq):
    return pl.BlockSpec((1, 8, D_MODEL), lambda i: (i // tiles_per_seq, 0, 0))


def _table_spec(tiles_per_seq):
    return pl.BlockSpec((ROW_TILE, LANES), lambda i: (i % tiles_per_seq, 0))


def _fwd_in(x, mod, ln_g, ln_b, w_in, q_g, kv_g, w_uq, w_ukv, cos_t, sin_a, sin_b, seq):
    rows = x.shape[0]
    tm = min(2 * ROW_TILE, seq)
    tps = seq // tm

    def body(x_ref, mod_ref, g_ref, b_ref, win_ref, qg_ref, kvg_ref, wuq_ref, wukv_ref, cos_ref, sa_ref, sb_ref,
             x0_ref, h_ref, qkv_ref, lat_ref, qp_ref, kp_ref, vm_ref):
        def chain(rs):
            x0, _, _ = _ln_fwd(x_ref[rs, :], g_ref[...], b_ref[...])
            x0_ref[rs, :] = x0
            h = (x0 * (1.0 + mod_ref[0, 1:2, :]) + mod_ref[0, 0:1, :]).astype(BF16)
            h_ref[rs, :] = h
            yield
            proj = _dot(h, win_ref[...])
            yield
            qkv_ref[rs, :SB_W] = (proj[:, :SB_W] * SB_SCALE).astype(BF16)
            qkv_ref[rs, SB_W:] = proj[:, SB_W:3 * SB_W].astype(BF16)
            lat_ref[rs, :] = proj[:, 3 * SB_W:3 * SB_W + Q_RANK + KV_RANK]
            cq = proj[:, 3 * SB_W:3 * SB_W + Q_RANK]
            ckv = proj[:, 3 * SB_W + Q_RANK:3 * SB_W + Q_RANK + KV_RANK]
            kr = proj[:, D_IN_PAD - LANES:]
            cos, sa, sb = cos_ref[rs, :], sa_ref[rs, :], sb_ref[rs, :]
            cqn = (cq * lax.rsqrt(_mean(cq * cq) + RMS_EPS) * qg_ref[...]).astype(BF16)
            q_all = _dot(cqn, wuq_ref[...])
            ckvn = (ckv * lax.rsqrt(_mean(ckv * ckv) + RMS_EPS) * kvg_ref[...]).astype(BF16)
            kv = _dot(ckvn, wukv_ref[...])
            yield
            for hd in range(HEADS):
                sl = slice(hd * LANES, (hd + 1) * LANES)
                qp_ref[rs, sl] = _rope(q_all[:, sl], cos, sa, sb).astype(BF16)
            kr_rot = _rope(kr, cos, sa, sb)
            for hd in range(HEADS):
                sl = slice(hd * LANES, (hd + 1) * LANES)
                kp_ref[rs, sl] = (kv[:, sl] + kr_rot).astype(BF16)
            vm_ref[rs, :] = kv[:, HEADS * LANES:].astype(BF16)

        half = tm // 2
        _staggered([chain(slice(0, half)), chain(slice(half, tm))])

    row_spec = lambda cols: pl.BlockSpec((tm, cols), lambda i: (i, 0))
    table_spec = pl.BlockSpec((tm, LANES), lambda i: (i % tps, 0))
    outs = [(D_MODEL, F32), (D_MODEL, BF16), (3 * SB_W, BF16), (Q_RANK + KV_RANK, F32),
            (HEADS * LANES, BF16), (HEADS * LANES, BF16), (MLA_W, BF16)]
    return pl.pallas_call(
        body, name="fwd_in", grid=(rows // tm,),
        out_shape=tuple(jax.ShapeDtypeStruct((rows, n), dt) for n, dt in outs),
        in_specs=[row_spec(D_MODEL), pl.BlockSpec((1, 8, D_MODEL), lambda i: (i // tps, 0, 0)),
                  _const_spec((1, D_MODEL)), _const_spec((1, D_MODEL)),
                  _const_spec(w_in.shape), _const_spec((1, Q_RANK)), _const_spec((1, KV_RANK)),
                  _const_spec(w_uq.shape), _const_spec(w_ukv.shape), table_spec, table_spec, table_spec],
        out_specs=tuple(row_spec(n) for n, _ in outs),
        compiler_params=_params(("arbitrary",)),
    )(x, mod, ln_g, ln_b, w_in, q_g, kv_g, w_uq, w_ukv, cos_t, sin_a, sin_b)


HALF = 512
SHARD = 1024


def _mlp_weight_specs():
    return [pl.BlockSpec((8, HALF, SHARD), lambda i: (0, 0, 0), pipeline_mode=pl.Buffered(1)),
            pl.BlockSpec((8, HALF, SHARD), lambda i: (0, 1, 0), pipeline_mode=pl.Buffered(1))]


def _fwd_out(sb_y, mla_y, x0, mod, w_o, ln_g, ln_b, g_mlp, seq):
    rows = x0.shape[0]
    tm = ROW_TILE
    tps = seq // tm

    def body(sb_ref, ml_ref, x0_ref, mod_ref, wo_ref, g_ref, b_ref, wu_ref, wd_ref,
             mix_ref, y1_ref, h2_ref, u_ref, ff_ref, y2_ref):
        mix = _dot(sb_ref[...], wo_ref[:SB_W, :]) + _dot(ml_ref[...].astype(BF16), wo_ref[SB_W:, :])
        mix_ref[...] = mix
        y1 = ALPHA * x0_ref[...] + (1.0 + mod_ref[0, 2:3, :]) * mix
        y1_ref[...] = y1
        x1, _, _ = _ln_fwd(y1, g_ref[...], b_ref[...])
        h2 = (x1 * (1.0 + mod_ref[0, 4:5, :]) + mod_ref[0, 3:4, :]).astype(BF16)
        h2_ref[...] = h2
        h_lo, h_hi = h2[:, :HALF], h2[:, HALF:]
        ff = jnp.zeros((tm, D_MODEL), F32)
        for chip in range(4):
            u = _dot(h_lo, wu_ref[2 * chip]) + _dot(h_hi, wu_ref[2 * chip + 1])
            u_ref[:, chip * SHARD:(chip + 1) * SHARD] = u.astype(BF16)
            act = jnp.square(jnp.maximum(u, 0.0)).astype(BF16)
            ff = ff + _dot(act[:, :HALF], wd_ref[2 * chip]) + _dot(act[:, HALF:], wd_ref[2 * chip + 1])
        ff_ref[...] = ff
        y2_ref[...] = ALPHA * x1 + (1.0 + mod_ref[0, 5:6, :]) * ff

    outs = [(D_MODEL, F32), (D_MODEL, F32), (D_MODEL, BF16), (D_FF, BF16), (D_MODEL, F32), (D_MODEL, F32)]
    return pl.pallas_call(
        body, name="fwd_out", grid=(rows // tm,),
        out_shape=tuple(jax.ShapeDtypeStruct((rows, n), dt) for n, dt in outs),
        in_specs=[_row_spec(SB_W), _row_spec(MLA_W), _row_spec(D_MODEL), _mod_spec(tps), _const_spec(w_o.shape),
                  _const_spec((1, D_MODEL)), _const_spec((1, D_MODEL))] + _mlp_weight_specs(),
        out_specs=tuple(_row_spec(n) for n, _ in outs),
        compiler_params=_params(("arbitrary",)),
    )(sb_y, mla_y, x0, mod, w_o, ln_g, ln_b, g_mlp, g_mlp)


def _staggered(chains):
    live = []
    for chain in chains:
        live.append(chain)
        live = [c for c in live if next(c, StopIteration) is not StopIteration]
    while live:
        live = [c for c in live if next(c, StopIteration) is not StopIteration]


def _acc_spec(rows=8, cols=D_MODEL):
    return pl.BlockSpec((rows, cols), lambda i: (0, 0))


def _bwd_out(y2, tgt, ff, u, y1, mix, mod, ln2_g, ln2_b, ln1_g, ln1_b, g_mlp, w_o, seq):
    rows = y2.shape[0]
    nb = rows // seq
    tm = ROW_TILE
    tps = seq // tm

    def body(y2_ref, t_ref, ff_ref, u_ref, y1_ref, mix_ref, mod_ref, g2_ref, b2_ref, g_ref, b_ref, wu_ref, wd_ref,
             wo_ref, dy1_ref, dmix_ref, do_ref, dff_ref, du_ref, acc_ref, dmod_ref):
        i = pl.program_id(0)

        @pl.when(i == 0)
        def _():
            acc_ref[...] = jnp.zeros_like(acc_ref)

        @pl.when(i % tps == 0)
        def _():
            dmod_ref[...] = jnp.zeros_like(dmod_ref)

        g2 = g2_ref[...]
        x2, xhat2, rstd2 = _ln_fwd(y2_ref[...], g2, b2_ref[...])
        err = x2 - t_ref[...]
        dx2 = err * (1.0 / D_MODEL)
        acc_ref[0:1, :] += _rowsum(dx2 * xhat2)
        acc_ref[1:2, :] += _rowsum(dx2)
        acc_ref[2:3, :] += _rowsum(err * err) * (0.5 / D_MODEL)
        dy2 = _ln_bwd(dx2, xhat2, rstd2, g2)
        dmod_ref[0, 5:6, :] += _rowsum(dy2 * ff_ref[...])
        dff = ((1.0 + mod_ref[0, 5:6, :]) * dy2).astype(BF16)
        dff_ref[...] = dff
        for blk in range(8):
            cols = slice(blk * HALF, (blk + 1) * HALF)
            da = _dot(dff, wd_ref[blk], NT)
            du_ref[:, cols] = (da * (2.0 * jnp.maximum(u_ref[:, cols].astype(F32), 0.0))).astype(BF16)

        g = g_ref[...]
        x1, xhat, rstd = _ln_fwd(y1_ref[...], g, b_ref[...])
        halves = []
        for half in range(2):
            acc = jnp.zeros((tm, HALF), F32)
            for chip in range(4):
                acc = acc + _dot(du_ref[:, chip * SHARD:(chip + 1) * SHARD], wu_ref[2 * chip + half], NT)
            halves.append(acc)
        dh2 = jnp.concatenate(halves, axis=1)
        dmod_ref[0, 3:4, :] += _rowsum(dh2)
        dmod_ref[0, 4:5, :] += _rowsum(dh2 * x1)
        dx1 = ALPHA * dy2 + dh2 * (1.0 + mod_ref[0, 4:5, :])
        acc_ref[3:4, :] += _rowsum(dx1 * xhat)
        acc_ref[4:5, :] += _rowsum(dx1)
        dy1 = _ln_bwd(dx1, xhat, rstd, g)
        dy1_ref[...] = dy1
        dmod_ref[0, 2:3, :] += _rowsum(dy1 * mix_ref[...])
        dmix = ((1.0 + mod_ref[0, 2:3, :]) * dy1).astype(BF16)
        dmix_ref[...] = dmix
        do_ref[...] = _dot(dmix, wo_ref[...], NT)

    outs = [(D_MODEL, F32), (D_MODEL, BF16), (D_MODEL, F32), (D_MODEL, BF16), (D_FF, BF16)]
    return pl.pallas_call(
        body, name="bwd_out", grid=(rows // tm,),
        out_shape=tuple(jax.ShapeDtypeStruct((rows, n), dt) for n, dt in outs)
        + (jax.ShapeDtypeStruct((8, D_MODEL), F32), jax.ShapeDtypeStruct((nb, 8, D_MODEL), F32)),
        in_specs=[_row_spec(D_MODEL), _row_spec(D_MODEL), _row_spec(D_MODEL), _row_spec(D_FF), _row_spec(D_MODEL),
                  _row_spec(D_MODEL), _mod_spec(tps), _const_spec((1, D_MODEL)), _const_spec((1, D_MODEL)),
                  _const_spec((1, D_MODEL)), _const_spec((1, D_MODEL))] + _mlp_weight_specs()
        + [_const_spec(w_o.shape)],
        out_specs=tuple(_row_spec(n) for n, _ in outs) + (_acc_spec(), _mod_spec(tps)),
        compiler_params=_params(("arbitrary",)),
    )(y2, tgt, ff, u, y1, mix, mod, ln2_g, ln2_b, ln1_g, ln1_b, g_mlp, g_mlp, w_o)


def _bwd_in(dqp, dkp, dvm, dq_sb, dk_sb, dv_sb, lat, x, dy1, mod, ln_g, ln_b, w_in, q_g, kv_g, w_uq, w_ukv,
            cos_t, sin_a, sin_b, seq):
    rows = x.shape[0]
    nb = rows // seq
    tm = ROW_TILE
    tps = seq // tm
    n_lat = Q_RANK + KV_RANK

    def body(dqp_ref, dkp_ref, dvm_ref, dqs_ref, dks_ref, dvs_ref, lat_ref, x_ref, dy1_ref, mod_ref,
             g_ref, b_ref, win_ref, qg_ref, kvg_ref, wuq_ref, wukv_ref, cos_ref, sa_ref, sb_ref,
             dx_ref, dproj_ref, dqall_ref, dkv_ref, latn_ref, acc_ref, accl_ref, dmod_ref):
        i = pl.program_id(0)

        @pl.when(i == 0)
        def _():
            acc_ref[...] = jnp.zeros_like(acc_ref)
            accl_ref[...] = jnp.zeros_like(accl_ref)

        @pl.when(i % tps == 0)
        def _():
            dmod_ref[...] = jnp.zeros_like(dmod_ref)

        cos, sa, sb = cos_ref[...], sa_ref[...], sb_ref[...]
        lane = lax.broadcasted_iota(jnp.int32, (tm, LANES), 1)
        for hd in range(HEADS):
            sl = slice(hd * LANES, (hd + 1) * LANES)
            dqall_ref[:, sl] = _rope_t(dqp_ref[:, sl], cos, sa, sb).astype(BF16)
        dcqn = _dot(dqall_ref[...], wuq_ref[...], NT)
        cq = lat_ref[:, :Q_RANK]
        qg = qg_ref[...]
        rq = lax.rsqrt(_mean(cq * cq) + RMS_EPS)
        cqn = cq * rq
        latn_ref[:, :Q_RANK] = (cqn * qg).astype(BF16)
        accl_ref[0:1, :Q_RANK] += _rowsum(dcqn * cqn)
        dqg = dcqn * qg
        dcq = rq * (dqg - cqn * _mean(dqg * cqn))
        dkr = jnp.zeros((tm, LANES), F32)
        for hd in range(HEADS):
            sl = slice(hd * LANES, (hd + 1) * LANES)
            dk = dkp_ref[:, sl]
            dkr = dkr + dk
            dkv_ref[:, sl] = jnp.where(lane < NOPE, dk, 0.0).astype(BF16)
        dkv_ref[:, HEADS * LANES:] = dvm_ref[...].astype(BF16)
        dckvn = _dot(dkv_ref[...], wukv_ref[...], NT)
        ckv = lat_ref[:, Q_RANK:]
        kvg = kvg_ref[...]
        rkv = lax.rsqrt(_mean(ckv * ckv) + RMS_EPS)
        ckvn = ckv * rkv
        latn_ref[:, Q_RANK:] = (ckvn * kvg).astype(BF16)
        accl_ref[1:2, :KV_RANK] += _rowsum(dckvn * ckvn)
        dkg = dckvn * kvg
        dckv = rkv * (dkg - ckvn * _mean(dkg * ckvn))
        dkr = _rope_t(jnp.where(lane >= NOPE, dkr, 0.0), cos, sa, sb)
        dproj_ref[:, :SB_W] = dqs_ref[...]
        dproj_ref[:, SB_W:2 * SB_W] = dks_ref[...].astype(BF16)
        dproj_ref[:, 2 * SB_W:3 * SB_W] = dvs_ref[...].astype(BF16)
        dproj_ref[:, 3 * SB_W:3 * SB_W + Q_RANK] = dcq.astype(BF16)
        dproj_ref[:, 3 * SB_W + Q_RANK:3 * SB_W + n_lat] = dckv.astype(BF16)
        dproj_ref[:, D_IN_PAD - LANES:] = dkr.astype(BF16)
        dh = _dot(dproj_ref[...], win_ref[...], NT)
        g = g_ref[...]
        x0, xhat, rstd = _ln_fwd(x_ref[...], g, b_ref[...])
        dmod_ref[0, 0:1, :] += _rowsum(dh)
        dmod_ref[0, 1:2, :] += _rowsum(dh * x0)
        dx0 = ALPHA * dy1_ref[...] + dh * (1.0 + mod_ref[0, 1:2, :])
        acc_ref[0:1, :] += _rowsum(dx0 * xhat)
        acc_ref[1:2, :] += _rowsum(dx0)
        dx_ref[...] = _ln_bwd(dx0, xhat, rstd, g)

    outs = [(D_MODEL, F32), (D_IN_PAD, BF16), (HEADS * LANES, BF16), (HEADS * LANES + MLA_W, BF16), (n_lat, BF16)]
    return pl.pallas_call(
        body, name="bwd_in", grid=(rows // tm,),
        out_shape=tuple(jax.ShapeDtypeStruct((rows, n), dt) for n, dt in outs)
        + (jax.ShapeDtypeStruct((8, D_MODEL), F32), jax.ShapeDtypeStruct((8, Q_RANK), F32),
           jax.ShapeDtypeStruct((nb, 8, D_MODEL), F32)),
        in_specs=[_row_spec(HEADS * LANES), _row_spec(HEADS * LANES), _row_spec(MLA_W),
                  _row_spec(SB_W), _row_spec(SB_W), _row_spec(SB_W), _row_spec(n_lat),
                  _row_spec(D_MODEL), _row_spec(D_MODEL), _mod_spec(tps),
                  _const_spec((1, D_MODEL)), _const_spec((1, D_MODEL)), _const_spec(w_in.shape),
                  _const_spec((1, Q_RANK)), _const_spec((1, KV_RANK)), _const_spec(w_uq.shape),
                  _const_spec(w_ukv.shape), _table_spec(tps), _table_spec(tps), _table_spec(tps)],
        out_specs=tuple(_row_spec(n) for n, _ in outs) + (_acc_spec(), _acc_spec(8, Q_RANK), _mod_spec(tps)),
        compiler_params=_params(("arbitrary",)),
    )(dqp, dkp, dvm, dq_sb, dk_sb, dv_sb, lat, x, dy1, mod, ln_g, ln_b, w_in, q_g, kv_g, w_uq, w_ukv,
      cos_t, sin_a, sin_b)


def _wgrad(a, b, name, tm=512, tn=1024, tk=2048, ex=None):
    rows, m = a.shape
    n = b.shape[1]
    tm, tn, tk = min(tm, m), min(tn, n), min(tk, rows)
    if m % tm:
        tm = m
    if n % tn:
        tn = n

    def body(a_ref, b_ref, o_ref):
        @pl.when(pl.program_id(2) == 0)
        def _():
            o_ref[...] = jnp.zeros_like(o_ref)

        o_ref[...] += _dot(a_ref[...].astype(BF16), b_ref[...].astype(BF16), TN)

    res = _carrier_call(
        body, ex, name, (m // tm, n // tn, rows // tk), [a, b],
        [pl.BlockSpec((tk, tm), lambda i, j, k: (k, i)), pl.BlockSpec((tk, tn), lambda i, j, k: (k, j))],
        [jax.ShapeDtypeStruct((m, n), F32)], [pl.BlockSpec((tm, tn), lambda i, j, k: (i, j))])
    return res[0] if ex is None else res


def _wgrad_packed(a, b, name, block_of, row_block, split=1, pre=None, into=None, tk=2048):
    rows, m = a.shape
    n = b.shape[1]
    tm = HALF
    part = tm // split
    tk = min(tk, rows)
    shape = jax.ShapeDtypeStruct((8, GROUP_MLP[0], PACK_COLS), F32)

    def body(a_ref, b_ref, *rest):
        o_ref = rest[-1]

        @pl.when(pl.program_id(2) == 0)
        def _():
            o_ref[...] = jnp.zeros_like(o_ref)

        av = a_ref[...]
        if pre == "relu2":
            av = jnp.square(jnp.maximum(av.astype(F32), 0.0))
        prod = _dot(av.astype(BF16), b_ref[...].astype(BF16), TN)
        for s in range(split):
            o_ref[s] += prod[s * part:(s + 1) * part]

    in_specs = [pl.BlockSpec((tk, tm), lambda i, j, k: (k, i)), pl.BlockSpec((tk, SHARD), lambda i, j, k: (k, j))]
    operands = [a, b]
    if into is not None:
        in_specs.append(pl.BlockSpec(memory_space=pl.ANY))
        operands.append(into)
    return pl.pallas_call(
        body, name=name, grid=(m // tm, n // SHARD, rows // tk), out_shape=shape,
        in_specs=in_specs,
        out_specs=pl.BlockSpec((split, part, SHARD), lambda i, j, k: (block_of(i, j), row_block, 0)),
        input_output_aliases={} if into is None else {2: 0},
        compiler_params=_params(("arbitrary", "arbitrary", "arbitrary")),
    )(*operands)


def _pair(pp):
    return slice(pp * LANES, (pp + 1) * LANES)


def _head_mask(lane, hh):
    return jnp.where((lane >= 64) if hh else (lane < 64), 1.0, 0.0).astype(BF16)


def _tri(t, kind):
    s = lax.broadcasted_iota(jnp.int32, (t, t), 0)
    j = lax.broadcasted_iota(jnp.int32, (t, t), 1)
    one = jnp.where(j > s if kind == "later" else j < s, 1.0, 0.0).astype(BF16)
    return jnp.concatenate([one, one], axis=1)


def _split_dot(tri2, v):
    hi = v.astype(BF16)
    lo = (v - hi.astype(F32)).astype(BF16)
    return _dot(tri2, jnp.concatenate([hi, lo], axis=0))


def _sb_logits(z, valid):
    log_keep = -(jnp.maximum(z, 0.0) + jnp.log(1.0 + jnp.exp2(jnp.abs(z) * NEG_LOG2E)))
    log_beta = z + log_keep
    if valid is not None:
        log_keep = jnp.where(valid, log_keep, 0.0)
    return log_keep, log_beta


def _carrier_call(body, ex, name, grid, operands, in_specs, out_shapes, out_specs, scratch=()):
    n_in, n_out = len(operands), len(out_shapes)
    total = grid[0] * grid[1] * grid[2]
    any_spec = pl.BlockSpec(memory_space=pl.ANY)

    def carrier(*refs):
        ins, outs, (start, middle, finish) = _carried(ex, refs, n_in, n_out, len(scratch))
        step = (pl.program_id(0) * grid[1] + pl.program_id(1)) * grid[2] + pl.program_id(2)
        pl.when(step == 0)(start)
        pl.when(step == total // 2)(middle)
        body(*ins, *outs)
        pl.when(step == total - 1)(finish)

    carried = ex is not None
    return pl.pallas_call(
        carrier if carried else body, name=name, grid=grid,
        out_shape=tuple(out_shapes) + ((ex.out_shape,) if carried else ()),
        in_specs=list(in_specs) + ([any_spec] if carried else []),
        out_specs=tuple(out_specs) + ((any_spec,) if carried else ()),
        scratch_shapes=list(scratch) + (ex.scratch if carried else []),
        compiler_params=_params(("arbitrary", "arbitrary", "arbitrary")),
    )(*operands, *([ex.operand] if carried else []))


def _sb_fwd(qkv, seq, ex=None):
    rows = qkv.shape[0]
    nb = rows // seq
    t = min(ATTN_TILE, seq)
    nq = seq // t
    assert nq <= CAR_SLOTS, (seq, t)
    ap = ATTN_PAIRS
    width = ap * LANES
    groups = SB_W // width
    hds = [(pp, hh) for pp in range(ap) for hh in range(2)]

    def body(q_ref, k_ref, v_ref, tri_ref, o_ref, car_ref, acc_ref):
        i = pl.program_id(2)
        lane = lax.broadcasted_iota(jnp.int32, (t, LANES), 1)
        key = lax.broadcasted_iota(jnp.int32, (t, t), 0)
        qry = lax.broadcasted_iota(jnp.int32, (t, t), 1)
        strict = key < qry
        tri = tri_ref[...]
        masks = [_head_mask(lane, hh) for hh in range(2)]
        qms = [q_ref[:, _pair(pp)] * masks[hh] for pp, hh in hds]
        acc_ref[...] = jnp.zeros_like(acc_ref)
        car_ref[...] = jnp.zeros_like(car_ref)

        def step(kb, c_sums, valid):
            start = pl.multiple_of(kb * t, t)
            kss = [k_ref[pl.ds(start, t), _pair(pp)] for pp in range(ap)]
            vss = [v_ref[pl.ds(start, t), _pair(pp)] for pp in range(ap)]
            zs = [_dot(kss[pp], qms[n], NT) for n, (pp, _) in enumerate(hds)]
            logs = [_sb_logits(z, valid) for z in zs]
            sufs = [_split_dot(tri, lg[0]) for lg in logs]
            new_sums = []
            for n, (pp, hh) in enumerate(hds):
                log_keep, log_beta = logs[n]
                w = jnp.exp(log_beta + sufs[n] + c_sums[n])
                if valid is not None:
                    w = jnp.where(valid, w, 0.0)
                acc_ref[pp] += _dot(vss[pp] * masks[hh], w.astype(BF16), TN)
                car_ref[0, pl.ds(n * CAR_SLOTS + kb, 1), :] = c_sums[n]
                new_sums.append(c_sums[n] + sufs[n][0:1, :] + log_keep[0:1, :])
            return tuple(new_sums)

        c_sums = step(i, tuple(jnp.zeros((1, t), F32) for _ in hds), strict)
        lax.fori_loop(0, i, lambda j, cr: step(i - 1 - j, cr, None), c_sums)
        for pp in range(ap):
            o_ref[:, _pair(pp)] = acc_ref[pp].T.astype(BF16)

    qspec = pl.BlockSpec((t, width), lambda b, p, i: (b * nq + i, p))
    car_rows = len(hds) * CAR_SLOTS
    return _carrier_call(
        body, ex, "sb_fwd", (nb, groups, nq),
        [qkv, qkv, qkv, _tri(t, "later")],
        [qspec,
         pl.BlockSpec((seq, width), lambda b, p, i: (b, groups + p)),
         pl.BlockSpec((seq, width), lambda b, p, i: (b, 2 * groups + p)),
         _const_spec((t, 2 * t))],
        [jax.ShapeDtypeStruct((rows, SB_W), BF16), jax.ShapeDtypeStruct((nb * nq, HEADS * CAR_SLOTS, t), F32)],
        [qspec, pl.BlockSpec((1, car_rows, t), lambda b, p, i: (b * nq + i, p, 0))],
        scratch=[pltpu.VMEM((ap, LANES, t), F32)])


def _sb_bwd(qkv, d_out, cars, seq, ex=None):
    rows = qkv.shape[0]
    nb = rows // seq
    t = min(ATTN_TILE, seq)
    nq = seq // t
    ap = ATTN_PAIRS
    width = ap * LANES
    groups = SB_W // width
    hds = [(pp, hh) for pp in range(ap) for hh in range(2)]

    def body(q_ref, k_ref, v_ref, do_ref, car_ref, tri_ref, pre_ref, dq_ref, dk_ref, dv_ref, dq_acc):
        i = pl.program_id(2)

        @pl.when(i == 0)
        def _():
            dk_ref[...] = jnp.zeros_like(dk_ref)
            dv_ref[...] = jnp.zeros_like(dv_ref)

        lane = lax.broadcasted_iota(jnp.int32, (t, LANES), 1)
        key = lax.broadcasted_iota(jnp.int32, (t, t), 0)
        qry = lax.broadcasted_iota(jnp.int32, (t, t), 1)
        strict = key < qry
        tri, pre = tri_ref[...], pre_ref[...]
        masks = [_head_mask(lane, hh) for hh in range(2)]
        qms = [q_ref[:, _pair(pp)] * masks[hh] for pp, hh in hds]
        doms = [do_ref[:, _pair(pp)].astype(BF16) * masks[hh] for pp, hh in hds]
        dq_acc[...] = jnp.zeros_like(dq_acc)

        def step(kb, g_pres, valid):
            start = pl.multiple_of(kb * t, t)
            kss = [k_ref[pl.ds(start, t), _pair(pp)] for pp in range(ap)]
            vss = [v_ref[pl.ds(start, t), _pair(pp)] for pp in range(ap)]
            zs = [_dot(kss[pp], qms[n], NT) for n, (pp, _) in enumerate(hds)]
            dws = [_dot(vss[pp], doms[n], NT) for n, (pp, _) in enumerate(hds)]
            logs = [_sb_logits(z, valid) for z in zs]
            sufs = [_split_dot(tri, lg[0]) for lg in logs]
            ws, gs = [], []
            for n in range(len(hds)):
                c_sum = car_ref[0, pl.ds(n * CAR_SLOTS + kb, 1), :]
                w = jnp.exp(logs[n][1] + sufs[n] + c_sum)
                if valid is not None:
                    w = jnp.where(valid, w, 0.0)
                ws.append(w)
                gs.append(dws[n] * w)
            pres = [_split_dot(pre, gs[n]) for n in range(len(hds))]
            befores = [g_pres[n] + pres[n] for n in range(len(hds))]
            for pp in range(ap):
                a, b = 2 * pp, 2 * pp + 1
                dv_ref[pl.ds(start, t), _pair(pp)] += _dot(ws[a].astype(BF16), doms[a]) + _dot(ws[b].astype(BF16), doms[b])
            dzbs = []
            for n in range(len(hds)):
                beta = jnp.exp(logs[n][1])
                dz = gs[n] * (1.0 - beta) - beta * befores[n]
                if valid is not None:
                    dz = jnp.where(valid, dz, 0.0)
                dzbs.append(dz.astype(BF16))
            for pp in range(ap):
                a, b = 2 * pp, 2 * pp + 1
                dq_acc[pp] += _dot(dzbs[a], kss[pp] * masks[0], TN) + _dot(dzbs[b], kss[pp] * masks[1], TN)
                dk_ref[pl.ds(start, t), _pair(pp)] += _dot(dzbs[a], qms[a]) + _dot(dzbs[b], qms[b])
            return tuple(g_pres[n] + pres[n][t - 1:t, :] + gs[n][t - 1:t, :] for n in range(len(hds)))

        g_pres = lax.fori_loop(0, i, lambda kb, cr: step(kb, cr, None), tuple(jnp.zeros((1, t), F32) for _ in hds))
        step(i, g_pres, strict)
        for pp in range(ap):
            dq_ref[:, _pair(pp)] = (dq_acc[pp] * SB_SCALE).astype(BF16)

    qspec = pl.BlockSpec((t, width), lambda b, p, i: (b * nq + i, p))
    kspec_out = pl.BlockSpec((seq, width), lambda b, p, i: (b, p))
    car_rows = len(hds) * CAR_SLOTS
    return _carrier_call(
        body, ex, "sb_bwd", (nb, groups, nq),
        [qkv, qkv, qkv, d_out, cars, _tri(t, "later"), _tri(t, "earlier")],
        [qspec,
         pl.BlockSpec((seq, width), lambda b, p, i: (b, groups + p)),
         pl.BlockSpec((seq, width), lambda b, p, i: (b, 2 * groups + p)),
         qspec, pl.BlockSpec((1, car_rows, t), lambda b, p, i: (b * nq + i, p, 0)),
         _const_spec((t, 2 * t)), _const_spec((t, 2 * t))],
        [jax.ShapeDtypeStruct((rows, SB_W), BF16), jax.ShapeDtypeStruct((rows, SB_W), F32),
         jax.ShapeDtypeStruct((rows, SB_W), F32)],
        [qspec, kspec_out, kspec_out],
        scratch=[pltpu.VMEM((ap, t, LANES), F32)])


def _mla_scores(ks, qh, allowed):
    s = _dot(ks, qh, NT) * (MLA_SCALE * -NEG_LOG2E)
    if allowed is not None:
        s = jnp.where(allowed, s, jnp.finfo(F32).min)
    return s


def _mla_fwd(qp, kp, vm, seq, ex=None, chunk=64):
    rows = qp.shape[0]
    nb = rows // seq
    t = min(ATTN_TILE, seq)
    nq = seq // t
    shift = int(math.log2(chunk))
    ap = ATTN_PAIRS
    width = ap * LANES
    groups = MLA_W // width
    hds = [(pp, hh) for pp in range(ap) for hh in range(2)]

    def body(q_ref, k_ref, v_ref, o_ref, lse_ref, acc_ref):
        i = pl.program_id(2)
        lane = lax.broadcasted_iota(jnp.int32, (t, LANES), 1)
        key = lax.broadcasted_iota(jnp.int32, (t, t), 0)
        qry = lax.broadcasted_iota(jnp.int32, (t, t), 1)
        allowed_diag = jnp.right_shift(key, shift) <= jnp.right_shift(qry, shift)
        masks = [_head_mask(lane, hh) for hh in range(2)]
        qhs = [q_ref[:, _pair(n)] for n in range(len(hds))]
        acc_ref[...] = jnp.zeros_like(acc_ref)

        def step(kb, carry, allowed):
            start = pl.multiple_of(kb * t, t)
            vss = [v_ref[pl.ds(start, t), _pair(pp)] for pp in range(ap)]
            scores = [_mla_scores(k_ref[pl.ds(start, t), _pair(n)], qhs[n], allowed) for n in range(len(hds))]
            new = []
            for n, (pp, hh) in enumerate(hds):
                m_run, l_run = carry[n]
                s = scores[n]
                m_new = jnp.maximum(m_run, jnp.max(s, axis=0, keepdims=True))
                p = jnp.exp2(s - m_new)
                scale = jnp.exp2(m_run - m_new)
                l_run = scale * l_run + jnp.sum(p, axis=0, keepdims=True)
                acc_ref[n] = scale * acc_ref[n] + _dot(vss[pp] * masks[hh], p.astype(BF16), TN)
                new.append((m_new, l_run))
            return tuple(new)

        init = (jnp.full((1, t), jnp.finfo(F32).min, F32), jnp.zeros((1, t), F32))
        carry = step(i, tuple(init for _ in hds), allowed_diag)
        carry = lax.fori_loop(0, i, lambda kb, cr: step(kb, cr, None), carry)
        lse_rows = []
        for pp in range(ap):
            out_t = jnp.zeros((LANES, t), F32)
            for hh in range(2):
                m_run, l_run = carry[2 * pp + hh]
                out_t = out_t + acc_ref[2 * pp + hh] / l_run
                lse_rows.append(m_run + jnp.log(l_run) * -NEG_LOG2E)
            o_ref[:, _pair(pp)] = out_t.T
        lse_t = jnp.concatenate(lse_rows + [jnp.zeros((LANES - len(hds), t), F32)], axis=0)
        lse_ref[...] = jnp.zeros_like(lse_ref)
        lse_ref[:, _pair(0)] = lse_t.T

    ospec = pl.BlockSpec((t, width), lambda b, p, i: (b * nq + i, p))
    return _carrier_call(
        body, ex, "mla_fwd", (nb, groups, nq), [qp, kp, vm],
        [pl.BlockSpec((t, 2 * width), lambda b, p, i: (b * nq + i, p)),
         pl.BlockSpec((seq, 2 * width), lambda b, p, i: (b, p)),
         pl.BlockSpec((seq, width), lambda b, p, i: (b, p))],
        [jax.ShapeDtypeStruct((rows, MLA_W), F32), jax.ShapeDtypeStruct((rows, MLA_W), F32)],
        [ospec, ospec], scratch=[pltpu.VMEM((len(hds), LANES, t), F32)])


def _mla_bwd(qp, kp, vm, d_out, out, lse, seq, ex=None, chunk=64):
    rows = qp.shape[0]
    nb = rows // seq
    t = min(ATTN_TILE, seq)
    nq = seq // t
    shift = int(math.log2(chunk))
    ap = ATTN_PAIRS
    width = ap * LANES
    groups = MLA_W // width
    hds = [(pp, hh) for pp in range(ap) for hh in range(2)]
    nh = len(hds)

    def body(q_ref, k_ref, v_ref, do_ref, o_ref, lse_ref, dq_ref, dk_ref, dv_ref):
        i = pl.program_id(2)

        @pl.when(i == 0)
        def _():
            dk_ref[...] = jnp.zeros_like(dk_ref)
            dv_ref[...] = jnp.zeros_like(dv_ref)

        lane = lax.broadcasted_iota(jnp.int32, (t, LANES), 1)
        key = lax.broadcasted_iota(jnp.int32, (t, t), 0)
        qry = lax.broadcasted_iota(jnp.int32, (t, t), 1)
        allowed_diag = jnp.right_shift(key, shift) <= jnp.right_shift(qry, shift)
        qhs = [q_ref[:, _pair(n)] for n in range(nh)]
        lse_t = lse_ref[:, _pair(0)].T
        doms, deltas, lse_hs = [], [], []
        for pp in range(ap):
            do = do_ref[:, _pair(pp)]
            d_o_t = (do * o_ref[:, _pair(pp)]).T
            for hh in range(2):
                doms.append(do.astype(BF16) * _head_mask(lane, hh))
                deltas.append(jnp.sum(d_o_t[hh * 64:(hh + 1) * 64], axis=0, keepdims=True))
                lse_hs.append(lse_t[2 * pp + hh:2 * pp + hh + 1])

        dq_ref[...] = jnp.zeros_like(dq_ref)

        def step(kb, allowed):
            start = pl.multiple_of(kb * t, t)
            vss = [v_ref[pl.ds(start, t), _pair(pp)] for pp in range(ap)]
            kss = [k_ref[pl.ds(start, t), _pair(n)] for n in range(nh)]
            scores = [_mla_scores(kss[n], qhs[n], allowed) for n in range(nh)]
            dps = [_dot(vss[pp], doms[n], NT) for n, (pp, _) in enumerate(hds)]
            ps = [jnp.exp2(scores[n] - lse_hs[n]) for n in range(nh)]
            dss = [(ps[n] * (dps[n] - deltas[n]) * MLA_SCALE).astype(BF16) for n in range(nh)]
            for pp in range(ap):
                a, b = 2 * pp, 2 * pp + 1
                dv_ref[pl.ds(start, t), _pair(pp)] += _dot(ps[a].astype(BF16), doms[a]) + _dot(ps[b].astype(BF16), doms[b])
            for n in range(nh):
                dk_ref[pl.ds(start, t), _pair(n)] += _dot(dss[n], qhs[n])
                dq_ref[:, _pair(n)] += _dot(dss[n], kss[n], TN)

        def off_diagonal(kb, nothing):
            step(kb, None)
            return nothing

        lax.fori_loop(0, i, off_diagonal, 0)
        step(i, allowed_diag)

    ospec = pl.BlockSpec((t, width), lambda b, p, i: (b * nq + i, p))
    return _carrier_call(
        body, ex, "mla_bwd", (nb, groups, nq), [qp, kp, vm, d_out, out, lse],
        [pl.BlockSpec((t, 2 * width), lambda b, p, i: (b * nq + i, p)),
         pl.BlockSpec((seq, 2 * width), lambda b, p, i: (b, p)),
         pl.BlockSpec((seq, width), lambda b, p, i: (b, p)),
         pl.BlockSpec((t, width), lambda b, p, i: (b * nq + i, groups + p)),
         ospec, ospec],
        [jax.ShapeDtypeStruct((rows, HEADS * LANES), F32), jax.ShapeDtypeStruct((rows, HEADS * LANES), F32),
         jax.ShapeDtypeStruct((rows, MLA_W), F32)],
        [pl.BlockSpec((t, 2 * width), lambda b, p, i: (b * nq + i, p)),
         pl.BlockSpec((seq, 2 * width), lambda b, p, i: (b, p)),
         pl.BlockSpec((seq, width), lambda b, p, i: (b, p))])


PACK_COLS = 1024
PACK_ALIGN = 16
GROUP_IN = (384, ((1024, 552, 1), (384, 192, 1), (256, 256, 1)))
GROUP_MLP = (1152, ((1024, 1024, 1), (1024, 1024, 0), (256, 1024, 0)))


def _pack_rows(r, c):
    return (r // 2) * c // PACK_COLS


def _slot_rows(r, c):
    return -(-_pack_rows(r, c) // PACK_ALIGN) * PACK_ALIGN


def _join_slots(parts, group):
    total, weights = group
    padded = [jnp.pad(p, ((0, 0), (0, _slot_rows(r, c) - p.shape[1]), (0, 0))) for p, (r, c, _) in zip(parts, weights)]
    used = sum(_slot_rows(r, c) for r, c, _ in weights)
    if total > used:
        padded.append(jnp.zeros((parts[0].shape[0], total - used, PACK_COLS), parts[0].dtype))
    return jnp.concatenate(padded, axis=1)


def _split_slots(packed, group):
    out, at = [], 0
    for r, c, _ in group[1]:
        out.append(packed[:, at:at + _pack_rows(r, c), :])
        at += _slot_rows(r, c)
    return out


def _pack_halves(shards, group):
    return _join_slots([s.reshape(2, _pack_rows(r, c), PACK_COLS) for s, (r, c, _) in zip(shards, group[1])], group)


def _unpack_half(packed, group):
    return [p.reshape(r // 2, c) for p, (r, c, _) in zip(_split_slots(packed[None], group), group[1])]


def _unpack_full(gathered, group):
    out = []
    for p, (r, c, axis) in zip(_split_slots(gathered, group), group[1]):
        shards = p.reshape(4, r, c)
        out.append(shards.reshape(4 * r, c) if axis == 0 else jnp.moveaxis(shards, 0, 1).reshape(r, 4 * c))
    return out


def _pack_full(grads, group):
    parts = []
    for gr, (r, c, axis) in zip(grads, group[1]):
        shards = gr.reshape(4, r, c) if axis == 0 else jnp.moveaxis(gr.reshape(r, 4, c), 1, 0)
        parts.append(shards.reshape(8, _pack_rows(r, c), PACK_COLS))
    return _join_slots(parts, group)


def _pad_w_in(w_in):
    z = jnp.zeros((D_MODEL, 1), w_in.dtype)
    return jnp.concatenate([w_in[:, :2176], jnp.tile(z, (1, 64)), w_in[:, 2176:], jnp.tile(z, (1, 32))], axis=1)


def _unpad_w_in(g):
    return jnp.concatenate([g[:, :2176], g[:, 2240:2272]], axis=1)


def _pad_heads(w, used):
    k = w.shape[0]
    w3 = w.reshape(k, HEADS, used)
    return jnp.pad(w3, ((0, 0), (0, 0), (0, LANES - used))).reshape(k, HEADS * LANES)


def _unpad_heads(g, used):
    k = g.shape[0]
    return g.reshape(k, HEADS, LANES)[:, :, :used].reshape(k, HEADS * used)


def _rope_tables(seq):
    inv_freq = 1.0 / (ROPE_BASE ** (jnp.arange(0, ROPE, 2, dtype=F32) / ROPE))
    ang = jnp.arange(seq, dtype=F32)[:, None] * inv_freq[None, :]
    cos, sin = jnp.cos(ang), jnp.sin(ang)
    one, zero = jnp.ones((seq, NOPE), F32), jnp.zeros((seq, NOPE), F32)
    z16, z32 = jnp.zeros((seq, 16), F32), jnp.zeros((seq, 32), F32)
    cos_t = jnp.concatenate([one, cos, cos, jnp.ones((seq, 32), F32)], axis=1)
    sin_a = jnp.concatenate([zero, -sin, z16, z32], axis=1)
    sin_b = jnp.concatenate([zero, z16, sin, z32], axis=1)
    return cos_t, sin_a, sin_b


SMALL = (("ln_in_g", 1024), ("ln_in_b", 1024), ("b_ada", 6144), ("q_norm_g", 384), ("kv_norm_g", 256),
         ("ln1_g", 1024), ("ln1_b", 1024), ("ln2_g", 1024), ("ln2_b", 1024))
SUBLANES = 8
SMALL_SLOTS = [-(-n // LANES // SUBLANES) * SUBLANES for _, n in SMALL]
SMALL_AT = [sum(SMALL_SLOTS[:p]) for p in range(len(SMALL))]
SMALL_ROWS = sum(SMALL_SLOTS)


def _pack_small(vals):
    parts = []
    for v, slot in zip(vals, SMALL_SLOTS):
        rows = v.reshape(-1, LANES)
        parts.append(jnp.pad(rows, ((0, slot - rows.shape[0]), (0, 0))))
    return jnp.concatenate(parts, axis=0)


def kernel(x, c, ln_in_g, ln_in_b, w_ada, b_ada, w_in, q_norm_g, kv_norm_g, w_uq, w_ukv, w_o, ln1_g, ln1_b, w_up, w_down, ln2_g, ln2_b, loss_target, m_ln_in_g, m_ln_in_b, m_w_ada, m_b_ada, m_w_in, m_q_norm_g, m_kv_norm_g, m_w_uq, m_w_ukv, m_w_o, m_ln1_g, m_ln1_b, m_w_up, m_w_down, m_ln2_g, m_ln2_b, v_ln_in_g, v_ln_in_b, v_w_ada, v_b_ada, v_w_in, v_q_norm_g, v_kv_norm_g, v_w_uq, v_w_ukv, v_w_o, v_ln1_g, v_ln1_b, v_w_up, v_w_down, v_ln2_g, v_ln2_b):
    nb, seq, _ = x.shape
    rows = nb * seq
    ix, iy, ic = lax.axis_index("x"), lax.axis_index("y"), lax.axis_index("c")
    chip = 2 * ix + iy
    dev = 2 * chip + ic

    def my_half(shards, group):
        packed = _pack_halves([s.astype(BF16) for s in shards], group)
        return lax.dynamic_index_in_dim(packed, ic, 0, keepdims=False)

    n_all = 8 * nb
    c_all = _gather8(c.reshape(-1, LANES), "gather_c").reshape(n_all, D_MODEL)
    ada_cols = w_ada.shape[2]
    b_sh = lax.dynamic_slice_in_dim(b_ada, chip * ada_cols, ada_cols, axis=1)
    mod_sh = _ada_fwd(c_all, w_ada[0], b_sh)

    half_in = my_half([w_in[0], w_uq[0], w_ukv[0]], GROUP_IN)
    mod_bits = lax.bitcast_convert_type(mod_sh, BF16).reshape(-1, PACK_COLS)
    first = _gather8(jnp.concatenate([half_in, mod_bits], axis=0), "gather_w_in")
    f_in, f_uq, f_ukv = _unpack_full(first[:, :GROUP_IN[0]], GROUP_IN)
    mod_g = lax.bitcast_convert_type(first[:, GROUP_IN[0]:].reshape(8, n_all, ada_cols, 2), F32)[0::2]
    half_mlp = my_half([w_up[0], w_down[0], w_o[0]], GROUP_MLP)
    late_weights = _gather_exchange(half_mlp)
    w_in_p = _pad_w_in(f_in)
    uq3 = f_uq.reshape(Q_RANK, HEADS, NOPE + ROPE)
    w_uq_p = jnp.pad(uq3, ((0, 0), (0, 0), (0, LANES - NOPE - ROPE))).reshape(Q_RANK, HEADS * LANES)
    w_ukv_p = jnp.concatenate([_pad_heads(f_ukv[:, :HEADS * NOPE], NOPE), f_ukv[:, HEADS * NOPE:]], axis=1)
    mod_all = jnp.moveaxis(mod_g, 0, 1).reshape(n_all, N_MOD * D_MODEL)
    mod_mine = lax.dynamic_slice_in_dim(mod_all, dev * nb, nb, axis=0).reshape(nb, N_MOD, D_MODEL)
    mod = jnp.pad(mod_mine, ((0, 0), (0, 8 - N_MOD), (0, 0)))

    cos_t, sin_a, sin_b = _rope_tables(seq)
    row2 = lambda v: v.reshape(1, -1)

    x2d = x.reshape(rows, D_MODEL)
    x0, h, qkv, lat, qp, kp, vm = _fwd_in(x2d, mod, row2(ln_in_g), row2(ln_in_b), w_in_p, q_norm_g, kv_norm_g,
                                          w_uq_p, w_ukv_p, cos_t, sin_a, sin_b, seq)
    sb_y, cars, g_mlp = _sb_fwd(qkv, seq, late_weights)
    g_mlp = _with_own(g_mlp, half_mlp)
    f_o = _split_slots(g_mlp, GROUP_MLP)[2].reshape(D_MODEL, D_MODEL)
    mla_y, lse = _mla_fwd(qp, kp, vm, seq)
    mix, y1, h2, u, ff, y2 = _fwd_out(sb_y, mla_y, x0, mod, f_o, ln1_g, ln1_b, g_mlp, seq)

    dy1, dmix, d_attn, dff, du, acc_out, dmod_a = _bwd_out(
        y2, loss_target.reshape(rows, D_MODEL), ff, u, y1, mix, mod, ln2_g, ln2_b, ln1_g, ln1_b, g_mlp, f_o, seq)
    c_idx = ic.reshape(1).astype(jnp.int32)
    blocks_mlp = _wgrad_packed(h2, du, "wgrad_up", lambda i, j: 2 * j + i, 0)
    blocks_mlp = _wgrad_packed(u, dff, "wgrad_down", lambda i, j: i, 1, pre="relu2", into=blocks_mlp)
    blocks_mlp = _wgrad_packed(sb_y, dmix, "wgrad_o_sb", lambda i, j: 0, 8, split=4, into=blocks_mlp)
    blocks_mlp = _wgrad_packed(mla_y, dmix, "wgrad_o_mla", lambda i, j: 1, 8, split=4, into=blocks_mlp)
    dq_sb, dk_sb, dv_sb, sibling_mlp = _sb_bwd(qkv, d_attn, cars, seq, _swap_cores_exchange(blocks_mlp))
    part_mlp, part_mlp_bf = _add_pairs(blocks_mlp, sibling_mlp, c_idx, "grad_add_cores_mlp")
    dqp, dkp, dvm, chips_mlp = _mla_bwd(qp, kp, vm, d_attn, mla_y, lse, seq, _scatter_chips_exchange(part_mlp_bf))
    grad_x, dproj, dqall, dkv, latn, acc0, acc_lat, dmod_c = _bwd_in(
        dqp, dkp, dvm, dq_sb, dk_sb, dv_sb, lat, x2d, dy1, mod, row2(ln_in_g), row2(ln_in_b), w_in_p,
        q_norm_g, kv_norm_g, w_uq_p, w_ukv_p, cos_t, sin_a, sin_b, seq)

    dmod = (dmod_a + dmod_c)[:, :N_MOD, :]
    small_part = _pack_small([acc0[0], acc0[1], jnp.zeros((N_MOD * D_MODEL,), F32), acc_lat[0, :Q_RANK],
                              acc_lat[1, :KV_RANK], acc_out[3], acc_out[4], acc_out[0], acc_out[1]])
    n_sum = SMALL_ROWS + D_MODEL // LANES
    payload = jnp.concatenate([small_part, acc_out[2].reshape(-1, LANES), dmod.reshape(-1, LANES)], axis=0)
    g_in_p, gathered = _wgrad(h, dproj, "wgrad_in", tn=768, ex=_gather_exchange(payload))
    gathered = _with_own(gathered, payload)
    g_in = _unpad_w_in(g_in_p)
    g_uq = _unpad_heads(_wgrad(latn[:, :Q_RANK], dqall, "wgrad_uq"), NOPE + ROPE)
    g_ukv_p = _wgrad(latn[:, Q_RANK:], dkv, "wgrad_ukv", tn=512)
    g_ukv = jnp.concatenate([_unpad_heads(g_ukv_p[:, :HEADS * LANES], NOPE), g_ukv_p[:, HEADS * LANES:]], axis=1)
    blocks_in = _pack_full([g_in, g_uq, g_ukv], GROUP_IN)
    sibling_in = _run_exchange(_swap_cores_exchange(blocks_in), "grads_in_to_sibling")
    part_in, part_in_bf = _add_pairs(blocks_in, sibling_in, c_idx, "grad_add_cores_in")
    chips_in = _run_exchange(_scatter_chips_exchange(part_in_bf), "grads_in_to_chips")

    def own(part):
        return lax.dynamic_index_in_dim(part, chip, 0, keepdims=False)

    half = jnp.concatenate([_add_chips(own(part_in), chips_in, "grad_add_chips_in"),
                            _add_chips(own(part_mlp), chips_mlp, "grad_add_chips_mlp")], axis=0)
    other = _run_exchange(_swap_one_exchange(half), "grads_halves")
    mine = _unpack_half(half[:GROUP_IN[0]], GROUP_IN) + _unpack_half(half[GROUP_IN[0]:], GROUP_MLP)
    theirs = _unpack_half(other[:GROUP_IN[0]], GROUP_IN) + _unpack_half(other[GROUP_IN[0]:], GROUP_MLP)

    small_sum = _sum_lead(gathered[:, :n_sum, :], "sum_small")
    loss = jnp.sum(small_sum[SMALL_ROWS:])
    dmod_all = gathered[:, n_sum:, :].reshape(n_all, N_MOD * D_MODEL)
    g_b_ada = _sum_lead(dmod_all.reshape(n_all, N_MOD * D_MODEL // LANES, LANES), "sum_b_ada")
    dmod_sh = lax.dynamic_slice_in_dim(dmod_all, chip * ada_cols, ada_cols, axis=1)
    g_w_ada = _ada_bwd(c_all, dmod_sh)

    res = {}
    d_ada, m_ada, v_ada = _adamw(w_ada[0], g_w_ada, m_w_ada[0], v_w_ada[0], "adamw_w_ada")
    res["w_ada"] = (g_w_ada[None], d_ada[None], m_ada[None], v_ada[None])
    sharded = {"w_in": (w_in, m_w_in, v_w_in), "w_uq": (w_uq, m_w_uq, v_w_uq), "w_ukv": (w_ukv, m_w_ukv, v_w_ukv),
               "w_up": (w_up, m_w_up, v_w_up), "w_down": (w_down, m_w_down, v_w_down), "w_o": (w_o, m_w_o, v_w_o)}
    for (name, (w, m, v)), g_mine, g_other in zip(sharded.items(), mine, theirs):
        quad = _adamw_halves(w[0], g_mine, g_other, m[0], v[0], c_idx, "adamw_" + name)
        res[name] = tuple(a[None] for a in quad)
    small_w = [ln_in_g, ln_in_b, b_ada, q_norm_g, kv_norm_g, ln1_g, ln1_b, ln2_g, ln2_b]
    small_m = [m_ln_in_g, m_ln_in_b, m_b_ada, m_q_norm_g, m_kv_norm_g, m_ln1_g, m_ln1_b, m_ln2_g, m_ln2_b]
    small_v = [v_ln_in_g, v_ln_in_b, v_b_ada, v_q_norm_g, v_kv_norm_g, v_ln1_g, v_ln1_b, v_ln2_g, v_ln2_b]
    for (name, _), quad in zip(SMALL, _adamw_small(small_sum, g_b_ada, small_w, small_m, small_v)):
        res[name] = quad

    order = ["ln_in_g", "ln_in_b", "w_ada", "b_ada", "w_in", "q_norm_g", "kv_norm_g", "w_uq", "w_ukv", "w_o",
             "ln1_g", "ln1_b", "w_up", "w_down", "ln2_g", "ln2_b"]
    outs = [loss, grad_x.reshape(nb, seq, D_MODEL)]
    for k in range(4):
        outs += [res[name][k] for name in order]
    return tuple(outs)
```

```python
import math

import jax
import jax.numpy as jnp
from jax import lax
from jax.experimental import pallas as pl
from jax.experimental.pallas import tpu as pltpu

F32 = jnp.float32
BF16 = jnp.bfloat16
MESH_IDS = pl.DeviceIdType.MESH

D_MODEL = 1024
HEADS = 8
SB_W = 512
MLA_W = 512
NOPE = 64
ROPE = 32
Q_RANK = 384
KV_RANK = 256
D_IN_PAD = 2304
D_FF = 4096
N_MOD = 6
LN_EPS = 1e-5
RMS_EPS = 1e-6
ALPHA = 2.0 ** 0.25
ROPE_BASE = 10000.0
SB_SCALE = 64 ** -0.5
NEG_LOG2E = -math.log2(math.e)
MLA_SCALE = 96 ** -0.5
ADAM_LR = 0.001
ADAM_B1 = 0.9
ADAM_B2 = 0.999
ADAM_EPS = 1e-08
ADAM_WD = 0.01
ADAM_STEP = 10

LANES = 128
ROW_TILE = 256
ATTN_TILE = 256
CAR_SLOTS = 8
ATTN_PAIRS = 4
VMEM_LIMIT = 56 << 20

NT = (((1,), (1,)), ((), ()))
TN = (((0,), (0,)), ((), ()))


def _params(sem=None):
    return pltpu.CompilerParams(vmem_limit_bytes=VMEM_LIMIT, dimension_semantics=sem)


def _const_spec(shape):
    zeros = (0,) * len(shape)
    return pl.BlockSpec(shape, lambda *_: zeros, pipeline_mode=pl.Buffered(1))


def _dot(a, b, dims=None):
    if dims is None:
        return jnp.dot(a, b, preferred_element_type=F32)
    return lax.dot_general(a, b, dims, preferred_element_type=F32)


def _mean(v):
    return jnp.mean(v, axis=-1, keepdims=True)


def _rowsum(v):
    return jnp.sum(v, axis=0, keepdims=True)


def _ln_fwd(y, g, b):
    mu = _mean(y)
    yc = y - mu
    rstd = lax.rsqrt(_mean(yc * yc) + LN_EPS)
    xhat = yc * rstd
    return xhat * g + b, xhat, rstd


def _ln_bwd(dx, xhat, rstd, g):
    dxh = dx * g
    return rstd * (dxh - _mean(dxh) - xhat * _mean(dxh * xhat))


def _rope(v, cos, sin_a, sin_b):
    return v * cos + pltpu.roll(v, 112, 1) * sin_a + pltpu.roll(v, 16, 1) * sin_b


def _rope_t(dv, cos, sin_a, sin_b):
    return dv * cos + pltpu.roll(dv * sin_a, 16, 1) + pltpu.roll(dv * sin_b, 112, 1)


def _my_place():
    return lax.axis_index("x"), lax.axis_index("y"), lax.axis_index("c")


class _Exchange:
    def __init__(self, operand, out_shape, n_copies, phases):
        self.operand = operand
        self.out_shape = out_shape
        self.phases = phases
        self.scratch = [pltpu.SemaphoreType.DMA((n_copies,)), pltpu.SemaphoreType.DMA((n_copies,))]


def _run_exchange(ex, name):
    def body(in_ref, out_ref, send_sems, recv_sems):
        for phase in ex.phases(in_ref, out_ref, send_sems, recv_sems):
            phase()

    return pl.pallas_call(
        body, name=name, out_shape=ex.out_shape,
        in_specs=[pl.BlockSpec(memory_space=pl.ANY)], out_specs=pl.BlockSpec(memory_space=pl.ANY),
        scratch_shapes=ex.scratch,
    )(ex.operand)


def _nothing():
    pass


def _gather_exchange(v):
    m, n = v.shape

    def phases(v_ref, out_ref, send_sems, recv_sems):
        x, y, c = _my_place()
        me, sibling = (x, y, c), (x, y, 1 - c)
        chips = [(1 - x, y), (x, 1 - y), (1 - x, 1 - y)]

        def rows(px, py, pc):
            return out_ref.at[4 * px + 2 * py + pc]

        def copy(k, block, to, src=None):
            return pltpu.make_async_remote_copy(
                src_ref=rows(*block) if src is None else src, dst_ref=rows(*block),
                send_sem=send_sems.at[k], recv_sem=recv_sems.at[k], device_id=to, device_id_type=MESH_IDS)

        first = [copy(0, me, sibling, src=v_ref)]
        first += [copy(1 + j, me, (*chip, c), src=v_ref) for j, chip in enumerate(chips)]
        passed = [copy(4 + j, (*chip, c), sibling) for j, chip in enumerate(chips)]

        def start():
            for cp in first:
                cp.start()

        def middle():
            for j, chip in enumerate(chips):
                copy(1 + j, (*chip, c), me).wait_recv()
                passed[j].start()

        def finish():
            copy(0, sibling, me).wait_recv()
            for j, chip in enumerate(chips):
                copy(4 + j, (*chip, 1 - c), me).wait_recv()
            for cp in first + passed:
                cp.wait_send()

        return start, middle, finish

    return _Exchange(v, jax.ShapeDtypeStruct((8, m, n), v.dtype), 7, phases)


def _with_own(gathered, v):
    dev = 4 * lax.axis_index("x") + 2 * lax.axis_index("y") + lax.axis_index("c")
    return lax.dynamic_update_index_in_dim(gathered, v, dev, 0)


def _direct_exchange(operand, out_shape, n_copies, make_copies):
    def phases(in_ref, out_ref, send_sems, recv_sems):
        copies = make_copies(in_ref, out_ref, send_sems, recv_sems)

        def start():
            for cp in copies:
                cp.start()

        def finish():
            for cp in copies:
                cp.wait()

        return start, _nothing, finish

    return _Exchange(operand, out_shape, n_copies, phases)


def _swap_cores_exchange(blocks):
    _, m, n = blocks.shape

    def make_copies(g_ref, out_ref, send_sems, recv_sems):
        x, y, c = _my_place()
        return [pltpu.make_async_remote_copy(
            src_ref=g_ref.at[2 * j + (1 - c)], dst_ref=out_ref.at[j],
            send_sem=send_sems.at[j], recv_sem=recv_sems.at[j],
            device_id=(x, y, 1 - c), device_id_type=MESH_IDS) for j in range(4)]

    return _direct_exchange(blocks, jax.ShapeDtypeStruct((4, m, n), blocks.dtype), 4, make_copies)


def _scatter_chips_exchange(parts):
    _, m, n = parts.shape
    flips = [(1, 0), (0, 1), (1, 1)]

    def make_copies(p_ref, out_ref, send_sems, recv_sems):
        x, y, c = _my_place()
        copies = []
        for k, (fx, fy) in enumerate(flips):
            tx = 1 - x if fx else x
            ty = 1 - y if fy else y
            copies.append(pltpu.make_async_remote_copy(
                src_ref=p_ref.at[2 * tx + ty], dst_ref=out_ref.at[k],
                send_sem=send_sems.at[k], recv_sem=recv_sems.at[k],
                device_id=(tx, ty, c), device_id_type=MESH_IDS))
        return copies

    return _direct_exchange(parts, jax.ShapeDtypeStruct((3, m, n), parts.dtype), 3, make_copies)


def _swap_one_exchange(v):
    def make_copies(v_ref, out_ref, send_sems, recv_sems):
        x, y, c = _my_place()
        return [pltpu.make_async_remote_copy(src_ref=v_ref, dst_ref=out_ref, send_sem=send_sems.at[0],
                                             recv_sem=recv_sems.at[0], device_id=(x, y, 1 - c),
                                             device_id_type=MESH_IDS)]

    return _direct_exchange(v, jax.ShapeDtypeStruct(v.shape, v.dtype), 1, make_copies)


def _gather8(v, name):
    return _with_own(_run_exchange(_gather_exchange(v), name), v)


def _carried(ex, refs, n_in, n_out, n_scratch):
    ins, ex_in = refs[:n_in], refs[n_in]
    outs, ex_out = refs[n_in + 1:n_in + 1 + n_out], refs[n_in + 1 + n_out]
    at = n_in + 2 + n_out
    return ins, outs + refs[at:at + n_scratch], ex.phases(ex_in, ex_out, *refs[at + n_scratch:])


def _ada_fwd(c_all, w_ada_sh, b_ada_sh):
    nb, cols = c_all.shape[0], w_ada_sh.shape[1]
    tn = 512

    def body(c_ref, w_ref, b_ref, o_ref):
        cv = c_ref[...]
        act = (cv * jax.nn.sigmoid(cv)).astype(BF16)
        o_ref[...] = _dot(act, w_ref[...].astype(BF16)) + b_ref[...]

    return pl.pallas_call(
        body, name="ada_fwd", grid=(cols // tn,),
        out_shape=jax.ShapeDtypeStruct((nb, cols), F32),
        in_specs=[pl.BlockSpec((nb, D_MODEL), lambda j: (0, 0)),
                  pl.BlockSpec((D_MODEL, tn), lambda j: (0, j)),
                  pl.BlockSpec((1, tn), lambda j: (0, j))],
        out_specs=pl.BlockSpec((nb, tn), lambda j: (0, j)),
        compiler_params=_params(("arbitrary",)),
    )(c_all, w_ada_sh, b_ada_sh)


def _ada_bwd(c_all, dmod_sh):
    nb, cols = dmod_sh.shape
    tn = 512

    def body(c_ref, d_ref, o_ref):
        cv = c_ref[...]
        act = (cv * jax.nn.sigmoid(cv)).astype(BF16)
        o_ref[...] = _dot(act, d_ref[...].astype(BF16), TN)

    return pl.pallas_call(
        body, name="ada_bwd", grid=(cols // tn,),
        out_shape=jax.ShapeDtypeStruct((D_MODEL, cols), F32),
        in_specs=[pl.BlockSpec((nb, D_MODEL), lambda j: (0, 0)),
                  pl.BlockSpec((nb, tn), lambda j: (0, j))],
        out_specs=pl.BlockSpec((D_MODEL, tn), lambda j: (0, j)),
        compiler_params=_params(("arbitrary",)),
    )(c_all, dmod_sh)


def _sum_lead(v, name):
    k, m, n = v.shape

    def body(v_ref, o_ref):
        acc = v_ref[0]
        for i in range(1, k):
            acc = acc + v_ref[i]
        o_ref[...] = acc

    return pl.pallas_call(
        body, name=name, out_shape=jax.ShapeDtypeStruct((m, n), F32),
        in_specs=[pl.BlockSpec((k, m, n), lambda: (0, 0, 0))],
        out_specs=pl.BlockSpec((m, n), lambda: (0, 0)),
        compiler_params=_params(),
    )(v)


def _adamw_math(w, g, m, v):
    mn = ADAM_B1 * m + (1.0 - ADAM_B1) * g
    vn = ADAM_B2 * v + (1.0 - ADAM_B2) * (g * g)
    m_hat = mn / (1.0 - ADAM_B1 ** ADAM_STEP)
    v_hat = vn / (1.0 - ADAM_B2 ** ADAM_STEP)
    return -ADAM_LR * (m_hat / (jnp.sqrt(v_hat) + ADAM_EPS) + ADAM_WD * w), mn, vn


def _adamw_small(g_sum, g_b_ada, ws, ms, vs):
    n = len(SMALL)

    def body(gs_ref, gb_ref, *refs):
        outs = refs[3 * n:]
        for p in range(n):
            rows_p = SMALL[p][1] // LANES
            g = gb_ref[...] if SMALL[p][0] == "b_ada" else gs_ref[SMALL_AT[p]:SMALL_AT[p] + rows_p, :]
            d, mn, vn = _adamw_math(refs[p][...], g, refs[n + p][...], refs[2 * n + p][...])
            outs[4 * p][...] = g
            outs[4 * p + 1][...] = d
            outs[4 * p + 2][...] = mn
            outs[4 * p + 3][...] = vn

    shapes = [jax.ShapeDtypeStruct((size // LANES, LANES), F32) for _, size in SMALL for _ in range(4)]
    flat = lambda arrs: [a.reshape(-1, LANES) for a in arrs]
    res = pl.pallas_call(body, name="adamw_small", out_shape=tuple(shapes), compiler_params=_params())(
        g_sum, g_b_ada, *flat(ws), *flat(ms), *flat(vs))
    return [tuple(r.reshape(w.shape) for r in res[4 * p:4 * p + 4]) for p, w in enumerate(ws)]


def _adamw_halves(w, g_mine, g_other, m, v, c_idx, name):
    r, cols = w.shape
    half = r // 2
    tr = half
    while tr * cols * 4 > (2 << 20) and tr % 16 == 0:
        tr //= 2

    def body(c_ref, w_ref, mine_ref, other_ref, m_ref, v_ref, g_ref, d_ref, mo_ref, vo_ref):
        g = jnp.where(pl.program_id(0) == c_ref[0], mine_ref[...], other_ref[...])
        g_ref[0] = g
        d_ref[0], mo_ref[0], vo_ref[0] = _adamw_math(w_ref[0], g, m_ref[0], v_ref[0])

    full = pl.BlockSpec((1, tr, cols), lambda h, i, c: (h, i, 0))
    part = pl.BlockSpec((tr, cols), lambda h, i, c: (i, 0))
    shape = jax.ShapeDtypeStruct((2, half, cols), F32)
    grid_spec = pltpu.PrefetchScalarGridSpec(
        num_scalar_prefetch=1, grid=(2, half // tr),
        in_specs=[full, part, part, full, full], out_specs=(full, full, full, full))
    split = lambda a: a.reshape(2, half, cols)
    res = pl.pallas_call(
        body, name=name, grid_spec=grid_spec, out_shape=(shape, shape, shape, shape),
        compiler_params=_params(("arbitrary", "arbitrary")),
    )(c_idx, split(w), g_mine, g_other, split(m), split(v))
    return tuple(a.reshape(r, cols) for a in res)


def _adamw(w, g, m, v, name, ex=None):
    rows, cols = w.shape
    tr = rows
    while tr * cols * 4 > (2 << 20) and tr % 16 == 0:
        tr //= 2

    def body(w_ref, g_ref, m_ref, v_ref, d_ref, mo_ref, vo_ref):
        d_ref[...], mo_ref[...], vo_ref[...] = _adamw_math(w_ref[...], g_ref[...], m_ref[...], v_ref[...])

    spec = pl.BlockSpec((tr, cols), lambda a, b, i: (i, 0))
    shape = jax.ShapeDtypeStruct((rows, cols), F32)
    return _carrier_call(body, ex, name, (1, 1, rows // tr), [w, g, m, v], [spec] * 4, [shape] * 3, [spec] * 3)


def _add_rows(m, n):
    fits = [d for d in range(16, m + 1, 16) if m % d == 0 and d * n * 4 <= (5 << 19)]
    assert fits, (m, n)
    return max(fits)


def _add_pairs(blocks, recv, c_idx, name):
    _, m, n = blocks.shape
    tr = _add_rows(m, n)

    def body(c_ref, a_ref, b_ref, o_ref, ob_ref):
        s = a_ref[...] + b_ref[...]
        o_ref[...] = s
        ob_ref[...] = s.astype(BF16)

    grid_spec = pltpu.PrefetchScalarGridSpec(
        num_scalar_prefetch=1, grid=(4, m // tr),
        in_specs=[pl.BlockSpec((1, tr, n), lambda j, i, c: (2 * j + c[0], i, 0)),
                  pl.BlockSpec((1, tr, n), lambda j, i, c: (j, i, 0))],
        out_specs=(pl.BlockSpec((1, tr, n), lambda j, i, c: (j, i, 0)),
                   pl.BlockSpec((1, tr, n), lambda j, i, c: (j, i, 0))))
    return pl.pallas_call(
        body, name=name, grid_spec=grid_spec,
        out_shape=(jax.ShapeDtypeStruct((4, m, n), F32), jax.ShapeDtypeStruct((4, m, n), BF16)),
        compiler_params=_params(("arbitrary", "arbitrary")),
    )(c_idx, blocks, recv)


def _add_chips(own, recv, name):
    m, n = own.shape
    tr = _add_rows(m, n)

    def body(a_ref, r_ref, o_ref):
        acc = a_ref[...]
        for k in range(3):
            acc = acc + r_ref[k].astype(F32)
        o_ref[...] = acc

    return pl.pallas_call(
        body, name=name, grid=(m // tr,),
        out_shape=jax.ShapeDtypeStruct((m, n), F32),
        in_specs=[pl.BlockSpec((tr, n), lambda i: (i, 0)), pl.BlockSpec((3, tr, n), lambda i: (0, i, 0))],
        out_specs=pl.BlockSpec((tr, n), lambda i: (i, 0)),
        compiler_params=_params(("arbitrary",)),
    )(own, recv)


def _row_spec(cols):
    return pl.BlockSpec((ROW_TILE, cols), lambda i: (i, 0))


def _mod_spec(tiles_per_seq):
    return pl.BlockSpec((1, 8, D_MODEL), lambda i: (i // tiles_per_seq, 0, 0))


def _table_spec(tiles_per_seq):
    return pl.BlockSpec((ROW_TILE, LANES), lambda i: (i % tiles_per_seq, 0))


def _fwd_in(x, mod, ln_g, ln_b, w_in, q_g, kv_g, w_uq, w_ukv, cos_t, sin_a, sin_b, seq):
    rows = x.shape[0]
    tm = min(2 * ROW_TILE, seq)
    tps = seq // tm

    def body(x_ref, mod_ref, g_ref, b_ref, win_ref, qg_ref, kvg_ref, wuq_ref, wukv_ref, cos_ref, sa_ref, sb_ref,
             x0_ref, h_ref, qkv_ref, lat_ref, qp_ref, kp_ref, vm_ref):
        def chain(rs):
            x0, _, _ = _ln_fwd(x_ref[rs, :], g_ref[...], b_ref[...])
            x0_ref[rs, :] = x0
            h = (x0 * (1.0 + mod_ref[0, 1:2, :]) + mod_ref[0, 0:1, :]).astype(BF16)
            h_ref[rs, :] = h
            yield
            proj = _dot(h, win_ref[...])
            yield
            qkv_ref[rs, :SB_W] = (proj[:, :SB_W] * SB_SCALE).astype(BF16)
            qkv_ref[rs, SB_W:] = proj[:, SB_W:3 * SB_W].astype(BF16)
            lat_ref[rs, :] = proj[:, 3 * SB_W:3 * SB_W + Q_RANK + KV_RANK]
            cq = proj[:, 3 * SB_W:3 * SB_W + Q_RANK]
            ckv = proj[:, 3 * SB_W + Q_RANK:3 * SB_W + Q_RANK + KV_RANK]
            kr = proj[:, D_IN_PAD - LANES:]
            cos, sa, sb = cos_ref[rs, :], sa_ref[rs, :], sb_ref[rs, :]
            cqn = (cq * lax.rsqrt(_mean(cq * cq) + RMS_EPS) * qg_ref[...]).astype(BF16)
            q_all = _dot(cqn, wuq_ref[...])
            ckvn = (ckv * lax.rsqrt(_mean(ckv * ckv) + RMS_EPS) * kvg_ref[...]).astype(BF16)
            kv = _dot(ckvn, wukv_ref[...])
            yield
            for hd in range(HEADS):
                sl = slice(hd * LANES, (hd + 1) * LANES)
                qp_ref[rs, sl] = _rope(q_all[:, sl], cos, sa, sb).astype(BF16)
            kr_rot = _rope(kr, cos, sa, sb)
            for hd in range(HEADS):
                sl = slice(hd * LANES, (hd + 1) * LANES)
                kp_ref[rs, sl] = (kv[:, sl] + kr_rot).astype(BF16)
            vm_ref[rs, :] = kv[:, HEADS * LANES:].astype(BF16)

        half = tm // 2
        _staggered([chain(slice(0, half)), chain(slice(half, tm))])

    row_spec = lambda cols: pl.BlockSpec((tm, cols), lambda i: (i, 0))
    table_spec = pl.BlockSpec((tm, LANES), lambda i: (i % tps, 0))
    outs = [(D_MODEL, F32), (D_MODEL, BF16), (3 * SB_W, BF16), (Q_RANK + KV_RANK, F32),
            (HEADS * LANES, BF16), (HEADS * LANES, BF16), (MLA_W, BF16)]
    return pl.pallas_call(
        body, name="fwd_in", grid=(rows // tm,),
        out_shape=tuple(jax.ShapeDtypeStruct((rows, n), dt) for n, dt in outs),
        in_specs=[row_spec(D_MODEL), pl.BlockSpec((1, 8, D_MODEL), lambda i: (i // tps, 0, 0)),
                  _const_spec((1, D_MODEL)), _const_spec((1, D_MODEL)),
                  _const_spec(w_in.shape), _const_spec((1, Q_RANK)), _const_spec((1, KV_RANK)),
                  _const_spec(w_uq.shape), _const_spec(w_ukv.shape), table_spec, table_spec, table_spec],
        out_specs=tuple(row_spec(n) for n, _ in outs),
        compiler_params=_params(("arbitrary",)),
    )(x, mod, ln_g, ln_b, w_in, q_g, kv_g, w_uq, w_ukv, cos_t, sin_a, sin_b)


HALF = 512
SHARD = 1024


def _mlp_weight_specs():
    return [pl.BlockSpec((8, HALF, SHARD), lambda i: (0, 0, 0), pipeline_mode=pl.Buffered(1)),
            pl.BlockSpec((8, HALF, SHARD), lambda i: (0, 1, 0), pipeline_mode=pl.Buffered(1))]


def _fwd_out(sb_y, mla_y, x0, mod, w_o, ln_g, ln_b, g_mlp, seq):
    rows = x0.shape[0]
    tm = ROW_TILE
    tps = seq // tm

    def body(sb_ref, ml_ref, x0_ref, mod_ref, wo_ref, g_ref, b_ref, wu_ref, wd_ref,
             mix_ref, y1_ref, h2_ref, u_ref, ff_ref, y2_ref):
        mix = _dot(sb_ref[...], wo_ref[:SB_W, :]) + _dot(ml_ref[...].astype(BF16), wo_ref[SB_W:, :])
        mix_ref[...] = mix
        y1 = ALPHA * x0_ref[...] + (1.0 + mod_ref[0, 2:3, :]) * mix
        y1_ref[...] = y1
        x1, _, _ = _ln_fwd(y1, g_ref[...], b_ref[...])
        h2 = (x1 * (1.0 + mod_ref[0, 4:5, :]) + mod_ref[0, 3:4, :]).astype(BF16)
        h2_ref[...] = h2
        h_lo, h_hi = h2[:, :HALF], h2[:, HALF:]
        ff = jnp.zeros((tm, D_MODEL), F32)
        for chip in range(4):
            u = _dot(h_lo, wu_ref[2 * chip]) + _dot(h_hi, wu_ref[2 * chip + 1])
            u_ref[:, chip * SHARD:(chip + 1) * SHARD] = u.astype(BF16)
            act = jnp.square(jnp.maximum(u, 0.0)).astype(BF16)
            ff = ff + _dot(act[:, :HALF], wd_ref[2 * chip]) + _dot(act[:, HALF:], wd_ref[2 * chip + 1])
        ff_ref[...] = ff
        y2_ref[...] = ALPHA * x1 + (1.0 + mod_ref[0, 5:6, :]) * ff

    outs = [(D_MODEL, F32), (D_MODEL, F32), (D_MODEL, BF16), (D_FF, BF16), (D_MODEL, F32), (D_MODEL, F32)]
    return pl.pallas_call(
        body, name="fwd_out", grid=(rows // tm,),
        out_shape=tuple(jax.ShapeDtypeStruct((rows, n), dt) for n, dt in outs),
        in_specs=[_row_spec(SB_W), _row_spec(MLA_W), _row_spec(D_MODEL), _mod_spec(tps), _const_spec(w_o.shape),
                  _const_spec((1, D_MODEL)), _const_spec((1, D_MODEL))] + _mlp_weight_specs(),
        out_specs=tuple(_row_spec(n) for n, _ in outs),
        compiler_params=_params(("arbitrary",)),
    )(sb_y, mla_y, x0, mod, w_o, ln_g, ln_b, g_mlp, g_mlp)


def _staggered(chains):
    live = []
    for chain in chains:
        live.append(chain)
        live = [c for c in live if next(c, StopIteration) is not StopIteration]
    while live:
        live = [c for c in live if next(c, StopIteration) is not StopIteration]


def _acc_spec(rows=8, cols=D_MODEL):
    return pl.BlockSpec((rows, cols), lambda i: (0, 0))


def _bwd_out(y2, tgt, ff, u, y1, mix, mod, ln2_g, ln2_b, ln1_g, ln1_b, g_mlp, w_o, seq):
    rows = y2.shape[0]
    nb = rows // seq
    tm = ROW_TILE
    tps = seq // tm

    def body(y2_ref, t_ref, ff_ref, u_ref, y1_ref, mix_ref, mod_ref, g2_ref, b2_ref, g_ref, b_ref, wu_ref, wd_ref,
             wo_ref, dy1_ref, dmix_ref, do_ref, dff_ref, du_ref, acc_ref, dmod_ref):
        i = pl.program_id(0)

        @pl.when(i == 0)
        def _():
            acc_ref[...] = jnp.zeros_like(acc_ref)

        @pl.when(i % tps == 0)
        def _():
            dmod_ref[...] = jnp.zeros_like(dmod_ref)

        g2 = g2_ref[...]
        x2, xhat2, rstd2 = _ln_fwd(y2_ref[...], g2, b2_ref[...])
        err = x2 - t_ref[...]
        dx2 = err * (1.0 / D_MODEL)
        acc_ref[0:1, :] += _rowsum(dx2 * xhat2)
        acc_ref[1:2, :] += _rowsum(dx2)
        acc_ref[2:3, :] += _rowsum(err * err) * (0.5 / D_MODEL)
        dy2 = _ln_bwd(dx2, xhat2, rstd2, g2)
        dmod_ref[0, 5:6, :] += _rowsum(dy2 * ff_ref[...])
        dff = ((1.0 + mod_ref[0, 5:6, :]) * dy2).astype(BF16)
        dff_ref[...] = dff
        for blk in range(8):
            cols = slice(blk * HALF, (blk + 1) * HALF)
            da = _dot(dff, wd_ref[blk], NT)
            du_ref[:, cols] = (da * (2.0 * jnp.maximum(u_ref[:, cols].astype(F32), 0.0))).astype(BF16)

        g = g_ref[...]
        x1, xhat, rstd = _ln_fwd(y1_ref[...], g, b_ref[...])
        halves = []
        for half in range(2):
            acc = jnp.zeros((tm, HALF), F32)
            for chip in range(4):
                acc = acc + _dot(du_ref[:, chip * SHARD:(chip + 1) * SHARD], wu_ref[2 * chip + half], NT)
            halves.append(acc)
        dh2 = jnp.concatenate(halves, axis=1)
        dmod_ref[0, 3:4, :] += _rowsum(dh2)
        dmod_ref[0, 4:5, :] += _rowsum(dh2 * x1)
        dx1 = ALPHA * dy2 + dh2 * (1.0 + mod_ref[0, 4:5, :])
        acc_ref[3:4, :] += _rowsum(dx1 * xhat)
        acc_ref[4:5, :] += _rowsum(dx1)
        dy1 = _ln_bwd(dx1, xhat, rstd, g)
        dy1_ref[...] = dy1
        dmod_ref[0, 2:3, :] += _rowsum(dy1 * mix_ref[...])
        dmix = ((1.0 + mod_ref[0, 2:3, :]) * dy1).astype(BF16)
        dmix_ref[...] = dmix
        do_ref[...] = _dot(dmix, wo_ref[...], NT)

    outs = [(D_MODEL, F32), (D_MODEL, BF16), (D_MODEL, F32), (D_MODEL, BF16), (D_FF, BF16)]
    return pl.pallas_call(
        body, name="bwd_out", grid=(rows // tm,),
        out_shape=tuple(jax.ShapeDtypeStruct((rows, n), dt) for n, dt in outs)
        + (jax.ShapeDtypeStruct((8, D_MODEL), F32), jax.ShapeDtypeStruct((nb, 8, D_MODEL), F32)),
        in_specs=[_row_spec(D_MODEL), _row_spec(D_MODEL), _row_spec(D_MODEL), _row_spec(D_FF), _row_spec(D_MODEL),
                  _row_spec(D_MODEL), _mod_spec(tps), _const_spec((1, D_MODEL)), _const_spec((1, D_MODEL)),
                  _const_spec((1, D_MODEL)), _const_spec((1, D_MODEL))] + _mlp_weight_specs()
        + [_const_spec(w_o.shape)],
        out_specs=tuple(_row_spec(n) for n, _ in outs) + (_acc_spec(), _mod_spec(tps)),
        compiler_params=_params(("arbitrary",)),
    )(y2, tgt, ff, u, y1, mix, mod, ln2_g, ln2_b, ln1_g, ln1_b, g_mlp, g_mlp, w_o)


def _bwd_in(dqp, dkp, dvm, dq_sb, dk_sb, dv_sb, lat, x, dy1, mod, ln_g, ln_b, w_in, q_g, kv_g, w_uq, w_ukv,
            cos_t, sin_a, sin_b, seq):
    rows = x.shape[0]
    nb = rows // seq
    tm = ROW_TILE
    tps = seq // tm
    n_lat = Q_RANK + KV_RANK

    def body(dqp_ref, dkp_ref, dvm_ref, dqs_ref, dks_ref, dvs_ref, lat_ref, x_ref, dy1_ref, mod_ref,
             g_ref, b_ref, win_ref, qg_ref, kvg_ref, wuq_ref, wukv_ref, cos_ref, sa_ref, sb_ref,
             dx_ref, dproj_ref, dqall_ref, dkv_ref, latn_ref, acc_ref, accl_ref, dmod_ref):
        i = pl.program_id(0)

        @pl.when(i == 0)
        def _():
            acc_ref[...] = jnp.zeros_like(acc_ref)
            accl_ref[...] = jnp.zeros_like(accl_ref)

        @pl.when(i % tps == 0)
        def _():
            dmod_ref[...] = jnp.zeros_like(dmod_ref)

        cos, sa, sb = cos_ref[...], sa_ref[...], sb_ref[...]
        lane = lax.broadcasted_iota(jnp.int32, (tm, LANES), 1)
        for hd in range(HEADS):
            sl = slice(hd * LANES, (hd + 1) * LANES)
            dqall_ref[:, sl] = _rope_t(dqp_ref[:, sl], cos, sa, sb).astype(BF16)
        dcqn = _dot(dqall_ref[...], wuq_ref[...], NT)
        cq = lat_ref[:, :Q_RANK]
        qg = qg_ref[...]
        rq = lax.rsqrt(_mean(cq * cq) + RMS_EPS)
        cqn = cq * rq
        latn_ref[:, :Q_RANK] = (cqn * qg).astype(BF16)
        accl_ref[0:1, :Q_RANK] += _rowsum(dcqn * cqn)
        dqg = dcqn * qg
        dcq = rq * (dqg - cqn * _mean(dqg * cqn))
        dkr = jnp.zeros((tm, LANES), F32)
        for hd in range(HEADS):
            sl = slice(hd * LANES, (hd + 1) * LANES)
            dk = dkp_ref[:, sl]
            dkr = dkr + dk
            dkv_ref[:, sl] = jnp.where(lane < NOPE, dk, 0.0).astype(BF16)
        dkv_ref[:, HEADS * LANES:] = dvm_ref[...].astype(BF16)
        dckvn = _dot(dkv_ref[...], wukv_ref[...], NT)
        ckv = lat_ref[:, Q_RANK:]
        kvg = kvg_ref[...]
        rkv = lax.rsqrt(_mean(ckv * ckv) + RMS_EPS)
        ckvn = ckv * rkv
        latn_ref[:, Q_RANK:] = (ckvn * kvg).astype(BF16)
        accl_ref[1:2, :KV_RANK] += _rowsum(dckvn * ckvn)
        dkg = dckvn * kvg
        dckv = rkv * (dkg - ckvn * _mean(dkg * ckvn))
        dkr = _rope_t(jnp.where(lane >= NOPE, dkr, 0.0), cos, sa, sb)
        dproj_ref[:, :SB_W] = dqs_ref[...]
        dproj_ref[:, SB_W:2 * SB_W] = dks_ref[...].astype(BF16)
        dproj_ref[:, 2 * SB_W:3 * SB_W] = dvs_ref[...].astype(BF16)
        dproj_ref[:, 3 * SB_W:3 * SB_W + Q_RANK] = dcq.astype(BF16)
        dproj_ref[:, 3 * SB_W + Q_RANK:3 * SB_W + n_lat] = dckv.astype(BF16)
        dproj_ref[:, D_IN_PAD - LANES:] = dkr.astype(BF16)
        dh = _dot(dproj_ref[...], win_ref[...], NT)
        g = g_ref[...]
        x0, xhat, rstd = _ln_fwd(x_ref[...], g, b_ref[...])
        dmod_ref[0, 0:1, :] += _rowsum(dh)
        dmod_ref[0, 1:2, :] += _rowsum(dh * x0)
        dx0 = ALPHA * dy1_ref[...] + dh * (1.0 + mod_ref[0, 1:2, :])
        acc_ref[0:1, :] += _rowsum(dx0 * xhat)
        acc_ref[1:2, :] += _rowsum(dx0)
        dx_ref[...] = _ln_bwd(dx0, xhat, rstd, g)

    outs = [(D_MODEL, F32), (D_IN_PAD, BF16), (HEADS * LANES, BF16), (HEADS * LANES + MLA_W, BF16), (n_lat, BF16)]
    return pl.pallas_call(
        body, name="bwd_in", grid=(rows // tm,),
        out_shape=tuple(jax.ShapeDtypeStruct((rows, n), dt) for n, dt in outs)
        + (jax.ShapeDtypeStruct((8, D_MODEL), F32), jax.ShapeDtypeStruct((8, Q_RANK), F32),
           jax.ShapeDtypeStruct((nb, 8, D_MODEL), F32)),
        in_specs=[_row_spec(HEADS * LANES), _row_spec(HEADS * LANES), _row_spec(MLA_W),
                  _row_spec(SB_W), _row_spec(SB_W), _row_spec(SB_W), _row_spec(n_lat),
                  _row_spec(D_MODEL), _row_spec(D_MODEL), _mod_spec(tps),
                  _const_spec((1, D_MODEL)), _const_spec((1, D_MODEL)), _const_spec(w_in.shape),
                  _const_spec((1, Q_RANK)), _const_spec((1, KV_RANK)), _const_spec(w_uq.shape),
                  _const_spec(w_ukv.shape), _table_spec(tps), _table_spec(tps), _table_spec(tps)],
        out_specs=tuple(_row_spec(n) for n, _ in outs) + (_acc_spec(), _acc_spec(8, Q_RANK), _mod_spec(tps)),
        compiler_params=_params(("arbitrary",)),
    )(dqp, dkp, dvm, dq_sb, dk_sb, dv_sb, lat, x, dy1, mod, ln_g, ln_b, w_in, q_g, kv_g, w_uq, w_ukv,
      cos_t, sin_a, sin_b)


def _wgrad(a, b, name, tm=512, tn=1024, tk=2048, ex=None):
    rows, m = a.shape
    n = b.shape[1]
    tm, tn, tk = min(tm, m), min(tn, n), min(tk, rows)
    if m % tm:
        tm = m
    if n % tn:
        tn = n

    def body(a_ref, b_ref, o_ref):
        @pl.when(pl.program_id(2) == 0)
        def _():
            o_ref[...] = jnp.zeros_like(o_ref)

        o_ref[...] += _dot(a_ref[...].astype(BF16), b_ref[...].astype(BF16), TN)

    res = _carrier_call(
        body, ex, name, (m // tm, n // tn, rows // tk), [a, b],
        [pl.BlockSpec((tk, tm), lambda i, j, k: (k, i)), pl.BlockSpec((tk, tn), lambda i, j, k: (k, j))],
        [jax.ShapeDtypeStruct((m, n), F32)], [pl.BlockSpec((tm, tn), lambda i, j, k: (i, j))])
    return res[0] if ex is None else res


def _wgrad_packed(a, b, name, block_of, row_block, split=1, pre=None, into=None, tk=2048):
    rows, m = a.shape
    n = b.shape[1]
    tm = HALF
    part = tm // split
    tk = min(tk, rows)
    shape = jax.ShapeDtypeStruct((8, GROUP_MLP[0], PACK_COLS), F32)

    def body(a_ref, b_ref, *rest):
        o_ref = rest[-1]

        @pl.when(pl.program_id(2) == 0)
        def _():
            o_ref[...] = jnp.zeros_like(o_ref)

        av = a_ref[...]
        if pre == "relu2":
            av = jnp.square(jnp.maximum(av.astype(F32), 0.0))
        prod = _dot(av.astype(BF16), b_ref[...].astype(BF16), TN)
        for s in range(split):
            o_ref[s] += prod[s * part:(s + 1) * part]

    in_specs = [pl.BlockSpec((tk, tm), lambda i, j, k: (k, i)), pl.BlockSpec((tk, SHARD), lambda i, j, k: (k, j))]
    operands = [a, b]
    if into is not None:
        in_specs.append(pl.BlockSpec(memory_space=pl.ANY))
        operands.append(into)
    return pl.pallas_call(
        body, name=name, grid=(m // tm, n // SHARD, rows // tk), out_shape=shape,
        in_specs=in_specs,
        out_specs=pl.BlockSpec((split, part, SHARD), lambda i, j, k: (block_of(i, j), row_block, 0)),
        input_output_aliases={} if into is None else {2: 0},
        compiler_params=_params(("arbitrary", "arbitrary", "arbitrary")),
    )(*operands)


def _pair(pp):
    return slice(pp * LANES, (pp + 1) * LANES)


def _head_mask(lane, hh):
    return jnp.where((lane >= 64) if hh else (lane < 64), 1.0, 0.0).astype(BF16)


def _tri(t, kind):
    s = lax.broadcasted_iota(jnp.int32, (t, t), 0)
    j = lax.broadcasted_iota(jnp.int32, (t, t), 1)
    one = jnp.where(j > s if kind == "later" else j < s, 1.0, 0.0).astype(BF16)
    return jnp.concatenate([one, one], axis=1)


def _split_dot(tri2, v):
    hi = v.astype(BF16)
    lo = (v - hi.astype(F32)).astype(BF16)
    return _dot(tri2, jnp.concatenate([hi, lo], axis=0))


def _sb_logits(z, valid):
    log_keep = -(jnp.maximum(z, 0.0) + jnp.log(1.0 + jnp.exp2(jnp.abs(z) * NEG_LOG2E)))
    log_beta = z + log_keep
    if valid is not None:
        log_keep = jnp.where(valid, log_keep, 0.0)
    return log_keep, log_beta


def _carrier_call(body, ex, name, grid, operands, in_specs, out_shapes, out_specs, scratch=()):
    n_in, n_out = len(operands), len(out_shapes)
    total = grid[0] * grid[1] * grid[2]
    any_spec = pl.BlockSpec(memory_space=pl.ANY)

    def carrier(*refs):
        ins, outs, (start, middle, finish) = _carried(ex, refs, n_in, n_out, len(scratch))
        step = (pl.program_id(0) * grid[1] + pl.program_id(1)) * grid[2] + pl.program_id(2)
        pl.when(step == 0)(start)
        pl.when(step == total // 2)(middle)
        body(*ins, *outs)
        pl.when(step == total - 1)(finish)

    carried = ex is not None
    return pl.pallas_call(
        carrier if carried else body, name=name, grid=grid,
        out_shape=tuple(out_shapes) + ((ex.out_shape,) if carried else ()),
        in_specs=list(in_specs) + ([any_spec] if carried else []),
        out_specs=tuple(out_specs) + ((any_spec,) if carried else ()),
        scratch_shapes=list(scratch) + (ex.scratch if carried else []),
        compiler_params=_params(("arbitrary", "arbitrary", "arbitrary")),
    )(*operands, *([ex.operand] if carried else []))


def _sb_fwd(qkv, seq, ex=None):
    rows = qkv.shape[0]
    nb = rows // seq
    t = min(ATTN_TILE, seq)
    nq = seq // t
    assert nq <= CAR_SLOTS, (seq, t)
    ap = ATTN_PAIRS
    width = ap * LANES
    groups = SB_W // width
    hds = [(pp, hh) for pp in range(ap) for hh in range(2)]

    def body(q_ref, k_ref, v_ref, tri_ref, o_ref, car_ref, acc_ref):
        i = pl.program_id(2)
        lane = lax.broadcasted_iota(jnp.int32, (t, LANES), 1)
        key = lax.broadcasted_iota(jnp.int32, (t, t), 0)
        qry = lax.broadcasted_iota(jnp.int32, (t, t), 1)
        strict = key < qry
        tri = tri_ref[...]
        masks = [_head_mask(lane, hh) for hh in range(2)]
        qms = [q_ref[:, _pair(pp)] * masks[hh] for pp, hh in hds]
        acc_ref[...] = jnp.zeros_like(acc_ref)
        car_ref[...] = jnp.zeros_like(car_ref)

        def step(kb, c_sums, valid):
            start = pl.multiple_of(kb * t, t)
            kss = [k_ref[pl.ds(start, t), _pair(pp)] for pp in range(ap)]
            vss = [v_ref[pl.ds(start, t), _pair(pp)] for pp in range(ap)]
            zs = [_dot(kss[pp], qms[n], NT) for n, (pp, _) in enumerate(hds)]
            logs = [_sb_logits(z, valid) for z in zs]
            sufs = [_split_dot(tri, lg[0]) for lg in logs]
            new_sums = []
            for n, (pp, hh) in enumerate(hds):
                log_keep, log_beta = logs[n]
                w = jnp.exp(log_beta + sufs[n] + c_sums[n])
                if valid is not None:
                    w = jnp.where(valid, w, 0.0)
                acc_ref[pp] += _dot(vss[pp] * masks[hh], w.astype(BF16), TN)
                car_ref[0, pl.ds(n * CAR_SLOTS + kb, 1), :] = c_sums[n]
                new_sums.append(c_sums[n] + sufs[n][0:1, :] + log_keep[0:1, :])
            return tuple(new_sums)

        c_sums = step(i, tuple(jnp.zeros((1, t), F32) for _ in hds), strict)
        lax.fori_loop(0, i, lambda j, cr: step(i - 1 - j, cr, None), c_sums)
        for pp in range(ap):
            o_ref[:, _pair(pp)] = acc_ref[pp].T.astype(BF16)

    qspec = pl.BlockSpec((t, width), lambda b, p, i: (b * nq + i, p))
    car_rows = len(hds) * CAR_SLOTS
    return _carrier_call(
        body, ex, "sb_fwd", (nb, groups, nq),
        [qkv, qkv, qkv, _tri(t, "later")],
        [qspec,
         pl.BlockSpec((seq, width), lambda b, p, i: (b, groups + p)),
         pl.BlockSpec((seq, width), lambda b, p, i: (b, 2 * groups + p)),
         _const_spec((t, 2 * t))],
        [jax.ShapeDtypeStruct((rows, SB_W), BF16), jax.ShapeDtypeStruct((nb * nq, HEADS * CAR_SLOTS, t), F32)],
        [qspec, pl.BlockSpec((1, car_rows, t), lambda b, p, i: (b * nq + i, p, 0))],
        scratch=[pltpu.VMEM((ap, LANES, t), F32)])


def _sb_bwd(qkv, d_out, cars, seq, ex=None):
    rows = qkv.shape[0]
    nb = rows // seq
    t = min(ATTN_TILE, seq)
    nq = seq // t
    ap = ATTN_PAIRS
    width = ap * LANES
    groups = SB_W // width
    hds = [(pp, hh) for pp in range(ap) for hh in range(2)]

    def body(q_ref, k_ref, v_ref, do_ref, car_ref, tri_ref, pre_ref, dq_ref, dk_ref, dv_ref, dq_acc):
        i = pl.program_id(2)

        @pl.when(i == 0)
        def _():
            dk_ref[...] = jnp.zeros_like(dk_ref)
            dv_ref[...] = jnp.zeros_like(dv_ref)

        lane = lax.broadcasted_iota(jnp.int32, (t, LANES), 1)
        key = lax.broadcasted_iota(jnp.int32, (t, t), 0)
        qry = lax.broadcasted_iota(jnp.int32, (t, t), 1)
        strict = key < qry
        tri, pre = tri_ref[...], pre_ref[...]
        masks = [_head_mask(lane, hh) for hh in range(2)]
        qms = [q_ref[:, _pair(pp)] * masks[hh] for pp, hh in hds]
        doms = [do_ref[:, _pair(pp)].astype(BF16) * masks[hh] for pp, hh in hds]
        dq_acc[...] = jnp.zeros_like(dq_acc)

        def step(kb, g_pres, valid):
            start = pl.multiple_of(kb * t, t)
            kss = [k_ref[pl.ds(start, t), _pair(pp)] for pp in range(ap)]
            vss = [v_ref[pl.ds(start, t), _pair(pp)] for pp in range(ap)]
            zs = [_dot(kss[pp], qms[n], NT) for n, (pp, _) in enumerate(hds)]
            dws = [_dot(vss[pp], doms[n], NT) for n, (pp, _) in enumerate(hds)]
            logs = [_sb_logits(z, valid) for z in zs]
            sufs = [_split_dot(tri, lg[0]) for lg in logs]
            ws, gs = [], []
            for n in range(len(hds)):
                c_sum = car_ref[0, pl.ds(n * CAR_SLOTS + kb, 1), :]
                w = jnp.exp(logs[n][1] + sufs[n] + c_sum)
                if valid is not None:
                    w = jnp.where(valid, w, 0.0)
                ws.append(w)
                gs.append(dws[n] * w)
            pres = [_split_dot(pre, gs[n]) for n in range(len(hds))]
            befores = [g_pres[n] + pres[n] for n in range(len(hds))]
            for pp in range(ap):
                a, b = 2 * pp, 2 * pp + 1
                dv_ref[pl.ds(start, t), _pair(pp)] += _dot(ws[a].astype(BF16), doms[a]) + _dot(ws[b].astype(BF16), doms[b])
            dzbs = []
            for n in range(len(hds)):
                beta = jnp.exp(logs[n][1])
                dz = gs[n] * (1.0 - beta) - beta * befores[n]
                if valid is not None:
                    dz = jnp.where(valid, dz, 0.0)
                dzbs.append(dz.astype(BF16))
            for pp in range(ap):
                a, b = 2 * pp, 2 * pp + 1
                dq_acc[pp] += _dot(dzbs[a], kss[pp] * masks[0], TN) + _dot(dzbs[b], kss[pp] * masks[1], TN)
                dk_ref[pl.ds(start, t), _pair(pp)] += _dot(dzbs[a], qms[a]) + _dot(dzbs[b], qms[b])
            return tuple(g_pres[n] + pres[n][t - 1:t, :] + gs[n][t - 1:t, :] for n in range(len(hds)))

        g_pres = lax.fori_loop(0, i, lambda kb, cr: step(kb, cr, None), tuple(jnp.zeros((1, t), F32) for _ in hds))
        step(i, g_pres, strict)
        for pp in range(ap):
            dq_ref[:, _pair(pp)] = (dq_acc[pp] * SB_SCALE).astype(BF16)

    qspec = pl.BlockSpec((t, width), lambda b, p, i: (b * nq + i, p))
    kspec_out = pl.BlockSpec((seq, width), lambda b, p, i: (b, p))
    car_rows = len(hds) * CAR_SLOTS
    return _carrier_call(
        body, ex, "sb_bwd", (nb, groups, nq),
        [qkv, qkv, qkv, d_out, cars, _tri(t, "later"), _tri(t, "earlier")],
        [qspec,
         pl.BlockSpec((seq, width), lambda b, p, i: (b, groups + p)),
         pl.BlockSpec((seq, width), lambda b, p, i: (b, 2 * groups + p)),
         qspec, pl.BlockSpec((1, car_rows, t), lambda b, p, i: (b * nq + i, p, 0)),
         _const_spec((t, 2 * t)), _const_spec((t, 2 * t))],
        [jax.ShapeDtypeStruct((rows, SB_W), BF16), jax.ShapeDtypeStruct((rows, SB_W), F32),
         jax.ShapeDtypeStruct((rows, SB_W), F32)],
        [qspec, kspec_out, kspec_out],
        scratch=[pltpu.VMEM((ap, t, LANES), F32)])


def _mla_scores(ks, qh, allowed):
    s = _dot(ks, qh, NT) * (MLA_SCALE * -NEG_LOG2E)
    if allowed is not None:
        s = jnp.where(allowed, s, jnp.finfo(F32).min)
    return s


def _mla_fwd(qp, kp, vm, seq, ex=None, chunk=64):
    rows = qp.shape[0]
    nb = rows // seq
    t = min(ATTN_TILE, seq)
    nq = seq // t
    shift = int(math.log2(chunk))
    ap = ATTN_PAIRS
    width = ap * LANES
    groups = MLA_W // width
    hds = [(pp, hh) for pp in range(ap) for hh in range(2)]

    def body(q_ref, k_ref, v_ref, o_ref, lse_ref, acc_ref):
        i = pl.program_id(2)
        lane = lax.broadcasted_iota(jnp.int32, (t, LANES), 1)
        key = lax.broadcasted_iota(jnp.int32, (t, t), 0)
        qry = lax.broadcasted_iota(jnp.int32, (t, t), 1)
        allowed_diag = jnp.right_shift(key, shift) <= jnp.right_shift(qry, shift)
        masks = [_head_mask(lane, hh) for hh in range(2)]
        qhs = [q_ref[:, _pair(n)] for n in range(len(hds))]
        acc_ref[...] = jnp.zeros_like(acc_ref)

        def step(kb, carry, allowed):
            start = pl.multiple_of(kb * t, t)
            vss = [v_ref[pl.ds(start, t), _pair(pp)] for pp in range(ap)]
            scores = [_mla_scores(k_ref[pl.ds(start, t), _pair(n)], qhs[n], allowed) for n in range(len(hds))]
            new = []
            for n, (pp, hh) in enumerate(hds):
                m_run, l_run = carry[n]
                s = scores[n]
                m_new = jnp.maximum(m_run, jnp.max(s, axis=0, keepdims=True))
                p = jnp.exp2(s - m_new)
                scale = jnp.exp2(m_run - m_new)
                l_run = scale * l_run + jnp.sum(p, axis=0, keepdims=True)
                acc_ref[n] = scale * acc_ref[n] + _dot(vss[pp] * masks[hh], p.astype(BF16), TN)
                new.append((m_new, l_run))
            return tuple(new)

        init = (jnp.full((1, t), jnp.finfo(F32).min, F32), jnp.zeros((1, t), F32))
        carry = step(i, tuple(init for _ in hds), allowed_diag)
        carry = lax.fori_loop(0, i, lambda kb, cr: step(kb, cr, None), carry)
        lse_rows = []
        for pp in range(ap):
            out_t = jnp.zeros((LANES, t), F32)
            for hh in range(2):
                m_run, l_run = carry[2 * pp + hh]
                out_t = out_t + acc_ref[2 * pp + hh] / l_run
                lse_rows.append(m_run + jnp.log(l_run) * -NEG_LOG2E)
            o_ref[:, _pair(pp)] = out_t.T
        lse_t = jnp.concatenate(lse_rows + [jnp.zeros((LANES - len(hds), t), F32)], axis=0)
        lse_ref[...] = jnp.zeros_like(lse_ref)
        lse_ref[:, _pair(0)] = lse_t.T

    ospec = pl.BlockSpec((t, width), lambda b, p, i: (b * nq + i, p))
    return _carrier_call(
        body, ex, "mla_fwd", (nb, groups, nq), [qp, kp, vm],
        [pl.BlockSpec((t, 2 * width), lambda b, p, i: (b * nq + i, p)),
         pl.BlockSpec((seq, 2 * width), lambda b, p, i: (b, p)),
         pl.BlockSpec((seq, width), lambda b, p, i: (b, p))],
        [jax.ShapeDtypeStruct((rows, MLA_W), F32), jax.ShapeDtypeStruct((rows, MLA_W), F32)],
        [ospec, ospec], scratch=[pltpu.VMEM((len(hds), LANES, t), F32)])


def _mla_bwd(qp, kp, vm, d_out, out, lse, seq, ex=None, chunk=64):
    rows = qp.shape[0]
    nb = rows // seq
    t = min(ATTN_TILE, seq)
    nq = seq // t
    shift = int(math.log2(chunk))
    ap = ATTN_PAIRS
    width = ap * LANES
    groups = MLA_W // width
    hds = [(pp, hh) for pp in range(ap) for hh in range(2)]
    nh = len(hds)

    def body(q_ref, k_ref, v_ref, do_ref, o_ref, lse_ref, dq_ref, dk_ref, dv_ref):
        i = pl.program_id(2)

        @pl.when(i == 0)
        def _():
            dk_ref[...] = jnp.zeros_like(dk_ref)
            dv_ref[...] = jnp.zeros_like(dv_ref)

        lane = lax.broadcasted_iota(jnp.int32, (t, LANES), 1)
        key = lax.broadcasted_iota(jnp.int32, (t, t), 0)
        qry = lax.broadcasted_iota(jnp.int32, (t, t), 1)
        allowed_diag = jnp.right_shift(key, shift) <= jnp.right_shift(qry, shift)
        qhs = [q_ref[:, _pair(n)] for n in range(nh)]
        lse_t = lse_ref[:, _pair(0)].T
        doms, deltas, lse_hs = [], [], []
        for pp in range(ap):
            do = do_ref[:, _pair(pp)]
            d_o_t = (do * o_ref[:, _pair(pp)]).T
            for hh in range(2):
                doms.append(do.astype(BF16) * _head_mask(lane, hh))
                deltas.append(jnp.sum(d_o_t[hh * 64:(hh + 1) * 64], axis=0, keepdims=True))
                lse_hs.append(lse_t[2 * pp + hh:2 * pp + hh + 1])

        dq_ref[...] = jnp.zeros_like(dq_ref)

        def step(kb, allowed):
            start = pl.multiple_of(kb * t, t)
            vss = [v_ref[pl.ds(start, t), _pair(pp)] for pp in range(ap)]
            kss = [k_ref[pl.ds(start, t), _pair(n)] for n in range(nh)]
            scores = [_mla_scores(kss[n], qhs[n], allowed) for n in range(nh)]
            dps = [_dot(vss[pp], doms[n], NT) for n, (pp, _) in enumerate(hds)]
            ps = [jnp.exp2(scores[n] - lse_hs[n]) for n in range(nh)]
            dss = [(ps[n] * (dps[n] - deltas[n]) * MLA_SCALE).astype(BF16) for n in range(nh)]
            for pp in range(ap):
                a, b = 2 * pp, 2 * pp + 1
                dv_ref[pl.ds(start, t), _pair(pp)] += _dot(ps[a].astype(BF16), doms[a]) + _dot(ps[b].astype(BF16), doms[b])
            for n in range(nh):
                dk_ref[pl.ds(start, t), _pair(n)] += _dot(dss[n], qhs[n])
                dq_ref[:, _pair(n)] += _dot(dss[n], kss[n], TN)

        def off_diagonal(kb, nothing):
            step(kb, None)
            return nothing

        lax.fori_loop(0, i, off_diagonal, 0)
        step(i, allowed_diag)

    ospec = pl.BlockSpec((t, width), lambda b, p, i: (b * nq + i, p))
    return _carrier_call(
        body, ex, "mla_bwd", (nb, groups, nq), [qp, kp, vm, d_out, out, lse],
        [pl.BlockSpec((t, 2 * width), lambda b, p, i: (b * nq + i, p)),
         pl.BlockSpec((seq, 2 * width), lambda b, p, i: (b, p)),
         pl.BlockSpec((seq, width), lambda b, p, i: (b, p)),
         pl.BlockSpec((t, width), lambda b, p, i: (b * nq + i, groups + p)),
         ospec, ospec],
        [jax.ShapeDtypeStruct((rows, HEADS * LANES), F32), jax.ShapeDtypeStruct((rows, HEADS * LANES), F32),
         jax.ShapeDtypeStruct((rows, MLA_W), F32)],
        [pl.BlockSpec((t, 2 * width), lambda b, p, i: (b * nq + i, p)),
         pl.BlockSpec((seq, 2 * width), lambda b, p, i: (b, p)),
         pl.BlockSpec((seq, width), lambda b, p, i: (b, p))])


PACK_COLS = 1024
PACK_ALIGN = 16
GROUP_IN = (384, ((1024, 552, 1), (384, 192, 1), (256, 256, 1)))
GROUP_MLP = (1152, ((1024, 1024, 1), (1024, 1024, 0), (256, 1024, 0)))


def _pack_rows(r, c):
    return (r // 2) * c // PACK_COLS


def _slot_rows(r, c):
    return -(-_pack_rows(r, c) // PACK_ALIGN) * PACK_ALIGN


def _join_slots(parts, group):
    total, weights = group
    padded = [jnp.pad(p, ((0, 0), (0, _slot_rows(r, c) - p.shape[1]), (0, 0))) for p, (r, c, _) in zip(parts, weights)]
    used = sum(_slot_rows(r, c) for r, c, _ in weights)
    if total > used:
        padded.append(jnp.zeros((parts[0].shape[0], total - used, PACK_COLS), parts[0].dtype))
    return jnp.concatenate(padded, axis=1)


def _split_slots(packed, group):
    out, at = [], 0
    for r, c, _ in group[1]:
        out.append(packed[:, at:at + _pack_rows(r, c), :])
        at += _slot_rows(r, c)
    return out


def _pack_halves(shards, group):
    return _join_slots([s.reshape(2, _pack_rows(r, c), PACK_COLS) for s, (r, c, _) in zip(shards, group[1])], group)


def _unpack_half(packed, group):
    return [p.reshape(r // 2, c) for p, (r, c, _) in zip(_split_slots(packed[None], group), group[1])]


def _unpack_full(gathered, group):
    out = []
    for p, (r, c, axis) in zip(_split_slots(gathered, group), group[1]):
        shards = p.reshape(4, r, c)
        out.append(shards.reshape(4 * r, c) if axis == 0 else jnp.moveaxis(shards, 0, 1).reshape(r, 4 * c))
    return out


def _pack_full(grads, group):
    parts = []
    for gr, (r, c, axis) in zip(grads, group[1]):
        shards = gr.reshape(4, r, c) if axis == 0 else jnp.moveaxis(gr.reshape(r, 4, c), 1, 0)
        parts.append(shards.reshape(8, _pack_rows(r, c), PACK_COLS))
    return _join_slots(parts, group)


def _pad_w_in(w_in):
    z = jnp.zeros((D_MODEL, 1), w_in.dtype)
    return jnp.concatenate([w_in[:, :2176], jnp.tile(z, (1, 64)), w_in[:, 2176:], jnp.tile(z, (1, 32))], axis=1)


def _unpad_w_in(g):
    return jnp.concatenate([g[:, :2176], g[:, 2240:2272]], axis=1)


def _pad_heads(w, used):
    k = w.shape[0]
    w3 = w.reshape(k, HEADS, used)
    return jnp.pad(w3, ((0, 0), (0, 0), (0, LANES - used))).reshape(k, HEADS * LANES)


def _unpad_heads(g, used):
    k = g.shape[0]
    return g.reshape(k, HEADS, LANES)[:, :, :used].reshape(k, HEADS * used)


def _rope_tables(seq):
    inv_freq = 1.0 / (ROPE_BASE ** (jnp.arange(0, ROPE, 2, dtype=F32) / ROPE))
    ang = jnp.arange(seq, dtype=F32)[:, None] * inv_freq[None, :]
    cos, sin = jnp.cos(ang), jnp.sin(ang)
    one, zero = jnp.ones((seq, NOPE), F32), jnp.zeros((seq, NOPE), F32)
    z16, z32 = jnp.zeros((seq, 16), F32), jnp.zeros((seq, 32), F32)
    cos_t = jnp.concatenate([one, cos, cos, jnp.ones((seq, 32), F32)], axis=1)
    sin_a = jnp.concatenate([zero, -sin, z16, z32], axis=1)
    sin_b = jnp.concatenate([zero, z16, sin, z32], axis=1)
    return cos_t, sin_a, sin_b


SMALL = (("ln_in_g", 1024), ("ln_in_b", 1024), ("b_ada", 6144), ("q_norm_g", 384), ("kv_norm_g", 256),
         ("ln1_g", 1024), ("ln1_b", 1024), ("ln2_g", 1024), ("ln2_b", 1024))
SUBLANES = 8
SMALL_SLOTS = [-(-n // LANES // SUBLANES) * SUBLANES for _, n in SMALL]
SMALL_AT = [sum(SMALL_SLOTS[:p]) for p in range(len(SMALL))]
SMALL_ROWS = sum(SMALL_SLOTS)


def _pack_small(vals):
    parts = []
    for v, slot in zip(vals, SMALL_SLOTS):
        rows = v.reshape(-1, LANES)
        parts.append(jnp.pad(rows, ((0, slot - rows.shape[0]), (0, 0))))
    return jnp.concatenate(parts, axis=0)


def kernel(x, c, ln_in_g, ln_in_b, w_ada, b_ada, w_in, q_norm_g, kv_norm_g, w_uq, w_ukv, w_o, ln1_g, ln1_b, w_up, w_down, ln2_g, ln2_b, loss_target, m_ln_in_g, m_ln_in_b, m_w_ada, m_b_ada, m_w_in, m_q_norm_g, m_kv_norm_g, m_w_uq, m_w_ukv, m_w_o, m_ln1_g, m_ln1_b, m_w_up, m_w_down, m_ln2_g, m_ln2_b, v_ln_in_g, v_ln_in_b, v_w_ada, v_b_ada, v_w_in, v_q_norm_g, v_kv_norm_g, v_w_uq, v_w_ukv, v_w_o, v_ln1_g, v_ln1_b, v_w_up, v_w_down, v_ln2_g, v_ln2_b):
    nb, seq, _ = x.shape
    rows = nb * seq
    ix, iy, ic = lax.axis_index("x"), lax.axis_index("y"), lax.axis_index("c")
    chip = 2 * ix + iy
    dev = 2 * chip + ic

    def my_half(shards, group):
        packed = _pack_halves([s.astype(BF16) for s in shards], group)
        return lax.dynamic_index_in_dim(packed, ic, 0, keepdims=False)

    f_in, f_uq, f_ukv = _unpack_full(_gather8(my_half([w_in[0], w_uq[0], w_ukv[0]], GROUP_IN), "gather_w_in"),
                                     GROUP_IN)
    half_mlp = my_half([w_up[0], w_down[0], w_o[0]], GROUP_MLP)
    late_weights = _gather_exchange(half_mlp)
    w_in_p = _pad_w_in(f_in)
    uq3 = f_uq.reshape(Q_RANK, HEADS, NOPE + ROPE)
    w_uq_p = jnp.pad(uq3, ((0, 0), (0, 0), (0, LANES - NOPE - ROPE))).reshape(Q_RANK, HEADS * LANES)
    w_ukv_p = jnp.concatenate([_pad_heads(f_ukv[:, :HEADS * NOPE], NOPE), f_ukv[:, HEADS * NOPE:]], axis=1)

    n_all = 8 * nb
    c_all = _gather8(c.reshape(-1, LANES), "gather_c").reshape(n_all, D_MODEL)
    ada_cols = w_ada.shape[2]
    b_sh = lax.dynamic_slice_in_dim(b_ada, chip * ada_cols, ada_cols, axis=1)
    mod_sh = _ada_fwd(c_all, w_ada[0], b_sh)
    mod_g = _gather8(mod_sh, "gather_mod")[0::2]
    mod_all = jnp.moveaxis(mod_g, 0, 1).reshape(n_all, N_MOD * D_MODEL)
    mod_mine = lax.dynamic_slice_in_dim(mod_all, dev * nb, nb, axis=0).reshape(nb, N_MOD, D_MODEL)
    mod = jnp.pad(mod_mine, ((0, 0), (0, 8 - N_MOD), (0, 0)))

    cos_t, sin_a, sin_b = _rope_tables(seq)
    row2 = lambda v: v.reshape(1, -1)

    x2d = x.reshape(rows, D_MODEL)
    x0, h, qkv, lat, qp, kp, vm = _fwd_in(x2d, mod, row2(ln_in_g), row2(ln_in_b), w_in_p, q_norm_g, kv_norm_g,
                                          w_uq_p, w_ukv_p, cos_t, sin_a, sin_b, seq)
    sb_y, cars, g_mlp = _sb_fwd(qkv, seq, late_weights)
    g_mlp = _with_own(g_mlp, half_mlp)
    f_o = _split_slots(g_mlp, GROUP_MLP)[2].reshape(D_MODEL, D_MODEL)
    mla_y, lse = _mla_fwd(qp, kp, vm, seq)
    mix, y1, h2, u, ff, y2 = _fwd_out(sb_y, mla_y, x0, mod, f_o, ln1_g, ln1_b, g_mlp, seq)

    dy1, dmix, d_attn, dff, du, acc_out, dmod_a = _bwd_out(
        y2, loss_target.reshape(rows, D_MODEL), ff, u, y1, mix, mod, ln2_g, ln2_b, ln1_g, ln1_b, g_mlp, f_o, seq)
    c_idx = ic.reshape(1).astype(jnp.int32)
    blocks_mlp = _wgrad_packed(h2, du, "wgrad_up", lambda i, j: 2 * j + i, 0)
    blocks_mlp = _wgrad_packed(u, dff, "wgrad_down", lambda i, j: i, 1, pre="relu2", into=blocks_mlp)
    blocks_mlp = _wgrad_packed(sb_y, dmix, "wgrad_o_sb", lambda i, j: 0, 8, split=4, into=blocks_mlp)
    blocks_mlp = _wgrad_packed(mla_y, dmix, "wgrad_o_mla", lambda i, j: 1, 8, split=4, into=blocks_mlp)
    dq_sb, dk_sb, dv_sb, sibling_mlp = _sb_bwd(qkv, d_attn, cars, seq, _swap_cores_exchange(blocks_mlp))
    part_mlp, part_mlp_bf = _add_pairs(blocks_mlp, sibling_mlp, c_idx, "grad_add_cores_mlp")
    dqp, dkp, dvm, chips_mlp = _mla_bwd(qp, kp, vm, d_attn, mla_y, lse, seq, _scatter_chips_exchange(part_mlp_bf))
    grad_x, dproj, dqall, dkv, latn, acc0, acc_lat, dmod_c = _bwd_in(
        dqp, dkp, dvm, dq_sb, dk_sb, dv_sb, lat, x2d, dy1, mod, row2(ln_in_g), row2(ln_in_b), w_in_p,
        q_norm_g, kv_norm_g, w_uq_p, w_ukv_p, cos_t, sin_a, sin_b, seq)

    dmod = (dmod_a + dmod_c)[:, :N_MOD, :]
    small_part = _pack_small([acc0[0], acc0[1], jnp.zeros((N_MOD * D_MODEL,), F32), acc_lat[0, :Q_RANK],
                              acc_lat[1, :KV_RANK], acc_out[3], acc_out[4], acc_out[0], acc_out[1]])
    n_sum = SMALL_ROWS + D_MODEL // LANES
    payload = jnp.concatenate([small_part, acc_out[2].reshape(-1, LANES), dmod.reshape(-1, LANES)], axis=0)
    g_in_p, gathered = _wgrad(h, dproj, "wgrad_in", tn=768, ex=_gather_exchange(payload))
    gathered = _with_own(gathered, payload)
    g_in = _unpad_w_in(g_in_p)
    g_uq = _unpad_heads(_wgrad(latn[:, :Q_RANK], dqall, "wgrad_uq"), NOPE + ROPE)
    g_ukv_p = _wgrad(latn[:, Q_RANK:], dkv, "wgrad_ukv", tn=512)
    g_ukv = jnp.concatenate([_unpad_heads(g_ukv_p[:, :HEADS * LANES], NOPE), g_ukv_p[:, HEADS * LANES:]], axis=1)
    blocks_in = _pack_full([g_in, g_uq, g_ukv], GROUP_IN)
    sibling_in = _run_exchange(_swap_cores_exchange(blocks_in), "grads_in_to_sibling")
    part_in, part_in_bf = _add_pairs(blocks_in, sibling_in, c_idx, "grad_add_cores_in")

    small_sum = _sum_lead(gathered[:, :n_sum, :], "sum_small")
    loss = jnp.sum(small_sum[SMALL_ROWS:])
    dmod_all = gathered[:, n_sum:, :].reshape(n_all, N_MOD * D_MODEL)
    g_b_ada = _sum_lead(dmod_all.reshape(n_all, N_MOD * D_MODEL // LANES, LANES), "sum_b_ada")
    dmod_sh = lax.dynamic_slice_in_dim(dmod_all, chip * ada_cols, ada_cols, axis=1)
    g_w_ada = _ada_bwd(c_all, dmod_sh)
    d_ada, m_ada, v_ada, chips_in = _adamw(w_ada[0], g_w_ada, m_w_ada[0], v_w_ada[0], "adamw_w_ada",
                                          _scatter_chips_exchange(part_in_bf))

    def own(part):
        return lax.dynamic_index_in_dim(part, chip, 0, keepdims=False)

    half = jnp.concatenate([_add_chips(own(part_in), chips_in, "grad_add_chips_in"),
                            _add_chips(own(part_mlp), chips_mlp, "grad_add_chips_mlp")], axis=0)
    other = _run_exchange(_swap_one_exchange(half), "grads_halves")
    mine = _unpack_half(half[:GROUP_IN[0]], GROUP_IN) + _unpack_half(half[GROUP_IN[0]:], GROUP_MLP)
    theirs = _unpack_half(other[:GROUP_IN[0]], GROUP_IN) + _unpack_half(other[GROUP_IN[0]:], GROUP_MLP)

    res = {"w_ada": (g_w_ada[None], d_ada[None], m_ada[None], v_ada[None])}
    sharded = {"w_in": (w_in, m_w_in, v_w_in), "w_uq": (w_uq, m_w_uq, v_w_uq), "w_ukv": (w_ukv, m_w_ukv, v_w_ukv),
               "w_up": (w_up, m_w_up, v_w_up), "w_down": (w_down, m_w_down, v_w_down), "w_o": (w_o, m_w_o, v_w_o)}
    for (name, (w, m, v)), g_mine, g_other in zip(sharded.items(), mine, theirs):
        quad = _adamw_halves(w[0], g_mine, g_other, m[0], v[0], c_idx, "adamw_" + name)
        res[name] = tuple(a[None] for a in quad)
    small_w = [ln_in_g, ln_in_b, b_ada, q_norm_g, kv_norm_g, ln1_g, ln1_b, ln2_g, ln2_b]
    small_m = [m_ln_in_g, m_ln_in_b, m_b_ada, m_q_norm_g, m_kv_norm_g, m_ln1_g, m_ln1_b, m_ln2_g, m_ln2_b]
    small_v = [v_ln_in_g, v_ln_in_b, v_b_ada, v_q_norm_g, v_kv_norm_g, v_ln1_g, v_ln1_b, v_ln2_g, v_ln2_b]
    for (name, _), quad in zip(SMALL, _adamw_small(small_sum, g_b_ada, small_w, small_m, small_v)):
        res[name] = quad

    order = ["ln_in_g", "ln_in_b", "w_ada", "b_ada", "w_in", "q_norm_g", "kv_norm_g", "w_uq", "w_ukv", "w_o",
             "ln1_g", "ln1_b", "w_up", "w_down", "ln2_g", "ln2_b"]
    outs = [loss, grad_x.reshape(nb, seq, D_MODEL)]
    for k in range(4):
        outs += [res[name][k] for name in order]
    return tuple(outs)
```

```python
import math

import jax
import jax.numpy as jnp
from jax import lax
from jax.experimental import pallas as pl
from jax.experimental.pallas import tpu as pltpu

F32 = jnp.float32
BF16 = jnp.bfloat16
MESH_IDS = pl.DeviceIdType.MESH

D_MODEL = 1024
HEADS = 8
SB_W = 512
MLA_W = 512
NOPE = 64
ROPE = 32
Q_RANK = 384
KV_RANK = 256
D_IN_PAD = 2304
D_FF = 4096
N_MOD = 6
LN_EPS = 1e-5
RMS_EPS = 1e-6
ALPHA = 2.0 ** 0.25
ROPE_BASE = 10000.0
SB_SCALE = 64 ** -0.5
NEG_LOG2E = -math.log2(math.e)
MLA_SCALE = 96 ** -0.5
ADAM_LR = 0.001
ADAM_B1 = 0.9
ADAM_B2 = 0.999
ADAM_EPS = 1e-08
ADAM_WD = 0.01
ADAM_STEP = 10

LANES = 128
ROW_TILE = 256
ATTN_TILE = 256
CAR_SLOTS = 8
ATTN_PAIRS = 4
VMEM_LIMIT = 56 << 20

NT = (((1,), (1,)), ((), ()))
TN = (((0,), (0,)), ((), ()))


def _params(sem=None):
    return pltpu.CompilerParams(vmem_limit_bytes=VMEM_LIMIT, dimension_semantics=sem)


def _const_spec(shape):
    zeros = (0,) * len(shape)
    return pl.BlockSpec(shape, lambda *_: zeros, pipeline_mode=pl.Buffered(1))


def _dot(a, b, dims=None):
    if dims is None:
        return jnp.dot(a, b, preferred_element_type=F32)
    return lax.dot_general(a, b, dims, preferred_element_type=F32)


def _mean(v):
    return jnp.mean(v, axis=-1, keepdims=True)


def _rowsum(v):
    return jnp.sum(v, axis=0, keepdims=True)


def _ln_fwd(y, g, b):
    mu = _mean(y)
    yc = y - mu
    rstd = lax.rsqrt(_mean(yc * yc) + LN_EPS)
    xhat = yc * rstd
    return xhat * g + b, xhat, rstd


def _ln_bwd(dx, xhat, rstd, g):
    dxh = dx * g
    return rstd * (dxh - _mean(dxh) - xhat * _mean(dxh * xhat))


def _rope(v, cos, sin_a, sin_b):
    return v * cos + pltpu.roll(v, 112, 1) * sin_a + pltpu.roll(v, 16, 1) * sin_b


def _rope_t(dv, cos, sin_a, sin_b):
    return dv * cos + pltpu.roll(dv * sin_a, 16, 1) + pltpu.roll(dv * sin_b, 112, 1)


def _my_place():
    return lax.axis_index("x"), lax.axis_index("y"), lax.axis_index("c")


class _Exchange:
    def __init__(self, operand, out_shape, n_copies, phases):
        self.operand = operand
        self.out_shape = out_shape
        self.phases = phases
        self.scratch = [pltpu.SemaphoreType.DMA((n_copies,)), pltpu.SemaphoreType.DMA((n_copies,))]


def _run_exchange(ex, name):
    def body(in_ref, out_ref, send_sems, recv_sems):
        for phase in ex.phases(in_ref, out_ref, send_sems, recv_sems):
            phase()

    return pl.pallas_call(
        body, name=name, out_shape=ex.out_shape,
        in_specs=[pl.BlockSpec(memory_space=pl.ANY)], out_specs=pl.BlockSpec(memory_space=pl.ANY),
        scratch_shapes=ex.scratch,
    )(ex.operand)


def _nothing():
    pass


def _gather_exchange(v):
    m, n = v.shape

    def phases(v_ref, out_ref, send_sems, recv_sems):
        x, y, c = _my_place()
        me, sibling = (x, y, c), (x, y, 1 - c)
        chips = [(1 - x, y), (x, 1 - y), (1 - x, 1 - y)]

        def rows(px, py, pc):
            return out_ref.at[4 * px + 2 * py + pc]

        def copy(k, block, to, src=None):
            return pltpu.make_async_remote_copy(
                src_ref=rows(*block) if src is None else src, dst_ref=rows(*block),
                send_sem=send_sems.at[k], recv_sem=recv_sems.at[k], device_id=to, device_id_type=MESH_IDS)

        first = [copy(0, me, sibling, src=v_ref)]
        first += [copy(1 + j, me, (*chip, c), src=v_ref) for j, chip in enumerate(chips)]
        passed = [copy(4 + j, (*chip, c), sibling) for j, chip in enumerate(chips)]

        def start():
            for cp in first:
                cp.start()

        def middle():
            for j, chip in enumerate(chips):
                copy(1 + j, (*chip, c), me).wait_recv()
                passed[j].start()

        def finish():
            copy(0, sibling, me).wait_recv()
            for j, chip in enumerate(chips):
                copy(4 + j, (*chip, 1 - c), me).wait_recv()
            for cp in first + passed:
                cp.wait_send()

        return start, middle, finish

    return _Exchange(v, jax.ShapeDtypeStruct((8, m, n), v.dtype), 7, phases)


def _with_own(gathered, v):
    dev = 4 * lax.axis_index("x") + 2 * lax.axis_index("y") + lax.axis_index("c")
    return lax.dynamic_update_index_in_dim(gathered, v, dev, 0)


def _direct_exchange(operand, out_shape, n_copies, make_copies):
    def phases(in_ref, out_ref, send_sems, recv_sems):
        copies = make_copies(in_ref, out_ref, send_sems, recv_sems)

        def start():
            for cp in copies:
                cp.start()

        def finish():
            for cp in copies:
                cp.wait()

        return start, _nothing, finish

    return _Exchange(operand, out_shape, n_copies, phases)


def _swap_cores_exchange(blocks):
    _, m, n = blocks.shape

    def make_copies(g_ref, out_ref, send_sems, recv_sems):
        x, y, c = _my_place()
        return [pltpu.make_async_remote_copy(
            src_ref=g_ref.at[2 * j + (1 - c)], dst_ref=out_ref.at[j],
            send_sem=send_sems.at[j], recv_sem=recv_sems.at[j],
            device_id=(x, y, 1 - c), device_id_type=MESH_IDS) for j in range(4)]

    return _direct_exchange(blocks, jax.ShapeDtypeStruct((4, m, n), blocks.dtype), 4, make_copies)


def _scatter_chips_exchange(parts):
    _, m, n = parts.shape
    flips = [(1, 0), (0, 1), (1, 1)]

    def make_copies(p_ref, out_ref, send_sems, recv_sems):
        x, y, c = _my_place()
        copies = []
        for k, (fx, fy) in enumerate(flips):
            tx = 1 - x if fx else x
            ty = 1 - y if fy else y
            copies.append(pltpu.make_async_remote_copy(
                src_ref=p_ref.at[2 * tx + ty], dst_ref=out_ref.at[k],
                send_sem=send_sems.at[k], recv_sem=recv_sems.at[k],
                device_id=(tx, ty, c), device_id_type=MESH_IDS))
        return copies

    return _direct_exchange(parts, jax.ShapeDtypeStruct((3, m, n), parts.dtype), 3, make_copies)


def _swap_one_exchange(v):
    def make_copies(v_ref, out_ref, send_sems, recv_sems):
        x, y, c = _my_place()
        return [pltpu.make_async_remote_copy(src_ref=v_ref, dst_ref=out_ref, send_sem=send_sems.at[0],
                                             recv_sem=recv_sems.at[0], device_id=(x, y, 1 - c),
                                             device_id_type=MESH_IDS)]

    return _direct_exchange(v, jax.ShapeDtypeStruct(v.shape, v.dtype), 1, make_copies)


def _gather8(v, name):
    return _with_own(_run_exchange(_gather_exchange(v), name), v)


def _carried(ex, refs, n_in, n_out, n_scratch):
    ins, ex_in = refs[:n_in], refs[n_in]
    outs, ex_out = refs[n_in + 1:n_in + 1 + n_out], refs[n_in + 1 + n_out]
    at = n_in + 2 + n_out
    return ins, outs + refs[at:at + n_scratch], ex.phases(ex_in, ex_out, *refs[at + n_scratch:])


def _ada_fwd(c_all, w_ada_sh, b_ada_sh):
    nb, cols = c_all.shape[0], w_ada_sh.shape[1]
    tn = 512

    def body(c_ref, w_ref, b_ref, o_ref):
        cv = c_ref[...]
        act = (cv * jax.nn.sigmoid(cv)).astype(BF16)
        o_ref[...] = _dot(act, w_ref[...].astype(BF16)) + b_ref[...]

    return pl.pallas_call(
        body, name="ada_fwd", grid=(cols // tn,),
        out_shape=jax.ShapeDtypeStruct((nb, cols), F32),
        in_specs=[pl.BlockSpec((nb, D_MODEL), lambda j: (0, 0)),
                  pl.BlockSpec((D_MODEL, tn), lambda j: (0, j)),
                  pl.BlockSpec((1, tn), lambda j: (0, j))],
        out_specs=pl.BlockSpec((nb, tn), lambda j: (0, j)),
        compiler_params=_params(("arbitrary",)),
    )(c_all, w_ada_sh, b_ada_sh)


def _ada_bwd(c_all, dmod_sh):
    nb, cols = dmod_sh.shape
    tn = 512

    def body(c_ref, d_ref, o_ref):
        cv = c_ref[...]
        act = (cv * jax.nn.sigmoid(cv)).astype(BF16)
        o_ref[...] = _dot(act, d_ref[...].astype(BF16), TN)

    return pl.pallas_call(
        body, name="ada_bwd", grid=(cols // tn,),
        out_shape=jax.ShapeDtypeStruct((D_MODEL, cols), F32),
        in_specs=[pl.BlockSpec((nb, D_MODEL), lambda j: (0, 0)),
                  pl.BlockSpec((nb, tn), lambda j: (0, j))],
        out_specs=pl.BlockSpec((D_MODEL, tn), lambda j: (0, j)),
        compiler_params=_params(("arbitrary",)),
    )(c_all, dmod_sh)


def _sum_lead(v, name):
    k, m, n = v.shape

    def body(v_ref, o_ref):
        acc = v_ref[0]
        for i in range(1, k):
            acc = acc + v_ref[i]
        o_ref[...] = acc

    return pl.pallas_call(
        body, name=name, out_shape=jax.ShapeDtypeStruct((m, n), F32),
        in_specs=[pl.BlockSpec((k, m, n), lambda: (0, 0, 0))],
        out_specs=pl.BlockSpec((m, n), lambda: (0, 0)),
        compiler_params=_params(),
    )(v)


def _adamw_math(w, g, m, v):
    mn = ADAM_B1 * m + (1.0 - ADAM_B1) * g
    vn = ADAM_B2 * v + (1.0 - ADAM_B2) * (g * g)
    m_hat = mn / (1.0 - ADAM_B1 ** ADAM_STEP)
    v_hat = vn / (1.0 - ADAM_B2 ** ADAM_STEP)
    return -ADAM_LR * (m_hat / (jnp.sqrt(v_hat) + ADAM_EPS) + ADAM_WD * w), mn, vn


def _adamw_small(g_sum, g_b_ada, ws, ms, vs):
    n = len(SMALL)

    def body(gs_ref, gb_ref, *refs):
        outs = refs[3 * n:]
        for p in range(n):
            rows_p = SMALL[p][1] // LANES
            g = gb_ref[...] if SMALL[p][0] == "b_ada" else gs_ref[SMALL_AT[p]:SMALL_AT[p] + rows_p, :]
            d, mn, vn = _adamw_math(refs[p][...], g, refs[n + p][...], refs[2 * n + p][...])
            outs[4 * p][...] = g
            outs[4 * p + 1][...] = d
            outs[4 * p + 2][...] = mn
            outs[4 * p + 3][...] = vn

    shapes = [jax.ShapeDtypeStruct((size // LANES, LANES), F32) for _, size in SMALL for _ in range(4)]
    flat = lambda arrs: [a.reshape(-1, LANES) for a in arrs]
    res = pl.pallas_call(body, name="adamw_small", out_shape=tuple(shapes), compiler_params=_params())(
        g_sum, g_b_ada, *flat(ws), *flat(ms), *flat(vs))
    return [tuple(r.reshape(w.shape) for r in res[4 * p:4 * p + 4]) for p, w in enumerate(ws)]


def _adamw_halves(w, g_mine, g_other, m, v, c_idx, name):
    r, cols = w.shape
    half = r // 2
    tr = half
    while tr * cols * 4 > (2 << 20) and tr % 16 == 0:
        tr //= 2

    def body(c_ref, w_ref, mine_ref, other_ref, m_ref, v_ref, g_ref, d_ref, mo_ref, vo_ref):
        g = jnp.where(pl.program_id(0) == c_ref[0], mine_ref[...], other_ref[...])
        g_ref[0] = g
        d_ref[0], mo_ref[0], vo_ref[0] = _adamw_math(w_ref[0], g, m_ref[0], v_ref[0])

    full = pl.BlockSpec((1, tr, cols), lambda h, i, c: (h, i, 0))
    part = pl.BlockSpec((tr, cols), lambda h, i, c: (i, 0))
    shape = jax.ShapeDtypeStruct((2, half, cols), F32)
    grid_spec = pltpu.PrefetchScalarGridSpec(
        num_scalar_prefetch=1, grid=(2, half // tr),
        in_specs=[full, part, part, full, full], out_specs=(full, full, full, full))
    split = lambda a: a.reshape(2, half, cols)
    res = pl.pallas_call(
        body, name=name, grid_spec=grid_spec, out_shape=(shape, shape, shape, shape),
        compiler_params=_params(("arbitrary", "arbitrary")),
    )(c_idx, split(w), g_mine, g_other, split(m), split(v))
    return tuple(a.reshape(r, cols) for a in res)


def _adamw(w, g, m, v, name, ex=None):
    rows, cols = w.shape
    tr = rows
    while tr * cols * 4 > (2 << 20) and tr % 16 == 0:
        tr //= 2

    def body(w_ref, g_ref, m_ref, v_ref, d_ref, mo_ref, vo_ref):
        d_ref[...], mo_ref[...], vo_ref[...] = _adamw_math(w_ref[...], g_ref[...], m_ref[...], v_ref[...])

    spec = pl.BlockSpec((tr, cols), lambda a, b, i: (i, 0))
    shape = jax.ShapeDtypeStruct((rows, cols), F32)
    return _carrier_call(body, ex, name, (1, 1, rows // tr), [w, g, m, v], [spec] * 4, [shape] * 3, [spec] * 3)


def _add_rows(m, n):
    fits = [d for d in range(16, m + 1, 16) if m % d == 0 and d * n * 4 <= (5 << 19)]
    assert fits, (m, n)
    return max(fits)


def _add_pairs(blocks, recv, c_idx, name):
    _, m, n = blocks.shape
    tr = _add_rows(m, n)

    def body(c_ref, a_ref, b_ref, o_ref, ob_ref):
        s = a_ref[...] + b_ref[...]
        o_ref[...] = s
        ob_ref[...] = s.astype(BF16)

    grid_spec = pltpu.PrefetchScalarGridSpec(
        num_scalar_prefetch=1, grid=(4, m // tr),
        in_specs=[pl.BlockSpec((1, tr, n), lambda j, i, c: (2 * j + c[0], i, 0)),
                  pl.BlockSpec((1, tr, n), lambda j, i, c: (j, i, 0))],
        out_specs=(pl.BlockSpec((1, tr, n), lambda j, i, c: (j, i, 0)),
                   pl.BlockSpec((1, tr, n), lambda j, i, c: (j, i, 0))))
    return pl.pallas_call(
        body, name=name, grid_spec=grid_spec,
        out_shape=(jax.ShapeDtypeStruct((4, m, n), F32), jax.ShapeDtypeStruct((4, m, n), BF16)),
        compiler_params=_params(("arbitrary", "arbitrary")),
    )(c_idx, blocks, recv)


def _add_chips(own, recv, name):
    m, n = own.shape
    tr = _add_rows(m, n)

    def body(a_ref, r_ref, o_ref):
        acc = a_ref[...]
        for k in range(3):
            acc = acc + r_ref[k].astype(F32)
        o_ref[...] = acc

    return pl.pallas_call(
        body, name=name, grid=(m // tr,),
        out_shape=jax.ShapeDtypeStruct((m, n), F32),
        in_specs=[pl.BlockSpec((tr, n), lambda i: (i, 0)), pl.BlockSpec((3, tr, n), lambda i: (0, i, 0))],
        out_specs=pl.BlockSpec((tr, n), lambda i: (i, 0)),
        compiler_params=_params(("arbitrary",)),
    )(own, recv)


def _row_spec(cols):
    return pl.BlockSpec((ROW_TILE, cols), lambda i: (i, 0))


def _mod_spec(tiles_per_seq):
    return pl.BlockSpec((1, 8, D_MODEL), lambda i: (i // tiles_per_seq, 0, 0))


def _table_spec(tiles_per_seq):
    return pl.BlockSpec((ROW_TILE, LANES), lambda i: (i % tiles_per_seq, 0))


def _fwd_in(x, mod, ln_g, ln_b, w_in, q_g, kv_g, w_uq, w_ukv, cos_t, sin_a, sin_b, seq):
    rows = x.shape[0]
    tm = min(2 * ROW_TILE, seq)
    tps = seq // tm

    def body(x_ref, mod_ref, g_ref, b_ref, win_ref, qg_ref, kvg_ref, wuq_ref, wukv_ref, cos_ref, sa_ref, sb_ref,
             x0_ref, h_ref, qkv_ref, lat_ref, qp_ref, kp_ref, vm_ref):
        def chain(rs):
            x0, _, _ = _ln_fwd(x_ref[rs, :], g_ref[...], b_ref[...])
            x0_ref[rs, :] = x0
            h = (x0 * (1.0 + mod_ref[0, 1:2, :]) + mod_ref[0, 0:1, :]).astype(BF16)
            h_ref[rs, :] = h
            yield
            proj = _dot(h, win_ref[...])
            yield
            qkv_ref[rs, :SB_W] = (proj[:, :SB_W] * SB_SCALE).astype(BF16)
            qkv_ref[rs, SB_W:] = proj[:, SB_W:3 * SB_W].astype(BF16)
            lat_ref[rs, :] = proj[:, 3 * SB_W:3 * SB_W + Q_RANK + KV_RANK]
            cq = proj[:, 3 * SB_W:3 * SB_W + Q_RANK]
            ckv = proj[:, 3 * SB_W + Q_RANK:3 * SB_W + Q_RANK + KV_RANK]
            kr = proj[:, D_IN_PAD - LANES:]
            cos, sa, sb = cos_ref[rs, :], sa_ref[rs, :], sb_ref[rs, :]
            cqn = (cq * lax.rsqrt(_mean(cq * cq) + RMS_EPS) * qg_ref[...]).astype(BF16)
            q_all = _dot(cqn, wuq_ref[...])
            ckvn = (ckv * lax.rsqrt(_mean(ckv * ckv) + RMS_EPS) * kvg_ref[...]).astype(BF16)
            kv = _dot(ckvn, wukv_ref[...])
            yield
            for hd in range(HEADS):
                sl = slice(hd * LANES, (hd + 1) * LANES)
                qp_ref[rs, sl] = _rope(q_all[:, sl], cos, sa, sb).astype(BF16)
            kr_rot = _rope(kr, cos, sa, sb)
            for hd in range(HEADS):
                sl = slice(hd * LANES, (hd + 1) * LANES)
                kp_ref[rs, sl] = (kv[:, sl] + kr_rot).astype(BF16)
            vm_ref[rs, :] = kv[:, HEADS * LANES:].astype(BF16)

        half = tm // 2
        _staggered([chain(slice(0, half)), chain(slice(half, tm))])

    row_spec = lambda cols: pl.BlockSpec((tm, cols), lambda i: (i, 0))
    table_spec = pl.BlockSpec((tm, LANES), lambda i: (i % tps, 0))
    outs = [(D_MODEL, F32), (D_MODEL, BF16), (3 * SB_W, BF16), (Q_RANK + KV_RANK, F32),
            (HEADS * LANES, BF16), (HEADS * LANES, BF16), (MLA_W, BF16)]
    return pl.pallas_call(
        body, name="fwd_in", grid=(rows // tm,),
        out_shape=tuple(jax.ShapeDtypeStruct((rows, n), dt) for n, dt in outs),
        in_specs=[row_spec(D_MODEL), pl.BlockSpec((1, 8, D_MODEL), lambda i: (i // tps, 0, 0)),
                  _const_spec((1, D_MODEL)), _const_spec((1, D_MODEL)),
                  _const_spec(w_in.shape), _const_spec((1, Q_RANK)), _const_spec((1, KV_RANK)),
                  _const_spec(w_uq.shape), _const_spec(w_ukv.shape), table_spec, table_spec, table_spec],
        out_specs=tuple(row_spec(n) for n, _ in outs),
        compiler_params=_params(("arbitrary",)),
    )(x, mod, ln_g, ln_b, w_in, q_g, kv_g, w_uq, w_ukv, cos_t, sin_a, sin_b)


HALF = 512
SHARD = 1024


def _mlp_weight_specs():
    return [pl.BlockSpec((8, HALF, SHARD), lambda i: (0, 0, 0), pipeline_mode=pl.Buffered(1)),
            pl.BlockSpec((8, HALF, SHARD), lambda i: (0, 1, 0), pipeline_mode=pl.Buffered(1))]


def _fwd_out(sb_y, mla_y, x0, mod, w_o, ln_g, ln_b, g_mlp, seq):
    rows = x0.shape[0]
    tm = ROW_TILE
    tps = seq // tm

    def body(sb_ref, ml_ref, x0_ref, mod_ref, wo_ref, g_ref, b_ref, wu_ref, wd_ref,
             mix_ref, y1_ref, h2_ref, u_ref, ff_ref, y2_ref):
        mix = _dot(sb_ref[...], wo_ref[:SB_W, :]) + _dot(ml_ref[...].astype(BF16), wo_ref[SB_W:, :])
        mix_ref[...] = mix
        y1 = ALPHA * x0_ref[...] + (1.0 + mod_ref[0, 2:3, :]) * mix
        y1_ref[...] = y1
        x1, _, _ = _ln_fwd(y1, g_ref[...], b_ref[...])
        h2 = (x1 * (1.0 + mod_ref[0, 4:5, :]) + mod_ref[0, 3:4, :]).astype(BF16)
        h2_ref[...] = h2
        h_lo, h_hi = h2[:, :HALF], h2[:, HALF:]
        ff = jnp.zeros((tm, D_MODEL), F32)
        for chip in range(4):
            u = _dot(h_lo, wu_ref[2 * chip]) + _dot(h_hi, wu_ref[2 * chip + 1])
            u_ref[:, chip * SHARD:(chip + 1) * SHARD] = u.astype(BF16)
            act = jnp.square(jnp.maximum(u, 0.0)).astype(BF16)
            ff = ff + _dot(act[:, :HALF], wd_ref[2 * chip]) + _dot(act[:, HALF:], wd_ref[2 * chip + 1])
        ff_ref[...] = ff
        y2_ref[...] = ALPHA * x1 + (1.0 + mod_ref[0, 5:6, :]) * ff

    outs = [(D_MODEL, F32), (D_MODEL, F32), (D_MODEL, BF16), (D_FF, BF16), (D_MODEL, F32), (D_MODEL, F32)]
    return pl.pallas_call(
        body, name="fwd_out", grid=(rows // tm,),
        out_shape=tuple(jax.ShapeDtypeStruct((rows, n), dt) for n, dt in outs),
        in_specs=[_row_spec(SB_W), _row_spec(MLA_W), _row_spec(D_MODEL), _mod_spec(tps), _const_spec(w_o.shape),
                  _const_spec((1, D_MODEL)), _const_spec((1, D_MODEL))] + _mlp_weight_specs(),
        out_specs=tuple(_row_spec(n) for n, _ in outs),
        compiler_params=_params(("arbitrary",)),
    )(sb_y, mla_y, x0, mod, w_o, ln_g, ln_b, g_mlp, g_mlp)


def _staggered(chains):
    live = []
    for chain in chains:
        live.append(chain)
        live = [c for c in live if next(c, StopIteration) is not StopIteration]
    while live:
        live = [c for c in live if next(c, StopIteration) is not StopIteration]


def _acc_spec(rows=8, cols=D_MODEL):
    return pl.BlockSpec((rows, cols), lambda i: (0, 0))


def _bwd_out(y2, tgt, ff, u, y1, mix, mod, ln2_g, ln2_b, ln1_g, ln1_b, g_mlp, w_o, seq):
    rows = y2.shape[0]
    nb = rows // seq
    tm = ROW_TILE
    tps = seq // tm

    def body(y2_ref, t_ref, ff_ref, u_ref, y1_ref, mix_ref, mod_ref, g2_ref, b2_ref, g_ref, b_ref, wu_ref, wd_ref,
             wo_ref, dy1_ref, dmix_ref, do_ref, dff_ref, du_ref, acc_ref, dmod_ref):
        i = pl.program_id(0)

        @pl.when(i == 0)
        def _():
            acc_ref[...] = jnp.zeros_like(acc_ref)

        @pl.when(i % tps == 0)
        def _():
            dmod_ref[...] = jnp.zeros_like(dmod_ref)

        g2 = g2_ref[...]
        x2, xhat2, rstd2 = _ln_fwd(y2_ref[...], g2, b2_ref[...])
        err = x2 - t_ref[...]
        dx2 = err * (1.0 / D_MODEL)
        acc_ref[0:1, :] += _rowsum(dx2 * xhat2)
        acc_ref[1:2, :] += _rowsum(dx2)
        acc_ref[2:3, :] += _rowsum(err * err) * (0.5 / D_MODEL)
        dy2 = _ln_bwd(dx2, xhat2, rstd2, g2)
        dmod_ref[0, 5:6, :] += _rowsum(dy2 * ff_ref[...])
        dff = ((1.0 + mod_ref[0, 5:6, :]) * dy2).astype(BF16)
        dff_ref[...] = dff
        for blk in range(8):
            cols = slice(blk * HALF, (blk + 1) * HALF)
            da = _dot(dff, wd_ref[blk], NT)
            du_ref[:, cols] = (da * (2.0 * jnp.maximum(u_ref[:, cols].astype(F32), 0.0))).astype(BF16)

        g = g_ref[...]
        x1, xhat, rstd = _ln_fwd(y1_ref[...], g, b_ref[...])
        halves = []
        for half in range(2):
            acc = jnp.zeros((tm, HALF), F32)
            for chip in range(4):
                acc = acc + _dot(du_ref[:, chip * SHARD:(chip + 1) * SHARD], wu_ref[2 * chip + half], NT)
            halves.append(acc)
        dh2 = jnp.concatenate(halves, axis=1)
        dmod_ref[0, 3:4, :] += _rowsum(dh2)
        dmod_ref[0, 4:5, :] += _rowsum(dh2 * x1)
        dx1 = ALPHA * dy2 + dh2 * (1.0 + mod_ref[0, 4:5, :])
        acc_ref[3:4, :] += _rowsum(dx1 * xhat)
        acc_ref[4:5, :] += _rowsum(dx1)
        dy1 = _ln_bwd(dx1, xhat, rstd, g)
        dy1_ref[...] = dy1
        dmod_ref[0, 2:3, :] += _rowsum(dy1 * mix_ref[...])
        dmix = ((1.0 + mod_ref[0, 2:3, :]) * dy1).astype(BF16)
        dmix_ref[...] = dmix
        do_ref[...] = _dot(dmix, wo_ref[...], NT)

    outs = [(D_MODEL, F32), (D_MODEL, BF16), (D_MODEL, F32), (D_MODEL, BF16), (D_FF, BF16)]
    return pl.pallas_call(
        body, name="bwd_out", grid=(rows // tm,),
        out_shape=tuple(jax.ShapeDtypeStruct((rows, n), dt) for n, dt in outs)
        + (jax.ShapeDtypeStruct((8, D_MODEL), F32), jax.ShapeDtypeStruct((nb, 8, D_MODEL), F32)),
        in_specs=[_row_spec(D_MODEL), _row_spec(D_MODEL), _row_spec(D_MODEL), _row_spec(D_FF), _row_spec(D_MODEL),
                  _row_spec(D_MODEL), _mod_spec(tps), _const_spec((1, D_MODEL)), _const_spec((1, D_MODEL)),
                  _const_spec((1, D_MODEL)), _const_spec((1, D_MODEL))] + _mlp_weight_specs()
        + [_const_spec(w_o.shape)],
        out_specs=tuple(_row_spec(n) for n, _ in outs) + (_acc_spec(), _mod_spec(tps)),
        compiler_params=_params(("arbitrary",)),
    )(y2, tgt, ff, u, y1, mix, mod, ln2_g, ln2_b, ln1_g, ln1_b, g_mlp, g_mlp, w_o)


def _bwd_in(dqp, dkp, dvm, dq_sb, dk_sb, dv_sb, lat, x, dy1, mod, ln_g, ln_b, w_in, q_g, kv_g, w_uq, w_ukv,
            cos_t, sin_a, sin_b, seq):
    rows = x.shape[0]
    nb = rows // seq
    tm = ROW_TILE
    tps = seq // tm
    n_lat = Q_RANK + KV_RANK

    def body(dqp_ref, dkp_ref, dvm_ref, dqs_ref, dks_ref, dvs_ref, lat_ref, x_ref, dy1_ref, mod_ref,
             g_ref, b_ref, win_ref, qg_ref, kvg_ref, wuq_ref, wukv_ref, cos_ref, sa_ref, sb_ref,
             dx_ref, dproj_ref, dqall_ref, dkv_ref, latn_ref, acc_ref, accl_ref, dmod_ref):
        i = pl.program_id(0)

        @pl.when(i == 0)
        def _():
            acc_ref[...] = jnp.zeros_like(acc_ref)
            accl_ref[...] = jnp.zeros_like(accl_ref)

        @pl.when(i % tps == 0)
        def _():
            dmod_ref[...] = jnp.zeros_like(dmod_ref)

        cos, sa, sb = cos_ref[...], sa_ref[...], sb_ref[...]
        lane = lax.broadcasted_iota(jnp.int32, (tm, LANES), 1)
        for hd in range(HEADS):
            sl = slice(hd * LANES, (hd + 1) * LANES)
            dqall_ref[:, sl] = _rope_t(dqp_ref[:, sl], cos, sa, sb).astype(BF16)
        dcqn = _dot(dqall_ref[...], wuq_ref[...], NT)
        cq = lat_ref[:, :Q_RANK]
        qg = qg_ref[...]
        rq = lax.rsqrt(_mean(cq * cq) + RMS_EPS)
        cqn = cq * rq
        latn_ref[:, :Q_RANK] = (cqn * qg).astype(BF16)
        accl_ref[0:1, :Q_RANK] += _rowsum(dcqn * cqn)
        dqg = dcqn * qg
        dcq = rq * (dqg - cqn * _mean(dqg * cqn))
        dkr = jnp.zeros((tm, LANES), F32)
        for hd in range(HEADS):
            sl = slice(hd * LANES, (hd + 1) * LANES)
            dk = dkp_ref[:, sl]
            dkr = dkr + dk
            dkv_ref[:, sl] = jnp.where(lane < NOPE, dk, 0.0).astype(BF16)
        dkv_ref[:, HEADS * LANES:] = dvm_ref[...].astype(BF16)
        dckvn = _dot(dkv_ref[...], wukv_ref[...], NT)
        ckv = lat_ref[:, Q_RANK:]
        kvg = kvg_ref[...]
        rkv = lax.rsqrt(_mean(ckv * ckv) + RMS_EPS)
        ckvn = ckv * rkv
        latn_ref[:, Q_RANK:] = (ckvn * kvg).astype(BF16)
        accl_ref[1:2, :KV_RANK] += _rowsum(dckvn * ckvn)
        dkg = dckvn * kvg
        dckv = rkv * (dkg - ckvn * _mean(dkg * ckvn))
        dkr = _rope_t(jnp.where(lane >= NOPE, dkr, 0.0), cos, sa, sb)
        dproj_ref[:, :SB_W] = dqs_ref[...]
        dproj_ref[:, SB_W:2 * SB_W] = dks_ref[...].astype(BF16)
        dproj_ref[:, 2 * SB_W:3 * SB_W] = dvs_ref[...].astype(BF16)
        dproj_ref[:, 3 * SB_W:3 * SB_W + Q_RANK] = dcq.astype(BF16)
        dproj_ref[:, 3 * SB_W + Q_RANK:3 * SB_W + n_lat] = dckv.astype(BF16)
        dproj_ref[:, D_IN_PAD - LANES:] = dkr.astype(BF16)
        dh = _dot(dproj_ref[...], win_ref[...], NT)
        g = g_ref[...]
        x0, xhat, rstd = _ln_fwd(x_ref[...], g, b_ref[...])
        dmod_ref[0, 0:1, :] += _rowsum(dh)
        dmod_ref[0, 1:2, :] += _rowsum(dh * x0)
        dx0 = ALPHA * dy1_ref[...] + dh * (1.0 + mod_ref[0, 1:2, :])
        acc_ref[0:1, :] += _rowsum(dx0 * xhat)
        acc_ref[1:2, :] += _rowsum(dx0)
        dx_ref[...] = _ln_bwd(dx0, xhat, rstd, g)

    outs = [(D_MODEL, F32), (D_IN_PAD, BF16), (HEADS * LANES, BF16), (HEADS * LANES + MLA_W, BF16), (n_lat, BF16)]
    return pl.pallas_call(
        body, name="bwd_in", grid=(rows // tm,),
        out_shape=tuple(jax.ShapeDtypeStruct((rows, n), dt) for n, dt in outs)
        + (jax.ShapeDtypeStruct((8, D_MODEL), F32), jax.ShapeDtypeStruct((8, Q_RANK), F32),
           jax.ShapeDtypeStruct((nb, 8, D_MODEL), F32)),
        in_specs=[_row_spec(HEADS * LANES), _row_spec(HEADS * LANES), _row_spec(MLA_W),
                  _row_spec(SB_W), _row_spec(SB_W), _row_spec(SB_W), _row_spec(n_lat),
                  _row_spec(D_MODEL), _row_spec(D_MODEL), _mod_spec(tps),
                  _const_spec((1, D_MODEL)), _const_spec((1, D_MODEL)), _const_spec(w_in.shape),
                  _const_spec((1, Q_RANK)), _const_spec((1, KV_RANK)), _const_spec(w_uq.shape),
                  _const_spec(w_ukv.shape), _table_spec(tps), _table_spec(tps), _table_spec(tps)],
        out_specs=tuple(_row_spec(n) for n, _ in outs) + (_acc_spec(), _acc_spec(8, Q_RANK), _mod_spec(tps)),
        compiler_params=_params(("arbitrary",)),
    )(dqp, dkp, dvm, dq_sb, dk_sb, dv_sb, lat, x, dy1, mod, ln_g, ln_b, w_in, q_g, kv_g, w_uq, w_ukv,
      cos_t, sin_a, sin_b)


def _wgrad(a, b, name, tm=512, tn=1024, tk=2048, ex=None):
    rows, m = a.shape
    n = b.shape[1]
    tm, tn, tk = min(tm, m), min(tn, n), min(tk, rows)
    if m % tm:
        tm = m
    if n % tn:
        tn = n

    def body(a_ref, b_ref, o_ref):
        @pl.when(pl.program_id(2) == 0)
        def _():
            o_ref[...] = jnp.zeros_like(o_ref)

        o_ref[...] += _dot(a_ref[...].astype(BF16), b_ref[...].astype(BF16), TN)

    res = _carrier_call(
        body, ex, name, (m // tm, n // tn, rows // tk), [a, b],
        [pl.BlockSpec((tk, tm), lambda i, j, k: (k, i)), pl.BlockSpec((tk, tn), lambda i, j, k: (k, j))],
        [jax.ShapeDtypeStruct((m, n), F32)], [pl.BlockSpec((tm, tn), lambda i, j, k: (i, j))])
    return res[0] if ex is None else res


def _wgrad_packed(a, b, name, block_of, row_block, split=1, pre=None, into=None, tk=2048):
    rows, m = a.shape
    n = b.shape[1]
    tm = HALF
    part = tm // split
    tk = min(tk, rows)
    shape = jax.ShapeDtypeStruct((8, GROUP_MLP[0], PACK_COLS), F32)

    def body(a_ref, b_ref, *rest):
        o_ref = rest[-1]

        @pl.when(pl.program_id(2) == 0)
        def _():
            o_ref[...] = jnp.zeros_like(o_ref)

        av = a_ref[...]
        if pre == "relu2":
            av = jnp.square(jnp.maximum(av.astype(F32), 0.0))
        prod = _dot(av.astype(BF16), b_ref[...].astype(BF16), TN)
        for s in range(split):
            o_ref[s] += prod[s * part:(s + 1) * part]

    in_specs = [pl.BlockSpec((tk, tm), lambda i, j, k: (k, i)), pl.BlockSpec((tk, SHARD), lambda i, j, k: (k, j))]
    operands = [a, b]
    if into is not None:
        in_specs.append(pl.BlockSpec(memory_space=pl.ANY))
        operands.append(into)
    return pl.pallas_call(
        body, name=name, grid=(m // tm, n // SHARD, rows // tk), out_shape=shape,
        in_specs=in_specs,
        out_specs=pl.BlockSpec((split, part, SHARD), lambda i, j, k: (block_of(i, j), row_block, 0)),
        input_output_aliases={} if into is None else {2: 0},
        compiler_params=_params(("arbitrary", "arbitrary", "arbitrary")),
    )(*operands)


def _pair(pp):
    return slice(pp * LANES, (pp + 1) * LANES)


def _head_mask(lane, hh):
    return jnp.where((lane >= 64) if hh else (lane < 64), 1.0, 0.0).astype(BF16)


def _tri(t, kind):
    s = lax.broadcasted_iota(jnp.int32, (t, t), 0)
    j = lax.broadcasted_iota(jnp.int32, (t, t), 1)
    one = jnp.where(j > s if kind == "later" else j < s, 1.0, 0.0).astype(BF16)
    return jnp.concatenate([one, one], axis=1)


def _split_dot(tri2, v):
    hi = v.astype(BF16)
    lo = (v - hi.astype(F32)).astype(BF16)
    return _dot(tri2, jnp.concatenate([hi, lo], axis=0))


def _sb_logits(z, valid):
    log_keep = -(jnp.maximum(z, 0.0) + jnp.log(1.0 + jnp.exp2(jnp.abs(z) * NEG_LOG2E)))
    log_beta = z + log_keep
    if valid is not None:
        log_keep = jnp.where(valid, log_keep, 0.0)
    return log_keep, log_beta


def _carrier_call(body, ex, name, grid, operands, in_specs, out_shapes, out_specs, scratch=()):
    n_in, n_out = len(operands), len(out_shapes)
    total = grid[0] * grid[1] * grid[2]
    any_spec = pl.BlockSpec(memory_space=pl.ANY)

    def carrier(*refs):
        ins, outs, (start, middle, finish) = _carried(ex, refs, n_in, n_out, len(scratch))
        step = (pl.program_id(0) * grid[1] + pl.program_id(1)) * grid[2] + pl.program_id(2)
        pl.when(step == 0)(start)
        pl.when(step == total // 2)(middle)
        body(*ins, *outs)
        pl.when(step == total - 1)(finish)

    carried = ex is not None
    return pl.pallas_call(
        carrier if carried else body, name=name, grid=grid,
        out_shape=tuple(out_shapes) + ((ex.out_shape,) if carried else ()),
        in_specs=list(in_specs) + ([any_spec] if carried else []),
        out_specs=tuple(out_specs) + ((any_spec,) if carried else ()),
        scratch_shapes=list(scratch) + (ex.scratch if carried else []),
        compiler_params=_params(("arbitrary", "arbitrary", "arbitrary")),
    )(*operands, *([ex.operand] if carried else []))


def _sb_fwd(qkv, seq, ex=None):
    rows = qkv.shape[0]
    nb = rows // seq
    t = min(ATTN_TILE, seq)
    nq = seq // t
    assert nq <= CAR_SLOTS, (seq, t)
    ap = ATTN_PAIRS
    width = ap * LANES
    groups = SB_W // width
    hds = [(pp, hh) for pp in range(ap) for hh in range(2)]

    def body(q_ref, k_ref, v_ref, tri_ref, o_ref, car_ref, acc_ref):
        i = pl.program_id(2)
        lane = lax.broadcasted_iota(jnp.int32, (t, LANES), 1)
        key = lax.broadcasted_iota(jnp.int32, (t, t), 0)
        qry = lax.broadcasted_iota(jnp.int32, (t, t), 1)
        strict = key < qry
        tri = tri_ref[...]
        masks = [_head_mask(lane, hh) for hh in range(2)]
        qms = [q_ref[:, _pair(pp)] * masks[hh] for pp, hh in hds]
        acc_ref[...] = jnp.zeros_like(acc_ref)
        car_ref[...] = jnp.zeros_like(car_ref)

        def step(kb, c_sums, valid):
            start = pl.multiple_of(kb * t, t)
            kss = [k_ref[pl.ds(start, t), _pair(pp)] for pp in range(ap)]
            vss = [v_ref[pl.ds(start, t), _pair(pp)] for pp in range(ap)]
            zs = [_dot(kss[pp], qms[n], NT) for n, (pp, _) in enumerate(hds)]
            logs = [_sb_logits(z, valid) for z in zs]
            sufs = [_split_dot(tri, lg[0]) for lg in logs]
            new_sums = []
            for n, (pp, hh) in enumerate(hds):
                log_keep, log_beta = logs[n]
                w = jnp.exp(log_beta + sufs[n] + c_sums[n])
                if valid is not None:
                    w = jnp.where(valid, w, 0.0)
                acc_ref[pp] += _dot(vss[pp] * masks[hh], w.astype(BF16), TN)
                car_ref[0, pl.ds(n * CAR_SLOTS + kb, 1), :] = c_sums[n]
                new_sums.append(c_sums[n] + sufs[n][0:1, :] + log_keep[0:1, :])
            return tuple(new_sums)

        c_sums = step(i, tuple(jnp.zeros((1, t), F32) for _ in hds), strict)
        lax.fori_loop(0, i, lambda j, cr: step(i - 1 - j, cr, None), c_sums)
        for pp in range(ap):
            o_ref[:, _pair(pp)] = acc_ref[pp].T.astype(BF16)

    qspec = pl.BlockSpec((t, width), lambda b, p, i: (b * nq + i, p))
    car_rows = len(hds) * CAR_SLOTS
    return _carrier_call(
        body, ex, "sb_fwd", (nb, groups, nq),
        [qkv, qkv, qkv, _tri(t, "later")],
        [qspec,
         pl.BlockSpec((seq, width), lambda b, p, i: (b, groups + p)),
         pl.BlockSpec((seq, width), lambda b, p, i: (b, 2 * groups + p)),
         _const_spec((t, 2 * t))],
        [jax.ShapeDtypeStruct((rows, SB_W), BF16), jax.ShapeDtypeStruct((nb * nq, HEADS * CAR_SLOTS, t), F32)],
        [qspec, pl.BlockSpec((1, car_rows, t), lambda b, p, i: (b * nq + i, p, 0))],
        scratch=[pltpu.VMEM((ap, LANES, t), F32)])


def _sb_bwd(qkv, d_out, cars, seq, ex=None):
    rows = qkv.shape[0]
    nb = rows // seq
    t = min(ATTN_TILE, seq)
    nq = seq // t
    ap = ATTN_PAIRS
    width = ap * LANES
    groups = SB_W // width
    hds = [(pp, hh) for pp in range(ap) for hh in range(2)]

    def body(q_ref, k_ref, v_ref, do_ref, car_ref, tri_ref, pre_ref, dq_ref, dk_ref, dv_ref, dq_acc):
        i = pl.program_id(2)

        @pl.when(i == 0)
        def _():
            dk_ref[...] = jnp.zeros_like(dk_ref)
            dv_ref[...] = jnp.zeros_like(dv_ref)

        lane = lax.broadcasted_iota(jnp.int32, (t, LANES), 1)
        key = lax.broadcasted_iota(jnp.int32, (t, t), 0)
        qry = lax.broadcasted_iota(jnp.int32, (t, t), 1)
        strict = key < qry
        tri, pre = tri_ref[...], pre_ref[...]
        masks = [_head_mask(lane, hh) for hh in range(2)]
        qms = [q_ref[:, _pair(pp)] * masks[hh] for pp, hh in hds]
        doms = [do_ref[:, _pair(pp)].astype(BF16) * masks[hh] for pp, hh in hds]
        dq_acc[...] = jnp.zeros_like(dq_acc)

        def step(kb, g_pres, valid):
            start = pl.multiple_of(kb * t, t)
            kss = [k_ref[pl.ds(start, t), _pair(pp)] for pp in range(ap)]
            vss = [v_ref[pl.ds(start, t), _pair(pp)] for pp in range(ap)]
            zs = [_dot(kss[pp], qms[n], NT) for n, (pp, _) in enumerate(hds)]
            dws = [_dot(vss[pp], doms[n], NT) for n, (pp, _) in enumerate(hds)]
            logs = [_sb_logits(z, valid) for z in zs]
            sufs = [_split_dot(tri, lg[0]) for lg in logs]
            ws, gs = [], []
            for n in range(len(hds)):
                c_sum = car_ref[0, pl.ds(n * CAR_SLOTS + kb, 1), :]
                w = jnp.exp(logs[n][1] + sufs[n] + c_sum)
                if valid is not None:
                    w = jnp.where(valid, w, 0.0)
                ws.append(w)
                gs.append(dws[n] * w)
            pres = [_split_dot(pre, gs[n]) for n in range(len(hds))]
            befores = [g_pres[n] + pres[n] for n in range(len(hds))]
            for pp in range(ap):
                a, b = 2 * pp, 2 * pp + 1
                dv_ref[pl.ds(start, t), _pair(pp)] += _dot(ws[a].astype(BF16), doms[a]) + _dot(ws[b].astype(BF16), doms[b])
            dzbs = []
            for n in range(len(hds)):
                beta = jnp.exp(logs[n][1])
                dz = gs[n] * (1.0 - beta) - beta * befores[n]
                if valid is not None:
                    dz = jnp.where(valid, dz, 0.0)
                dzbs.append(dz.astype(BF16))
            for pp in range(ap):
                a, b = 2 * pp, 2 * pp + 1
                dq_acc[pp] += _dot(dzbs[a], kss[pp] * masks[0], TN) + _dot(dzbs[b], kss[pp] * masks[1], TN)
                dk_ref[pl.ds(start, t), _pair(pp)] += _dot(dzbs[a], qms[a]) + _dot(dzbs[b], qms[b])
            return tuple(g_pres[n] + pres[n][t - 1:t, :] + gs[n][t - 1:t, :] for n in range(len(hds)))

        g_pres = lax.fori_loop(0, i, lambda kb, cr: step(kb, cr, None), tuple(jnp.zeros((1, t), F32) for _ in hds))
        step(i, g_pres, strict)
        for pp in range(ap):
            dq_ref[:, _pair(pp)] = (dq_acc[pp] * SB_SCALE).astype(BF16)

    qspec = pl.BlockSpec((t, width), lambda b, p, i: (b * nq + i, p))
    kspec_out = pl.BlockSpec((seq, width), lambda b, p, i: (b, p))
    car_rows = len(hds) * CAR_SLOTS
    return _carrier_call(
        body, ex, "sb_bwd", (nb, groups, nq),
        [qkv, qkv, qkv, d_out, cars, _tri(t, "later"), _tri(t, "earlier")],
        [qspec,
         pl.BlockSpec((seq, width), lambda b, p, i: (b, groups + p)),
         pl.BlockSpec((seq, width), lambda b, p, i: (b, 2 * groups + p)),
         qspec, pl.BlockSpec((1, car_rows, t), lambda b, p, i: (b * nq + i, p, 0)),
         _const_spec((t, 2 * t)), _const_spec((t, 2 * t))],
        [jax.ShapeDtypeStruct((rows, SB_W), BF16), jax.ShapeDtypeStruct((rows, SB_W), F32),
         jax.ShapeDtypeStruct((rows, SB_W), F32)],
        [qspec, kspec_out, kspec_out],
        scratch=[pltpu.VMEM((ap, t, LANES), F32)])


def _mla_scores(ks, qh, allowed):
    s = _dot(ks, qh, NT) * (MLA_SCALE * -NEG_LOG2E)
    if allowed is not None:
        s = jnp.where(allowed, s, jnp.finfo(F32).min)
    return s


def _mla_fwd(qp, kp, vm, seq, ex=None, chunk=64):
    rows = qp.shape[0]
    nb = rows // seq
    t = min(ATTN_TILE, seq)
    nq = seq // t
    shift = int(math.log2(chunk))
    ap = ATTN_PAIRS
    width = ap * LANES
    groups = MLA_W // width
    hds = [(pp, hh) for pp in range(ap) for hh in range(2)]

    def body(q_ref, k_ref, v_ref, o_ref, lse_ref, acc_ref):
        i = pl.program_id(2)
        lane = lax.broadcasted_iota(jnp.int32, (t, LANES), 1)
        key = lax.broadcasted_iota(jnp.int32, (t, t), 0)
        qry = lax.broadcasted_iota(jnp.int32, (t, t), 1)
        allowed_diag = jnp.right_shift(key, shift) <= jnp.right_shift(qry, shift)
        masks = [_head_mask(lane, hh) for hh in range(2)]
        qhs = [q_ref[:, _pair(n)] for n in range(len(hds))]
        acc_ref[...] = jnp.zeros_like(acc_ref)

        def step(kb, carry, allowed):
            start = pl.multiple_of(kb * t, t)
            vss = [v_ref[pl.ds(start, t), _pair(pp)] for pp in range(ap)]
            scores = [_mla_scores(k_ref[pl.ds(start, t), _pair(n)], qhs[n], allowed) for n in range(len(hds))]
            new = []
            for n, (pp, hh) in enumerate(hds):
                m_run, l_run = carry[n]
                s = scores[n]
                m_new = jnp.maximum(m_run, jnp.max(s, axis=0, keepdims=True))
                p = jnp.exp2(s - m_new)
                scale = jnp.exp2(m_run - m_new)
                l_run = scale * l_run + jnp.sum(p, axis=0, keepdims=True)
                acc_ref[n] = scale * acc_ref[n] + _dot(vss[pp] * masks[hh], p.astype(BF16), TN)
                new.append((m_new, l_run))
            return tuple(new)

        init = (jnp.full((1, t), jnp.finfo(F32).min, F32), jnp.zeros((1, t), F32))
        carry = step(i, tuple(init for _ in hds), allowed_diag)
        carry = lax.fori_loop(0, i, lambda kb, cr: step(kb, cr, None), carry)
        lse_rows = []
        for pp in range(ap):
            out_t = jnp.zeros((LANES, t), F32)
            for hh in range(2):
                m_run, l_run = carry[2 * pp + hh]
                out_t = out_t + acc_ref[2 * pp + hh] / l_run
                lse_rows.append(m_run + jnp.log(l_run) * -NEG_LOG2E)
            o_ref[:, _pair(pp)] = out_t.T
        lse_t = jnp.concatenate(lse_rows + [jnp.zeros((LANES - len(hds), t), F32)], axis=0)
        lse_ref[...] = jnp.zeros_like(lse_ref)
        lse_ref[:, _pair(0)] = lse_t.T

    ospec = pl.BlockSpec((t, width), lambda b, p, i: (b * nq + i, p))
    return _carrier_call(
        body, ex, "mla_fwd", (nb, groups, nq), [qp, kp, vm],
        [pl.BlockSpec((t, 2 * width), lambda b, p, i: (b * nq + i, p)),
         pl.BlockSpec((seq, 2 * width), lambda b, p, i: (b, p)),
         pl.BlockSpec((seq, width), lambda b, p, i: (b, p))],
        [jax.ShapeDtypeStruct((rows, MLA_W), F32), jax.ShapeDtypeStruct((rows, MLA_W), F32)],
        [ospec, ospec], scratch=[pltpu.VMEM((len(hds), LANES, t), F32)])


def _mla_bwd(qp, kp, vm, d_out, out, lse, seq, ex=None, chunk=64):
    rows = qp.shape[0]
    nb = rows // seq
    t = min(ATTN_TILE, seq)
    nq = seq // t
    shift = int(math.log2(chunk))
    ap = ATTN_PAIRS
    width = ap * LANES
    groups = MLA_W // width
    hds = [(pp, hh) for pp in range(ap) for hh in range(2)]
    nh = len(hds)

    def body(q_ref, k_ref, v_ref, do_ref, o_ref, lse_ref, dq_ref, dk_ref, dv_ref):
        i = pl.program_id(2)

        @pl.when(i == 0)
        def _():
            dk_ref[...] = jnp.zeros_like(dk_ref)
            dv_ref[...] = jnp.zeros_like(dv_ref)

        lane = lax.broadcasted_iota(jnp.int32, (t, LANES), 1)
        key = lax.broadcasted_iota(jnp.int32, (t, t), 0)
        qry = lax.broadcasted_iota(jnp.int32, (t, t), 1)
        allowed_diag = jnp.right_shift(key, shift) <= jnp.right_shift(qry, shift)
        qhs = [q_ref[:, _pair(n)] for n in range(nh)]
        lse_t = lse_ref[:, _pair(0)].T
        doms, deltas, lse_hs = [], [], []
        for pp in range(ap):
            do = do_ref[:, _pair(pp)]
            d_o_t = (do * o_ref[:, _pair(pp)]).T
            for hh in range(2):
                doms.append(do.astype(BF16) * _head_mask(lane, hh))
                deltas.append(jnp.sum(d_o_t[hh * 64:(hh + 1) * 64], axis=0, keepdims=True))
                lse_hs.append(lse_t[2 * pp + hh:2 * pp + hh + 1])

        dq_ref[...] = jnp.zeros_like(dq_ref)

        def step(kb, allowed):
            start = pl.multiple_of(kb * t, t)
            vss = [v_ref[pl.ds(start, t), _pair(pp)] for pp in range(ap)]
            kss = [k_ref[pl.ds(start, t), _pair(n)] for n in range(nh)]
            scores = [_mla_scores(kss[n], qhs[n], allowed) for n in range(nh)]
            dps = [_dot(vss[pp], doms[n], NT) for n, (pp, _) in enumerate(hds)]
            ps = [jnp.exp2(scores[n] - lse_hs[n]) for n in range(nh)]
            dss = [(ps[n] * (dps[n] - deltas[n]) * MLA_SCALE).astype(BF16) for n in range(nh)]
            for pp in range(ap):
                a, b = 2 * pp, 2 * pp + 1
                dv_ref[pl.ds(start, t), _pair(pp)] += _dot(ps[a].astype(BF16), doms[a]) + _dot(ps[b].astype(BF16), doms[b])
            for n in range(nh):
                dk_ref[pl.ds(start, t), _pair(n)] += _dot(dss[n], qhs[n])
                dq_ref[:, _pair(n)] += _dot(dss[n], kss[n], TN)

        def off_diagonal(kb, nothing):
            step(kb, None)
            return nothing

        lax.fori_loop(0, i, off_diagonal, 0)
        step(i, allowed_diag)

    ospec = pl.BlockSpec((t, width), lambda b, p, i: (b * nq + i, p))
    return _carrier_call(
        body, ex, "mla_bwd", (nb, groups, nq), [qp, kp, vm, d_out, out, lse],
        [pl.BlockSpec((t, 2 * width), lambda b, p, i: (b * nq + i, p)),
         pl.BlockSpec((seq, 2 * width), lambda b, p, i: (b, p)),
         pl.BlockSpec((seq, width), lambda b, p, i: (b, p)),
         pl.BlockSpec((t, width), lambda b, p, i: (b * nq + i, groups + p)),
         ospec, ospec],
        [jax.ShapeDtypeStruct((rows, HEADS * LANES), F32), jax.ShapeDtypeStruct((rows, HEADS * LANES), F32),
         jax.ShapeDtypeStruct((rows, MLA_W), F32)],
        [pl.BlockSpec((t, 2 * width), lambda b, p, i: (b * nq + i, p)),
         pl.BlockSpec((seq, 2 * width), lambda b, p, i: (b, p)),
         pl.BlockSpec((seq, width), lambda b, p, i: (b, p))])


PACK_COLS = 1024
PACK_ALIGN = 16
GROUP_IN = (384, ((1024, 552, 1), (384, 192, 1), (256, 256, 1)))
GROUP_MLP = (1152, ((1024, 1024, 1), (1024, 1024, 0), (256, 1024, 0)))


def _pack_rows(r, c):
    return (r // 2) * c // PACK_COLS


def _slot_rows(r, c):
    return -(-_pack_rows(r, c) // PACK_ALIGN) * PACK_ALIGN


def _join_slots(parts, group):
    total, weights = group
    padded = [jnp.pad(p, ((0, 0), (0, _slot_rows(r, c) - p.shape[1]), (0, 0))) for p, (r, c, _) in zip(parts, weights)]
    used = sum(_slot_rows(r, c) for r, c, _ in weights)
    if total > used:
        padded.append(jnp.zeros((parts[0].shape[0], total - used, PACK_COLS), parts[0].dtype))
    return jnp.concatenate(padded, axis=1)


def _split_slots(packed, group):
    out, at = [], 0
    for r, c, _ in group[1]:
        out.append(packed[:, at:at + _pack_rows(r, c), :])
        at += _slot_rows(r, c)
    return out


def _pack_halves(shards, group):
    return _join_slots([s.reshape(2, _pack_rows(r, c), PACK_COLS) for s, (r, c, _) in zip(shards, group[1])], group)


def _unpack_half(packed, group):
    return [p.reshape(r // 2, c) for p, (r, c, _) in zip(_split_slots(packed[None], group), group[1])]


def _unpack_full(gathered, group):
    out = []
    for p, (r, c, axis) in zip(_split_slots(gathered, group), group[1]):
        shards = p.reshape(4, r, c)
        out.append(shards.reshape(4 * r, c) if axis == 0 else jnp.moveaxis(shards, 0, 1).reshape(r, 4 * c))
    return out


def _pack_full(grads, group):
    parts = []
    for gr, (r, c, axis) in zip(grads, group[1]):
        shards = gr.reshape(4, r, c) if axis == 0 else jnp.moveaxis(gr.reshape(r, 4, c), 1, 0)
        parts.append(shards.reshape(8, _pack_rows(r, c), PACK_COLS))
    return _join_slots(parts, group)


def _pad_w_in(w_in):
    z = jnp.zeros((D_MODEL, 1), w_in.dtype)
    return jnp.concatenate([w_in[:, :2176], jnp.tile(z, (1, 64)), w_in[:, 2176:], jnp.tile(z, (1, 32))], axis=1)


def _unpad_w_in(g):
    return jnp.concatenate([g[:, :2176], g[:, 2240:2272]], axis=1)


def _pad_heads(w, used):
    k = w.shape[0]
    w3 = w.reshape(k, HEADS, used)
    return jnp.pad(w3, ((0, 0), (0, 0), (0, LANES - used))).reshape(k, HEADS * LANES)


def _unpad_heads(g, used):
    k = g.shape[0]
    return g.reshape(k, HEADS, LANES)[:, :, :used].reshape(k, HEADS * used)


def _rope_tables(seq):
    inv_freq = 1.0 / (ROPE_BASE ** (jnp.arange(0, ROPE, 2, dtype=F32) / ROPE))
    ang = jnp.arange(seq, dtype=F32)[:, None] * inv_freq[None, :]
    cos, sin = jnp.cos(ang), jnp.sin(ang)
    one, zero = jnp.ones((seq, NOPE), F32), jnp.zeros((seq, NOPE), F32)
    z16, z32 = jnp.zeros((seq, 16), F32), jnp.zeros((seq, 32), F32)
    cos_t = jnp.concatenate([one, cos, cos, jnp.ones((seq, 32), F32)], axis=1)
    sin_a = jnp.concatenate([zero, -sin, z16, z32], axis=1)
    sin_b = jnp.concatenate([zero, z16, sin, z32], axis=1)
    return cos_t, sin_a, sin_b


SMALL = (("ln_in_g", 1024), ("ln_in_b", 1024), ("b_ada", 6144), ("q_norm_g", 384), ("kv_norm_g", 256),
         ("ln1_g", 1024), ("ln1_b", 1024), ("ln2_g", 1024), ("ln2_b", 1024))
SUBLANES = 8
SMALL_SLOTS = [-(-n // LANES // SUBLANES) * SUBLANES for _, n in SMALL]
SMALL_AT = [sum(SMALL_SLOTS[:p]) for p in range(len(SMALL))]
SMALL_ROWS = sum(SMALL_SLOTS)


def _pack_small(vals):
    parts = []
    for v, slot in zip(vals, SMALL_SLOTS):
        rows = v.reshape(-1, LANES)
        parts.append(jnp.pad(rows, ((0, slot - rows.shape[0]), (0, 0))))
    return jnp.concatenate(parts, axis=0)


def kernel(x, c, ln_in_g, ln_in_b, w_ada, b_ada, w_in, q_norm_g, kv_norm_g, w_uq, w_ukv, w_o, ln1_g, ln1_b, w_up, w_down, ln2_g, ln2_b, loss_target, m_ln_in_g, m_ln_in_b, m_w_ada, m_b_ada, m_w_in, m_q_norm_g, m_kv_norm_g, m_w_uq, m_w_ukv, m_w_o, m_ln1_g, m_ln1_b, m_w_up, m_w_down, m_ln2_g, m_ln2_b, v_ln_in_g, v_ln_in_b, v_w_ada, v_b_ada, v_w_in, v_q_norm_g, v_kv_norm_g, v_w_uq, v_w_ukv, v_w_o, v_ln1_g, v_ln1_b, v_w_up, v_w_down, v_ln2_g, v_ln2_b):
    nb, seq, _ = x.shape
    rows = nb * seq
    ix, iy, ic = lax.axis_index("x"), lax.axis_index("y"), lax.axis_index("c")
    chip = 2 * ix + iy
    dev = 2 * chip + ic

    def my_half(shards, group):
        packed = _pack_halves([s.astype(BF16) for s in shards], group)
        return lax.dynamic_index_in_dim(packed, ic, 0, keepdims=False)

    f_in, f_uq, f_ukv = _unpack_full(_gather8(my_half([w_in[0], w_uq[0], w_ukv[0]], GROUP_IN), "gather_w_in"),
                                     GROUP_IN)
    half_mlp = my_half([w_up[0], w_down[0], w_o[0]], GROUP_MLP)
    late_weights = _gather_exchange(half_mlp)
    w_in_p = _pad_w_in(f_in)
    uq3 = f_uq.reshape(Q_RANK, HEADS, NOPE + ROPE)
    w_uq_p = jnp.pad(uq3, ((0, 0), (0, 0), (0, LANES - NOPE - ROPE))).reshape(Q_RANK, HEADS * LANES)
    w_ukv_p = jnp.concatenate([_pad_heads(f_ukv[:, :HEADS * NOPE], NOPE), f_ukv[:, HEADS * NOPE:]], axis=1)

    n_all = 8 * nb
    c_all = _gather8(c.reshape(-1, LANES), "gather_c").reshape(n_all, D_MODEL)
    ada_cols = w_ada.shape[2]
    b_sh = lax.dynamic_slice_in_dim(b_ada, chip * ada_cols, ada_cols, axis=1)
    mod_sh = _ada_fwd(c_all, w_ada[0], b_sh)
    mod_g = _gather8(mod_sh, "gather_mod")[0::2]
    mod_all = jnp.moveaxis(mod_g, 0, 1).reshape(n_all, N_MOD * D_MODEL)
    mod_mine = lax.dynamic_slice_in_dim(mod_all, dev * nb, nb, axis=0).reshape(nb, N_MOD, D_MODEL)
    mod = jnp.pad(mod_mine, ((0, 0), (0, 8 - N_MOD), (0, 0)))

    cos_t, sin_a, sin_b = _rope_tables(seq)
    row2 = lambda v: v.reshape(1, -1)

    x2d = x.reshape(rows, D_MODEL)
    x0, h, qkv, lat, qp, kp, vm = _fwd_in(x2d, mod, row2(ln_in_g), row2(ln_in_b), w_in_p, q_norm_g, kv_norm_g,
                                          w_uq_p, w_ukv_p, cos_t, sin_a, sin_b, seq)
    sb_y, cars, g_mlp = _sb_fwd(qkv, seq, late_weights)
    g_mlp = _with_own(g_mlp, half_mlp)
    f_o = _split_slots(g_mlp, GROUP_MLP)[2].reshape(D_MODEL, D_MODEL)
    mla_y, lse = _mla_fwd(qp, kp, vm, seq)
    mix, y1, h2, u, ff, y2 = _fwd_out(sb_y, mla_y, x0, mod, f_o, ln1_g, ln1_b, g_mlp, seq)

    dy1, dmix, d_attn, dff, du, acc_out, dmod_a = _bwd_out(
        y2, loss_target.reshape(rows, D_MODEL), ff, u, y1, mix, mod, ln2_g, ln2_b, ln1_g, ln1_b, g_mlp, f_o, seq)
    c_idx = ic.reshape(1).astype(jnp.int32)
    blocks_mlp = _wgrad_packed(h2, du, "wgrad_up", lambda i, j: 2 * j + i, 0)
    blocks_mlp = _wgrad_packed(u, dff, "wgrad_down", lambda i, j: i, 1, pre="relu2", into=blocks_mlp)
    blocks_mlp = _wgrad_packed(sb_y, dmix, "wgrad_o_sb", lambda i, j: 0, 8, split=4, into=blocks_mlp)
    blocks_mlp = _wgrad_packed(mla_y, dmix, "wgrad_o_mla", lambda i, j: 1, 8, split=4, into=blocks_mlp)
    dq_sb, dk_sb, dv_sb, sibling_mlp = _sb_bwd(qkv, d_attn, cars, seq, _swap_cores_exchange(blocks_mlp))
    part_mlp, part_mlp_bf = _add_pairs(blocks_mlp, sibling_mlp, c_idx, "grad_add_cores_mlp")
    dqp, dkp, dvm, chips_mlp = _mla_bwd(qp, kp, vm, d_attn, mla_y, lse, seq, _scatter_chips_exchange(part_mlp_bf))
    grad_x, dproj, dqall, dkv, latn, acc0, acc_lat, dmod_c = _bwd_in(
        dqp, dkp, dvm, dq_sb, dk_sb, dv_sb, lat, x2d, dy1, mod, row2(ln_in_g), row2(ln_in_b), w_in_p,
        q_norm_g, kv_norm_g, w_uq_p, w_ukv_p, cos_t, sin_a, sin_b, seq)

    dmod = (dmod_a + dmod_c)[:, :N_MOD, :]
    small_part = _pack_small([acc0[0], acc0[1], jnp.zeros((N_MOD * D_MODEL,), F32), acc_lat[0, :Q_RANK],
                              acc_lat[1, :KV_RANK], acc_out[3], acc_out[4], acc_out[0], acc_out[1]])
    n_sum = SMALL_ROWS + D_MODEL // LANES
    payload = jnp.concatenate([small_part, acc_out[2].reshape(-1, LANES), dmod.reshape(-1, LANES)], axis=0)
    g_in_p, gathered = _wgrad(h, dproj, "wgrad_in", tn=768, ex=_gather_exchange(payload))
    gathered = _with_own(gathered, payload)
    g_in = _unpad_w_in(g_in_p)
    g_uq = _unpad_heads(_wgrad(latn[:, :Q_RANK], dqall, "wgrad_uq"), NOPE + ROPE)
    g_ukv_p = _wgrad(latn[:, Q_RANK:], dkv, "wgrad_ukv", tn=512)
    g_ukv = jnp.concatenate([_unpad_heads(g_ukv_p[:, :HEADS * LANES], NOPE), g_ukv_p[:, HEADS * LANES:]], axis=1)
    blocks_in = _pack_full([g_in, g_uq, g_ukv], GROUP_IN)
    sibling_in = _run_exchange(_swap_cores_exchange(blocks_in), "grads_in_to_sibling")
    part_in, part_in_bf = _add_pairs(blocks_in, sibling_in, c_idx, "grad_add_cores_in")
    chips_in = _run_exchange(_scatter_chips_exchange(part_in_bf), "grads_in_to_chips")

    def own(part):
        return lax.dynamic_index_in_dim(part, chip, 0, keepdims=False)

    half = jnp.concatenate([_add_chips(own(part_in), chips_in, "grad_add_chips_in"),
                            _add_chips(own(part_mlp), chips_mlp, "grad_add_chips_mlp")], axis=0)
    small_sum = _sum_lead(gathered[:, :n_sum, :], "sum_small")
    loss = jnp.sum(small_sum[SMALL_ROWS:])
    dmod_all = gathered[:, n_sum:, :].reshape(n_all, N_MOD * D_MODEL)
    g_b_ada = _sum_lead(dmod_all.reshape(n_all, N_MOD * D_MODEL // LANES, LANES), "sum_b_ada")
    dmod_sh = lax.dynamic_slice_in_dim(dmod_all, chip * ada_cols, ada_cols, axis=1)
    g_w_ada = _ada_bwd(c_all, dmod_sh)

    d_ada, m_ada, v_ada, other = _adamw(w_ada[0], g_w_ada, m_w_ada[0], v_w_ada[0], "adamw_w_ada",
                                       _swap_one_exchange(half))
    mine = _unpack_half(half[:GROUP_IN[0]], GROUP_IN) + _unpack_half(half[GROUP_IN[0]:], GROUP_MLP)
    theirs = _unpack_half(other[:GROUP_IN[0]], GROUP_IN) + _unpack_half(other[GROUP_IN[0]:], GROUP_MLP)
    res = {"w_ada": (g_w_ada[None], d_ada[None], m_ada[None], v_ada[None])}
    sharded = {"w_in": (w_in, m_w_in, v_w_in), "w_uq": (w_uq, m_w_uq, v_w_uq), "w_ukv": (w_ukv, m_w_ukv, v_w_ukv),
               "w_up": (w_up, m_w_up, v_w_up), "w_down": (w_down, m_w_down, v_w_down), "w_o": (w_o, m_w_o, v_w_o)}
    for (name, (w, m, v)), g_mine, g_other in zip(sharded.items(), mine, theirs):
        quad = _adamw_halves(w[0], g_mine, g_other, m[0], v[0], c_idx, "adamw_" + name)
        res[name] = tuple(a[None] for a in quad)
    small_w = [ln_in_g, ln_in_b, b_ada, q_norm_g, kv_norm_g, ln1_g, ln1_b, ln2_g, ln2_b]
    small_m = [m_ln_in_g, m_ln_in_b, m_b_ada, m_q_norm_g, m_kv_norm_g, m_ln1_g, m_ln1_b, m_ln2_g, m_ln2_b]
    small_v = [v_ln_in_g, v_ln_in_b, v_b_ada, v_q_norm_g, v_kv_norm_g, v_ln1_g, v_ln1_b, v_ln2_g, v_ln2_b]
    for (name, _), quad in zip(SMALL, _adamw_small(small_sum, g_b_ada, small_w, small_m, small_v)):
        res[name] = quad

    order = ["ln_in_g", "ln_in_b", "w_ada", "b_ada", "w_in", "q_norm_g", "kv_norm_g", "w_uq", "w_ukv", "w_o",
             "ln1_g", "ln1_b", "w_up", "w_down", "ln2_g", "ln2_b"]
    outs = [loss, grad_x.reshape(nb, seq, D_MODEL)]
    for k in range(4):
        outs += [res[name][k] for name in order]
    return tuple(outs)
```

```python
import math

import jax
import jax.numpy as jnp
from jax import lax
from jax.experimental import pallas as pl
from jax.experimental.pallas import tpu as pltpu

F32 = jnp.float32
BF16 = jnp.bfloat16
MESH_IDS = pl.DeviceIdType.MESH

D_MODEL = 1024
HEADS = 8
SB_W = 512
MLA_W = 512
NOPE = 64
ROPE = 32
Q_RANK = 384
KV_RANK = 256
D_IN_PAD = 2304
D_FF = 4096
N_MOD = 6
LN_EPS = 1e-5
RMS_EPS = 1e-6
ALPHA = 2.0 ** 0.25
ROPE_BASE = 10000.0
SB_SCALE = 64 ** -0.5
NEG_LOG2E = -math.log2(math.e)
MLA_SCALE = 96 ** -0.5
ADAM_LR = 0.001
ADAM_B1 = 0.9
ADAM_B2 = 0.999
ADAM_EPS = 1e-08
ADAM_WD = 0.01
ADAM_STEP = 10

LANES = 128
ROW_TILE = 256
ATTN_TILE = 256
CAR_SLOTS = 8
ATTN_PAIRS = 4
VMEM_LIMIT = 56 << 20

NT = (((1,), (1,)), ((), ()))
TN = (((0,), (0,)), ((), ()))


def _params(sem=None):
    return pltpu.CompilerParams(vmem_limit_bytes=VMEM_LIMIT, dimension_semantics=sem)


def _const_spec(shape):
    zeros = (0,) * len(shape)
    return pl.BlockSpec(shape, lambda *_: zeros, pipeline_mode=pl.Buffered(1))


def _dot(a, b, dims=None):
    if dims is None:
        return jnp.dot(a, b, preferred_element_type=F32)
    return lax.dot_general(a, b, dims, preferred_element_type=F32)


def _mean(v):
    return jnp.mean(v, axis=-1, keepdims=True)


def _rowsum(v):
    return jnp.sum(v, axis=0, keepdims=True)


def _ln_fwd(y, g, b):
    mu = _mean(y)
    yc = y - mu
    rstd = lax.rsqrt(_mean(yc * yc) + LN_EPS)
    xhat = yc * rstd
    return xhat * g + b, xhat, rstd


def _ln_bwd(dx, xhat, rstd, g):
    dxh = dx * g
    return rstd * (dxh - _mean(dxh) - xhat * _mean(dxh * xhat))


def _rope(v, cos, sin_a, sin_b):
    return v * cos + pltpu.roll(v, 112, 1) * sin_a + pltpu.roll(v, 16, 1) * sin_b


def _rope_t(dv, cos, sin_a, sin_b):
    return dv * cos + pltpu.roll(dv * sin_a, 16, 1) + pltpu.roll(dv * sin_b, 112, 1)


def _my_place():
    return lax.axis_index("x"), lax.axis_index("y"), lax.axis_index("c")


class _Exchange:
    def __init__(self, operand, out_shape, n_copies, phases):
        self.operand = operand
        self.out_shape = out_shape
        self.phases = phases
        self.scratch = [pltpu.SemaphoreType.DMA((n_copies,)), pltpu.SemaphoreType.DMA((n_copies,))]


def _run_exchange(ex, name):
    def body(in_ref, out_ref, send_sems, recv_sems):
        for phase in ex.phases(in_ref, out_ref, send_sems, recv_sems):
            phase()

    return pl.pallas_call(
        body, name=name, out_shape=ex.out_shape,
        in_specs=[pl.BlockSpec(memory_space=pl.ANY)], out_specs=pl.BlockSpec(memory_space=pl.ANY),
        scratch_shapes=ex.scratch,
    )(ex.operand)


def _nothing():
    pass


def _gather_exchange(v):
    m, n = v.shape

    def phases(v_ref, out_ref, send_sems, recv_sems):
        x, y, c = _my_place()
        me, sibling = (x, y, c), (x, y, 1 - c)
        chips = [(1 - x, y), (x, 1 - y), (1 - x, 1 - y)]

        def rows(px, py, pc):
            return out_ref.at[4 * px + 2 * py + pc]

        def copy(k, block, to, src=None):
            return pltpu.make_async_remote_copy(
                src_ref=rows(*block) if src is None else src, dst_ref=rows(*block),
                send_sem=send_sems.at[k], recv_sem=recv_sems.at[k], device_id=to, device_id_type=MESH_IDS)

        first = [copy(0, me, sibling, src=v_ref)]
        first += [copy(1 + j, me, (*chip, c), src=v_ref) for j, chip in enumerate(chips)]
        passed = [copy(4 + j, (*chip, c), sibling) for j, chip in enumerate(chips)]

        def start():
            for cp in first:
                cp.start()

        def middle():
            for j, chip in enumerate(chips):
                copy(1 + j, (*chip, c), me).wait_recv()
                passed[j].start()

        def finish():
            copy(0, sibling, me).wait_recv()
            for j, chip in enumerate(chips):
                copy(4 + j, (*chip, 1 - c), me).wait_recv()
            for cp in first + passed:
                cp.wait_send()

        return start, middle, finish

    return _Exchange(v, jax.ShapeDtypeStruct((8, m, n), v.dtype), 7, phases)


def _with_own(gathered, v):
    dev = 4 * lax.axis_index("x") + 2 * lax.axis_index("y") + lax.axis_index("c")
    return lax.dynamic_update_index_in_dim(gathered, v, dev, 0)


def _direct_exchange(operand, out_shape, n_copies, make_copies):
    def phases(in_ref, out_ref, send_sems, recv_sems):
        copies = make_copies(in_ref, out_ref, send_sems, recv_sems)

        def start():
            for cp in copies:
                cp.start()

        def finish():
            for cp in copies:
                cp.wait()

        return start, _nothing, finish

    return _Exchange(operand, out_shape, n_copies, phases)


def _swap_cores_exchange(blocks):
    _, m, n = blocks.shape

    def make_copies(g_ref, out_ref, send_sems, recv_sems):
        x, y, c = _my_place()
        return [pltpu.make_async_remote_copy(
            src_ref=g_ref.at[2 * j + (1 - c)], dst_ref=out_ref.at[j],
            send_sem=send_sems.at[j], recv_sem=recv_sems.at[j],
            device_id=(x, y, 1 - c), device_id_type=MESH_IDS) for j in range(4)]

    return _direct_exchange(blocks, jax.ShapeDtypeStruct((4, m, n), blocks.dtype), 4, make_copies)


def _scatter_chips_exchange(parts):
    _, m, n = parts.shape
    flips = [(1, 0), (0, 1), (1, 1)]

    def make_copies(p_ref, out_ref, send_sems, recv_sems):
        x, y, c = _my_place()
        copies = []
        for k, (fx, fy) in enumerate(flips):
            tx = 1 - x if fx else x
            ty = 1 - y if fy else y
            copies.append(pltpu.make_async_remote_copy(
                src_ref=p_ref.at[2 * tx + ty], dst_ref=out_ref.at[k],
                send_sem=send_sems.at[k], recv_sem=recv_sems.at[k],
                device_id=(tx, ty, c), device_id_type=MESH_IDS))
        return copies

    return _direct_exchange(parts, jax.ShapeDtypeStruct((3, m, n), parts.dtype), 3, make_copies)


def _swap_one_exchange(v):
    def make_copies(v_ref, out_ref, send_sems, recv_sems):
        x, y, c = _my_place()
        return [pltpu.make_async_remote_copy(src_ref=v_ref, dst_ref=out_ref, send_sem=send_sems.at[0],
                                             recv_sem=recv_sems.at[0], device_id=(x, y, 1 - c),
                                             device_id_type=MESH_IDS)]

    return _direct_exchange(v, jax.ShapeDtypeStruct(v.shape, v.dtype), 1, make_copies)


def _gather8(v, name):
    return _with_own(_run_exchange(_gather_exchange(v), name), v)


def _carried(ex, refs, n_in, n_out, n_scratch):
    ins, ex_in = refs[:n_in], refs[n_in]
    outs, ex_out = refs[n_in + 1:n_in + 1 + n_out], refs[n_in + 1 + n_out]
    at = n_in + 2 + n_out
    return ins, outs + refs[at:at + n_scratch], ex.phases(ex_in, ex_out, *refs[at + n_scratch:])


def _ada_fwd(c_all, w_ada_sh, b_ada_sh):
    nb, cols = c_all.shape[0], w_ada_sh.shape[1]
    tn = 512

    def body(c_ref, w_ref, b_ref, o_ref):
        cv = c_ref[...]
        act = (cv * jax.nn.sigmoid(cv)).astype(BF16)
        o_ref[...] = _dot(act, w_ref[...].astype(BF16)) + b_ref[...]

    return pl.pallas_call(
        body, name="ada_fwd", grid=(cols // tn,),
        out_shape=jax.ShapeDtypeStruct((nb, cols), F32),
        in_specs=[pl.BlockSpec((nb, D_MODEL), lambda j: (0, 0)),
                  pl.BlockSpec((D_MODEL, tn), lambda j: (0, j)),
                  pl.BlockSpec((1, tn), lambda j: (0, j))],
        out_specs=pl.BlockSpec((nb, tn), lambda j: (0, j)),
        compiler_params=_params(("arbitrary",)),
    )(c_all, w_ada_sh, b_ada_sh)


def _ada_bwd(c_all, dmod_sh):
    nb, cols = dmod_sh.shape
    tn = 512

    def body(c_ref, d_ref, o_ref):
        cv = c_ref[...]
        act = (cv * jax.nn.sigmoid(cv)).astype(BF16)
        o_ref[...] = _dot(act, d_ref[...].astype(BF16), TN)

    return pl.pallas_call(
        body, name="ada_bwd", grid=(cols // tn,),
        out_shape=jax.ShapeDtypeStruct((D_MODEL, cols), F32),
        in_specs=[pl.BlockSpec((nb, D_MODEL), lambda j: (0, 0)),
                  pl.BlockSpec((nb, tn), lambda j: (0, j))],
        out_specs=pl.BlockSpec((D_MODEL, tn), lambda j: (0, j)),
        compiler_params=_params(("arbitrary",)),
    )(c_all, dmod_sh)


def _sum_lead(v, name):
    k, m, n = v.shape

    def body(v_ref, o_ref):
        acc = v_ref[0]
        for i in range(1, k):
            acc = acc + v_ref[i]
        o_ref[...] = acc

    return pl.pallas_call(
        body, name=name, out_shape=jax.ShapeDtypeStruct((m, n), F32),
        in_specs=[pl.BlockSpec((k, m, n), lambda: (0, 0, 0))],
        out_specs=pl.BlockSpec((m, n), lambda: (0, 0)),
        compiler_params=_params(),
    )(v)


def _adamw_math(w, g, m, v):
    mn = ADAM_B1 * m + (1.0 - ADAM_B1) * g
    vn = ADAM_B2 * v + (1.0 - ADAM_B2) * (g * g)
    m_hat = mn / (1.0 - ADAM_B1 ** ADAM_STEP)
    v_hat = vn / (1.0 - ADAM_B2 ** ADAM_STEP)
    return -ADAM_LR * (m_hat / (jnp.sqrt(v_hat) + ADAM_EPS) + ADAM_WD * w), mn, vn


def _adamw_small(g_sum, g_b_ada, ws, ms, vs):
    n = len(SMALL)

    def body(gs_ref, gb_ref, *refs):
        outs = refs[3 * n:]
        for p in range(n):
            rows_p = SMALL[p][1] // LANES
            g = gb_ref[...] if SMALL[p][0] == "b_ada" else gs_ref[SMALL_AT[p]:SMALL_AT[p] + rows_p, :]
            d, mn, vn = _adamw_math(refs[p][...], g, refs[n + p][...], refs[2 * n + p][...])
            outs[4 * p][...] = g
            outs[4 * p + 1][...] = d
            outs[4 * p + 2][...] = mn
            outs[4 * p + 3][...] = vn

    shapes = [jax.ShapeDtypeStruct((size // LANES, LANES), F32) for _, size in SMALL for _ in range(4)]
    flat = lambda arrs: [a.reshape(-1, LANES) for a in arrs]
    res = pl.pallas_call(body, name="adamw_small", out_shape=tuple(shapes), compiler_params=_params())(
        g_sum, g_b_ada, *flat(ws), *flat(ms), *flat(vs))
    return [tuple(r.reshape(w.shape) for r in res[4 * p:4 * p + 4]) for p, w in enumerate(ws)]


def _adamw_halves(w, g_mine, g_other, m, v, c_idx, name):
    r, cols = w.shape
    half = r // 2
    tr = half
    while tr * cols * 4 > (2 << 20) and tr % 16 == 0:
        tr //= 2

    def body(c_ref, w_ref, mine_ref, other_ref, m_ref, v_ref, g_ref, d_ref, mo_ref, vo_ref):
        g = jnp.where(pl.program_id(0) == c_ref[0], mine_ref[...], other_ref[...])
        g_ref[0] = g
        d_ref[0], mo_ref[0], vo_ref[0] = _adamw_math(w_ref[0], g, m_ref[0], v_ref[0])

    full = pl.BlockSpec((1, tr, cols), lambda h, i, c: (h, i, 0))
    part = pl.BlockSpec((tr, cols), lambda h, i, c: (i, 0))
    shape = jax.ShapeDtypeStruct((2, half, cols), F32)
    grid_spec = pltpu.PrefetchScalarGridSpec(
        num_scalar_prefetch=1, grid=(2, half // tr),
        in_specs=[full, part, part, full, full], out_specs=(full, full, full, full))
    split = lambda a: a.reshape(2, half, cols)
    res = pl.pallas_call(
        body, name=name, grid_spec=grid_spec, out_shape=(shape, shape, shape, shape),
        compiler_params=_params(("arbitrary", "arbitrary")),
    )(c_idx, split(w), g_mine, g_other, split(m), split(v))
    return tuple(a.reshape(r, cols) for a in res)


def _adamw(w, g, m, v, name):
    rows, cols = w.shape
    tr = rows
    while tr * cols * 4 > (2 << 20) and tr % 16 == 0:
        tr //= 2

    def body(w_ref, g_ref, m_ref, v_ref, d_ref, mo_ref, vo_ref):
        d_ref[...], mo_ref[...], vo_ref[...] = _adamw_math(w_ref[...], g_ref[...], m_ref[...], v_ref[...])

    spec = pl.BlockSpec((tr, cols), lambda i: (i, 0))
    shape = jax.ShapeDtypeStruct((rows, cols), F32)
    return pl.pallas_call(
        body, name=name, grid=(rows // tr,), out_shape=(shape, shape, shape),
        in_specs=[spec, spec, spec, spec], out_specs=(spec, spec, spec),
        compiler_params=_params(("arbitrary",)),
    )(w, g, m, v)


def _add_rows(m, n):
    fits = [d for d in range(16, m + 1, 16) if m % d == 0 and d * n * 4 <= (5 << 19)]
    assert fits, (m, n)
    return max(fits)


def _add_pairs(blocks, recv, c_idx, name):
    _, m, n = blocks.shape
    tr = _add_rows(m, n)

    def body(c_ref, a_ref, b_ref, o_ref, ob_ref):
        s = a_ref[...] + b_ref[...]
        o_ref[...] = s
        ob_ref[...] = s.astype(BF16)

    grid_spec = pltpu.PrefetchScalarGridSpec(
        num_scalar_prefetch=1, grid=(4, m // tr),
        in_specs=[pl.BlockSpec((1, tr, n), lambda j, i, c: (2 * j + c[0], i, 0)),
                  pl.BlockSpec((1, tr, n), lambda j, i, c: (j, i, 0))],
        out_specs=(pl.BlockSpec((1, tr, n), lambda j, i, c: (j, i, 0)),
                   pl.BlockSpec((1, tr, n), lambda j, i, c: (j, i, 0))))
    return pl.pallas_call(
        body, name=name, grid_spec=grid_spec,
        out_shape=(jax.ShapeDtypeStruct((4, m, n), F32), jax.ShapeDtypeStruct((4, m, n), BF16)),
        compiler_params=_params(("arbitrary", "arbitrary")),
    )(c_idx, blocks, recv)


def _add_chips(own, recv, name):
    m, n = own.shape
    tr = _add_rows(m, n)

    def body(a_ref, r_ref, o_ref):
        acc = a_ref[...]
        for k in range(3):
            acc = acc + r_ref[k].astype(F32)
        o_ref[...] = acc

    return pl.pallas_call(
        body, name=name, grid=(m // tr,),
        out_shape=jax.ShapeDtypeStruct((m, n), F32),
        in_specs=[pl.BlockSpec((tr, n), lambda i: (i, 0)), pl.BlockSpec((3, tr, n), lambda i: (0, i, 0))],
        out_specs=pl.BlockSpec((tr, n), lambda i: (i, 0)),
        compiler_params=_params(("arbitrary",)),
    )(own, recv)


def _row_spec(cols):
    return pl.BlockSpec((ROW_TILE, cols), lambda i: (i, 0))


def _mod_spec(tiles_per_seq):
    return pl.BlockSpec((1, 8, D_MODEL), lambda i: (i // tiles_per_seq, 0, 0))


def _table_spec(tiles_per_seq):
    return pl.BlockSpec((ROW_TILE, LANES), lambda i: (i % tiles_per_seq, 0))


def _fwd_in(x, mod, ln_g, ln_b, w_in, q_g, kv_g, w_uq, w_ukv, cos_t, sin_a, sin_b, seq):
    rows = x.shape[0]
    tm = min(2 * ROW_TILE, seq)
    tps = seq // tm

    def body(x_ref, mod_ref, g_ref, b_ref, win_ref, qg_ref, kvg_ref, wuq_ref, wukv_ref, cos_ref, sa_ref, sb_ref,
             x0_ref, h_ref, qkv_ref, lat_ref, qp_ref, kp_ref, vm_ref):
        def chain(rs):
            x0, _, _ = _ln_fwd(x_ref[rs, :], g_ref[...], b_ref[...])
            x0_ref[rs, :] = x0
            h = (x0 * (1.0 + mod_ref[0, 1:2, :]) + mod_ref[0, 0:1, :]).astype(BF16)
            h_ref[rs, :] = h
            yield
            proj = _dot(h, win_ref[...])
            yield
            qkv_ref[rs, :SB_W] = (proj[:, :SB_W] * SB_SCALE).astype(BF16)
            qkv_ref[rs, SB_W:] = proj[:, SB_W:3 * SB_W].astype(BF16)
            lat_ref[rs, :] = proj[:, 3 * SB_W:3 * SB_W + Q_RANK + KV_RANK]
            cq = proj[:, 3 * SB_W:3 * SB_W + Q_RANK]
            ckv = proj[:, 3 * SB_W + Q_RANK:3 * SB_W + Q_RANK + KV_RANK]
            kr = proj[:, D_IN_PAD - LANES:]
            cos, sa, sb = cos_ref[rs, :], sa_ref[rs, :], sb_ref[rs, :]
            cqn = (cq * lax.rsqrt(_mean(cq * cq) + RMS_EPS) * qg_ref[...]).astype(BF16)
            q_all = _dot(cqn, wuq_ref[...])
            ckvn = (ckv * lax.rsqrt(_mean(ckv * ckv) + RMS_EPS) * kvg_ref[...]).astype(BF16)
            kv = _dot(ckvn, wukv_ref[...])
            yield
            for hd in range(HEADS):
                sl = slice(hd * LANES, (hd + 1) * LANES)
                qp_ref[rs, sl] = _rope(q_all[:, sl], cos, sa, sb).astype(BF16)
            kr_rot = _rope(kr, cos, sa, sb)
            for hd in range(HEADS):
                sl = slice(hd * LANES, (hd + 1) * LANES)
                kp_ref[rs, sl] = (kv[:, sl] + kr_rot).astype(BF16)
            vm_ref[rs, :] = kv[:, HEADS * LANES:].astype(BF16)

        half = tm // 2
        _staggered([chain(slice(0, half)), chain(slice(half, tm))])

    row_spec = lambda cols: pl.BlockSpec((tm, cols), lambda i: (i, 0))
    table_spec = pl.BlockSpec((tm, LANES), lambda i: (i % tps, 0))
    outs = [(D_MODEL, F32), (D_MODEL, BF16), (3 * SB_W, BF16), (Q_RANK + KV_RANK, F32),
            (HEADS * LANES, BF16), (HEADS * LANES, BF16), (MLA_W, BF16)]
    return pl.pallas_call(
        body, name="fwd_in", grid=(rows // tm,),
        out_shape=tuple(jax.ShapeDtypeStruct((rows, n), dt) for n, dt in outs),
        in_specs=[row_spec(D_MODEL), pl.BlockSpec((1, 8, D_MODEL), lambda i: (i // tps, 0, 0)),
                  _const_spec((1, D_MODEL)), _const_spec((1, D_MODEL)),
                  _const_spec(w_in.shape), _const_spec((1, Q_RANK)), _const_spec((1, KV_RANK)),
                  _const_spec(w_uq.shape), _const_spec(w_ukv.shape), table_spec, table_spec, table_spec],
        out_specs=tuple(row_spec(n) for n, _ in outs),
        compiler_params=_params(("arbitrary",)),
    )(x, mod, ln_g, ln_b, w_in, q_g, kv_g, w_uq, w_ukv, cos_t, sin_a, sin_b)


HALF = 512
SHARD = 1024


def _mlp_weight_specs():
    return [pl.BlockSpec((8, HALF, SHARD), lambda i: (0, 0, 0), pipeline_mode=pl.Buffered(1)),
            pl.BlockSpec((8, HALF, SHARD), lambda i: (0, 1, 0), pipeline_mode=pl.Buffered(1))]


def _fwd_out(sb_y, mla_y, x0, mod, w_o, ln_g, ln_b, g_mlp, seq):
    rows = x0.shape[0]
    tm = ROW_TILE
    tps = seq // tm

    def body(sb_ref, ml_ref, x0_ref, mod_ref, wo_ref, g_ref, b_ref, wu_ref, wd_ref,
             mix_ref, y1_ref, h2_ref, u_ref, ff_ref, y2_ref):
        mix = _dot(sb_ref[...], wo_ref[:SB_W, :]) + _dot(ml_ref[...].astype(BF16), wo_ref[SB_W:, :])
        mix_ref[...] = mix
        y1 = ALPHA * x0_ref[...] + (1.0 + mod_ref[0, 2:3, :]) * mix
        y1_ref[...] = y1
        x1, _, _ = _ln_fwd(y1, g_ref[...], b_ref[...])
        h2 = (x1 * (1.0 + mod_ref[0, 4:5, :]) + mod_ref[0, 3:4, :]).astype(BF16)
        h2_ref[...] = h2
        h_lo, h_hi = h2[:, :HALF], h2[:, HALF:]
        ff = jnp.zeros((tm, D_MODEL), F32)
        for chip in range(4):
            u = _dot(h_lo, wu_ref[2 * chip]) + _dot(h_hi, wu_ref[2 * chip + 1])
            u_ref[:, chip * SHARD:(chip + 1) * SHARD] = u.astype(BF16)
            act = jnp.square(jnp.maximum(u, 0.0)).astype(BF16)
            ff = ff + _dot(act[:, :HALF], wd_ref[2 * chip]) + _dot(act[:, HALF:], wd_ref[2 * chip + 1])
        ff_ref[...] = ff
        y2_ref[...] = ALPHA * x1 + (1.0 + mod_ref[0, 5:6, :]) * ff

    outs = [(D_MODEL, F32), (D_MODEL, F32), (D_MODEL, BF16), (D_FF, BF16), (D_MODEL, F32), (D_MODEL, F32)]
    return pl.pallas_call(
        body, name="fwd_out", grid=(rows // tm,),
        out_shape=tuple(jax.ShapeDtypeStruct((rows, n), dt) for n, dt in outs),
        in_specs=[_row_spec(SB_W), _row_spec(MLA_W), _row_spec(D_MODEL), _mod_spec(tps), _const_spec(w_o.shape),
                  _const_spec((1, D_MODEL)), _const_spec((1, D_MODEL))] + _mlp_weight_specs(),
        out_specs=tuple(_row_spec(n) for n, _ in outs),
        compiler_params=_params(("arbitrary",)),
    )(sb_y, mla_y, x0, mod, w_o, ln_g, ln_b, g_mlp, g_mlp)


def _staggered(chains):
    live = []
    for chain in chains:
        live.append(chain)
        live = [c for c in live if next(c, StopIteration) is not StopIteration]
    while live:
        live = [c for c in live if next(c, StopIteration) is not StopIteration]


def _acc_spec(rows=8, cols=D_MODEL):
    return pl.BlockSpec((rows, cols), lambda i: (0, 0))


def _bwd_out(y2, tgt, ff, u, y1, mix, mod, ln2_g, ln2_b, ln1_g, ln1_b, g_mlp, w_o, seq):
    rows = y2.shape[0]
    nb = rows // seq
    tm = ROW_TILE
    tps = seq // tm

    def body(y2_ref, t_ref, ff_ref, u_ref, y1_ref, mix_ref, mod_ref, g2_ref, b2_ref, g_ref, b_ref, wu_ref, wd_ref,
             wo_ref, dy1_ref, dmix_ref, do_ref, dff_ref, du_ref, acc_ref, dmod_ref):
        i = pl.program_id(0)

        @pl.when(i == 0)
        def _():
            acc_ref[...] = jnp.zeros_like(acc_ref)

        @pl.when(i % tps == 0)
        def _():
            dmod_ref[...] = jnp.zeros_like(dmod_ref)

        g2 = g2_ref[...]
        x2, xhat2, rstd2 = _ln_fwd(y2_ref[...], g2, b2_ref[...])
        err = x2 - t_ref[...]
        dx2 = err * (1.0 / D_MODEL)
        acc_ref[0:1, :] += _rowsum(dx2 * xhat2)
        acc_ref[1:2, :] += _rowsum(dx2)
        acc_ref[2:3, :] += _rowsum(err * err) * (0.5 / D_MODEL)
        dy2 = _ln_bwd(dx2, xhat2, rstd2, g2)
        dmod_ref[0, 5:6, :] += _rowsum(dy2 * ff_ref[...])
        dff = ((1.0 + mod_ref[0, 5:6, :]) * dy2).astype(BF16)
        dff_ref[...] = dff
        for blk in range(8):
            cols = slice(blk * HALF, (blk + 1) * HALF)
            da = _dot(dff, wd_ref[blk], NT)
            du_ref[:, cols] = (da * (2.0 * jnp.maximum(u_ref[:, cols].astype(F32), 0.0))).astype(BF16)

        g = g_ref[...]
        x1, xhat, rstd = _ln_fwd(y1_ref[...], g, b_ref[...])
        halves = []
        for half in range(2):
            acc = jnp.zeros((tm, HALF), F32)
            for chip in range(4):
                acc = acc + _dot(du_ref[:, chip * SHARD:(chip + 1) * SHARD], wu_ref[2 * chip + half], NT)
            halves.append(acc)
        dh2 = jnp.concatenate(halves, axis=1)
        dmod_ref[0, 3:4, :] += _rowsum(dh2)
        dmod_ref[0, 4:5, :] += _rowsum(dh2 * x1)
        dx1 = ALPHA * dy2 + dh2 * (1.0 + mod_ref[0, 4:5, :])
        acc_ref[3:4, :] += _rowsum(dx1 * xhat)
        acc_ref[4:5, :] += _rowsum(dx1)
        dy1 = _ln_bwd(dx1, xhat, rstd, g)
        dy1_ref[...] = dy1
        dmod_ref[0, 2:3, :] += _rowsum(dy1 * mix_ref[...])
        dmix = ((1.0 + mod_ref[0, 2:3, :]) * dy1).astype(BF16)
        dmix_ref[...] = dmix
        do_ref[...] = _dot(dmix, wo_ref[...], NT)

    outs = [(D_MODEL, F32), (D_MODEL, BF16), (D_MODEL, F32), (D_MODEL, BF16), (D_FF, BF16)]
    return pl.pallas_call(
        body, name="bwd_out", grid=(rows // tm,),
        out_shape=tuple(jax.ShapeDtypeStruct((rows, n), dt) for n, dt in outs)
        + (jax.ShapeDtypeStruct((8, D_MODEL), F32), jax.ShapeDtypeStruct((nb, 8, D_MODEL), F32)),
        in_specs=[_row_spec(D_MODEL), _row_spec(D_MODEL), _row_spec(D_MODEL), _row_spec(D_FF), _row_spec(D_MODEL),
                  _row_spec(D_MODEL), _mod_spec(tps), _const_spec((1, D_MODEL)), _const_spec((1, D_MODEL)),
                  _const_spec((1, D_MODEL)), _const_spec((1, D_MODEL))] + _mlp_weight_specs()
        + [_const_spec(w_o.shape)],
        out_specs=tuple(_row_spec(n) for n, _ in outs) + (_acc_spec(), _mod_spec(tps)),
        compiler_params=_params(("arbitrary",)),
    )(y2, tgt, ff, u, y1, mix, mod, ln2_g, ln2_b, ln1_g, ln1_b, g_mlp, g_mlp, w_o)


def _bwd_in(dqp, dkp, dvm, dq_sb, dk_sb, dv_sb, lat, x, dy1, mod, ln_g, ln_b, w_in, q_g, kv_g, w_uq, w_ukv,
            cos_t, sin_a, sin_b, seq):
    rows = x.shape[0]
    nb = rows // seq
    tm = min(2 * ROW_TILE, seq)
    tps = seq // tm
    n_lat = Q_RANK + KV_RANK

    def body(dqp_ref, dkp_ref, dvm_ref, dqs_ref, dks_ref, dvs_ref, lat_ref, x_ref, dy1_ref, mod_ref,
             g_ref, b_ref, win_ref, qg_ref, kvg_ref, wuq_ref, wukv_ref, cos_ref, sa_ref, sb_ref,
             dx_ref, dproj_ref, dqall_ref, dkv_ref, latn_ref, acc_ref, accl_ref, dmod_ref):
        i = pl.program_id(0)

        @pl.when(i == 0)
        def _():
            acc_ref[...] = jnp.zeros_like(acc_ref)
            accl_ref[...] = jnp.zeros_like(accl_ref)

        @pl.when(i % tps == 0)
        def _():
            dmod_ref[...] = jnp.zeros_like(dmod_ref)

        def chain(rs):
            n_rows = rs.stop - rs.start
            cos, sa, sb = cos_ref[rs, :], sa_ref[rs, :], sb_ref[rs, :]
            lane = lax.broadcasted_iota(jnp.int32, (n_rows, LANES), 1)
            for hd in range(HEADS):
                sl = slice(hd * LANES, (hd + 1) * LANES)
                dqall_ref[rs, sl] = _rope_t(dqp_ref[rs, sl], cos, sa, sb).astype(BF16)
            dkr = jnp.zeros((n_rows, LANES), F32)
            for hd in range(HEADS):
                sl = slice(hd * LANES, (hd + 1) * LANES)
                dk = dkp_ref[rs, sl]
                dkr = dkr + dk
                dkv_ref[rs, sl] = jnp.where(lane < NOPE, dk, 0.0).astype(BF16)
            dkv_ref[rs, HEADS * LANES:] = dvm_ref[rs, :].astype(BF16)
            dkr = _rope_t(jnp.where(lane >= NOPE, dkr, 0.0), cos, sa, sb)
            yield
            dcqn = _dot(dqall_ref[rs, :], wuq_ref[...], NT)
            dckvn = _dot(dkv_ref[rs, :], wukv_ref[...], NT)
            yield
            cq = lat_ref[rs, :Q_RANK]
            qg = qg_ref[...]
            rq = lax.rsqrt(_mean(cq * cq) + RMS_EPS)
            cqn = cq * rq
            latn_ref[rs, :Q_RANK] = (cqn * qg).astype(BF16)
            accl_ref[0:1, :Q_RANK] += _rowsum(dcqn * cqn)
            dqg = dcqn * qg
            dcq = rq * (dqg - cqn * _mean(dqg * cqn))
            ckv = lat_ref[rs, Q_RANK:]
            kvg = kvg_ref[...]
            rkv = lax.rsqrt(_mean(ckv * ckv) + RMS_EPS)
            ckvn = ckv * rkv
            latn_ref[rs, Q_RANK:] = (ckvn * kvg).astype(BF16)
            accl_ref[1:2, :KV_RANK] += _rowsum(dckvn * ckvn)
            dkg = dckvn * kvg
            dckv = rkv * (dkg - ckvn * _mean(dkg * ckvn))
            dproj_ref[rs, :SB_W] = dqs_ref[rs, :]
            dproj_ref[rs, SB_W:2 * SB_W] = dks_ref[rs, :].astype(BF16)
            dproj_ref[rs, 2 * SB_W:3 * SB_W] = dvs_ref[rs, :].astype(BF16)
            dproj_ref[rs, 3 * SB_W:3 * SB_W + Q_RANK] = dcq.astype(BF16)
            dproj_ref[rs, 3 * SB_W + Q_RANK:3 * SB_W + n_lat] = dckv.astype(BF16)
            dproj_ref[rs, D_IN_PAD - LANES:] = dkr.astype(BF16)
            yield
            dh = _dot(dproj_ref[rs, :], win_ref[...], NT)
            yield
            g = g_ref[...]
            x0, xhat, rstd = _ln_fwd(x_ref[rs, :], g, b_ref[...])
            dmod_ref[0, 0:1, :] += _rowsum(dh)
            dmod_ref[0, 1:2, :] += _rowsum(dh * x0)
            dx0 = ALPHA * dy1_ref[rs, :] + dh * (1.0 + mod_ref[0, 1:2, :])
            acc_ref[0:1, :] += _rowsum(dx0 * xhat)
            acc_ref[1:2, :] += _rowsum(dx0)
            dx_ref[rs, :] = _ln_bwd(dx0, xhat, rstd, g)

        _staggered([chain(slice(0, tm // 2)), chain(slice(tm // 2, tm))])

    row_spec = lambda cols: pl.BlockSpec((tm, cols), lambda i: (i, 0))
    table_spec = pl.BlockSpec((tm, LANES), lambda i: (i % tps, 0))
    mod_spec = pl.BlockSpec((1, 8, D_MODEL), lambda i: (i // tps, 0, 0))
    outs = [(D_MODEL, F32), (D_IN_PAD, BF16), (HEADS * LANES, BF16), (HEADS * LANES + MLA_W, BF16), (n_lat, BF16)]
    return pl.pallas_call(
        body, name="bwd_in", grid=(rows // tm,),
        out_shape=tuple(jax.ShapeDtypeStruct((rows, n), dt) for n, dt in outs)
        + (jax.ShapeDtypeStruct((8, D_MODEL), F32), jax.ShapeDtypeStruct((8, Q_RANK), F32),
           jax.ShapeDtypeStruct((nb, 8, D_MODEL), F32)),
        in_specs=[row_spec(HEADS * LANES), row_spec(HEADS * LANES), row_spec(MLA_W),
                  row_spec(SB_W), row_spec(SB_W), row_spec(SB_W), row_spec(n_lat),
                  row_spec(D_MODEL), row_spec(D_MODEL), mod_spec,
                  _const_spec((1, D_MODEL)), _const_spec((1, D_MODEL)), _const_spec(w_in.shape),
                  _const_spec((1, Q_RANK)), _const_spec((1, KV_RANK)), _const_spec(w_uq.shape),
                  _const_spec(w_ukv.shape), table_spec, table_spec, table_spec],
        out_specs=tuple(row_spec(n) for n, _ in outs) + (_acc_spec(), _acc_spec(8, Q_RANK), mod_spec),
        compiler_params=_params(("arbitrary",)),
    )(dqp, dkp, dvm, dq_sb, dk_sb, dv_sb, lat, x, dy1, mod, ln_g, ln_b, w_in, q_g, kv_g, w_uq, w_ukv,
      cos_t, sin_a, sin_b)


def _wgrad(a, b, name, tm=512, tn=1024, tk=2048, ex=None):
    rows, m = a.shape
    n = b.shape[1]
    tm, tn, tk = min(tm, m), min(tn, n), min(tk, rows)
    if m % tm:
        tm = m
    if n % tn:
        tn = n

    def body(a_ref, b_ref, o_ref):
        @pl.when(pl.program_id(2) == 0)
        def _():
            o_ref[...] = jnp.zeros_like(o_ref)

        o_ref[...] += _dot(a_ref[...].astype(BF16), b_ref[...].astype(BF16), TN)

    res = _carrier_call(
        body, ex, name, (m // tm, n // tn, rows // tk), [a, b],
        [pl.BlockSpec((tk, tm), lambda i, j, k: (k, i)), pl.BlockSpec((tk, tn), lambda i, j, k: (k, j))],
        [jax.ShapeDtypeStruct((m, n), F32)], [pl.BlockSpec((tm, tn), lambda i, j, k: (i, j))])
    return res[0] if ex is None else res


def _wgrad_packed(a, b, name, block_of, row_block, split=1, pre=None, into=None, tk=2048):
    rows, m = a.shape
    n = b.shape[1]
    tm = HALF
    part = tm // split
    tk = min(tk, rows)
    shape = jax.ShapeDtypeStruct((8, GROUP_MLP[0], PACK_COLS), F32)

    def body(a_ref, b_ref, *rest):
        o_ref = rest[-1]

        @pl.when(pl.program_id(2) == 0)
        def _():
            o_ref[...] = jnp.zeros_like(o_ref)

        av = a_ref[...]
        if pre == "relu2":
            av = jnp.square(jnp.maximum(av.astype(F32), 0.0))
        prod = _dot(av.astype(BF16), b_ref[...].astype(BF16), TN)
        for s in range(split):
            o_ref[s] += prod[s * part:(s + 1) * part]

    in_specs = [pl.BlockSpec((tk, tm), lambda i, j, k: (k, i)), pl.BlockSpec((tk, SHARD), lambda i, j, k: (k, j))]
    operands = [a, b]
    if into is not None:
        in_specs.append(pl.BlockSpec(memory_space=pl.ANY))
        operands.append(into)
    return pl.pallas_call(
        body, name=name, grid=(m // tm, n // SHARD, rows // tk), out_shape=shape,
        in_specs=in_specs,
        out_specs=pl.BlockSpec((split, part, SHARD), lambda i, j, k: (block_of(i, j), row_block, 0)),
        input_output_aliases={} if into is None else {2: 0},
        compiler_params=_params(("arbitrary", "arbitrary", "arbitrary")),
    )(*operands)


def _pair(pp):
    return slice(pp * LANES, (pp + 1) * LANES)


def _head_mask(lane, hh):
    return jnp.where((lane >= 64) if hh else (lane < 64), 1.0, 0.0).astype(BF16)


def _tri(t, kind):
    s = lax.broadcasted_iota(jnp.int32, (t, t), 0)
    j = lax.broadcasted_iota(jnp.int32, (t, t), 1)
    one = jnp.where(j > s if kind == "later" else j < s, 1.0, 0.0).astype(BF16)
    return jnp.concatenate([one, one], axis=1)


def _split_dot(tri2, v):
    hi = v.astype(BF16)
    lo = (v - hi.astype(F32)).astype(BF16)
    return _dot(tri2, jnp.concatenate([hi, lo], axis=0))


def _sb_logits(z, valid):
    log_keep = -(jnp.maximum(z, 0.0) + jnp.log(1.0 + jnp.exp2(jnp.abs(z) * NEG_LOG2E)))
    log_beta = z + log_keep
    if valid is not None:
        log_keep = jnp.where(valid, log_keep, 0.0)
    return log_keep, log_beta


def _carrier_call(body, ex, name, grid, operands, in_specs, out_shapes, out_specs, scratch=()):
    n_in, n_out = len(operands), len(out_shapes)
    total = grid[0] * grid[1] * grid[2]
    any_spec = pl.BlockSpec(memory_space=pl.ANY)

    def carrier(*refs):
        ins, outs, (start, middle, finish) = _carried(ex, refs, n_in, n_out, len(scratch))
        step = (pl.program_id(0) * grid[1] + pl.program_id(1)) * grid[2] + pl.program_id(2)
        pl.when(step == 0)(start)
        pl.when(step == total // 2)(middle)
        body(*ins, *outs)
        pl.when(step == total - 1)(finish)

    carried = ex is not None
    return pl.pallas_call(
        carrier if carried else body, name=name, grid=grid,
        out_shape=tuple(out_shapes) + ((ex.out_shape,) if carried else ()),
        in_specs=list(in_specs) + ([any_spec] if carried else []),
        out_specs=tuple(out_specs) + ((any_spec,) if carried else ()),
        scratch_shapes=list(scratch) + (ex.scratch if carried else []),
        compiler_params=_params(("arbitrary", "arbitrary", "arbitrary")),
    )(*operands, *([ex.operand] if carried else []))


def _sb_fwd(qkv, seq, ex=None):
    rows = qkv.shape[0]
    nb = rows // seq
    t = min(ATTN_TILE, seq)
    nq = seq // t
    assert nq <= CAR_SLOTS, (seq, t)
    ap = ATTN_PAIRS
    width = ap * LANES
    groups = SB_W // width
    hds = [(pp, hh) for pp in range(ap) for hh in range(2)]

    def body(q_ref, k_ref, v_ref, tri_ref, o_ref, car_ref, acc_ref):
        i = pl.program_id(2)
        lane = lax.broadcasted_iota(jnp.int32, (t, LANES), 1)
        key = lax.broadcasted_iota(jnp.int32, (t, t), 0)
        qry = lax.broadcasted_iota(jnp.int32, (t, t), 1)
        strict = key < qry
        tri = tri_ref[...]
        masks = [_head_mask(lane, hh) for hh in range(2)]
        qms = [q_ref[:, _pair(pp)] * masks[hh] for pp, hh in hds]
        acc_ref[...] = jnp.zeros_like(acc_ref)
        car_ref[...] = jnp.zeros_like(car_ref)

        def step(kb, c_sums, valid):
            start = pl.multiple_of(kb * t, t)
            kss = [k_ref[pl.ds(start, t), _pair(pp)] for pp in range(ap)]
            vss = [v_ref[pl.ds(start, t), _pair(pp)] for pp in range(ap)]
            zs = [_dot(kss[pp], qms[n], NT) for n, (pp, _) in enumerate(hds)]
            logs = [_sb_logits(z, valid) for z in zs]
            sufs = [_split_dot(tri, lg[0]) for lg in logs]
            new_sums = []
            for n, (pp, hh) in enumerate(hds):
                log_keep, log_beta = logs[n]
                w = jnp.exp(log_beta + sufs[n] + c_sums[n])
                if valid is not None:
                    w = jnp.where(valid, w, 0.0)
                acc_ref[pp] += _dot(vss[pp] * masks[hh], w.astype(BF16), TN)
                car_ref[0, pl.ds(n * CAR_SLOTS + kb, 1), :] = c_sums[n]
                new_sums.append(c_sums[n] + sufs[n][0:1, :] + log_keep[0:1, :])
            return tuple(new_sums)

        c_sums = step(i, tuple(jnp.zeros((1, t), F32) for _ in hds), strict)
        lax.fori_loop(0, i, lambda j, cr: step(i - 1 - j, cr, None), c_sums)
        for pp in range(ap):
            o_ref[:, _pair(pp)] = acc_ref[pp].T.astype(BF16)

    qspec = pl.BlockSpec((t, width), lambda b, p, i: (b * nq + i, p))
    car_rows = len(hds) * CAR_SLOTS
    return _carrier_call(
        body, ex, "sb_fwd", (nb, groups, nq),
        [qkv, qkv, qkv, _tri(t, "later")],
        [qspec,
         pl.BlockSpec((seq, width), lambda b, p, i: (b, groups + p)),
         pl.BlockSpec((seq, width), lambda b, p, i: (b, 2 * groups + p)),
         _const_spec((t, 2 * t))],
        [jax.ShapeDtypeStruct((rows, SB_W), BF16), jax.ShapeDtypeStruct((nb * nq, HEADS * CAR_SLOTS, t), F32)],
        [qspec, pl.BlockSpec((1, car_rows, t), lambda b, p, i: (b * nq + i, p, 0))],
        scratch=[pltpu.VMEM((ap, LANES, t), F32)])


def _sb_bwd(qkv, d_out, cars, seq, ex=None):
    rows = qkv.shape[0]
    nb = rows // seq
    t = min(ATTN_TILE, seq)
    nq = seq // t
    ap = ATTN_PAIRS
    width = ap * LANES
    groups = SB_W // width
    hds = [(pp, hh) for pp in range(ap) for hh in range(2)]

    def body(q_ref, k_ref, v_ref, do_ref, car_ref, tri_ref, pre_ref, dq_ref, dk_ref, dv_ref, dq_acc):
        i = pl.program_id(2)

        @pl.when(i == 0)
        def _():
            dk_ref[...] = jnp.zeros_like(dk_ref)
            dv_ref[...] = jnp.zeros_like(dv_ref)

        lane = lax.broadcasted_iota(jnp.int32, (t, LANES), 1)
        key = lax.broadcasted_iota(jnp.int32, (t, t), 0)
        qry = lax.broadcasted_iota(jnp.int32, (t, t), 1)
        strict = key < qry
        tri, pre = tri_ref[...], pre_ref[...]
        masks = [_head_mask(lane, hh) for hh in range(2)]
        qms = [q_ref[:, _pair(pp)] * masks[hh] for pp, hh in hds]
        doms = [do_ref[:, _pair(pp)].astype(BF16) * masks[hh] for pp, hh in hds]
        dq_acc[...] = jnp.zeros_like(dq_acc)

        def step(kb, g_pres, valid):
            start = pl.multiple_of(kb * t, t)
            kss = [k_ref[pl.ds(start, t), _pair(pp)] for pp in range(ap)]
            vss = [v_ref[pl.ds(start, t), _pair(pp)] for pp in range(ap)]
            zs = [_dot(kss[pp], qms[n], NT) for n, (pp, _) in enumerate(hds)]
            dws = [_dot(vss[pp], doms[n], NT) for n, (pp, _) in enumerate(hds)]
            logs = [_sb_logits(z, valid) for z in zs]
            sufs = [_split_dot(tri, lg[0]) for lg in logs]
            ws, gs = [], []
            for n in range(len(hds)):
                c_sum = car_ref[0, pl.ds(n * CAR_SLOTS + kb, 1), :]
                w = jnp.exp(logs[n][1] + sufs[n] + c_sum)
                if valid is not None:
                    w = jnp.where(valid, w, 0.0)
                ws.append(w)
                gs.append(dws[n] * w)
            pres = [_split_dot(pre, gs[n]) for n in range(len(hds))]
            befores = [g_pres[n] + pres[n] for n in range(len(hds))]
            for pp in range(ap):
                a, b = 2 * pp, 2 * pp + 1
                dv_ref[pl.ds(start, t), _pair(pp)] += _dot(ws[a].astype(BF16), doms[a]) + _dot(ws[b].astype(BF16), doms[b])
            dzbs = []
            for n in range(len(hds)):
                beta = jnp.exp(logs[n][1])
                dz = gs[n] * (1.0 - beta) - beta * befores[n]
                if valid is not None:
                    dz = jnp.where(valid, dz, 0.0)
                dzbs.append(dz.astype(BF16))
            for pp in range(ap):
                a, b = 2 * pp, 2 * pp + 1
                dq_acc[pp] += _dot(dzbs[a], kss[pp] * masks[0], TN) + _dot(dzbs[b], kss[pp] * masks[1], TN)
                dk_ref[pl.ds(start, t), _pair(pp)] += _dot(dzbs[a], qms[a]) + _dot(dzbs[b], qms[b])
            return tuple(g_pres[n] + pres[n][t - 1:t, :] + gs[n][t - 1:t, :] for n in range(len(hds)))

        g_pres = lax.fori_loop(0, i, lambda kb, cr: step(kb, cr, None), tuple(jnp.zeros((1, t), F32) for _ in hds))
        step(i, g_pres, strict)
        for pp in range(ap):
            dq_ref[:, _pair(pp)] = (dq_acc[pp] * SB_SCALE).astype(BF16)

    qspec = pl.BlockSpec((t, width), lambda b, p, i: (b * nq + i, p))
    kspec_out = pl.BlockSpec((seq, width), lambda b, p, i: (b, p))
    car_rows = len(hds) * CAR_SLOTS
    return _carrier_call(
        body, ex, "sb_bwd", (nb, groups, nq),
        [qkv, qkv, qkv, d_out, cars, _tri(t, "later"), _tri(t, "earlier")],
        [qspec,
         pl.BlockSpec((seq, width), lambda b, p, i: (b, groups + p)),
         pl.BlockSpec((seq, width), lambda b, p, i: (b, 2 * groups + p)),
         qspec, pl.BlockSpec((1, car_rows, t), lambda b, p, i: (b * nq + i, p, 0)),
         _const_spec((t, 2 * t)), _const_spec((t, 2 * t))],
        [jax.ShapeDtypeStruct((rows, SB_W), BF16), jax.ShapeDtypeStruct((rows, SB_W), F32),
         jax.ShapeDtypeStruct((rows, SB_W), F32)],
        [qspec, kspec_out, kspec_out],
        scratch=[pltpu.VMEM((ap, t, LANES), F32)])


def _mla_scores(ks, qh, allowed):
    s = _dot(ks, qh, NT) * (MLA_SCALE * -NEG_LOG2E)
    if allowed is not None:
        s = jnp.where(allowed, s, jnp.finfo(F32).min)
    return s


def _mla_fwd(qp, kp, vm, seq, ex=None, chunk=64):
    rows = qp.shape[0]
    nb = rows // seq
    t = min(ATTN_TILE, seq)
    nq = seq // t
    shift = int(math.log2(chunk))
    ap = ATTN_PAIRS
    width = ap * LANES
    groups = MLA_W // width
    hds = [(pp, hh) for pp in range(ap) for hh in range(2)]

    def body(q_ref, k_ref, v_ref, o_ref, lse_ref, acc_ref):
        i = pl.program_id(2)
        lane = lax.broadcasted_iota(jnp.int32, (t, LANES), 1)
        key = lax.broadcasted_iota(jnp.int32, (t, t), 0)
        qry = lax.broadcasted_iota(jnp.int32, (t, t), 1)
        allowed_diag = jnp.right_shift(key, shift) <= jnp.right_shift(qry, shift)
        masks = [_head_mask(lane, hh) for hh in range(2)]
        qhs = [q_ref[:, _pair(n)] for n in range(len(hds))]
        acc_ref[...] = jnp.zeros_like(acc_ref)

        def step(kb, carry, allowed):
            start = pl.multiple_of(kb * t, t)
            vss = [v_ref[pl.ds(start, t), _pair(pp)] for pp in range(ap)]
            scores = [_mla_scores(k_ref[pl.ds(start, t), _pair(n)], qhs[n], allowed) for n in range(len(hds))]
            new = []
            for n, (pp, hh) in enumerate(hds):
                m_run, l_run = carry[n]
                s = scores[n]
                m_new = jnp.maximum(m_run, jnp.max(s, axis=0, keepdims=True))
                p = jnp.exp2(s - m_new)
                scale = jnp.exp2(m_run - m_new)
                l_run = scale * l_run + jnp.sum(p, axis=0, keepdims=True)
                acc_ref[n] = scale * acc_ref[n] + _dot(vss[pp] * masks[hh], p.astype(BF16), TN)
                new.append((m_new, l_run))
            return tuple(new)

        init = (jnp.full((1, t), jnp.finfo(F32).min, F32), jnp.zeros((1, t), F32))
        carry = step(i, tuple(init for _ in hds), allowed_diag)
        carry = lax.fori_loop(0, i, lambda kb, cr: step(kb, cr, None), carry)
        lse_rows = []
        for pp in range(ap):
            out_t = jnp.zeros((LANES, t), F32)
            for hh in range(2):
                m_run, l_run = carry[2 * pp + hh]
                out_t = out_t + acc_ref[2 * pp + hh] / l_run
                lse_rows.append(m_run + jnp.log(l_run) * -NEG_LOG2E)
            o_ref[:, _pair(pp)] = out_t.T
        lse_t = jnp.concatenate(lse_rows + [jnp.zeros((LANES - len(hds), t), F32)], axis=0)
        lse_ref[...] = jnp.zeros_like(lse_ref)
        lse_ref[:, _pair(0)] = lse_t.T

    ospec = pl.BlockSpec((t, width), lambda b, p, i: (b * nq + i, p))
    return _carrier_call(
        body, ex, "mla_fwd", (nb, groups, nq), [qp, kp, vm],
        [pl.BlockSpec((t, 2 * width), lambda b, p, i: (b * nq + i, p)),
         pl.BlockSpec((seq, 2 * width), lambda b, p, i: (b, p)),
         pl.BlockSpec((seq, width), lambda b, p, i: (b, p))],
        [jax.ShapeDtypeStruct((rows, MLA_W), F32), jax.ShapeDtypeStruct((rows, MLA_W), F32)],
        [ospec, ospec], scratch=[pltpu.VMEM((len(hds), LANES, t), F32)])


def _mla_bwd(qp, kp, vm, d_out, out, lse, seq, ex=None, chunk=64):
    rows = qp.shape[0]
    nb = rows // seq
    t = min(ATTN_TILE, seq)
    nq = seq // t
    shift = int(math.log2(chunk))
    ap = ATTN_PAIRS
    width = ap * LANES
    groups = MLA_W // width
    hds = [(pp, hh) for pp in range(ap) for hh in range(2)]
    nh = len(hds)

    def body(q_ref, k_ref, v_ref, do_ref, o_ref, lse_ref, dq_ref, dk_ref, dv_ref):
        i = pl.program_id(2)

        @pl.when(i == 0)
        def _():
            dk_ref[...] = jnp.zeros_like(dk_ref)
            dv_ref[...] = jnp.zeros_like(dv_ref)

        lane = lax.broadcasted_iota(jnp.int32, (t, LANES), 1)
        key = lax.broadcasted_iota(jnp.int32, (t, t), 0)
        qry = lax.broadcasted_iota(jnp.int32, (t, t), 1)
        allowed_diag = jnp.right_shift(key, shift) <= jnp.right_shift(qry, shift)
        qhs = [q_ref[:, _pair(n)] for n in range(nh)]
        lse_t = lse_ref[:, _pair(0)].T
        doms, deltas, lse_hs = [], [], []
        for pp in range(ap):
            do = do_ref[:, _pair(pp)]
            d_o_t = (do * o_ref[:, _pair(pp)]).T
            for hh in range(2):
                doms.append(do.astype(BF16) * _head_mask(lane, hh))
                deltas.append(jnp.sum(d_o_t[hh * 64:(hh + 1) * 64], axis=0, keepdims=True))
                lse_hs.append(lse_t[2 * pp + hh:2 * pp + hh + 1])

        dq_ref[...] = jnp.zeros_like(dq_ref)

        def step(kb, allowed):
            start = pl.multiple_of(kb * t, t)
            vss = [v_ref[pl.ds(start, t), _pair(pp)] for pp in range(ap)]
            kss = [k_ref[pl.ds(start, t), _pair(n)] for n in range(nh)]
            scores = [_mla_scores(kss[n], qhs[n], allowed) for n in range(nh)]
            dps = [_dot(vss[pp], doms[n], NT) for n, (pp, _) in enumerate(hds)]
            ps = [jnp.exp2(scores[n] - lse_hs[n]) for n in range(nh)]
            dss = [(ps[n] * (dps[n] - deltas[n]) * MLA_SCALE).astype(BF16) for n in range(nh)]
            for pp in range(ap):
                a, b = 2 * pp, 2 * pp + 1
                dv_ref[pl.ds(start, t), _pair(pp)] += _dot(ps[a].astype(BF16), doms[a]) + _dot(ps[b].astype(BF16), doms[b])
            for n in range(nh):
                dk_ref[pl.ds(start, t), _pair(n)] += _dot(dss[n], qhs[n])
                dq_ref[:, _pair(n)] += _dot(dss[n], kss[n], TN)

        def off_diagonal(kb, nothing):
            step(kb, None)
            return nothing

        lax.fori_loop(0, i, off_diagonal, 0)
        step(i, allowed_diag)

    ospec = pl.BlockSpec((t, width), lambda b, p, i: (b * nq + i, p))
    return _carrier_call(
        body, ex, "mla_bwd", (nb, groups, nq), [qp, kp, vm, d_out, out, lse],
        [pl.BlockSpec((t, 2 * width), lambda b, p, i: (b * nq + i, p)),
         pl.BlockSpec((seq, 2 * width), lambda b, p, i: (b, p)),
         pl.BlockSpec((seq, width), lambda b, p, i: (b, p)),
         pl.BlockSpec((t, width), lambda b, p, i: (b * nq + i, groups + p)),
         ospec, ospec],
        [jax.ShapeDtypeStruct((rows, HEADS * LANES), F32), jax.ShapeDtypeStruct((rows, HEADS * LANES), F32),
         jax.ShapeDtypeStruct((rows, MLA_W), F32)],
        [pl.BlockSpec((t, 2 * width), lambda b, p, i: (b * nq + i, p)),
         pl.BlockSpec((seq, 2 * width), lambda b, p, i: (b, p)),
         pl.BlockSpec((seq, width), lambda b, p, i: (b, p))])


PACK_COLS = 1024
PACK_ALIGN = 16
GROUP_IN = (384, ((1024, 552, 1), (384, 192, 1), (256, 256, 1)))
GROUP_MLP = (1152, ((1024, 1024, 1), (1024, 1024, 0), (256, 1024, 0)))


def _pack_rows(r, c):
    return (r // 2) * c // PACK_COLS


def _slot_rows(r, c):
    return -(-_pack_rows(r, c) // PACK_ALIGN) * PACK_ALIGN


def _join_slots(parts, group):
    total, weights = group
    padded = [jnp.pad(p, ((0, 0), (0, _slot_rows(r, c) - p.shape[1]), (0, 0))) for p, (r, c, _) in zip(parts, weights)]
    used = sum(_slot_rows(r, c) for r, c, _ in weights)
    if total > used:
        padded.append(jnp.zeros((parts[0].shape[0], total - used, PACK_COLS), parts[0].dtype))
    return jnp.concatenate(padded, axis=1)


def _split_slots(packed, group):
    out, at = [], 0
    for r, c, _ in group[1]:
        out.append(packed[:, at:at + _pack_rows(r, c), :])
        at += _slot_rows(r, c)
    return out


def _pack_halves(shards, group):
    return _join_slots([s.reshape(2, _pack_rows(r, c), PACK_COLS) for s, (r, c, _) in zip(shards, group[1])], group)


def _unpack_half(packed, group):
    return [p.reshape(r // 2, c) for p, (r, c, _) in zip(_split_slots(packed[None], group), group[1])]


def _unpack_full(gathered, group):
    out = []
    for p, (r, c, axis) in zip(_split_slots(gathered, group), group[1]):
        shards = p.reshape(4, r, c)
        out.append(shards.reshape(4 * r, c) if axis == 0 else jnp.moveaxis(shards, 0, 1).reshape(r, 4 * c))
    return out


def _pack_full(grads, group):
    parts = []
    for gr, (r, c, axis) in zip(grads, group[1]):
        shards = gr.reshape(4, r, c) if axis == 0 else jnp.moveaxis(gr.reshape(r, 4, c), 1, 0)
        parts.append(shards.reshape(8, _pack_rows(r, c), PACK_COLS))
    return _join_slots(parts, group)


def _pad_w_in(w_in):
    z = jnp.zeros((D_MODEL, 1), w_in.dtype)
    return jnp.concatenate([w_in[:, :2176], jnp.tile(z, (1, 64)), w_in[:, 2176:], jnp.tile(z, (1, 32))], axis=1)


def _unpad_w_in(g):
    return jnp.concatenate([g[:, :2176], g[:, 2240:2272]], axis=1)


def _pad_heads(w, used):
    k = w.shape[0]
    w3 = w.reshape(k, HEADS, used)
    return jnp.pad(w3, ((0, 0), (0, 0), (0, LANES - used))).reshape(k, HEADS * LANES)


def _unpad_heads(g, used):
    k = g.shape[0]
    return g.reshape(k, HEADS, LANES)[:, :, :used].reshape(k, HEADS * used)


def _rope_tables(seq):
    inv_freq = 1.0 / (ROPE_BASE ** (jnp.arange(0, ROPE, 2, dtype=F32) / ROPE))
    ang = jnp.arange(seq, dtype=F32)[:, None] * inv_freq[None, :]
    cos, sin = jnp.cos(ang), jnp.sin(ang)
    one, zero = jnp.ones((seq, NOPE), F32), jnp.zeros((seq, NOPE), F32)
    z16, z32 = jnp.zeros((seq, 16), F32), jnp.zeros((seq, 32), F32)
    cos_t = jnp.concatenate([one, cos, cos, jnp.ones((seq, 32), F32)], axis=1)
    sin_a = jnp.concatenate([zero, -sin, z16, z32], axis=1)
    sin_b = jnp.concatenate([zero, z16, sin, z32], axis=1)
    return cos_t, sin_a, sin_b


SMALL = (("ln_in_g", 1024), ("ln_in_b", 1024), ("b_ada", 6144), ("q_norm_g", 384), ("kv_norm_g", 256),
         ("ln1_g", 1024), ("ln1_b", 1024), ("ln2_g", 1024), ("ln2_b", 1024))
SUBLANES = 8
SMALL_SLOTS = [-(-n // LANES // SUBLANES) * SUBLANES for _, n in SMALL]
SMALL_AT = [sum(SMALL_SLOTS[:p]) for p in range(len(SMALL))]
SMALL_ROWS = sum(SMALL_SLOTS)


def _pack_small(vals):
    parts = []
    for v, slot in zip(vals, SMALL_SLOTS):
        rows = v.reshape(-1, LANES)
        parts.append(jnp.pad(rows, ((0, slot - rows.shape[0]), (0, 0))))
    return jnp.concatenate(parts, axis=0)


def kernel(x, c, ln_in_g, ln_in_b, w_ada, b_ada, w_in, q_norm_g, kv_norm_g, w_uq, w_ukv, w_o, ln1_g, ln1_b, w_up, w_down, ln2_g, ln2_b, loss_target, m_ln_in_g, m_ln_in_b, m_w_ada, m_b_ada, m_w_in, m_q_norm_g, m_kv_norm_g, m_w_uq, m_w_ukv, m_w_o, m_ln1_g, m_ln1_b, m_w_up, m_w_down, m_ln2_g, m_ln2_b, v_ln_in_g, v_ln_in_b, v_w_ada, v_b_ada, v_w_in, v_q_norm_g, v_kv_norm_g, v_w_uq, v_w_ukv, v_w_o, v_ln1_g, v_ln1_b, v_w_up, v_w_down, v_ln2_g, v_ln2_b):
    nb, seq, _ = x.shape
    rows = nb * seq
    ix, iy, ic = lax.axis_index("x"), lax.axis_index("y"), lax.axis_index("c")
    chip = 2 * ix + iy
    dev = 2 * chip + ic

    def my_half(shards, group):
        packed = _pack_halves([s.astype(BF16) for s in shards], group)
        return lax.dynamic_index_in_dim(packed, ic, 0, keepdims=False)

    f_in, f_uq, f_ukv = _unpack_full(_gather8(my_half([w_in[0], w_uq[0], w_ukv[0]], GROUP_IN), "gather_w_in"),
                                     GROUP_IN)
    half_mlp = my_half([w_up[0], w_down[0], w_o[0]], GROUP_MLP)
    late_weights = _gather_exchange(half_mlp)
    w_in_p = _pad_w_in(f_in)
    uq3 = f_uq.reshape(Q_RANK, HEADS, NOPE + ROPE)
    w_uq_p = jnp.pad(uq3, ((0, 0), (0, 0), (0, LANES - NOPE - ROPE))).reshape(Q_RANK, HEADS * LANES)
    w_ukv_p = jnp.concatenate([_pad_heads(f_ukv[:, :HEADS * NOPE], NOPE), f_ukv[:, HEADS * NOPE:]], axis=1)

    n_all = 8 * nb
    c_all = _gather8(c.reshape(-1, LANES), "gather_c").reshape(n_all, D_MODEL)
    ada_cols = w_ada.shape[2]
    b_sh = lax.dynamic_slice_in_dim(b_ada, chip * ada_cols, ada_cols, axis=1)
    mod_sh = _ada_fwd(c_all, w_ada[0], b_sh)
    mod_g = _gather8(mod_sh, "gather_mod")[0::2]
    mod_all = jnp.moveaxis(mod_g, 0, 1).reshape(n_all, N_MOD * D_MODEL)
    mod_mine = lax.dynamic_slice_in_dim(mod_all, dev * nb, nb, axis=0).reshape(nb, N_MOD, D_MODEL)
    mod = jnp.pad(mod_mine, ((0, 0), (0, 8 - N_MOD), (0, 0)))

    cos_t, sin_a, sin_b = _rope_tables(seq)
    row2 = lambda v: v.reshape(1, -1)

    x2d = x.reshape(rows, D_MODEL)
    x0, h, qkv, lat, qp, kp, vm = _fwd_in(x2d, mod, row2(ln_in_g), row2(ln_in_b), w_in_p, q_norm_g, kv_norm_g,
                                          w_uq_p, w_ukv_p, cos_t, sin_a, sin_b, seq)
    sb_y, cars, g_mlp = _sb_fwd(qkv, seq, late_weights)
    g_mlp = _with_own(g_mlp, half_mlp)
    f_o = _split_slots(g_mlp, GROUP_MLP)[2].reshape(D_MODEL, D_MODEL)
    mla_y, lse = _mla_fwd(qp, kp, vm, seq)
    mix, y1, h2, u, ff, y2 = _fwd_out(sb_y, mla_y, x0, mod, f_o, ln1_g, ln1_b, g_mlp, seq)

    dy1, dmix, d_attn, dff, du, acc_out, dmod_a = _bwd_out(
        y2, loss_target.reshape(rows, D_MODEL), ff, u, y1, mix, mod, ln2_g, ln2_b, ln1_g, ln1_b, g_mlp, f_o, seq)
    c_idx = ic.reshape(1).astype(jnp.int32)
    blocks_mlp = _wgrad_packed(h2, du, "wgrad_up", lambda i, j: 2 * j + i, 0)
    blocks_mlp = _wgrad_packed(u, dff, "wgrad_down", lambda i, j: i, 1, pre="relu2", into=blocks_mlp)
    blocks_mlp = _wgrad_packed(sb_y, dmix, "wgrad_o_sb", lambda i, j: 0, 8, split=4, into=blocks_mlp)
    blocks_mlp = _wgrad_packed(mla_y, dmix, "wgrad_o_mla", lambda i, j: 1, 8, split=4, into=blocks_mlp)
    dq_sb, dk_sb, dv_sb, sibling_mlp = _sb_bwd(qkv, d_attn, cars, seq, _swap_cores_exchange(blocks_mlp))
    part_mlp, part_mlp_bf = _add_pairs(blocks_mlp, sibling_mlp, c_idx, "grad_add_cores_mlp")
    dqp, dkp, dvm, chips_mlp = _mla_bwd(qp, kp, vm, d_attn, mla_y, lse, seq, _scatter_chips_exchange(part_mlp_bf))
    grad_x, dproj, dqall, dkv, latn, acc0, acc_lat, dmod_c = _bwd_in(
        dqp, dkp, dvm, dq_sb, dk_sb, dv_sb, lat, x2d, dy1, mod, row2(ln_in_g), row2(ln_in_b), w_in_p,
        q_norm_g, kv_norm_g, w_uq_p, w_ukv_p, cos_t, sin_a, sin_b, seq)

    dmod = (dmod_a + dmod_c)[:, :N_MOD, :]
    small_part = _pack_small([acc0[0], acc0[1], jnp.zeros((N_MOD * D_MODEL,), F32), acc_lat[0, :Q_RANK],
                              acc_lat[1, :KV_RANK], acc_out[3], acc_out[4], acc_out[0], acc_out[1]])
    n_sum = SMALL_ROWS + D_MODEL // LANES
    payload = jnp.concatenate([small_part, acc_out[2].reshape(-1, LANES), dmod.reshape(-1, LANES)], axis=0)
    g_in_p, gathered = _wgrad(h, dproj, "wgrad_in", tn=768, ex=_gather_exchange(payload))
    gathered = _with_own(gathered, payload)
    g_in = _unpad_w_in(g_in_p)
    g_uq = _unpad_heads(_wgrad(latn[:, :Q_RANK], dqall, "wgrad_uq"), NOPE + ROPE)
    g_ukv_p = _wgrad(latn[:, Q_RANK:], dkv, "wgrad_ukv", tn=512)
    g_ukv = jnp.concatenate([_unpad_heads(g_ukv_p[:, :HEADS * LANES], NOPE), g_ukv_p[:, HEADS * LANES:]], axis=1)
    blocks_in = _pack_full([g_in, g_uq, g_ukv], GROUP_IN)
    sibling_in = _run_exchange(_swap_cores_exchange(blocks_in), "grads_in_to_sibling")
    part_in, part_in_bf = _add_pairs(blocks_in, sibling_in, c_idx, "grad_add_cores_in")
    chips_in = _run_exchange(_scatter_chips_exchange(part_in_bf), "grads_in_to_chips")

    def own(part):
        return lax.dynamic_index_in_dim(part, chip, 0, keepdims=False)

    half = jnp.concatenate([_add_chips(own(part_in), chips_in, "grad_add_chips_in"),
                            _add_chips(own(part_mlp), chips_mlp, "grad_add_chips_mlp")], axis=0)
    other = _run_exchange(_swap_one_exchange(half), "grads_halves")
    mine = _unpack_half(half[:GROUP_IN[0]], GROUP_IN) + _unpack_half(half[GROUP_IN[0]:], GROUP_MLP)
    theirs = _unpack_half(other[:GROUP_IN[0]], GROUP_IN) + _unpack_half(other[GROUP_IN[0]:], GROUP_MLP)

    small_sum = _sum_lead(gathered[:, :n_sum, :], "sum_small")
    loss = jnp.sum(small_sum[SMALL_ROWS:])
    dmod_all = gathered[:, n_sum:, :].reshape(n_all, N_MOD * D_MODEL)
    g_b_ada = _sum_lead(dmod_all.reshape(n_all, N_MOD * D_MODEL // LANES, LANES), "sum_b_ada")
    dmod_sh = lax.dynamic_slice_in_dim(dmod_all, chip * ada_cols, ada_cols, axis=1)
    g_w_ada = _ada_bwd(c_all, dmod_sh)

    res = {}
    d_ada, m_ada, v_ada = _adamw(w_ada[0], g_w_ada, m_w_ada[0], v_w_ada[0], "adamw_w_ada")
    res["w_ada"] = (g_w_ada[None], d_ada[None], m_ada[None], v_ada[None])
    sharded = {"w_in": (w_in, m_w_in, v_w_in), "w_uq": (w_uq, m_w_uq, v_w_uq), "w_ukv": (w_ukv, m_w_ukv, v_w_ukv),
               "w_up": (w_up, m_w_up, v_w_up), "w_down": (w_down, m_w_down, v_w_down), "w_o": (w_o, m_w_o, v_w_o)}
    for (name, (w, m, v)), g_mine, g_other in zip(sharded.items(), mine, theirs):
        quad = _adamw_halves(w[0], g_mine, g_other, m[0], v[0], c_idx, "adamw_" + name)
        res[name] = tuple(a[None] for a in quad)
    small_w = [ln_in_g, ln_in_b, b_ada, q_norm_g, kv_norm_g, ln1_g, ln1_b, ln2_g, ln2_b]
    small_m = [m_ln_in_g, m_ln_in_b, m_b_ada, m_q_norm_g, m_kv_norm_g, m_ln1_g, m_ln1_b, m_ln2_g, m_ln2_b]
    small_v = [v_ln_in_g, v_ln_in_b, v_b_ada, v_q_norm_g, v_kv_norm_g, v_ln1_g, v_ln1_b, v_ln2_g, v_ln2_b]
    for (name, _), quad in zip(SMALL, _adamw_small(small_sum, g_b_ada, small_w, small_m, small_v)):
        res[name] = quad

    order = ["ln_in_g", "ln_in_b", "w_ada", "b_ada", "w_in", "q_norm_g", "kv_norm_g", "w_uq", "w_ukv", "w_o",
             "ln1_g", "ln1_b", "w_up", "w_down", "ln2_g", "ln2_b"]
    outs = [loss, grad_x.reshape(nb, seq, D_MODEL)]
    for k in range(4):
        outs += [res[name][k] for name in order]
    return tuple(outs)
```

```python
import math

import jax
import jax.numpy as jnp
from jax import lax
from jax.experimental import pallas as pl
from jax.experimental.pallas import tpu as pltpu

F32 = jnp.float32
BF16 = jnp.bfloat16
MESH_IDS = pl.DeviceIdType.MESH

D_MODEL = 1024
HEADS = 8
SB_W = 512
MLA_W = 512
NOPE = 64
ROPE = 32
Q_RANK = 384
KV_RANK = 256
D_IN_PAD = 2304
D_FF = 4096
N_MOD = 6
LN_EPS = 1e-5
RMS_EPS = 1e-6
ALPHA = 2.0 ** 0.25
ROPE_BASE = 10000.0
SB_SCALE = 64 ** -0.5
NEG_LOG2E = -math.log2(math.e)
MLA_SCALE = 96 ** -0.5
ADAM_LR = 0.001
ADAM_B1 = 0.9
ADAM_B2 = 0.999
ADAM_EPS = 1e-08
ADAM_WD = 0.01
ADAM_STEP = 10

LANES = 128
ROW_TILE = 256
ATTN_TILE = 256
CAR_SLOTS = 8
ATTN_PAIRS = 4
VMEM_LIMIT = 56 << 20

NT = (((1,), (1,)), ((), ()))
TN = (((0,), (0,)), ((), ()))


def _params(sem=None):
    return pltpu.CompilerParams(vmem_limit_bytes=VMEM_LIMIT, dimension_semantics=sem)


def _const_spec(shape):
    zeros = (0,) * len(shape)
    return pl.BlockSpec(shape, lambda *_: zeros, pipeline_mode=pl.Buffered(1))


def _dot(a, b, dims=None):
    if dims is None:
        return jnp.dot(a, b, preferred_element_type=F32)
    return lax.dot_general(a, b, dims, preferred_element_type=F32)


def _mean(v):
    return jnp.mean(v, axis=-1, keepdims=True)


def _rowsum(v):
    return jnp.sum(v, axis=0, keepdims=True)


def _ln_fwd(y, g, b):
    mu = _mean(y)
    yc = y - mu
    rstd = lax.rsqrt(_mean(yc * yc) + LN_EPS)
    xhat = yc * rstd
    return xhat * g + b, xhat, rstd


def _ln_bwd(dx, xhat, rstd, g):
    dxh = dx * g
    return rstd * (dxh - _mean(dxh) - xhat * _mean(dxh * xhat))


def _rope(v, cos, sin_a, sin_b):
    return v * cos + pltpu.roll(v, 112, 1) * sin_a + pltpu.roll(v, 16, 1) * sin_b


def _rope_t(dv, cos, sin_a, sin_b):
    return dv * cos + pltpu.roll(dv * sin_a, 16, 1) + pltpu.roll(dv * sin_b, 112, 1)


def _my_place():
    return lax.axis_index("x"), lax.axis_index("y"), lax.axis_index("c")


class _Exchange:
    def __init__(self, operand, out_shape, n_copies, phases):
        self.operand = operand
        self.out_shape = out_shape
        self.phases = phases
        self.scratch = [pltpu.SemaphoreType.DMA((n_copies,)), pltpu.SemaphoreType.DMA((n_copies,))]


def _run_exchange(ex, name):
    def body(in_ref, out_ref, send_sems, recv_sems):
        for phase in ex.phases(in_ref, out_ref, send_sems, recv_sems):
            phase()

    return pl.pallas_call(
        body, name=name, out_shape=ex.out_shape,
        in_specs=[pl.BlockSpec(memory_space=pl.ANY)], out_specs=pl.BlockSpec(memory_space=pl.ANY),
        scratch_shapes=ex.scratch,
    )(ex.operand)


def _nothing():
    pass


def _gather_exchange(v):
    m, n = v.shape

    def phases(v_ref, out_ref, send_sems, recv_sems):
        x, y, c = _my_place()
        me, sibling = (x, y, c), (x, y, 1 - c)
        chips = [(1 - x, y), (x, 1 - y), (1 - x, 1 - y)]

        def rows(px, py, pc):
            return out_ref.at[4 * px + 2 * py + pc]

        def copy(k, block, to, src=None):
            return pltpu.make_async_remote_copy(
                src_ref=rows(*block) if src is None else src, dst_ref=rows(*block),
                send_sem=send_sems.at[k], recv_sem=recv_sems.at[k], device_id=to, device_id_type=MESH_IDS)

        first = [copy(0, me, sibling, src=v_ref)]
        first += [copy(1 + j, me, (*chip, c), src=v_ref) for j, chip in enumerate(chips)]
        passed = [copy(4 + j, (*chip, c), sibling) for j, chip in enumerate(chips)]

        def start():
            for cp in first:
                cp.start()

        def middle():
            for j, chip in enumerate(chips):
                copy(1 + j, (*chip, c), me).wait_recv()
                passed[j].start()

        def finish():
            copy(0, sibling, me).wait_recv()
            for j, chip in enumerate(chips):
                copy(4 + j, (*chip, 1 - c), me).wait_recv()
            for cp in first + passed:
                cp.wait_send()

        return start, middle, finish

    return _Exchange(v, jax.ShapeDtypeStruct((8, m, n), v.dtype), 7, phases)


def _with_own(gathered, v):
    dev = 4 * lax.axis_index("x") + 2 * lax.axis_index("y") + lax.axis_index("c")
    return lax.dynamic_update_index_in_dim(gathered, v, dev, 0)


def _direct_exchange(operand, out_shape, n_copies, make_copies):
    def phases(in_ref, out_ref, send_sems, recv_sems):
        copies = make_copies(in_ref, out_ref, send_sems, recv_sems)

        def start():
            for cp in copies:
                cp.start()

        def finish():
            for cp in copies:
                cp.wait()

        return start, _nothing, finish

    return _Exchange(operand, out_shape, n_copies, phases)


def _swap_cores_exchange(blocks):
    _, m, n = blocks.shape

    def make_copies(g_ref, out_ref, send_sems, recv_sems):
        x, y, c = _my_place()
        return [pltpu.make_async_remote_copy(
            src_ref=g_ref.at[2 * j + (1 - c)], dst_ref=out_ref.at[j],
            send_sem=send_sems.at[j], recv_sem=recv_sems.at[j],
            device_id=(x, y, 1 - c), device_id_type=MESH_IDS) for j in range(4)]

    return _direct_exchange(blocks, jax.ShapeDtypeStruct((4, m, n), blocks.dtype), 4, make_copies)


def _scatter_chips_exchange(parts):
    _, m, n = parts.shape
    flips = [(1, 0), (0, 1), (1, 1)]

    def make_copies(p_ref, out_ref, send_sems, recv_sems):
        x, y, c = _my_place()
        copies = []
        for k, (fx, fy) in enumerate(flips):
            tx = 1 - x if fx else x
            ty = 1 - y if fy else y
            copies.append(pltpu.make_async_remote_copy(
                src_ref=p_ref.at[2 * tx + ty], dst_ref=out_ref.at[k],
                send_sem=send_sems.at[k], recv_sem=recv_sems.at[k],
                device_id=(tx, ty, c), device_id_type=MESH_IDS))
        return copies

    return _direct_exchange(parts, jax.ShapeDtypeStruct((3, m, n), parts.dtype), 3, make_copies)


def _swap_one_exchange(v):
    def make_copies(v_ref, out_ref, send_sems, recv_sems):
        x, y, c = _my_place()
        return [pltpu.make_async_remote_copy(src_ref=v_ref, dst_ref=out_ref, send_sem=send_sems.at[0],
                                             recv_sem=recv_sems.at[0], device_id=(x, y, 1 - c),
                                             device_id_type=MESH_IDS)]

    return _direct_exchange(v, jax.ShapeDtypeStruct(v.shape, v.dtype), 1, make_copies)


def _gather8(v, name):
    return _with_own(_run_exchange(_gather_exchange(v), name), v)


def _carried(ex, refs, n_in, n_out, n_scratch):
    ins, ex_in = refs[:n_in], refs[n_in]
    outs, ex_out = refs[n_in + 1:n_in + 1 + n_out], refs[n_in + 1 + n_out]
    at = n_in + 2 + n_out
    return ins, outs + refs[at:at + n_scratch], ex.phases(ex_in, ex_out, *refs[at + n_scratch:])


def _ada_fwd(c_all, w_ada_sh, b_ada_sh):
    nb, cols = c_all.shape[0], w_ada_sh.shape[1]
    tn = 512

    def body(c_ref, w_ref, b_ref, o_ref):
        cv = c_ref[...]
        act = (cv * jax.nn.sigmoid(cv)).astype(BF16)
        o_ref[...] = _dot(act, w_ref[...].astype(BF16)) + b_ref[...]

    return pl.pallas_call(
        body, name="ada_fwd", grid=(cols // tn,),
        out_shape=jax.ShapeDtypeStruct((nb, cols), F32),
        in_specs=[pl.BlockSpec((nb, D_MODEL), lambda j: (0, 0)),
                  pl.BlockSpec((D_MODEL, tn), lambda j: (0, j)),
                  pl.BlockSpec((1, tn), lambda j: (0, j))],
        out_specs=pl.BlockSpec((nb, tn), lambda j: (0, j)),
        compiler_params=_params(("arbitrary",)),
    )(c_all, w_ada_sh, b_ada_sh)


def _ada_bwd(c_all, dmod_sh):
    nb, cols = dmod_sh.shape
    tn = 512

    def body(c_ref, d_ref, o_ref):
        cv = c_ref[...]
        act = (cv * jax.nn.sigmoid(cv)).astype(BF16)
        o_ref[...] = _dot(act, d_ref[...].astype(BF16), TN)

    return pl.pallas_call(
        body, name="ada_bwd", grid=(cols // tn,),
        out_shape=jax.ShapeDtypeStruct((D_MODEL, cols), F32),
        in_specs=[pl.BlockSpec((nb, D_MODEL), lambda j: (0, 0)),
                  pl.BlockSpec((nb, tn), lambda j: (0, j))],
        out_specs=pl.BlockSpec((D_MODEL, tn), lambda j: (0, j)),
        compiler_params=_params(("arbitrary",)),
    )(c_all, dmod_sh)


def _sum_lead(v, name):
    k, m, n = v.shape

    def body(v_ref, o_ref):
        acc = v_ref[0]
        for i in range(1, k):
            acc = acc + v_ref[i]
        o_ref[...] = acc

    return pl.pallas_call(
        body, name=name, out_shape=jax.ShapeDtypeStruct((m, n), F32),
        in_specs=[pl.BlockSpec((k, m, n), lambda: (0, 0, 0))],
        out_specs=pl.BlockSpec((m, n), lambda: (0, 0)),
        compiler_params=_params(),
    )(v)


def _adamw_math(w, g, m, v):
    mn = ADAM_B1 * m + (1.0 - ADAM_B1) * g
    vn = ADAM_B2 * v + (1.0 - ADAM_B2) * (g * g)
    m_hat = mn / (1.0 - ADAM_B1 ** ADAM_STEP)
    v_hat = vn / (1.0 - ADAM_B2 ** ADAM_STEP)
    return -ADAM_LR * (m_hat / (jnp.sqrt(v_hat) + ADAM_EPS) + ADAM_WD * w), mn, vn


def _adamw_small(g_sum, g_b_ada, ws, ms, vs):
    n = len(SMALL)

    def body(gs_ref, gb_ref, *refs):
        outs = refs[3 * n:]
        for p in range(n):
            rows_p = SMALL[p][1] // LANES
            g = gb_ref[...] if SMALL[p][0] == "b_ada" else gs_ref[SMALL_AT[p]:SMALL_AT[p] + rows_p, :]
            d, mn, vn = _adamw_math(refs[p][...], g, refs[n + p][...], refs[2 * n + p][...])
            outs[4 * p][...] = g
            outs[4 * p + 1][...] = d
            outs[4 * p + 2][...] = mn
            outs[4 * p + 3][...] = vn

    shapes = [jax.ShapeDtypeStruct((size // LANES, LANES), F32) for _, size in SMALL for _ in range(4)]
    flat = lambda arrs: [a.reshape(-1, LANES) for a in arrs]
    res = pl.pallas_call(body, name="adamw_small", out_shape=tuple(shapes), compiler_params=_params())(
        g_sum, g_b_ada, *flat(ws), *flat(ms), *flat(vs))
    return [tuple(r.reshape(w.shape) for r in res[4 * p:4 * p + 4]) for p, w in enumerate(ws)]


def _adamw_halves(w, g_mine, g_other, m, v, c_idx, name):
    r, cols = w.shape
    half = r // 2
    tr = half
    while tr * cols * 4 > (2 << 20) and tr % 16 == 0:
        tr //= 2

    def body(c_ref, w_ref, mine_ref, other_ref, m_ref, v_ref, g_ref, d_ref, mo_ref, vo_ref):
        g = jnp.where(pl.program_id(0) == c_ref[0], mine_ref[...], other_ref[...])
        g_ref[0] = g
        d_ref[0], mo_ref[0], vo_ref[0] = _adamw_math(w_ref[0], g, m_ref[0], v_ref[0])

    full = pl.BlockSpec((1, tr, cols), lambda h, i, c: (h, i, 0))
    part = pl.BlockSpec((tr, cols), lambda h, i, c: (i, 0))
    shape = jax.ShapeDtypeStruct((2, half, cols), F32)
    grid_spec = pltpu.PrefetchScalarGridSpec(
        num_scalar_prefetch=1, grid=(2, half // tr),
        in_specs=[full, part, part, full, full], out_specs=(full, full, full, full))
    split = lambda a: a.reshape(2, half, cols)
    res = pl.pallas_call(
        body, name=name, grid_spec=grid_spec, out_shape=(shape, shape, shape, shape),
        compiler_params=_params(("arbitrary", "arbitrary")),
    )(c_idx, split(w), g_mine, g_other, split(m), split(v))
    return tuple(a.reshape(r, cols) for a in res)


def _adamw(w, g, m, v, name):
    rows, cols = w.shape
    tr = rows
    while tr * cols * 4 > (2 << 20) and tr % 16 == 0:
        tr //= 2

    def body(w_ref, g_ref, m_ref, v_ref, d_ref, mo_ref, vo_ref):
        d_ref[...], mo_ref[...], vo_ref[...] = _adamw_math(w_ref[...], g_ref[...], m_ref[...], v_ref[...])

    spec = pl.BlockSpec((tr, cols), lambda i: (i, 0))
    shape = jax.ShapeDtypeStruct((rows, cols), F32)
    return pl.pallas_call(
        body, name=name, grid=(rows // tr,), out_shape=(shape, shape, shape),
        in_specs=[spec, spec, spec, spec], out_specs=(spec, spec, spec),
        compiler_params=_params(("arbitrary",)),
    )(w, g, m, v)


def _add_rows(m, n):
    fits = [d for d in range(16, m + 1, 16) if m % d == 0 and d * n * 4 <= (5 << 19)]
    assert fits, (m, n)
    return max(fits)


def _add_pairs(blocks, recv, c_idx, name):
    _, m, n = blocks.shape
    tr = _add_rows(m, n)

    def body(c_ref, a_ref, b_ref, o_ref, ob_ref):
        s = a_ref[...] + b_ref[...]
        o_ref[...] = s
        ob_ref[...] = s.astype(BF16)

    grid_spec = pltpu.PrefetchScalarGridSpec(
        num_scalar_prefetch=1, grid=(4, m // tr),
        in_specs=[pl.BlockSpec((1, tr, n), lambda j, i, c: (2 * j + c[0], i, 0)),
                  pl.BlockSpec((1, tr, n), lambda j, i, c: (j, i, 0))],
        out_specs=(pl.BlockSpec((1, tr, n), lambda j, i, c: (j, i, 0)),
                   pl.BlockSpec((1, tr, n), lambda j, i, c: (j, i, 0))))
    return pl.pallas_call(
        body, name=name, grid_spec=grid_spec,
        out_shape=(jax.ShapeDtypeStruct((4, m, n), F32), jax.ShapeDtypeStruct((4, m, n), BF16)),
        compiler_params=_params(("arbitrary", "arbitrary")),
    )(c_idx, blocks, recv)


def _add_chips(own, recv, name):
    m, n = own.shape
    tr = _add_rows(m, n)

    def body(a_ref, r_ref, o_ref):
        acc = a_ref[...]
        for k in range(3):
            acc = acc + r_ref[k].astype(F32)
        o_ref[...] = acc

    return pl.pallas_call(
        body, name=name, grid=(m // tr,),
        out_shape=jax.ShapeDtypeStruct((m, n), F32),
        in_specs=[pl.BlockSpec((tr, n), lambda i: (i, 0)), pl.BlockSpec((3, tr, n), lambda i: (0, i, 0))],
        out_specs=pl.BlockSpec((tr, n), lambda i: (i, 0)),
        compiler_params=_params(("arbitrary",)),
    )(own, recv)


def _row_spec(cols):
    return pl.BlockSpec((ROW_TILE, cols), lambda i: (i, 0))


def _mod_spec(tiles_per_seq):
    return pl.BlockSpec((1, 8, D_MODEL), lambda i: (i // tiles_per_seq, 0, 0))


def _table_spec(tiles_per_seq):
    return pl.BlockSpec((ROW_TILE, LANES), lambda i: (i % tiles_per_seq, 0))


def _fwd_in(x, mod, ln_g, ln_b, w_in, q_g, kv_g, w_uq, w_ukv, cos_t, sin_a, sin_b, seq):
    rows = x.shape[0]
    tm = min(2 * ROW_TILE, seq)
    tps = seq // tm

    def body(x_ref, mod_ref, g_ref, b_ref, win_ref, qg_ref, kvg_ref, wuq_ref, wukv_ref, cos_ref, sa_ref, sb_ref,
             x0_ref, h_ref, qkv_ref, lat_ref, qp_ref, kp_ref, vm_ref):
        def chain(rs):
            x0, _, _ = _ln_fwd(x_ref[rs, :], g_ref[...], b_ref[...])
            x0_ref[rs, :] = x0
            h = (x0 * (1.0 + mod_ref[0, 1:2, :]) + mod_ref[0, 0:1, :]).astype(BF16)
            h_ref[rs, :] = h
            yield
            proj = _dot(h, win_ref[...])
            yield
            qkv_ref[rs, :SB_W] = (proj[:, :SB_W] * SB_SCALE).astype(BF16)
            qkv_ref[rs, SB_W:] = proj[:, SB_W:3 * SB_W].astype(BF16)
            lat_ref[rs, :] = proj[:, 3 * SB_W:3 * SB_W + Q_RANK + KV_RANK]
            cq = proj[:, 3 * SB_W:3 * SB_W + Q_RANK]
            ckv = proj[:, 3 * SB_W + Q_RANK:3 * SB_W + Q_RANK + KV_RANK]
            kr = proj[:, D_IN_PAD - LANES:]
            cos, sa, sb = cos_ref[rs, :], sa_ref[rs, :], sb_ref[rs, :]
            cqn = (cq * lax.rsqrt(_mean(cq * cq) + RMS_EPS) * qg_ref[...]).astype(BF16)
            q_all = _dot(cqn, wuq_ref[...])
            ckvn = (ckv * lax.rsqrt(_mean(ckv * ckv) + RMS_EPS) * kvg_ref[...]).astype(BF16)
            kv = _dot(ckvn, wukv_ref[...])
            yield
            for hd in range(HEADS):
                sl = slice(hd * LANES, (hd + 1) * LANES)
                qp_ref[rs, sl] = _rope(q_all[:, sl], cos, sa, sb).astype(BF16)
            kr_rot = _rope(kr, cos, sa, sb)
            for hd in range(HEADS):
                sl = slice(hd * LANES, (hd + 1) * LANES)
                kp_ref[rs, sl] = (kv[:, sl] + kr_rot).astype(BF16)
            vm_ref[rs, :] = kv[:, HEADS * LANES:].astype(BF16)

        half = tm // 2
        _staggered([chain(slice(0, half)), chain(slice(half, tm))])

    row_spec = lambda cols: pl.BlockSpec((tm, cols), lambda i: (i, 0))
    table_spec = pl.BlockSpec((tm, LANES), lambda i: (i % tps, 0))
    outs = [(D_MODEL, F32), (D_MODEL, BF16), (3 * SB_W, BF16), (Q_RANK + KV_RANK, F32),
            (HEADS * LANES, BF16), (HEADS * LANES, BF16), (MLA_W, BF16)]
    return pl.pallas_call(
        body, name="fwd_in", grid=(rows // tm,),
        out_shape=tuple(jax.ShapeDtypeStruct((rows, n), dt) for n, dt in outs),
        in_specs=[row_spec(D_MODEL), pl.BlockSpec((1, 8, D_MODEL), lambda i: (i // tps, 0, 0)),
                  _const_spec((1, D_MODEL)), _const_spec((1, D_MODEL)),
                  _const_spec(w_in.shape), _const_spec((1, Q_RANK)), _const_spec((1, KV_RANK)),
                  _const_spec(w_uq.shape), _const_spec(w_ukv.shape), table_spec, table_spec, table_spec],
        out_specs=tuple(row_spec(n) for n, _ in outs),
        compiler_params=_params(("arbitrary",)),
    )(x, mod, ln_g, ln_b, w_in, q_g, kv_g, w_uq, w_ukv, cos_t, sin_a, sin_b)


HALF = 512
SHARD = 1024


def _mlp_weight_specs():
    return [pl.BlockSpec((8, HALF, SHARD), lambda i: (0, 0, 0), pipeline_mode=pl.Buffered(1)),
            pl.BlockSpec((8, HALF, SHARD), lambda i: (0, 1, 0), pipeline_mode=pl.Buffered(1))]


def _fwd_out(sb_y, mla_y, x0, mod, w_o, ln_g, ln_b, g_mlp, seq):
    rows = x0.shape[0]
    tm = ROW_TILE
    tps = seq // tm

    def body(sb_ref, ml_ref, x0_ref, mod_ref, wo_ref, g_ref, b_ref, wu_ref, wd_ref,
             mix_ref, y1_ref, h2_ref, u_ref, ff_ref, y2_ref):
        mix = _dot(sb_ref[...], wo_ref[:SB_W, :]) + _dot(ml_ref[...].astype(BF16), wo_ref[SB_W:, :])
        mix_ref[...] = mix
        y1 = ALPHA * x0_ref[...] + (1.0 + mod_ref[0, 2:3, :]) * mix
        y1_ref[...] = y1
        x1, _, _ = _ln_fwd(y1, g_ref[...], b_ref[...])
        h2 = (x1 * (1.0 + mod_ref[0, 4:5, :]) + mod_ref[0, 3:4, :]).astype(BF16)
        h2_ref[...] = h2
        h_lo, h_hi = h2[:, :HALF], h2[:, HALF:]
        ff = jnp.zeros((tm, D_MODEL), F32)
        for chip in range(4):
            u = _dot(h_lo, wu_ref[2 * chip]) + _dot(h_hi, wu_ref[2 * chip + 1])
            u_ref[:, chip * SHARD:(chip + 1) * SHARD] = u.astype(BF16)
            act = jnp.square(jnp.maximum(u, 0.0)).astype(BF16)
            ff = ff + _dot(act[:, :HALF], wd_ref[2 * chip]) + _dot(act[:, HALF:], wd_ref[2 * chip + 1])
        ff_ref[...] = ff
        y2_ref[...] = ALPHA * x1 + (1.0 + mod_ref[0, 5:6, :]) * ff

    outs = [(D_MODEL, F32), (D_MODEL, F32), (D_MODEL, BF16), (D_FF, BF16), (D_MODEL, F32), (D_MODEL, F32)]
    return pl.pallas_call(
        body, name="fwd_out", grid=(rows // tm,),
        out_shape=tuple(jax.ShapeDtypeStruct((rows, n), dt) for n, dt in outs),
        in_specs=[_row_spec(SB_W), _row_spec(MLA_W), _row_spec(D_MODEL), _mod_spec(tps), _const_spec(w_o.shape),
                  _const_spec((1, D_MODEL)), _const_spec((1, D_MODEL))] + _mlp_weight_specs(),
        out_specs=tuple(_row_spec(n) for n, _ in outs),
        compiler_params=_params(("arbitrary",)),
    )(sb_y, mla_y, x0, mod, w_o, ln_g, ln_b, g_mlp, g_mlp)


def _staggered(chains):
    live = []
    for chain in chains:
        live.append(chain)
        live = [c for c in live if next(c, StopIteration) is not StopIteration]
    while live:
        live = [c for c in live if next(c, StopIteration) is not StopIteration]


def _acc_spec(rows=8, cols=D_MODEL):
    return pl.BlockSpec((rows, cols), lambda i: (0, 0))


def _bwd_out(y2, tgt, ff, u, y1, mix, mod, ln2_g, ln2_b, ln1_g, ln1_b, g_mlp, w_o, seq):
    rows = y2.shape[0]
    nb = rows // seq
    tm = ROW_TILE
    tps = seq // tm

    def body(y2_ref, t_ref, ff_ref, u_ref, y1_ref, mix_ref, mod_ref, g2_ref, b2_ref, g_ref, b_ref, wu_ref, wd_ref,
             wo_ref, dy1_ref, dmix_ref, do_ref, dff_ref, du_ref, acc_ref, dmod_ref):
        i = pl.program_id(0)

        @pl.when(i == 0)
        def _():
            acc_ref[...] = jnp.zeros_like(acc_ref)

        @pl.when(i % tps == 0)
        def _():
            dmod_ref[...] = jnp.zeros_like(dmod_ref)

        g2 = g2_ref[...]
        x2, xhat2, rstd2 = _ln_fwd(y2_ref[...], g2, b2_ref[...])
        err = x2 - t_ref[...]
        dx2 = err * (1.0 / D_MODEL)
        acc_ref[0:1, :] += _rowsum(dx2 * xhat2)
        acc_ref[1:2, :] += _rowsum(dx2)
        acc_ref[2:3, :] += _rowsum(err * err) * (0.5 / D_MODEL)
        dy2 = _ln_bwd(dx2, xhat2, rstd2, g2)
        dmod_ref[0, 5:6, :] += _rowsum(dy2 * ff_ref[...])
        dff = ((1.0 + mod_ref[0, 5:6, :]) * dy2).astype(BF16)
        dff_ref[...] = dff
        for blk in range(8):
            cols = slice(blk * HALF, (blk + 1) * HALF)
            da = _dot(dff, wd_ref[blk], NT)
            du_ref[:, cols] = (da * (2.0 * jnp.maximum(u_ref[:, cols].astype(F32), 0.0))).astype(BF16)

        g = g_ref[...]
        x1, xhat, rstd = _ln_fwd(y1_ref[...], g, b_ref[...])
        halves = []
        for half in range(2):
            acc = jnp.zeros((tm, HALF), F32)
            for chip in range(4):
                acc = acc + _dot(du_ref[:, chip * SHARD:(chip + 1) * SHARD], wu_ref[2 * chip + half], NT)
            halves.append(acc)
        dh2 = jnp.concatenate(halves, axis=1)
        dmod_ref[0, 3:4, :] += _rowsum(dh2)
        dmod_ref[0, 4:5, :] += _rowsum(dh2 * x1)
        dx1 = ALPHA * dy2 + dh2 * (1.0 + mod_ref[0, 4:5, :])
        acc_ref[3:4, :] += _rowsum(dx1 * xhat)
        acc_ref[4:5, :] += _rowsum(dx1)
        dy1 = _ln_bwd(dx1, xhat, rstd, g)
        dy1_ref[...] = dy1
        dmod_ref[0, 2:3, :] += _rowsum(dy1 * mix_ref[...])
        dmix = ((1.0 + mod_ref[0, 2:3, :]) * dy1).astype(BF16)
        dmix_ref[...] = dmix
        do_ref[...] = _dot(dmix, wo_ref[...], NT)

    outs = [(D_MODEL, F32), (D_MODEL, BF16), (D_MODEL, F32), (D_MODEL, BF16), (D_FF, BF16)]
    return pl.pallas_call(
        body, name="bwd_out", grid=(rows // tm,),
        out_shape=tuple(jax.ShapeDtypeStruct((rows, n), dt) for n, dt in outs)
        + (jax.ShapeDtypeStruct((8, D_MODEL), F32), jax.ShapeDtypeStruct((nb, 8, D_MODEL), F32)),
        in_specs=[_row_spec(D_MODEL), _row_spec(D_MODEL), _row_spec(D_MODEL), _row_spec(D_FF), _row_spec(D_MODEL),
                  _row_spec(D_MODEL), _mod_spec(tps), _const_spec((1, D_MODEL)), _const_spec((1, D_MODEL)),
                  _const_spec((1, D_MODEL)), _const_spec((1, D_MODEL))] + _mlp_weight_specs()
        + [_const_spec(w_o.shape)],
        out_specs=tuple(_row_spec(n) for n, _ in outs) + (_acc_spec(), _mod_spec(tps)),
        compiler_params=_params(("arbitrary",)),
    )(y2, tgt, ff, u, y1, mix, mod, ln2_g, ln2_b, ln1_g, ln1_b, g_mlp, g_mlp, w_o)


def _bwd_in(dqp, dkp, dvm, dq_sb, dk_sb, dv_sb, lat, x, dy1, mod, ln_g, ln_b, w_in, q_g, kv_g, w_uq, w_ukv,
            cos_t, sin_a, sin_b, seq):
    rows = x.shape[0]
    nb = rows // seq
    tm = min(2 * ROW_TILE, seq)
    tps = seq // tm
    n_lat = Q_RANK + KV_RANK

    def body(dqp_ref, dkp_ref, dvm_ref, dqs_ref, dks_ref, dvs_ref, lat_ref, x_ref, dy1_ref, mod_ref,
             g_ref, b_ref, win_ref, qg_ref, kvg_ref, wuq_ref, wukv_ref, cos_ref, sa_ref, sb_ref,
             dx_ref, dproj_ref, dqall_ref, dkv_ref, latn_ref, acc_ref, accl_ref, dmod_ref):
        i = pl.program_id(0)

        @pl.when(i == 0)
        def _():
            acc_ref[...] = jnp.zeros_like(acc_ref)
            accl_ref[...] = jnp.zeros_like(accl_ref)

        @pl.when(i % tps == 0)
        def _():
            dmod_ref[...] = jnp.zeros_like(dmod_ref)

        def chain(rs):
            n_rows = rs.stop - rs.start
            cos, sa, sb = cos_ref[rs, :], sa_ref[rs, :], sb_ref[rs, :]
            lane = lax.broadcasted_iota(jnp.int32, (n_rows, LANES), 1)
            for hd in range(HEADS):
                sl = slice(hd * LANES, (hd + 1) * LANES)
                dqall_ref[rs, sl] = _rope_t(dqp_ref[rs, sl], cos, sa, sb).astype(BF16)
            dkr = jnp.zeros((n_rows, LANES), F32)
            for hd in range(HEADS):
                sl = slice(hd * LANES, (hd + 1) * LANES)
                dk = dkp_ref[rs, sl]
                dkr = dkr + dk
                dkv_ref[rs, sl] = jnp.where(lane < NOPE, dk, 0.0).astype(BF16)
            dkv_ref[rs, HEADS * LANES:] = dvm_ref[rs, :].astype(BF16)
            dkr = _rope_t(jnp.where(lane >= NOPE, dkr, 0.0), cos, sa, sb)
            yield
            dcqn = _dot(dqall_ref[rs, :], wuq_ref[...], NT)
            dckvn = _dot(dkv_ref[rs, :], wukv_ref[...], NT)
            yield
            cq = lat_ref[rs, :Q_RANK]
            qg = qg_ref[...]
            rq = lax.rsqrt(_mean(cq * cq) + RMS_EPS)
            cqn = cq * rq
            latn_ref[rs, :Q_RANK] = (cqn * qg).astype(BF16)
            accl_ref[0:1, :Q_RANK] += _rowsum(dcqn * cqn)
            dqg = dcqn * qg
            dcq = rq * (dqg - cqn * _mean(dqg * cqn))
            ckv = lat_ref[rs, Q_RANK:]
            kvg = kvg_ref[...]
            rkv = lax.rsqrt(_mean(ckv * ckv) + RMS_EPS)
            ckvn = ckv * rkv
            latn_ref[rs, Q_RANK:] = (ckvn * kvg).astype(BF16)
            accl_ref[1:2, :KV_RANK] += _rowsum(dckvn * ckvn)
            dkg = dckvn * kvg
            dckv = rkv * (dkg - ckvn * _mean(dkg * ckvn))
            dproj_ref[rs, :SB_W] = dqs_ref[rs, :]
            dproj_ref[rs, SB_W:2 * SB_W] = dks_ref[rs, :].astype(BF16)
            dproj_ref[rs, 2 * SB_W:3 * SB_W] = dvs_ref[rs, :].astype(BF16)
            dproj_ref[rs, 3 * SB_W:3 * SB_W + Q_RANK] = dcq.astype(BF16)
            dproj_ref[rs, 3 * SB_W + Q_RANK:3 * SB_W + n_lat] = dckv.astype(BF16)
            dproj_ref[rs, D_IN_PAD - LANES:] = dkr.astype(BF16)
            yield
            dh = _dot(dproj_ref[rs, :], win_ref[...], NT)
            yield
            g = g_ref[...]
            x0, xhat, rstd = _ln_fwd(x_ref[rs, :], g, b_ref[...])
            dmod_ref[0, 0:1, :] += _rowsum(dh)
            dmod_ref[0, 1:2, :] += _rowsum(dh * x0)
            dx0 = ALPHA * dy1_ref[rs, :] + dh * (1.0 + mod_ref[0, 1:2, :])
            acc_ref[0:1, :] += _rowsum(dx0 * xhat)
            acc_ref[1:2, :] += _rowsum(dx0)
            dx_ref[rs, :] = _ln_bwd(dx0, xhat, rstd, g)

        _staggered([chain(slice(0, tm // 2)), chain(slice(tm // 2, tm))])

    row_spec = lambda cols: pl.BlockSpec((tm, cols), lambda i: (i, 0))
    table_spec = pl.BlockSpec((tm, LANES), lambda i: (i % tps, 0))
    mod_spec = pl.BlockSpec((1, 8, D_MODEL), lambda i: (i // tps, 0, 0))
    outs = [(D_MODEL, F32), (D_IN_PAD, BF16), (HEADS * LANES, BF16), (HEADS * LANES + MLA_W, BF16), (n_lat, BF16)]
    return pl.pallas_call(
        body, name="bwd_in", grid=(rows // tm,),
        out_shape=tuple(jax.ShapeDtypeStruct((rows, n), dt) for n, dt in outs)
        + (jax.ShapeDtypeStruct((8, D_MODEL), F32), jax.ShapeDtypeStruct((8, Q_RANK), F32),
           jax.ShapeDtypeStruct((nb, 8, D_MODEL), F32)),
        in_specs=[row_spec(HEADS * LANES), row_spec(HEADS * LANES), row_spec(MLA_W),
                  row_spec(SB_W), row_spec(SB_W), row_spec(SB_W), row_spec(n_lat),
                  row_spec(D_MODEL), row_spec(D_MODEL), mod_spec,
                  _const_spec((1, D_MODEL)), _const_spec((1, D_MODEL)), _const_spec(w_in.shape),
                  _const_spec((1, Q_RANK)), _const_spec((1, KV_RANK)), _const_spec(w_uq.shape),
                  _const_spec(w_ukv.shape), table_spec, table_spec, table_spec],
        out_specs=tuple(row_spec(n) for n, _ in outs) + (_acc_spec(), _acc_spec(8, Q_RANK), mod_spec),
        compiler_params=_params(("arbitrary",)),
    )(dqp, dkp, dvm, dq_sb, dk_sb, dv_sb, lat, x, dy1, mod, ln_g, ln_b, w_in, q_g, kv_g, w_uq, w_ukv,
      cos_t, sin_a, sin_b)


def _wgrad(a, b, name, tm=512, tn=1024, tk=2048, ex=None):
    rows, m = a.shape
    n = b.shape[1]
    tm, tn, tk = min(tm, m), min(tn, n), min(tk, rows)
    if m % tm:
        tm = m
    if n % tn:
        tn = n

    def body(a_ref, b_ref, o_ref):
        @pl.when(pl.program_id(2) == 0)
        def _():
            o_ref[...] = jnp.zeros_like(o_ref)

        o_ref[...] += _dot(a_ref[...].astype(BF16), b_ref[...].astype(BF16), TN)

    res = _carrier_call(
        body, ex, name, (m // tm, n // tn, rows // tk), [a, b],
        [pl.BlockSpec((tk, tm), lambda i, j, k: (k, i)), pl.BlockSpec((tk, tn), lambda i, j, k: (k, j))],
        [jax.ShapeDtypeStruct((m, n), F32)], [pl.BlockSpec((tm, tn), lambda i, j, k: (i, j))])
    return res[0] if ex is None else res


def _wgrad_packed(a, b, name, block_of, row_block, split=1, pre=None, into=None, tk=2048):
    rows, m = a.shape
    n = b.shape[1]
    tm = HALF
    part = tm // split
    tk = min(tk, rows)
    shape = jax.ShapeDtypeStruct((8, GROUP_MLP[0], PACK_COLS), F32)

    def body(a_ref, b_ref, *rest):
        o_ref = rest[-1]

        @pl.when(pl.program_id(2) == 0)
        def _():
            o_ref[...] = jnp.zeros_like(o_ref)

        av = a_ref[...]
        if pre == "relu2":
            av = jnp.square(jnp.maximum(av.astype(F32), 0.0))
        prod = _dot(av.astype(BF16), b_ref[...].astype(BF16), TN)
        for s in range(split):
            o_ref[s] += prod[s * part:(s + 1) * part]

    in_specs = [pl.BlockSpec((tk, tm), lambda i, j, k: (k, i)), pl.BlockSpec((tk, SHARD), lambda i, j, k: (k, j))]
    operands = [a, b]
    if into is not None:
        in_specs.append(pl.BlockSpec(memory_space=pl.ANY))
        operands.append(into)
    return pl.pallas_call(
        body, name=name, grid=(m // tm, n // SHARD, rows // tk), out_shape=shape,
        in_specs=in_specs,
        out_specs=pl.BlockSpec((split, part, SHARD), lambda i, j, k: (block_of(i, j), row_block, 0)),
        input_output_aliases={} if into is None else {2: 0},
        compiler_params=_params(("arbitrary", "arbitrary", "arbitrary")),
    )(*operands)


def _pair(pp):
    return slice(pp * LANES, (pp + 1) * LANES)


def _head_mask(lane, hh):
    return jnp.where((lane >= 64) if hh else (lane < 64), 1.0, 0.0).astype(BF16)


def _tri(t, kind):
    s = lax.broadcasted_iota(jnp.int32, (t, t), 0)
    j = lax.broadcasted_iota(jnp.int32, (t, t), 1)
    one = jnp.where(j > s if kind == "later" else j < s, 1.0, 0.0).astype(BF16)
    return jnp.concatenate([one, one], axis=1)


def _split_dot(tri2, v):
    hi = v.astype(BF16)
    lo = (v - hi.astype(F32)).astype(BF16)
    return _dot(tri2, jnp.concatenate([hi, lo], axis=0))


def _sb_logits(z, valid):
    log_keep = -(jnp.maximum(z, 0.0) + jnp.log(1.0 + jnp.exp2(jnp.abs(z) * NEG_LOG2E)))
    log_beta = z + log_keep
    if valid is not None:
        log_keep = jnp.where(valid, log_keep, 0.0)
    return log_keep, log_beta


def _carrier_call(body, ex, name, grid, operands, in_specs, out_shapes, out_specs, scratch=()):
    n_in, n_out = len(operands), len(out_shapes)
    total = grid[0] * grid[1] * grid[2]
    any_spec = pl.BlockSpec(memory_space=pl.ANY)

    def carrier(*refs):
        ins, outs, (start, middle, finish) = _carried(ex, refs, n_in, n_out, len(scratch))
        step = (pl.program_id(0) * grid[1] + pl.program_id(1)) * grid[2] + pl.program_id(2)
        pl.when(step == 0)(start)
        pl.when(step == total // 2)(middle)
        body(*ins, *outs)
        pl.when(step == total - 1)(finish)

    carried = ex is not None
    return pl.pallas_call(
        carrier if carried else body, name=name, grid=grid,
        out_shape=tuple(out_shapes) + ((ex.out_shape,) if carried else ()),
        in_specs=list(in_specs) + ([any_spec] if carried else []),
        out_specs=tuple(out_specs) + ((any_spec,) if carried else ()),
        scratch_shapes=list(scratch) + (ex.scratch if carried else []),
        compiler_params=_params(("arbitrary", "arbitrary", "arbitrary")),
    )(*operands, *([ex.operand] if carried else []))


def _sb_fwd(qkv, seq, ex=None):
    rows = qkv.shape[0]
    nb = rows // seq
    t = min(ATTN_TILE, seq)
    nq = seq // t
    assert nq <= CAR_SLOTS, (seq, t)
    ap = ATTN_PAIRS
    width = ap * LANES
    groups = SB_W // width
    hds = [(pp, hh) for pp in range(ap) for hh in range(2)]

    def body(q_ref, k_ref, v_ref, tri_ref, o_ref, car_ref, acc_ref):
        i = pl.program_id(2)
        lane = lax.broadcasted_iota(jnp.int32, (t, LANES), 1)
        key = lax.broadcasted_iota(jnp.int32, (t, t), 0)
        qry = lax.broadcasted_iota(jnp.int32, (t, t), 1)
        strict = key < qry
        tri = tri_ref[...]
        masks = [_head_mask(lane, hh) for hh in range(2)]
        qms = [q_ref[:, _pair(pp)] * masks[hh] for pp, hh in hds]
        acc_ref[...] = jnp.zeros_like(acc_ref)
        car_ref[...] = jnp.zeros_like(car_ref)

        def step(kb, c_sums, valid):
            start = pl.multiple_of(kb * t, t)
            kss = [k_ref[pl.ds(start, t), _pair(pp)] for pp in range(ap)]
            vss = [v_ref[pl.ds(start, t), _pair(pp)] for pp in range(ap)]
            zs = [_dot(kss[pp], qms[n], NT) for n, (pp, _) in enumerate(hds)]
            logs = [_sb_logits(z, valid) for z in zs]
            sufs = [_split_dot(tri, lg[0]) for lg in logs]
            new_sums = []
            for n, (pp, hh) in enumerate(hds):
                log_keep, log_beta = logs[n]
                w = jnp.exp(log_beta + sufs[n] + c_sums[n])
                if valid is not None:
                    w = jnp.where(valid, w, 0.0)
                acc_ref[pp] += _dot(vss[pp] * masks[hh], w.astype(BF16), TN)
                car_ref[0, pl.ds(n * CAR_SLOTS + kb, 1), :] = c_sums[n]
                new_sums.append(c_sums[n] + sufs[n][0:1, :] + log_keep[0:1, :])
            return tuple(new_sums)

        c_sums = step(i, tuple(jnp.zeros((1, t), F32) for _ in hds), strict)
        lax.fori_loop(0, i, lambda j, cr: step(i - 1 - j, cr, None), c_sums)
        for pp in range(ap):
            o_ref[:, _pair(pp)] = acc_ref[pp].T.astype(BF16)

    qspec = pl.BlockSpec((t, width), lambda b, p, i: (b * nq + i, p))
    car_rows = len(hds) * CAR_SLOTS
    return _carrier_call(
        body, ex, "sb_fwd", (nb, groups, nq),
        [qkv, qkv, qkv, _tri(t, "later")],
        [qspec,
         pl.BlockSpec((seq, width), lambda b, p, i: (b, groups + p)),
         pl.BlockSpec((seq, width), lambda b, p, i: (b, 2 * groups + p)),
         _const_spec((t, 2 * t))],
        [jax.ShapeDtypeStruct((rows, SB_W), BF16), jax.ShapeDtypeStruct((nb * nq, HEADS * CAR_SLOTS, t), F32)],
        [qspec, pl.BlockSpec((1, car_rows, t), lambda b, p, i: (b * nq + i, p, 0))],
        scratch=[pltpu.VMEM((ap, LANES, t), F32)])


def _sb_bwd(qkv, d_out, cars, seq, ex=None):
    rows = qkv.shape[0]
    nb = rows // seq
    t = min(ATTN_TILE, seq)
    nq = seq // t
    ap = ATTN_PAIRS
    width = ap * LANES
    groups = SB_W // width
    hds = [(pp, hh) for pp in range(ap) for hh in range(2)]

    def body(q_ref, k_ref, v_ref, do_ref, car_ref, tri_ref, pre_ref, dq_ref, dk_ref, dv_ref, dq_acc):
        i = pl.program_id(2)

        @pl.when(i == 0)
        def _():
            dk_ref[...] = jnp.zeros_like(dk_ref)
            dv_ref[...] = jnp.zeros_like(dv_ref)

        lane = lax.broadcasted_iota(jnp.int32, (t, LANES), 1)
        key = lax.broadcasted_iota(jnp.int32, (t, t), 0)
        qry = lax.broadcasted_iota(jnp.int32, (t, t), 1)
        strict = key < qry
        tri, pre = tri_ref[...], pre_ref[...]
        masks = [_head_mask(lane, hh) for hh in range(2)]
        qms = [q_ref[:, _pair(pp)] * masks[hh] for pp, hh in hds]
        doms = [do_ref[:, _pair(pp)].astype(BF16) * masks[hh] for pp, hh in hds]
        dq_acc[...] = jnp.zeros_like(dq_acc)

        def step(kb, g_pres, valid):
            start = pl.multiple_of(kb * t, t)
            kss = [k_ref[pl.ds(start, t), _pair(pp)] for pp in range(ap)]
            vss = [v_ref[pl.ds(start, t), _pair(pp)] for pp in range(ap)]
            zs = [_dot(kss[pp], qms[n], NT) for n, (pp, _) in enumerate(hds)]
            dws = [_dot(vss[pp], doms[n], NT) for n, (pp, _) in enumerate(hds)]
            logs = [_sb_logits(z, valid) for z in zs]
            sufs = [_split_dot(tri, lg[0]) for lg in logs]
            ws, gs = [], []
            for n in range(len(hds)):
                c_sum = car_ref[0, pl.ds(n * CAR_SLOTS + kb, 1), :]
                w = jnp.exp(logs[n][1] + sufs[n] + c_sum)
                if valid is not None:
                    w = jnp.where(valid, w, 0.0)
                ws.append(w)
                gs.append(dws[n] * w)
            pres = [_split_dot(pre, gs[n]) for n in range(len(hds))]
            befores = [g_pres[n] + pres[n] for n in range(len(hds))]
            for pp in range(ap):
                a, b = 2 * pp, 2 * pp + 1
                dv_ref[pl.ds(start, t), _pair(pp)] += _dot(ws[a].astype(BF16), doms[a]) + _dot(ws[b].astype(BF16), doms[b])
            dzbs = []
            for n in range(len(hds)):
                beta = jnp.exp(logs[n][1])
                dz = gs[n] * (1.0 - beta) - beta * befores[n]
                if valid is not None:
                    dz = jnp.where(valid, dz, 0.0)
                dzbs.append(dz.astype(BF16))
            for pp in range(ap):
                a, b = 2 * pp, 2 * pp + 1
                dq_acc[pp] += _dot(dzbs[a], kss[pp] * masks[0], TN) + _dot(dzbs[b], kss[pp] * masks[1], TN)
                dk_ref[pl.ds(start, t), _pair(pp)] += _dot(dzbs[a], qms[a]) + _dot(dzbs[b], qms[b])
            return tuple(g_pres[n] + pres[n][t - 1:t, :] + gs[n][t - 1:t, :] for n in range(len(hds)))

        g_pres = lax.fori_loop(0, i, lambda kb, cr: step(kb, cr, None), tuple(jnp.zeros((1, t), F32) for _ in hds))
        step(i, g_pres, strict)
        for pp in range(ap):
            dq_ref[:, _pair(pp)] = (dq_acc[pp] * SB_SCALE).astype(BF16)

    qspec = pl.BlockSpec((t, width), lambda b, p, i: (b * nq + i, p))
    kspec_out = pl.BlockSpec((seq, width), lambda b, p, i: (b, p))
    car_rows = len(hds) * CAR_SLOTS
    return _carrier_call(
        body, ex, "sb_bwd", (nb, groups, nq),
        [qkv, qkv, qkv, d_out, cars, _tri(t, "later"), _tri(t, "earlier")],
        [qspec,
         pl.BlockSpec((seq, width), lambda b, p, i: (b, groups + p)),
         pl.BlockSpec((seq, width), lambda b, p, i: (b, 2 * groups + p)),
         qspec, pl.BlockSpec((1, car_rows, t), lambda b, p, i: (b * nq + i, p, 0)),
         _const_spec((t, 2 * t)), _const_spec((t, 2 * t))],
        [jax.ShapeDtypeStruct((rows, SB_W), BF16), jax.ShapeDtypeStruct((rows, SB_W), F32),
         jax.ShapeDtypeStruct((rows, SB_W), F32)],
        [qspec, kspec_out, kspec_out],
        scratch=[pltpu.VMEM((ap, t, LANES), F32)])


def _mla_scores(ks, qh, allowed):
    s = _dot(ks, qh, NT) * (MLA_SCALE * -NEG_LOG2E)
    if allowed is not None:
        s = jnp.where(allowed, s, jnp.finfo(F32).min)
    return s


def _mla_fwd(qp, kp, vm, seq, ex=None, chunk=64):
    rows = qp.shape[0]
    nb = rows // seq
    t = min(ATTN_TILE, seq)
    nq = seq // t
    shift = int(math.log2(chunk))
    ap = ATTN_PAIRS
    width = ap * LANES
    groups = MLA_W // width
    hds = [(pp, hh) for pp in range(ap) for hh in range(2)]

    def body(q_ref, k_ref, v_ref, o_ref, lse_ref, acc_ref):
        i = pl.program_id(2)
        lane = lax.broadcasted_iota(jnp.int32, (t, LANES), 1)
        key = lax.broadcasted_iota(jnp.int32, (t, t), 0)
        qry = lax.broadcasted_iota(jnp.int32, (t, t), 1)
        allowed_diag = jnp.right_shift(key, shift) <= jnp.right_shift(qry, shift)
        masks = [_head_mask(lane, hh) for hh in range(2)]
        qhs = [q_ref[:, _pair(n)] for n in range(len(hds))]
        acc_ref[...] = jnp.zeros_like(acc_ref)

        def step(kb, carry, allowed):
            start = pl.multiple_of(kb * t, t)
            vss = [v_ref[pl.ds(start, t), _pair(pp)] for pp in range(ap)]
            scores = [_mla_scores(k_ref[pl.ds(start, t), _pair(n)], qhs[n], allowed) for n in range(len(hds))]
            new = []
            for n, (pp, hh) in enumerate(hds):
                m_run, l_run = carry[n]
                s = scores[n]
                m_new = jnp.maximum(m_run, jnp.max(s, axis=0, keepdims=True))
                p = jnp.exp2(s - m_new)
                scale = jnp.exp2(m_run - m_new)
                l_run = scale * l_run + jnp.sum(p, axis=0, keepdims=True)
                acc_ref[n] = scale * acc_ref[n] + _dot(vss[pp] * masks[hh], p.astype(BF16), TN)
                new.append((m_new, l_run))
            return tuple(new)

        init = (jnp.full((1, t), jnp.finfo(F32).min, F32), jnp.zeros((1, t), F32))
        carry = step(i, tuple(init for _ in hds), allowed_diag)
        carry = lax.fori_loop(0, i, lambda kb, cr: step(kb, cr, None), carry)
        lse_rows = []
        for pp in range(ap):
            out_t = jnp.zeros((LANES, t), F32)
            for hh in range(2):
                m_run, l_run = carry[2 * pp + hh]
                out_t = out_t + acc_ref[2 * pp + hh] / l_run
                lse_rows.append(m_run + jnp.log(l_run) * -NEG_LOG2E)
            o_ref[:, _pair(pp)] = out_t.T
        lse_t = jnp.concatenate(lse_rows + [jnp.zeros((LANES - len(hds), t), F32)], axis=0)
        lse_ref[...] = jnp.zeros_like(lse_ref)
        lse_ref[:, _pair(0)] = lse_t.T

    ospec = pl.BlockSpec((t, width), lambda b, p, i: (b * nq + i, p))
    return _carrier_call(
        body, ex, "mla_fwd", (nb, groups, nq), [qp, kp, vm],
        [pl.BlockSpec((t, 2 * width), lambda b, p, i: (b * nq + i, p)),
         pl.BlockSpec((seq, 2 * width), lambda b, p, i: (b, p)),
         pl.BlockSpec((seq, width), lambda b, p, i: (b, p))],
        [jax.ShapeDtypeStruct((rows, MLA_W), F32), jax.ShapeDtypeStruct((rows, MLA_W), F32)],
        [ospec, ospec], scratch=[pltpu.VMEM((len(hds), LANES, t), F32)])


def _mla_bwd(qp, kp, vm, d_out, out, lse, seq, ex=None, chunk=64):
    rows = qp.shape[0]
    nb = rows // seq
    t = min(ATTN_TILE, seq)
    nq = seq // t
    shift = int(math.log2(chunk))
    ap = ATTN_PAIRS
    width = ap * LANES
    groups = MLA_W // width
    hds = [(pp, hh) for pp in range(ap) for hh in range(2)]
    nh = len(hds)

    def body(q_ref, k_ref, v_ref, do_ref, o_ref, lse_ref, dq_ref, dk_ref, dv_ref):
        i = pl.program_id(2)

        @pl.when(i == 0)
        def _():
            dk_ref[...] = jnp.zeros_like(dk_ref)
            dv_ref[...] = jnp.zeros_like(dv_ref)

        lane = lax.broadcasted_iota(jnp.int32, (t, LANES), 1)
        key = lax.broadcasted_iota(jnp.int32, (t, t), 0)
        qry = lax.broadcasted_iota(jnp.int32, (t, t), 1)
        allowed_diag = jnp.right_shift(key, shift) <= jnp.right_shift(qry, shift)
        qhs = [q_ref[:, _pair(n)] for n in range(nh)]
        lse_t = lse_ref[:, _pair(0)].T
        doms, deltas, lse_hs = [], [], []
        for pp in range(ap):
            do = do_ref[:, _pair(pp)]
            d_o_t = (do * o_ref[:, _pair(pp)]).T
            for hh in range(2):
                doms.append(do.astype(BF16) * _head_mask(lane, hh))
                deltas.append(jnp.sum(d_o_t[hh * 64:(hh + 1) * 64], axis=0, keepdims=True))
                lse_hs.append(lse_t[2 * pp + hh:2 * pp + hh + 1])

        dq_ref[...] = jnp.zeros_like(dq_ref)

        def step(kb, allowed):
            start = pl.multiple_of(kb * t, t)
            vss = [v_ref[pl.ds(start, t), _pair(pp)] for pp in range(ap)]
            kss = [k_ref[pl.ds(start, t), _pair(n)] for n in range(nh)]
            scores = [_mla_scores(kss[n], qhs[n], allowed) for n in range(nh)]
            dps = [_dot(vss[pp], doms[n], NT) for n, (pp, _) in enumerate(hds)]
            ps = [jnp.exp2(scores[n] - lse_hs[n]) for n in range(nh)]
            dss = [(ps[n] * (dps[n] - deltas[n]) * MLA_SCALE).astype(BF16) for n in range(nh)]
            for pp in range(ap):
                a, b = 2 * pp, 2 * pp + 1
                dv_ref[pl.ds(start, t), _pair(pp)] += _dot(ps[a].astype(BF16), doms[a]) + _dot(ps[b].astype(BF16), doms[b])
            for n in range(nh):
                dk_ref[pl.ds(start, t), _pair(n)] += _dot(dss[n], qhs[n])
                dq_ref[:, _pair(n)] += _dot(dss[n], kss[n], TN)

        def off_diagonal(kb, nothing):
            step(kb, None)
            return nothing

        lax.fori_loop(0, i, off_diagonal, 0)
        step(i, allowed_diag)

    ospec = pl.BlockSpec((t, width), lambda b, p, i: (b * nq + i, p))
    return _carrier_call(
        body, ex, "mla_bwd", (nb, groups, nq), [qp, kp, vm, d_out, out, lse],
        [pl.BlockSpec((t, 2 * width), lambda b, p, i: (b * nq + i, p)),
         pl.BlockSpec((seq, 2 * width), lambda b, p, i: (b, p)),
         pl.BlockSpec((seq, width), lambda b, p, i: (b, p)),
         pl.BlockSpec((t, width), lambda b, p, i: (b * nq + i, groups + p)),
         ospec, ospec],
        [jax.ShapeDtypeStruct((rows, HEADS * LANES), F32), jax.ShapeDtypeStruct((rows, HEADS * LANES), F32),
         jax.ShapeDtypeStruct((rows, MLA_W), F32)],
        [pl.BlockSpec((t, 2 * width), lambda b, p, i: (b * nq + i, p)),
         pl.BlockSpec((seq, 2 * width), lambda b, p, i: (b, p)),
         pl.BlockSpec((seq, width), lambda b, p, i: (b, p))])


PACK_COLS = 1024
PACK_ALIGN = 16
GROUP_IN = (384, ((1024, 552, 1), (384, 192, 1), (256, 256, 1)))
GROUP_MLP = (1152, ((1024, 1024, 1), (1024, 1024, 0), (256, 1024, 0)))


def _pack_rows(r, c):
    return (r // 2) * c // PACK_COLS


def _slot_rows(r, c):
    return -(-_pack_rows(r, c) // PACK_ALIGN) * PACK_ALIGN


def _join_slots(parts, group):
    total, weights = group
    padded = [jnp.pad(p, ((0, 0), (0, _slot_rows(r, c) - p.shape[1]), (0, 0))) for p, (r, c, _) in zip(parts, weights)]
    used = sum(_slot_rows(r, c) for r, c, _ in weights)
    if total > used:
        padded.append(jnp.zeros((parts[0].shape[0], total - used, PACK_COLS), parts[0].dtype))
    return jnp.concatenate(padded, axis=1)


def _split_slots(packed, group):
    out, at = [], 0
    for r, c, _ in group[1]:
        out.append(packed[:, at:at + _pack_rows(r, c), :])
        at += _slot_rows(r, c)
    return out


def _pack_halves(shards, group):
    return _join_slots([s.reshape(2, _pack_rows(r, c), PACK_COLS) for s, (r, c, _) in zip(shards, group[1])], group)


def _unpack_half(packed, group):
    return [p.reshape(r // 2, c) for p, (r, c, _) in zip(_split_slots(packed[None], group), group[1])]


def _unpack_full(gathered, group):
    out = []
    for p, (r, c, axis) in zip(_split_slots(gathered, group), group[1]):
        shards = p.reshape(4, r, c)
        out.append(shards.reshape(4 * r, c) if axis == 0 else jnp.moveaxis(shards, 0, 1).reshape(r, 4 * c))
    return out


def _pack_full(grads, group):
    parts = []
    for gr, (r, c, axis) in zip(grads, group[1]):
        shards = gr.reshape(4, r, c) if axis == 0 else jnp.moveaxis(gr.reshape(r, 4, c), 1, 0)
        parts.append(shards.reshape(8, _pack_rows(r, c), PACK_COLS))
    return _join_slots(parts, group)


def _pad_w_in(w_in):
    z = jnp.zeros((D_MODEL, 1), w_in.dtype)
    return jnp.concatenate([w_in[:, :2176], jnp.tile(z, (1, 64)), w_in[:, 2176:], jnp.tile(z, (1, 32))], axis=1)


def _unpad_w_in(g):
    return jnp.concatenate([g[:, :2176], g[:, 2240:2272]], axis=1)


def _pad_heads(w, used):
    k = w.shape[0]
    w3 = w.reshape(k, HEADS, used)
    return jnp.pad(w3, ((0, 0), (0, 0), (0, LANES - used))).reshape(k, HEADS * LANES)


def _unpad_heads(g, used):
    k = g.shape[0]
    return g.reshape(k, HEADS, LANES)[:, :, :used].reshape(k, HEADS * used)


def _rope_tables(seq):
    inv_freq = 1.0 / (ROPE_BASE ** (jnp.arange(0, ROPE, 2, dtype=F32) / ROPE))
    ang = jnp.arange(seq, dtype=F32)[:, None] * inv_freq[None, :]
    cos, sin = jnp.cos(ang), jnp.sin(ang)
    one, zero = jnp.ones((seq, NOPE), F32), jnp.zeros((seq, NOPE), F32)
    z16, z32 = jnp.zeros((seq, 16), F32), jnp.zeros((seq, 32), F32)
    cos_t = jnp.concatenate([one, cos, cos, jnp.ones((seq, 32), F32)], axis=1)
    sin_a = jnp.concatenate([zero, -sin, z16, z32], axis=1)
    sin_b = jnp.concatenate([zero, z16, sin, z32], axis=1)
    return cos_t, sin_a, sin_b


SMALL = (("ln_in_g", 1024), ("ln_in_b", 1024), ("b_ada", 6144), ("q_norm_g", 384), ("kv_norm_g", 256),
         ("ln1_g", 1024), ("ln1_b", 1024), ("ln2_g", 1024), ("ln2_b", 1024))
SUBLANES = 8
SMALL_SLOTS = [-(-n // LANES // SUBLANES) * SUBLANES for _, n in SMALL]
SMALL_AT = [sum(SMALL_SLOTS[:p]) for p in range(len(SMALL))]
SMALL_ROWS = sum(SMALL_SLOTS)


def _pack_small(vals):
    parts = []
    for v, slot in zip(vals, SMALL_SLOTS):
        rows = v.reshape(-1, LANES)
        parts.append(jnp.pad(rows, ((0, slot - rows.shape[0]), (0, 0))))
    return jnp.concatenate(parts, axis=0)


def kernel(x, c, ln_in_g, ln_in_b, w_ada, b_ada, w_in, q_norm_g, kv_norm_g, w_uq, w_ukv, w_o, ln1_g, ln1_b, w_up, w_down, ln2_g, ln2_b, loss_target, m_ln_in_g, m_ln_in_b, m_w_ada, m_b_ada, m_w_in, m_q_norm_g, m_kv_norm_g, m_w_uq, m_w_ukv, m_w_o, m_ln1_g, m_ln1_b, m_w_up, m_w_down, m_ln2_g, m_ln2_b, v_ln_in_g, v_ln_in_b, v_w_ada, v_b_ada, v_w_in, v_q_norm_g, v_kv_norm_g, v_w_uq, v_w_ukv, v_w_o, v_ln1_g, v_ln1_b, v_w_up, v_w_down, v_ln2_g, v_ln2_b):
    nb, seq, _ = x.shape
    rows = nb * seq
    ix, iy, ic = lax.axis_index("x"), lax.axis_index("y"), lax.axis_index("c")
    chip = 2 * ix + iy
    dev = 2 * chip + ic

    def my_half(shards, group):
        packed = _pack_halves([s.astype(BF16) for s in shards], group)
        return lax.dynamic_index_in_dim(packed, ic, 0, keepdims=False)

    f_in, f_uq, f_ukv = _unpack_full(_gather8(my_half([w_in[0], w_uq[0], w_ukv[0]], GROUP_IN), "gather_w_in"),
                                     GROUP_IN)
    half_mlp = my_half([w_up[0], w_down[0], w_o[0]], GROUP_MLP)
    late_weights = _gather_exchange(half_mlp)
    w_in_p = _pad_w_in(f_in)
    uq3 = f_uq.reshape(Q_RANK, HEADS, NOPE + ROPE)
    w_uq_p = jnp.pad(uq3, ((0, 0), (0, 0), (0, LANES - NOPE - ROPE))).reshape(Q_RANK, HEADS * LANES)
    w_ukv_p = jnp.concatenate([_pad_heads(f_ukv[:, :HEADS * NOPE], NOPE), f_ukv[:, HEADS * NOPE:]], axis=1)

    n_all = 8 * nb
    c_all = _gather8(c.reshape(-1, LANES), "gather_c").reshape(n_all, D_MODEL)
    ada_cols = w_ada.shape[2]
    b_sh = lax.dynamic_slice_in_dim(b_ada, chip * ada_cols, ada_cols, axis=1)
    mod_sh = _ada_fwd(c_all, w_ada[0], b_sh)
    mod_parts = lax.dynamic_index_in_dim(mod_sh.reshape(4, 2, nb, ada_cols), ic, axis=1, keepdims=False)
    mod_parts = jnp.pad(mod_parts, ((0, 0), (0, SUBLANES - nb), (0, 0)))
    mod_recv = _run_exchange(_scatter_chips_exchange(mod_parts), "mod_to_chips")
    by_chip = lax.dynamic_update_slice(jnp.zeros_like(mod_parts),
                                       lax.dynamic_index_in_dim(mod_parts, chip, 0, keepdims=True), (chip, 0, 0))
    for k, (fx, fy) in enumerate([(1, 0), (0, 1), (1, 1)]):
        src = 2 * (ix ^ fx) + (iy ^ fy)
        by_chip = lax.dynamic_update_slice(by_chip, mod_recv[k:k + 1], (src, 0, 0))
    mod_mine = jnp.moveaxis(by_chip[:, :nb], 0, 1).reshape(nb, N_MOD, D_MODEL)
    mod = jnp.pad(mod_mine, ((0, 0), (0, 8 - N_MOD), (0, 0)))

    cos_t, sin_a, sin_b = _rope_tables(seq)
    row2 = lambda v: v.reshape(1, -1)

    x2d = x.reshape(rows, D_MODEL)
    x0, h, qkv, lat, qp, kp, vm = _fwd_in(x2d, mod, row2(ln_in_g), row2(ln_in_b), w_in_p, q_norm_g, kv_norm_g,
                                          w_uq_p, w_ukv_p, cos_t, sin_a, sin_b, seq)
    sb_y, cars, g_mlp = _sb_fwd(qkv, seq, late_weights)
    g_mlp = _with_own(g_mlp, half_mlp)
    f_o = _split_slots(g_mlp, GROUP_MLP)[2].reshape(D_MODEL, D_MODEL)
    mla_y, lse = _mla_fwd(qp, kp, vm, seq)
    mix, y1, h2, u, ff, y2 = _fwd_out(sb_y, mla_y, x0, mod, f_o, ln1_g, ln1_b, g_mlp, seq)

    dy1, dmix, d_attn, dff, du, acc_out, dmod_a = _bwd_out(
        y2, loss_target.reshape(rows, D_MODEL), ff, u, y1, mix, mod, ln2_g, ln2_b, ln1_g, ln1_b, g_mlp, f_o, seq)
    c_idx = ic.reshape(1).astype(jnp.int32)
    blocks_mlp = _wgrad_packed(h2, du, "wgrad_up", lambda i, j: 2 * j + i, 0)
    blocks_mlp = _wgrad_packed(u, dff, "wgrad_down", lambda i, j: i, 1, pre="relu2", into=blocks_mlp)
    blocks_mlp = _wgrad_packed(sb_y, dmix, "wgrad_o_sb", lambda i, j: 0, 8, split=4, into=blocks_mlp)
    blocks_mlp = _wgrad_packed(mla_y, dmix, "wgrad_o_mla", lambda i, j: 1, 8, split=4, into=blocks_mlp)
    dq_sb, dk_sb, dv_sb, sibling_mlp = _sb_bwd(qkv, d_attn, cars, seq, _swap_cores_exchange(blocks_mlp))
    part_mlp, part_mlp_bf = _add_pairs(blocks_mlp, sibling_mlp, c_idx, "grad_add_cores_mlp")
    dqp, dkp, dvm, chips_mlp = _mla_bwd(qp, kp, vm, d_attn, mla_y, lse, seq, _scatter_chips_exchange(part_mlp_bf))
    grad_x, dproj, dqall, dkv, latn, acc0, acc_lat, dmod_c = _bwd_in(
        dqp, dkp, dvm, dq_sb, dk_sb, dv_sb, lat, x2d, dy1, mod, row2(ln_in_g), row2(ln_in_b), w_in_p,
        q_norm_g, kv_norm_g, w_uq_p, w_ukv_p, cos_t, sin_a, sin_b, seq)

    dmod = (dmod_a + dmod_c)[:, :N_MOD, :]
    small_part = _pack_small([acc0[0], acc0[1], jnp.zeros((N_MOD * D_MODEL,), F32), acc_lat[0, :Q_RANK],
                              acc_lat[1, :KV_RANK], acc_out[3], acc_out[4], acc_out[0], acc_out[1]])
    n_sum = SMALL_ROWS + D_MODEL // LANES
    payload = jnp.concatenate([small_part, acc_out[2].reshape(-1, LANES), dmod.reshape(-1, LANES)], axis=0)
    g_in_p, gathered = _wgrad(h, dproj, "wgrad_in", tn=768, ex=_gather_exchange(payload))
    gathered = _with_own(gathered, payload)
    g_in = _unpad_w_in(g_in_p)
    g_uq = _unpad_heads(_wgrad(latn[:, :Q_RANK], dqall, "wgrad_uq"), NOPE + ROPE)
    g_ukv_p = _wgrad(latn[:, Q_RANK:], dkv, "wgrad_ukv", tn=512)
    g_ukv = jnp.concatenate([_unpad_heads(g_ukv_p[:, :HEADS * LANES], NOPE), g_ukv_p[:, HEADS * LANES:]], axis=1)
    blocks_in = _pack_full([g_in, g_uq, g_ukv], GROUP_IN)
    sibling_in = _run_exchange(_swap_cores_exchange(blocks_in), "grads_in_to_sibling")
    part_in, part_in_bf = _add_pairs(blocks_in, sibling_in, c_idx, "grad_add_cores_in")
    chips_in = _run_exchange(_scatter_chips_exchange(part_in_bf), "grads_in_to_chips")

    def own(part):
        return lax.dynamic_index_in_dim(part, chip, 0, keepdims=False)

    half = jnp.concatenate([_add_chips(own(part_in), chips_in, "grad_add_chips_in"),
                            _add_chips(own(part_mlp), chips_mlp, "grad_add_chips_mlp")], axis=0)
    other = _run_exchange(_swap_one_exchange(half), "grads_halves")
    mine = _unpack_half(half[:GROUP_IN[0]], GROUP_IN) + _unpack_half(half[GROUP_IN[0]:], GROUP_MLP)
    theirs = _unpack_half(other[:GROUP_IN[0]], GROUP_IN) + _unpack_half(other[GROUP_IN[0]:], GROUP_MLP)

    small_sum = _sum_lead(gathered[:, :n_sum, :], "sum_small")
    loss = jnp.sum(small_sum[SMALL_ROWS:])
    dmod_all = gathered[:, n_sum:, :].reshape(n_all, N_MOD * D_MODEL)
    g_b_ada = _sum_lead(dmod_all.reshape(n_all, N_MOD * D_MODEL // LANES, LANES), "sum_b_ada")
    dmod_sh = lax.dynamic_slice_in_dim(dmod_all, chip * ada_cols, ada_cols, axis=1)
    g_w_ada = _ada_bwd(c_all, dmod_sh)

    res = {}
    d_ada, m_ada, v_ada = _adamw(w_ada[0], g_w_ada, m_w_ada[0], v_w_ada[0], "adamw_w_ada")
    res["w_ada"] = (g_w_ada[None], d_ada[None], m_ada[None], v_ada[None])
    sharded = {"w_in": (w_in, m_w_in, v_w_in), "w_uq": (w_uq, m_w_uq, v_w_uq), "w_ukv": (w_ukv, m_w_ukv, v_w_ukv),
               "w_up": (w_up, m_w_up, v_w_up), "w_down": (w_down, m_w_down, v_w_down), "w_o": (w_o, m_w_o, v_w_o)}
    for (name, (w, m, v)), g_mine, g_other in zip(sharded.items(), mine, theirs):
        quad = _adamw_halves(w[0], g_mine, g_other, m[0], v[0], c_idx, "adamw_" + name)
        res[name] = tuple(a[None] for a in quad)
    small_w = [ln_in_g, ln_in_b, b_ada, q_norm_g, kv_norm_g, ln1_g, ln1_b, ln2_g, ln2_b]
    small_m = [m_ln_in_g, m_ln_in_b, m_b_ada, m_q_norm_g, m_kv_norm_g, m_ln1_g, m_ln1_b, m_ln2_g, m_ln2_b]
    small_v = [v_ln_in_g, v_ln_in_b, v_b_ada, v_q_norm_g, v_kv_norm_g, v_ln1_g, v_ln1_b, v_ln2_g, v_ln2_b]
    for (name, _), quad in zip(SMALL, _adamw_small(small_sum, g_b_ada, small_w, small_m, small_v)):
        res[name] = quad

    order = ["ln_in_g", "ln_in_b", "w_ada", "b_ada", "w_in", "q_norm_g", "kv_norm_g", "w_uq", "w_ukv", "w_o",
             "ln1_g", "ln1_b", "w_up", "w_down", "ln2_g", "ln2_b"]
    outs = [loss, grad_x.reshape(nb, seq, D_MODEL)]
    for k in range(4):
        outs += [res[name][k] for name in order]
    return tuple(outs)
```

```python
import math

import jax
import jax.numpy as jnp
from jax import lax
from jax.experimental import pallas as pl
from jax.experimental.pallas import tpu as pltpu

F32 = jnp.float32
BF16 = jnp.bfloat16
MESH_IDS = pl.DeviceIdType.MESH

D_MODEL = 1024
HEADS = 8
SB_W = 512
MLA_W = 512
NOPE = 64
ROPE = 32
Q_RANK = 384
KV_RANK = 256
D_IN_PAD = 2304
D_FF = 4096
N_MOD = 6
LN_EPS = 1e-5
RMS_EPS = 1e-6
ALPHA = 2.0 ** 0.25
ROPE_BASE = 10000.0
SB_SCALE = 64 ** -0.5
NEG_LOG2E = -math.log2(math.e)
MLA_SCALE = 96 ** -0.5
ADAM_LR = 0.001
ADAM_B1 = 0.9
ADAM_B2 = 0.999
ADAM_EPS = 1e-08
ADAM_WD = 0.01
ADAM_STEP = 10

LANES = 128
ROW_TILE = 256
ATTN_TILE = 256
CAR_SLOTS = 8
ATTN_PAIRS = 4
VMEM_LIMIT = 56 << 20

NT = (((1,), (1,)), ((), ()))
TN = (((0,), (0,)), ((), ()))


def _params(sem=None):
    return pltpu.CompilerParams(vmem_limit_bytes=VMEM_LIMIT, dimension_semantics=sem)


def _const_spec(shape):
    zeros = (0,) * len(shape)
    return pl.BlockSpec(shape, lambda *_: zeros, pipeline_mode=pl.Buffered(1))


def _dot(a, b, dims=None):
    if dims is None:
        return jnp.dot(a, b, preferred_element_type=F32)
    return lax.dot_general(a, b, dims, preferred_element_type=F32)


def _mean(v):
    return jnp.mean(v, axis=-1, keepdims=True)


def _rowsum(v):
    return jnp.sum(v, axis=0, keepdims=True)


def _ln_fwd(y, g, b):
    mu = _mean(y)
    yc = y - mu
    rstd = lax.rsqrt(_mean(yc * yc) + LN_EPS)
    xhat = yc * rstd
    return xhat * g + b, xhat, rstd


def _ln_bwd(dx, xhat, rstd, g):
    dxh = dx * g
    return rstd * (dxh - _mean(dxh) - xhat * _mean(dxh * xhat))


def _rope(v, cos, sin_a, sin_b):
    return v * cos + pltpu.roll(v, 112, 1) * sin_a + pltpu.roll(v, 16, 1) * sin_b


def _rope_t(dv, cos, sin_a, sin_b):
    return dv * cos + pltpu.roll(dv * sin_a, 16, 1) + pltpu.roll(dv * sin_b, 112, 1)


def _my_place():
    return lax.axis_index("x"), lax.axis_index("y"), lax.axis_index("c")


class _Exchange:
    def __init__(self, operand, out_shape, n_copies, phases):
        self.operand = operand
        self.out_shape = out_shape
        self.phases = phases
        self.scratch = [pltpu.SemaphoreType.DMA((n_copies,)), pltpu.SemaphoreType.DMA((n_copies,))]


def _run_exchange(ex, name):
    def body(in_ref, out_ref, send_sems, recv_sems):
        for phase in ex.phases(in_ref, out_ref, send_sems, recv_sems):
            phase()

    return pl.pallas_call(
        body, name=name, out_shape=ex.out_shape,
        in_specs=[pl.BlockSpec(memory_space=pl.ANY)], out_specs=pl.BlockSpec(memory_space=pl.ANY),
        scratch_shapes=ex.scratch,
    )(ex.operand)


def _nothing():
    pass


def _gather_exchange(v):
    m, n = v.shape

    def phases(v_ref, out_ref, send_sems, recv_sems):
        x, y, c = _my_place()
        me, sibling = (x, y, c), (x, y, 1 - c)
        chips = [(1 - x, y), (x, 1 - y), (1 - x, 1 - y)]

        def rows(px, py, pc):
            return out_ref.at[4 * px + 2 * py + pc]

        def copy(k, block, to, src=None):
            return pltpu.make_async_remote_copy(
                src_ref=rows(*block) if src is None else src, dst_ref=rows(*block),
                send_sem=send_sems.at[k], recv_sem=recv_sems.at[k], device_id=to, device_id_type=MESH_IDS)

        first = [copy(0, me, sibling, src=v_ref)]
        first += [copy(1 + j, me, (*chip, c), src=v_ref) for j, chip in enumerate(chips)]
        passed = [copy(4 + j, (*chip, c), sibling) for j, chip in enumerate(chips)]

        def start():
            for cp in first:
                cp.start()

        def middle():
            for j, chip in enumerate(chips):
                copy(1 + j, (*chip, c), me).wait_recv()
                passed[j].start()

        def finish():
            copy(0, sibling, me).wait_recv()
            for j, chip in enumerate(chips):
                copy(4 + j, (*chip, 1 - c), me).wait_recv()
            for cp in first + passed:
                cp.wait_send()

        return start, middle, finish

    return _Exchange(v, jax.ShapeDtypeStruct((8, m, n), v.dtype), 7, phases)


def _gather_direct_exchange(v):
    m, n = v.shape
    flips = [(fx, fy, fc) for fx in (0, 1) for fy in (0, 1) for fc in (0, 1)][1:]

    def make_copies(v_ref, out_ref, send_sems, recv_sems):
        x, y, c = _my_place()
        return [pltpu.make_async_remote_copy(
            src_ref=v_ref, dst_ref=out_ref.at[4 * x + 2 * y + c],
            send_sem=send_sems.at[k], recv_sem=recv_sems.at[k],
            device_id=(x ^ fx, y ^ fy, c ^ fc), device_id_type=MESH_IDS) for k, (fx, fy, fc) in enumerate(flips)]

    return _direct_exchange(v, jax.ShapeDtypeStruct((8, m, n), v.dtype), 7, make_copies)


def _with_own(gathered, v):
    dev = 4 * lax.axis_index("x") + 2 * lax.axis_index("y") + lax.axis_index("c")
    return lax.dynamic_update_index_in_dim(gathered, v, dev, 0)


def _direct_exchange(operand, out_shape, n_copies, make_copies):
    def phases(in_ref, out_ref, send_sems, recv_sems):
        copies = make_copies(in_ref, out_ref, send_sems, recv_sems)

        def start():
            for cp in copies:
                cp.start()

        def finish():
            for cp in copies:
                cp.wait()

        return start, _nothing, finish

    return _Exchange(operand, out_shape, n_copies, phases)


def _swap_cores_exchange(blocks):
    _, m, n = blocks.shape

    def make_copies(g_ref, out_ref, send_sems, recv_sems):
        x, y, c = _my_place()
        return [pltpu.make_async_remote_copy(
            src_ref=g_ref.at[2 * j + (1 - c)], dst_ref=out_ref.at[j],
            send_sem=send_sems.at[j], recv_sem=recv_sems.at[j],
            device_id=(x, y, 1 - c), device_id_type=MESH_IDS) for j in range(4)]

    return _direct_exchange(blocks, jax.ShapeDtypeStruct((4, m, n), blocks.dtype), 4, make_copies)


def _scatter_chips_exchange(parts):
    _, m, n = parts.shape
    flips = [(1, 0), (0, 1), (1, 1)]

    def make_copies(p_ref, out_ref, send_sems, recv_sems):
        x, y, c = _my_place()
        copies = []
        for k, (fx, fy) in enumerate(flips):
            tx = 1 - x if fx else x
            ty = 1 - y if fy else y
            copies.append(pltpu.make_async_remote_copy(
                src_ref=p_ref.at[2 * tx + ty], dst_ref=out_ref.at[k],
                send_sem=send_sems.at[k], recv_sem=recv_sems.at[k],
                device_id=(tx, ty, c), device_id_type=MESH_IDS))
        return copies

    return _direct_exchange(parts, jax.ShapeDtypeStruct((3, m, n), parts.dtype), 3, make_copies)


def _swap_one_exchange(v):
    def make_copies(v_ref, out_ref, send_sems, recv_sems):
        x, y, c = _my_place()
        return [pltpu.make_async_remote_copy(src_ref=v_ref, dst_ref=out_ref, send_sem=send_sems.at[0],
                                             recv_sem=recv_sems.at[0], device_id=(x, y, 1 - c),
                                             device_id_type=MESH_IDS)]

    return _direct_exchange(v, jax.ShapeDtypeStruct(v.shape, v.dtype), 1, make_copies)


def _gather8(v, name):
    return _with_own(_run_exchange(_gather_exchange(v), name), v)


def _carried(ex, refs, n_in, n_out, n_scratch):
    ins, ex_in = refs[:n_in], refs[n_in]
    outs, ex_out = refs[n_in + 1:n_in + 1 + n_out], refs[n_in + 1 + n_out]
    at = n_in + 2 + n_out
    return ins, outs + refs[at:at + n_scratch], ex.phases(ex_in, ex_out, *refs[at + n_scratch:])


def _ada_fwd(c_all, w_ada_sh, b_ada_sh):
    nb, cols = c_all.shape[0], w_ada_sh.shape[1]
    tn = 512

    def body(c_ref, w_ref, b_ref, o_ref):
        cv = c_ref[...]
        act = (cv * jax.nn.sigmoid(cv)).astype(BF16)
        o_ref[...] = _dot(act, w_ref[...].astype(BF16)) + b_ref[...]

    return pl.pallas_call(
        body, name="ada_fwd", grid=(cols // tn,),
        out_shape=jax.ShapeDtypeStruct((nb, cols), F32),
        in_specs=[pl.BlockSpec((nb, D_MODEL), lambda j: (0, 0)),
                  pl.BlockSpec((D_MODEL, tn), lambda j: (0, j)),
                  pl.BlockSpec((1, tn), lambda j: (0, j))],
        out_specs=pl.BlockSpec((nb, tn), lambda j: (0, j)),
        compiler_params=_params(("arbitrary",)),
    )(c_all, w_ada_sh, b_ada_sh)


def _ada_bwd(c_all, dmod_sh):
    nb, cols = dmod_sh.shape
    tn = 512

    def body(c_ref, d_ref, o_ref):
        cv = c_ref[...]
        act = (cv * jax.nn.sigmoid(cv)).astype(BF16)
        o_ref[...] = _dot(act, d_ref[...].astype(BF16), TN)

    return pl.pallas_call(
        body, name="ada_bwd", grid=(cols // tn,),
        out_shape=jax.ShapeDtypeStruct((D_MODEL, cols), F32),
        in_specs=[pl.BlockSpec((nb, D_MODEL), lambda j: (0, 0)),
                  pl.BlockSpec((nb, tn), lambda j: (0, j))],
        out_specs=pl.BlockSpec((D_MODEL, tn), lambda j: (0, j)),
        compiler_params=_params(("arbitrary",)),
    )(c_all, dmod_sh)


def _sum_lead(v, name):
    k, m, n = v.shape

    def body(v_ref, o_ref):
        acc = v_ref[0]
        for i in range(1, k):
            acc = acc + v_ref[i]
        o_ref[...] = acc

    return pl.pallas_call(
        body, name=name, out_shape=jax.ShapeDtypeStruct((m, n), F32),
        in_specs=[pl.BlockSpec((k, m, n), lambda: (0, 0, 0))],
        out_specs=pl.BlockSpec((m, n), lambda: (0, 0)),
        compiler_params=_params(),
    )(v)


def _adamw_math(w, g, m, v):
    mn = ADAM_B1 * m + (1.0 - ADAM_B1) * g
    vn = ADAM_B2 * v + (1.0 - ADAM_B2) * (g * g)
    m_hat = mn / (1.0 - ADAM_B1 ** ADAM_STEP)
    v_hat = vn / (1.0 - ADAM_B2 ** ADAM_STEP)
    return -ADAM_LR * (m_hat / (jnp.sqrt(v_hat) + ADAM_EPS) + ADAM_WD * w), mn, vn


def _adamw_small(g_sum, g_b_ada, ws, ms, vs):
    n = len(SMALL)

    def body(gs_ref, gb_ref, *refs):
        outs = refs[3 * n:]
        for p in range(n):
            rows_p = SMALL[p][1] // LANES
            g = gb_ref[...] if SMALL[p][0] == "b_ada" else gs_ref[SMALL_AT[p]:SMALL_AT[p] + rows_p, :]
            d, mn, vn = _adamw_math(refs[p][...], g, refs[n + p][...], refs[2 * n + p][...])
            outs[4 * p][...] = g
            outs[4 * p + 1][...] = d
            outs[4 * p + 2][...] = mn
            outs[4 * p + 3][...] = vn

    shapes = [jax.ShapeDtypeStruct((size // LANES, LANES), F32) for _, size in SMALL for _ in range(4)]
    flat = lambda arrs: [a.reshape(-1, LANES) for a in arrs]
    res = pl.pallas_call(body, name="adamw_small", out_shape=tuple(shapes), compiler_params=_params())(
        g_sum, g_b_ada, *flat(ws), *flat(ms), *flat(vs))
    return [tuple(r.reshape(w.shape) for r in res[4 * p:4 * p + 4]) for p, w in enumerate(ws)]


def _adamw_halves(w, g_mine, g_other, m, v, c_idx, name):
    r, cols = w.shape
    half = r // 2
    tr = half
    while tr * cols * 4 > (2 << 20) and tr % 16 == 0:
        tr //= 2

    def body(c_ref, w_ref, mine_ref, other_ref, m_ref, v_ref, g_ref, d_ref, mo_ref, vo_ref):
        g = jnp.where(pl.program_id(0) == c_ref[0], mine_ref[...], other_ref[...])
        g_ref[0] = g
        d_ref[0], mo_ref[0], vo_ref[0] = _adamw_math(w_ref[0], g, m_ref[0], v_ref[0])

    full = pl.BlockSpec((1, tr, cols), lambda h, i, c: (h, i, 0))
    part = pl.BlockSpec((tr, cols), lambda h, i, c: (i, 0))
    shape = jax.ShapeDtypeStruct((2, half, cols), F32)
    grid_spec = pltpu.PrefetchScalarGridSpec(
        num_scalar_prefetch=1, grid=(2, half // tr),
        in_specs=[full, part, part, full, full], out_specs=(full, full, full, full))
    split = lambda a: a.reshape(2, half, cols)
    res = pl.pallas_call(
        body, name=name, grid_spec=grid_spec, out_shape=(shape, shape, shape, shape),
        compiler_params=_params(("arbitrary", "arbitrary")),
    )(c_idx, split(w), g_mine, g_other, split(m), split(v))
    return tuple(a.reshape(r, cols) for a in res)


def _adamw(w, g, m, v, name):
    rows, cols = w.shape
    tr = rows
    while tr * cols * 4 > (2 << 20) and tr % 16 == 0:
        tr //= 2

    def body(w_ref, g_ref, m_ref, v_ref, d_ref, mo_ref, vo_ref):
        d_ref[...], mo_ref[...], vo_ref[...] = _adamw_math(w_ref[...], g_ref[...], m_ref[...], v_ref[...])

    spec = pl.BlockSpec((tr, cols), lambda i: (i, 0))
    shape = jax.ShapeDtypeStruct((rows, cols), F32)
    return pl.pallas_call(
        body, name=name, grid=(rows // tr,), out_shape=(shape, shape, shape),
        in_specs=[spec, spec, spec, spec], out_specs=(spec, spec, spec),
        compiler_params=_params(("arbitrary",)),
    )(w, g, m, v)


def _add_rows(m, n):
    fits = [d for d in range(16, m + 1, 16) if m % d == 0 and d * n * 4 <= (5 << 19)]
    assert fits, (m, n)
    return max(fits)


def _add_pairs(blocks, recv, c_idx, name):
    _, m, n = blocks.shape
    tr = _add_rows(m, n)

    def body(c_ref, a_ref, b_ref, o_ref, ob_ref):
        s = a_ref[...] + b_ref[...]
        o_ref[...] = s
        ob_ref[...] = s.astype(BF16)

    grid_spec = pltpu.PrefetchScalarGridSpec(
        num_scalar_prefetch=1, grid=(4, m // tr),
        in_specs=[pl.BlockSpec((1, tr, n), lambda j, i, c: (2 * j + c[0], i, 0)),
                  pl.BlockSpec((1, tr, n), lambda j, i, c: (j, i, 0))],
        out_specs=(pl.BlockSpec((1, tr, n), lambda j, i, c: (j, i, 0)),
                   pl.BlockSpec((1, tr, n), lambda j, i, c: (j, i, 0))))
    return pl.pallas_call(
        body, name=name, grid_spec=grid_spec,
        out_shape=(jax.ShapeDtypeStruct((4, m, n), F32), jax.ShapeDtypeStruct((4, m, n), BF16)),
        compiler_params=_params(("arbitrary", "arbitrary")),
    )(c_idx, blocks, recv)


def _add_chips(own, recv, name):
    m, n = own.shape
    tr = _add_rows(m, n)

    def body(a_ref, r_ref, o_ref):
        acc = a_ref[...]
        for k in range(3):
            acc = acc + r_ref[k].astype(F32)
        o_ref[...] = acc

    return pl.pallas_call(
        body, name=name, grid=(m // tr,),
        out_shape=jax.ShapeDtypeStruct((m, n), F32),
        in_specs=[pl.BlockSpec((tr, n), lambda i: (i, 0)), pl.BlockSpec((3, tr, n), lambda i: (0, i, 0))],
        out_specs=pl.BlockSpec((tr, n), lambda i: (i, 0)),
        compiler_params=_params(("arbitrary",)),
    )(own, recv)


def _row_spec(cols):
    return pl.BlockSpec((ROW_TILE, cols), lambda i: (i, 0))


def _mod_spec(tiles_per_seq):
    return pl.BlockSpec((1, 8, D_MODEL), lambda i: (i // tiles_per_seq, 0, 0))


def _table_spec(tiles_per_seq):
    return pl.BlockSpec((ROW_TILE, LANES), lambda i: (i % tiles_per_seq, 0))


def _fwd_in(x, mod, ln_g, ln_b, w_in, q_g, kv_g, w_uq, w_ukv, cos_t, sin_a, sin_b, seq):
    rows = x.shape[0]
    tm = min(2 * ROW_TILE, seq)
    tps = seq // tm

    def body(x_ref, mod_ref, g_ref, b_ref, win_ref, qg_ref, kvg_ref, wuq_ref, wukv_ref, cos_ref, sa_ref, sb_ref,
             x0_ref, h_ref, qkv_ref, lat_ref, qp_ref, kp_ref, vm_ref):
        def chain(rs):
            x0, _, _ = _ln_fwd(x_ref[rs, :], g_ref[...], b_ref[...])
            x0_ref[rs, :] = x0
            h = (x0 * (1.0 + mod_ref[0, 1:2, :]) + mod_ref[0, 0:1, :]).astype(BF16)
            h_ref[rs, :] = h
            yield
            proj = _dot(h, win_ref[...])
            yield
            qkv_ref[rs, :SB_W] = (proj[:, :SB_W] * SB_SCALE).astype(BF16)
            qkv_ref[rs, SB_W:] = proj[:, SB_W:3 * SB_W].astype(BF16)
            lat_ref[rs, :] = proj[:, 3 * SB_W:3 * SB_W + Q_RANK + KV_RANK]
            cq = proj[:, 3 * SB_W:3 * SB_W + Q_RANK]
            ckv = proj[:, 3 * SB_W + Q_RANK:3 * SB_W + Q_RANK + KV_RANK]
            kr = proj[:, D_IN_PAD - LANES:]
            cos, sa, sb = cos_ref[rs, :], sa_ref[rs, :], sb_ref[rs, :]
            cqn = (cq * lax.rsqrt(_mean(cq * cq) + RMS_EPS) * qg_ref[...]).astype(BF16)
            q_all = _dot(cqn, wuq_ref[...])
            ckvn = (ckv * lax.rsqrt(_mean(ckv * ckv) + RMS_EPS) * kvg_ref[...]).astype(BF16)
            kv = _dot(ckvn, wukv_ref[...])
            yield
            for hd in range(HEADS):
                sl = slice(hd * LANES, (hd + 1) * LANES)
                qp_ref[rs, sl] = _rope(q_all[:, sl], cos, sa, sb).astype(BF16)
            kr_rot = _rope(kr, cos, sa, sb)
            for hd in range(HEADS):
                sl = slice(hd * LANES, (hd + 1) * LANES)
                kp_ref[rs, sl] = (kv[:, sl] + kr_rot).astype(BF16)
            vm_ref[rs, :] = kv[:, HEADS * LANES:].astype(BF16)

        half = tm // 2
        _staggered([chain(slice(0, half)), chain(slice(half, tm))])

    row_spec = lambda cols: pl.BlockSpec((tm, cols), lambda i: (i, 0))
    table_spec = pl.BlockSpec((tm, LANES), lambda i: (i % tps, 0))
    outs = [(D_MODEL, F32), (D_MODEL, BF16), (3 * SB_W, BF16), (Q_RANK + KV_RANK, F32),
            (HEADS * LANES, BF16), (HEADS * LANES, BF16), (MLA_W, BF16)]
    return pl.pallas_call(
        body, name="fwd_in", grid=(rows // tm,),
        out_shape=tuple(jax.ShapeDtypeStruct((rows, n), dt) for n, dt in outs),
        in_specs=[row_spec(D_MODEL), pl.BlockSpec((1, 8, D_MODEL), lambda i: (i // tps, 0, 0)),
                  _const_spec((1, D_MODEL)), _const_spec((1, D_MODEL)),
                  _const_spec(w_in.shape), _const_spec((1, Q_RANK)), _const_spec((1, KV_RANK)),
                  _const_spec(w_uq.shape), _const_spec(w_ukv.shape), table_spec, table_spec, table_spec],
        out_specs=tuple(row_spec(n) for n, _ in outs),
        compiler_params=_params(("arbitrary",)),
    )(x, mod, ln_g, ln_b, w_in, q_g, kv_g, w_uq, w_ukv, cos_t, sin_a, sin_b)


HALF = 512
SHARD = 1024


def _mlp_weight_specs():
    return [pl.BlockSpec((8, HALF, SHARD), lambda i: (0, 0, 0), pipeline_mode=pl.Buffered(1)),
            pl.BlockSpec((8, HALF, SHARD), lambda i: (0, 1, 0), pipeline_mode=pl.Buffered(1))]


def _fwd_out(sb_y, mla_y, x0, mod, w_o, ln_g, ln_b, g_mlp, seq):
    rows = x0.shape[0]
    tm = ROW_TILE
    tps = seq // tm

    def body(sb_ref, ml_ref, x0_ref, mod_ref, wo_ref, g_ref, b_ref, wu_ref, wd_ref,
             mix_ref, y1_ref, h2_ref, u_ref, ff_ref, y2_ref):
        mix = _dot(sb_ref[...], wo_ref[:SB_W, :]) + _dot(ml_ref[...].astype(BF16), wo_ref[SB_W:, :])
        mix_ref[...] = mix
        y1 = ALPHA * x0_ref[...] + (1.0 + mod_ref[0, 2:3, :]) * mix
        y1_ref[...] = y1
        x1, _, _ = _ln_fwd(y1, g_ref[...], b_ref[...])
        h2 = (x1 * (1.0 + mod_ref[0, 4:5, :]) + mod_ref[0, 3:4, :]).astype(BF16)
        h2_ref[...] = h2
        h_lo, h_hi = h2[:, :HALF], h2[:, HALF:]
        ff = jnp.zeros((tm, D_MODEL), F32)
        for chip in range(4):
            u = _dot(h_lo, wu_ref[2 * chip]) + _dot(h_hi, wu_ref[2 * chip + 1])
            u_ref[:, chip * SHARD:(chip + 1) * SHARD] = u.astype(BF16)
            act = jnp.square(jnp.maximum(u, 0.0)).astype(BF16)
            ff = ff + _dot(act[:, :HALF], wd_ref[2 * chip]) + _dot(act[:, HALF:], wd_ref[2 * chip + 1])
        ff_ref[...] = ff
        y2_ref[...] = ALPHA * x1 + (1.0 + mod_ref[0, 5:6, :]) * ff

    outs = [(D_MODEL, F32), (D_MODEL, F32), (D_MODEL, BF16), (D_FF, BF16), (D_MODEL, F32), (D_MODEL, F32)]
    return pl.pallas_call(
        body, name="fwd_out", grid=(rows // tm,),
        out_shape=tuple(jax.ShapeDtypeStruct((rows, n), dt) for n, dt in outs),
        in_specs=[_row_spec(SB_W), _row_spec(MLA_W), _row_spec(D_MODEL), _mod_spec(tps), _const_spec(w_o.shape),
                  _const_spec((1, D_MODEL)), _const_spec((1, D_MODEL))] + _mlp_weight_specs(),
        out_specs=tuple(_row_spec(n) for n, _ in outs),
        compiler_params=_params(("arbitrary",)),
    )(sb_y, mla_y, x0, mod, w_o, ln_g, ln_b, g_mlp, g_mlp)


def _staggered(chains):
    live = []
    for chain in chains:
        live.append(chain)
        live = [c for c in live if next(c, StopIteration) is not StopIteration]
    while live:
        live = [c for c in live if next(c, StopIteration) is not StopIteration]


def _acc_spec(rows=8, cols=D_MODEL):
    return pl.BlockSpec((rows, cols), lambda i: (0, 0))


def _bwd_out(y2, tgt, ff, u, y1, mix, mod, ln2_g, ln2_b, ln1_g, ln1_b, g_mlp, w_o, seq):
    rows = y2.shape[0]
    nb = rows // seq
    tm = ROW_TILE
    tps = seq // tm

    def body(y2_ref, t_ref, ff_ref, u_ref, y1_ref, mix_ref, mod_ref, g2_ref, b2_ref, g_ref, b_ref, wu_ref, wd_ref,
             wo_ref, dy1_ref, dmix_ref, do_ref, dff_ref, du_ref, acc_ref, dmod_ref):
        i = pl.program_id(0)

        @pl.when(i == 0)
        def _():
            acc_ref[...] = jnp.zeros_like(acc_ref)

        @pl.when(i % tps == 0)
        def _():
            dmod_ref[...] = jnp.zeros_like(dmod_ref)

        g2 = g2_ref[...]
        x2, xhat2, rstd2 = _ln_fwd(y2_ref[...], g2, b2_ref[...])
        err = x2 - t_ref[...]
        dx2 = err * (1.0 / D_MODEL)
        acc_ref[0:1, :] += _rowsum(dx2 * xhat2)
        acc_ref[1:2, :] += _rowsum(dx2)
        acc_ref[2:3, :] += _rowsum(err * err) * (0.5 / D_MODEL)
        dy2 = _ln_bwd(dx2, xhat2, rstd2, g2)
        dmod_ref[0, 5:6, :] += _rowsum(dy2 * ff_ref[...])
        dff = ((1.0 + mod_ref[0, 5:6, :]) * dy2).astype(BF16)
        dff_ref[...] = dff
        for blk in range(8):
            cols = slice(blk * HALF, (blk + 1) * HALF)
            da = _dot(dff, wd_ref[blk], NT)
            du_ref[:, cols] = (da * (2.0 * jnp.maximum(u_ref[:, cols].astype(F32), 0.0))).astype(BF16)

        g = g_ref[...]
        x1, xhat, rstd = _ln_fwd(y1_ref[...], g, b_ref[...])
        halves = []
        for half in range(2):
            acc = jnp.zeros((tm, HALF), F32)
            for chip in range(4):
                acc = acc + _dot(du_ref[:, chip * SHARD:(chip + 1) * SHARD], wu_ref[2 * chip + half], NT)
            halves.append(acc)
        dh2 = jnp.concatenate(halves, axis=1)
        dmod_ref[0, 3:4, :] += _rowsum(dh2)
        dmod_ref[0, 4:5, :] += _rowsum(dh2 * x1)
        dx1 = ALPHA * dy2 + dh2 * (1.0 + mod_ref[0, 4:5, :])
        acc_ref[3:4, :] += _rowsum(dx1 * xhat)
        acc_ref[4:5, :] += _rowsum(dx1)
        dy1 = _ln_bwd(dx1, xhat, rstd, g)
        dy1_ref[...] = dy1
        dmod_ref[0, 2:3, :] += _rowsum(dy1 * mix_ref[...])
        dmix = ((1.0 + mod_ref[0, 2:3, :]) * dy1).astype(BF16)
        dmix_ref[...] = dmix
        do_ref[...] = _dot(dmix, wo_ref[...], NT)

    outs = [(D_MODEL, F32), (D_MODEL, BF16), (D_MODEL, F32), (D_MODEL, BF16), (D_FF, BF16)]
    return pl.pallas_call(
        body, name="bwd_out", grid=(rows // tm,),
        out_shape=tuple(jax.ShapeDtypeStruct((rows, n), dt) for n, dt in outs)
        + (jax.ShapeDtypeStruct((8, D_MODEL), F32), jax.ShapeDtypeStruct((nb, 8, D_MODEL), F32)),
        in_specs=[_row_spec(D_MODEL), _row_spec(D_MODEL), _row_spec(D_MODEL), _row_spec(D_FF), _row_spec(D_MODEL),
                  _row_spec(D_MODEL), _mod_spec(tps), _const_spec((1, D_MODEL)), _const_spec((1, D_MODEL)),
                  _const_spec((1, D_MODEL)), _const_spec((1, D_MODEL))] + _mlp_weight_specs()
        + [_const_spec(w_o.shape)],
        out_specs=tuple(_row_spec(n) for n, _ in outs) + (_acc_spec(), _mod_spec(tps)),
        compiler_params=_params(("arbitrary",)),
    )(y2, tgt, ff, u, y1, mix, mod, ln2_g, ln2_b, ln1_g, ln1_b, g_mlp, g_mlp, w_o)


def _bwd_in(dqp, dkp, dvm, dq_sb, dk_sb, dv_sb, lat, x, dy1, mod, ln_g, ln_b, w_in, q_g, kv_g, w_uq, w_ukv,
            cos_t, sin_a, sin_b, seq):
    rows = x.shape[0]
    nb = rows // seq
    tm = min(2 * ROW_TILE, seq)
    tps = seq // tm
    n_lat = Q_RANK + KV_RANK

    def body(dqp_ref, dkp_ref, dvm_ref, dqs_ref, dks_ref, dvs_ref, lat_ref, x_ref, dy1_ref, mod_ref,
             g_ref, b_ref, win_ref, qg_ref, kvg_ref, wuq_ref, wukv_ref, cos_ref, sa_ref, sb_ref,
             dx_ref, dproj_ref, dqall_ref, dkv_ref, latn_ref, acc_ref, accl_ref, dmod_ref):
        i = pl.program_id(0)

        @pl.when(i == 0)
        def _():
            acc_ref[...] = jnp.zeros_like(acc_ref)
            accl_ref[...] = jnp.zeros_like(accl_ref)

        @pl.when(i % tps == 0)
        def _():
            dmod_ref[...] = jnp.zeros_like(dmod_ref)

        def chain(rs):
            n_rows = rs.stop - rs.start
            cos, sa, sb = cos_ref[rs, :], sa_ref[rs, :], sb_ref[rs, :]
            lane = lax.broadcasted_iota(jnp.int32, (n_rows, LANES), 1)
            for hd in range(HEADS):
                sl = slice(hd * LANES, (hd + 1) * LANES)
                dqall_ref[rs, sl] = _rope_t(dqp_ref[rs, sl], cos, sa, sb).astype(BF16)
            dkr = jnp.zeros((n_rows, LANES), F32)
            for hd in range(HEADS):
                sl = slice(hd * LANES, (hd + 1) * LANES)
                dk = dkp_ref[rs, sl]
                dkr = dkr + dk
                dkv_ref[rs, sl] = jnp.where(lane < NOPE, dk, 0.0).astype(BF16)
            dkv_ref[rs, HEADS * LANES:] = dvm_ref[rs, :].astype(BF16)
            dkr = _rope_t(jnp.where(lane >= NOPE, dkr, 0.0), cos, sa, sb)
            yield
            dcqn = _dot(dqall_ref[rs, :], wuq_ref[...], NT)
            dckvn = _dot(dkv_ref[rs, :], wukv_ref[...], NT)
            yield
            cq = lat_ref[rs, :Q_RANK]
            qg = qg_ref[...]
            rq = lax.rsqrt(_mean(cq * cq) + RMS_EPS)
            cqn = cq * rq
            latn_ref[rs, :Q_RANK] = (cqn * qg).astype(BF16)
            accl_ref[0:1, :Q_RANK] += _rowsum(dcqn * cqn)
            dqg = dcqn * qg
            dcq = rq * (dqg - cqn * _mean(dqg * cqn))
            ckv = lat_ref[rs, Q_RANK:]
            kvg = kvg_ref[...]
            rkv = lax.rsqrt(_mean(ckv * ckv) + RMS_EPS)
            ckvn = ckv * rkv
            latn_ref[rs, Q_RANK:] = (ckvn * kvg).astype(BF16)
            accl_ref[1:2, :KV_RANK] += _rowsum(dckvn * ckvn)
            dkg = dckvn * kvg
            dckv = rkv * (dkg - ckvn * _mean(dkg * ckvn))
            dproj_ref[rs, :SB_W] = dqs_ref[rs, :]
            dproj_ref[rs, SB_W:2 * SB_W] = dks_ref[rs, :].astype(BF16)
            dproj_ref[rs, 2 * SB_W:3 * SB_W] = dvs_ref[rs, :].astype(BF16)
            dproj_ref[rs, 3 * SB_W:3 * SB_W + Q_RANK] = dcq.astype(BF16)
            dproj_ref[rs, 3 * SB_W + Q_RANK:3 * SB_W + n_lat] = dckv.astype(BF16)
            dproj_ref[rs, D_IN_PAD - LANES:] = dkr.astype(BF16)
            yield
            dh = _dot(dproj_ref[rs, :], win_ref[...], NT)
            yield
            g = g_ref[...]
            x0, xhat, rstd = _ln_fwd(x_ref[rs, :], g, b_ref[...])
            dmod_ref[0, 0:1, :] += _rowsum(dh)
            dmod_ref[0, 1:2, :] += _rowsum(dh * x0)
            dx0 = ALPHA * dy1_ref[rs, :] + dh * (1.0 + mod_ref[0, 1:2, :])
            acc_ref[0:1, :] += _rowsum(dx0 * xhat)
            acc_ref[1:2, :] += _rowsum(dx0)
            dx_ref[rs, :] = _ln_bwd(dx0, xhat, rstd, g)

        _staggered([chain(slice(0, tm // 2)), chain(slice(tm // 2, tm))])

    row_spec = lambda cols: pl.BlockSpec((tm, cols), lambda i: (i, 0))
    table_spec = pl.BlockSpec((tm, LANES), lambda i: (i % tps, 0))
    mod_spec = pl.BlockSpec((1, 8, D_MODEL), lambda i: (i // tps, 0, 0))
    outs = [(D_MODEL, F32), (D_IN_PAD, BF16), (HEADS * LANES, BF16), (HEADS * LANES + MLA_W, BF16), (n_lat, BF16)]
    return pl.pallas_call(
        body, name="bwd_in", grid=(rows // tm,),
        out_shape=tuple(jax.ShapeDtypeStruct((rows, n), dt) for n, dt in outs)
        + (jax.ShapeDtypeStruct((8, D_MODEL), F32), jax.ShapeDtypeStruct((8, Q_RANK), F32),
           jax.ShapeDtypeStruct((nb, 8, D_MODEL), F32)),
        in_specs=[row_spec(HEADS * LANES), row_spec(HEADS * LANES), row_spec(MLA_W),
                  row_spec(SB_W), row_spec(SB_W), row_spec(SB_W), row_spec(n_lat),
                  row_spec(D_MODEL), row_spec(D_MODEL), mod_spec,
                  _const_spec((1, D_MODEL)), _const_spec((1, D_MODEL)), _const_spec(w_in.shape),
                  _const_spec((1, Q_RANK)), _const_spec((1, KV_RANK)), _const_spec(w_uq.shape),
                  _const_spec(w_ukv.shape), table_spec, table_spec, table_spec],
        out_specs=tuple(row_spec(n) for n, _ in outs) + (_acc_spec(), _acc_spec(8, Q_RANK), mod_spec),
        compiler_params=_params(("arbitrary",)),
    )(dqp, dkp, dvm, dq_sb, dk_sb, dv_sb, lat, x, dy1, mod, ln_g, ln_b, w_in, q_g, kv_g, w_uq, w_ukv,
      cos_t, sin_a, sin_b)


def _wgrad(a, b, name, tm=512, tn=1024, tk=2048, ex=None):
    rows, m = a.shape
    n = b.shape[1]
    tm, tn, tk = min(tm, m), min(tn, n), min(tk, rows)
    if m % tm:
        tm = m
    if n % tn:
        tn = n

    def body(a_ref, b_ref, o_ref):
        @pl.when(pl.program_id(2) == 0)
        def _():
            o_ref[...] = jnp.zeros_like(o_ref)

        o_ref[...] += _dot(a_ref[...].astype(BF16), b_ref[...].astype(BF16), TN)

    res = _carrier_call(
        body, ex, name, (m // tm, n // tn, rows // tk), [a, b],
        [pl.BlockSpec((tk, tm), lambda i, j, k: (k, i)), pl.BlockSpec((tk, tn), lambda i, j, k: (k, j))],
        [jax.ShapeDtypeStruct((m, n), F32)], [pl.BlockSpec((tm, tn), lambda i, j, k: (i, j))])
    return res[0] if ex is None else res


def _wgrad_packed(a, b, name, block_of, row_block, split=1, pre=None, into=None, tk=2048):
    rows, m = a.shape
    n = b.shape[1]
    tm = HALF
    part = tm // split
    tk = min(tk, rows)
    shape = jax.ShapeDtypeStruct((8, GROUP_MLP[0], PACK_COLS), F32)

    def body(a_ref, b_ref, *rest):
        o_ref = rest[-1]

        @pl.when(pl.program_id(2) == 0)
        def _():
            o_ref[...] = jnp.zeros_like(o_ref)

        av = a_ref[...]
        if pre == "relu2":
            av = jnp.square(jnp.maximum(av.astype(F32), 0.0))
        prod = _dot(av.astype(BF16), b_ref[...].astype(BF16), TN)
        for s in range(split):
            o_ref[s] += prod[s * part:(s + 1) * part]

    in_specs = [pl.BlockSpec((tk, tm), lambda i, j, k: (k, i)), pl.BlockSpec((tk, SHARD), lambda i, j, k: (k, j))]
    operands = [a, b]
    if into is not None:
        in_specs.append(pl.BlockSpec(memory_space=pl.ANY))
        operands.append(into)
    return pl.pallas_call(
        body, name=name, grid=(m // tm, n // SHARD, rows // tk), out_shape=shape,
        in_specs=in_specs,
        out_specs=pl.BlockSpec((split, part, SHARD), lambda i, j, k: (block_of(i, j), row_block, 0)),
        input_output_aliases={} if into is None else {2: 0},
        compiler_params=_params(("arbitrary", "arbitrary", "arbitrary")),
    )(*operands)


def _pair(pp):
    return slice(pp * LANES, (pp + 1) * LANES)


def _head_mask(lane, hh):
    return jnp.where((lane >= 64) if hh else (lane < 64), 1.0, 0.0).astype(BF16)


def _tri(t, kind):
    s = lax.broadcasted_iota(jnp.int32, (t, t), 0)
    j = lax.broadcasted_iota(jnp.int32, (t, t), 1)
    one = jnp.where(j > s if kind == "later" else j < s, 1.0, 0.0).astype(BF16)
    return jnp.concatenate([one, one], axis=1)


def _split_dot(tri2, v):
    hi = v.astype(BF16)
    lo = (v - hi.astype(F32)).astype(BF16)
    return _dot(tri2, jnp.concatenate([hi, lo], axis=0))


def _sb_logits(z, valid):
    log_keep = -(jnp.maximum(z, 0.0) + jnp.log(1.0 + jnp.exp2(jnp.abs(z) * NEG_LOG2E)))
    log_beta = z + log_keep
    if valid is not None:
        log_keep = jnp.where(valid, log_keep, 0.0)
    return log_keep, log_beta


def _carrier_call(body, ex, name, grid, operands, in_specs, out_shapes, out_specs, scratch=()):
    n_in, n_out = len(operands), len(out_shapes)
    total = grid[0] * grid[1] * grid[2]
    any_spec = pl.BlockSpec(memory_space=pl.ANY)

    def carrier(*refs):
        ins, outs, (start, middle, finish) = _carried(ex, refs, n_in, n_out, len(scratch))
        step = (pl.program_id(0) * grid[1] + pl.program_id(1)) * grid[2] + pl.program_id(2)
        pl.when(step == 0)(start)
        pl.when(step == total // 2)(middle)
        body(*ins, *outs)
        pl.when(step == total - 1)(finish)

    carried = ex is not None
    return pl.pallas_call(
        carrier if carried else body, name=name, grid=grid,
        out_shape=tuple(out_shapes) + ((ex.out_shape,) if carried else ()),
        in_specs=list(in_specs) + ([any_spec] if carried else []),
        out_specs=tuple(out_specs) + ((any_spec,) if carried else ()),
        scratch_shapes=list(scratch) + (ex.scratch if carried else []),
        compiler_params=_params(("arbitrary", "arbitrary", "arbitrary")),
    )(*operands, *([ex.operand] if carried else []))


def _sb_fwd(qkv, seq, ex=None):
    rows = qkv.shape[0]
    nb = rows // seq
    t = min(ATTN_TILE, seq)
    nq = seq // t
    assert nq <= CAR_SLOTS, (seq, t)
    ap = ATTN_PAIRS
    width = ap * LANES
    groups = SB_W // width
    hds = [(pp, hh) for pp in range(ap) for hh in range(2)]

    def body(q_ref, k_ref, v_ref, tri_ref, o_ref, car_ref, acc_ref):
        i = pl.program_id(2)
        lane = lax.broadcasted_iota(jnp.int32, (t, LANES), 1)
        key = lax.broadcasted_iota(jnp.int32, (t, t), 0)
        qry = lax.broadcasted_iota(jnp.int32, (t, t), 1)
        strict = key < qry
        tri = tri_ref[...]
        masks = [_head_mask(lane, hh) for hh in range(2)]
        qms = [q_ref[:, _pair(pp)] * masks[hh] for pp, hh in hds]
        acc_ref[...] = jnp.zeros_like(acc_ref)
        car_ref[...] = jnp.zeros_like(car_ref)

        def step(kb, c_sums, valid):
            start = pl.multiple_of(kb * t, t)
            kss = [k_ref[pl.ds(start, t), _pair(pp)] for pp in range(ap)]
            vss = [v_ref[pl.ds(start, t), _pair(pp)] for pp in range(ap)]
            zs = [_dot(kss[pp], qms[n], NT) for n, (pp, _) in enumerate(hds)]
            logs = [_sb_logits(z, valid) for z in zs]
            sufs = [_split_dot(tri, lg[0]) for lg in logs]
            new_sums = []
            for n, (pp, hh) in enumerate(hds):
                log_keep, log_beta = logs[n]
                w = jnp.exp(log_beta + sufs[n] + c_sums[n])
                if valid is not None:
                    w = jnp.where(valid, w, 0.0)
                acc_ref[pp] += _dot(vss[pp] * masks[hh], w.astype(BF16), TN)
                car_ref[0, pl.ds(n * CAR_SLOTS + kb, 1), :] = c_sums[n]
                new_sums.append(c_sums[n] + sufs[n][0:1, :] + log_keep[0:1, :])
            return tuple(new_sums)

        c_sums = step(i, tuple(jnp.zeros((1, t), F32) for _ in hds), strict)
        lax.fori_loop(0, i, lambda j, cr: step(i - 1 - j, cr, None), c_sums)
        for pp in range(ap):
            o_ref[:, _pair(pp)] = acc_ref[pp].T.astype(BF16)

    qspec = pl.BlockSpec((t, width), lambda b, p, i: (b * nq + i, p))
    car_rows = len(hds) * CAR_SLOTS
    return _carrier_call(
        body, ex, "sb_fwd", (nb, groups, nq),
        [qkv, qkv, qkv, _tri(t, "later")],
        [qspec,
         pl.BlockSpec((seq, width), lambda b, p, i: (b, groups + p)),
         pl.BlockSpec((seq, width), lambda b, p, i: (b, 2 * groups + p)),
         _const_spec((t, 2 * t))],
        [jax.ShapeDtypeStruct((rows, SB_W), BF16), jax.ShapeDtypeStruct((nb * nq, HEADS * CAR_SLOTS, t), F32)],
        [qspec, pl.BlockSpec((1, car_rows, t), lambda b, p, i: (b * nq + i, p, 0))],
        scratch=[pltpu.VMEM((ap, LANES, t), F32)])


def _sb_bwd(qkv, d_out, cars, seq, ex=None):
    rows = qkv.shape[0]
    nb = rows // seq
    t = min(ATTN_TILE, seq)
    nq = seq // t
    ap = ATTN_PAIRS
    width = ap * LANES
    groups = SB_W // width
    hds = [(pp, hh) for pp in range(ap) for hh in range(2)]

    def body(q_ref, k_ref, v_ref, do_ref, car_ref, tri_ref, pre_ref, dq_ref, dk_ref, dv_ref, dq_acc):
        i = pl.program_id(2)

        @pl.when(i == 0)
        def _():
            dk_ref[...] = jnp.zeros_like(dk_ref)
            dv_ref[...] = jnp.zeros_like(dv_ref)

        lane = lax.broadcasted_iota(jnp.int32, (t, LANES), 1)
        key = lax.broadcasted_iota(jnp.int32, (t, t), 0)
        qry = lax.broadcasted_iota(jnp.int32, (t, t), 1)
        strict = key < qry
        tri, pre = tri_ref[...], pre_ref[...]
        masks = [_head_mask(lane, hh) for hh in range(2)]
        qms = [q_ref[:, _pair(pp)] * masks[hh] for pp, hh in hds]
        doms = [do_ref[:, _pair(pp)].astype(BF16) * masks[hh] for pp, hh in hds]
        dq_acc[...] = jnp.zeros_like(dq_acc)

        def step(kb, g_pres, valid):
            start = pl.multiple_of(kb * t, t)
            kss = [k_ref[pl.ds(start, t), _pair(pp)] for pp in range(ap)]
            vss = [v_ref[pl.ds(start, t), _pair(pp)] for pp in range(ap)]
            zs = [_dot(kss[pp], qms[n], NT) for n, (pp, _) in enumerate(hds)]
            dws = [_dot(vss[pp], doms[n], NT) for n, (pp, _) in enumerate(hds)]
            logs = [_sb_logits(z, valid) for z in zs]
            sufs = [_split_dot(tri, lg[0]) for lg in logs]
            ws, gs = [], []
            for n in range(len(hds)):
                c_sum = car_ref[0, pl.ds(n * CAR_SLOTS + kb, 1), :]
                w = jnp.exp(logs[n][1] + sufs[n] + c_sum)
                if valid is not None:
                    w = jnp.where(valid, w, 0.0)
                ws.append(w)
                gs.append(dws[n] * w)
            pres = [_split_dot(pre, gs[n]) for n in range(len(hds))]
            befores = [g_pres[n] + pres[n] for n in range(len(hds))]
            for pp in range(ap):
                a, b = 2 * pp, 2 * pp + 1
                dv_ref[pl.ds(start, t), _pair(pp)] += _dot(ws[a].astype(BF16), doms[a]) + _dot(ws[b].astype(BF16), doms[b])
            dzbs = []
            for n in range(len(hds)):
                beta = jnp.exp(logs[n][1])
                dz = gs[n] * (1.0 - beta) - beta * befores[n]
                if valid is not None:
                    dz = jnp.where(valid, dz, 0.0)
                dzbs.append(dz.astype(BF16))
            for pp in range(ap):
                a, b = 2 * pp, 2 * pp + 1
                dq_acc[pp] += _dot(dzbs[a], kss[pp] * masks[0], TN) + _dot(dzbs[b], kss[pp] * masks[1], TN)
                dk_ref[pl.ds(start, t), _pair(pp)] += _dot(dzbs[a], qms[a]) + _dot(dzbs[b], qms[b])
            return tuple(g_pres[n] + pres[n][t - 1:t, :] + gs[n][t - 1:t, :] for n in range(len(hds)))

        g_pres = lax.fori_loop(0, i, lambda kb, cr: step(kb, cr, None), tuple(jnp.zeros((1, t), F32) for _ in hds))
        step(i, g_pres, strict)
        for pp in range(ap):
            dq_ref[:, _pair(pp)] = (dq_acc[pp] * SB_SCALE).astype(BF16)

    qspec = pl.BlockSpec((t, width), lambda b, p, i: (b * nq + i, p))
    kspec_out = pl.BlockSpec((seq, width), lambda b, p, i: (b, p))
    car_rows = len(hds) * CAR_SLOTS
    return _carrier_call(
        body, ex, "sb_bwd", (nb, groups, nq),
        [qkv, qkv, qkv, d_out, cars, _tri(t, "later"), _tri(t, "earlier")],
        [qspec,
         pl.BlockSpec((seq, width), lambda b, p, i: (b, groups + p)),
         pl.BlockSpec((seq, width), lambda b, p, i: (b, 2 * groups + p)),
         qspec, pl.BlockSpec((1, car_rows, t), lambda b, p, i: (b * nq + i, p, 0)),
         _const_spec((t, 2 * t)), _const_spec((t, 2 * t))],
        [jax.ShapeDtypeStruct((rows, SB_W), BF16), jax.ShapeDtypeStruct((rows, SB_W), F32),
         jax.ShapeDtypeStruct((rows, SB_W), F32)],
        [qspec, kspec_out, kspec_out],
        scratch=[pltpu.VMEM((ap, t, LANES), F32)])


def _mla_scores(ks, qh, allowed):
    s = _dot(ks, qh, NT) * (MLA_SCALE * -NEG_LOG2E)
    if allowed is not None:
        s = jnp.where(allowed, s, jnp.finfo(F32).min)
    return s


def _mla_fwd(qp, kp, vm, seq, ex=None, chunk=64):
    rows = qp.shape[0]
    nb = rows // seq
    t = min(ATTN_TILE, seq)
    nq = seq // t
    shift = int(math.log2(chunk))
    ap = ATTN_PAIRS
    width = ap * LANES
    groups = MLA_W // width
    hds = [(pp, hh) for pp in range(ap) for hh in range(2)]

    def body(q_ref, k_ref, v_ref, o_ref, lse_ref, acc_ref):
        i = pl.program_id(2)
        lane = lax.broadcasted_iota(jnp.int32, (t, LANES), 1)
        key = lax.broadcasted_iota(jnp.int32, (t, t), 0)
        qry = lax.broadcasted_iota(jnp.int32, (t, t), 1)
        allowed_diag = jnp.right_shift(key, shift) <= jnp.right_shift(qry, shift)
        masks = [_head_mask(lane, hh) for hh in range(2)]
        qhs = [q_ref[:, _pair(n)] for n in range(len(hds))]
        acc_ref[...] = jnp.zeros_like(acc_ref)

        def step(kb, carry, allowed):
            start = pl.multiple_of(kb * t, t)
            vss = [v_ref[pl.ds(start, t), _pair(pp)] for pp in range(ap)]
            scores = [_mla_scores(k_ref[pl.ds(start, t), _pair(n)], qhs[n], allowed) for n in range(len(hds))]
            new = []
            for n, (pp, hh) in enumerate(hds):
                m_run, l_run = carry[n]
                s = scores[n]
                m_new = jnp.maximum(m_run, jnp.max(s, axis=0, keepdims=True))
                p = jnp.exp2(s - m_new)
                scale = jnp.exp2(m_run - m_new)
                l_run = scale * l_run + jnp.sum(p, axis=0, keepdims=True)
                acc_ref[n] = scale * acc_ref[n] + _dot(vss[pp] * masks[hh], p.astype(BF16), TN)
                new.append((m_new, l_run))
            return tuple(new)

        init = (jnp.full((1, t), jnp.finfo(F32).min, F32), jnp.zeros((1, t), F32))
        carry = step(i, tuple(init for _ in hds), allowed_diag)
        carry = lax.fori_loop(0, i, lambda kb, cr: step(kb, cr, None), carry)
        lse_rows = []
        for pp in range(ap):
            out_t = jnp.zeros((LANES, t), F32)
            for hh in range(2):
                m_run, l_run = carry[2 * pp + hh]
                out_t = out_t + acc_ref[2 * pp + hh] / l_run
                lse_rows.append(m_run + jnp.log(l_run) * -NEG_LOG2E)
            o_ref[:, _pair(pp)] = out_t.T
        lse_t = jnp.concatenate(lse_rows + [jnp.zeros((LANES - len(hds), t), F32)], axis=0)
        lse_ref[...] = jnp.zeros_like(lse_ref)
        lse_ref[:, _pair(0)] = lse_t.T

    ospec = pl.BlockSpec((t, width), lambda b, p, i: (b * nq + i, p))
    return _carrier_call(
        body, ex, "mla_fwd", (nb, groups, nq), [qp, kp, vm],
        [pl.BlockSpec((t, 2 * width), lambda b, p, i: (b * nq + i, p)),
         pl.BlockSpec((seq, 2 * width), lambda b, p, i: (b, p)),
         pl.BlockSpec((seq, width), lambda b, p, i: (b, p))],
        [jax.ShapeDtypeStruct((rows, MLA_W), F32), jax.ShapeDtypeStruct((rows, MLA_W), F32)],
        [ospec, ospec], scratch=[pltpu.VMEM((len(hds), LANES, t), F32)])


def _mla_bwd(qp, kp, vm, d_out, out, lse, seq, ex=None, chunk=64):
    rows = qp.shape[0]
    nb = rows // seq
    t = min(ATTN_TILE, seq)
    nq = seq // t
    shift = int(math.log2(chunk))
    ap = ATTN_PAIRS
    width = ap * LANES
    groups = MLA_W // width
    hds = [(pp, hh) for pp in range(ap) for hh in range(2)]
    nh = len(hds)

    def body(q_ref, k_ref, v_ref, do_ref, o_ref, lse_ref, dq_ref, dk_ref, dv_ref):
        i = pl.program_id(2)

        @pl.when(i == 0)
        def _():
            dk_ref[...] = jnp.zeros_like(dk_ref)
            dv_ref[...] = jnp.zeros_like(dv_ref)

        lane = lax.broadcasted_iota(jnp.int32, (t, LANES), 1)
        key = lax.broadcasted_iota(jnp.int32, (t, t), 0)
        qry = lax.broadcasted_iota(jnp.int32, (t, t), 1)
        allowed_diag = jnp.right_shift(key, shift) <= jnp.right_shift(qry, shift)
        qhs = [q_ref[:, _pair(n)] for n in range(nh)]
        lse_t = lse_ref[:, _pair(0)].T
        doms, deltas, lse_hs = [], [], []
        for pp in range(ap):
            do = do_ref[:, _pair(pp)]
            d_o_t = (do * o_ref[:, _pair(pp)]).T
            for hh in range(2):
                doms.append(do.astype(BF16) * _head_mask(lane, hh))
                deltas.append(jnp.sum(d_o_t[hh * 64:(hh + 1) * 64], axis=0, keepdims=True))
                lse_hs.append(lse_t[2 * pp + hh:2 * pp + hh + 1])

        dq_ref[...] = jnp.zeros_like(dq_ref)

        def step(kb, allowed):
            start = pl.multiple_of(kb * t, t)
            vss = [v_ref[pl.ds(start, t), _pair(pp)] for pp in range(ap)]
            kss = [k_ref[pl.ds(start, t), _pair(n)] for n in range(nh)]
            scores = [_mla_scores(kss[n], qhs[n], allowed) for n in range(nh)]
            dps = [_dot(vss[pp], doms[n], NT) for n, (pp, _) in enumerate(hds)]
            ps = [jnp.exp2(scores[n] - lse_hs[n]) for n in range(nh)]
            dss = [(ps[n] * (dps[n] - deltas[n]) * MLA_SCALE).astype(BF16) for n in range(nh)]
            for pp in range(ap):
                a, b = 2 * pp, 2 * pp + 1
                dv_ref[pl.ds(start, t), _pair(pp)] += _dot(ps[a].astype(BF16), doms[a]) + _dot(ps[b].astype(BF16), doms[b])
            for n in range(nh):
                dk_ref[pl.ds(start, t), _pair(n)] += _dot(dss[n], qhs[n])
                dq_ref[:, _pair(n)] += _dot(dss[n], kss[n], TN)

        def off_diagonal(kb, nothing):
            step(kb, None)
            return nothing

        lax.fori_loop(0, i, off_diagonal, 0)
        step(i, allowed_diag)

    ospec = pl.BlockSpec((t, width), lambda b, p, i: (b * nq + i, p))
    return _carrier_call(
        body, ex, "mla_bwd", (nb, groups, nq), [qp, kp, vm, d_out, out, lse],
        [pl.BlockSpec((t, 2 * width), lambda b, p, i: (b * nq + i, p)),
         pl.BlockSpec((seq, 2 * width), lambda b, p, i: (b, p)),
         pl.BlockSpec((seq, width), lambda b, p, i: (b, p)),
         pl.BlockSpec((t, width), lambda b, p, i: (b * nq + i, groups + p)),
         ospec, ospec],
        [jax.ShapeDtypeStruct((rows, HEADS * LANES), F32), jax.ShapeDtypeStruct((rows, HEADS * LANES), F32),
         jax.ShapeDtypeStruct((rows, MLA_W), F32)],
        [pl.BlockSpec((t, 2 * width), lambda b, p, i: (b * nq + i, p)),
         pl.BlockSpec((seq, 2 * width), lambda b, p, i: (b, p)),
         pl.BlockSpec((seq, width), lambda b, p, i: (b, p))])


PACK_COLS = 1024
PACK_ALIGN = 16
GROUP_IN = (384, ((1024, 552, 1), (384, 192, 1), (256, 256, 1)))
GROUP_MLP = (1152, ((1024, 1024, 1), (1024, 1024, 0), (256, 1024, 0)))


def _pack_rows(r, c):
    return (r // 2) * c // PACK_COLS


def _slot_rows(r, c):
    return -(-_pack_rows(r, c) // PACK_ALIGN) * PACK_ALIGN


def _join_slots(parts, group):
    total, weights = group
    padded = [jnp.pad(p, ((0, 0), (0, _slot_rows(r, c) - p.shape[1]), (0, 0))) for p, (r, c, _) in zip(parts, weights)]
    used = sum(_slot_rows(r, c) for r, c, _ in weights)
    if total > used:
        padded.append(jnp.zeros((parts[0].shape[0], total - used, PACK_COLS), parts[0].dtype))
    return jnp.concatenate(padded, axis=1)


def _split_slots(packed, group):
    out, at = [], 0
    for r, c, _ in group[1]:
        out.append(packed[:, at:at + _pack_rows(r, c), :])
        at += _slot_rows(r, c)
    return out


def _pack_halves(shards, group):
    return _join_slots([s.reshape(2, _pack_rows(r, c), PACK_COLS) for s, (r, c, _) in zip(shards, group[1])], group)


def _unpack_half(packed, group):
    return [p.reshape(r // 2, c) for p, (r, c, _) in zip(_split_slots(packed[None], group), group[1])]


def _unpack_full(gathered, group):
    out = []
    for p, (r, c, axis) in zip(_split_slots(gathered, group), group[1]):
        shards = p.reshape(4, r, c)
        out.append(shards.reshape(4 * r, c) if axis == 0 else jnp.moveaxis(shards, 0, 1).reshape(r, 4 * c))
    return out


def _pack_full(grads, group):
    parts = []
    for gr, (r, c, axis) in zip(grads, group[1]):
        shards = gr.reshape(4, r, c) if axis == 0 else jnp.moveaxis(gr.reshape(r, 4, c), 1, 0)
        parts.append(shards.reshape(8, _pack_rows(r, c), PACK_COLS))
    return _join_slots(parts, group)


def _pad_w_in(w_in):
    z = jnp.zeros((D_MODEL, 1), w_in.dtype)
    return jnp.concatenate([w_in[:, :2176], jnp.tile(z, (1, 64)), w_in[:, 2176:], jnp.tile(z, (1, 32))], axis=1)


def _unpad_w_in(g):
    return jnp.concatenate([g[:, :2176], g[:, 2240:2272]], axis=1)


def _pad_heads(w, used):
    k = w.shape[0]
    w3 = w.reshape(k, HEADS, used)
    return jnp.pad(w3, ((0, 0), (0, 0), (0, LANES - used))).reshape(k, HEADS * LANES)


def _unpad_heads(g, used):
    k = g.shape[0]
    return g.reshape(k, HEADS, LANES)[:, :, :used].reshape(k, HEADS * used)


def _rope_tables(seq):
    inv_freq = 1.0 / (ROPE_BASE ** (jnp.arange(0, ROPE, 2, dtype=F32) / ROPE))
    ang = jnp.arange(seq, dtype=F32)[:, None] * inv_freq[None, :]
    cos, sin = jnp.cos(ang), jnp.sin(ang)
    one, zero = jnp.ones((seq, NOPE), F32), jnp.zeros((seq, NOPE), F32)
    z16, z32 = jnp.zeros((seq, 16), F32), jnp.zeros((seq, 32), F32)
    cos_t = jnp.concatenate([one, cos, cos, jnp.ones((seq, 32), F32)], axis=1)
    sin_a = jnp.concatenate([zero, -sin, z16, z32], axis=1)
    sin_b = jnp.concatenate([zero, z16, sin, z32], axis=1)
    return cos_t, sin_a, sin_b


SMALL = (("ln_in_g", 1024), ("ln_in_b", 1024), ("b_ada", 6144), ("q_norm_g", 384), ("kv_norm_g", 256),
         ("ln1_g", 1024), ("ln1_b", 1024), ("ln2_g", 1024), ("ln2_b", 1024))
SUBLANES = 8
SMALL_SLOTS = [-(-n // LANES // SUBLANES) * SUBLANES for _, n in SMALL]
SMALL_AT = [sum(SMALL_SLOTS[:p]) for p in range(len(SMALL))]
SMALL_ROWS = sum(SMALL_SLOTS)


def _pack_small(vals):
    parts = []
    for v, slot in zip(vals, SMALL_SLOTS):
        rows = v.reshape(-1, LANES)
        parts.append(jnp.pad(rows, ((0, slot - rows.shape[0]), (0, 0))))
    return jnp.concatenate(parts, axis=0)


def kernel(x, c, ln_in_g, ln_in_b, w_ada, b_ada, w_in, q_norm_g, kv_norm_g, w_uq, w_ukv, w_o, ln1_g, ln1_b, w_up, w_down, ln2_g, ln2_b, loss_target, m_ln_in_g, m_ln_in_b, m_w_ada, m_b_ada, m_w_in, m_q_norm_g, m_kv_norm_g, m_w_uq, m_w_ukv, m_w_o, m_ln1_g, m_ln1_b, m_w_up, m_w_down, m_ln2_g, m_ln2_b, v_ln_in_g, v_ln_in_b, v_w_ada, v_b_ada, v_w_in, v_q_norm_g, v_kv_norm_g, v_w_uq, v_w_ukv, v_w_o, v_ln1_g, v_ln1_b, v_w_up, v_w_down, v_ln2_g, v_ln2_b):
    nb, seq, _ = x.shape
    rows = nb * seq
    ix, iy, ic = lax.axis_index("x"), lax.axis_index("y"), lax.axis_index("c")
    chip = 2 * ix + iy
    dev = 2 * chip + ic

    def my_half(shards, group):
        packed = _pack_halves([s.astype(BF16) for s in shards], group)
        return lax.dynamic_index_in_dim(packed, ic, 0, keepdims=False)

    f_in, f_uq, f_ukv = _unpack_full(_gather8(my_half([w_in[0], w_uq[0], w_ukv[0]], GROUP_IN), "gather_w_in"),
                                     GROUP_IN)
    half_mlp = my_half([w_up[0], w_down[0], w_o[0]], GROUP_MLP)
    late_weights = _gather_exchange(half_mlp)
    w_in_p = _pad_w_in(f_in)
    uq3 = f_uq.reshape(Q_RANK, HEADS, NOPE + ROPE)
    w_uq_p = jnp.pad(uq3, ((0, 0), (0, 0), (0, LANES - NOPE - ROPE))).reshape(Q_RANK, HEADS * LANES)
    w_ukv_p = jnp.concatenate([_pad_heads(f_ukv[:, :HEADS * NOPE], NOPE), f_ukv[:, HEADS * NOPE:]], axis=1)

    n_all = 8 * nb
    c_rows = c.reshape(-1, LANES)
    c_all = _with_own(_run_exchange(_gather_direct_exchange(c_rows), "gather_c"), c_rows).reshape(n_all, D_MODEL)
    ada_cols = w_ada.shape[2]
    b_sh = lax.dynamic_slice_in_dim(b_ada, chip * ada_cols, ada_cols, axis=1)
    mod_sh = _ada_fwd(c_all, w_ada[0], b_sh)
    mod_parts = lax.dynamic_index_in_dim(mod_sh.reshape(4, 2, nb, ada_cols), ic, axis=1, keepdims=False)
    mod_parts = jnp.pad(mod_parts, ((0, 0), (0, SUBLANES - nb), (0, 0)))
    mod_recv = _run_exchange(_scatter_chips_exchange(mod_parts), "mod_to_chips")
    by_chip = lax.dynamic_update_slice(jnp.zeros_like(mod_parts),
                                       lax.dynamic_index_in_dim(mod_parts, chip, 0, keepdims=True), (chip, 0, 0))
    for k, (fx, fy) in enumerate([(1, 0), (0, 1), (1, 1)]):
        src = 2 * (ix ^ fx) + (iy ^ fy)
        by_chip = lax.dynamic_update_slice(by_chip, mod_recv[k:k + 1], (src, 0, 0))
    mod_mine = jnp.moveaxis(by_chip[:, :nb], 0, 1).reshape(nb, N_MOD, D_MODEL)
    mod = jnp.pad(mod_mine, ((0, 0), (0, 8 - N_MOD), (0, 0)))

    cos_t, sin_a, sin_b = _rope_tables(seq)
    row2 = lambda v: v.reshape(1, -1)

    x2d = x.reshape(rows, D_MODEL)
    x0, h, qkv, lat, qp, kp, vm = _fwd_in(x2d, mod, row2(ln_in_g), row2(ln_in_b), w_in_p, q_norm_g, kv_norm_g,
                                          w_uq_p, w_ukv_p, cos_t, sin_a, sin_b, seq)
    sb_y, cars, g_mlp = _sb_fwd(qkv, seq, late_weights)
    g_mlp = _with_own(g_mlp, half_mlp)
    f_o = _split_slots(g_mlp, GROUP_MLP)[2].reshape(D_MODEL, D_MODEL)
    mla_y, lse = _mla_fwd(qp, kp, vm, seq)
    mix, y1, h2, u, ff, y2 = _fwd_out(sb_y, mla_y, x0, mod, f_o, ln1_g, ln1_b, g_mlp, seq)

    dy1, dmix, d_attn, dff, du, acc_out, dmod_a = _bwd_out(
        y2, loss_target.reshape(rows, D_MODEL), ff, u, y1, mix, mod, ln2_g, ln2_b, ln1_g, ln1_b, g_mlp, f_o, seq)
    c_idx = ic.reshape(1).astype(jnp.int32)
    blocks_mlp = _wgrad_packed(h2, du, "wgrad_up", lambda i, j: 2 * j + i, 0)
    blocks_mlp = _wgrad_packed(u, dff, "wgrad_down", lambda i, j: i, 1, pre="relu2", into=blocks_mlp)
    blocks_mlp = _wgrad_packed(sb_y, dmix, "wgrad_o_sb", lambda i, j: 0, 8, split=4, into=blocks_mlp)
    blocks_mlp = _wgrad_packed(mla_y, dmix, "wgrad_o_mla", lambda i, j: 1, 8, split=4, into=blocks_mlp)
    dq_sb, dk_sb, dv_sb, sibling_mlp = _sb_bwd(qkv, d_attn, cars, seq, _swap_cores_exchange(blocks_mlp))
    part_mlp, part_mlp_bf = _add_pairs(blocks_mlp, sibling_mlp, c_idx, "grad_add_cores_mlp")
    dqp, dkp, dvm, chips_mlp = _mla_bwd(qp, kp, vm, d_attn, mla_y, lse, seq, _scatter_chips_exchange(part_mlp_bf))
    grad_x, dproj, dqall, dkv, latn, acc0, acc_lat, dmod_c = _bwd_in(
        dqp, dkp, dvm, dq_sb, dk_sb, dv_sb, lat, x2d, dy1, mod, row2(ln_in_g), row2(ln_in_b), w_in_p,
        q_norm_g, kv_norm_g, w_uq_p, w_ukv_p, cos_t, sin_a, sin_b, seq)

    dmod = (dmod_a + dmod_c)[:, :N_MOD, :]
    small_part = _pack_small([acc0[0], acc0[1], jnp.zeros((N_MOD * D_MODEL,), F32), acc_lat[0, :Q_RANK],
                              acc_lat[1, :KV_RANK], acc_out[3], acc_out[4], acc_out[0], acc_out[1]])
    n_sum = SMALL_ROWS + D_MODEL // LANES
    payload = jnp.concatenate([small_part, acc_out[2].reshape(-1, LANES), dmod.reshape(-1, LANES)], axis=0)
    g_in_p, gathered = _wgrad(h, dproj, "wgrad_in", tn=768, ex=_gather_exchange(payload))
    gathered = _with_own(gathered, payload)
    g_in = _unpad_w_in(g_in_p)
    g_uq = _unpad_heads(_wgrad(latn[:, :Q_RANK], dqall, "wgrad_uq"), NOPE + ROPE)
    g_ukv_p = _wgrad(latn[:, Q_RANK:], dkv, "wgrad_ukv", tn=512)
    g_ukv = jnp.concatenate([_unpad_heads(g_ukv_p[:, :HEADS * LANES], NOPE), g_ukv_p[:, HEADS * LANES:]], axis=1)
    blocks_in = _pack_full([g_in, g_uq, g_ukv], GROUP_IN)
    sibling_in = _run_exchange(_swap_cores_exchange(blocks_in), "grads_in_to_sibling")
    part_in, part_in_bf = _add_pairs(blocks_in, sibling_in, c_idx, "grad_add_cores_in")
    chips_in = _run_exchange(_scatter_chips_exchange(part_in_bf), "grads_in_to_chips")

    def own(part):
        return lax.dynamic_index_in_dim(part, chip, 0, keepdims=False)

    half = jnp.concatenate([_add_chips(own(part_in), chips_in, "grad_add_chips_in"),
                            _add_chips(own(part_mlp), chips_mlp, "grad_add_chips_mlp")], axis=0)
    other = _run_exchange(_swap_one_exchange(half), "grads_halves")
    mine = _unpack_half(half[:GROUP_IN[0]], GROUP_IN) + _unpack_half(half[GROUP_IN[0]:], GROUP_MLP)
    theirs = _unpack_half(other[:GROUP_IN[0]], GROUP_IN) + _unpack_half(other[GROUP_IN[0]:], GROUP_MLP)

    small_sum = _sum_lead(gathered[:, :n_sum, :], "sum_small")
    loss = jnp.sum(small_sum[SMALL_ROWS:])
    dmod_all = gathered[:, n_sum:, :].reshape(n_all, N_MOD * D_MODEL)
    g_b_ada = _sum_lead(dmod_all.reshape(n_all, N_MOD * D_MODEL // LANES, LANES), "sum_b_ada")
    dmod_sh = lax.dynamic_slice_in_dim(dmod_all, chip * ada_cols, ada_cols, axis=1)
    g_w_ada = _ada_bwd(c_all, dmod_sh)

    res = {}
    d_ada, m_ada, v_ada = _adamw(w_ada[0], g_w_ada, m_w_ada[0], v_w_ada[0], "adamw_w_ada")
    res["w_ada"] = (g_w_ada[None], d_ada[None], m_ada[None], v_ada[None])
    sharded = {"w_in": (w_in, m_w_in, v_w_in), "w_uq": (w_uq, m_w_uq, v_w_uq), "w_ukv": (w_ukv, m_w_ukv, v_w_ukv),
               "w_up": (w_up, m_w_up, v_w_up), "w_down": (w_down, m_w_down, v_w_down), "w_o": (w_o, m_w_o, v_w_o)}
    for (name, (w, m, v)), g_mine, g_other in zip(sharded.items(), mine, theirs):
        quad = _adamw_halves(w[0], g_mine, g_other, m[0], v[0], c_idx, "adamw_" + name)
        res[name] = tuple(a[None] for a in quad)
    small_w = [ln_in_g, ln_in_b, b_ada, q_norm_g, kv_norm_g, ln1_g, ln1_b, ln2_g, ln2_b]
    small_m = [m_ln_in_g, m_ln_in_b, m_b_ada, m_q_norm_g, m_kv_norm_g, m_ln1_g, m_ln1_b, m_ln2_g, m_ln2_b]
    small_v = [v_ln_in_g, v_ln_in_b, v_b_ada, v_q_norm_g, v_kv_norm_g, v_ln1_g, v_ln1_b, v_ln2_g, v_ln2_b]
    for (name, _), quad in zip(SMALL, _adamw_small(small_sum, g_b_ada, small_w, small_m, small_v)):
        res[name] = quad

    order = ["ln_in_g", "ln_in_b", "w_ada", "b_ada", "w_in", "q_norm_g", "kv_norm_g", "w_uq", "w_ukv", "w_o",
             "ln1_g", "ln1_b", "w_up", "w_down", "ln2_g", "ln2_b"]
    outs = [loss, grad_x.reshape(nb, seq, D_MODEL)]
    for k in range(4):
        outs += [res[name][k] for name in order]
    return tuple(outs)
```
